```python
import math
import jax, jax.numpy as jnp
from jax import lax
import numpy as np

D_MODEL = 1024
BATCH = 8
SEQ = 8192
DEPTH = 2

MIX_WIDTH = D_MODEL
MLA_WIDTH = D_MODEL // 2
SGU_WIDTH = D_MODEL // 4
POOL_WIDTH = D_MODEL // 4

MLA_HEADS = 4
V_HEAD = MLA_WIDTH // MLA_HEADS
QK_NOPE = 64
QK_ROPE = 32
QK_HEAD = QK_NOPE + QK_ROPE
Q_LORA = D_MODEL // 4
KV_LORA = D_MODEL // 8
ROPE_THETA = 10000.0
Q_BLOCK = 128

SGU_HEADS = 4
SGU_HEAD_DIM = SGU_WIDTH // SGU_HEADS
CHUNK = 128

POOL_WINDOWS = (2, 4, 8, 16)
POOL_GROUPS = len(POOL_WINDOWS)
POOL_GROUP_DIM = POOL_WIDTH // POOL_GROUPS

IN_WIDTH = Q_LORA + KV_LORA + QK_ROPE + 2 * SGU_WIDTH + POOL_WIDTH
FFN_HIDDEN = -(-8 * D_MODEL // (3 * 256)) * 256
EPS = 1e-6

kernel_name = "hybrid_mla_sgu_pool_block"


def rms_norm(x, g):
    xf = x.astype(jnp.float32)
    y = xf * lax.rsqrt(jnp.mean(xf * xf, axis=-1, keepdims=True) + EPS)
    return (y * g.astype(jnp.float32)).astype(x.dtype)


def apply_rope(x, positions):
    half = x.shape[-1] // 2
    inv_freq = 1.0 / (ROPE_THETA ** (jnp.arange(half, dtype=jnp.float32) / half))
    ang = positions.astype(jnp.float32)[:, :, None, None] * inv_freq
    cos, sin = jnp.cos(ang), jnp.sin(ang)
    xf = x.astype(jnp.float32)
    x1, x2 = xf[..., :half], xf[..., half:]
    return jnp.concatenate([x1 * cos - x2 * sin, x2 * cos + x1 * sin], axis=-1).astype(x.dtype)


def mla_mixer(q_lat, kv_lat, k_rope, positions, g_q_lat, w_q_up, g_kv_lat, w_kv_up, g_q_head, g_k_head):
    B, S, _ = q_lat.shape
    q = (rms_norm(q_lat, g_q_lat) @ w_q_up).reshape(B, S, MLA_HEADS, QK_HEAD)
    kv = (rms_norm(kv_lat, g_kv_lat) @ w_kv_up).reshape(B, S, MLA_HEADS, QK_NOPE + V_HEAD)
    k_nope, v = kv[..., :QK_NOPE], kv[..., QK_NOPE:]
    k_pe = jnp.broadcast_to(k_rope[:, :, None, :], (B, S, MLA_HEADS, QK_ROPE))
    k = jnp.concatenate([k_nope, k_pe], axis=-1)
    q = rms_norm(q, g_q_head)
    k = rms_norm(k, g_k_head)
    q = jnp.concatenate([q[..., :QK_NOPE], apply_rope(q[..., QK_NOPE:], positions)], axis=-1)
    k = jnp.concatenate([k[..., :QK_NOPE], apply_rope(k[..., QK_NOPE:], positions)], axis=-1)

    n_blocks = S // Q_BLOCK
    scale = 1.0 / math.sqrt(QK_HEAD)
    qb = q.reshape(B, n_blocks, Q_BLOCK, MLA_HEADS, QK_HEAD).transpose(1, 0, 2, 3, 4)
    kpos = jnp.arange(S)

    def attend_block(args):
        qi, bi = args
        s = jnp.einsum('bqhd,bkhd->bhqk', qi, k).astype(jnp.float32) * scale
        qpos = bi * Q_BLOCK + jnp.arange(Q_BLOCK)
        causal = kpos[None, :] <= qpos[:, None]
        s = jnp.where(causal[None, None], s, jnp.finfo(jnp.float32).min)
        p = jax.nn.softmax(s, axis=-1)
        return jnp.einsum('bhqk,bkhd->bqhd', p.astype(v.dtype), v)

    o = lax.map(attend_block, (qb, jnp.arange(n_blocks)))
    return o.transpose(1, 0, 2, 3, 4).reshape(B, S, MLA_WIDTH)


def sgu_mixer(uv, g_v, w_spatial, b_spatial):
    B, S, _ = uv.shape
    u, v = uv[..., :SGU_WIDTH], uv[..., SGU_WIDTH:]
    v = rms_norm(v, g_v)
    vc = v.reshape(B, S // CHUNK, CHUNK, SGU_HEADS, SGU_HEAD_DIM)
    w = w_spatial * jnp.tril(jnp.ones((CHUNK, CHUNK), dtype=w_spatial.dtype))
    zc = jnp.einsum('hts,bcshd->bcthd', w, vc) + b_spatial.T[None, None, :, :, None]
    return u * zc.reshape(B, S, SGU_WIDTH)


def pool_mixer(p, w_pool, pool_scale):
    B, S, _ = p.shape
    pf = p.astype(jnp.float32).reshape(B, S, POOL_GROUPS, POOL_GROUP_DIM)
    t1 = jnp.arange(1, S + 1, dtype=jnp.float32)
    outs = []
    for g, win in enumerate(POOL_WINDOWS):
        xg = pf[:, :, g]
        cs = jnp.cumsum(xg, axis=1)
        cs_shift = jnp.pad(cs, ((0, 0), (win, 0), (0, 0)))[:, :S]
        count = jnp.minimum(t1, float(win))[None, :, None]
        outs.append((cs - cs_shift) / count - xg)
    m = jnp.stack(outs, axis=2).astype(p.dtype)
    y = jnp.einsum('bsgc,gcd->bsgd', m, w_pool).reshape(B, S, POOL_WIDTH)
    return y * pool_scale


def _fwd_setup_inputs(seed: int = 0) -> dict:
    key = jax.random.key(seed)
    ks = jax.random.split(key, 24)
    f32 = jnp.float32

    def dense(k, shape, fan_in):
        return jax.random.normal(k, shape, f32) * fan_in ** -0.5

    def gain(k, shape):
        return 1.0 + 0.01 * jax.random.normal(k, shape, f32)

    x = jax.random.normal(ks[0], (BATCH, SEQ, D_MODEL), f32)
    start = jax.random.randint(ks[1], (BATCH, 1), 0, 4096, dtype=jnp.int32)
    positions = start + jnp.arange(SEQ, dtype=jnp.int32)[None, :]
    return {
        "x": x,
        "positions": positions,
        "g_mix_norm": gain(ks[2], (DEPTH, D_MODEL)),
        "w_in": dense(ks[3], (DEPTH, D_MODEL, IN_WIDTH), D_MODEL),
        "g_q_lat": gain(ks[4], (DEPTH, Q_LORA)),
        "w_q_up": dense(ks[5], (DEPTH, Q_LORA, MLA_HEADS * QK_HEAD), Q_LORA),
        "g_kv_lat": gain(ks[6], (DEPTH, KV_LORA)),
        "w_kv_up": dense(ks[7], (DEPTH, KV_LORA, MLA_HEADS * (QK_NOPE + V_HEAD)), KV_LORA),
        "g_q_head": gain(ks[8], (DEPTH, QK_HEAD)),
        "g_k_head": gain(ks[9], (DEPTH, QK_HEAD)),
        "g_sgu_v": gain(ks[10], (DEPTH, SGU_WIDTH)),
        "w_spatial": dense(ks[11], (DEPTH, SGU_HEADS, CHUNK, CHUNK), CHUNK),
        "b_spatial": 1.0 + 0.01 * jax.random.normal(ks[12], (DEPTH, SGU_HEADS, CHUNK), f32),
        "w_pool": dense(ks[13], (DEPTH, POOL_GROUPS, POOL_GROUP_DIM, POOL_GROUP_DIM), POOL_GROUP_DIM),
        "pool_scale": 1.0 + 0.1 * jax.random.normal(ks[14], (DEPTH, POOL_WIDTH), f32),
        "g_out_mla": gain(ks[15], (DEPTH, MLA_WIDTH)),
        "g_out_sgu": gain(ks[16], (DEPTH, SGU_WIDTH)),
        "g_out_pool": gain(ks[17], (DEPTH, POOL_WIDTH)),
        "w_out": dense(ks[18], (DEPTH, MIX_WIDTH, D_MODEL), MIX_WIDTH),
        "g_ffn_norm": gain(ks[19], (DEPTH, D_MODEL)),
        "w_gate": dense(ks[20], (DEPTH, D_MODEL, FFN_HIDDEN), D_MODEL),
        "w_up": dense(ks[21], (DEPTH, D_MODEL, FFN_HIDDEN), D_MODEL),
        "w_down": dense(ks[22], (DEPTH, FFN_HIDDEN, D_MODEL), FFN_HIDDEN),
    }


def _fwd_reference(x, positions, g_mix_norm, w_in, g_q_lat, w_q_up, g_kv_lat, w_kv_up, g_q_head, g_k_head,
              g_sgu_v, w_spatial, b_spatial, w_pool, pool_scale, g_out_mla, g_out_sgu, g_out_pool,
              w_out, g_ffn_norm, w_gate, w_up, w_down):
    o1 = Q_LORA
    o2 = o1 + KV_LORA
    o3 = o2 + QK_ROPE
    o4 = o3 + 2 * SGU_WIDTH
    for l in range(DEPTH):
        h = rms_norm(x, g_mix_norm[l])
        z = h @ w_in[l]
        q_lat, kv_lat, k_rope = z[..., :o1], z[..., o1:o2], z[..., o2:o3]
        uv, pin = z[..., o3:o4], z[..., o4:]
        a = mla_mixer(q_lat, kv_lat, k_rope, positions, g_q_lat[l], w_q_up[l], g_kv_lat[l],
                      w_kv_up[l], g_q_head[l], g_k_head[l])
        gm = sgu_mixer(uv, g_sgu_v[l], w_spatial[l], b_spatial[l])
        po = pool_mixer(pin, w_pool[l], pool_scale[l])
        mix = jnp.concatenate([rms_norm(a, g_out_mla[l]), rms_norm(gm, g_out_sgu[l]),
                               rms_norm(po, g_out_pool[l])], axis=-1)
        x = x + mix @ w_out[l]
        h = rms_norm(x, g_ffn_norm[l])
        x = x + (jax.nn.silu(h @ w_gate[l]) * (h @ w_up[l])) @ w_down[l]
    return x


import jax as _jax
import jax.numpy as _jnp

TWIN_FORMAT = 'train_step'
FWD_PARAMS = ['x', 'positions', 'g_mix_norm', 'w_in', 'g_q_lat', 'w_q_up', 'g_kv_lat', 'w_kv_up', 'g_q_head', 'g_k_head', 'g_sgu_v', 'w_spatial', 'b_spatial', 'w_pool', 'pool_scale', 'g_out_mla', 'g_out_sgu', 'g_out_pool', 'w_out', 'g_ffn_norm', 'w_gate', 'w_up', 'w_down']
TWIN_WEIGHTS = ['g_mix_norm', 'w_in', 'g_q_lat', 'w_q_up', 'g_kv_lat', 'w_kv_up', 'g_q_head', 'g_k_head', 'g_sgu_v', 'w_spatial', 'b_spatial', 'w_pool', 'pool_scale', 'g_out_mla', 'g_out_sgu', 'g_out_pool', 'w_out', 'g_ffn_norm', 'w_gate', 'w_up', 'w_down']
TWIN_DIFF_INPUT = 'x'
TWIN_INPUTS = ['x', 'positions', 'g_mix_norm', 'w_in', 'g_q_lat', 'w_q_up', 'g_kv_lat', 'w_kv_up', 'g_q_head', 'g_k_head', 'g_sgu_v', 'w_spatial', 'b_spatial', 'w_pool', 'pool_scale', 'g_out_mla', 'g_out_sgu', 'g_out_pool', 'w_out', 'g_ffn_norm', 'w_gate', 'w_up', 'w_down', 'loss_target', 'm_g_mix_norm', 'm_w_in', 'm_g_q_lat', 'm_w_q_up', 'm_g_kv_lat', 'm_w_kv_up', 'm_g_q_head', 'm_g_k_head', 'm_g_sgu_v', 'm_w_spatial', 'm_b_spatial', 'm_w_pool', 'm_pool_scale', 'm_g_out_mla', 'm_g_out_sgu', 'm_g_out_pool', 'm_w_out', 'm_g_ffn_norm', 'm_w_gate', 'm_w_up', 'm_w_down', 'v_g_mix_norm', 'v_w_in', 'v_g_q_lat', 'v_w_q_up', 'v_g_kv_lat', 'v_w_kv_up', 'v_g_q_head', 'v_g_k_head', 'v_g_sgu_v', 'v_w_spatial', 'v_b_spatial', 'v_w_pool', 'v_pool_scale', 'v_g_out_mla', 'v_g_out_sgu', 'v_g_out_pool', 'v_w_out', 'v_g_ffn_norm', 'v_w_gate', 'v_w_up', 'v_w_down']
TWIN_OUTPUTS = ['loss', 'grad_x', 'grad_g_mix_norm', 'grad_w_in', 'grad_g_q_lat', 'grad_w_q_up', 'grad_g_kv_lat', 'grad_w_kv_up', 'grad_g_q_head', 'grad_g_k_head', 'grad_g_sgu_v', 'grad_w_spatial', 'grad_b_spatial', 'grad_w_pool', 'grad_pool_scale', 'grad_g_out_mla', 'grad_g_out_sgu', 'grad_g_out_pool', 'grad_w_out', 'grad_g_ffn_norm', 'grad_w_gate', 'grad_w_up', 'grad_w_down', 'delta_g_mix_norm', 'delta_w_in', 'delta_g_q_lat', 'delta_w_q_up', 'delta_g_kv_lat', 'delta_w_kv_up', 'delta_g_q_head', 'delta_g_k_head', 'delta_g_sgu_v', 'delta_w_spatial', 'delta_b_spatial', 'delta_w_pool', 'delta_pool_scale', 'delta_g_out_mla', 'delta_g_out_sgu', 'delta_g_out_pool', 'delta_w_out', 'delta_g_ffn_norm', 'delta_w_gate', 'delta_w_up', 'delta_w_down', 'new_m_g_mix_norm', 'new_m_w_in', 'new_m_g_q_lat', 'new_m_w_q_up', 'new_m_g_kv_lat', 'new_m_w_kv_up', 'new_m_g_q_head', 'new_m_g_k_head', 'new_m_g_sgu_v', 'new_m_w_spatial', 'new_m_b_spatial', 'new_m_w_pool', 'new_m_pool_scale', 'new_m_g_out_mla', 'new_m_g_out_sgu', 'new_m_g_out_pool', 'new_m_w_out', 'new_m_g_ffn_norm', 'new_m_w_gate', 'new_m_w_up', 'new_m_w_down', 'new_v_g_mix_norm', 'new_v_w_in', 'new_v_g_q_lat', 'new_v_w_q_up', 'new_v_g_kv_lat', 'new_v_w_kv_up', 'new_v_g_q_head', 'new_v_g_k_head', 'new_v_g_sgu_v', 'new_v_w_spatial', 'new_v_b_spatial', 'new_v_w_pool', 'new_v_pool_scale', 'new_v_g_out_mla', 'new_v_g_out_sgu', 'new_v_g_out_pool', 'new_v_w_out', 'new_v_g_ffn_norm', 'new_v_w_gate', 'new_v_w_up', 'new_v_w_down']
TWIN_LEAF_KINDS = {'loss': 'loss', 'grad_x': 'grad_x', 'grad_g_mix_norm': 'grad_w', 'grad_w_in': 'grad_w', 'grad_g_q_lat': 'grad_w', 'grad_w_q_up': 'grad_w', 'grad_g_kv_lat': 'grad_w', 'grad_w_kv_up': 'grad_w', 'grad_g_q_head': 'grad_w', 'grad_g_k_head': 'grad_w', 'grad_g_sgu_v': 'grad_w', 'grad_w_spatial': 'grad_w', 'grad_b_spatial': 'grad_w', 'grad_w_pool': 'grad_w', 'grad_pool_scale': 'grad_w', 'grad_g_out_mla': 'grad_w', 'grad_g_out_sgu': 'grad_w', 'grad_g_out_pool': 'grad_w', 'grad_w_out': 'grad_w', 'grad_g_ffn_norm': 'grad_w', 'grad_w_gate': 'grad_w', 'grad_w_up': 'grad_w', 'grad_w_down': 'grad_w', 'delta_g_mix_norm': 'delta_w', 'delta_w_in': 'delta_w', 'delta_g_q_lat': 'delta_w', 'delta_w_q_up': 'delta_w', 'delta_g_kv_lat': 'delta_w', 'delta_w_kv_up': 'delta_w', 'delta_g_q_head': 'delta_w', 'delta_g_k_head': 'delta_w', 'delta_g_sgu_v': 'delta_w', 'delta_w_spatial': 'delta_w', 'delta_b_spatial': 'delta_w', 'delta_w_pool': 'delta_w', 'delta_pool_scale': 'delta_w', 'delta_g_out_mla': 'delta_w', 'delta_g_out_sgu': 'delta_w', 'delta_g_out_pool': 'delta_w', 'delta_w_out': 'delta_w', 'delta_g_ffn_norm': 'delta_w', 'delta_w_gate': 'delta_w', 'delta_w_up': 'delta_w', 'delta_w_down': 'delta_w', 'new_m_g_mix_norm': 'new_m', 'new_m_w_in': 'new_m', 'new_m_g_q_lat': 'new_m', 'new_m_w_q_up': 'new_m', 'new_m_g_kv_lat': 'new_m', 'new_m_w_kv_up': 'new_m', 'new_m_g_q_head': 'new_m', 'new_m_g_k_head': 'new_m', 'new_m_g_sgu_v': 'new_m', 'new_m_w_spatial': 'new_m', 'new_m_b_spatial': 'new_m', 'new_m_w_pool': 'new_m', 'new_m_pool_scale': 'new_m', 'new_m_g_out_mla': 'new_m', 'new_m_g_out_sgu': 'new_m', 'new_m_g_out_pool': 'new_m', 'new_m_w_out': 'new_m', 'new_m_g_ffn_norm': 'new_m', 'new_m_w_gate': 'new_m', 'new_m_w_up': 'new_m', 'new_m_w_down': 'new_m', 'new_v_g_mix_norm': 'new_v', 'new_v_w_in': 'new_v', 'new_v_g_q_lat': 'new_v', 'new_v_w_q_up': 'new_v', 'new_v_g_kv_lat': 'new_v', 'new_v_w_kv_up': 'new_v', 'new_v_g_q_head': 'new_v', 'new_v_g_k_head': 'new_v', 'new_v_g_sgu_v': 'new_v', 'new_v_w_spatial': 'new_v', 'new_v_b_spatial': 'new_v', 'new_v_w_pool': 'new_v', 'new_v_pool_scale': 'new_v', 'new_v_g_out_mla': 'new_v', 'new_v_g_out_sgu': 'new_v', 'new_v_g_out_pool': 'new_v', 'new_v_w_out': 'new_v', 'new_v_g_ffn_norm': 'new_v', 'new_v_w_gate': 'new_v', 'new_v_w_up': 'new_v', 'new_v_w_down': 'new_v'}


def _forward(args):
    return _fwd_reference(*[args[k] for k in FWD_PARAMS])


def _output_shape():
    def fwd():
        inp = _fwd_setup_inputs(0)
        return _fwd_reference(*[inp[k] for k in FWD_PARAMS])
    out = _jax.eval_shape(fwd)
    return out.shape, out.dtype

N_MICROBATCH = 1
ADAM_LR = 0.001
ADAM_B1 = 0.9
ADAM_B2 = 0.999
ADAM_EPS = 1e-08
ADAM_WD = 0.01
ADAM_STEP = 10
PER_EXAMPLE_BATCH_AXIS = {'x': 0, 'positions': 0, 'loss_target': 0}
SHARED_INPUTS = []
_WEIGHT_DTYPES = {'g_mix_norm': _jnp.float32, 'w_in': _jnp.float32, 'g_q_lat': _jnp.float32, 'w_q_up': _jnp.float32, 'g_kv_lat': _jnp.float32, 'w_kv_up': _jnp.float32, 'g_q_head': _jnp.float32, 'g_k_head': _jnp.float32, 'g_sgu_v': _jnp.float32, 'w_spatial': _jnp.float32, 'b_spatial': _jnp.float32, 'w_pool': _jnp.float32, 'pool_scale': _jnp.float32, 'g_out_mla': _jnp.float32, 'g_out_sgu': _jnp.float32, 'g_out_pool': _jnp.float32, 'w_out': _jnp.float32, 'g_ffn_norm': _jnp.float32, 'w_gate': _jnp.float32, 'w_up': _jnp.float32, 'w_down': _jnp.float32}
MOMENT_SCALE = {'g_mix_norm': 8.185660e+00, 'w_in': 7.971197e+00, 'g_q_lat': 3.047113e+00, 'w_q_up': 2.477230e+00, 'g_kv_lat': 3.248025e+01, 'w_kv_up': 1.084788e+01, 'g_q_head': 8.900729e+00, 'g_k_head': 9.058495e+00, 'g_sgu_v': 8.786746e-01, 'w_spatial': 3.377089e-01, 'b_spatial': 5.290256e-01, 'w_pool': 3.410699e+00, 'pool_scale': 4.198700e+00, 'g_out_mla': 6.813053e+01, 'g_out_sgu': 6.454035e+01, 'g_out_pool': 6.733345e+01, 'w_out': 9.491567e+00, 'g_ffn_norm': 4.922655e+01, 'w_gate': 6.688937e-01, 'w_up': 9.014210e-01, 'w_down': 1.429989e+00}


def _to_microbatches(a, axis):
    t = _jnp.moveaxis(a, axis, 0)
    t = t.reshape((N_MICROBATCH, t.shape[0] // N_MICROBATCH) + t.shape[1:])
    return _jnp.moveaxis(t, 1, axis + 1)


def setup_inputs(seed: int = 0) -> dict:
    inp = _fwd_setup_inputs(seed)
    key = _jax.random.fold_in(_jax.random.key(seed), 7919)
    shape, _ = _output_shape()
    out = dict(inp)
    out["loss_target"] = _jax.random.normal(_jax.random.fold_in(key, 0), shape, _jnp.float32)
    for i, name in enumerate(TWIN_WEIGHTS):
        w = inp[name].astype(_jnp.float32)
        if MOMENT_SCALE is None:
            s = _jnp.sqrt(_jnp.mean(_jnp.square(w)) + 1e-30)
        else:
            s = MOMENT_SCALE[name]
        km, kv = _jax.random.split(_jax.random.fold_in(key, i + 1))
        out[name] = w
        out["m_" + name] = s * _jax.random.normal(km, w.shape, _jnp.float32)
        out["v_" + name] = (s * s) * _jax.random.uniform(kv, w.shape, _jnp.float32, 0.5, 1.5)
    if N_MICROBATCH > 1:
        for name, axis in PER_EXAMPLE_BATCH_AXIS.items():
            out[name] = _to_microbatches(out[name], axis)
    return {'x': out['x'], 'positions': out['positions'], 'g_mix_norm': out['g_mix_norm'], 'w_in': out['w_in'], 'g_q_lat': out['g_q_lat'], 'w_q_up': out['w_q_up'], 'g_kv_lat': out['g_kv_lat'], 'w_kv_up': out['w_kv_up'], 'g_q_head': out['g_q_head'], 'g_k_head': out['g_k_head'], 'g_sgu_v': out['g_sgu_v'], 'w_spatial': out['w_spatial'], 'b_spatial': out['b_spatial'], 'w_pool': out['w_pool'], 'pool_scale': out['pool_scale'], 'g_out_mla': out['g_out_mla'], 'g_out_sgu': out['g_out_sgu'], 'g_out_pool': out['g_out_pool'], 'w_out': out['w_out'], 'g_ffn_norm': out['g_ffn_norm'], 'w_gate': out['w_gate'], 'w_up': out['w_up'], 'w_down': out['w_down'], 'loss_target': out['loss_target'], 'm_g_mix_norm': out['m_g_mix_norm'], 'm_w_in': out['m_w_in'], 'm_g_q_lat': out['m_g_q_lat'], 'm_w_q_up': out['m_w_q_up'], 'm_g_kv_lat': out['m_g_kv_lat'], 'm_w_kv_up': out['m_w_kv_up'], 'm_g_q_head': out['m_g_q_head'], 'm_g_k_head': out['m_g_k_head'], 'm_g_sgu_v': out['m_g_sgu_v'], 'm_w_spatial': out['m_w_spatial'], 'm_b_spatial': out['m_b_spatial'], 'm_w_pool': out['m_w_pool'], 'm_pool_scale': out['m_pool_scale'], 'm_g_out_mla': out['m_g_out_mla'], 'm_g_out_sgu': out['m_g_out_sgu'], 'm_g_out_pool': out['m_g_out_pool'], 'm_w_out': out['m_w_out'], 'm_g_ffn_norm': out['m_g_ffn_norm'], 'm_w_gate': out['m_w_gate'], 'm_w_up': out['m_w_up'], 'm_w_down': out['m_w_down'], 'v_g_mix_norm': out['v_g_mix_norm'], 'v_w_in': out['v_w_in'], 'v_g_q_lat': out['v_g_q_lat'], 'v_w_q_up': out['v_w_q_up'], 'v_g_kv_lat': out['v_g_kv_lat'], 'v_w_kv_up': out['v_w_kv_up'], 'v_g_q_head': out['v_g_q_head'], 'v_g_k_head': out['v_g_k_head'], 'v_g_sgu_v': out['v_g_sgu_v'], 'v_w_spatial': out['v_w_spatial'], 'v_b_spatial': out['v_b_spatial'], 'v_w_pool': out['v_w_pool'], 'v_pool_scale': out['v_pool_scale'], 'v_g_out_mla': out['v_g_out_mla'], 'v_g_out_sgu': out['v_g_out_sgu'], 'v_g_out_pool': out['v_g_out_pool'], 'v_w_out': out['v_w_out'], 'v_g_ffn_norm': out['v_g_ffn_norm'], 'v_w_gate': out['v_w_gate'], 'v_w_up': out['v_w_up'], 'v_w_down': out['v_w_down']}


def _loss(weights, diff, rest, loss_target):
    with _jax.named_scope("forward"):
        args = {**rest, TWIN_DIFF_INPUT: diff, **{k: w.astype(_WEIGHT_DTYPES[k]) for k, w in weights.items()}}
        y = _forward(args)
    with _jax.named_scope("loss_head"):
        err = _jnp.square(y.astype(_jnp.float32) - loss_target)
        return 0.5 * _jnp.sum(_jnp.mean(err, axis=-1)) if err.ndim else 0.5 * err


def _adamw(w, g, m, v):
    m = ADAM_B1 * m + (1.0 - ADAM_B1) * g
    v = ADAM_B2 * v + (1.0 - ADAM_B2) * _jnp.square(g)
    m_hat = m / (1.0 - ADAM_B1 ** ADAM_STEP)
    v_hat = v / (1.0 - ADAM_B2 ** ADAM_STEP)
    delta = -ADAM_LR * (m_hat / (_jnp.sqrt(v_hat) + ADAM_EPS) + ADAM_WD * w)
    return delta, m, v


def reference(x, positions, g_mix_norm, w_in, g_q_lat, w_q_up, g_kv_lat, w_kv_up, g_q_head, g_k_head, g_sgu_v, w_spatial, b_spatial, w_pool, pool_scale, g_out_mla, g_out_sgu, g_out_pool, w_out, g_ffn_norm, w_gate, w_up, w_down, loss_target, m_g_mix_norm, m_w_in, m_g_q_lat, m_w_q_up, m_g_kv_lat, m_w_kv_up, m_g_q_head, m_g_k_head, m_g_sgu_v, m_w_spatial, m_b_spatial, m_w_pool, m_pool_scale, m_g_out_mla, m_g_out_sgu, m_g_out_pool, m_w_out, m_g_ffn_norm, m_w_gate, m_w_up, m_w_down, v_g_mix_norm, v_w_in, v_g_q_lat, v_w_q_up, v_g_kv_lat, v_w_kv_up, v_g_q_head, v_g_k_head, v_g_sgu_v, v_w_spatial, v_b_spatial, v_w_pool, v_pool_scale, v_g_out_mla, v_g_out_sgu, v_g_out_pool, v_w_out, v_g_ffn_norm, v_w_gate, v_w_up, v_w_down):
    given = dict(x=x, positions=positions, g_mix_norm=g_mix_norm, w_in=w_in, g_q_lat=g_q_lat, w_q_up=w_q_up, g_kv_lat=g_kv_lat, w_kv_up=w_kv_up, g_q_head=g_q_head, g_k_head=g_k_head, g_sgu_v=g_sgu_v, w_spatial=w_spatial, b_spatial=b_spatial, w_pool=w_pool, pool_scale=pool_scale, g_out_mla=g_out_mla, g_out_sgu=g_out_sgu, g_out_pool=g_out_pool, w_out=w_out, g_ffn_norm=g_ffn_norm, w_gate=w_gate, w_up=w_up, w_down=w_down, loss_target=loss_target, m_g_mix_norm=m_g_mix_norm, m_w_in=m_w_in, m_g_q_lat=m_g_q_lat, m_w_q_up=m_w_q_up, m_g_kv_lat=m_g_kv_lat, m_w_kv_up=m_w_kv_up, m_g_q_head=m_g_q_head, m_g_k_head=m_g_k_head, m_g_sgu_v=m_g_sgu_v, m_w_spatial=m_w_spatial, m_b_spatial=m_b_spatial, m_w_pool=m_w_pool, m_pool_scale=m_pool_scale, m_g_out_mla=m_g_out_mla, m_g_out_sgu=m_g_out_sgu, m_g_out_pool=m_g_out_pool, m_w_out=m_w_out, m_g_ffn_norm=m_g_ffn_norm, m_w_gate=m_w_gate, m_w_up=m_w_up, m_w_down=m_w_down, v_g_mix_norm=v_g_mix_norm, v_w_in=v_w_in, v_g_q_lat=v_g_q_lat, v_w_q_up=v_w_q_up, v_g_kv_lat=v_g_kv_lat, v_w_kv_up=v_w_kv_up, v_g_q_head=v_g_q_head, v_g_k_head=v_g_k_head, v_g_sgu_v=v_g_sgu_v, v_w_spatial=v_w_spatial, v_b_spatial=v_b_spatial, v_w_pool=v_w_pool, v_pool_scale=v_pool_scale, v_g_out_mla=v_g_out_mla, v_g_out_sgu=v_g_out_sgu, v_g_out_pool=v_g_out_pool, v_w_out=v_w_out, v_g_ffn_norm=v_g_ffn_norm, v_w_gate=v_w_gate, v_w_up=v_w_up, v_w_down=v_w_down)
    weights = {n: given[n] for n in TWIN_WEIGHTS}
    shared = {n: given[n] for n in SHARED_INPUTS}
    per_example = {n: given[n] for n in ['x', 'positions']}
    grad_fn = _jax.value_and_grad(_loss, argnums=(0, 1))

    def one_microbatch(ex, loss_target):
        ex = dict(ex)
        diff = ex.pop(TWIN_DIFF_INPUT)
        return grad_fn(weights, diff, {**shared, **ex}, loss_target)

    if N_MICROBATCH == 1:
        loss, (grad_w, grad_x) = one_microbatch(per_example, given["loss_target"])
    else:
        def body(carry, xs):
            loss_sum, grad_sum = carry
            l_k, (gw_k, gx_k) = one_microbatch(xs[0], xs[1])
            with _jax.named_scope("update"):
                return (loss_sum + l_k, _jax.tree.map(_jnp.add, grad_sum, gw_k)), gx_k

        init = (_jnp.zeros((), _jnp.float32), _jax.tree.map(_jnp.zeros_like, weights))
        (loss, grad_w), grad_x = _jax.lax.scan(body, init, (per_example, given["loss_target"]))
    with _jax.named_scope("update"):
        delta_w, new_m, new_v = {}, {}, {}
        for n in TWIN_WEIGHTS:
            delta_w[n], new_m[n], new_v[n] = _adamw(weights[n], grad_w[n], given["m_" + n], given["v_" + n])
    return (loss, grad_x, *[grad_w[n] for n in TWIN_WEIGHTS], *[delta_w[n] for n in TWIN_WEIGHTS],
            *[new_m[n] for n in TWIN_WEIGHTS], *[new_v[n] for n in TWIN_WEIGHTS])
```

```python
import functools
import math

import jax
import jax.numpy as jnp
from jax import lax
from jax.experimental import pallas as pl
from jax.experimental.pallas import tpu as pltpu

F32 = jnp.float32
BF16 = jnp.bfloat16
MESH = pl.DeviceIdType.MESH

D = 1024
HEADS = 4
QK = 96
NOPE = 64
ROPE = 32
VH = 128
HP = 128
QL = 256
KVL = 128
SGU = 256
POOL = 256
CHUNK = 128
HID = 2816
IN_W = 1184
IN_P = 1280
EPS = 1e-6
ROPE_THETA = 10000.0
SCALE = 1.0 / math.sqrt(QK)
NEG = -1e30
HALO = 16

LR, B1, B2, ADAM_EPS, WD, STEP = 0.001, 0.9, 0.999, 1e-08, 0.01, 10

VMEM_LIMIT = 56 * 1024 * 1024
LANES = 128
HC = 256


def _cp(sem, vmem=None):
    return pltpu.CompilerParams(dimension_semantics=sem, vmem_limit_bytes=vmem)


def _res(shape):
    nd = len(shape)
    return pl.BlockSpec(shape, lambda *_: (0,) * nd, pipeline_mode=pl.Buffered(1))


def _acc(shape):
    nd = len(shape)
    return pl.BlockSpec(shape, lambda *_: (0,) * nd)


def _dot(a, b):
    return jnp.dot(a, b, preferred_element_type=F32)


def _dot_nt(a, b):
    return lax.dot_general(a, b, (((1,), (1,)), ((), ())), preferred_element_type=F32)


def _dot_tn(a, b):
    return lax.dot_general(a, b, (((0,), (0,)), ((), ())), preferred_element_type=F32)


def _rms(x, n):
    r = lax.rsqrt(jnp.sum(x * x, axis=-1, keepdims=True) * (1.0 / n) + EPS)
    return x * r, r


def _rms_bwd(xn, r, g, dy, n):
    dn = dy * g
    dx = r * (dn - xn * (jnp.sum(dn * xn, axis=-1, keepdims=True) * (1.0 / n)))
    return dx, jnp.sum(dy * xn, axis=0, keepdims=True)


def _accumulate(ref, val, first):
    @pl.when(first)
    def _():
        ref[...] = val

    @pl.when(jnp.logical_not(first))
    def _():
        ref[...] += val


def _tile(s, t):
    return min(s, t)


def _row_tile(r, cap):
    if r <= cap:
        return r
    return max(t for t in range(8, cap + 1, 8) if r % t == 0)


def _rope_tables(pos, invf):
    s = pos.shape[0]
    tm = _tile(s, 1024)

    def body(pos_ref, invf_ref, c_ref, sa_ref, sb_ref):
        ang = pos_ref[...].astype(F32) * invf_ref[...]
        c, sn = jnp.cos(ang), jnp.sin(ang)
        lane = lax.broadcasted_iota(jnp.int32, ang.shape, 1)
        first = (lane >= NOPE) & (lane < NOPE + ROPE // 2)
        second = (lane >= NOPE + ROPE // 2) & (lane < QK)
        c_ref[...] = jnp.where(first | second, c, 1.0)
        sa_ref[...] = jnp.where(first, -sn, 0.0)
        sb_ref[...] = jnp.where(second, sn, 0.0)

    out = jax.ShapeDtypeStruct((s, HP), F32)
    return pl.pallas_call(
        body, name="rope_tables", grid=(s // tm,),
        in_specs=[pl.BlockSpec((tm, 1), lambda i: (i, 0)), _acc((1, HP))],
        out_specs=[pl.BlockSpec((tm, HP), lambda i: (i, 0))] * 3,
        out_shape=[out] * 3, compiler_params=_cp(("parallel",)),
    )(pos, invf)


def _rope(x, c, sa, sb):
    return x * c + pltpu.roll(x, HP - ROPE // 2, 1) * sa + pltpu.roll(x, ROPE // 2, 1) * sb


def _rope_t(d, c, sa, sb):
    return d * c + pltpu.roll(d * sa, ROPE // 2, 1) + pltpu.roll(d * sb, HP - ROPE // 2, 1)


def _in_proj_fwd(x, g, w, name):
    s = x.shape[0]
    tm = _tile(s, 512)

    def body(x_ref, g_ref, w_ref, z_ref, h_ref):
        xn, _ = _rms(x_ref[...], D)
        h = (xn * g_ref[...]).astype(BF16)
        h_ref[...] = h
        z_ref[...] = _dot(h, w_ref[...])

    return pl.pallas_call(
        body, name=name, grid=(s // tm,),
        in_specs=[pl.BlockSpec((tm, D), lambda i: (i, 0)), _acc((1, D)), _res((D, IN_P))],
        out_specs=[pl.BlockSpec((tm, IN_P), lambda i: (i, 0)), pl.BlockSpec((tm, D), lambda i: (i, 0))],
        out_shape=[jax.ShapeDtypeStruct((s, IN_P), F32), jax.ShapeDtypeStruct((s, D), BF16)],
        compiler_params=_cp(("parallel",), VMEM_LIMIT),
    )(x, g, w)


def _mla_prep_fwd(z, tabs, gql, gkv, gq, gk, wq, wk, wv, name):
    s = z.shape[0]
    tm = _tile(s, 512)

    def body(ql_ref, kv_ref, kr_ref, c_ref, sa_ref, sb_ref, gql_ref, gkv_ref, gq_ref, gk_ref,
             wq_ref, wk_ref, wv_ref, q_out, k_out, v_out):
        qn = (_rms(ql_ref[...], QL)[0] * gql_ref[...]).astype(BF16)
        kvn = (_rms(kv_ref[...], KVL)[0] * gkv_ref[...]).astype(BF16)
        qraw = _dot(qn, wq_ref[...])
        kraw = _dot(kvn, wk_ref[...])
        vraw = _dot(kvn, wv_ref[...])
        kr = kr_ref[...]
        c, sa, sb = c_ref[...], sa_ref[...], sb_ref[...]
        for h in range(HEADS):
            sl = slice(h * HP, (h + 1) * HP)
            xq = _rms(qraw[:, sl], QK)[0] * gq_ref[...]
            q_out[h] = _rope(xq, c, sa, sb).astype(BF16)
            xk = _rms(kraw[:, sl] + kr, QK)[0] * gk_ref[...]
            k_out[h] = _rope(xk, c, sa, sb).astype(BF16)
            v_out[h] = vraw[:, sl].astype(BF16)

    row = lambda w, j: pl.BlockSpec((tm, w), lambda i: (i, j))
    hspec = pl.BlockSpec((HEADS, tm, HP), lambda i: (0, i, 0))
    hshape = jax.ShapeDtypeStruct((HEADS, s, HP), BF16)
    return pl.pallas_call(
        body, name=name, grid=(s // tm,),
        in_specs=[row(QL, 0), row(KVL, 2), row(HP, 3), row(HP, 0), row(HP, 0), row(HP, 0),
                  _acc((1, QL)), _acc((1, KVL)), _acc((1, HP)), _acc((1, HP)),
                  _acc((QL, HEADS * HP)), _acc((KVL, HEADS * HP)), _acc((KVL, HEADS * HP))],
        out_specs=[hspec] * 3, out_shape=[hshape] * 3,
        compiler_params=_cp(("parallel",)),
    )(z, z, z, *tabs, gql, gkv, gq, gk, wq, wk, wv)


def _causal_mask(s, tq, tk):
    row = lax.broadcasted_iota(jnp.int32, (tq, tk), 0)
    col = lax.broadcasted_iota(jnp.int32, (tq, tk), 1)
    return jnp.where(col <= row, s, NEG)


def _attn_fwd(q, k, v, name):
    s = q.shape[1]
    tq = tk = _tile(s, 512)

    def body(q_ref, k_ref, v_ref, o_ref, lse_ref):
        i = pl.program_id(1)
        qb = q_ref[0]

        def blk(j, m, l, acc, masked):
            off = pl.multiple_of(j * tk, tk)
            kj = k_ref[0, pl.ds(off, tk), :]
            vj = v_ref[0, pl.ds(off, tk), :]
            sc = _dot_nt(qb, kj) * SCALE
            if masked:
                sc = _causal_mask(sc, tq, tk)
            m_new = jnp.maximum(m, jnp.max(sc, axis=-1, keepdims=True))
            p = jnp.exp(sc - m_new)
            alpha = jnp.exp(m - m_new)
            l = alpha * l + jnp.sum(p, axis=-1, keepdims=True)
            acc = alpha * acc + _dot(p.astype(BF16), vj)
            return m_new, l, acc

        init = (jnp.full((tq, 1), NEG, F32), jnp.zeros((tq, 1), F32), jnp.zeros((tq, VH), F32))
        m, l, acc = lax.fori_loop(0, i, lambda j, c: blk(j, *c, False), init)
        m, l, acc = blk(i, m, l, acc, True)
        o_ref[...] = acc / l
        lse_ref[0] = jnp.broadcast_to(m + jnp.log(l), (tq, LANES))

    return pl.pallas_call(
        body, name=name, grid=(HEADS, s // tq),
        in_specs=[pl.BlockSpec((1, tq, HP), lambda h, i: (h, i, 0)),
                  pl.BlockSpec((1, s, HP), lambda h, i: (h, 0, 0)),
                  pl.BlockSpec((1, s, HP), lambda h, i: (h, 0, 0))],
        out_specs=[pl.BlockSpec((tq, VH), lambda h, i: (i, h)),
                   pl.BlockSpec((1, tq, LANES), lambda h, i: (h, i, 0))],
        out_shape=[jax.ShapeDtypeStruct((s, HEADS * VH), F32), jax.ShapeDtypeStruct((HEADS, s, LANES), F32)],
        compiler_params=_cp(("parallel", "arbitrary"), VMEM_LIMIT),
    )(q, k, v)


def _lane_group(shape, j):
    return (lax.broadcasted_iota(jnp.int32, shape, 1) + j * LANES) // (POOL // 4)


def _pool_win_fwd(z, name):
    s = z.shape[0]
    ch = _tile(s, 512)
    col0 = (IN_P - POOL) // LANES

    def body(p_ref, m_ref):
        j = pl.program_id(0)

        def chunk(r, _):
            off = pl.multiple_of(r * ch, ch)
            cur = p_ref[pl.ds(off, ch), :]
            hoff = pl.multiple_of(jnp.maximum(off - HALO, 0), 8)
            halo = jnp.where(r > 0, p_ref[pl.ds(hoff, HALO), :], 0.0)
            x = jnp.concatenate([halo, cur], axis=0)
            s2 = x + pltpu.roll(x, 1, 0)
            s4 = s2 + pltpu.roll(s2, 2, 0)
            s8 = s4 + pltpu.roll(s4, 4, 0)
            s16 = s8 + pltpu.roll(s8, 8, 0)
            grp = _lane_group((ch, LANES), j)
            sel = jnp.where(grp == 0, s2[HALO:], jnp.where(grp == 1, s4[HALO:], jnp.where(grp == 2, s8[HALO:], s16[HALO:])))
            t1 = (lax.broadcasted_iota(jnp.int32, (ch, LANES), 0) + off + 1).astype(F32)
            win = jnp.where(grp == 0, 2.0, jnp.where(grp == 1, 4.0, jnp.where(grp == 2, 8.0, 16.0)))
            m_ref[pl.ds(off, ch), :] = sel / jnp.minimum(t1, win) - cur
            return 0

        lax.fori_loop(0, s // ch, chunk, 0)

    return pl.pallas_call(
        body, name=name, grid=(POOL // LANES,),
        in_specs=[pl.BlockSpec((s, LANES), lambda j: (0, col0 + j))],
        out_specs=pl.BlockSpec((s, LANES), lambda j: (0, j)),
        out_shape=jax.ShapeDtypeStruct((s, POOL), F32),
        compiler_params=_cp(("parallel",), VMEM_LIMIT),
    )(z)


def _pool_win_bwd(dm, name):
    s = dm.shape[0]
    ch = _tile(s, 512)
    n = s // ch

    def body(dm_ref, dp_ref):
        j = pl.program_id(0)

        def chunk(r, _):
            off = pl.multiple_of(r * ch, ch)
            grp = _lane_group((ch + HALO, LANES), j)
            win = jnp.where(grp == 0, 2.0, jnp.where(grp == 1, 4.0, jnp.where(grp == 2, 8.0, 16.0)))
            cur = dm_ref[pl.ds(off, ch), :]
            hoff = pl.multiple_of(jnp.minimum(off + ch, s - HALO), 8)
            halo = jnp.where(r < n - 1, dm_ref[pl.ds(hoff, HALO), :], 0.0)
            x = jnp.concatenate([cur, halo], axis=0)
            t1 = (lax.broadcasted_iota(jnp.int32, (ch + HALO, LANES), 0) + off + 1).astype(F32)
            e = x / jnp.minimum(t1, win)
            tot = ch + HALO
            r2 = e + pltpu.roll(e, tot - 1, 0)
            r4 = r2 + pltpu.roll(r2, tot - 2, 0)
            r8 = r4 + pltpu.roll(r4, tot - 4, 0)
            r16 = r8 + pltpu.roll(r8, tot - 8, 0)
            g = grp[:ch]
            sel = jnp.where(g == 0, r2[:ch], jnp.where(g == 1, r4[:ch], jnp.where(g == 2, r8[:ch], r16[:ch])))
            dp_ref[pl.ds(off, ch), :] = (sel - cur).astype(BF16)
            return 0

        lax.fori_loop(0, n, chunk, 0)

    return pl.pallas_call(
        body, name=name, grid=(POOL // LANES,),
        in_specs=[pl.BlockSpec((s, LANES), lambda j: (0, j))],
        out_specs=pl.BlockSpec((s, LANES), lambda j: (0, j)),
        out_shape=jax.ShapeDtypeStruct((s, POOL), BF16),
        compiler_params=_cp(("parallel",), VMEM_LIMIT),
    )(dm)


def _head_mask(h):
    lane = lax.broadcasted_iota(jnp.int32, (CHUNK, SGU), 1)
    return (lane // (SGU // HEADS)) == h


def _tril(upper=False):
    row = lax.broadcasted_iota(jnp.int32, (CHUNK, CHUNK), 0)
    col = lax.broadcasted_iota(jnp.int32, (CHUNK, CHUNK), 1)
    return col >= row if upper else col <= row


def _sgu_gate(vn, wsp, bsp):
    out = []
    for cidx in range(vn.shape[0] // CHUNK):
        vc = vn[cidx * CHUNK:(cidx + 1) * CHUNK]
        zc = bsp
        for h in range(HEADS):
            zc = zc + jnp.where(_head_mask(h), _dot(wsp[h], vc), 0.0)
        out.append(zc)
    return jnp.concatenate(out, axis=0)


def _mix_out_fwd(o, z, m, x, wsp, bsp, wbd, psc, gsv, gout, wout, name):
    s = x.shape[0]
    tm = _tile(s, 512)

    def body(o_ref, uv_ref, m_ref, x_ref, wsp_ref, bsp_ref, wbd_ref, psc_ref, gsv_ref, gout_ref, wout_ref,
             x1_ref, mix_ref):
        g = gout_ref[...]
        an = _rms(o_ref[...], HEADS * VH)[0] * g[:, :512]
        uv = uv_ref[...]
        u, v = uv[:, :SGU], uv[:, SGU:]
        vn = (_rms(v, SGU)[0] * gsv_ref[...]).astype(BF16)
        tri = _tril()
        wsp_m = [jnp.where(tri, wsp_ref[h], 0.0).astype(BF16) for h in range(HEADS)]
        gm = u * _sgu_gate(vn, wsp_m, bsp_ref[...])
        gn = _rms(gm, SGU)[0] * g[:, 512:768]
        po = _dot(m_ref[...].astype(BF16), wbd_ref[...]) * psc_ref[...]
        pn = _rms(po, POOL)[0] * g[:, 768:]
        mix = jnp.concatenate([an, gn, pn], axis=1).astype(BF16)
        mix_ref[...] = mix
        x1_ref[...] = x_ref[...] + _dot(mix, wout_ref[...])

    row = lambda w, j: pl.BlockSpec((tm, w), lambda i: (i, j))
    return pl.pallas_call(
        body, name=name, grid=(s // tm,),
        in_specs=[row(512, 0), row(512, 1), row(POOL, 0), row(D, 0),
                  _acc((HEADS, CHUNK, CHUNK)), _acc((CHUNK, SGU)), _acc((POOL, POOL)), _acc((1, POOL)),
                  _acc((1, SGU)), _acc((1, D)), _res((D, D))],
        out_specs=[row(D, 0), row(D, 0)],
        out_shape=[jax.ShapeDtypeStruct((s, D), F32), jax.ShapeDtypeStruct((s, D), BF16)],
        compiler_params=_cp(("parallel",), VMEM_LIMIT),
    )(o, z, m, x, wsp, bsp, wbd, psc, gsv, gout, wout)


def _ffn_fwd(x1, g, wg, wu, wd, name):
    s = x1.shape[0]
    tm = _tile(s, 256)

    def body(x_ref, g_ref, wg_ref, wu_ref, wd_ref, x2_ref, a_ref, b_ref, h_ref):
        x = x_ref[...]
        h = (_rms(x, D)[0] * g_ref[...]).astype(BF16)
        h_ref[...] = h
        acc = jnp.zeros((tm, D), F32)
        for c in range(HID // HC):
            sl = slice(c * HC, (c + 1) * HC)
            a = _dot(h, wg_ref[:, sl])
            b = _dot(h, wu_ref[:, sl])
            a_ref[:, sl] = a
            b_ref[:, sl] = b
            acc = acc + _dot((a * jax.nn.sigmoid(a) * b).astype(BF16), wd_ref[sl, :])
        x2_ref[...] = x + acc

    row = lambda w: pl.BlockSpec((tm, w), lambda i: (i, 0))
    return pl.pallas_call(
        body, name=name, grid=(s // tm,),
        in_specs=[row(D), _acc((1, D)), _res((D, HID)), _res((D, HID)), _res((HID, D))],
        out_specs=[row(D), row(HID), row(HID), row(D)],
        out_shape=[jax.ShapeDtypeStruct((s, D), F32), jax.ShapeDtypeStruct((s, HID), F32),
                   jax.ShapeDtypeStruct((s, HID), F32), jax.ShapeDtypeStruct((s, D), BF16)],
        compiler_params=_cp(("parallel",), VMEM_LIMIT),
    )(x1, g, wg, wu, wd)


def _loss_grad(y, tgt):
    s = y.shape[0]
    tm = _tile(s, 512)

    def body(y_ref, t_ref, dy_ref, l_ref):
        e = y_ref[...] - t_ref[...]
        dy_ref[...] = e * (1.0 / D)
        sq = jnp.sum(e * e, axis=0, keepdims=True)
        part = sq[:, :LANES]
        for c in range(1, D // LANES):
            part = part + sq[:, c * LANES:(c + 1) * LANES]
        _accumulate(l_ref, part, pl.program_id(0) == 0)

    row = pl.BlockSpec((tm, D), lambda i: (i, 0))
    return pl.pallas_call(
        body, name="loss_grad", grid=(s // tm,),
        in_specs=[row, row], out_specs=[row, _acc((1, LANES))],
        out_shape=[jax.ShapeDtypeStruct((s, D), F32), jax.ShapeDtypeStruct((1, LANES), F32)],
        compiler_params=_cp(("arbitrary",)),
    )(y, tgt)


def _wgrad(a, b, name):
    s, k = a.shape
    n = b.shape[1]
    half = lambda v: v if v <= 1408 else v // 2
    kb, nb, tt = half(k), half(n), _tile(s, 1024)

    def body(a_ref, b_ref, o_ref):
        _accumulate(o_ref, _dot_tn(a_ref[...].astype(BF16), b_ref[...].astype(BF16)), pl.program_id(2) == 0)

    return pl.pallas_call(
        body, name=name, grid=(k // kb, n // nb, s // tt),
        in_specs=[pl.BlockSpec((tt, kb), lambda i, j, t: (t, i)), pl.BlockSpec((tt, nb), lambda i, j, t: (t, j))],
        out_specs=pl.BlockSpec((kb, nb), lambda i, j, t: (i, j)),
        out_shape=jax.ShapeDtypeStruct((k, n), F32),
        compiler_params=_cp(("parallel", "parallel", "arbitrary"), VMEM_LIMIT),
    )(a, b)


def _ffn_bwd(dx2, x1, a, b, g, wdt, wgt, wut, name):
    s = x1.shape[0]
    tm = _tile(s, 256)

    def body(dx2_ref, x_ref, a_ref, b_ref, g_ref, wdt_ref, wgt_ref, wut_ref,
             dx1_ref, hid_ref, da_ref, db_ref, dg_ref):
        dx2 = dx2_ref[...]
        dyb = dx2.astype(BF16)
        dh = jnp.zeros((tm, D), F32)
        for c in range(HID // HC):
            sl = slice(c * HC, (c + 1) * HC)
            av, bv = a_ref[:, sl], b_ref[:, sl]
            dhid = _dot(dyb, wdt_ref[:, sl])
            sig = jax.nn.sigmoid(av)
            sa = av * sig
            hid_ref[:, sl] = (sa * bv).astype(BF16)
            dbv = (dhid * sa).astype(BF16)
            dav = (dhid * bv * (sig * (1.0 + av * (1.0 - sig)))).astype(BF16)
            db_ref[:, sl] = dbv
            da_ref[:, sl] = dav
            dh = dh + _dot(dav, wgt_ref[sl, :]) + _dot(dbv, wut_ref[sl, :])
        xn, r = _rms(x_ref[...], D)
        dxr, dg = _rms_bwd(xn, r, g_ref[...], dh, D)
        dx1_ref[...] = dx2 + dxr
        _accumulate(dg_ref, dg, pl.program_id(0) == 0)

    row = lambda w: pl.BlockSpec((tm, w), lambda i: (i, 0))
    hid = jax.ShapeDtypeStruct((s, HID), BF16)
    return pl.pallas_call(
        body, name=name, grid=(s // tm,),
        in_specs=[row(D), row(D), row(HID), row(HID), _acc((1, D)), _res((D, HID)), _res((HID, D)), _res((HID, D))],
        out_specs=[row(D), row(HID), row(HID), row(HID), _acc((1, D))],
        out_shape=[jax.ShapeDtypeStruct((s, D), F32), hid, hid, hid, jax.ShapeDtypeStruct((1, D), F32)],
        compiler_params=_cp(("arbitrary",), VMEM_LIMIT),
    )(dx2, x1, a, b, g, wdt, wgt, wut)


def _mix_out_bwd(dx1, o, z, m, wsp, wspt, bsp, wbd, wbdt, psc, gsv, gout, woutt, name):
    s = dx1.shape[0]
    tm = _tile(s, 512)

    def body(dx1_ref, o_ref, uv_ref, m_ref, wsp_ref, wspt_ref, bsp_ref, wbd_ref, wbdt_ref, psc_ref, gsv_ref,
             gout_ref, woutt_ref,
             do_ref, dl_ref, duv_ref, dm_ref, dgo_ref, dgsv_ref, dpsc_ref, dwsp_ref, dbsp_ref, dwbd_ref):
        first = pl.program_id(0) == 0
        g = gout_ref[...]
        dmix = _dot(dx1_ref[...].astype(BF16), woutt_ref[...])
        o = o_ref[...]
        on, ro = _rms(o, HEADS * VH)
        do, dga = _rms_bwd(on, ro, g[:, :512], dmix[:, :512], HEADS * VH)
        for h in range(HEADS):
            sl = slice(h * VH, (h + 1) * VH)
            do_ref[h] = do[:, sl].astype(BF16)
            dl_ref[h] = jnp.broadcast_to(jnp.sum(do[:, sl] * o[:, sl], axis=-1, keepdims=True), (tm, LANES))
        uv = uv_ref[...]
        u, v = uv[:, :SGU], uv[:, SGU:]
        vx, rv = _rms(v, SGU)
        vn = (vx * gsv_ref[...]).astype(BF16)
        tri = _tril()
        wsp_m = [jnp.where(tri, wsp_ref[h], 0.0).astype(BF16) for h in range(HEADS)]
        zc = _sgu_gate(vn, wsp_m, bsp_ref[...])
        gm = u * zc
        gmn, rg = _rms(gm, SGU)
        dgm, dgg = _rms_bwd(gmn, rg, g[:, 512:768], dmix[:, 512:768], SGU)
        du = dgm * zc
        dzc = dgm * u
        wt_m = [jnp.where(_tril(upper=True), wspt_ref[h], 0.0).astype(BF16) for h in range(HEADS)]
        dvn_parts = []
        dbsp = jnp.zeros((CHUNK, SGU), F32)
        dwsp = [jnp.zeros((CHUNK, CHUNK), F32) for _ in range(HEADS)]
        for cidx in range(tm // CHUNK):
            rs = slice(cidx * CHUNK, (cidx + 1) * CHUNK)
            dzc_c = dzc[rs]
            dbsp = dbsp + dzc_c
            dzb = dzc_c.astype(BF16)
            vc = vn[rs]
            dvn_c = jnp.zeros((CHUNK, SGU), F32)
            for h in range(HEADS):
                hm = _head_mask(h)
                dvn_c = dvn_c + jnp.where(hm, _dot(wt_m[h], dzb), 0.0)
                dwsp[h] = dwsp[h] + _dot_nt(jnp.where(hm, dzc_c, 0.0).astype(BF16), vc)
            dvn_parts.append(dvn_c)
        dvn = jnp.concatenate(dvn_parts, axis=0)
        dv, dgsv = _rms_bwd(vx, rv, gsv_ref[...], dvn, SGU)
        duv_ref[...] = jnp.concatenate([du, dv], axis=1).astype(BF16)
        mb = m_ref[...].astype(BF16)
        pw = _dot(mb, wbd_ref[...])
        po = pw * psc_ref[...]
        pon, rp = _rms(po, POOL)
        dpo, dgp = _rms_bwd(pon, rp, g[:, 768:], dmix[:, 768:], POOL)
        dpw = (dpo * psc_ref[...]).astype(BF16)
        dm_ref[...] = _dot(dpw, wbdt_ref[...])
        _accumulate(dgo_ref, jnp.concatenate([dga, dgg, dgp], axis=1), first)
        _accumulate(dgsv_ref, dgsv, first)
        _accumulate(dpsc_ref, jnp.sum(dpo * pw, axis=0, keepdims=True), first)
        _accumulate(dbsp_ref, dbsp, first)
        _accumulate(dwbd_ref, _dot_tn(mb, dpw), first)
        for h in range(HEADS):
            val = jnp.where(tri, dwsp[h], 0.0)

            @pl.when(first)
            def _(val=val, h=h):
                dwsp_ref[h] = val

            @pl.when(jnp.logical_not(first))
            def _(val=val, h=h):
                dwsp_ref[h] += val

    row = lambda w, j: pl.BlockSpec((tm, w), lambda i: (i, j))
    hspec = pl.BlockSpec((HEADS, tm, HP), lambda i: (0, i, 0))
    return pl.pallas_call(
        body, name=name, grid=(s // tm,),
        in_specs=[row(D, 0), row(512, 0), row(512, 1), row(POOL, 0),
                  _acc((HEADS, CHUNK, CHUNK)), _acc((HEADS, CHUNK, CHUNK)), _acc((CHUNK, SGU)),
                  _acc((POOL, POOL)), _acc((POOL, POOL)), _acc((1, POOL)), _acc((1, SGU)), _acc((1, D)), _res((D, D))],
        out_specs=[hspec, hspec, row(512, 0), row(POOL, 0), _acc((1, D)), _acc((1, SGU)), _acc((1, POOL)),
                   _acc((HEADS, CHUNK, CHUNK)), _acc((CHUNK, SGU)), _acc((POOL, POOL))],
        out_shape=[jax.ShapeDtypeStruct((HEADS, s, HP), BF16), jax.ShapeDtypeStruct((HEADS, s, LANES), F32),
                   jax.ShapeDtypeStruct((s, 512), BF16), jax.ShapeDtypeStruct((s, POOL), F32),
                   jax.ShapeDtypeStruct((1, D), F32), jax.ShapeDtypeStruct((1, SGU), F32),
                   jax.ShapeDtypeStruct((1, POOL), F32), jax.ShapeDtypeStruct((HEADS, CHUNK, CHUNK), F32),
                   jax.ShapeDtypeStruct((CHUNK, SGU), F32), jax.ShapeDtypeStruct((POOL, POOL), F32)],
        compiler_params=_cp(("arbitrary",), VMEM_LIMIT),
    )(dx1, o, z, m, wsp, wspt, bsp, wbd, wbdt, psc, gsv, gout, woutt)


def _attn_bwd(q, k, v, do, lse, delta, name):
    s = q.shape[1]
    tq = tk = _tile(s, 512)
    nq = s // tq

    def body(q_ref, k_ref, v_ref, do_ref, lse_ref, dl_ref, dq_ref, dk_ref, dv_ref):
        j = pl.program_id(1)

        @pl.when(j == 0)
        def _():
            dq_ref[...] = jnp.zeros_like(dq_ref)

        kj, vj = k_ref[0], v_ref[0]

        def blk(i, dk, dv, masked):
            off = pl.multiple_of(i * tq, tq)
            qi = q_ref[0, pl.ds(off, tq), :]
            doi = do_ref[0, pl.ds(off, tq), :]
            lse_i = lse_ref[0, pl.ds(off, tq), :][:, :1]
            dl_i = dl_ref[0, pl.ds(off, tq), :][:, :1]
            sc = _dot_nt(qi, kj) * SCALE
            if masked:
                sc = _causal_mask(sc, tq, tk)
            p = jnp.exp(sc - lse_i)
            dp = _dot_nt(doi, vj)
            ds = (p * (dp - dl_i) * SCALE).astype(BF16)
            dv = dv + _dot_tn(p.astype(BF16), doi)
            dk = dk + _dot_tn(ds, qi)
            dq_ref[0, pl.ds(off, tq), :] += _dot(ds, kj)
            return dk, dv

        zero = jnp.zeros((tk, HP), F32)
        dk, dv = blk(j, zero, zero, True)
        dk, dv = lax.fori_loop(j + 1, nq, lambda i, c: blk(i, *c, False), (dk, dv))
        dk_ref[0] = dk
        dv_ref[0] = dv

    full = lambda: pl.BlockSpec((1, s, HP), lambda h, j: (h, 0, 0))
    blk_spec = lambda: pl.BlockSpec((1, tk, HP), lambda h, j: (h, j, 0))
    out = jax.ShapeDtypeStruct((HEADS, s, HP), F32)
    return pl.pallas_call(
        body, name=name, grid=(HEADS, s // tk),
        in_specs=[full(), blk_spec(), blk_spec(), full(), full(), full()],
        out_specs=[full(), blk_spec(), blk_spec()], out_shape=[out] * 3,
        compiler_params=_cp(("parallel", "arbitrary"), VMEM_LIMIT),
    )(q, k, v, do, lse, delta)


def _mla_prep_bwd(dq, dk, dv, z, tabs, gql, gkv, gq, gk, wq, wk, wqt, wkt, wvt, name):
    s = z.shape[0]
    tm = _tile(s, 512)

    def body(dq_ref, dk_ref, dv_ref, ql_ref, kv_ref, kr_ref, c_ref, sa_ref, sb_ref, gql_ref, gkv_ref, gq_ref, gk_ref,
             wq_ref, wk_ref, wqt_ref, wkt_ref, wvt_ref,
             dz_ref, qn_ref, kvn_ref, dqr_ref, dkr_ref, dvr_ref, dgql_ref, dgkv_ref, dgq_ref, dgk_ref):
        first = pl.program_id(0) == 0
        qx, rq = _rms(ql_ref[...], QL)
        qn = (qx * gql_ref[...]).astype(BF16)
        kx, rk = _rms(kv_ref[...], KVL)
        kvn = (kx * gkv_ref[...]).astype(BF16)
        qn_ref[...] = qn
        kvn_ref[...] = kvn
        qraw = _dot(qn, wq_ref[...])
        kraw = _dot(kvn, wk_ref[...])
        kr = kr_ref[...]
        c, sa, sb = c_ref[...], sa_ref[...], sb_ref[...]
        lane = lax.broadcasted_iota(jnp.int32, (tm, HP), 1)
        rope_lanes = (lane >= NOPE) & (lane < QK)
        dkrope = jnp.zeros((tm, HP), F32)
        dgq = jnp.zeros((1, HP), F32)
        dgk = jnp.zeros((1, HP), F32)
        for h in range(HEADS):
            sl = slice(h * HP, (h + 1) * HP)
            xn, r = _rms(qraw[:, sl], QK)
            dx, dg = _rms_bwd(xn, r, gq_ref[...], _rope_t(dq_ref[h], c, sa, sb), QK)
            dqr_ref[:, sl] = dx.astype(BF16)
            dgq = dgq + dg
            xn, r = _rms(kraw[:, sl] + kr, QK)
            dx, dg = _rms_bwd(xn, r, gk_ref[...], _rope_t(dk_ref[h], c, sa, sb), QK)
            dkr_ref[:, sl] = dx.astype(BF16)
            dgk = dgk + dg
            dkrope = dkrope + jnp.where(rope_lanes, dx, 0.0)
            dvr_ref[:, sl] = dv_ref[h].astype(BF16)
        dqn = _dot(dqr_ref[...], wqt_ref[...])
        dql, dgql = _rms_bwd(qx, rq, gql_ref[...], dqn, QL)
        dkvn = _dot(dkr_ref[...], wkt_ref[...]) + _dot(dvr_ref[...], wvt_ref[...])
        dkv, dgkv = _rms_bwd(kx, rk, gkv_ref[...], dkvn, KVL)
        dz_ref[...] = jnp.concatenate([dql, dkv, dkrope], axis=1).astype(BF16)
        _accumulate(dgql_ref, dgql, first)
        _accumulate(dgkv_ref, dgkv, first)
        _accumulate(dgq_ref, dgq, first)
        _accumulate(dgk_ref, dgk, first)

    row = lambda w, j: pl.BlockSpec((tm, w), lambda i: (i, j))
    hspec = pl.BlockSpec((HEADS, tm, HP), lambda i: (0, i, 0))
    sd = lambda w, dt: jax.ShapeDtypeStruct((s, w), dt)
    return pl.pallas_call(
        body, name=name, grid=(s // tm,),
        in_specs=[hspec, hspec, hspec, row(QL, 0), row(KVL, 2), row(HP, 3), row(HP, 0), row(HP, 0), row(HP, 0),
                  _acc((1, QL)), _acc((1, KVL)), _acc((1, HP)), _acc((1, HP)),
                  _acc((QL, HEADS * HP)), _acc((KVL, HEADS * HP)),
                  _acc((HEADS * HP, QL)), _acc((HEADS * HP, KVL)), _acc((HEADS * HP, KVL))],
        out_specs=[row(512, 0), row(QL, 0), row(KVL, 0), row(512, 0), row(512, 0), row(512, 0),
                   _acc((1, QL)), _acc((1, KVL)), _acc((1, HP)), _acc((1, HP))],
        out_shape=[sd(512, BF16), sd(QL, BF16), sd(KVL, BF16), sd(512, BF16), sd(512, BF16), sd(512, BF16),
                   jax.ShapeDtypeStruct((1, QL), F32), jax.ShapeDtypeStruct((1, KVL), F32),
                   jax.ShapeDtypeStruct((1, HP), F32), jax.ShapeDtypeStruct((1, HP), F32)],
        compiler_params=_cp(("arbitrary",), VMEM_LIMIT),
    )(dq, dk, dv, z, z, z, *tabs, gql, gkv, gq, gk, wq, wk, wqt, wkt, wvt)


def _in_proj_bwd(dzm, duv, dp, x, dx1, g, wint, name):
    s = x.shape[0]
    tm = _tile(s, 512)

    def body(dzm_ref, duv_ref, dp_ref, x_ref, dx1_ref, g_ref, w_ref, dx_ref, dg_ref):
        dh = _dot(dzm_ref[...], w_ref[0:512, :]) + _dot(duv_ref[...], w_ref[512:1024, :]) \
            + _dot(dp_ref[...], w_ref[1024:IN_P, :])
        xn, r = _rms(x_ref[...], D)
        dxr, dg = _rms_bwd(xn, r, g_ref[...], dh, D)
        dx_ref[...] = dx1_ref[...] + dxr
        _accumulate(dg_ref, dg, pl.program_id(0) == 0)

    row = lambda w: pl.BlockSpec((tm, w), lambda i: (i, 0))
    return pl.pallas_call(
        body, name=name, grid=(s // tm,),
        in_specs=[row(512), row(512), row(POOL), row(D), row(D), _acc((1, D)), _res((IN_P, D))],
        out_specs=[row(D), _acc((1, D))],
        out_shape=[jax.ShapeDtypeStruct((s, D), F32), jax.ShapeDtypeStruct((1, D), F32)],
        compiler_params=_cp(("arbitrary",), VMEM_LIMIT),
    )(dzm, duv, dp, x, dx1, g, wint)


def _adamw(w, g, m, v, name):
    r, c = w.shape
    tr = _row_tile(r, 512)
    c1 = 1.0 - B1 ** STEP
    c2 = 1.0 - B2 ** STEP

    def body(w_ref, g_ref, m_ref, v_ref, d_ref, nm_ref, nv_ref):
        gv = g_ref[...]
        nm = B1 * m_ref[...] + (1.0 - B1) * gv
        nv = B2 * v_ref[...] + (1.0 - B2) * (gv * gv)
        nm_ref[...] = nm
        nv_ref[...] = nv
        d_ref[...] = -LR * ((nm / c1) / (jnp.sqrt(nv / c2) + ADAM_EPS) + WD * w_ref[...])

    spec = pl.BlockSpec((tr, c), lambda i: (i, 0))
    out = jax.ShapeDtypeStruct((r, c), F32)
    return pl.pallas_call(
        body, name=name, grid=(r // tr,), in_specs=[spec] * 4, out_specs=[spec] * 3, out_shape=[out] * 3,
        compiler_params=_cp(("parallel",)),
    )(w, g, m, v)


ANY = pl.BlockSpec(memory_space=pl.ANY)


def _place():
    x, y, c = lax.axis_index("x"), lax.axis_index("y"), lax.axis_index("c")
    chips = [(1 - x, y), (x, 1 - y), (1 - x, 1 - y)]
    return x, y, c, chips


def _all_gather_chips(shard, name):
    rows, cols = shard.shape
    half = rows // 2
    align = 16 if shard.dtype == BF16 else 8
    assert half % align == 0

    def body(x_ref, out_ref, send_sems, recv_sems, local_sem):
        x, y, c, chips = _place()
        me = 2 * x + y
        sibling = (x, y, 1 - c)

        def part(ref, kk, hh):
            return ref.at[kk, pl.ds(pl.multiple_of(hh * half, align), half), :]

        def copy(sem, src, dst, to):
            return pltpu.make_async_remote_copy(src_ref=src, dst_ref=dst, send_sem=send_sems.at[sem],
                                                recv_sem=recv_sems.at[sem], device_id=to, device_id_type=MESH)

        mine = pltpu.make_async_copy(x_ref, out_ref.at[me], local_sem)
        mine.start()
        my_half = x_ref.at[pl.ds(pl.multiple_of(c * half, align), half), :]
        first = [copy(j, my_half, part(out_ref, me, c), (cx, cy, c)) for j, (cx, cy) in enumerate(chips)]
        for cp in first:
            cp.start()
        passed = []
        for j, (cx, cy) in enumerate(chips):
            landed = part(out_ref, 2 * cx + cy, c)
            copy(j, landed, landed, (cx, cy, c)).wait_recv()
            fwd = copy(3 + j, landed, landed, sibling)
            fwd.start()
            passed.append(fwd)
        for j, (cx, cy) in enumerate(chips):
            other = part(out_ref, 2 * cx + cy, 1 - c)
            copy(3 + j, other, other, sibling).wait_recv()
        for cp in first + passed:
            cp.wait_send()
        mine.wait()

    return pl.pallas_call(
        body, name=name, in_specs=[ANY], out_specs=ANY,
        out_shape=jax.ShapeDtypeStruct((4, rows, cols), shard.dtype),
        scratch_shapes=[pltpu.SemaphoreType.DMA((6,)), pltpu.SemaphoreType.DMA((6,)), pltpu.SemaphoreType.DMA],
        compiler_params=pltpu.CompilerParams(has_side_effects=True),
    )(shard)


def _pair_swap_halves(g2, name):
    _, nk, half, cols = g2.shape

    def body(g_ref, out_ref, send_sem, recv_sem):
        x, y, c, _ = _place()
        cp = pltpu.make_async_remote_copy(src_ref=g_ref.at[1 - c], dst_ref=out_ref, send_sem=send_sem, recv_sem=recv_sem,
                                          device_id=(x, y, 1 - c), device_id_type=MESH)
        cp.start()
        cp.wait()

    return pl.pallas_call(
        body, name=name, in_specs=[ANY], out_specs=ANY,
        out_shape=jax.ShapeDtypeStruct((nk, half, cols), g2.dtype),
        scratch_shapes=[pltpu.SemaphoreType.DMA, pltpu.SemaphoreType.DMA],
        compiler_params=pltpu.CompilerParams(has_side_effects=True),
    )(g2)


def _pair_add(g2, got, cidx, name):
    _, nk, half, cols = g2.shape
    tr = 304 if half % 304 == 0 else half
    assert half % tr == 0 and tr % 8 == 0

    grid_spec = pltpu.PrefetchScalarGridSpec(
        num_scalar_prefetch=1, grid=(nk, half // tr),
        in_specs=[pl.BlockSpec((1, 1, tr, cols), lambda k, r, c_ref: (c_ref[0], k, r, 0)),
                  pl.BlockSpec((1, tr, cols), lambda k, r, c_ref: (k, r, 0))],
        out_specs=pl.BlockSpec((1, tr, cols), lambda k, r, c_ref: (k, r, 0)))

    def body(c_ref, a_ref, b_ref, o_ref):
        o_ref[0] = a_ref[0, 0] + b_ref[0]

    return pl.pallas_call(
        body, name=name, grid_spec=grid_spec, out_shape=jax.ShapeDtypeStruct((nk, half, cols), g2.dtype),
        compiler_params=_cp(("parallel", "parallel")),
    )(cidx, g2, got)


def _chip_scatter(p, name):
    nk, half, cols = p.shape

    def body(p_ref, q_ref, send_sems, recv_sems, local_sem):
        x, y, c, chips = _place()
        me = 2 * x + y
        mine = pltpu.make_async_copy(p_ref.at[me], q_ref.at[me], local_sem)
        mine.start()
        sends = []
        for j, (cx, cy) in enumerate(chips):
            cp = pltpu.make_async_remote_copy(src_ref=p_ref.at[2 * cx + cy], dst_ref=q_ref.at[me],
                                              send_sem=send_sems.at[j], recv_sem=recv_sems.at[j],
                                              device_id=(cx, cy, c), device_id_type=MESH)
            cp.start()
            sends.append(cp)
        for j, (cx, cy) in enumerate(chips):
            slot = q_ref.at[2 * cx + cy]
            pltpu.make_async_remote_copy(src_ref=slot, dst_ref=slot, send_sem=send_sems.at[j], recv_sem=recv_sems.at[j],
                                         device_id=(cx, cy, c), device_id_type=MESH).wait_recv()
        for cp in sends:
            cp.wait_send()
        mine.wait()

    return pl.pallas_call(
        body, name=name, in_specs=[ANY], out_specs=ANY, out_shape=jax.ShapeDtypeStruct(p.shape, p.dtype),
        scratch_shapes=[pltpu.SemaphoreType.DMA((3,)), pltpu.SemaphoreType.DMA((3,)), pltpu.SemaphoreType.DMA],
        compiler_params=pltpu.CompilerParams(has_side_effects=True),
    )(p)


def _sum_chips(q, name):
    nk, half, cols = q.shape
    tr = 304 if half % 304 == 0 else half

    def body(q_ref, o_ref):
        o_ref[...] = ((q_ref[0] + q_ref[1]) + q_ref[2]) + q_ref[3]

    return pl.pallas_call(
        body, name=name, grid=(half // tr,),
        in_specs=[pl.BlockSpec((nk, tr, cols), lambda r: (0, r, 0))],
        out_specs=pl.BlockSpec((tr, cols), lambda r: (r, 0)),
        out_shape=jax.ShapeDtypeStruct((half, cols), q.dtype), compiler_params=_cp(("parallel",)),
    )(q)


def _pair_join(mine, name):
    half, cols = mine.shape

    def body(a_ref, out_ref, send_sem, recv_sem, local_sem):
        x, y, c, _ = _place()
        dst = out_ref.at[pl.ds(pl.multiple_of(c * half, 8), half), :]
        loc = pltpu.make_async_copy(a_ref, dst, local_sem)
        loc.start()
        cp = pltpu.make_async_remote_copy(src_ref=a_ref, dst_ref=dst, send_sem=send_sem, recv_sem=recv_sem,
                                          device_id=(x, y, 1 - c), device_id_type=MESH)
        cp.start()
        cp.wait()
        loc.wait()

    return pl.pallas_call(
        body, name=name, in_specs=[ANY], out_specs=ANY, out_shape=jax.ShapeDtypeStruct((2 * half, cols), mine.dtype),
        scratch_shapes=[pltpu.SemaphoreType.DMA, pltpu.SemaphoreType.DMA, pltpu.SemaphoreType.DMA],
        compiler_params=pltpu.CompilerParams(has_side_effects=True),
    )(mine)


BIG = [("w_in", (D, IN_W), 1), ("w_q_up", (QL, HEADS * QK), 1), ("w_kv_up", (KVL, HEADS * (NOPE + VH)), 1),
       ("w_out", (D, D), 0), ("w_gate", (D, HID), 1), ("w_up", (D, HID), 1), ("w_down", (HID, D), 0)]
SMALL = [("g_mix_norm", (D,)), ("g_q_lat", (QL,)), ("g_kv_lat", (KVL,)), ("g_q_head", (QK,)), ("g_k_head", (QK,)),
         ("g_sgu_v", (SGU,)), ("w_spatial", (HEADS, CHUNK, CHUNK)), ("b_spatial", (HEADS, CHUNK)),
         ("w_pool", (4, 64, 64)), ("pool_scale", (POOL,)), ("g_out_mla", (512,)), ("g_out_sgu", (SGU,)),
         ("g_out_pool", (POOL,)), ("g_ffn_norm", (D,))]
ORDER = ["g_mix_norm", "w_in", "g_q_lat", "w_q_up", "g_kv_lat", "w_kv_up", "g_q_head", "g_k_head", "g_sgu_v",
         "w_spatial", "b_spatial", "w_pool", "pool_scale", "g_out_mla", "g_out_sgu", "g_out_pool", "w_out",
         "g_ffn_norm", "w_gate", "w_up", "w_down"]
DEPTH = 2
COLS = 1024
BIG_N = sum(r * c // 4 for _, (r, c), _ in BIG) * DEPTH
SMALL_N = sum(math.prod(s) for _, s in SMALL) * DEPTH
assert BIG_N % COLS == 0 and SMALL_N % 4 == 0
BIG_ROWS = BIG_N // COLS
W_ROWS = -(-BIG_ROWS // 32) * 32
SMALL_ROWS = -(-(SMALL_N // 4) // (16 * COLS)) * 16
G_ROWS = BIG_ROWS + SMALL_ROWS
assert G_ROWS % 16 == 0


def _shard_shape(shape, axis):
    r, c = shape
    return (r // 4, c) if axis == 0 else (r, c // 4)


def _pack_weight_shards(p):
    flat = [p[n][l].reshape(-1) for l in range(DEPTH) for n, _, _ in BIG]
    flat = jnp.concatenate(flat).astype(BF16)
    return jnp.pad(flat, (0, W_ROWS * COLS - BIG_N)).reshape(W_ROWS, COLS)


def _unpack_weights(gathered):
    flat = gathered.reshape(4, W_ROWS * COLS)
    out, off = [], 0
    for _ in range(DEPTH):
        layer = {}
        for n, shape, axis in BIG:
            r, c = _shard_shape(shape, axis)
            piece = flat[:, off:off + r * c].reshape(4, r, c)
            off += r * c
            layer[n] = piece.reshape(4 * r, c) if axis == 0 else piece.transpose(1, 0, 2).reshape(r, 4 * c)
        out.append(layer)
    return out


def _kernel_weights(w):
    win = w["w_in"]
    zeros = lambda r, c: jnp.zeros((r, c), BF16)
    o1, o2, o3, o4 = QL, QL + KVL, QL + KVL + ROPE, QL + KVL + ROPE + 2 * SGU
    win_p = jnp.concatenate([win[:, :o2], zeros(D, NOPE), win[:, o2:o3], zeros(D, HP - QK), win[:, o3:o4], win[:, o4:]], axis=1)
    wq = w["w_q_up"].reshape(QL, HEADS, QK)
    wq_p = jnp.pad(wq, ((0, 0), (0, 0), (0, HP - QK))).reshape(QL, HEADS * HP)
    wkv = w["w_kv_up"].reshape(KVL, HEADS, NOPE + VH)
    wk_p = jnp.pad(wkv[:, :, :NOPE], ((0, 0), (0, 0), (0, HP - NOPE))).reshape(KVL, HEADS * HP)
    wv_p = wkv[:, :, NOPE:].reshape(KVL, HEADS * VH)
    return dict(win=win_p, wint=win_p.T, wq=wq_p, wqt=wq_p.T, wk=wk_p, wkt=wk_p.T, wv=wv_p, wvt=wv_p.T,
                wout=w["w_out"], woutt=w["w_out"].T, wg=w["w_gate"], wgt=w["w_gate"].T,
                wu=w["w_up"], wut=w["w_up"].T, wd=w["w_down"], wdt=w["w_down"].T)


def _small_operands(p, l):
    row = lambda v: v.reshape(1, -1)
    pad = lambda v: jnp.pad(v, (0, HP - QK)).reshape(1, HP)
    wpool = p["w_pool"][l]
    wbd = jnp.zeros((POOL, POOL), F32)
    for g in range(4):
        wbd = lax.dynamic_update_slice(wbd, wpool[g], (g * 64, g * 64))
    return dict(
        g_mix=row(p["g_mix_norm"][l]), gql=row(p["g_q_lat"][l]), gkv=row(p["g_kv_lat"][l]),
        gq=pad(p["g_q_head"][l]), gk=pad(p["g_k_head"][l]), gsv=row(p["g_sgu_v"][l]),
        wsp=p["w_spatial"][l], wspt=p["w_spatial"][l].transpose(0, 2, 1),
        bsp=jnp.repeat(p["b_spatial"][l].T, SGU // HEADS, axis=1),
        wbd=wbd.astype(BF16), wbdt=wbd.T.astype(BF16), psc=row(p["pool_scale"][l]),
        gout=jnp.concatenate([p["g_out_mla"][l], p["g_out_sgu"][l], p["g_out_pool"][l]]).reshape(1, D),
        g_ffn=row(p["g_ffn_norm"][l]))


def _big_grads(g):
    dwin = g["win"]
    o2 = QL + KVL
    gin = jnp.concatenate([dwin[:, :o2], dwin[:, o2 + NOPE:o2 + NOPE + ROPE], dwin[:, 512:]], axis=1)
    gq = g["wq"].reshape(QL, HEADS, HP)[:, :, :QK].reshape(QL, HEADS * QK)
    gk = g["wk"].reshape(KVL, HEADS, HP)[:, :, :NOPE]
    gv = g["wv"].reshape(KVL, HEADS, VH)
    gkv = jnp.concatenate([gk, gv], axis=2).reshape(KVL, HEADS * (NOPE + VH))
    return {"w_in": gin, "w_q_up": gq, "w_kv_up": gkv, "w_out": g["wout"], "w_gate": g["wg"], "w_up": g["wu"],
            "w_down": g["wd"]}


def _small_grads(g):
    go = g["gout"].reshape(-1)
    return {"g_mix_norm": g["g_mix"].reshape(-1), "g_q_lat": g["gql"].reshape(-1), "g_kv_lat": g["gkv"].reshape(-1),
            "g_q_head": g["gq"].reshape(-1)[:QK], "g_k_head": g["gk"].reshape(-1)[:QK], "g_sgu_v": g["gsv"].reshape(-1),
            "w_spatial": g["wsp"], "b_spatial": g["bsp"].reshape(CHUNK, HEADS, SGU // HEADS).sum(-1).T,
            "w_pool": jnp.stack([g["wbd"][i * 64:(i + 1) * 64, i * 64:(i + 1) * 64] for i in range(4)]),
            "pool_scale": g["psc"].reshape(-1), "g_out_mla": go[:512], "g_out_sgu": go[512:768],
            "g_out_pool": go[768:], "g_ffn_norm": g["g_ffn"].reshape(-1)}


def _pack_grads(big, small):
    parts = []
    for l in range(DEPTH):
        for n, shape, axis in BIG:
            r, c = _shard_shape(shape, axis)
            full = big[l][n]
            sh = full.reshape(4, r, c) if axis == 0 else full.reshape(r, 4, c).transpose(1, 0, 2)
            parts.append(sh.reshape(4, r * c))
    sm = jnp.concatenate([small[l][n].reshape(-1) for l in range(DEPTH) for n, _ in SMALL]).reshape(4, SMALL_N // 4)
    parts.append(jnp.pad(sm, ((0, 0), (0, SMALL_ROWS * COLS - SMALL_N // 4))))
    g = jnp.concatenate(parts, axis=1).reshape(4, 2, G_ROWS // 2, COLS)
    return g.transpose(1, 0, 2, 3)


def _unpack_big_grads(gsum):
    flat = gsum[:BIG_ROWS].reshape(-1)
    out, off = {n: [] for n, _, _ in BIG}, 0
    for _ in range(DEPTH):
        for n, shape, axis in BIG:
            r, c = _shard_shape(shape, axis)
            out[n].append(flat[off:off + r * c].reshape(r, c))
            off += r * c
    return {n: jnp.stack(v) for n, v in out.items()}


def _unpack_small_grads(gathered):
    flat = gathered.reshape(4, SMALL_ROWS * COLS)[:, :SMALL_N // 4].reshape(-1)
    out, off = {n: [] for n, _ in SMALL}, 0
    for _ in range(DEPTH):
        for n, shape in SMALL:
            k = math.prod(shape)
            out[n].append(flat[off:off + k].reshape(shape))
            off += k
    return {n: jnp.stack(v) for n, v in out.items()}


def _layer_fwd(x, tabs, kw, sp, l):
    t = f"_l{l}"
    z, hb = _in_proj_fwd(x, sp["g_mix"], kw["win"], "in_proj_fwd" + t)
    q, k, v = _mla_prep_fwd(z, tabs, sp["gql"], sp["gkv"], sp["gq"], sp["gk"], kw["wq"], kw["wk"], kw["wv"],
                            "mla_prep_fwd" + t)
    o, lse = _attn_fwd(q, k, v, "attn_fwd" + t)
    m = _pool_win_fwd(z, "pool_win_fwd" + t)
    x1, mix = _mix_out_fwd(o, z, m, x, sp["wsp"], sp["bsp"], sp["wbd"], sp["psc"], sp["gsv"], sp["gout"], kw["wout"],
                           "mix_out_fwd" + t)
    x2, a, b, h2 = _ffn_fwd(x1, sp["g_ffn"], kw["wg"], kw["wu"], kw["wd"], "ffn_fwd" + t)
    saved = dict(x=x, z=z, hb=hb, q=q, k=k, v=v, o=o, lse=lse, m=m, x1=x1, mix=mix, a=a, b=b, h2=h2)
    return x2, saved


def _layer_bwd(dx2, sv, tabs, kw, sp, l):
    t = f"_l{l}"
    g = {}
    dx1, hid, da, db, g["g_ffn"] = _ffn_bwd(dx2, sv["x1"], sv["a"], sv["b"], sp["g_ffn"], kw["wdt"], kw["wgt"], kw["wut"],
                                            "ffn_bwd" + t)
    g["wd"] = _wgrad(hid, dx2, "wgrad_down" + t)
    g["wg"] = _wgrad(sv["h2"], da, "wgrad_gate" + t)
    g["wu"] = _wgrad(sv["h2"], db, "wgrad_up" + t)
    do, delta, duv, dm, g["gout"], g["gsv"], g["psc"], g["wsp"], g["bsp"], g["wbd"] = _mix_out_bwd(
        dx1, sv["o"], sv["z"], sv["m"], sp["wsp"], sp["wspt"], sp["bsp"], sp["wbd"], sp["wbdt"], sp["psc"], sp["gsv"],
        sp["gout"], kw["woutt"], "mix_out_bwd" + t)
    g["wout"] = _wgrad(sv["mix"], dx1, "wgrad_out" + t)
    dp = _pool_win_bwd(dm, "pool_win_bwd" + t)
    dq, dk, dv = _attn_bwd(sv["q"], sv["k"], sv["v"], do, sv["lse"], delta, "attn_bwd" + t)
    dzm, qn, kvn, dqr, dkr, dvr, g["gql"], g["gkv"], g["gq"], g["gk"] = _mla_prep_bwd(
        dq, dk, dv, sv["z"], tabs, sp["gql"], sp["gkv"], sp["gq"], sp["gk"], kw["wq"], kw["wk"], kw["wqt"], kw["wkt"],
        kw["wvt"], "mla_prep_bwd" + t)
    g["wq"] = _wgrad(qn, dqr, "wgrad_q_up" + t)
    g["wk"] = _wgrad(kvn, dkr, "wgrad_k_up" + t)
    g["wv"] = _wgrad(kvn, dvr, "wgrad_v_up" + t)
    dx, g["g_mix"] = _in_proj_bwd(dzm, duv, dp, sv["x"], dx1, sp["g_mix"], kw["wint"], "in_proj_bwd" + t)
    g["win"] = jnp.concatenate([_wgrad(sv["hb"], dzm, "wgrad_in_a" + t), _wgrad(sv["hb"], duv, "wgrad_in_b" + t),
                                _wgrad(sv["hb"], dp, "wgrad_in_c" + t)], axis=1)
    return dx, g


def _rope_inv_freq():
    half = ROPE // 2
    inv = 1.0 / (ROPE_THETA ** (jnp.arange(half, dtype=F32) / half))
    return jnp.concatenate([jnp.zeros((NOPE,), F32), inv, inv, jnp.zeros((HP - QK,), F32)]).reshape(1, HP)


def kernel(x, positions, g_mix_norm, w_in, g_q_lat, w_q_up, g_kv_lat, w_kv_up, g_q_head, g_k_head, g_sgu_v, w_spatial, b_spatial, w_pool, pool_scale, g_out_mla, g_out_sgu, g_out_pool, w_out, g_ffn_norm, w_gate, w_up, w_down, loss_target, m_g_mix_norm, m_w_in, m_g_q_lat, m_w_q_up, m_g_kv_lat, m_w_kv_up, m_g_q_head, m_g_k_head, m_g_sgu_v, m_w_spatial, m_b_spatial, m_w_pool, m_pool_scale, m_g_out_mla, m_g_out_sgu, m_g_out_pool, m_w_out, m_g_ffn_norm, m_w_gate, m_w_up, m_w_down, v_g_mix_norm, v_w_in, v_g_q_lat, v_w_q_up, v_g_kv_lat, v_w_kv_up, v_g_q_head, v_g_k_head, v_g_sgu_v, v_w_spatial, v_b_spatial, v_w_pool, v_pool_scale, v_g_out_mla, v_g_out_sgu, v_g_out_pool, v_w_out, v_g_ffn_norm, v_w_gate, v_w_up, v_w_down):
    given = dict(locals())
    p = {n: given[n] for n in ORDER}
    seq = x.shape[1]
    xs = x.reshape(seq, D)
    tgt = loss_target.reshape(seq, D)

    full = _unpack_weights(_all_gather_chips(_pack_weight_shards(p), "all_gather_weights"))
    kws = [_kernel_weights(full[l]) for l in range(DEPTH)]
    sps = [_small_operands(p, l) for l in range(DEPTH)]
    tabs = _rope_tables(positions.reshape(seq, 1), _rope_inv_freq())

    saved, h = [], xs
    for l in range(DEPTH):
        h, sv = _layer_fwd(h, tabs, kws[l], sps[l], l)
        saved.append(sv)
    dy, lpart = _loss_grad(h, tgt)
    loss = lax.psum(0.5 / D * jnp.sum(lpart), ("x", "y", "c"))
    grads = [None] * DEPTH
    for l in reversed(range(DEPTH)):
        dy, grads[l] = _layer_bwd(dy, saved[l], tabs, kws[l], sps[l], l)

    g2 = _pack_grads([_big_grads(g) for g in grads], [_small_grads(g) for g in grads])
    cidx = lax.axis_index("c").astype(jnp.int32).reshape(1)
    pair = _pair_add(g2, _pair_swap_halves(g2, "grad_pair_swap"), cidx, "grad_pair_add")
    gsum = _pair_join(_sum_chips(_chip_scatter(pair, "grad_chip_scatter"), "grad_chip_sum"), "grad_pair_join")
    gw = _unpack_big_grads(gsum)
    gw.update(_unpack_small_grads(_all_gather_chips(gsum[BIG_ROWS:], "all_gather_small_grads")))

    delta, new_m, new_v = {}, {}, {}
    for n in ORDER:
        w = p[n]
        two_d = (-1, w.shape[-1])
        d, nm, nv = _adamw(w.reshape(two_d), gw[n].reshape(two_d), given["m_" + n].reshape(two_d),
                           given["v_" + n].reshape(two_d), "adamw_" + n)
        delta[n], new_m[n], new_v[n] = d.reshape(w.shape), nm.reshape(w.shape), nv.reshape(w.shape)
    return (loss, dy.reshape(x.shape), *[gw[n] for n in ORDER], *[delta[n] for n in ORDER],
            *[new_m[n] for n in ORDER], *[new_v[n] for n in ORDER])
```

```python
import functools
import math

import jax
import jax.numpy as jnp
from jax import lax
from jax.experimental import pallas as pl
from jax.experimental.pallas import tpu as pltpu

F32 = jnp.float32
BF16 = jnp.bfloat16
MESH = pl.DeviceIdType.MESH

D = 1024
HEADS = 4
QK = 96
NOPE = 64
ROPE = 32
VH = 128
HP = 128
QL = 256
KVL = 128
SGU = 256
POOL = 256
CHUNK = 128
HID = 2816
IN_W = 1184
IN_P = 1280
EPS = 1e-6
ROPE_THETA = 10000.0
SCALE = 1.0 / math.sqrt(QK)
LOG2E = 1.4426950408889634
EXP2_C = SCALE * LOG2E
ATT_SPLIT = 2
ATT_WIDE = 4
NEG = -1e30
HALO = 16

LR, B1, B2, ADAM_EPS, WD, STEP = 0.001, 0.9, 0.999, 1e-08, 0.01, 10

VMEM_LIMIT = 56 * 1024 * 1024
LANES = 128
HC = 256


def _cp(sem, vmem=None):
    return pltpu.CompilerParams(dimension_semantics=sem, vmem_limit_bytes=vmem)


def _res(shape):
    nd = len(shape)
    return pl.BlockSpec(shape, lambda *_: (0,) * nd, pipeline_mode=pl.Buffered(1))


def _acc(shape):
    nd = len(shape)
    return pl.BlockSpec(shape, lambda *_: (0,) * nd)


def _dot(a, b):
    return jnp.dot(a, b, preferred_element_type=F32)


def _dot_nt(a, b):
    return lax.dot_general(a, b, (((1,), (1,)), ((), ())), preferred_element_type=F32)


def _dot_tn(a, b):
    return lax.dot_general(a, b, (((0,), (0,)), ((), ())), preferred_element_type=F32)


def _rms(x, n):
    r = lax.rsqrt(jnp.sum(x * x, axis=-1, keepdims=True) * (1.0 / n) + EPS)
    return x * r, r


def _rms_bwd(xn, r, g, dy, n):
    dn = dy * g
    dx = r * (dn - xn * (jnp.sum(dn * xn, axis=-1, keepdims=True) * (1.0 / n)))
    return dx, jnp.sum(dy * xn, axis=0, keepdims=True)


def _accumulate(ref, val, first):
    @pl.when(first)
    def _():
        ref[...] = val

    @pl.when(jnp.logical_not(first))
    def _():
        ref[...] += val


def _tile(s, t):
    return min(s, t)


def _row_tile(r, cap):
    if r <= cap:
        return r
    return max(t for t in range(8, cap + 1, 8) if r % t == 0)


def _rope_tables(pos, invf):
    s = pos.shape[0]
    tm = _tile(s, 1024)

    def body(pos_ref, invf_ref, c_ref, sa_ref, sb_ref):
        ang = pos_ref[...].astype(F32) * invf_ref[...]
        c, sn = jnp.cos(ang), jnp.sin(ang)
        lane = lax.broadcasted_iota(jnp.int32, ang.shape, 1)
        first = (lane >= NOPE) & (lane < NOPE + ROPE // 2)
        second = (lane >= NOPE + ROPE // 2) & (lane < QK)
        c_ref[...] = jnp.where(first | second, c, 1.0)
        sa_ref[...] = jnp.where(first, -sn, 0.0)
        sb_ref[...] = jnp.where(second, sn, 0.0)

    out = jax.ShapeDtypeStruct((s, HP), F32)
    return pl.pallas_call(
        body, name="rope_tables", grid=(s // tm,),
        in_specs=[pl.BlockSpec((tm, 1), lambda i: (i, 0)), _acc((1, HP))],
        out_specs=[pl.BlockSpec((tm, HP), lambda i: (i, 0))] * 3,
        out_shape=[out] * 3, compiler_params=_cp(("parallel",)),
    )(pos, invf)


def _rope(x, c, sa, sb):
    return x * c + pltpu.roll(x, HP - ROPE // 2, 1) * sa + pltpu.roll(x, ROPE // 2, 1) * sb


def _rope_t(d, c, sa, sb):
    return d * c + pltpu.roll(d * sa, ROPE // 2, 1) + pltpu.roll(d * sb, HP - ROPE // 2, 1)


def _in_proj_fwd(x, g, w, name):
    s = x.shape[0]
    tm = _tile(s, 512)

    def body(x_ref, g_ref, w_ref, z_ref, h_ref):
        xn, _ = _rms(x_ref[...], D)
        h = (xn * g_ref[...]).astype(BF16)
        h_ref[...] = h
        z_ref[...] = _dot(h, w_ref[...])

    return pl.pallas_call(
        body, name=name, grid=(s // tm,),
        in_specs=[pl.BlockSpec((tm, D), lambda i: (i, 0)), _acc((1, D)), _res((D, IN_P))],
        out_specs=[pl.BlockSpec((tm, IN_P), lambda i: (i, 0)), pl.BlockSpec((tm, D), lambda i: (i, 0))],
        out_shape=[jax.ShapeDtypeStruct((s, IN_P), F32), jax.ShapeDtypeStruct((s, D), BF16)],
        compiler_params=_cp(("parallel",), VMEM_LIMIT),
    )(x, g, w)


def _mla_prep_fwd(z, tabs, gql, gkv, gq, gk, wq, wk, wv, name):
    s = z.shape[0]
    tm = _tile(s, 512)

    def body(ql_ref, kv_ref, kr_ref, c_ref, sa_ref, sb_ref, gql_ref, gkv_ref, gq_ref, gk_ref,
             wq_ref, wk_ref, wv_ref, q_out, k_out, v_out):
        qn = (_rms(ql_ref[...], QL)[0] * gql_ref[...]).astype(BF16)
        kvn = (_rms(kv_ref[...], KVL)[0] * gkv_ref[...]).astype(BF16)
        qraw = _dot(qn, wq_ref[...])
        kraw = _dot(kvn, wk_ref[...])
        vraw = _dot(kvn, wv_ref[...])
        kr = kr_ref[...]
        c, sa, sb = c_ref[...], sa_ref[...], sb_ref[...]
        for h in range(HEADS):
            sl = slice(h * HP, (h + 1) * HP)
            xq = _rms(qraw[:, sl], QK)[0] * gq_ref[...]
            q_out[h] = _rope(xq, c, sa, sb).astype(BF16)
            xk = _rms(kraw[:, sl] + kr, QK)[0] * gk_ref[...]
            k_out[h] = _rope(xk, c, sa, sb).astype(BF16)
            v_out[h] = vraw[:, sl].astype(BF16)

    row = lambda w, j: pl.BlockSpec((tm, w), lambda i: (i, j))
    hspec = pl.BlockSpec((HEADS, tm, HP), lambda i: (0, i, 0))
    hshape = jax.ShapeDtypeStruct((HEADS, s, HP), BF16)
    return pl.pallas_call(
        body, name=name, grid=(s // tm,),
        in_specs=[row(QL, 0), row(KVL, 2), row(HP, 3), row(HP, 0), row(HP, 0), row(HP, 0),
                  _acc((1, QL)), _acc((1, KVL)), _acc((1, HP)), _acc((1, HP)),
                  _acc((QL, HEADS * HP)), _acc((KVL, HEADS * HP)), _acc((KVL, HEADS * HP))],
        out_specs=[hspec] * 3, out_shape=[hshape] * 3,
        compiler_params=_cp(("parallel",)),
    )(z, z, z, *tabs, gql, gkv, gq, gk, wq, wk, wv)


def _causal_mask(s, row0):
    row = lax.broadcasted_iota(jnp.int32, s.shape, 0) + row0
    col = lax.broadcasted_iota(jnp.int32, s.shape, 1)
    return jnp.where(col <= row, s, NEG)


def _attn_fwd(q, k, v, name):
    s = q.shape[1]
    tq = _tile(s, 512)
    wide = ATT_WIDE * tq if s % (ATT_WIDE * tq) == 0 else tq
    rh = tq // ATT_SPLIT

    def body(q_ref, k_ref, v_ref, o_ref, lse_ref):
        i = pl.program_id(1)

        def blk(off, tk, carry, masked):
            off = pl.multiple_of(off, tq)
            kj = k_ref[0, pl.ds(off, tk), :]
            vj = v_ref[0, pl.ds(off, tk), :]
            out = []
            scs = [_dot_nt(q_ref[0, g * rh:(g + 1) * rh, :], kj) for g in range(ATT_SPLIT)]
            for g, (m, l, acc) in enumerate(carry):
                sc = scs[g]
                if masked:
                    sc = _causal_mask(sc, g * rh)
                m_new = jnp.maximum(m, jnp.max(sc, axis=-1, keepdims=True))
                p = jnp.exp2((sc - m_new) * EXP2_C)
                alpha = jnp.exp2((m - m_new) * EXP2_C)
                l = alpha * l + jnp.sum(p, axis=-1, keepdims=True)
                acc = alpha * acc + _dot(p.astype(BF16), vj)
                out.append((m_new, l, acc))
            return tuple(out)

        one = (jnp.full((rh, 1), NEG, F32), jnp.zeros((rh, 1), F32), jnp.zeros((rh, VH), F32))
        nwide = (i * tq) // wide
        carry = lax.fori_loop(0, nwide, lambda j, c: blk(j * wide, wide, c, False), (one,) * ATT_SPLIT)
        carry = lax.fori_loop(nwide * (wide // tq), i, lambda j, c: blk(j * tq, tq, c, False), carry)
        carry = blk(i * tq, tq, carry, True)
        for g, (m, l, acc) in enumerate(carry):
            o_ref[g * rh:(g + 1) * rh, :] = acc / l
            lse_ref[0, g * rh:(g + 1) * rh, :] = jnp.broadcast_to(m * EXP2_C + jnp.log(l) * LOG2E, (rh, LANES))

    return pl.pallas_call(
        body, name=name, grid=(HEADS, s // tq),
        in_specs=[pl.BlockSpec((1, tq, HP), lambda h, i: (h, i, 0)),
                  pl.BlockSpec((1, s, HP), lambda h, i: (h, 0, 0)),
                  pl.BlockSpec((1, s, HP), lambda h, i: (h, 0, 0))],
        out_specs=[pl.BlockSpec((tq, VH), lambda h, i: (i, h)),
                   pl.BlockSpec((1, tq, LANES), lambda h, i: (h, i, 0))],
        out_shape=[jax.ShapeDtypeStruct((s, HEADS * VH), F32), jax.ShapeDtypeStruct((HEADS, s, LANES), F32)],
        compiler_params=_cp(("parallel", "arbitrary"), VMEM_LIMIT),
    )(q, k, v)


def _lane_group(shape, j):
    return (lax.broadcasted_iota(jnp.int32, shape, 1) + j * LANES) // (POOL // 4)


def _pool_win_fwd(z, name):
    s = z.shape[0]
    ch = _tile(s, 512)
    col0 = (IN_P - POOL) // LANES

    def body(p_ref, m_ref):
        j = pl.program_id(0)

        def chunk(r, _):
            off = pl.multiple_of(r * ch, ch)
            cur = p_ref[pl.ds(off, ch), :]
            hoff = pl.multiple_of(jnp.maximum(off - HALO, 0), 8)
            halo = jnp.where(r > 0, p_ref[pl.ds(hoff, HALO), :], 0.0)
            x = jnp.concatenate([halo, cur], axis=0)
            s2 = x + pltpu.roll(x, 1, 0)
            s4 = s2 + pltpu.roll(s2, 2, 0)
            s8 = s4 + pltpu.roll(s4, 4, 0)
            s16 = s8 + pltpu.roll(s8, 8, 0)
            grp = _lane_group((ch, LANES), j)
            sel = jnp.where(grp == 0, s2[HALO:], jnp.where(grp == 1, s4[HALO:], jnp.where(grp == 2, s8[HALO:], s16[HALO:])))
            t1 = (lax.broadcasted_iota(jnp.int32, (ch, LANES), 0) + off + 1).astype(F32)
            win = jnp.where(grp == 0, 2.0, jnp.where(grp == 1, 4.0, jnp.where(grp == 2, 8.0, 16.0)))
            m_ref[pl.ds(off, ch), :] = sel / jnp.minimum(t1, win) - cur
            return 0

        lax.fori_loop(0, s // ch, chunk, 0)

    return pl.pallas_call(
        body, name=name, grid=(POOL // LANES,),
        in_specs=[pl.BlockSpec((s, LANES), lambda j: (0, col0 + j))],
        out_specs=pl.BlockSpec((s, LANES), lambda j: (0, j)),
        out_shape=jax.ShapeDtypeStruct((s, POOL), F32),
        compiler_params=_cp(("parallel",), VMEM_LIMIT),
    )(z)


def _pool_win_bwd(dm, name):
    s = dm.shape[0]
    ch = _tile(s, 512)
    n = s // ch

    def body(dm_ref, dp_ref):
        j = pl.program_id(0)

        def chunk(r, _):
            off = pl.multiple_of(r * ch, ch)
            grp = _lane_group((ch + HALO, LANES), j)
            win = jnp.where(grp == 0, 2.0, jnp.where(grp == 1, 4.0, jnp.where(grp == 2, 8.0, 16.0)))
            cur = dm_ref[pl.ds(off, ch), :]
            hoff = pl.multiple_of(jnp.minimum(off + ch, s - HALO), 8)
            halo = jnp.where(r < n - 1, dm_ref[pl.ds(hoff, HALO), :], 0.0)
            x = jnp.concatenate([cur, halo], axis=0)
            t1 = (lax.broadcasted_iota(jnp.int32, (ch + HALO, LANES), 0) + off + 1).astype(F32)
            e = x / jnp.minimum(t1, win)
            tot = ch + HALO
            r2 = e + pltpu.roll(e, tot - 1, 0)
            r4 = r2 + pltpu.roll(r2, tot - 2, 0)
            r8 = r4 + pltpu.roll(r4, tot - 4, 0)
            r16 = r8 + pltpu.roll(r8, tot - 8, 0)
            g = grp[:ch]
            sel = jnp.where(g == 0, r2[:ch], jnp.where(g == 1, r4[:ch], jnp.where(g == 2, r8[:ch], r16[:ch])))
            dp_ref[pl.ds(off, ch), :] = (sel - cur).astype(BF16)
            return 0

        lax.fori_loop(0, n, chunk, 0)

    return pl.pallas_call(
        body, name=name, grid=(POOL // LANES,),
        in_specs=[pl.BlockSpec((s, LANES), lambda j: (0, j))],
        out_specs=pl.BlockSpec((s, LANES), lambda j: (0, j)),
        out_shape=jax.ShapeDtypeStruct((s, POOL), BF16),
        compiler_params=_cp(("parallel",), VMEM_LIMIT),
    )(dm)


def _head_mask(h):
    lane = lax.broadcasted_iota(jnp.int32, (CHUNK, SGU), 1)
    return (lane // (SGU // HEADS)) == h


def _tril(upper=False):
    row = lax.broadcasted_iota(jnp.int32, (CHUNK, CHUNK), 0)
    col = lax.broadcasted_iota(jnp.int32, (CHUNK, CHUNK), 1)
    return col >= row if upper else col <= row


def _sgu_gate(vn, wsp, bsp):
    out = []
    for cidx in range(vn.shape[0] // CHUNK):
        vc = vn[cidx * CHUNK:(cidx + 1) * CHUNK]
        zc = bsp
        for h in range(HEADS):
            zc = zc + jnp.where(_head_mask(h), _dot(wsp[h], vc), 0.0)
        out.append(zc)
    return jnp.concatenate(out, axis=0)


def _mix_out_fwd(o, z, m, x, wsp, bsp, wbd, psc, gsv, gout, wout, name):
    s = x.shape[0]
    tm = _tile(s, 512)

    def body(o_ref, uv_ref, m_ref, x_ref, wsp_ref, bsp_ref, wbd_ref, psc_ref, gsv_ref, gout_ref, wout_ref,
             x1_ref, mix_ref):
        g = gout_ref[...]
        an = _rms(o_ref[...], HEADS * VH)[0] * g[:, :512]
        uv = uv_ref[...]
        u, v = uv[:, :SGU], uv[:, SGU:]
        vn = (_rms(v, SGU)[0] * gsv_ref[...]).astype(BF16)
        tri = _tril()
        wsp_m = [jnp.where(tri, wsp_ref[h], 0.0).astype(BF16) for h in range(HEADS)]
        gm = u * _sgu_gate(vn, wsp_m, bsp_ref[...])
        gn = _rms(gm, SGU)[0] * g[:, 512:768]
        po = _dot(m_ref[...].astype(BF16), wbd_ref[...]) * psc_ref[...]
        pn = _rms(po, POOL)[0] * g[:, 768:]
        mix = jnp.concatenate([an, gn, pn], axis=1).astype(BF16)
        mix_ref[...] = mix
        x1_ref[...] = x_ref[...] + _dot(mix, wout_ref[...])

    row = lambda w, j: pl.BlockSpec((tm, w), lambda i: (i, j))
    return pl.pallas_call(
        body, name=name, grid=(s // tm,),
        in_specs=[row(512, 0), row(512, 1), row(POOL, 0), row(D, 0),
                  _acc((HEADS, CHUNK, CHUNK)), _acc((CHUNK, SGU)), _acc((POOL, POOL)), _acc((1, POOL)),
                  _acc((1, SGU)), _acc((1, D)), _res((D, D))],
        out_specs=[row(D, 0), row(D, 0)],
        out_shape=[jax.ShapeDtypeStruct((s, D), F32), jax.ShapeDtypeStruct((s, D), BF16)],
        compiler_params=_cp(("parallel",), VMEM_LIMIT),
    )(o, z, m, x, wsp, bsp, wbd, psc, gsv, gout, wout)


def _ffn_fwd(x1, g, wg, wu, wd, name):
    s = x1.shape[0]
    tm = _tile(s, 256)

    def body(x_ref, g_ref, wg_ref, wu_ref, wd_ref, x2_ref, a_ref, b_ref, h_ref):
        x = x_ref[...]
        h = (_rms(x, D)[0] * g_ref[...]).astype(BF16)
        h_ref[...] = h
        acc = jnp.zeros((tm, D), F32)
        for c in range(HID // HC):
            sl = slice(c * HC, (c + 1) * HC)
            a = _dot(h, wg_ref[:, sl])
            b = _dot(h, wu_ref[:, sl])
            a_ref[:, sl] = a
            b_ref[:, sl] = b
            acc = acc + _dot((a * jax.nn.sigmoid(a) * b).astype(BF16), wd_ref[sl, :])
        x2_ref[...] = x + acc

    row = lambda w: pl.BlockSpec((tm, w), lambda i: (i, 0))
    return pl.pallas_call(
        body, name=name, grid=(s // tm,),
        in_specs=[row(D), _acc((1, D)), _res((D, HID)), _res((D, HID)), _res((HID, D))],
        out_specs=[row(D), row(HID), row(HID), row(D)],
        out_shape=[jax.ShapeDtypeStruct((s, D), F32), jax.ShapeDtypeStruct((s, HID), F32),
                   jax.ShapeDtypeStruct((s, HID), F32), jax.ShapeDtypeStruct((s, D), BF16)],
        compiler_params=_cp(("parallel",), VMEM_LIMIT),
    )(x1, g, wg, wu, wd)


def _loss_grad(y, tgt):
    s = y.shape[0]
    tm = _tile(s, 512)

    def body(y_ref, t_ref, dy_ref, l_ref):
        e = y_ref[...] - t_ref[...]
        dy_ref[...] = e * (1.0 / D)
        sq = jnp.sum(e * e, axis=0, keepdims=True)
        part = sq[:, :LANES]
        for c in range(1, D // LANES):
            part = part + sq[:, c * LANES:(c + 1) * LANES]
        _accumulate(l_ref, part, pl.program_id(0) == 0)

    row = pl.BlockSpec((tm, D), lambda i: (i, 0))
    return pl.pallas_call(
        body, name="loss_grad", grid=(s // tm,),
        in_specs=[row, row], out_specs=[row, _acc((1, LANES))],
        out_shape=[jax.ShapeDtypeStruct((s, D), F32), jax.ShapeDtypeStruct((1, LANES), F32)],
        compiler_params=_cp(("arbitrary",)),
    )(y, tgt)


def _wgrad(a, b, name):
    s, k = a.shape
    n = b.shape[1]
    half = lambda v: v if v <= 1408 else v // 2
    kb, nb, tt = half(k), half(n), _tile(s, 1024)

    def body(a_ref, b_ref, o_ref):
        _accumulate(o_ref, _dot_tn(a_ref[...].astype(BF16), b_ref[...].astype(BF16)), pl.program_id(2) == 0)

    return pl.pallas_call(
        body, name=name, grid=(k // kb, n // nb, s // tt),
        in_specs=[pl.BlockSpec((tt, kb), lambda i, j, t: (t, i)), pl.BlockSpec((tt, nb), lambda i, j, t: (t, j))],
        out_specs=pl.BlockSpec((kb, nb), lambda i, j, t: (i, j)),
        out_shape=jax.ShapeDtypeStruct((k, n), F32),
        compiler_params=_cp(("parallel", "parallel", "arbitrary"), VMEM_LIMIT),
    )(a, b)


def _ffn_bwd(dx2, x1, a, b, g, wdt, wgt, wut, name):
    s = x1.shape[0]
    tm = _tile(s, 256)

    def body(dx2_ref, x_ref, a_ref, b_ref, g_ref, wdt_ref, wgt_ref, wut_ref,
             dx1_ref, hid_ref, da_ref, db_ref, dg_ref):
        dx2 = dx2_ref[...]
        dyb = dx2.astype(BF16)
        dh = jnp.zeros((tm, D), F32)
        for c in range(HID // HC):
            sl = slice(c * HC, (c + 1) * HC)
            av, bv = a_ref[:, sl], b_ref[:, sl]
            dhid = _dot(dyb, wdt_ref[:, sl])
            sig = jax.nn.sigmoid(av)
            sa = av * sig
            hid_ref[:, sl] = (sa * bv).astype(BF16)
            dbv = (dhid * sa).astype(BF16)
            dav = (dhid * bv * (sig * (1.0 + av * (1.0 - sig)))).astype(BF16)
            db_ref[:, sl] = dbv
            da_ref[:, sl] = dav
            dh = dh + _dot(dav, wgt_ref[sl, :]) + _dot(dbv, wut_ref[sl, :])
        xn, r = _rms(x_ref[...], D)
        dxr, dg = _rms_bwd(xn, r, g_ref[...], dh, D)
        dx1_ref[...] = dx2 + dxr
        _accumulate(dg_ref, dg, pl.program_id(0) == 0)

    row = lambda w: pl.BlockSpec((tm, w), lambda i: (i, 0))
    hid = jax.ShapeDtypeStruct((s, HID), BF16)
    return pl.pallas_call(
        body, name=name, grid=(s // tm,),
        in_specs=[row(D), row(D), row(HID), row(HID), _acc((1, D)), _res((D, HID)), _res((HID, D)), _res((HID, D))],
        out_specs=[row(D), row(HID), row(HID), row(HID), _acc((1, D))],
        out_shape=[jax.ShapeDtypeStruct((s, D), F32), hid, hid, hid, jax.ShapeDtypeStruct((1, D), F32)],
        compiler_params=_cp(("arbitrary",), VMEM_LIMIT),
    )(dx2, x1, a, b, g, wdt, wgt, wut)


def _mix_out_bwd(dx1, o, z, m, wsp, wspt, bsp, wbd, wbdt, psc, gsv, gout, woutt, name):
    s = dx1.shape[0]
    tm = _tile(s, 512)

    def body(dx1_ref, o_ref, uv_ref, m_ref, wsp_ref, wspt_ref, bsp_ref, wbd_ref, wbdt_ref, psc_ref, gsv_ref,
             gout_ref, woutt_ref,
             do_ref, dl_ref, duv_ref, dm_ref, dgo_ref, dgsv_ref, dpsc_ref, dwsp_ref, dbsp_ref, dwbd_ref):
        first = pl.program_id(0) == 0
        g = gout_ref[...]
        dmix = _dot(dx1_ref[...].astype(BF16), woutt_ref[...])
        o = o_ref[...]
        on, ro = _rms(o, HEADS * VH)
        do, dga = _rms_bwd(on, ro, g[:, :512], dmix[:, :512], HEADS * VH)
        for h in range(HEADS):
            sl = slice(h * VH, (h + 1) * VH)
            do_ref[h] = do[:, sl].astype(BF16)
            dl_ref[h] = jnp.broadcast_to(jnp.sum(do[:, sl] * o[:, sl], axis=-1, keepdims=True), (tm, LANES))
        uv = uv_ref[...]
        u, v = uv[:, :SGU], uv[:, SGU:]
        vx, rv = _rms(v, SGU)
        vn = (vx * gsv_ref[...]).astype(BF16)
        tri = _tril()
        wsp_m = [jnp.where(tri, wsp_ref[h], 0.0).astype(BF16) for h in range(HEADS)]
        zc = _sgu_gate(vn, wsp_m, bsp_ref[...])
        gm = u * zc
        gmn, rg = _rms(gm, SGU)
        dgm, dgg = _rms_bwd(gmn, rg, g[:, 512:768], dmix[:, 512:768], SGU)
        du = dgm * zc
        dzc = dgm * u
        wt_m = [jnp.where(_tril(upper=True), wspt_ref[h], 0.0).astype(BF16) for h in range(HEADS)]
        dvn_parts = []
        dbsp = jnp.zeros((CHUNK, SGU), F32)
        dwsp = [jnp.zeros((CHUNK, CHUNK), F32) for _ in range(HEADS)]
        for cidx in range(tm // CHUNK):
            rs = slice(cidx * CHUNK, (cidx + 1) * CHUNK)
            dzc_c = dzc[rs]
            dbsp = dbsp + dzc_c
            dzb = dzc_c.astype(BF16)
            vc = vn[rs]
            dvn_c = jnp.zeros((CHUNK, SGU), F32)
            for h in range(HEADS):
                hm = _head_mask(h)
                dvn_c = dvn_c + jnp.where(hm, _dot(wt_m[h], dzb), 0.0)
                dwsp[h] = dwsp[h] + _dot_nt(jnp.where(hm, dzc_c, 0.0).astype(BF16), vc)
            dvn_parts.append(dvn_c)
        dvn = jnp.concatenate(dvn_parts, axis=0)
        dv, dgsv = _rms_bwd(vx, rv, gsv_ref[...], dvn, SGU)
        duv_ref[...] = jnp.concatenate([du, dv], axis=1).astype(BF16)
        mb = m_ref[...].astype(BF16)
        pw = _dot(mb, wbd_ref[...])
        po = pw * psc_ref[...]
        pon, rp = _rms(po, POOL)
        dpo, dgp = _rms_bwd(pon, rp, g[:, 768:], dmix[:, 768:], POOL)
        dpw = (dpo * psc_ref[...]).astype(BF16)
        dm_ref[...] = _dot(dpw, wbdt_ref[...])
        _accumulate(dgo_ref, jnp.concatenate([dga, dgg, dgp], axis=1), first)
        _accumulate(dgsv_ref, dgsv, first)
        _accumulate(dpsc_ref, jnp.sum(dpo * pw, axis=0, keepdims=True), first)
        _accumulate(dbsp_ref, dbsp, first)
        _accumulate(dwbd_ref, _dot_tn(mb, dpw), first)
        for h in range(HEADS):
            val = jnp.where(tri, dwsp[h], 0.0)

            @pl.when(first)
            def _(val=val, h=h):
                dwsp_ref[h] = val

            @pl.when(jnp.logical_not(first))
            def _(val=val, h=h):
                dwsp_ref[h] += val

    row = lambda w, j: pl.BlockSpec((tm, w), lambda i: (i, j))
    hspec = pl.BlockSpec((HEADS, tm, HP), lambda i: (0, i, 0))
    return pl.pallas_call(
        body, name=name, grid=(s // tm,),
        in_specs=[row(D, 0), row(512, 0), row(512, 1), row(POOL, 0),
                  _acc((HEADS, CHUNK, CHUNK)), _acc((HEADS, CHUNK, CHUNK)), _acc((CHUNK, SGU)),
                  _acc((POOL, POOL)), _acc((POOL, POOL)), _acc((1, POOL)), _acc((1, SGU)), _acc((1, D)), _res((D, D))],
        out_specs=[hspec, hspec, row(512, 0), row(POOL, 0), _acc((1, D)), _acc((1, SGU)), _acc((1, POOL)),
                   _acc((HEADS, CHUNK, CHUNK)), _acc((CHUNK, SGU)), _acc((POOL, POOL))],
        out_shape=[jax.ShapeDtypeStruct((HEADS, s, HP), BF16), jax.ShapeDtypeStruct((HEADS, s, LANES), F32),
                   jax.ShapeDtypeStruct((s, 512), BF16), jax.ShapeDtypeStruct((s, POOL), F32),
                   jax.ShapeDtypeStruct((1, D), F32), jax.ShapeDtypeStruct((1, SGU), F32),
                   jax.ShapeDtypeStruct((1, POOL), F32), jax.ShapeDtypeStruct((HEADS, CHUNK, CHUNK), F32),
                   jax.ShapeDtypeStruct((CHUNK, SGU), F32), jax.ShapeDtypeStruct((POOL, POOL), F32)],
        compiler_params=_cp(("arbitrary",), VMEM_LIMIT),
    )(dx1, o, z, m, wsp, wspt, bsp, wbd, wbdt, psc, gsv, gout, woutt)


def _attn_bwd(q, k, v, do, lse, delta, name):
    s = q.shape[1]
    tq = tk = _tile(s, 512)
    nq = s // tq
    wide = ATT_WIDE * tq if s % (ATT_WIDE * tq) == 0 else tq

    def body(q_ref, k_ref, v_ref, do_ref, lse_ref, dl_ref, dq_ref, dk_ref, dv_ref):
        j = pl.program_id(1)

        @pl.when(j == 0)
        def _():
            dq_ref[...] = jnp.zeros_like(dq_ref)

        kj, vj = k_ref[0], v_ref[0]
        rh = tq // ATT_SPLIT

        def blk(start, rows, dk, dv, masked):
            offs = [pl.multiple_of(start + g * rh, rh) for g in range(rows // rh)]
            qs = [q_ref[0, pl.ds(off, rh), :] for off in offs]
            dos = [do_ref[0, pl.ds(off, rh), :] for off in offs]
            scs = [_dot_nt(qi, kj) for qi in qs]
            dps = [_dot_nt(doi, vj) for doi in dos]
            for g, off in enumerate(offs):
                lse_i = lse_ref[0, pl.ds(off, rh), :][:, :1]
                dl_i = dl_ref[0, pl.ds(off, rh), :][:, :1]
                sc = _causal_mask(scs[g], g * rh) if masked else scs[g]
                p = jnp.exp2(sc * EXP2_C - lse_i)
                ds = (p * (dps[g] - dl_i)).astype(BF16)
                dv = dv + _dot_tn(p.astype(BF16), dos[g])
                dk = dk + _dot_tn(ds, qs[g])
                dq_ref[0, pl.ds(off, rh), :] += _dot(ds, kj) * SCALE
            return dk, dv

        per = wide // tq
        zero = jnp.zeros((tk, HP), F32)
        dk, dv = blk(j * tq, tq, zero, zero, True)
        first_wide = (j + per) // per
        dk, dv = lax.fori_loop(j + 1, jnp.minimum(first_wide * per, nq), lambda i, c: blk(i * tq, tq, *c, False), (dk, dv))
        dk, dv = lax.fori_loop(first_wide, nq // per, lambda i, c: blk(i * wide, wide, *c, False), (dk, dv))
        dk_ref[0] = dk * SCALE
        dv_ref[0] = dv

    full = lambda: pl.BlockSpec((1, s, HP), lambda h, j: (h, 0, 0))
    blk_spec = lambda: pl.BlockSpec((1, tk, HP), lambda h, j: (h, j, 0))
    out = jax.ShapeDtypeStruct((HEADS, s, HP), F32)
    return pl.pallas_call(
        body, name=name, grid=(HEADS, s // tk),
        in_specs=[full(), blk_spec(), blk_spec(), full(), full(), full()],
        out_specs=[full(), blk_spec(), blk_spec()], out_shape=[out] * 3,
        compiler_params=_cp(("parallel", "arbitrary"), VMEM_LIMIT),
    )(q, k, v, do, lse, delta)


def _mla_prep_bwd(dq, dk, dv, z, tabs, gql, gkv, gq, gk, wq, wk, wqt, wkt, wvt, name):
    s = z.shape[0]
    tm = _tile(s, 512)

    def body(dq_ref, dk_ref, dv_ref, ql_ref, kv_ref, kr_ref, c_ref, sa_ref, sb_ref, gql_ref, gkv_ref, gq_ref, gk_ref,
             wq_ref, wk_ref, wqt_ref, wkt_ref, wvt_ref,
             dz_ref, qn_ref, kvn_ref, dqr_ref, dkr_ref, dvr_ref, dgql_ref, dgkv_ref, dgq_ref, dgk_ref):
        first = pl.program_id(0) == 0
        qx, rq = _rms(ql_ref[...], QL)
        qn = (qx * gql_ref[...]).astype(BF16)
        kx, rk = _rms(kv_ref[...], KVL)
        kvn = (kx * gkv_ref[...]).astype(BF16)
        qn_ref[...] = qn
        kvn_ref[...] = kvn
        qraw = _dot(qn, wq_ref[...])
        kraw = _dot(kvn, wk_ref[...])
        kr = kr_ref[...]
        c, sa, sb = c_ref[...], sa_ref[...], sb_ref[...]
        lane = lax.broadcasted_iota(jnp.int32, (tm, HP), 1)
        rope_lanes = (lane >= NOPE) & (lane < QK)
        dkrope = jnp.zeros((tm, HP), F32)
        dgq = jnp.zeros((1, HP), F32)
        dgk = jnp.zeros((1, HP), F32)
        for h in range(HEADS):
            sl = slice(h * HP, (h + 1) * HP)
            xn, r = _rms(qraw[:, sl], QK)
            dx, dg = _rms_bwd(xn, r, gq_ref[...], _rope_t(dq_ref[h], c, sa, sb), QK)
            dqr_ref[:, sl] = dx.astype(BF16)
            dgq = dgq + dg
            xn, r = _rms(kraw[:, sl] + kr, QK)
            dx, dg = _rms_bwd(xn, r, gk_ref[...], _rope_t(dk_ref[h], c, sa, sb), QK)
            dkr_ref[:, sl] = dx.astype(BF16)
            dgk = dgk + dg
            dkrope = dkrope + jnp.where(rope_lanes, dx, 0.0)
            dvr_ref[:, sl] = dv_ref[h].astype(BF16)
        dqn = _dot(dqr_ref[...], wqt_ref[...])
        dql, dgql = _rms_bwd(qx, rq, gql_ref[...], dqn, QL)
        dkvn = _dot(dkr_ref[...], wkt_ref[...]) + _dot(dvr_ref[...], wvt_ref[...])
        dkv, dgkv = _rms_bwd(kx, rk, gkv_ref[...], dkvn, KVL)
        dz_ref[...] = jnp.concatenate([dql, dkv, dkrope], axis=1).astype(BF16)
        _accumulate(dgql_ref, dgql, first)
        _accumulate(dgkv_ref, dgkv, first)
        _accumulate(dgq_ref, dgq, first)
        _accumulate(dgk_ref, dgk, first)

    row = lambda w, j: pl.BlockSpec((tm, w), lambda i: (i, j))
    hspec = pl.BlockSpec((HEADS, tm, HP), lambda i: (0, i, 0))
    sd = lambda w, dt: jax.ShapeDtypeStruct((s, w), dt)
    return pl.pallas_call(
        body, name=name, grid=(s // tm,),
        in_specs=[hspec, hspec, hspec, row(QL, 0), row(KVL, 2), row(HP, 3), row(HP, 0), row(HP, 0), row(HP, 0),
                  _acc((1, QL)), _acc((1, KVL)), _acc((1, HP)), _acc((1, HP)),
                  _acc((QL, HEADS * HP)), _acc((KVL, HEADS * HP)),
                  _acc((HEADS * HP, QL)), _acc((HEADS * HP, KVL)), _acc((HEADS * HP, KVL))],
        out_specs=[row(512, 0), row(QL, 0), row(KVL, 0), row(512, 0), row(512, 0), row(512, 0),
                   _acc((1, QL)), _acc((1, KVL)), _acc((1, HP)), _acc((1, HP))],
        out_shape=[sd(512, BF16), sd(QL, BF16), sd(KVL, BF16), sd(512, BF16), sd(512, BF16), sd(512, BF16),
                   jax.ShapeDtypeStruct((1, QL), F32), jax.ShapeDtypeStruct((1, KVL), F32),
                   jax.ShapeDtypeStruct((1, HP), F32), jax.ShapeDtypeStruct((1, HP), F32)],
        compiler_params=_cp(("arbitrary",), VMEM_LIMIT),
    )(dq, dk, dv, z, z, z, *tabs, gql, gkv, gq, gk, wq, wk, wqt, wkt, wvt)


def _in_proj_bwd(dzm, duv, dp, x, dx1, g, wint, name):
    s = x.shape[0]
    tm = _tile(s, 512)

    def body(dzm_ref, duv_ref, dp_ref, x_ref, dx1_ref, g_ref, w_ref, dx_ref, dg_ref):
        dh = _dot(dzm_ref[...], w_ref[0:512, :]) + _dot(duv_ref[...], w_ref[512:1024, :]) \
            + _dot(dp_ref[...], w_ref[1024:IN_P, :])
        xn, r = _rms(x_ref[...], D)
        dxr, dg = _rms_bwd(xn, r, g_ref[...], dh, D)
        dx_ref[...] = dx1_ref[...] + dxr
        _accumulate(dg_ref, dg, pl.program_id(0) == 0)

    row = lambda w: pl.BlockSpec((tm, w), lambda i: (i, 0))
    return pl.pallas_call(
        body, name=name, grid=(s // tm,),
        in_specs=[row(512), row(512), row(POOL), row(D), row(D), _acc((1, D)), _res((IN_P, D))],
        out_specs=[row(D), _acc((1, D))],
        out_shape=[jax.ShapeDtypeStruct((s, D), F32), jax.ShapeDtypeStruct((1, D), F32)],
        compiler_params=_cp(("arbitrary",), VMEM_LIMIT),
    )(dzm, duv, dp, x, dx1, g, wint)


def _adamw(w, g, m, v, name):
    r, c = w.shape
    tr = _row_tile(r, 512)
    c1 = 1.0 - B1 ** STEP
    c2 = 1.0 - B2 ** STEP

    def body(w_ref, g_ref, m_ref, v_ref, d_ref, nm_ref, nv_ref):
        gv = g_ref[...]
        nm = B1 * m_ref[...] + (1.0 - B1) * gv
        nv = B2 * v_ref[...] + (1.0 - B2) * (gv * gv)
        nm_ref[...] = nm
        nv_ref[...] = nv
        d_ref[...] = -LR * ((nm / c1) / (jnp.sqrt(nv / c2) + ADAM_EPS) + WD * w_ref[...])

    spec = pl.BlockSpec((tr, c), lambda i: (i, 0))
    out = jax.ShapeDtypeStruct((r, c), F32)
    return pl.pallas_call(
        body, name=name, grid=(r // tr,), in_specs=[spec] * 4, out_specs=[spec] * 3, out_shape=[out] * 3,
        compiler_params=_cp(("parallel",)),
    )(w, g, m, v)


ANY = pl.BlockSpec(memory_space=pl.ANY)


def _place():
    x, y, c = lax.axis_index("x"), lax.axis_index("y"), lax.axis_index("c")
    chips = [(1 - x, y), (x, 1 - y), (1 - x, 1 - y)]
    return x, y, c, chips


def _all_gather_chips(shard, name):
    rows, cols = shard.shape
    half = rows // 2
    align = 16 if shard.dtype == BF16 else 8
    assert half % align == 0

    def body(x_ref, out_ref, send_sems, recv_sems, local_sem):
        x, y, c, chips = _place()
        me = 2 * x + y
        sibling = (x, y, 1 - c)

        def part(ref, kk, hh):
            return ref.at[kk, pl.ds(pl.multiple_of(hh * half, align), half), :]

        def copy(sem, src, dst, to):
            return pltpu.make_async_remote_copy(src_ref=src, dst_ref=dst, send_sem=send_sems.at[sem],
                                                recv_sem=recv_sems.at[sem], device_id=to, device_id_type=MESH)

        mine = pltpu.make_async_copy(x_ref, out_ref.at[me], local_sem)
        mine.start()
        my_half = x_ref.at[pl.ds(pl.multiple_of(c * half, align), half), :]
        first = [copy(j, my_half, part(out_ref, me, c), (cx, cy, c)) for j, (cx, cy) in enumerate(chips)]
        for cp in first:
            cp.start()
        passed = []
        for j, (cx, cy) in enumerate(chips):
            landed = part(out_ref, 2 * cx + cy, c)
            copy(j, landed, landed, (cx, cy, c)).wait_recv()
            fwd = copy(3 + j, landed, landed, sibling)
            fwd.start()
            passed.append(fwd)
        for j, (cx, cy) in enumerate(chips):
            other = part(out_ref, 2 * cx + cy, 1 - c)
            copy(3 + j, other, other, sibling).wait_recv()
        for cp in first + passed:
            cp.wait_send()
        mine.wait()

    return pl.pallas_call(
        body, name=name, in_specs=[ANY], out_specs=ANY,
        out_shape=jax.ShapeDtypeStruct((4, rows, cols), shard.dtype),
        scratch_shapes=[pltpu.SemaphoreType.DMA((6,)), pltpu.SemaphoreType.DMA((6,)), pltpu.SemaphoreType.DMA],
        compiler_params=pltpu.CompilerParams(has_side_effects=True),
    )(shard)


def _pair_swap_halves(g2, name):
    _, nk, half, cols = g2.shape

    def body(g_ref, out_ref, send_sem, recv_sem):
        x, y, c, _ = _place()
        cp = pltpu.make_async_remote_copy(src_ref=g_ref.at[1 - c], dst_ref=out_ref, send_sem=send_sem, recv_sem=recv_sem,
                                          device_id=(x, y, 1 - c), device_id_type=MESH)
        cp.start()
        cp.wait()

    return pl.pallas_call(
        body, name=name, in_specs=[ANY], out_specs=ANY,
        out_shape=jax.ShapeDtypeStruct((nk, half, cols), g2.dtype),
        scratch_shapes=[pltpu.SemaphoreType.DMA, pltpu.SemaphoreType.DMA],
        compiler_params=pltpu.CompilerParams(has_side_effects=True),
    )(g2)


def _pair_add(g2, got, cidx, name):
    _, nk, half, cols = g2.shape
    tr = 304 if half % 304 == 0 else half
    assert half % tr == 0 and tr % 8 == 0

    grid_spec = pltpu.PrefetchScalarGridSpec(
        num_scalar_prefetch=1, grid=(nk, half // tr),
        in_specs=[pl.BlockSpec((1, 1, tr, cols), lambda k, r, c_ref: (c_ref[0], k, r, 0)),
                  pl.BlockSpec((1, tr, cols), lambda k, r, c_ref: (k, r, 0))],
        out_specs=pl.BlockSpec((1, tr, cols), lambda k, r, c_ref: (k, r, 0)))

    def body(c_ref, a_ref, b_ref, o_ref):
        o_ref[0] = a_ref[0, 0] + b_ref[0]

    return pl.pallas_call(
        body, name=name, grid_spec=grid_spec, out_shape=jax.ShapeDtypeStruct((nk, half, cols), g2.dtype),
        compiler_params=_cp(("parallel", "parallel")),
    )(cidx, g2, got)


def _chip_scatter(p, name):
    nk, half, cols = p.shape

    def body(p_ref, q_ref, send_sems, recv_sems, local_sem):
        x, y, c, chips = _place()
        me = 2 * x + y
        mine = pltpu.make_async_copy(p_ref.at[me], q_ref.at[me], local_sem)
        mine.start()
        sends = []
        for j, (cx, cy) in enumerate(chips):
            cp = pltpu.make_async_remote_copy(src_ref=p_ref.at[2 * cx + cy], dst_ref=q_ref.at[me],
                                              send_sem=send_sems.at[j], recv_sem=recv_sems.at[j],
                                              device_id=(cx, cy, c), device_id_type=MESH)
            cp.start()
            sends.append(cp)
        for j, (cx, cy) in enumerate(chips):
            slot = q_ref.at[2 * cx + cy]
            pltpu.make_async_remote_copy(src_ref=slot, dst_ref=slot, send_sem=send_sems.at[j], recv_sem=recv_sems.at[j],
                                         device_id=(cx, cy, c), device_id_type=MESH).wait_recv()
        for cp in sends:
            cp.wait_send()
        mine.wait()

    return pl.pallas_call(
        body, name=name, in_specs=[ANY], out_specs=ANY, out_shape=jax.ShapeDtypeStruct(p.shape, p.dtype),
        scratch_shapes=[pltpu.SemaphoreType.DMA((3,)), pltpu.SemaphoreType.DMA((3,)), pltpu.SemaphoreType.DMA],
        compiler_params=pltpu.CompilerParams(has_side_effects=True),
    )(p)


def _sum_chips(q, name):
    nk, half, cols = q.shape
    tr = 304 if half % 304 == 0 else half

    def body(q_ref, o_ref):
        o_ref[...] = ((q_ref[0] + q_ref[1]) + q_ref[2]) + q_ref[3]

    return pl.pallas_call(
        body, name=name, grid=(half // tr,),
        in_specs=[pl.BlockSpec((nk, tr, cols), lambda r: (0, r, 0))],
        out_specs=pl.BlockSpec((tr, cols), lambda r: (r, 0)),
        out_shape=jax.ShapeDtypeStruct((half, cols), q.dtype), compiler_params=_cp(("parallel",)),
    )(q)


def _pair_join(mine, name):
    half, cols = mine.shape

    def body(a_ref, out_ref, send_sem, recv_sem, local_sem):
        x, y, c, _ = _place()
        dst = out_ref.at[pl.ds(pl.multiple_of(c * half, 8), half), :]
        loc = pltpu.make_async_copy(a_ref, dst, local_sem)
        loc.start()
        cp = pltpu.make_async_remote_copy(src_ref=a_ref, dst_ref=dst, send_sem=send_sem, recv_sem=recv_sem,
                                          device_id=(x, y, 1 - c), device_id_type=MESH)
        cp.start()
        cp.wait()
        loc.wait()

    return pl.pallas_call(
        body, name=name, in_specs=[ANY], out_specs=ANY, out_shape=jax.ShapeDtypeStruct((2 * half, cols), mine.dtype),
        scratch_shapes=[pltpu.SemaphoreType.DMA, pltpu.SemaphoreType.DMA, pltpu.SemaphoreType.DMA],
        compiler_params=pltpu.CompilerParams(has_side_effects=True),
    )(mine)


BIG = [("w_in", (D, IN_W), 1), ("w_q_up", (QL, HEADS * QK), 1), ("w_kv_up", (KVL, HEADS * (NOPE + VH)), 1),
       ("w_out", (D, D), 0), ("w_gate", (D, HID), 1), ("w_up", (D, HID), 1), ("w_down", (HID, D), 0)]
SMALL = [("g_mix_norm", (D,)), ("g_q_lat", (QL,)), ("g_kv_lat", (KVL,)), ("g_q_head", (QK,)), ("g_k_head", (QK,)),
         ("g_sgu_v", (SGU,)), ("w_spatial", (HEADS, CHUNK, CHUNK)), ("b_spatial", (HEADS, CHUNK)),
         ("w_pool", (4, 64, 64)), ("pool_scale", (POOL,)), ("g_out_mla", (512,)), ("g_out_sgu", (SGU,)),
         ("g_out_pool", (POOL,)), ("g_ffn_norm", (D,))]
ORDER = ["g_mix_norm", "w_in", "g_q_lat", "w_q_up", "g_kv_lat", "w_kv_up", "g_q_head", "g_k_head", "g_sgu_v",
         "w_spatial", "b_spatial", "w_pool", "pool_scale", "g_out_mla", "g_out_sgu", "g_out_pool", "w_out",
         "g_ffn_norm", "w_gate", "w_up", "w_down"]
DEPTH = 2
COLS = 1024
BIG_N = sum(r * c // 4 for _, (r, c), _ in BIG) * DEPTH
SMALL_N = sum(math.prod(s) for _, s in SMALL) * DEPTH
assert BIG_N % COLS == 0 and SMALL_N % 4 == 0
BIG_ROWS = BIG_N // COLS
W_ROWS = -(-BIG_ROWS // 32) * 32
SMALL_ROWS = -(-(SMALL_N // 4) // (16 * COLS)) * 16
G_ROWS = BIG_ROWS + SMALL_ROWS
assert G_ROWS % 16 == 0


def _shard_shape(shape, axis):
    r, c = shape
    return (r // 4, c) if axis == 0 else (r, c // 4)


def _pack_weight_shards(p):
    flat = [p[n][l].reshape(-1) for l in range(DEPTH) for n, _, _ in BIG]
    flat = jnp.concatenate(flat).astype(BF16)
    return jnp.pad(flat, (0, W_ROWS * COLS - BIG_N)).reshape(W_ROWS, COLS)


def _unpack_weights(gathered):
    flat = gathered.reshape(4, W_ROWS * COLS)
    out, off = [], 0
    for _ in range(DEPTH):
        layer = {}
        for n, shape, axis in BIG:
            r, c = _shard_shape(shape, axis)
            piece = flat[:, off:off + r * c].reshape(4, r, c)
            off += r * c
            layer[n] = piece.reshape(4 * r, c) if axis == 0 else piece.transpose(1, 0, 2).reshape(r, 4 * c)
        out.append(layer)
    return out


def _kernel_weights(w):
    win = w["w_in"]
    zeros = lambda r, c: jnp.zeros((r, c), BF16)
    o1, o2, o3, o4 = QL, QL + KVL, QL + KVL + ROPE, QL + KVL + ROPE + 2 * SGU
    win_p = jnp.concatenate([win[:, :o2], zeros(D, NOPE), win[:, o2:o3], zeros(D, HP - QK), win[:, o3:o4], win[:, o4:]], axis=1)
    wq = w["w_q_up"].reshape(QL, HEADS, QK)
    wq_p = jnp.pad(wq, ((0, 0), (0, 0), (0, HP - QK))).reshape(QL, HEADS * HP)
    wkv = w["w_kv_up"].reshape(KVL, HEADS, NOPE + VH)
    wk_p = jnp.pad(wkv[:, :, :NOPE], ((0, 0), (0, 0), (0, HP - NOPE))).reshape(KVL, HEADS * HP)
    wv_p = wkv[:, :, NOPE:].reshape(KVL, HEADS * VH)
    return dict(win=win_p, wint=win_p.T, wq=wq_p, wqt=wq_p.T, wk=wk_p, wkt=wk_p.T, wv=wv_p, wvt=wv_p.T,
                wout=w["w_out"], woutt=w["w_out"].T, wg=w["w_gate"], wgt=w["w_gate"].T,
                wu=w["w_up"], wut=w["w_up"].T, wd=w["w_down"], wdt=w["w_down"].T)


def _small_operands(p, l):
    row = lambda v: v.reshape(1, -1)
    pad = lambda v: jnp.pad(v, (0, HP - QK)).reshape(1, HP)
    wpool = p["w_pool"][l]
    wbd = jnp.zeros((POOL, POOL), F32)
    for g in range(4):
        wbd = lax.dynamic_update_slice(wbd, wpool[g], (g * 64, g * 64))
    return dict(
        g_mix=row(p["g_mix_norm"][l]), gql=row(p["g_q_lat"][l]), gkv=row(p["g_kv_lat"][l]),
        gq=pad(p["g_q_head"][l]), gk=pad(p["g_k_head"][l]), gsv=row(p["g_sgu_v"][l]),
        wsp=p["w_spatial"][l], wspt=p["w_spatial"][l].transpose(0, 2, 1),
        bsp=jnp.repeat(p["b_spatial"][l].T, SGU // HEADS, axis=1),
        wbd=wbd.astype(BF16), wbdt=wbd.T.astype(BF16), psc=row(p["pool_scale"][l]),
        gout=jnp.concatenate([p["g_out_mla"][l], p["g_out_sgu"][l], p["g_out_pool"][l]]).reshape(1, D),
        g_ffn=row(p["g_ffn_norm"][l]))


def _big_grads(g):
    dwin = g["win"]
    o2 = QL + KVL
    gin = jnp.concatenate([dwin[:, :o2], dwin[:, o2 + NOPE:o2 + NOPE + ROPE], dwin[:, 512:]], axis=1)
    gq = g["wq"].reshape(QL, HEADS, HP)[:, :, :QK].reshape(QL, HEADS * QK)
    gk = g["wk"].reshape(KVL, HEADS, HP)[:, :, :NOPE]
    gv = g["wv"].reshape(KVL, HEADS, VH)
    gkv = jnp.concatenate([gk, gv], axis=2).reshape(KVL, HEADS * (NOPE + VH))
    return {"w_in": gin, "w_q_up": gq, "w_kv_up": gkv, "w_out": g["wout"], "w_gate": g["wg"], "w_up": g["wu"],
            "w_down": g["wd"]}


def _small_grads(g):
    go = g["gout"].reshape(-1)
    return {"g_mix_norm": g["g_mix"].reshape(-1), "g_q_lat": g["gql"].reshape(-1), "g_kv_lat": g["gkv"].reshape(-1),
            "g_q_head": g["gq"].reshape(-1)[:QK], "g_k_head": g["gk"].reshape(-1)[:QK], "g_sgu_v": g["gsv"].reshape(-1),
            "w_spatial": g["wsp"], "b_spatial": g["bsp"].reshape(CHUNK, HEADS, SGU // HEADS).sum(-1).T,
            "w_pool": jnp.stack([g["wbd"][i * 64:(i + 1) * 64, i * 64:(i + 1) * 64] for i in range(4)]),
            "pool_scale": g["psc"].reshape(-1), "g_out_mla": go[:512], "g_out_sgu": go[512:768],
            "g_out_pool": go[768:], "g_ffn_norm": g["g_ffn"].reshape(-1)}


def _pack_grads(big, small):
    parts = []
    for l in range(DEPTH):
        for n, shape, axis in BIG:
            r, c = _shard_shape(shape, axis)
            full = big[l][n]
            sh = full.reshape(4, r, c) if axis == 0 else full.reshape(r, 4, c).transpose(1, 0, 2)
            parts.append(sh.reshape(4, r * c))
    sm = jnp.concatenate([small[l][n].reshape(-1) for l in range(DEPTH) for n, _ in SMALL]).reshape(4, SMALL_N // 4)
    parts.append(jnp.pad(sm, ((0, 0), (0, SMALL_ROWS * COLS - SMALL_N // 4))))
    g = jnp.concatenate(parts, axis=1).reshape(4, 2, G_ROWS // 2, COLS)
    return g.transpose(1, 0, 2, 3)


def _unpack_big_grads(gsum):
    flat = gsum[:BIG_ROWS].reshape(-1)
    out, off = {n: [] for n, _, _ in BIG}, 0
    for _ in range(DEPTH):
        for n, shape, axis in BIG:
            r, c = _shard_shape(shape, axis)
            out[n].append(flat[off:off + r * c].reshape(r, c))
            off += r * c
    return {n: jnp.stack(v) for n, v in out.items()}


def _unpack_small_grads(gathered):
    flat = gathered.reshape(4, SMALL_ROWS * COLS)[:, :SMALL_N // 4].reshape(-1)
    out, off = {n: [] for n, _ in SMALL}, 0
    for _ in range(DEPTH):
        for n, shape in SMALL:
            k = math.prod(shape)
            out[n].append(flat[off:off + k].reshape(shape))
            off += k
    return {n: jnp.stack(v) for n, v in out.items()}


def _layer_fwd(x, tabs, kw, sp, l):
    t = f"_l{l}"
    z, hb = _in_proj_fwd(x, sp["g_mix"], kw["win"], "in_proj_fwd" + t)
    q, k, v = _mla_prep_fwd(z, tabs, sp["gql"], sp["gkv"], sp["gq"], sp["gk"], kw["wq"], kw["wk"], kw["wv"],
                            "mla_prep_fwd" + t)
    o, lse = _attn_fwd(q, k, v, "attn_fwd" + t)
    m = _pool_win_fwd(z, "pool_win_fwd" + t)
    x1, mix = _mix_out_fwd(o, z, m, x, sp["wsp"], sp["bsp"], sp["wbd"], sp["psc"], sp["gsv"], sp["gout"], kw["wout"],
                           "mix_out_fwd" + t)
    x2, a, b, h2 = _ffn_fwd(x1, sp["g_ffn"], kw["wg"], kw["wu"], kw["wd"], "ffn_fwd" + t)
    saved = dict(x=x, z=z, hb=hb, q=q, k=k, v=v, o=o, lse=lse, m=m, x1=x1, mix=mix, a=a, b=b, h2=h2)
    return x2, saved


def _layer_bwd(dx2, sv, tabs, kw, sp, l):
    t = f"_l{l}"
    g = {}
    dx1, hid, da, db, g["g_ffn"] = _ffn_bwd(dx2, sv["x1"], sv["a"], sv["b"], sp["g_ffn"], kw["wdt"], kw["wgt"], kw["wut"],
                                            "ffn_bwd" + t)
    g["wd"] = _wgrad(hid, dx2, "wgrad_down" + t)
    g["wg"] = _wgrad(sv["h2"], da, "wgrad_gate" + t)
    g["wu"] = _wgrad(sv["h2"], db, "wgrad_up" + t)
    do, delta, duv, dm, g["gout"], g["gsv"], g["psc"], g["wsp"], g["bsp"], g["wbd"] = _mix_out_bwd(
        dx1, sv["o"], sv["z"], sv["m"], sp["wsp"], sp["wspt"], sp["bsp"], sp["wbd"], sp["wbdt"], sp["psc"], sp["gsv"],
        sp["gout"], kw["woutt"], "mix_out_bwd" + t)
    g["wout"] = _wgrad(sv["mix"], dx1, "wgrad_out" + t)
    dp = _pool_win_bwd(dm, "pool_win_bwd" + t)
    dq, dk, dv = _attn_bwd(sv["q"], sv["k"], sv["v"], do, sv["lse"], delta, "attn_bwd" + t)
    dzm, qn, kvn, dqr, dkr, dvr, g["gql"], g["gkv"], g["gq"], g["gk"] = _mla_prep_bwd(
        dq, dk, dv, sv["z"], tabs, sp["gql"], sp["gkv"], sp["gq"], sp["gk"], kw["wq"], kw["wk"], kw["wqt"], kw["wkt"],
        kw["wvt"], "mla_prep_bwd" + t)
    g["wq"] = _wgrad(qn, dqr, "wgrad_q_up" + t)
    g["wk"] = _wgrad(kvn, dkr, "wgrad_k_up" + t)
    g["wv"] = _wgrad(kvn, dvr, "wgrad_v_up" + t)
    dx, g["g_mix"] = _in_proj_bwd(dzm, duv, dp, sv["x"], dx1, sp["g_mix"], kw["wint"], "in_proj_bwd" + t)
    g["win"] = jnp.concatenate([_wgrad(sv["hb"], dzm, "wgrad_in_a" + t), _wgrad(sv["hb"], duv, "wgrad_in_b" + t),
                                _wgrad(sv["hb"], dp, "wgrad_in_c" + t)], axis=1)
    return dx, g


def _rope_inv_freq():
    half = ROPE // 2
    inv = 1.0 / (ROPE_THETA ** (jnp.arange(half, dtype=F32) / half))
    return jnp.concatenate([jnp.zeros((NOPE,), F32), inv, inv, jnp.zeros((HP - QK,), F32)]).reshape(1, HP)


def kernel(x, positions, g_mix_norm, w_in, g_q_lat, w_q_up, g_kv_lat, w_kv_up, g_q_head, g_k_head, g_sgu_v, w_spatial, b_spatial, w_pool, pool_scale, g_out_mla, g_out_sgu, g_out_pool, w_out, g_ffn_norm, w_gate, w_up, w_down, loss_target, m_g_mix_norm, m_w_in, m_g_q_lat, m_w_q_up, m_g_kv_lat, m_w_kv_up, m_g_q_head, m_g_k_head, m_g_sgu_v, m_w_spatial, m_b_spatial, m_w_pool, m_pool_scale, m_g_out_mla, m_g_out_sgu, m_g_out_pool, m_w_out, m_g_ffn_norm, m_w_gate, m_w_up, m_w_down, v_g_mix_norm, v_w_in, v_g_q_lat, v_w_q_up, v_g_kv_lat, v_w_kv_up, v_g_q_head, v_g_k_head, v_g_sgu_v, v_w_spatial, v_b_spatial, v_w_pool, v_pool_scale, v_g_out_mla, v_g_out_sgu, v_g_out_pool, v_w_out, v_g_ffn_norm, v_w_gate, v_w_up, v_w_down):
    given = dict(locals())
    p = {n: given[n] for n in ORDER}
    seq = x.shape[1]
    xs = x.reshape(seq, D)
    tgt = loss_target.reshape(seq, D)

    full = _unpack_weights(_all_gather_chips(_pack_weight_shards(p), "all_gather_weights"))
    kws = [_kernel_weights(full[l]) for l in range(DEPTH)]
    sps = [_small_operands(p, l) for l in range(DEPTH)]
    tabs = _rope_tables(positions.reshape(seq, 1), _rope_inv_freq())

    saved, h = [], xs
    for l in range(DEPTH):
        h, sv = _layer_fwd(h, tabs, kws[l], sps[l], l)
        saved.append(sv)
    dy, lpart = _loss_grad(h, tgt)
    loss = lax.psum(0.5 / D * jnp.sum(lpart), ("x", "y", "c"))
    grads = [None] * DEPTH
    for l in reversed(range(DEPTH)):
        dy, grads[l] = _layer_bwd(dy, saved[l], tabs, kws[l], sps[l], l)

    g2 = _pack_grads([_big_grads(g) for g in grads], [_small_grads(g) for g in grads])
    cidx = lax.axis_index("c").astype(jnp.int32).reshape(1)
    pair = _pair_add(g2, _pair_swap_halves(g2, "grad_pair_swap"), cidx, "grad_pair_add")
    gsum = _pair_join(_sum_chips(_chip_scatter(pair, "grad_chip_scatter"), "grad_chip_sum"), "grad_pair_join")
    gw = _unpack_big_grads(gsum)
    gw.update(_unpack_small_grads(_all_gather_chips(gsum[BIG_ROWS:], "all_gather_small_grads")))

    delta, new_m, new_v = {}, {}, {}
    for n in ORDER:
        w = p[n]
        two_d = (-1, w.shape[-1])
        d, nm, nv = _adamw(w.reshape(two_d), gw[n].reshape(two_d), given["m_" + n].reshape(two_d),
                           given["v_" + n].reshape(two_d), "adamw_" + n)
        delta[n], new_m[n], new_v[n] = d.reshape(w.shape), nm.reshape(w.shape), nv.reshape(w.shape)
    return (loss, dy.reshape(x.shape), *[gw[n] for n in ORDER], *[delta[n] for n in ORDER],
            *[new_m[n] for n in ORDER], *[new_v[n] for n in ORDER])
```

```python
import functools
import math

import jax
import jax.numpy as jnp
from jax import lax
from jax.experimental import pallas as pl
from jax.experimental.pallas import tpu as pltpu

F32 = jnp.float32
BF16 = jnp.bfloat16
MESH = pl.DeviceIdType.MESH

D = 1024
HEADS = 4
QK = 96
NOPE = 64
ROPE = 32
VH = 128
HP = 128
QL = 256
KVL = 128
SGU = 256
POOL = 256
CHUNK = 128
HID = 2816
CHIPS = 4
SH = HID // CHIPS
IN_W = 1184
IN_P = 1280
EPS = 1e-6
ROPE_THETA = 10000.0
SCALE = 1.0 / math.sqrt(QK)
LOG2E = 1.4426950408889634
EXP2_C = SCALE * LOG2E
ATT_SPLIT = 2
ATT_WIDE = 4
NEG = -1e30
HALO = 16

LR, B1, B2, ADAM_EPS, WD, STEP = 0.001, 0.9, 0.999, 1e-08, 0.01, 10

VMEM_LIMIT = 56 * 1024 * 1024
LANES = 128
HC = 256


def _cp(sem, vmem=None):
    return pltpu.CompilerParams(dimension_semantics=sem, vmem_limit_bytes=vmem)


def _res(shape):
    nd = len(shape)
    return pl.BlockSpec(shape, lambda *_: (0,) * nd, pipeline_mode=pl.Buffered(1))


def _acc(shape):
    nd = len(shape)
    return pl.BlockSpec(shape, lambda *_: (0,) * nd)


def _dot(a, b):
    return jnp.dot(a, b, preferred_element_type=F32)


def _dot_nt(a, b):
    return lax.dot_general(a, b, (((1,), (1,)), ((), ())), preferred_element_type=F32)


def _dot_tn(a, b):
    return lax.dot_general(a, b, (((0,), (0,)), ((), ())), preferred_element_type=F32)


def _rms(x, n):
    r = lax.rsqrt(jnp.sum(x * x, axis=-1, keepdims=True) * (1.0 / n) + EPS)
    return x * r, r


def _rms_bwd(xn, r, g, dy, n):
    dn = dy * g
    dx = r * (dn - xn * (jnp.sum(dn * xn, axis=-1, keepdims=True) * (1.0 / n)))
    return dx, jnp.sum(dy * xn, axis=0, keepdims=True)


def _accumulate(ref, val, first):
    @pl.when(first)
    def _():
        ref[...] = val

    @pl.when(jnp.logical_not(first))
    def _():
        ref[...] += val


def _accumulate0(ref, val, first):
    @pl.when(first)
    def _():
        ref[0] = val

    @pl.when(jnp.logical_not(first))
    def _():
        ref[0] += val


def _tile(s, t):
    return min(s, t)


def _row_tile(r, cap):
    if r <= cap:
        return r
    return max(t for t in range(8, cap + 1, 8) if r % t == 0)


def _rope_tables(pos, invf):
    s = pos.shape[0]
    tm = _tile(s, 1024)

    def body(pos_ref, invf_ref, c_ref, sa_ref, sb_ref):
        ang = pos_ref[...].astype(F32) * invf_ref[...]
        c, sn = jnp.cos(ang), jnp.sin(ang)
        lane = lax.broadcasted_iota(jnp.int32, ang.shape, 1)
        first = (lane >= NOPE) & (lane < NOPE + ROPE // 2)
        second = (lane >= NOPE + ROPE // 2) & (lane < QK)
        c_ref[...] = jnp.where(first | second, c, 1.0)
        sa_ref[...] = jnp.where(first, -sn, 0.0)
        sb_ref[...] = jnp.where(second, sn, 0.0)

    out = jax.ShapeDtypeStruct((s, HP), F32)
    return pl.pallas_call(
        body, name="rope_tables", grid=(s // tm,),
        in_specs=[pl.BlockSpec((tm, 1), lambda i: (i, 0)), _acc((1, HP))],
        out_specs=[pl.BlockSpec((tm, HP), lambda i: (i, 0))] * 3,
        out_shape=[out] * 3, compiler_params=_cp(("parallel",)),
    )(pos, invf)


def _rope(x, c, sa, sb):
    return x * c + pltpu.roll(x, HP - ROPE // 2, 1) * sa + pltpu.roll(x, ROPE // 2, 1) * sb


def _rope_t(d, c, sa, sb):
    return d * c + pltpu.roll(d * sa, ROPE // 2, 1) + pltpu.roll(d * sb, HP - ROPE // 2, 1)


def _in_proj_fwd(x, g, w, name):
    s = x.shape[0]
    tm = _tile(s, 512)

    def body(x_ref, g_ref, w_ref, z_ref, h_ref):
        xn, _ = _rms(x_ref[...], D)
        h = (xn * g_ref[...]).astype(BF16)
        h_ref[...] = h
        z_ref[...] = _dot(h, w_ref[...])

    return pl.pallas_call(
        body, name=name, grid=(s // tm,),
        in_specs=[pl.BlockSpec((tm, D), lambda i: (i, 0)), _acc((1, D)), _res((D, IN_P))],
        out_specs=[pl.BlockSpec((tm, IN_P), lambda i: (i, 0)), pl.BlockSpec((tm, D), lambda i: (i, 0))],
        out_shape=[jax.ShapeDtypeStruct((s, IN_P), F32), jax.ShapeDtypeStruct((s, D), BF16)],
        compiler_params=_cp(("parallel",), VMEM_LIMIT),
    )(x, g, w)


def _mla_prep_fwd(z, tabs, gql, gkv, gq, gk, wq, wk, wv, name):
    s = z.shape[0]
    tm = _tile(s, 512)

    def body(ql_ref, kv_ref, kr_ref, c_ref, sa_ref, sb_ref, gql_ref, gkv_ref, gq_ref, gk_ref,
             wq_ref, wk_ref, wv_ref, q_out, k_out, v_out):
        qn = (_rms(ql_ref[...], QL)[0] * gql_ref[...]).astype(BF16)
        kvn = (_rms(kv_ref[...], KVL)[0] * gkv_ref[...]).astype(BF16)
        qraw = _dot(qn, wq_ref[...])
        kraw = _dot(kvn, wk_ref[...])
        vraw = _dot(kvn, wv_ref[...])
        kr = kr_ref[...]
        c, sa, sb = c_ref[...], sa_ref[...], sb_ref[...]
        for h in range(HEADS):
            sl = slice(h * HP, (h + 1) * HP)
            xq = _rms(qraw[:, sl], QK)[0] * gq_ref[...]
            q_out[h] = _rope(xq, c, sa, sb).astype(BF16)
            xk = _rms(kraw[:, sl] + kr, QK)[0] * gk_ref[...]
            k_out[h] = _rope(xk, c, sa, sb).astype(BF16)
            v_out[h] = vraw[:, sl].astype(BF16)

    row = lambda w, j: pl.BlockSpec((tm, w), lambda i: (i, j))
    hspec = pl.BlockSpec((HEADS, tm, HP), lambda i: (0, i, 0))
    hshape = jax.ShapeDtypeStruct((HEADS, s, HP), BF16)
    return pl.pallas_call(
        body, name=name, grid=(s // tm,),
        in_specs=[row(QL, 0), row(KVL, 2), row(HP, 3), row(HP, 0), row(HP, 0), row(HP, 0),
                  _acc((1, QL)), _acc((1, KVL)), _acc((1, HP)), _acc((1, HP)),
                  _acc((QL, HEADS * HP)), _acc((KVL, HEADS * HP)), _acc((KVL, HEADS * HP))],
        out_specs=[hspec] * 3, out_shape=[hshape] * 3,
        compiler_params=_cp(("parallel",)),
    )(z, z, z, *tabs, gql, gkv, gq, gk, wq, wk, wv)


def _causal_mask(s, row0):
    row = lax.broadcasted_iota(jnp.int32, s.shape, 0) + row0
    col = lax.broadcasted_iota(jnp.int32, s.shape, 1)
    return jnp.where(col <= row, s, NEG)


def _attn_fwd(q, k, v, name):
    s = q.shape[1]
    tq = _tile(s, 512)
    wide = ATT_WIDE * tq if s % (ATT_WIDE * tq) == 0 else tq
    rh = tq // ATT_SPLIT

    def body(q_ref, k_ref, v_ref, o_ref, lse_ref):
        i = pl.program_id(1)

        def blk(off, tk, carry, masked):
            off = pl.multiple_of(off, tq)
            kj = k_ref[0, pl.ds(off, tk), :]
            vj = v_ref[0, pl.ds(off, tk), :]
            out = []
            scs = [_dot_nt(q_ref[0, g * rh:(g + 1) * rh, :], kj) for g in range(ATT_SPLIT)]
            for g, (m, l, acc) in enumerate(carry):
                sc = scs[g]
                if masked:
                    sc = _causal_mask(sc, g * rh)
                m_new = jnp.maximum(m, jnp.max(sc, axis=-1, keepdims=True))
                p = jnp.exp2((sc - m_new) * EXP2_C)
                alpha = jnp.exp2((m - m_new) * EXP2_C)
                l = alpha * l + jnp.sum(p, axis=-1, keepdims=True)
                acc = alpha * acc + _dot(p.astype(BF16), vj)
                out.append((m_new, l, acc))
            return tuple(out)

        one = (jnp.full((rh, 1), NEG, F32), jnp.zeros((rh, 1), F32), jnp.zeros((rh, VH), F32))
        nwide = (i * tq) // wide
        carry = lax.fori_loop(0, nwide, lambda j, c: blk(j * wide, wide, c, False), (one,) * ATT_SPLIT)
        carry = lax.fori_loop(nwide * (wide // tq), i, lambda j, c: blk(j * tq, tq, c, False), carry)
        carry = blk(i * tq, tq, carry, True)
        for g, (m, l, acc) in enumerate(carry):
            o_ref[g * rh:(g + 1) * rh, :] = acc / l
            lse_ref[0, g * rh:(g + 1) * rh, :] = jnp.broadcast_to(m * EXP2_C + jnp.log(l) * LOG2E, (rh, LANES))

    return pl.pallas_call(
        body, name=name, grid=(HEADS, s // tq),
        in_specs=[pl.BlockSpec((1, tq, HP), lambda h, i: (h, i, 0)),
                  pl.BlockSpec((1, s, HP), lambda h, i: (h, 0, 0)),
                  pl.BlockSpec((1, s, HP), lambda h, i: (h, 0, 0))],
        out_specs=[pl.BlockSpec((tq, VH), lambda h, i: (i, h)),
                   pl.BlockSpec((1, tq, LANES), lambda h, i: (h, i, 0))],
        out_shape=[jax.ShapeDtypeStruct((s, HEADS * VH), F32), jax.ShapeDtypeStruct((HEADS, s, LANES), F32)],
        compiler_params=_cp(("parallel", "arbitrary"), VMEM_LIMIT),
    )(q, k, v)


def _lane_group(shape, j):
    return (lax.broadcasted_iota(jnp.int32, shape, 1) + j * LANES) // (POOL // 4)


def _pool_win_fwd(z, name):
    s = z.shape[0]
    ch = _tile(s, 512)
    col0 = (IN_P - POOL) // LANES

    def body(p_ref, m_ref):
        j = pl.program_id(0)

        def chunk(r, _):
            off = pl.multiple_of(r * ch, ch)
            cur = p_ref[pl.ds(off, ch), :]
            hoff = pl.multiple_of(jnp.maximum(off - HALO, 0), 8)
            halo = jnp.where(r > 0, p_ref[pl.ds(hoff, HALO), :], 0.0)
            x = jnp.concatenate([halo, cur], axis=0)
            s2 = x + pltpu.roll(x, 1, 0)
            s4 = s2 + pltpu.roll(s2, 2, 0)
            s8 = s4 + pltpu.roll(s4, 4, 0)
            s16 = s8 + pltpu.roll(s8, 8, 0)
            grp = _lane_group((ch, LANES), j)
            sel = jnp.where(grp == 0, s2[HALO:], jnp.where(grp == 1, s4[HALO:], jnp.where(grp == 2, s8[HALO:], s16[HALO:])))
            t1 = (lax.broadcasted_iota(jnp.int32, (ch, LANES), 0) + off + 1).astype(F32)
            win = jnp.where(grp == 0, 2.0, jnp.where(grp == 1, 4.0, jnp.where(grp == 2, 8.0, 16.0)))
            m_ref[pl.ds(off, ch), :] = sel / jnp.minimum(t1, win) - cur
            return 0

        lax.fori_loop(0, s // ch, chunk, 0)

    return pl.pallas_call(
        body, name=name, grid=(POOL // LANES,),
        in_specs=[pl.BlockSpec((s, LANES), lambda j: (0, col0 + j))],
        out_specs=pl.BlockSpec((s, LANES), lambda j: (0, j)),
        out_shape=jax.ShapeDtypeStruct((s, POOL), F32),
        compiler_params=_cp(("parallel",), VMEM_LIMIT),
    )(z)


def _pool_win_bwd(dm, name):
    s = dm.shape[0]
    ch = _tile(s, 512)
    n = s // ch

    def body(dm_ref, dp_ref):
        j = pl.program_id(0)

        def chunk(r, _):
            off = pl.multiple_of(r * ch, ch)
            grp = _lane_group((ch + HALO, LANES), j)
            win = jnp.where(grp == 0, 2.0, jnp.where(grp == 1, 4.0, jnp.where(grp == 2, 8.0, 16.0)))
            cur = dm_ref[pl.ds(off, ch), :]
            hoff = pl.multiple_of(jnp.minimum(off + ch, s - HALO), 8)
            halo = jnp.where(r < n - 1, dm_ref[pl.ds(hoff, HALO), :], 0.0)
            x = jnp.concatenate([cur, halo], axis=0)
            t1 = (lax.broadcasted_iota(jnp.int32, (ch + HALO, LANES), 0) + off + 1).astype(F32)
            e = x / jnp.minimum(t1, win)
            tot = ch + HALO
            r2 = e + pltpu.roll(e, tot - 1, 0)
            r4 = r2 + pltpu.roll(r2, tot - 2, 0)
            r8 = r4 + pltpu.roll(r4, tot - 4, 0)
            r16 = r8 + pltpu.roll(r8, tot - 8, 0)
            g = grp[:ch]
            sel = jnp.where(g == 0, r2[:ch], jnp.where(g == 1, r4[:ch], jnp.where(g == 2, r8[:ch], r16[:ch])))
            dp_ref[pl.ds(off, ch), :] = (sel - cur).astype(BF16)
            return 0

        lax.fori_loop(0, n, chunk, 0)

    return pl.pallas_call(
        body, name=name, grid=(POOL // LANES,),
        in_specs=[pl.BlockSpec((s, LANES), lambda j: (0, j))],
        out_specs=pl.BlockSpec((s, LANES), lambda j: (0, j)),
        out_shape=jax.ShapeDtypeStruct((s, POOL), BF16),
        compiler_params=_cp(("parallel",), VMEM_LIMIT),
    )(dm)


def _head_mask(h):
    lane = lax.broadcasted_iota(jnp.int32, (CHUNK, SGU), 1)
    return (lane // (SGU // HEADS)) == h


def _tril(upper=False):
    row = lax.broadcasted_iota(jnp.int32, (CHUNK, CHUNK), 0)
    col = lax.broadcasted_iota(jnp.int32, (CHUNK, CHUNK), 1)
    return col >= row if upper else col <= row


def _sgu_gate(vn, wsp, bsp):
    out = []
    for cidx in range(vn.shape[0] // CHUNK):
        vc = vn[cidx * CHUNK:(cidx + 1) * CHUNK]
        zc = bsp
        for h in range(HEADS):
            zc = zc + jnp.where(_head_mask(h), _dot(wsp[h], vc), 0.0)
        out.append(zc)
    return jnp.concatenate(out, axis=0)


def _mix_out_fwd(o, z, m, x, wsp, bsp, wbd, psc, gsv, gout, wout, name):
    s = x.shape[0]
    tm = _tile(s, 512)

    def body(o_ref, uv_ref, m_ref, x_ref, wsp_ref, bsp_ref, wbd_ref, psc_ref, gsv_ref, gout_ref, wout_ref,
             x1_ref, mix_ref):
        g = gout_ref[...]
        an = _rms(o_ref[...], HEADS * VH)[0] * g[:, :512]
        uv = uv_ref[...]
        u, v = uv[:, :SGU], uv[:, SGU:]
        vn = (_rms(v, SGU)[0] * gsv_ref[...]).astype(BF16)
        tri = _tril()
        wsp_m = [jnp.where(tri, wsp_ref[h], 0.0).astype(BF16) for h in range(HEADS)]
        gm = u * _sgu_gate(vn, wsp_m, bsp_ref[...])
        gn = _rms(gm, SGU)[0] * g[:, 512:768]
        po = _dot(m_ref[...].astype(BF16), wbd_ref[...]) * psc_ref[...]
        pn = _rms(po, POOL)[0] * g[:, 768:]
        mix = jnp.concatenate([an, gn, pn], axis=1).astype(BF16)
        mix_ref[...] = mix
        x1_ref[...] = x_ref[...] + _dot(mix, wout_ref[...])

    row = lambda w, j: pl.BlockSpec((tm, w), lambda i: (i, j))
    return pl.pallas_call(
        body, name=name, grid=(s // tm,),
        in_specs=[row(512, 0), row(512, 1), row(POOL, 0), row(D, 0),
                  _acc((HEADS, CHUNK, CHUNK)), _acc((CHUNK, SGU)), _acc((POOL, POOL)), _acc((1, POOL)),
                  _acc((1, SGU)), _acc((1, D)), _res((D, D))],
        out_specs=[row(D, 0), row(D, 0)],
        out_shape=[jax.ShapeDtypeStruct((s, D), F32), jax.ShapeDtypeStruct((s, D), BF16)],
        compiler_params=_cp(("parallel",), VMEM_LIMIT),
    )(o, z, m, x, wsp, bsp, wbd, psc, gsv, gout, wout)


def _ffn_fwd(x1, g, wg, wu, wd, name):
    s = x1.shape[0]
    tm = _tile(s, 256)

    def body(x_ref, g_ref, wg_ref, wu_ref, wd_ref, x2_ref, a_ref, b_ref, h_ref):
        x = x_ref[...]
        h = (_rms(x, D)[0] * g_ref[...]).astype(BF16)
        h_ref[...] = h
        acc = jnp.zeros((tm, D), F32)
        for k in range(CHIPS):
            a = _dot(h, wg_ref[k])
            b = _dot(h, wu_ref[k])
            a_ref[k] = a
            b_ref[k] = b
            acc = acc + _dot((a * jax.nn.sigmoid(a) * b).astype(BF16), wd_ref[k])
        x2_ref[...] = x + acc

    row = lambda w: pl.BlockSpec((tm, w), lambda i: (i, 0))
    hrow = pl.BlockSpec((CHIPS, tm, SH), lambda i: (0, i, 0))
    hshape = jax.ShapeDtypeStruct((CHIPS, s, SH), F32)
    return pl.pallas_call(
        body, name=name, grid=(s // tm,),
        in_specs=[row(D), _acc((1, D)), _res((CHIPS, D, SH)), _res((CHIPS, D, SH)), _res((CHIPS, SH, D))],
        out_specs=[row(D), hrow, hrow, row(D)],
        out_shape=[jax.ShapeDtypeStruct((s, D), F32), hshape, hshape, jax.ShapeDtypeStruct((s, D), BF16)],
        compiler_params=_cp(("parallel",), VMEM_LIMIT),
    )(x1, g, wg, wu, wd)


def _loss_grad(y, tgt):
    s = y.shape[0]
    tm = _tile(s, 512)

    def body(y_ref, t_ref, dy_ref, l_ref):
        e = y_ref[...] - t_ref[...]
        dy_ref[...] = e * (1.0 / D)
        sq = jnp.sum(e * e, axis=0, keepdims=True)
        part = sq[:, :LANES]
        for c in range(1, D // LANES):
            part = part + sq[:, c * LANES:(c + 1) * LANES]
        _accumulate(l_ref, part, pl.program_id(0) == 0)

    row = pl.BlockSpec((tm, D), lambda i: (i, 0))
    return pl.pallas_call(
        body, name="loss_grad", grid=(s // tm,),
        in_specs=[row, row], out_specs=[row, _acc((1, LANES))],
        out_shape=[jax.ShapeDtypeStruct((s, D), F32), jax.ShapeDtypeStruct((1, LANES), F32)],
        compiler_params=_cp(("arbitrary",)),
    )(y, tgt)


def _wgrad(a, b, name):
    s, k = a.shape
    n = b.shape[1]
    half = lambda v: v if v <= 1408 else v // 2
    kb, nb, tt = half(k), half(n), _tile(s, 1024)

    def body(a_ref, b_ref, o_ref):
        _accumulate(o_ref, _dot_tn(a_ref[...].astype(BF16), b_ref[...].astype(BF16)), pl.program_id(2) == 0)

    return pl.pallas_call(
        body, name=name, grid=(k // kb, n // nb, s // tt),
        in_specs=[pl.BlockSpec((tt, kb), lambda i, j, t: (t, i)), pl.BlockSpec((tt, nb), lambda i, j, t: (t, j))],
        out_specs=pl.BlockSpec((kb, nb), lambda i, j, t: (i, j)),
        out_shape=jax.ShapeDtypeStruct((k, n), F32),
        compiler_params=_cp(("parallel", "parallel", "arbitrary"), VMEM_LIMIT),
    )(a, b)


def _wgrad_cols(a, b, name):
    s, k = a.shape
    n = b.shape[2]
    tt = _tile(s, 1024)

    def body(a_ref, b_ref, o_ref):
        _accumulate0(o_ref, _dot_tn(a_ref[...].astype(BF16), b_ref[0].astype(BF16)), pl.program_id(1) == 0)

    return pl.pallas_call(
        body, name=name, grid=(CHIPS, s // tt),
        in_specs=[pl.BlockSpec((tt, k), lambda c, t: (t, 0)), pl.BlockSpec((1, tt, n), lambda c, t: (c, t, 0))],
        out_specs=pl.BlockSpec((1, k, n), lambda c, t: (c, 0, 0)),
        out_shape=jax.ShapeDtypeStruct((CHIPS, k, n), F32),
        compiler_params=_cp(("parallel", "arbitrary"), VMEM_LIMIT),
    )(a, b)


def _wgrad_rows(a, b, name):
    s, n = a.shape[1:]
    nn = b.shape[1]
    tt = _tile(s, 1024)

    def body(a_ref, b_ref, o_ref):
        _accumulate0(o_ref, _dot_tn(a_ref[0].astype(BF16), b_ref[...].astype(BF16)), pl.program_id(1) == 0)

    return pl.pallas_call(
        body, name=name, grid=(CHIPS, s // tt),
        in_specs=[pl.BlockSpec((1, tt, n), lambda c, t: (c, t, 0)), pl.BlockSpec((tt, nn), lambda c, t: (t, 0))],
        out_specs=pl.BlockSpec((1, n, nn), lambda c, t: (c, 0, 0)),
        out_shape=jax.ShapeDtypeStruct((CHIPS, n, nn), F32),
        compiler_params=_cp(("parallel", "arbitrary"), VMEM_LIMIT),
    )(a, b)


def _ffn_bwd(dx2, x1, a, b, g, wg, wu, wd, name):
    s = x1.shape[0]
    tm = _tile(s, 256)

    def body(dx2_ref, x_ref, a_ref, b_ref, g_ref, wg_ref, wu_ref, wd_ref,
             dx1_ref, hid_ref, da_ref, db_ref, dg_ref):
        dx2 = dx2_ref[...]
        dyb = dx2.astype(BF16)
        dh = jnp.zeros((tm, D), F32)
        for k in range(CHIPS):
            av, bv = a_ref[k], b_ref[k]
            dhid = _dot_nt(dyb, wd_ref[k])
            sig = jax.nn.sigmoid(av)
            sa = av * sig
            hid_ref[k] = (sa * bv).astype(BF16)
            dbv = (dhid * sa).astype(BF16)
            dav = (dhid * bv * (sig * (1.0 + av * (1.0 - sig)))).astype(BF16)
            db_ref[k] = dbv
            da_ref[k] = dav
            dh = dh + _dot_nt(dav, wg_ref[k]) + _dot_nt(dbv, wu_ref[k])
        xn, r = _rms(x_ref[...], D)
        dxr, dg = _rms_bwd(xn, r, g_ref[...], dh, D)
        dx1_ref[...] = dx2 + dxr
        _accumulate(dg_ref, dg, pl.program_id(0) == 0)

    row = lambda w: pl.BlockSpec((tm, w), lambda i: (i, 0))
    hrow = pl.BlockSpec((CHIPS, tm, SH), lambda i: (0, i, 0))
    hid = jax.ShapeDtypeStruct((CHIPS, s, SH), BF16)
    return pl.pallas_call(
        body, name=name, grid=(s // tm,),
        in_specs=[row(D), row(D), hrow, hrow, _acc((1, D)), _res((CHIPS, D, SH)), _res((CHIPS, D, SH)),
                  _res((CHIPS, SH, D))],
        out_specs=[row(D), hrow, hrow, hrow, _acc((1, D))],
        out_shape=[jax.ShapeDtypeStruct((s, D), F32), hid, hid, hid, jax.ShapeDtypeStruct((1, D), F32)],
        compiler_params=_cp(("arbitrary",), VMEM_LIMIT),
    )(dx2, x1, a, b, g, wg, wu, wd)


def _mix_out_bwd(dx1, o, z, m, wsp, bsp, wbd, psc, gsv, gout, wout, name):
    s = dx1.shape[0]
    tm = _tile(s, 512)

    def body(dx1_ref, o_ref, uv_ref, m_ref, wsp_ref, bsp_ref, wbd_ref, psc_ref, gsv_ref, gout_ref, wout_ref,
             do_ref, dl_ref, duv_ref, dm_ref, dgo_ref, dgsv_ref, dpsc_ref, dwsp_ref, dbsp_ref, dwbd_ref):
        first = pl.program_id(0) == 0
        g = gout_ref[...]
        dmix = _dot_nt(dx1_ref[...].astype(BF16), wout_ref[...])
        o = o_ref[...]
        on, ro = _rms(o, HEADS * VH)
        do, dga = _rms_bwd(on, ro, g[:, :512], dmix[:, :512], HEADS * VH)
        for h in range(HEADS):
            sl = slice(h * VH, (h + 1) * VH)
            do_ref[h] = do[:, sl].astype(BF16)
            dl_ref[h] = jnp.broadcast_to(jnp.sum(do[:, sl] * o[:, sl], axis=-1, keepdims=True), (tm, LANES))
        uv = uv_ref[...]
        u, v = uv[:, :SGU], uv[:, SGU:]
        vx, rv = _rms(v, SGU)
        vn = (vx * gsv_ref[...]).astype(BF16)
        tri = _tril()
        wsp_m = [jnp.where(tri, wsp_ref[h], 0.0).astype(BF16) for h in range(HEADS)]
        zc = _sgu_gate(vn, wsp_m, bsp_ref[...])
        gm = u * zc
        gmn, rg = _rms(gm, SGU)
        dgm, dgg = _rms_bwd(gmn, rg, g[:, 512:768], dmix[:, 512:768], SGU)
        du = dgm * zc
        dzc = dgm * u
        dvn_parts = []
        dbsp = jnp.zeros((CHUNK, SGU), F32)
        dwsp = [jnp.zeros((CHUNK, CHUNK), F32) for _ in range(HEADS)]
        for cidx in range(tm // CHUNK):
            rs = slice(cidx * CHUNK, (cidx + 1) * CHUNK)
            dzc_c = dzc[rs]
            dbsp = dbsp + dzc_c
            dzb = dzc_c.astype(BF16)
            vc = vn[rs]
            dvn_c = jnp.zeros((CHUNK, SGU), F32)
            for h in range(HEADS):
                hm = _head_mask(h)
                dvn_c = dvn_c + jnp.where(hm, _dot_tn(wsp_m[h], dzb), 0.0)
                dwsp[h] = dwsp[h] + _dot_nt(jnp.where(hm, dzc_c, 0.0).astype(BF16), vc)
            dvn_parts.append(dvn_c)
        dvn = jnp.concatenate(dvn_parts, axis=0)
        dv, dgsv = _rms_bwd(vx, rv, gsv_ref[...], dvn, SGU)
        duv_ref[...] = jnp.concatenate([du, dv], axis=1).astype(BF16)
        mb = m_ref[...].astype(BF16)
        pw = _dot(mb, wbd_ref[...])
        po = pw * psc_ref[...]
        pon, rp = _rms(po, POOL)
        dpo, dgp = _rms_bwd(pon, rp, g[:, 768:], dmix[:, 768:], POOL)
        dpw = (dpo * psc_ref[...]).astype(BF16)
        dm_ref[...] = _dot_nt(dpw, wbd_ref[...])
        _accumulate(dgo_ref, jnp.concatenate([dga, dgg, dgp], axis=1), first)
        _accumulate(dgsv_ref, dgsv, first)
        _accumulate(dpsc_ref, jnp.sum(dpo * pw, axis=0, keepdims=True), first)
        _accumulate(dbsp_ref, dbsp, first)
        _accumulate(dwbd_ref, _dot_tn(mb, dpw), first)
        for h in range(HEADS):
            val = jnp.where(tri, dwsp[h], 0.0)

            @pl.when(first)
            def _(val=val, h=h):
                dwsp_ref[h] = val

            @pl.when(jnp.logical_not(first))
            def _(val=val, h=h):
                dwsp_ref[h] += val

    row = lambda w, j: pl.BlockSpec((tm, w), lambda i: (i, j))
    hspec = pl.BlockSpec((HEADS, tm, HP), lambda i: (0, i, 0))
    return pl.pallas_call(
        body, name=name, grid=(s // tm,),
        in_specs=[row(D, 0), row(512, 0), row(512, 1), row(POOL, 0),
                  _acc((HEADS, CHUNK, CHUNK)), _acc((CHUNK, SGU)),
                  _acc((POOL, POOL)), _acc((1, POOL)), _acc((1, SGU)), _acc((1, D)), _res((D, D))],
        out_specs=[hspec, hspec, row(512, 0), row(POOL, 0), _acc((1, D)), _acc((1, SGU)), _acc((1, POOL)),
                   _acc((HEADS, CHUNK, CHUNK)), _acc((CHUNK, SGU)), _acc((POOL, POOL))],
        out_shape=[jax.ShapeDtypeStruct((HEADS, s, HP), BF16), jax.ShapeDtypeStruct((HEADS, s, LANES), F32),
                   jax.ShapeDtypeStruct((s, 512), BF16), jax.ShapeDtypeStruct((s, POOL), F32),
                   jax.ShapeDtypeStruct((1, D), F32), jax.ShapeDtypeStruct((1, SGU), F32),
                   jax.ShapeDtypeStruct((1, POOL), F32), jax.ShapeDtypeStruct((HEADS, CHUNK, CHUNK), F32),
                   jax.ShapeDtypeStruct((CHUNK, SGU), F32), jax.ShapeDtypeStruct((POOL, POOL), F32)],
        compiler_params=_cp(("arbitrary",), VMEM_LIMIT),
    )(dx1, o, z, m, wsp, bsp, wbd, psc, gsv, gout, wout)


def _attn_bwd(q, k, v, do, lse, delta, name):
    s = q.shape[1]
    tq = tk = _tile(s, 512)
    nq = s // tq
    wide = ATT_WIDE * tq if s % (ATT_WIDE * tq) == 0 else tq

    def body(q_ref, k_ref, v_ref, do_ref, lse_ref, dl_ref, dq_ref, dk_ref, dv_ref):
        j = pl.program_id(1)

        @pl.when(j == 0)
        def _():
            dq_ref[...] = jnp.zeros_like(dq_ref)

        kj, vj = k_ref[0], v_ref[0]
        rh = tq // ATT_SPLIT

        def blk(start, rows, dk, dv, masked):
            offs = [pl.multiple_of(start + g * rh, rh) for g in range(rows // rh)]
            qs = [q_ref[0, pl.ds(off, rh), :] for off in offs]
            dos = [do_ref[0, pl.ds(off, rh), :] for off in offs]
            scs = [_dot_nt(qi, kj) for qi in qs]
            dps = [_dot_nt(doi, vj) for doi in dos]
            for g, off in enumerate(offs):
                lse_i = lse_ref[0, pl.ds(off, rh), :][:, :1]
                dl_i = dl_ref[0, pl.ds(off, rh), :][:, :1]
                sc = _causal_mask(scs[g], g * rh) if masked else scs[g]
                p = jnp.exp2(sc * EXP2_C - lse_i)
                ds = (p * (dps[g] - dl_i)).astype(BF16)
                dv = dv + _dot_tn(p.astype(BF16), dos[g])
                dk = dk + _dot_tn(ds, qs[g])
                dq_ref[0, pl.ds(off, rh), :] += _dot(ds, kj) * SCALE
            return dk, dv

        per = wide // tq
        zero = jnp.zeros((tk, HP), F32)
        dk, dv = blk(j * tq, tq, zero, zero, True)
        first_wide = (j + per) // per
        dk, dv = lax.fori_loop(j + 1, jnp.minimum(first_wide * per, nq), lambda i, c: blk(i * tq, tq, *c, False), (dk, dv))
        dk, dv = lax.fori_loop(first_wide, nq // per, lambda i, c: blk(i * wide, wide, *c, False), (dk, dv))
        dk_ref[0] = dk * SCALE
        dv_ref[0] = dv

    full = lambda: pl.BlockSpec((1, s, HP), lambda h, j: (h, 0, 0))
    blk_spec = lambda: pl.BlockSpec((1, tk, HP), lambda h, j: (h, j, 0))
    out = jax.ShapeDtypeStruct((HEADS, s, HP), F32)
    return pl.pallas_call(
        body, name=name, grid=(HEADS, s // tk),
        in_specs=[full(), blk_spec(), blk_spec(), full(), full(), full()],
        out_specs=[full(), blk_spec(), blk_spec()], out_shape=[out] * 3,
        compiler_params=_cp(("parallel", "arbitrary"), VMEM_LIMIT),
    )(q, k, v, do, lse, delta)


def _mla_prep_bwd(dq, dk, dv, z, tabs, gql, gkv, gq, gk, wq, wk, wv, name):
    s = z.shape[0]
    tm = _tile(s, 512)

    def body(dq_ref, dk_ref, dv_ref, ql_ref, kv_ref, kr_ref, c_ref, sa_ref, sb_ref, gql_ref, gkv_ref, gq_ref, gk_ref,
             wq_ref, wk_ref, wv_ref,
             dz_ref, qn_ref, kvn_ref, dqr_ref, dkr_ref, dvr_ref, dgql_ref, dgkv_ref, dgq_ref, dgk_ref):
        first = pl.program_id(0) == 0
        qx, rq = _rms(ql_ref[...], QL)
        qn = (qx * gql_ref[...]).astype(BF16)
        kx, rk = _rms(kv_ref[...], KVL)
        kvn = (kx * gkv_ref[...]).astype(BF16)
        qn_ref[...] = qn
        kvn_ref[...] = kvn
        qraw = _dot(qn, wq_ref[...])
        kraw = _dot(kvn, wk_ref[...])
        kr = kr_ref[...]
        c, sa, sb = c_ref[...], sa_ref[...], sb_ref[...]
        lane = lax.broadcasted_iota(jnp.int32, (tm, HP), 1)
        rope_lanes = (lane >= NOPE) & (lane < QK)
        dkrope = jnp.zeros((tm, HP), F32)
        dgq = jnp.zeros((1, HP), F32)
        dgk = jnp.zeros((1, HP), F32)
        for h in range(HEADS):
            sl = slice(h * HP, (h + 1) * HP)
            xn, r = _rms(qraw[:, sl], QK)
            dx, dg = _rms_bwd(xn, r, gq_ref[...], _rope_t(dq_ref[h], c, sa, sb), QK)
            dqr_ref[:, sl] = dx.astype(BF16)
            dgq = dgq + dg
            xn, r = _rms(kraw[:, sl] + kr, QK)
            dx, dg = _rms_bwd(xn, r, gk_ref[...], _rope_t(dk_ref[h], c, sa, sb), QK)
            dkr_ref[:, sl] = dx.astype(BF16)
            dgk = dgk + dg
            dkrope = dkrope + jnp.where(rope_lanes, dx, 0.0)
            dvr_ref[:, sl] = dv_ref[h].astype(BF16)
        dqn = _dot_nt(dqr_ref[...], wq_ref[...])
        dql, dgql = _rms_bwd(qx, rq, gql_ref[...], dqn, QL)
        dkvn = _dot_nt(dkr_ref[...], wk_ref[...]) + _dot_nt(dvr_ref[...], wv_ref[...])
        dkv, dgkv = _rms_bwd(kx, rk, gkv_ref[...], dkvn, KVL)
        dz_ref[...] = jnp.concatenate([dql, dkv, dkrope], axis=1).astype(BF16)
        _accumulate(dgql_ref, dgql, first)
        _accumulate(dgkv_ref, dgkv, first)
        _accumulate(dgq_ref, dgq, first)
        _accumulate(dgk_ref, dgk, first)

    row = lambda w, j: pl.BlockSpec((tm, w), lambda i: (i, j))
    hspec = pl.BlockSpec((HEADS, tm, HP), lambda i: (0, i, 0))
    sd = lambda w, dt: jax.ShapeDtypeStruct((s, w), dt)
    return pl.pallas_call(
        body, name=name, grid=(s // tm,),
        in_specs=[hspec, hspec, hspec, row(QL, 0), row(KVL, 2), row(HP, 3), row(HP, 0), row(HP, 0), row(HP, 0),
                  _acc((1, QL)), _acc((1, KVL)), _acc((1, HP)), _acc((1, HP)),
                  _acc((QL, HEADS * HP)), _acc((KVL, HEADS * HP)), _acc((KVL, HEADS * HP))],
        out_specs=[row(512, 0), row(QL, 0), row(KVL, 0), row(512, 0), row(512, 0), row(512, 0),
                   _acc((1, QL)), _acc((1, KVL)), _acc((1, HP)), _acc((1, HP))],
        out_shape=[sd(512, BF16), sd(QL, BF16), sd(KVL, BF16), sd(512, BF16), sd(512, BF16), sd(512, BF16),
                   jax.ShapeDtypeStruct((1, QL), F32), jax.ShapeDtypeStruct((1, KVL), F32),
                   jax.ShapeDtypeStruct((1, HP), F32), jax.ShapeDtypeStruct((1, HP), F32)],
        compiler_params=_cp(("arbitrary",), VMEM_LIMIT),
    )(dq, dk, dv, z, z, z, *tabs, gql, gkv, gq, gk, wq, wk, wv)


def _in_proj_bwd(dzm, duv, dp, x, dx1, g, win, name):
    s = x.shape[0]
    tm = _tile(s, 512)

    def body(dzm_ref, duv_ref, dp_ref, x_ref, dx1_ref, g_ref, w_ref, dx_ref, dg_ref):
        dh = _dot_nt(dzm_ref[...], w_ref[:, 0:512]) + _dot_nt(duv_ref[...], w_ref[:, 512:1024]) \
            + _dot_nt(dp_ref[...], w_ref[:, 1024:IN_P])
        xn, r = _rms(x_ref[...], D)
        dxr, dg = _rms_bwd(xn, r, g_ref[...], dh, D)
        dx_ref[...] = dx1_ref[...] + dxr
        _accumulate(dg_ref, dg, pl.program_id(0) == 0)

    row = lambda w: pl.BlockSpec((tm, w), lambda i: (i, 0))
    return pl.pallas_call(
        body, name=name, grid=(s // tm,),
        in_specs=[row(512), row(512), row(POOL), row(D), row(D), _acc((1, D)), _res((D, IN_P))],
        out_specs=[row(D), _acc((1, D))],
        out_shape=[jax.ShapeDtypeStruct((s, D), F32), jax.ShapeDtypeStruct((1, D), F32)],
        compiler_params=_cp(("arbitrary",), VMEM_LIMIT),
    )(dzm, duv, dp, x, dx1, g, win)


def _adamw(w, g0, g1, m, v, name):
    _, r, c = w.shape
    tr = _row_tile(r, 512)
    c1 = 1.0 - B1 ** STEP
    c2 = 1.0 - B2 ** STEP

    def body(w_ref, g0_ref, g1_ref, m_ref, v_ref, g_ref, d_ref, nm_ref, nv_ref):
        gv = jnp.where(pl.program_id(0) == 0, g0_ref[...], g1_ref[...])
        g_ref[0] = gv
        nm = B1 * m_ref[0] + (1.0 - B1) * gv
        nv = B2 * v_ref[0] + (1.0 - B2) * (gv * gv)
        nm_ref[0] = nm
        nv_ref[0] = nv
        d_ref[0] = -LR * ((nm / c1) / (jnp.sqrt(nv / c2) + ADAM_EPS) + WD * w_ref[0])

    spec = pl.BlockSpec((1, tr, c), lambda l, i: (l, i, 0))
    out = jax.ShapeDtypeStruct((DEPTH, r, c), F32)
    return pl.pallas_call(
        body, name=name, grid=(DEPTH, r // tr),
        in_specs=[spec, pl.BlockSpec((tr, c), lambda l, i: (i * (1 - l), 0)), pl.BlockSpec((tr, c), lambda l, i: (i * l, 0)),
                  spec, spec],
        out_specs=[spec] * 4, out_shape=[out] * 4, compiler_params=_cp(("parallel", "parallel")),
    )(w, g0, g1, m, v)


ANY = pl.BlockSpec(memory_space=pl.ANY)


def _place():
    x, y, c = lax.axis_index("x"), lax.axis_index("y"), lax.axis_index("c")
    chips = [(1 - x, y), (x, 1 - y), (1 - x, 1 - y)]
    return x, y, c, chips


def _half_rows(ref, lead, hh, half, align):
    rows = pl.ds(pl.multiple_of(hh * half, align), half)
    return ref.at[rows, :] if lead is None else ref.at[lead, rows, :]


def _row_align(dtype):
    return 16 if dtype == BF16 else 8


def _sems(n):
    return [pltpu.SemaphoreType.DMA((n,)), pltpu.SemaphoreType.DMA((n,)), pltpu.SemaphoreType.DMA((n,))]


def _comm_call(body, ins, out_shapes, nsems, name):
    return pl.pallas_call(
        body, name=name, in_specs=[ANY] * len(ins), out_specs=[ANY] * len(out_shapes), out_shape=out_shapes,
        scratch_shapes=_sems(nsems), compiler_params=pltpu.CompilerParams(has_side_effects=True),
    )(*ins)


def _all_gather_chips(shards, name):
    n = len(shards)
    halves = [a.shape[0] // 2 for a in shards]
    aligns = [_row_align(a.dtype) for a in shards]
    assert all(h % al == 0 for h, al in zip(halves, aligns))

    def body(*refs):
        ins, outs, (send_sems, recv_sems, local_sems) = refs[:n], refs[n:2 * n], refs[2 * n:]
        x, y, c, chips = _place()
        me = 2 * x + y
        sibling = (x, y, 1 - c)

        def copy(sem, src, dst, to):
            return pltpu.make_async_remote_copy(src_ref=src, dst_ref=dst, send_sem=send_sems.at[sem],
                                                recv_sem=recv_sems.at[sem], device_id=to, device_id_type=MESH)

        own, first, passed = [], [], []
        for a in range(n):
            cp = pltpu.make_async_copy(ins[a], outs[a].at[me], local_sems.at[a])
            cp.start()
            own.append(cp)
            my_half = _half_rows(ins[a], None, c, halves[a], aligns[a])
            for j, (cx, cy) in enumerate(chips):
                cp = copy(6 * a + j, my_half, _half_rows(outs[a], me, c, halves[a], aligns[a]), (cx, cy, c))
                cp.start()
                first.append(cp)
        for a in range(n):
            for j, (cx, cy) in enumerate(chips):
                landed = _half_rows(outs[a], 2 * cx + cy, c, halves[a], aligns[a])
                copy(6 * a + j, landed, landed, (cx, cy, c)).wait_recv()
                fwd = copy(6 * a + 3 + j, landed, landed, sibling)
                fwd.start()
                passed.append(fwd)
        for a in range(n):
            for j, (cx, cy) in enumerate(chips):
                other = _half_rows(outs[a], 2 * cx + cy, 1 - c, halves[a], aligns[a])
                copy(6 * a + 3 + j, other, other, sibling).wait_recv()
        for cp in first + passed:
            cp.wait_send()
        for cp in own:
            cp.wait()

    return _comm_call(body, shards, [jax.ShapeDtypeStruct((CHIPS,) + a.shape, a.dtype) for a in shards], 6 * n, name)


def _pair_swap_halves(arrs, name):
    n = len(arrs)
    halves = [a.shape[1] // 2 for a in arrs]

    def body(*refs):
        ins, outs, (send_sems, recv_sems, _) = refs[:n], refs[n:2 * n], refs[2 * n:]
        x, y, c, _ = _place()
        cps = []
        for a in range(n):
            src = ins[a].at[:, pl.ds(pl.multiple_of((1 - c) * halves[a], 8), halves[a]), :]
            cp = pltpu.make_async_remote_copy(src_ref=src, dst_ref=outs[a], send_sem=send_sems.at[a],
                                              recv_sem=recv_sems.at[a], device_id=(x, y, 1 - c), device_id_type=MESH)
            cp.start()
            cps.append(cp)
        for cp in cps:
            cp.wait()

    return _comm_call(body, arrs, [jax.ShapeDtypeStruct((CHIPS, h, a.shape[2]), a.dtype) for a, h in zip(arrs, halves)],
                      n, name)


def _pair_add(full, got, cidx, name):
    _, half, cols = got.shape
    tr = _row_tile(half, 256)
    nt = half // tr

    grid_spec = pltpu.PrefetchScalarGridSpec(
        num_scalar_prefetch=1, grid=(CHIPS, nt),
        in_specs=[pl.BlockSpec((1, tr, cols), lambda k, r, c_ref: (k, c_ref[0] * nt + r, 0)),
                  pl.BlockSpec((1, tr, cols), lambda k, r, c_ref: (k, r, 0))],
        out_specs=pl.BlockSpec((1, tr, cols), lambda k, r, c_ref: (k, r, 0)))

    def body(c_ref, a_ref, b_ref, o_ref):
        o_ref[...] = a_ref[...] + b_ref[...]

    return pl.pallas_call(
        body, name=name, grid_spec=grid_spec, out_shape=jax.ShapeDtypeStruct(got.shape, got.dtype),
        compiler_params=_cp(("parallel", "parallel")),
    )(cidx, full, got)


def _chip_scatter(parts, name):
    n = len(parts)

    def body(*refs):
        ins, outs, (send_sems, recv_sems, local_sems) = refs[:n], refs[n:2 * n], refs[2 * n:]
        x, y, c, chips = _place()
        me = 2 * x + y
        own, sends = [], []
        for a in range(n):
            cp = pltpu.make_async_copy(ins[a].at[me], outs[a].at[me], local_sems.at[a])
            cp.start()
            own.append(cp)
            for j, (cx, cy) in enumerate(chips):
                cp = pltpu.make_async_remote_copy(src_ref=ins[a].at[2 * cx + cy], dst_ref=outs[a].at[me],
                                                  send_sem=send_sems.at[3 * a + j], recv_sem=recv_sems.at[3 * a + j],
                                                  device_id=(cx, cy, c), device_id_type=MESH)
                cp.start()
                sends.append(cp)
        for a in range(n):
            for j, (cx, cy) in enumerate(chips):
                slot = outs[a].at[2 * cx + cy]
                pltpu.make_async_remote_copy(src_ref=slot, dst_ref=slot, send_sem=send_sems.at[3 * a + j],
                                             recv_sem=recv_sems.at[3 * a + j], device_id=(cx, cy, c),
                                             device_id_type=MESH).wait_recv()
        for cp in sends:
            cp.wait_send()
        for cp in own:
            cp.wait()

    return _comm_call(body, parts, [jax.ShapeDtypeStruct(a.shape, a.dtype) for a in parts], 3 * n, name)


def _sum_chips(q, name):
    nk, half, cols = q.shape
    tr = _row_tile(half, 256)

    def body(q_ref, o_ref):
        o_ref[...] = ((q_ref[0] + q_ref[1]) + q_ref[2]) + q_ref[3]

    return pl.pallas_call(
        body, name=name, grid=(half // tr,),
        in_specs=[pl.BlockSpec((nk, tr, cols), lambda r: (0, r, 0))],
        out_specs=pl.BlockSpec((tr, cols), lambda r: (r, 0)),
        out_shape=jax.ShapeDtypeStruct((half, cols), q.dtype), compiler_params=_cp(("parallel",)),
    )(q)


def _pair_join(mine, name):
    n = len(mine)
    halves = [a.shape[0] for a in mine]

    def body(*refs):
        ins, outs, (send_sems, recv_sems, local_sems) = refs[:n], refs[n:2 * n], refs[2 * n:]
        x, y, c, _ = _place()
        cps, own = [], []
        for a in range(n):
            dst = _half_rows(outs[a], None, c, halves[a], 8)
            loc = pltpu.make_async_copy(ins[a], dst, local_sems.at[a])
            loc.start()
            own.append(loc)
            cp = pltpu.make_async_remote_copy(src_ref=ins[a], dst_ref=dst, send_sem=send_sems.at[a], recv_sem=recv_sems.at[a],
                                              device_id=(x, y, 1 - c), device_id_type=MESH)
            cp.start()
            cps.append(cp)
        for cp in cps:
            cp.wait()
        for cp in own:
            cp.wait()

    return _comm_call(body, mine, [jax.ShapeDtypeStruct((2 * h, a.shape[1]), a.dtype) for a, h in zip(mine, halves)],
                      n, name)


BIG = [("w_in", (D, IN_W), 1), ("w_q_up", (QL, HEADS * QK), 1), ("w_kv_up", (KVL, HEADS * (NOPE + VH)), 1),
       ("w_out", (D, D), 0), ("w_gate", (D, HID), 1), ("w_up", (D, HID), 1), ("w_down", (HID, D), 0)]
SMALL = [("g_mix_norm", (D,)), ("g_q_lat", (QL,)), ("g_kv_lat", (KVL,)), ("g_q_head", (QK,)), ("g_k_head", (QK,)),
         ("g_sgu_v", (SGU,)), ("w_spatial", (HEADS, CHUNK, CHUNK)), ("b_spatial", (HEADS, CHUNK)),
         ("w_pool", (4, 64, 64)), ("pool_scale", (POOL,)), ("g_out_mla", (512,)), ("g_out_sgu", (SGU,)),
         ("g_out_pool", (POOL,)), ("g_ffn_norm", (D,))]
ORDER = ["g_mix_norm", "w_in", "g_q_lat", "w_q_up", "g_kv_lat", "w_kv_up", "g_q_head", "g_k_head", "g_sgu_v",
         "w_spatial", "b_spatial", "w_pool", "pool_scale", "g_out_mla", "g_out_sgu", "g_out_pool", "w_out",
         "g_ffn_norm", "w_gate", "w_up", "w_down"]
DEPTH = 2
COLS = 1024
SMALL_N = sum(math.prod(s) for _, s in SMALL) * DEPTH
assert SMALL_N % CHIPS == 0
SMALL_ROWS = -(-(SMALL_N // CHIPS) // (16 * COLS)) * 16


def _unsplit_cols(g):
    return g.transpose(1, 0, 2).reshape(g.shape[1], CHIPS * g.shape[2])


def _split_cols(full):
    r, c = full.shape
    return full.reshape(r, CHIPS, c // CHIPS).transpose(1, 0, 2)


def _kernel_weights(g):
    win = _unsplit_cols(g["w_in"])
    zeros = lambda r, c: jnp.zeros((r, c), BF16)
    o2, o3, o4 = QL + KVL, QL + KVL + ROPE, QL + KVL + ROPE + 2 * SGU
    win_p = jnp.concatenate([win[:, :o2], zeros(D, NOPE), win[:, o2:o3], zeros(D, HP - QK), win[:, o3:o4], win[:, o4:]], axis=1)
    wq = _unsplit_cols(g["w_q_up"]).reshape(QL, HEADS, QK)
    wq_p = jnp.pad(wq, ((0, 0), (0, 0), (0, HP - QK))).reshape(QL, HEADS * HP)
    wkv = _unsplit_cols(g["w_kv_up"]).reshape(KVL, HEADS, NOPE + VH)
    wk_p = jnp.pad(wkv[:, :, :NOPE], ((0, 0), (0, 0), (0, HP - NOPE))).reshape(KVL, HEADS * HP)
    wv_p = wkv[:, :, NOPE:].reshape(KVL, HEADS * VH)
    return dict(win=win_p, wq=wq_p, wk=wk_p, wv=wv_p, wout=g["w_out"].reshape(D, D), wg=g["w_gate"], wu=g["w_up"],
                wd=g["w_down"])


def _small_operands(p, l):
    row = lambda v: v.reshape(1, -1)
    pad = lambda v: jnp.pad(v, (0, HP - QK)).reshape(1, HP)
    wpool = p["w_pool"][l]
    wbd = jnp.zeros((POOL, POOL), F32)
    for g in range(4):
        wbd = lax.dynamic_update_slice(wbd, wpool[g], (g * 64, g * 64))
    return dict(
        g_mix=row(p["g_mix_norm"][l]), gql=row(p["g_q_lat"][l]), gkv=row(p["g_kv_lat"][l]),
        gq=pad(p["g_q_head"][l]), gk=pad(p["g_k_head"][l]), gsv=row(p["g_sgu_v"][l]),
        wsp=p["w_spatial"][l], bsp=jnp.repeat(p["b_spatial"][l].T, SGU // HEADS, axis=1),
        wbd=wbd.astype(BF16), psc=row(p["pool_scale"][l]),
        gout=jnp.concatenate([p["g_out_mla"][l], p["g_out_sgu"][l], p["g_out_pool"][l]]).reshape(1, D),
        g_ffn=row(p["g_ffn_norm"][l]))


def _big_grads(g):
    dwin = g["win"]
    o2 = QL + KVL
    gin = jnp.concatenate([dwin[:, :o2], dwin[:, o2 + NOPE:o2 + NOPE + ROPE], dwin[:, 512:]], axis=1)
    gq = g["wq"].reshape(QL, HEADS, HP)[:, :, :QK].reshape(QL, HEADS * QK)
    gk = g["wk"].reshape(KVL, HEADS, HP)[:, :, :NOPE]
    gv = g["wv"].reshape(KVL, HEADS, VH)
    gkv = jnp.concatenate([gk, gv], axis=2).reshape(KVL, HEADS * (NOPE + VH))
    return {"w_in": _split_cols(gin), "w_q_up": _split_cols(gq), "w_kv_up": _split_cols(gkv),
            "w_out": g["wout"].reshape(CHIPS, D // CHIPS, D), "w_gate": g["wg"], "w_up": g["wu"], "w_down": g["wd"]}


def _small_grads(g):
    go = g["gout"].reshape(-1)
    return {"g_mix_norm": g["g_mix"].reshape(-1), "g_q_lat": g["gql"].reshape(-1), "g_kv_lat": g["gkv"].reshape(-1),
            "g_q_head": g["gq"].reshape(-1)[:QK], "g_k_head": g["gk"].reshape(-1)[:QK], "g_sgu_v": g["gsv"].reshape(-1),
            "w_spatial": g["wsp"], "b_spatial": g["bsp"].reshape(CHUNK, HEADS, SGU // HEADS).sum(-1).T,
            "w_pool": jnp.stack([g["wbd"][i * 64:(i + 1) * 64, i * 64:(i + 1) * 64] for i in range(4)]),
            "pool_scale": g["psc"].reshape(-1), "g_out_mla": go[:512], "g_out_sgu": go[512:768],
            "g_out_pool": go[768:], "g_ffn_norm": g["g_ffn"].reshape(-1)}


def _pack_small_grads(small):
    sm = jnp.concatenate([small[l][n].reshape(-1) for l in range(DEPTH) for n, _ in SMALL]).reshape(CHIPS, SMALL_N // CHIPS)
    return jnp.pad(sm, ((0, 0), (0, SMALL_ROWS * COLS - SMALL_N // CHIPS))).reshape(CHIPS, SMALL_ROWS, COLS)


def _unpack_small_grads(gathered):
    flat = gathered.reshape(CHIPS, SMALL_ROWS * COLS)[:, :SMALL_N // CHIPS].reshape(-1)
    out, off = [], 0
    for _ in range(DEPTH):
        layer = {}
        for n, shape in SMALL:
            k = math.prod(shape)
            layer[n] = flat[off:off + k].reshape(shape)
            off += k
        out.append(layer)
    return out


def _layer_fwd(x, tabs, kw, sp, l):
    t = f"_l{l}"
    z, hb = _in_proj_fwd(x, sp["g_mix"], kw["win"], "in_proj_fwd" + t)
    q, k, v = _mla_prep_fwd(z, tabs, sp["gql"], sp["gkv"], sp["gq"], sp["gk"], kw["wq"], kw["wk"], kw["wv"],
                            "mla_prep_fwd" + t)
    o, lse = _attn_fwd(q, k, v, "attn_fwd" + t)
    m = _pool_win_fwd(z, "pool_win_fwd" + t)
    x1, mix = _mix_out_fwd(o, z, m, x, sp["wsp"], sp["bsp"], sp["wbd"], sp["psc"], sp["gsv"], sp["gout"], kw["wout"],
                           "mix_out_fwd" + t)
    x2, a, b, h2 = _ffn_fwd(x1, sp["g_ffn"], kw["wg"], kw["wu"], kw["wd"], "ffn_fwd" + t)
    saved = dict(x=x, z=z, hb=hb, q=q, k=k, v=v, o=o, lse=lse, m=m, x1=x1, mix=mix, a=a, b=b, h2=h2)
    return x2, saved


def _layer_bwd(dx2, sv, tabs, kw, sp, l):
    t = f"_l{l}"
    g = {}
    dx1, hid, da, db, g["g_ffn"] = _ffn_bwd(dx2, sv["x1"], sv["a"], sv["b"], sp["g_ffn"], kw["wg"], kw["wu"], kw["wd"],
                                            "ffn_bwd" + t)
    g["wd"] = _wgrad_rows(hid, dx2, "wgrad_down" + t)
    g["wg"] = _wgrad_cols(sv["h2"], da, "wgrad_gate" + t)
    g["wu"] = _wgrad_cols(sv["h2"], db, "wgrad_up" + t)
    do, delta, duv, dm, g["gout"], g["gsv"], g["psc"], g["wsp"], g["bsp"], g["wbd"] = _mix_out_bwd(
        dx1, sv["o"], sv["z"], sv["m"], sp["wsp"], sp["bsp"], sp["wbd"], sp["psc"], sp["gsv"], sp["gout"], kw["wout"],
        "mix_out_bwd" + t)
    g["wout"] = _wgrad(sv["mix"], dx1, "wgrad_out" + t)
    dp = _pool_win_bwd(dm, "pool_win_bwd" + t)
    dq, dk, dv = _attn_bwd(sv["q"], sv["k"], sv["v"], do, sv["lse"], delta, "attn_bwd" + t)
    dzm, qn, kvn, dqr, dkr, dvr, g["gql"], g["gkv"], g["gq"], g["gk"] = _mla_prep_bwd(
        dq, dk, dv, sv["z"], tabs, sp["gql"], sp["gkv"], sp["gq"], sp["gk"], kw["wq"], kw["wk"], kw["wv"],
        "mla_prep_bwd" + t)
    g["wq"] = _wgrad(qn, dqr, "wgrad_q_up" + t)
    g["wk"] = _wgrad(kvn, dkr, "wgrad_k_up" + t)
    g["wv"] = _wgrad(kvn, dvr, "wgrad_v_up" + t)
    dx, g["g_mix"] = _in_proj_bwd(dzm, duv, dp, sv["x"], dx1, sp["g_mix"], kw["win"], "in_proj_bwd" + t)
    g["win"] = jnp.concatenate([_wgrad(sv["hb"], dzm, "wgrad_in_a" + t), _wgrad(sv["hb"], duv, "wgrad_in_b" + t),
                                _wgrad(sv["hb"], dp, "wgrad_in_c" + t)], axis=1)
    return dx, g


def _rope_inv_freq():
    half = ROPE // 2
    inv = 1.0 / (ROPE_THETA ** (jnp.arange(half, dtype=F32) / half))
    return jnp.concatenate([jnp.zeros((NOPE,), F32), inv, inv, jnp.zeros((HP - QK,), F32)]).reshape(1, HP)


def kernel(x, positions, g_mix_norm, w_in, g_q_lat, w_q_up, g_kv_lat, w_kv_up, g_q_head, g_k_head, g_sgu_v, w_spatial, b_spatial, w_pool, pool_scale, g_out_mla, g_out_sgu, g_out_pool, w_out, g_ffn_norm, w_gate, w_up, w_down, loss_target, m_g_mix_norm, m_w_in, m_g_q_lat, m_w_q_up, m_g_kv_lat, m_w_kv_up, m_g_q_head, m_g_k_head, m_g_sgu_v, m_w_spatial, m_b_spatial, m_w_pool, m_pool_scale, m_g_out_mla, m_g_out_sgu, m_g_out_pool, m_w_out, m_g_ffn_norm, m_w_gate, m_w_up, m_w_down, v_g_mix_norm, v_w_in, v_g_q_lat, v_w_q_up, v_g_kv_lat, v_w_kv_up, v_g_q_head, v_g_k_head, v_g_sgu_v, v_w_spatial, v_b_spatial, v_w_pool, v_pool_scale, v_g_out_mla, v_g_out_sgu, v_g_out_pool, v_w_out, v_g_ffn_norm, v_w_gate, v_w_up, v_w_down):
    given = dict(locals())
    p = {n: given[n] for n in ORDER}
    seq = x.shape[1]

    names = [(l, n) for l in range(DEPTH) for n, _, _ in BIG]
    got = _all_gather_chips([p[n][l].astype(BF16) for l, n in names], "all_gather_weights")
    gathered = [{n: got[names.index((l, n))] for n, _, _ in BIG} for l in range(DEPTH)]

    loss_part, dx, big, small = _local_step(x.reshape(seq, D), positions.reshape(seq, 1), loss_target.reshape(seq, D),
                                            gathered, p)
    loss = lax.psum(loss_part, ("x", "y", "c"))

    arrs = [big[l][n] for l, n in names] + [_pack_small_grads(small)]
    cidx = lax.axis_index("c").astype(jnp.int32).reshape(1)
    theirs = _pair_swap_halves(arrs, "grad_pair_swap")
    pair = [_pair_add(a, t, cidx, f"grad_pair_add_{i}") for i, (a, t) in enumerate(zip(arrs, theirs))]
    landed = _chip_scatter(pair, "grad_chip_scatter")
    sums = _pair_join([_sum_chips(q, f"grad_chip_sum_{i}") for i, q in enumerate(landed)], "grad_pair_join")
    gsmall = _unpack_small_grads(_all_gather_chips([sums[-1]], "all_gather_small_grads")[0])
    grads = {n: [sums[names.index((l, n))] for l in range(DEPTH)] for n, _, _ in BIG}
    grads.update({n: [gsmall[l][n] for l in range(DEPTH)] for n, _ in SMALL})

    out = {}
    for n in ORDER:
        w = p[n]
        three_d = (DEPTH, -1, w.shape[-1])
        two_d = three_d[1:]
        res = _adamw(w.reshape(three_d), grads[n][0].reshape(two_d), grads[n][1].reshape(two_d),
                     given["m_" + n].reshape(three_d), given["v_" + n].reshape(three_d), "adamw_" + n)
        out[n] = [r.reshape(w.shape) for r in res]
    return (loss, dx.reshape(x.shape), *[out[n][i] for i in range(4) for n in ORDER])


def _local_step(xs, pos, tgt, gathered, p):
    kws = [_kernel_weights(gathered[l]) for l in range(DEPTH)]
    sps = [_small_operands(p, l) for l in range(DEPTH)]
    tabs = _rope_tables(pos, _rope_inv_freq())
    saved, h = [], xs
    for l in range(DEPTH):
        h, sv = _layer_fwd(h, tabs, kws[l], sps[l], l)
        saved.append(sv)
    dy, lpart = _loss_grad(h, tgt)
    grads = [None] * DEPTH
    for l in reversed(range(DEPTH)):
        dy, grads[l] = _layer_bwd(dy, saved[l], tabs, kws[l], sps[l], l)
    return 0.5 / D * jnp.sum(lpart), dy, [_big_grads(g) for g in grads], [_small_grads(g) for g in grads]
```

```python
import functools
import math

import jax
import jax.numpy as jnp
from jax import lax
from jax.experimental import pallas as pl
from jax.experimental.pallas import tpu as pltpu

F32 = jnp.float32
BF16 = jnp.bfloat16
MESH = pl.DeviceIdType.MESH

D = 1024
HEADS = 4
QK = 96
NOPE = 64
ROPE = 32
VH = 128
HP = 128
QL = 256
KVL = 128
SGU = 256
POOL = 256
CHUNK = 128
HID = 2816
CHIPS = 4
SH = HID // CHIPS
IN_W = 1184
IN_P = 1280
EPS = 1e-6
ROPE_THETA = 10000.0
SCALE = 1.0 / math.sqrt(QK)
LOG2E = 1.4426950408889634
EXP2_C = SCALE * LOG2E
ATT_SPLIT = 2
ATT_WIDE = 4
NEG = -1e30
HALO = 16

LR, B1, B2, ADAM_EPS, WD, STEP = 0.001, 0.9, 0.999, 1e-08, 0.01, 10

VMEM_LIMIT = 56 * 1024 * 1024
LANES = 128
HC = 256


def _cp(sem, vmem=None):
    return pltpu.CompilerParams(dimension_semantics=sem, vmem_limit_bytes=vmem)


def _res(shape):
    nd = len(shape)
    return pl.BlockSpec(shape, lambda *_: (0,) * nd, pipeline_mode=pl.Buffered(1))


def _acc(shape):
    nd = len(shape)
    return pl.BlockSpec(shape, lambda *_: (0,) * nd)


def _dot(a, b):
    return jnp.dot(a, b, preferred_element_type=F32)


def _dot_nt(a, b):
    return lax.dot_general(a, b, (((1,), (1,)), ((), ())), preferred_element_type=F32)


def _dot_tn(a, b):
    return lax.dot_general(a, b, (((0,), (0,)), ((), ())), preferred_element_type=F32)


def _rms(x, n):
    r = lax.rsqrt(jnp.sum(x * x, axis=-1, keepdims=True) * (1.0 / n) + EPS)
    return x * r, r


def _rms_bwd(xn, r, g, dy, n):
    dn = dy * g
    dx = r * (dn - xn * (jnp.sum(dn * xn, axis=-1, keepdims=True) * (1.0 / n)))
    return dx, jnp.sum(dy * xn, axis=0, keepdims=True)


def _accumulate(ref, val, first):
    @pl.when(first)
    def _():
        ref[...] = val

    @pl.when(jnp.logical_not(first))
    def _():
        ref[...] += val


def _accumulate0(ref, val, first):
    @pl.when(first)
    def _():
        ref[0] = val

    @pl.when(jnp.logical_not(first))
    def _():
        ref[0] += val


def _tile(s, t):
    return min(s, t)


def _row_tile(r, cap):
    if r <= cap:
        return r
    return max(t for t in range(8, cap + 1, 8) if r % t == 0)


def _rope_tables(pos, invf):
    s = pos.shape[0]
    tm = _tile(s, 1024)

    def body(pos_ref, invf_ref, c_ref, sa_ref, sb_ref):
        ang = pos_ref[...].astype(F32) * invf_ref[...]
        c, sn = jnp.cos(ang), jnp.sin(ang)
        lane = lax.broadcasted_iota(jnp.int32, ang.shape, 1)
        first = (lane >= NOPE) & (lane < NOPE + ROPE // 2)
        second = (lane >= NOPE + ROPE // 2) & (lane < QK)
        c_ref[...] = jnp.where(first | second, c, 1.0)
        sa_ref[...] = jnp.where(first, -sn, 0.0)
        sb_ref[...] = jnp.where(second, sn, 0.0)

    out = jax.ShapeDtypeStruct((s, HP), F32)
    return pl.pallas_call(
        body, name="rope_tables", grid=(s // tm,),
        in_specs=[pl.BlockSpec((tm, 1), lambda i: (i, 0)), _acc((1, HP))],
        out_specs=[pl.BlockSpec((tm, HP), lambda i: (i, 0))] * 3,
        out_shape=[out] * 3, compiler_params=_cp(("parallel",)),
    )(pos, invf)


def _rope(x, c, sa, sb):
    return x * c + pltpu.roll(x, HP - ROPE // 2, 1) * sa + pltpu.roll(x, ROPE // 2, 1) * sb


def _rope_t(d, c, sa, sb):
    return d * c + pltpu.roll(d * sa, ROPE // 2, 1) + pltpu.roll(d * sb, HP - ROPE // 2, 1)


def _in_proj_fwd(x, g, w, name):
    s = x.shape[0]
    tm = _tile(s, 512)

    def body(x_ref, g_ref, w_ref, z_ref, h_ref):
        xn, _ = _rms(x_ref[...], D)
        h = (xn * g_ref[...]).astype(BF16)
        h_ref[...] = h
        z_ref[...] = _dot(h, w_ref[...])

    return pl.pallas_call(
        body, name=name, grid=(s // tm,),
        in_specs=[pl.BlockSpec((tm, D), lambda i: (i, 0)), _acc((1, D)), _res((D, IN_P))],
        out_specs=[pl.BlockSpec((tm, IN_P), lambda i: (i, 0)), pl.BlockSpec((tm, D), lambda i: (i, 0))],
        out_shape=[jax.ShapeDtypeStruct((s, IN_P), F32), jax.ShapeDtypeStruct((s, D), BF16)],
        compiler_params=_cp(("parallel",), VMEM_LIMIT),
    )(x, g, w)


def _mla_prep_fwd(z, tabs, gql, gkv, gq, gk, wq, wk, wv, name):
    s = z.shape[0]
    tm = _tile(s, 512)

    def body(ql_ref, kv_ref, kr_ref, c_ref, sa_ref, sb_ref, gql_ref, gkv_ref, gq_ref, gk_ref,
             wq_ref, wk_ref, wv_ref, q_out, k_out, v_out):
        qn = (_rms(ql_ref[...], QL)[0] * gql_ref[...]).astype(BF16)
        kvn = (_rms(kv_ref[...], KVL)[0] * gkv_ref[...]).astype(BF16)
        qraw = _dot(qn, wq_ref[...])
        kraw = _dot(kvn, wk_ref[...])
        vraw = _dot(kvn, wv_ref[...])
        kr = kr_ref[...]
        c, sa, sb = c_ref[...], sa_ref[...], sb_ref[...]
        for h in range(HEADS):
            sl = slice(h * HP, (h + 1) * HP)
            xq = _rms(qraw[:, sl], QK)[0] * gq_ref[...]
            q_out[h] = _rope(xq, c, sa, sb).astype(BF16)
            xk = _rms(kraw[:, sl] + kr, QK)[0] * gk_ref[...]
            k_out[h] = _rope(xk, c, sa, sb).astype(BF16)
            v_out[h] = vraw[:, sl].astype(BF16)

    row = lambda w, j: pl.BlockSpec((tm, w), lambda i: (i, j))
    hspec = pl.BlockSpec((HEADS, tm, HP), lambda i: (0, i, 0))
    hshape = jax.ShapeDtypeStruct((HEADS, s, HP), BF16)
    return pl.pallas_call(
        body, name=name, grid=(s // tm,),
        in_specs=[row(QL, 0), row(KVL, 2), row(HP, 3), row(HP, 0), row(HP, 0), row(HP, 0),
                  _acc((1, QL)), _acc((1, KVL)), _acc((1, HP)), _acc((1, HP)),
                  _acc((QL, HEADS * HP)), _acc((KVL, HEADS * HP)), _acc((KVL, HEADS * HP))],
        out_specs=[hspec] * 3, out_shape=[hshape] * 3,
        compiler_params=_cp(("parallel",)),
    )(z, z, z, *tabs, gql, gkv, gq, gk, wq, wk, wv)


def _causal_mask(s, row0):
    row = lax.broadcasted_iota(jnp.int32, s.shape, 0) + row0
    col = lax.broadcasted_iota(jnp.int32, s.shape, 1)
    return jnp.where(col <= row, s, NEG)


def _attn_fwd(q, k, v, name):
    s = q.shape[1]
    tq = _tile(s, 512)
    wide = ATT_WIDE * tq if s % (ATT_WIDE * tq) == 0 else tq
    rh = tq // ATT_SPLIT

    def body(q_ref, k_ref, v_ref, o_ref, lse_ref):
        i = pl.program_id(1)

        def blk(off, tk, carry, masked):
            off = pl.multiple_of(off, tq)
            kj = k_ref[0, pl.ds(off, tk), :]
            vj = v_ref[0, pl.ds(off, tk), :]
            out = []
            scs = [_dot_nt(q_ref[0, g * rh:(g + 1) * rh, :], kj) for g in range(ATT_SPLIT)]
            for g, (m, l, acc) in enumerate(carry):
                sc = scs[g]
                if masked:
                    sc = _causal_mask(sc, g * rh)
                m_new = jnp.maximum(m, jnp.max(sc, axis=-1, keepdims=True))
                p = jnp.exp2((sc - m_new) * EXP2_C)
                alpha = jnp.exp2((m - m_new) * EXP2_C)
                l = alpha * l + jnp.sum(p, axis=-1, keepdims=True)
                acc = alpha * acc + _dot(p.astype(BF16), vj)
                out.append((m_new, l, acc))
            return tuple(out)

        one = (jnp.full((rh, 1), NEG, F32), jnp.zeros((rh, 1), F32), jnp.zeros((rh, VH), F32))
        nwide = (i * tq) // wide
        carry = lax.fori_loop(0, nwide, lambda j, c: blk(j * wide, wide, c, False), (one,) * ATT_SPLIT)
        carry = lax.fori_loop(nwide * (wide // tq), i, lambda j, c: blk(j * tq, tq, c, False), carry)
        carry = blk(i * tq, tq, carry, True)
        for g, (m, l, acc) in enumerate(carry):
            o_ref[g * rh:(g + 1) * rh, :] = acc / l
            lse_ref[0, g * rh:(g + 1) * rh, :] = jnp.broadcast_to(m * EXP2_C + jnp.log(l) * LOG2E, (rh, LANES))

    return pl.pallas_call(
        body, name=name, grid=(HEADS, s // tq),
        in_specs=[pl.BlockSpec((1, tq, HP), lambda h, i: (h, i, 0)),
                  pl.BlockSpec((1, s, HP), lambda h, i: (h, 0, 0)),
                  pl.BlockSpec((1, s, HP), lambda h, i: (h, 0, 0))],
        out_specs=[pl.BlockSpec((tq, VH), lambda h, i: (i, h)),
                   pl.BlockSpec((1, tq, LANES), lambda h, i: (h, i, 0))],
        out_shape=[jax.ShapeDtypeStruct((s, HEADS * VH), F32), jax.ShapeDtypeStruct((HEADS, s, LANES), F32)],
        compiler_params=_cp(("parallel", "arbitrary"), VMEM_LIMIT),
    )(q, k, v)


def _lane_group(shape, j):
    return (lax.broadcasted_iota(jnp.int32, shape, 1) + j * LANES) // (POOL // 4)


def _pool_win_fwd(z, name):
    s = z.shape[0]
    ch = _tile(s, 512)
    col0 = (IN_P - POOL) // LANES

    def body(p_ref, m_ref):
        j = pl.program_id(0)

        def chunk(r, _):
            off = pl.multiple_of(r * ch, ch)
            cur = p_ref[pl.ds(off, ch), :]
            hoff = pl.multiple_of(jnp.maximum(off - HALO, 0), 8)
            halo = jnp.where(r > 0, p_ref[pl.ds(hoff, HALO), :], 0.0)
            x = jnp.concatenate([halo, cur], axis=0)
            s2 = x + pltpu.roll(x, 1, 0)
            s4 = s2 + pltpu.roll(s2, 2, 0)
            s8 = s4 + pltpu.roll(s4, 4, 0)
            s16 = s8 + pltpu.roll(s8, 8, 0)
            grp = _lane_group((ch, LANES), j)
            sel = jnp.where(grp == 0, s2[HALO:], jnp.where(grp == 1, s4[HALO:], jnp.where(grp == 2, s8[HALO:], s16[HALO:])))
            t1 = (lax.broadcasted_iota(jnp.int32, (ch, LANES), 0) + off + 1).astype(F32)
            win = jnp.where(grp == 0, 2.0, jnp.where(grp == 1, 4.0, jnp.where(grp == 2, 8.0, 16.0)))
            m_ref[pl.ds(off, ch), :] = sel / jnp.minimum(t1, win) - cur
            return 0

        lax.fori_loop(0, s // ch, chunk, 0)

    return pl.pallas_call(
        body, name=name, grid=(POOL // LANES,),
        in_specs=[pl.BlockSpec((s, LANES), lambda j: (0, col0 + j))],
        out_specs=pl.BlockSpec((s, LANES), lambda j: (0, j)),
        out_shape=jax.ShapeDtypeStruct((s, POOL), F32),
        compiler_params=_cp(("parallel",), VMEM_LIMIT),
    )(z)


def _pool_win_bwd(dm, name):
    s = dm.shape[0]
    ch = _tile(s, 512)
    n = s // ch

    def body(dm_ref, dp_ref):
        j = pl.program_id(0)

        def chunk(r, _):
            off = pl.multiple_of(r * ch, ch)
            grp = _lane_group((ch + HALO, LANES), j)
            win = jnp.where(grp == 0, 2.0, jnp.where(grp == 1, 4.0, jnp.where(grp == 2, 8.0, 16.0)))
            cur = dm_ref[pl.ds(off, ch), :]
            hoff = pl.multiple_of(jnp.minimum(off + ch, s - HALO), 8)
            halo = jnp.where(r < n - 1, dm_ref[pl.ds(hoff, HALO), :], 0.0)
            x = jnp.concatenate([cur, halo], axis=0)
            t1 = (lax.broadcasted_iota(jnp.int32, (ch + HALO, LANES), 0) + off + 1).astype(F32)
            e = x / jnp.minimum(t1, win)
            tot = ch + HALO
            r2 = e + pltpu.roll(e, tot - 1, 0)
            r4 = r2 + pltpu.roll(r2, tot - 2, 0)
            r8 = r4 + pltpu.roll(r4, tot - 4, 0)
            r16 = r8 + pltpu.roll(r8, tot - 8, 0)
            g = grp[:ch]
            sel = jnp.where(g == 0, r2[:ch], jnp.where(g == 1, r4[:ch], jnp.where(g == 2, r8[:ch], r16[:ch])))
            dp_ref[pl.ds(off, ch), :] = (sel - cur).astype(BF16)
            return 0

        lax.fori_loop(0, n, chunk, 0)

    return pl.pallas_call(
        body, name=name, grid=(POOL // LANES,),
        in_specs=[pl.BlockSpec((s, LANES), lambda j: (0, j))],
        out_specs=pl.BlockSpec((s, LANES), lambda j: (0, j)),
        out_shape=jax.ShapeDtypeStruct((s, POOL), BF16),
        compiler_params=_cp(("parallel",), VMEM_LIMIT),
    )(dm)


def _head_mask(h):
    lane = lax.broadcasted_iota(jnp.int32, (CHUNK, SGU), 1)
    return (lane // (SGU // HEADS)) == h


def _tril(upper=False):
    row = lax.broadcasted_iota(jnp.int32, (CHUNK, CHUNK), 0)
    col = lax.broadcasted_iota(jnp.int32, (CHUNK, CHUNK), 1)
    return col >= row if upper else col <= row


def _sgu_gate(vn, wsp, bsp):
    out = []
    for cidx in range(vn.shape[0] // CHUNK):
        vc = vn[cidx * CHUNK:(cidx + 1) * CHUNK]
        zc = bsp
        for h in range(HEADS):
            zc = zc + jnp.where(_head_mask(h), _dot(wsp[h], vc), 0.0)
        out.append(zc)
    return jnp.concatenate(out, axis=0)


def _mix_out_fwd(o, z, m, x, wsp, bsp, wbd, psc, gsv, gout, wout, name):
    s = x.shape[0]
    tm = _tile(s, 512)

    def body(o_ref, uv_ref, m_ref, x_ref, wsp_ref, bsp_ref, wbd_ref, psc_ref, gsv_ref, gout_ref, wout_ref,
             x1_ref, mix_ref):
        g = gout_ref[...]
        an = _rms(o_ref[...], HEADS * VH)[0] * g[:, :512]
        uv = uv_ref[...]
        u, v = uv[:, :SGU], uv[:, SGU:]
        vn = (_rms(v, SGU)[0] * gsv_ref[...]).astype(BF16)
        tri = _tril()
        wsp_m = [jnp.where(tri, wsp_ref[h], 0.0).astype(BF16) for h in range(HEADS)]
        gm = u * _sgu_gate(vn, wsp_m, bsp_ref[...])
        gn = _rms(gm, SGU)[0] * g[:, 512:768]
        po = _dot(m_ref[...].astype(BF16), wbd_ref[...]) * psc_ref[...]
        pn = _rms(po, POOL)[0] * g[:, 768:]
        mix = jnp.concatenate([an, gn, pn], axis=1).astype(BF16)
        mix_ref[...] = mix
        x1_ref[...] = x_ref[...] + _dot(mix, wout_ref[...])

    row = lambda w, j: pl.BlockSpec((tm, w), lambda i: (i, j))
    return pl.pallas_call(
        body, name=name, grid=(s // tm,),
        in_specs=[row(512, 0), row(512, 1), row(POOL, 0), row(D, 0),
                  _acc((HEADS, CHUNK, CHUNK)), _acc((CHUNK, SGU)), _acc((POOL, POOL)), _acc((1, POOL)),
                  _acc((1, SGU)), _acc((1, D)), _res((D, D))],
        out_specs=[row(D, 0), row(D, 0)],
        out_shape=[jax.ShapeDtypeStruct((s, D), F32), jax.ShapeDtypeStruct((s, D), BF16)],
        compiler_params=_cp(("parallel",), VMEM_LIMIT),
    )(o, z, m, x, wsp, bsp, wbd, psc, gsv, gout, wout)


def _ffn_fwd(x1, g, wg, wu, wd, name):
    s = x1.shape[0]
    tm = _tile(s, 256)

    def body(x_ref, g_ref, wg_ref, wu_ref, wd_ref, x2_ref, a_ref, b_ref, h_ref):
        x = x_ref[...]
        h = (_rms(x, D)[0] * g_ref[...]).astype(BF16)
        h_ref[...] = h
        acc = jnp.zeros((tm, D), F32)
        for k in range(CHIPS):
            a = _dot(h, wg_ref[k])
            b = _dot(h, wu_ref[k])
            a_ref[k] = a
            b_ref[k] = b
            acc = acc + _dot((a * jax.nn.sigmoid(a) * b).astype(BF16), wd_ref[k])
        x2_ref[...] = x + acc

    row = lambda w: pl.BlockSpec((tm, w), lambda i: (i, 0))
    hrow = pl.BlockSpec((CHIPS, tm, SH), lambda i: (0, i, 0))
    hshape = jax.ShapeDtypeStruct((CHIPS, s, SH), F32)
    return pl.pallas_call(
        body, name=name, grid=(s // tm,),
        in_specs=[row(D), _acc((1, D)), _res((CHIPS, D, SH)), _res((CHIPS, D, SH)), _res((CHIPS, SH, D))],
        out_specs=[row(D), hrow, hrow, row(D)],
        out_shape=[jax.ShapeDtypeStruct((s, D), F32), hshape, hshape, jax.ShapeDtypeStruct((s, D), BF16)],
        compiler_params=_cp(("parallel",), VMEM_LIMIT),
    )(x1, g, wg, wu, wd)


def _loss_grad(y, tgt):
    s = y.shape[0]
    tm = _tile(s, 512)

    def body(y_ref, t_ref, dy_ref, l_ref):
        e = y_ref[...] - t_ref[...]
        dy_ref[...] = e * (1.0 / D)
        sq = jnp.sum(e * e, axis=0, keepdims=True)
        part = sq[:, :LANES]
        for c in range(1, D // LANES):
            part = part + sq[:, c * LANES:(c + 1) * LANES]
        _accumulate(l_ref, part, pl.program_id(0) == 0)

    row = pl.BlockSpec((tm, D), lambda i: (i, 0))
    return pl.pallas_call(
        body, name="loss_grad", grid=(s // tm,),
        in_specs=[row, row], out_specs=[row, _acc((1, LANES))],
        out_shape=[jax.ShapeDtypeStruct((s, D), F32), jax.ShapeDtypeStruct((1, LANES), F32)],
        compiler_params=_cp(("arbitrary",)),
    )(y, tgt)


def _wgrad(a, b, name):
    s, k = a.shape
    n = b.shape[1]
    half = lambda v: v if v <= 1408 else v // 2
    kb, nb, tt = half(k), half(n), _tile(s, 1024)

    def body(a_ref, b_ref, o_ref):
        _accumulate(o_ref, _dot_tn(a_ref[...].astype(BF16), b_ref[...].astype(BF16)), pl.program_id(2) == 0)

    return pl.pallas_call(
        body, name=name, grid=(k // kb, n // nb, s // tt),
        in_specs=[pl.BlockSpec((tt, kb), lambda i, j, t: (t, i)), pl.BlockSpec((tt, nb), lambda i, j, t: (t, j))],
        out_specs=pl.BlockSpec((kb, nb), lambda i, j, t: (i, j)),
        out_shape=jax.ShapeDtypeStruct((k, n), F32),
        compiler_params=_cp(("parallel", "parallel", "arbitrary"), VMEM_LIMIT),
    )(a, b)


def _wgrad_cols(a, b, name):
    s, k = a.shape
    n = b.shape[2]
    tt = _tile(s, 1024)

    def body(a_ref, b_ref, o_ref):
        _accumulate0(o_ref, _dot_tn(a_ref[...].astype(BF16), b_ref[0].astype(BF16)), pl.program_id(1) == 0)

    return pl.pallas_call(
        body, name=name, grid=(CHIPS, s // tt),
        in_specs=[pl.BlockSpec((tt, k), lambda c, t: (t, 0)), pl.BlockSpec((1, tt, n), lambda c, t: (c, t, 0))],
        out_specs=pl.BlockSpec((1, k, n), lambda c, t: (c, 0, 0)),
        out_shape=jax.ShapeDtypeStruct((CHIPS, k, n), F32),
        compiler_params=_cp(("parallel", "arbitrary"), VMEM_LIMIT),
    )(a, b)


def _wgrad_rows(a, b, name):
    s, n = a.shape[1:]
    nn = b.shape[1]
    tt = _tile(s, 1024)

    def body(a_ref, b_ref, o_ref):
        _accumulate0(o_ref, _dot_tn(a_ref[0].astype(BF16), b_ref[...].astype(BF16)), pl.program_id(1) == 0)

    return pl.pallas_call(
        body, name=name, grid=(CHIPS, s // tt),
        in_specs=[pl.BlockSpec((1, tt, n), lambda c, t: (c, t, 0)), pl.BlockSpec((tt, nn), lambda c, t: (t, 0))],
        out_specs=pl.BlockSpec((1, n, nn), lambda c, t: (c, 0, 0)),
        out_shape=jax.ShapeDtypeStruct((CHIPS, n, nn), F32),
        compiler_params=_cp(("parallel", "arbitrary"), VMEM_LIMIT),
    )(a, b)


def _ffn_bwd(dx2, x1, a, b, g, wg, wu, wd, name):
    s = x1.shape[0]
    tm = _tile(s, 256)

    def body(dx2_ref, x_ref, a_ref, b_ref, g_ref, wg_ref, wu_ref, wd_ref,
             dx1_ref, hid_ref, da_ref, db_ref, dg_ref):
        dx2 = dx2_ref[...]
        dyb = dx2.astype(BF16)
        dh = jnp.zeros((tm, D), F32)
        for k in range(CHIPS):
            av, bv = a_ref[k], b_ref[k]
            dhid = _dot_nt(dyb, wd_ref[k])
            sig = jax.nn.sigmoid(av)
            sa = av * sig
            hid_ref[k] = (sa * bv).astype(BF16)
            dbv = (dhid * sa).astype(BF16)
            dav = (dhid * bv * (sig * (1.0 + av * (1.0 - sig)))).astype(BF16)
            db_ref[k] = dbv
            da_ref[k] = dav
            dh = dh + _dot_nt(dav, wg_ref[k]) + _dot_nt(dbv, wu_ref[k])
        xn, r = _rms(x_ref[...], D)
        dxr, dg = _rms_bwd(xn, r, g_ref[...], dh, D)
        dx1_ref[...] = dx2 + dxr
        _accumulate(dg_ref, dg, pl.program_id(0) == 0)

    row = lambda w: pl.BlockSpec((tm, w), lambda i: (i, 0))
    hrow = pl.BlockSpec((CHIPS, tm, SH), lambda i: (0, i, 0))
    hid = jax.ShapeDtypeStruct((CHIPS, s, SH), BF16)
    return pl.pallas_call(
        body, name=name, grid=(s // tm,),
        in_specs=[row(D), row(D), hrow, hrow, _acc((1, D)), _res((CHIPS, D, SH)), _res((CHIPS, D, SH)),
                  _res((CHIPS, SH, D))],
        out_specs=[row(D), hrow, hrow, hrow, _acc((1, D))],
        out_shape=[jax.ShapeDtypeStruct((s, D), F32), hid, hid, hid, jax.ShapeDtypeStruct((1, D), F32)],
        compiler_params=_cp(("arbitrary",), VMEM_LIMIT),
    )(dx2, x1, a, b, g, wg, wu, wd)


def _mix_out_bwd(dx1, o, z, m, wsp, bsp, wbd, psc, gsv, gout, wout, name):
    s = dx1.shape[0]
    tm = _tile(s, 512)

    def body(dx1_ref, o_ref, uv_ref, m_ref, wsp_ref, bsp_ref, wbd_ref, psc_ref, gsv_ref, gout_ref, wout_ref,
             do_ref, dl_ref, duv_ref, dm_ref, dgo_ref, dgsv_ref, dpsc_ref, dwsp_ref, dbsp_ref, dwbd_ref):
        first = pl.program_id(0) == 0
        g = gout_ref[...]
        dmix = _dot_nt(dx1_ref[...].astype(BF16), wout_ref[...])
        o = o_ref[...]
        on, ro = _rms(o, HEADS * VH)
        do, dga = _rms_bwd(on, ro, g[:, :512], dmix[:, :512], HEADS * VH)
        for h in range(HEADS):
            sl = slice(h * VH, (h + 1) * VH)
            do_ref[h] = do[:, sl].astype(BF16)
            dl_ref[h] = jnp.broadcast_to(jnp.sum(do[:, sl] * o[:, sl], axis=-1, keepdims=True), (tm, LANES))
        uv = uv_ref[...]
        u, v = uv[:, :SGU], uv[:, SGU:]
        vx, rv = _rms(v, SGU)
        vn = (vx * gsv_ref[...]).astype(BF16)
        tri = _tril()
        wsp_m = [jnp.where(tri, wsp_ref[h], 0.0).astype(BF16) for h in range(HEADS)]
        zc = _sgu_gate(vn, wsp_m, bsp_ref[...])
        gm = u * zc
        gmn, rg = _rms(gm, SGU)
        dgm, dgg = _rms_bwd(gmn, rg, g[:, 512:768], dmix[:, 512:768], SGU)
        du = dgm * zc
        dzc = dgm * u
        dvn_parts = []
        dbsp = jnp.zeros((CHUNK, SGU), F32)
        dwsp = [jnp.zeros((CHUNK, CHUNK), F32) for _ in range(HEADS)]
        for cidx in range(tm // CHUNK):
            rs = slice(cidx * CHUNK, (cidx + 1) * CHUNK)
            dzc_c = dzc[rs]
            dbsp = dbsp + dzc_c
            dzb = dzc_c.astype(BF16)
            vc = vn[rs]
            dvn_c = jnp.zeros((CHUNK, SGU), F32)
            for h in range(HEADS):
                hm = _head_mask(h)
                dvn_c = dvn_c + jnp.where(hm, _dot_tn(wsp_m[h], dzb), 0.0)
                dwsp[h] = dwsp[h] + _dot_nt(jnp.where(hm, dzc_c, 0.0).astype(BF16), vc)
            dvn_parts.append(dvn_c)
        dvn = jnp.concatenate(dvn_parts, axis=0)
        dv, dgsv = _rms_bwd(vx, rv, gsv_ref[...], dvn, SGU)
        duv_ref[...] = jnp.concatenate([du, dv], axis=1).astype(BF16)
        mb = m_ref[...].astype(BF16)
        pw = _dot(mb, wbd_ref[...])
        po = pw * psc_ref[...]
        pon, rp = _rms(po, POOL)
        dpo, dgp = _rms_bwd(pon, rp, g[:, 768:], dmix[:, 768:], POOL)
        dpw = (dpo * psc_ref[...]).astype(BF16)
        dm_ref[...] = _dot_nt(dpw, wbd_ref[...])
        _accumulate(dgo_ref, jnp.concatenate([dga, dgg, dgp], axis=1), first)
        _accumulate(dgsv_ref, dgsv, first)
        _accumulate(dpsc_ref, jnp.sum(dpo * pw, axis=0, keepdims=True), first)
        _accumulate(dbsp_ref, dbsp, first)
        _accumulate(dwbd_ref, _dot_tn(mb, dpw), first)
        for h in range(HEADS):
            val = jnp.where(tri, dwsp[h], 0.0)

            @pl.when(first)
            def _(val=val, h=h):
                dwsp_ref[h] = val

            @pl.when(jnp.logical_not(first))
            def _(val=val, h=h):
                dwsp_ref[h] += val

    row = lambda w, j: pl.BlockSpec((tm, w), lambda i: (i, j))
    hspec = pl.BlockSpec((HEADS, tm, HP), lambda i: (0, i, 0))
    return pl.pallas_call(
        body, name=name, grid=(s // tm,),
        in_specs=[row(D, 0), row(512, 0), row(512, 1), row(POOL, 0),
                  _acc((HEADS, CHUNK, CHUNK)), _acc((CHUNK, SGU)),
                  _acc((POOL, POOL)), _acc((1, POOL)), _acc((1, SGU)), _acc((1, D)), _res((D, D))],
        out_specs=[hspec, hspec, row(512, 0), row(POOL, 0), _acc((1, D)), _acc((1, SGU)), _acc((1, POOL)),
                   _acc((HEADS, CHUNK, CHUNK)), _acc((CHUNK, SGU)), _acc((POOL, POOL))],
        out_shape=[jax.ShapeDtypeStruct((HEADS, s, HP), BF16), jax.ShapeDtypeStruct((HEADS, s, LANES), F32),
                   jax.ShapeDtypeStruct((s, 512), BF16), jax.ShapeDtypeStruct((s, POOL), F32),
                   jax.ShapeDtypeStruct((1, D), F32), jax.ShapeDtypeStruct((1, SGU), F32),
                   jax.ShapeDtypeStruct((1, POOL), F32), jax.ShapeDtypeStruct((HEADS, CHUNK, CHUNK), F32),
                   jax.ShapeDtypeStruct((CHUNK, SGU), F32), jax.ShapeDtypeStruct((POOL, POOL), F32)],
        compiler_params=_cp(("arbitrary",), VMEM_LIMIT),
    )(dx1, o, z, m, wsp, bsp, wbd, psc, gsv, gout, wout)


def _attn_bwd(q, k, v, do, lse, delta, name):
    s = q.shape[1]
    tq = tk = _tile(s, 512)
    nq = s // tq
    wide = ATT_WIDE * tq if s % (ATT_WIDE * tq) == 0 else tq

    def body(q_ref, k_ref, v_ref, do_ref, lse_ref, dl_ref, dq_ref, dk_ref, dv_ref):
        j = pl.program_id(1)

        @pl.when(j == 0)
        def _():
            dq_ref[...] = jnp.zeros_like(dq_ref)

        kj, vj = k_ref[0], v_ref[0]
        rh = tq // ATT_SPLIT

        def blk(start, rows, dk, dv, masked):
            offs = [pl.multiple_of(start + g * rh, rh) for g in range(rows // rh)]
            qs = [q_ref[0, pl.ds(off, rh), :] for off in offs]
            dos = [do_ref[0, pl.ds(off, rh), :] for off in offs]
            scs = [_dot_nt(qi, kj) for qi in qs]
            dps = [_dot_nt(doi, vj) for doi in dos]
            for g, off in enumerate(offs):
                lse_i = lse_ref[0, pl.ds(off, rh), :][:, :1]
                dl_i = dl_ref[0, pl.ds(off, rh), :][:, :1]
                sc = _causal_mask(scs[g], g * rh) if masked else scs[g]
                p = jnp.exp2(sc * EXP2_C - lse_i)
                ds = (p * (dps[g] - dl_i)).astype(BF16)
                dv = dv + _dot_tn(p.astype(BF16), dos[g])
                dk = dk + _dot_tn(ds, qs[g])
                dq_ref[0, pl.ds(off, rh), :] += _dot(ds, kj) * SCALE
            return dk, dv

        per = wide // tq
        zero = jnp.zeros((tk, HP), F32)
        dk, dv = blk(j * tq, tq, zero, zero, True)
        first_wide = (j + per) // per
        dk, dv = lax.fori_loop(j + 1, jnp.minimum(first_wide * per, nq), lambda i, c: blk(i * tq, tq, *c, False), (dk, dv))
        dk, dv = lax.fori_loop(first_wide, nq // per, lambda i, c: blk(i * wide, wide, *c, False), (dk, dv))
        dk_ref[0] = dk * SCALE
        dv_ref[0] = dv

    full = lambda: pl.BlockSpec((1, s, HP), lambda h, j: (h, 0, 0))
    blk_spec = lambda: pl.BlockSpec((1, tk, HP), lambda h, j: (h, j, 0))
    out = jax.ShapeDtypeStruct((HEADS, s, HP), F32)
    return pl.pallas_call(
        body, name=name, grid=(HEADS, s // tk),
        in_specs=[full(), blk_spec(), blk_spec(), full(), full(), full()],
        out_specs=[full(), blk_spec(), blk_spec()], out_shape=[out] * 3,
        compiler_params=_cp(("parallel", "arbitrary"), VMEM_LIMIT),
    )(q, k, v, do, lse, delta)


def _mla_prep_bwd(dq, dk, dv, z, tabs, gql, gkv, gq, gk, wq, wk, wv, name):
    s = z.shape[0]
    tm = _tile(s, 512)

    def body(dq_ref, dk_ref, dv_ref, ql_ref, kv_ref, kr_ref, c_ref, sa_ref, sb_ref, gql_ref, gkv_ref, gq_ref, gk_ref,
             wq_ref, wk_ref, wv_ref,
             dz_ref, qn_ref, kvn_ref, dqr_ref, dkr_ref, dvr_ref, dgql_ref, dgkv_ref, dgq_ref, dgk_ref):
        first = pl.program_id(0) == 0
        qx, rq = _rms(ql_ref[...], QL)
        qn = (qx * gql_ref[...]).astype(BF16)
        kx, rk = _rms(kv_ref[...], KVL)
        kvn = (kx * gkv_ref[...]).astype(BF16)
        qn_ref[...] = qn
        kvn_ref[...] = kvn
        qraw = _dot(qn, wq_ref[...])
        kraw = _dot(kvn, wk_ref[...])
        kr = kr_ref[...]
        c, sa, sb = c_ref[...], sa_ref[...], sb_ref[...]
        lane = lax.broadcasted_iota(jnp.int32, (tm, HP), 1)
        rope_lanes = (lane >= NOPE) & (lane < QK)
        dkrope = jnp.zeros((tm, HP), F32)
        dgq = jnp.zeros((1, HP), F32)
        dgk = jnp.zeros((1, HP), F32)
        for h in range(HEADS):
            sl = slice(h * HP, (h + 1) * HP)
            xn, r = _rms(qraw[:, sl], QK)
            dx, dg = _rms_bwd(xn, r, gq_ref[...], _rope_t(dq_ref[h], c, sa, sb), QK)
            dqr_ref[:, sl] = dx.astype(BF16)
            dgq = dgq + dg
            xn, r = _rms(kraw[:, sl] + kr, QK)
            dx, dg = _rms_bwd(xn, r, gk_ref[...], _rope_t(dk_ref[h], c, sa, sb), QK)
            dkr_ref[:, sl] = dx.astype(BF16)
            dgk = dgk + dg
            dkrope = dkrope + jnp.where(rope_lanes, dx, 0.0)
            dvr_ref[:, sl] = dv_ref[h].astype(BF16)
        dqn = _dot_nt(dqr_ref[...], wq_ref[...])
        dql, dgql = _rms_bwd(qx, rq, gql_ref[...], dqn, QL)
        dkvn = _dot_nt(dkr_ref[...], wk_ref[...]) + _dot_nt(dvr_ref[...], wv_ref[...])
        dkv, dgkv = _rms_bwd(kx, rk, gkv_ref[...], dkvn, KVL)
        dz_ref[...] = jnp.concatenate([dql, dkv, dkrope], axis=1).astype(BF16)
        _accumulate(dgql_ref, dgql, first)
        _accumulate(dgkv_ref, dgkv, first)
        _accumulate(dgq_ref, dgq, first)
        _accumulate(dgk_ref, dgk, first)

    row = lambda w, j: pl.BlockSpec((tm, w), lambda i: (i, j))
    hspec = pl.BlockSpec((HEADS, tm, HP), lambda i: (0, i, 0))
    sd = lambda w, dt: jax.ShapeDtypeStruct((s, w), dt)
    return pl.pallas_call(
        body, name=name, grid=(s // tm,),
        in_specs=[hspec, hspec, hspec, row(QL, 0), row(KVL, 2), row(HP, 3), row(HP, 0), row(HP, 0), row(HP, 0),
                  _acc((1, QL)), _acc((1, KVL)), _acc((1, HP)), _acc((1, HP)),
                  _acc((QL, HEADS * HP)), _acc((KVL, HEADS * HP)), _acc((KVL, HEADS * HP))],
        out_specs=[row(512, 0), row(QL, 0), row(KVL, 0), row(512, 0), row(512, 0), row(512, 0),
                   _acc((1, QL)), _acc((1, KVL)), _acc((1, HP)), _acc((1, HP))],
        out_shape=[sd(512, BF16), sd(QL, BF16), sd(KVL, BF16), sd(512, BF16), sd(512, BF16), sd(512, BF16),
                   jax.ShapeDtypeStruct((1, QL), F32), jax.ShapeDtypeStruct((1, KVL), F32),
                   jax.ShapeDtypeStruct((1, HP), F32), jax.ShapeDtypeStruct((1, HP), F32)],
        compiler_params=_cp(("arbitrary",), VMEM_LIMIT),
    )(dq, dk, dv, z, z, z, *tabs, gql, gkv, gq, gk, wq, wk, wv)


def _in_proj_bwd(dzm, duv, dp, x, dx1, g, win, name):
    s = x.shape[0]
    tm = _tile(s, 512)

    def body(dzm_ref, duv_ref, dp_ref, x_ref, dx1_ref, g_ref, w_ref, dx_ref, dg_ref):
        dh = _dot_nt(dzm_ref[...], w_ref[:, 0:512]) + _dot_nt(duv_ref[...], w_ref[:, 512:1024]) \
            + _dot_nt(dp_ref[...], w_ref[:, 1024:IN_P])
        xn, r = _rms(x_ref[...], D)
        dxr, dg = _rms_bwd(xn, r, g_ref[...], dh, D)
        dx_ref[...] = dx1_ref[...] + dxr
        _accumulate(dg_ref, dg, pl.program_id(0) == 0)

    row = lambda w: pl.BlockSpec((tm, w), lambda i: (i, 0))
    return pl.pallas_call(
        body, name=name, grid=(s // tm,),
        in_specs=[row(512), row(512), row(POOL), row(D), row(D), _acc((1, D)), _res((D, IN_P))],
        out_specs=[row(D), _acc((1, D))],
        out_shape=[jax.ShapeDtypeStruct((s, D), F32), jax.ShapeDtypeStruct((1, D), F32)],
        compiler_params=_cp(("arbitrary",), VMEM_LIMIT),
    )(dzm, duv, dp, x, dx1, g, win)


def _adamw(w, g0, g1, m, v, name):
    _, r, c = w.shape
    tr = _row_tile(r, 512)
    c1 = 1.0 - B1 ** STEP
    c2 = 1.0 - B2 ** STEP

    def body(w_ref, g0_ref, g1_ref, m_ref, v_ref, g_ref, d_ref, nm_ref, nv_ref):
        gv = jnp.where(pl.program_id(0) == 0, g0_ref[...], g1_ref[...])
        g_ref[0] = gv
        nm = B1 * m_ref[0] + (1.0 - B1) * gv
        nv = B2 * v_ref[0] + (1.0 - B2) * (gv * gv)
        nm_ref[0] = nm
        nv_ref[0] = nv
        d_ref[0] = -LR * ((nm / c1) / (jnp.sqrt(nv / c2) + ADAM_EPS) + WD * w_ref[0])

    spec = pl.BlockSpec((1, tr, c), lambda l, i: (l, i, 0))
    out = jax.ShapeDtypeStruct((DEPTH, r, c), F32)
    return pl.pallas_call(
        body, name=name, grid=(DEPTH, r // tr),
        in_specs=[spec, pl.BlockSpec((tr, c), lambda l, i: (i * (1 - l), 0)), pl.BlockSpec((tr, c), lambda l, i: (i * l, 0)),
                  spec, spec],
        out_specs=[spec] * 4, out_shape=[out] * 4, compiler_params=_cp(("parallel", "parallel")),
    )(w, g0, g1, m, v)


ANY = pl.BlockSpec(memory_space=pl.ANY)


def _place():
    x, y, c = lax.axis_index("x"), lax.axis_index("y"), lax.axis_index("c")
    chips = [(1 - x, y), (x, 1 - y), (1 - x, 1 - y)]
    return x, y, c, chips


def _half_rows(ref, lead, hh, half, align):
    rows = pl.ds(pl.multiple_of(hh * half, align), half)
    return ref.at[rows, :] if lead is None else ref.at[lead, rows, :]


def _row_align(dtype):
    return 16 if dtype == BF16 else 8


def _sems(n):
    return [pltpu.SemaphoreType.DMA((n,)), pltpu.SemaphoreType.DMA((n,)), pltpu.SemaphoreType.DMA((n,))]


def _comm_call(body, ins, out_shapes, nsems, name):
    return pl.pallas_call(
        body, name=name, in_specs=[ANY] * len(ins), out_specs=[ANY] * len(out_shapes), out_shape=out_shapes,
        scratch_shapes=_sems(nsems), compiler_params=pltpu.CompilerParams(has_side_effects=True),
    )(*ins)


def _all_gather_chips(shards, name):
    n = len(shards)
    halves = [a.shape[0] // 2 for a in shards]
    aligns = [_row_align(a.dtype) for a in shards]
    assert all(h % al == 0 for h, al in zip(halves, aligns))

    def body(*refs):
        ins, outs, (send_sems, recv_sems, local_sems) = refs[:n], refs[n:2 * n], refs[2 * n:]
        x, y, c, chips = _place()
        me = 2 * x + y
        sibling = (x, y, 1 - c)

        def copy(sem, src, dst, to):
            return pltpu.make_async_remote_copy(src_ref=src, dst_ref=dst, send_sem=send_sems.at[sem],
                                                recv_sem=recv_sems.at[sem], device_id=to, device_id_type=MESH)

        own, first, passed = [], [], []
        for a in range(n):
            cp = pltpu.make_async_copy(ins[a], outs[a].at[me], local_sems.at[a])
            cp.start()
            own.append(cp)
            my_half = _half_rows(ins[a], None, c, halves[a], aligns[a])
            for j, (cx, cy) in enumerate(chips):
                cp = copy(6 * a + j, my_half, _half_rows(outs[a], me, c, halves[a], aligns[a]), (cx, cy, c))
                cp.start()
                first.append(cp)
        for a in range(n):
            for j, (cx, cy) in enumerate(chips):
                landed = _half_rows(outs[a], 2 * cx + cy, c, halves[a], aligns[a])
                copy(6 * a + j, landed, landed, (cx, cy, c)).wait_recv()
                fwd = copy(6 * a + 3 + j, landed, landed, sibling)
                fwd.start()
                passed.append(fwd)
        for a in range(n):
            for j, (cx, cy) in enumerate(chips):
                other = _half_rows(outs[a], 2 * cx + cy, 1 - c, halves[a], aligns[a])
                copy(6 * a + 3 + j, other, other, sibling).wait_recv()
        for cp in first + passed:
            cp.wait_send()
        for cp in own:
            cp.wait()

    return _comm_call(body, shards, [jax.ShapeDtypeStruct((CHIPS,) + a.shape, a.dtype) for a in shards], 6 * n, name)


def _pair_swap_halves(arrs, name):
    n = len(arrs)
    halves = [a.shape[1] // 2 for a in arrs]

    def body(*refs):
        ins, outs, (send_sems, recv_sems, _) = refs[:n], refs[n:2 * n], refs[2 * n:]
        x, y, c, _ = _place()
        cps = []
        for a in range(n):
            src = ins[a].at[:, pl.ds(pl.multiple_of((1 - c) * halves[a], 8), halves[a]), :]
            cp = pltpu.make_async_remote_copy(src_ref=src, dst_ref=outs[a], send_sem=send_sems.at[a],
                                              recv_sem=recv_sems.at[a], device_id=(x, y, 1 - c), device_id_type=MESH)
            cp.start()
            cps.append(cp)
        for cp in cps:
            cp.wait()

    return _comm_call(body, arrs, [jax.ShapeDtypeStruct((CHIPS, h, a.shape[2]), a.dtype) for a, h in zip(arrs, halves)],
                      n, name)


def _pair_add(full, got, cidx, name):
    _, half, cols = got.shape
    tr = _row_tile(half, 256)
    nt = half // tr

    grid_spec = pltpu.PrefetchScalarGridSpec(
        num_scalar_prefetch=1, grid=(CHIPS, nt),
        in_specs=[pl.BlockSpec((1, tr, cols), lambda k, r, c_ref: (k, c_ref[0] * nt + r, 0)),
                  pl.BlockSpec((1, tr, cols), lambda k, r, c_ref: (k, r, 0))],
        out_specs=pl.BlockSpec((1, tr, cols), lambda k, r, c_ref: (k, r, 0)))

    def body(c_ref, a_ref, b_ref, o_ref):
        o_ref[...] = a_ref[...] + b_ref[...]

    return pl.pallas_call(
        body, name=name, grid_spec=grid_spec, out_shape=jax.ShapeDtypeStruct(got.shape, got.dtype),
        compiler_params=_cp(("parallel", "parallel")),
    )(cidx, full, got)


def _chip_scatter(parts, name):
    n = len(parts)

    def body(*refs):
        ins, outs, (send_sems, recv_sems, local_sems) = refs[:n], refs[n:2 * n], refs[2 * n:]
        x, y, c, chips = _place()
        me = 2 * x + y
        own, sends = [], []
        for a in range(n):
            cp = pltpu.make_async_copy(ins[a].at[me], outs[a].at[me], local_sems.at[a])
            cp.start()
            own.append(cp)
            for j, (cx, cy) in enumerate(chips):
                cp = pltpu.make_async_remote_copy(src_ref=ins[a].at[2 * cx + cy], dst_ref=outs[a].at[me],
                                                  send_sem=send_sems.at[3 * a + j], recv_sem=recv_sems.at[3 * a + j],
                                                  device_id=(cx, cy, c), device_id_type=MESH)
                cp.start()
                sends.append(cp)
        for a in range(n):
            for j, (cx, cy) in enumerate(chips):
                slot = outs[a].at[2 * cx + cy]
                pltpu.make_async_remote_copy(src_ref=slot, dst_ref=slot, send_sem=send_sems.at[3 * a + j],
                                             recv_sem=recv_sems.at[3 * a + j], device_id=(cx, cy, c),
                                             device_id_type=MESH).wait_recv()
        for cp in sends:
            cp.wait_send()
        for cp in own:
            cp.wait()

    return _comm_call(body, parts, [jax.ShapeDtypeStruct(a.shape, a.dtype) for a in parts], 3 * n, name)


def _sum_chips(q, name):
    nk, half, cols = q.shape
    tr = _row_tile(half, 256)

    def body(q_ref, o_ref):
        o_ref[...] = ((q_ref[0] + q_ref[1]) + q_ref[2]) + q_ref[3]

    return pl.pallas_call(
        body, name=name, grid=(half // tr,),
        in_specs=[pl.BlockSpec((nk, tr, cols), lambda r: (0, r, 0))],
        out_specs=pl.BlockSpec((tr, cols), lambda r: (r, 0)),
        out_shape=jax.ShapeDtypeStruct((half, cols), q.dtype), compiler_params=_cp(("parallel",)),
    )(q)


def _pair_join(mine, name):
    n = len(mine)
    halves = [a.shape[0] for a in mine]

    def body(*refs):
        ins, outs, (send_sems, recv_sems, local_sems) = refs[:n], refs[n:2 * n], refs[2 * n:]
        x, y, c, _ = _place()
        cps, own = [], []
        for a in range(n):
            dst = _half_rows(outs[a], None, c, halves[a], 8)
            loc = pltpu.make_async_copy(ins[a], dst, local_sems.at[a])
            loc.start()
            own.append(loc)
            cp = pltpu.make_async_remote_copy(src_ref=ins[a], dst_ref=dst, send_sem=send_sems.at[a], recv_sem=recv_sems.at[a],
                                              device_id=(x, y, 1 - c), device_id_type=MESH)
            cp.start()
            cps.append(cp)
        for cp in cps:
            cp.wait()
        for cp in own:
            cp.wait()

    return _comm_call(body, mine, [jax.ShapeDtypeStruct((2 * h, a.shape[1]), a.dtype) for a, h in zip(mine, halves)],
                      n, name)


HBM = pl.BlockSpec(memory_space=pltpu.HBM)
SEM = pl.BlockSpec(memory_space=pltpu.SEMAPHORE)
DATAFLOW = pltpu.SideEffectType.DATAFLOW_SIDE_EFFECTING


def _remote_copies(pairs, ins, lands, send_sems, recv_sems):
    return [pltpu.make_async_remote_copy(src_ref=src, dst_ref=dst, send_sem=send_sems.at[i], recv_sem=recv_sems.at[i],
                                         device_id=to, device_id_type=MESH)
            for i, (src, dst, to) in enumerate(pairs(ins, lands))]


def _split_start(srcs, land_shapes, ncopies, pairs, name):
    n, m = len(srcs), len(land_shapes)

    def body(*refs):
        ins, lands = refs[:n], refs[n:n + m]
        send_sems, recv_sems, token = refs[n + m], refs[n + m + 1], refs[-1]
        for cp in _remote_copies(pairs, ins, lands, send_sems, recv_sems):
            cp.start()
        token[...] = jnp.zeros_like(token)

    hbm = lambda a: pltpu.with_memory_space_constraint(a, pltpu.HBM)
    lands = [hbm(lax.empty(s.shape, s.dtype)) for s in land_shapes]
    thru = [pltpu.HBM(a.shape, a.dtype) for a in list(srcs) + lands]
    out = pl.pallas_call(
        body, name=name,
        out_shape=(pltpu.SemaphoreType.DMA((ncopies,)), pltpu.SemaphoreType.DMA((ncopies,)), *thru,
                   jax.ShapeDtypeStruct((8, LANES), F32)),
        in_specs=[HBM] * (n + m), out_specs=(SEM, SEM, *[HBM] * (n + m), pl.BlockSpec(memory_space=pltpu.VMEM)),
        input_output_aliases={i: 2 + i for i in range(n + m)},
        compiler_params=pltpu.CompilerParams(has_side_effects=DATAFLOW),
    )(*[hbm(a) for a in srcs], *lands)
    return out[0], out[1], list(out[2:2 + n]), list(out[2 + n:2 + n + m]), out[-1]


def _split_wait(send_sems, recv_sems, srcs, lands, after, pairs, name):
    n, m = len(srcs), len(lands)

    def body(*refs):
        ins, lands_ = refs[:n], refs[n:n + m]
        for cp in _remote_copies(pairs, ins, lands_, refs[n + m], refs[n + m + 1]):
            cp.wait_send()
            cp.wait_recv()

    out = pl.pallas_call(
        body, name=name, out_shape=tuple(pltpu.HBM(a.shape, a.dtype) for a in list(srcs) + list(lands)),
        in_specs=[HBM] * (n + m) + [SEM, SEM, ANY], out_specs=tuple([HBM] * (n + m)),
        input_output_aliases={i: i for i in range(n + m)},
        compiler_params=pltpu.CompilerParams(has_side_effects=DATAFLOW),
    )(*srcs, *lands, send_sems, recv_sems, after)
    return list(out[:n]), list(out[n:])


def _gather_pairs(halves, aligns):
    def pairs(ins, lands):
        x, y, c, chips = _place()
        me = 2 * x + y
        return [(_half_rows(ins[a], None, c, halves[a], aligns[a]), _half_rows(lands[a], me, c, halves[a], aligns[a]),
                 (cx, cy, c)) for a in range(len(ins)) for cx, cy in chips]
    return pairs


def _scatter_pairs(ins, lands):
    x, y, c, chips = _place()
    return [(ins[a].at[2 * cx + cy], lands[a].at[j], (cx, cy, c)) for a in range(len(ins)) for j, (cx, cy) in enumerate(chips)]


def _gather_finish(shards, lands, name):
    n = len(shards)
    halves = [a.shape[0] // 2 for a in shards]
    aligns = [_row_align(a.dtype) for a in shards]

    def body(*refs):
        ins, zones, outs, (send_sems, recv_sems, local_sems) = refs[:n], refs[n:2 * n], refs[2 * n:3 * n], refs[3 * n:]
        del zones
        x, y, c, chips = _place()
        me = 2 * x + y
        own, passed = [], []
        for a in range(n):
            cp = pltpu.make_async_copy(ins[a], outs[a].at[me], local_sems.at[a])
            cp.start()
            own.append(cp)
            for j, (cx, cy) in enumerate(chips):
                landed = _half_rows(outs[a], 2 * cx + cy, c, halves[a], aligns[a])
                cp = pltpu.make_async_remote_copy(src_ref=landed, dst_ref=landed, send_sem=send_sems.at[3 * a + j],
                                                  recv_sem=recv_sems.at[3 * a + j], device_id=(x, y, 1 - c),
                                                  device_id_type=MESH)
                cp.start()
                passed.append(cp)
        for a in range(n):
            for j, (cx, cy) in enumerate(chips):
                other = _half_rows(outs[a], 2 * cx + cy, 1 - c, halves[a], aligns[a])
                pltpu.make_async_remote_copy(src_ref=other, dst_ref=other, send_sem=send_sems.at[3 * a + j],
                                             recv_sem=recv_sems.at[3 * a + j], device_id=(x, y, 1 - c),
                                             device_id_type=MESH).wait_recv()
        for cp in passed:
            cp.wait_send()
        for cp in own:
            cp.wait()

    return pl.pallas_call(
        body, name=name, in_specs=[ANY] * (2 * n), out_specs=[ANY] * n,
        out_shape=[jax.ShapeDtypeStruct(a.shape, a.dtype) for a in lands],
        input_output_aliases={n + i: i for i in range(n)}, scratch_shapes=_sems(3 * n),
        compiler_params=pltpu.CompilerParams(has_side_effects=True),
    )(*shards, *lands)


def _sum_own_and_landed(own, landed, me, name):
    _, half, cols = own.shape
    tr = _row_tile(half, 256)

    grid_spec = pltpu.PrefetchScalarGridSpec(
        num_scalar_prefetch=1, grid=(half // tr,),
        in_specs=[pl.BlockSpec((1, tr, cols), lambda r, me_ref: (me_ref[0], r, 0)),
                  pl.BlockSpec((3, tr, cols), lambda r, me_ref: (0, r, 0))],
        out_specs=pl.BlockSpec((tr, cols), lambda r, me_ref: (r, 0)))

    def body(me_ref, p_ref, q_ref, o_ref):
        o_ref[...] = ((p_ref[0] + q_ref[0]) + q_ref[1]) + q_ref[2]

    return pl.pallas_call(
        body, name=name, grid_spec=grid_spec, out_shape=jax.ShapeDtypeStruct((half, cols), own.dtype),
        compiler_params=_cp(("parallel",)),
    )(me, own, landed)


BIG = [("w_in", (D, IN_W), 1), ("w_q_up", (QL, HEADS * QK), 1), ("w_kv_up", (KVL, HEADS * (NOPE + VH)), 1),
       ("w_out", (D, D), 0), ("w_gate", (D, HID), 1), ("w_up", (D, HID), 1), ("w_down", (HID, D), 0)]
SMALL = [("g_mix_norm", (D,)), ("g_q_lat", (QL,)), ("g_kv_lat", (KVL,)), ("g_q_head", (QK,)), ("g_k_head", (QK,)),
         ("g_sgu_v", (SGU,)), ("w_spatial", (HEADS, CHUNK, CHUNK)), ("b_spatial", (HEADS, CHUNK)),
         ("w_pool", (4, 64, 64)), ("pool_scale", (POOL,)), ("g_out_mla", (512,)), ("g_out_sgu", (SGU,)),
         ("g_out_pool", (POOL,)), ("g_ffn_norm", (D,))]
ORDER = ["g_mix_norm", "w_in", "g_q_lat", "w_q_up", "g_kv_lat", "w_kv_up", "g_q_head", "g_k_head", "g_sgu_v",
         "w_spatial", "b_spatial", "w_pool", "pool_scale", "g_out_mla", "g_out_sgu", "g_out_pool", "w_out",
         "g_ffn_norm", "w_gate", "w_up", "w_down"]
MIX_BIG = ["w_in", "w_q_up", "w_kv_up", "w_out"]
FFN_BIG = ["w_gate", "w_up", "w_down"]
DEPTH = 2
COLS = 1024
SMALL_N = sum(math.prod(s) for _, s in SMALL) * DEPTH
assert SMALL_N % CHIPS == 0
SMALL_ROWS = -(-(SMALL_N // CHIPS) // (16 * COLS)) * 16


def _unsplit_cols(g):
    return g.transpose(1, 0, 2).reshape(g.shape[1], CHIPS * g.shape[2])


def _split_cols(full):
    r, c = full.shape
    return full.reshape(r, CHIPS, c // CHIPS).transpose(1, 0, 2)


def _kernel_weights(g):
    win = _unsplit_cols(g["w_in"])
    zeros = lambda r, c: jnp.zeros((r, c), BF16)
    o2, o3, o4 = QL + KVL, QL + KVL + ROPE, QL + KVL + ROPE + 2 * SGU
    win_p = jnp.concatenate([win[:, :o2], zeros(D, NOPE), win[:, o2:o3], zeros(D, HP - QK), win[:, o3:o4], win[:, o4:]], axis=1)
    wq = _unsplit_cols(g["w_q_up"]).reshape(QL, HEADS, QK)
    wq_p = jnp.pad(wq, ((0, 0), (0, 0), (0, HP - QK))).reshape(QL, HEADS * HP)
    wkv = _unsplit_cols(g["w_kv_up"]).reshape(KVL, HEADS, NOPE + VH)
    wk_p = jnp.pad(wkv[:, :, :NOPE], ((0, 0), (0, 0), (0, HP - NOPE))).reshape(KVL, HEADS * HP)
    wv_p = wkv[:, :, NOPE:].reshape(KVL, HEADS * VH)
    return dict(win=win_p, wq=wq_p, wk=wk_p, wv=wv_p, wout=g["w_out"].reshape(D, D))


def _small_operands(p, l):
    row = lambda v: v.reshape(1, -1)
    pad = lambda v: jnp.pad(v, (0, HP - QK)).reshape(1, HP)
    wpool = p["w_pool"][l]
    wbd = jnp.zeros((POOL, POOL), F32)
    for g in range(4):
        wbd = lax.dynamic_update_slice(wbd, wpool[g], (g * 64, g * 64))
    return dict(
        g_mix=row(p["g_mix_norm"][l]), gql=row(p["g_q_lat"][l]), gkv=row(p["g_kv_lat"][l]),
        gq=pad(p["g_q_head"][l]), gk=pad(p["g_k_head"][l]), gsv=row(p["g_sgu_v"][l]),
        wsp=p["w_spatial"][l], bsp=jnp.repeat(p["b_spatial"][l].T, SGU // HEADS, axis=1),
        wbd=wbd.astype(BF16), psc=row(p["pool_scale"][l]),
        gout=jnp.concatenate([p["g_out_mla"][l], p["g_out_sgu"][l], p["g_out_pool"][l]]).reshape(1, D),
        g_ffn=row(p["g_ffn_norm"][l]))


def _big_grads(g):
    dwin = g["win"]
    o2 = QL + KVL
    gin = jnp.concatenate([dwin[:, :o2], dwin[:, o2 + NOPE:o2 + NOPE + ROPE], dwin[:, 512:]], axis=1)
    gq = g["wq"].reshape(QL, HEADS, HP)[:, :, :QK].reshape(QL, HEADS * QK)
    gk = g["wk"].reshape(KVL, HEADS, HP)[:, :, :NOPE]
    gv = g["wv"].reshape(KVL, HEADS, VH)
    gkv = jnp.concatenate([gk, gv], axis=2).reshape(KVL, HEADS * (NOPE + VH))
    return {"w_in": _split_cols(gin), "w_q_up": _split_cols(gq), "w_kv_up": _split_cols(gkv),
            "w_out": g["wout"].reshape(CHIPS, D // CHIPS, D), "w_gate": g["wg"], "w_up": g["wu"], "w_down": g["wd"]}


def _small_grads(g):
    go = g["gout"].reshape(-1)
    return {"g_mix_norm": g["g_mix"].reshape(-1), "g_q_lat": g["gql"].reshape(-1), "g_kv_lat": g["gkv"].reshape(-1),
            "g_q_head": g["gq"].reshape(-1)[:QK], "g_k_head": g["gk"].reshape(-1)[:QK], "g_sgu_v": g["gsv"].reshape(-1),
            "w_spatial": g["wsp"], "b_spatial": g["bsp"].reshape(CHUNK, HEADS, SGU // HEADS).sum(-1).T,
            "w_pool": jnp.stack([g["wbd"][i * 64:(i + 1) * 64, i * 64:(i + 1) * 64] for i in range(4)]),
            "pool_scale": g["psc"].reshape(-1), "g_out_mla": go[:512], "g_out_sgu": go[512:768],
            "g_out_pool": go[768:], "g_ffn_norm": g["g_ffn"].reshape(-1)}


def _pack_small_grads(small):
    sm = jnp.concatenate([small[l][n].reshape(-1) for l in range(DEPTH) for n, _ in SMALL]).reshape(CHIPS, SMALL_N // CHIPS)
    return jnp.pad(sm, ((0, 0), (0, SMALL_ROWS * COLS - SMALL_N // CHIPS))).reshape(CHIPS, SMALL_ROWS, COLS)


def _unpack_small_grads(gathered):
    flat = gathered.reshape(CHIPS, SMALL_ROWS * COLS)[:, :SMALL_N // CHIPS].reshape(-1)
    out, off = [], 0
    for _ in range(DEPTH):
        layer = {}
        for n, shape in SMALL:
            k = math.prod(shape)
            layer[n] = flat[off:off + k].reshape(shape)
            off += k
        out.append(layer)
    return out


def _layer_fwd(x, tabs, kw, ffn_weights, sp, l):
    t = f"_l{l}"
    z, hb = _in_proj_fwd(x, sp["g_mix"], kw["win"], "in_proj_fwd" + t)
    q, k, v = _mla_prep_fwd(z, tabs, sp["gql"], sp["gkv"], sp["gq"], sp["gk"], kw["wq"], kw["wk"], kw["wv"],
                            "mla_prep_fwd" + t)
    o, lse = _attn_fwd(q, k, v, "attn_fwd" + t)
    m = _pool_win_fwd(z, "pool_win_fwd" + t)
    x1, mix = _mix_out_fwd(o, z, m, x, sp["wsp"], sp["bsp"], sp["wbd"], sp["psc"], sp["gsv"], sp["gout"], kw["wout"],
                           "mix_out_fwd" + t)
    wg, wu, wd = ffn_weights(x1)
    x2, a, b, h2 = _ffn_fwd(x1, sp["g_ffn"], wg, wu, wd, "ffn_fwd" + t)
    saved = dict(x=x, z=z, hb=hb, q=q, k=k, v=v, o=o, lse=lse, m=m, x1=x1, mix=mix, a=a, b=b, h2=h2, wg=wg, wu=wu, wd=wd)
    return x2, saved


def _layer_bwd(dx2, sv, tabs, kw, sp, l, ffn_hook):
    t = f"_l{l}"
    g = {}
    dx1, hid, da, db, g["g_ffn"] = _ffn_bwd(dx2, sv["x1"], sv["a"], sv["b"], sp["g_ffn"], sv["wg"], sv["wu"], sv["wd"],
                                            "ffn_bwd" + t)
    g["wd"] = _wgrad_rows(hid, dx2, "wgrad_down" + t)
    g["wg"] = _wgrad_cols(sv["h2"], da, "wgrad_gate" + t)
    g["wu"] = _wgrad_cols(sv["h2"], db, "wgrad_up" + t)
    gout = sp["gout"] + ffn_hook(g)
    do, delta, duv, dm, g["gout"], g["gsv"], g["psc"], g["wsp"], g["bsp"], g["wbd"] = _mix_out_bwd(
        dx1, sv["o"], sv["z"], sv["m"], sp["wsp"], sp["bsp"], sp["wbd"], sp["psc"], sp["gsv"], gout, kw["wout"],
        "mix_out_bwd" + t)
    g["wout"] = _wgrad(sv["mix"], dx1, "wgrad_out" + t)
    dp = _pool_win_bwd(dm, "pool_win_bwd" + t)
    dq, dk, dv = _attn_bwd(sv["q"], sv["k"], sv["v"], do, sv["lse"], delta, "attn_bwd" + t)
    dzm, qn, kvn, dqr, dkr, dvr, g["gql"], g["gkv"], g["gq"], g["gk"] = _mla_prep_bwd(
        dq, dk, dv, sv["z"], tabs, sp["gql"], sp["gkv"], sp["gq"], sp["gk"], kw["wq"], kw["wk"], kw["wv"],
        "mla_prep_bwd" + t)
    g["wq"] = _wgrad(qn, dqr, "wgrad_q_up" + t)
    g["wk"] = _wgrad(kvn, dkr, "wgrad_k_up" + t)
    g["wv"] = _wgrad(kvn, dvr, "wgrad_v_up" + t)
    dx, g["g_mix"] = _in_proj_bwd(dzm, duv, dp, sv["x"], dx1, sp["g_mix"], kw["win"], "in_proj_bwd" + t)
    g["win"] = jnp.concatenate([_wgrad(sv["hb"], dzm, "wgrad_in_a" + t), _wgrad(sv["hb"], duv, "wgrad_in_b" + t),
                                _wgrad(sv["hb"], dp, "wgrad_in_c" + t)], axis=1)
    return dx, g


def _rope_inv_freq():
    half = ROPE // 2
    inv = 1.0 / (ROPE_THETA ** (jnp.arange(half, dtype=F32) / half))
    return jnp.concatenate([jnp.zeros((NOPE,), F32), inv, inv, jnp.zeros((HP - QK,), F32)]).reshape(1, HP)


def kernel(x, positions, g_mix_norm, w_in, g_q_lat, w_q_up, g_kv_lat, w_kv_up, g_q_head, g_k_head, g_sgu_v, w_spatial, b_spatial, w_pool, pool_scale, g_out_mla, g_out_sgu, g_out_pool, w_out, g_ffn_norm, w_gate, w_up, w_down, loss_target, m_g_mix_norm, m_w_in, m_g_q_lat, m_w_q_up, m_g_kv_lat, m_w_kv_up, m_g_q_head, m_g_k_head, m_g_sgu_v, m_w_spatial, m_b_spatial, m_w_pool, m_pool_scale, m_g_out_mla, m_g_out_sgu, m_g_out_pool, m_w_out, m_g_ffn_norm, m_w_gate, m_w_up, m_w_down, v_g_mix_norm, v_w_in, v_g_q_lat, v_w_q_up, v_g_kv_lat, v_w_kv_up, v_g_q_head, v_g_k_head, v_g_sgu_v, v_w_spatial, v_b_spatial, v_w_pool, v_pool_scale, v_g_out_mla, v_g_out_sgu, v_g_out_pool, v_w_out, v_g_ffn_norm, v_w_gate, v_w_up, v_w_down):
    given = dict(locals())
    p = {n: given[n] for n in ORDER}
    seq = x.shape[1]
    cidx = lax.axis_index("c").astype(jnp.int32).reshape(1)
    me = (2 * lax.axis_index("x") + lax.axis_index("y")).astype(jnp.int32).reshape(1)
    shards = lambda names: [p[n][l].astype(BF16) for l, n in names]
    zero11 = lambda token: token[:1, :1]

    names_0a = [(0, n) for n in MIX_BIG]
    names_0b = [(0, n) for n in FFN_BIG]
    names_1 = [(1, n) for n, _, _ in BIG]
    got_0a = dict(zip(MIX_BIG, _all_gather_chips(shards(names_0a), "all_gather_w0a")))
    started = {}
    for tag, names in (("w0b", names_0b), ("w1", names_1)):
        sh = shards(names)
        pairs = _gather_pairs([a.shape[0] // 2 for a in sh], [_row_align(a.dtype) for a in sh])
        lands = [jax.ShapeDtypeStruct((CHIPS,) + a.shape, a.dtype) for a in sh]
        started[tag] = (sh, pairs) + _split_start(sh, lands, 3 * len(sh), pairs, "gather_start_" + tag)

    def arrived(tag, after):
        _, pairs, send, recv, srcs, lands, _ = started[tag]
        srcs, lands = _split_wait(send, recv, srcs, lands, after, pairs, "gather_wait_" + tag)
        return _gather_finish(srcs, lands, "gather_finish_" + tag)

    layer1 = {}

    def mix_weights(l, h):
        if l == 0:
            return got_0a
        layer1.update(zip([n for _, n in names_1], arrived("w1", h)))
        return layer1

    def ffn_weights(l, x1):
        return arrived("w0b", x1) if l == 0 else [layer1[n] for n in FFN_BIG]

    reducing, last = {}, {}

    def reduce_start(tag, arrs):
        theirs = _pair_swap_halves(arrs, "grad_pair_swap_" + tag)
        pair = [_pair_add(a, t, cidx, f"grad_pair_add_{tag}_{i}") for i, (a, t) in enumerate(zip(arrs, theirs))]
        lands = [jax.ShapeDtypeStruct((3,) + a.shape[1:], a.dtype) for a in pair]
        reducing[tag] = _split_start(pair, lands, 3 * len(pair), _scatter_pairs, "grad_scatter_start_" + tag)
        return zero11(reducing[tag][4])

    def reduce_finish(tag, after):
        send, recv, srcs, lands, _ = reducing[tag]
        srcs, lands = _split_wait(send, recv, srcs, lands, after, _scatter_pairs, "grad_scatter_wait_" + tag)
        return [_sum_own_and_landed(a, q, me, f"grad_sum_{tag}_{i}") for i, (a, q) in enumerate(zip(srcs, lands))]

    def ffn_hook(l, g):
        if l == 1:
            return jnp.zeros((1, 1), F32)
        return reduce_start("g0b", [g["wg"], g["wu"], g["wd"]])

    def layer_hook(l, big, small):
        last[l] = (big, small)
        if l == 1:
            return reduce_start("g1", [big[n] for n, _, _ in BIG])
        return None

    entry = zero11(started["w0b"][6]) + zero11(started["w1"][6])
    loss_part, dx = _step(x.reshape(seq, D), positions.reshape(seq, 1), loss_target.reshape(seq, D), p, entry,
                          mix_weights, ffn_weights, ffn_hook, layer_hook)
    loss = lax.psum(loss_part, ("x", "y", "c"))

    rest = [last[0][0][n] for n in MIX_BIG] + [_pack_small_grads([last[l][1] for l in range(DEPTH)])]
    theirs = _pair_swap_halves(rest, "grad_pair_swap_g0a")
    pair = [_pair_add(a, t, cidx, f"grad_pair_add_g0a_{i}") for i, (a, t) in enumerate(zip(rest, theirs))]
    sums_0a = [_sum_chips(q, f"grad_chip_sum_g0a_{i}") for i, q in enumerate(_chip_scatter(pair, "grad_chip_scatter_g0a"))]
    sums_1 = reduce_finish("g1", dx)
    sums_0b = reduce_finish("g0b", dx)
    order = names_1 + names_0b + names_0a + ["small"]
    sums = dict(zip(order, _pair_join(sums_1 + sums_0b + sums_0a, "grad_pair_join")))
    gsmall = _unpack_small_grads(_all_gather_chips([sums["small"]], "all_gather_small_grads")[0])
    grads = {n: [sums[(l, n)] for l in range(DEPTH)] for n, _, _ in BIG}
    grads.update({n: [gsmall[l][n] for l in range(DEPTH)] for n, _ in SMALL})

    out = {}
    for n in ORDER:
        w = p[n]
        three_d = (DEPTH, -1, w.shape[-1])
        two_d = three_d[1:]
        res = _adamw(w.reshape(three_d), grads[n][0].reshape(two_d), grads[n][1].reshape(two_d),
                     given["m_" + n].reshape(three_d), given["v_" + n].reshape(three_d), "adamw_" + n)
        out[n] = [r.reshape(w.shape) for r in res]
    return (loss, dx.reshape(x.shape), *[out[n][i] for i in range(4) for n in ORDER])


def _step(xs, pos, tgt, p, entry, mix_weights, ffn_weights, ffn_hook, layer_hook):
    sps = [_small_operands(p, l) for l in range(DEPTH)]
    sps[0]["g_mix"] = sps[0]["g_mix"] + entry
    tabs = _rope_tables(pos, _rope_inv_freq())
    saved, h = [], xs
    for l in range(DEPTH):
        kw = _kernel_weights(mix_weights(l, h))
        h, sv = _layer_fwd(h, tabs, kw, functools.partial(ffn_weights, l), sps[l], l)
        saved.append(dict(sv, kw=kw))
    dy, lpart = _loss_grad(h, tgt)
    for l in reversed(range(DEPTH)):
        dy, g = _layer_bwd(dy, saved[l], tabs, saved[l]["kw"], sps[l], l, functools.partial(ffn_hook, l))
        zero = layer_hook(l, _big_grads(g), _small_grads(g))
        if zero is not None and l > 0:
            sps[l - 1]["g_ffn"] = sps[l - 1]["g_ffn"] + zero
    return 0.5 / D * jnp.sum(lpart), dy
```

```python
import functools
import math

import jax
import jax.numpy as jnp
from jax import lax
from jax.experimental import pallas as pl
from jax.experimental.pallas import tpu as pltpu

F32 = jnp.float32
BF16 = jnp.bfloat16
MESH = pl.DeviceIdType.MESH

D = 1024
HEADS = 4
QK = 96
NOPE = 64
ROPE = 32
VH = 128
HP = 128
QL = 256
KVL = 128
SGU = 256
POOL = 256
CHUNK = 128
HID = 2816
CHIPS = 4
SH = HID // CHIPS
IN_W = 1184
IN_P = 1280
EPS = 1e-6
ROPE_THETA = 10000.0
SCALE = 1.0 / math.sqrt(QK)
LOG2E = 1.4426950408889634
EXP2_C = SCALE * LOG2E
ATT_SPLIT = 2
ATT_WIDE = 4
NEG = -1e30
HALO = 16

LR, B1, B2, ADAM_EPS, WD, STEP = 0.001, 0.9, 0.999, 1e-08, 0.01, 10

VMEM_LIMIT = 56 * 1024 * 1024
LANES = 128


def _cp(sem, vmem=None):
    return pltpu.CompilerParams(dimension_semantics=sem, vmem_limit_bytes=vmem)


def _res(shape):
    nd = len(shape)
    return pl.BlockSpec(shape, lambda *_: (0,) * nd, pipeline_mode=pl.Buffered(1))


def _acc(shape):
    nd = len(shape)
    return pl.BlockSpec(shape, lambda *_: (0,) * nd)


def _dot(a, b):
    return jnp.dot(a, b, preferred_element_type=F32)


def _dot_nt(a, b):
    return lax.dot_general(a, b, (((1,), (1,)), ((), ())), preferred_element_type=F32)


def _dot_tn(a, b):
    return lax.dot_general(a, b, (((0,), (0,)), ((), ())), preferred_element_type=F32)


def _rms(x, n):
    r = lax.rsqrt(jnp.sum(x * x, axis=-1, keepdims=True) * (1.0 / n) + EPS)
    return x * r, r


def _rms_bwd(xn, r, g, dy, n):
    dn = dy * g
    dx = r * (dn - xn * (jnp.sum(dn * xn, axis=-1, keepdims=True) * (1.0 / n)))
    return dx, jnp.sum(dy * xn, axis=0, keepdims=True)


def _accumulate(ref, val, first):
    @pl.when(first)
    def _():
        ref[...] = val

    @pl.when(jnp.logical_not(first))
    def _():
        ref[...] += val


def _accumulate0(ref, val, first):
    @pl.when(first)
    def _():
        ref[0] = val

    @pl.when(jnp.logical_not(first))
    def _():
        ref[0] += val


def _tile(s, t):
    return min(s, t)


def _row_tile(r, cap):
    if r <= cap:
        return r
    return max(t for t in range(8, cap + 1, 8) if r % t == 0)


def _rope_tables(pos, invf):
    s = pos.shape[0]
    tm = _tile(s, 1024)

    def body(pos_ref, invf_ref, c_ref, sa_ref, sb_ref):
        ang = pos_ref[...].astype(F32) * invf_ref[...]
        c, sn = jnp.cos(ang), jnp.sin(ang)
        lane = lax.broadcasted_iota(jnp.int32, ang.shape, 1)
        first = (lane >= NOPE) & (lane < NOPE + ROPE // 2)
        second = (lane >= NOPE + ROPE // 2) & (lane < QK)
        c_ref[...] = jnp.where(first | second, c, 1.0)
        sa_ref[...] = jnp.where(first, -sn, 0.0)
        sb_ref[...] = jnp.where(second, sn, 0.0)

    out = jax.ShapeDtypeStruct((s, HP), F32)
    return pl.pallas_call(
        body, name="rope_tables", grid=(s // tm,),
        in_specs=[pl.BlockSpec((tm, 1), lambda i: (i, 0)), _acc((1, HP))],
        out_specs=[pl.BlockSpec((tm, HP), lambda i: (i, 0))] * 3,
        out_shape=[out] * 3, compiler_params=_cp(("parallel",)),
    )(pos, invf)


def _rope(x, c, sa, sb):
    return x * c + pltpu.roll(x, HP - ROPE // 2, 1) * sa + pltpu.roll(x, ROPE // 2, 1) * sb


def _rope_t(d, c, sa, sb):
    return d * c + pltpu.roll(d * sa, ROPE // 2, 1) + pltpu.roll(d * sb, HP - ROPE // 2, 1)


def _in_proj_fwd(x, g, w, name):
    s = x.shape[0]
    tm = _tile(s, 512)

    def body(x_ref, g_ref, w_ref, z_ref, h_ref):
        xn, _ = _rms(x_ref[...], D)
        h = (xn * g_ref[...]).astype(BF16)
        h_ref[...] = h
        z_ref[...] = _dot(h, w_ref[...])

    return pl.pallas_call(
        body, name=name, grid=(s // tm,),
        in_specs=[pl.BlockSpec((tm, D), lambda i: (i, 0)), _acc((1, D)), _res((D, IN_P))],
        out_specs=[pl.BlockSpec((tm, IN_P), lambda i: (i, 0)), pl.BlockSpec((tm, D), lambda i: (i, 0))],
        out_shape=[jax.ShapeDtypeStruct((s, IN_P), F32), jax.ShapeDtypeStruct((s, D), BF16)],
        compiler_params=_cp(("parallel",), VMEM_LIMIT),
    )(x, g, w)


def _mla_prep_fwd(z, tabs, gql, gkv, gq, gk, wq, wk, wv, name):
    s = z.shape[0]
    tm = _tile(s, 512)

    def body(ql_ref, kv_ref, kr_ref, c_ref, sa_ref, sb_ref, gql_ref, gkv_ref, gq_ref, gk_ref,
             wq_ref, wk_ref, wv_ref, q_out, k_out, v_out):
        qn = (_rms(ql_ref[...], QL)[0] * gql_ref[...]).astype(BF16)
        kvn = (_rms(kv_ref[...], KVL)[0] * gkv_ref[...]).astype(BF16)
        qraw = _dot(qn, wq_ref[...])
        kraw = _dot(kvn, wk_ref[...])
        vraw = _dot(kvn, wv_ref[...])
        kr = kr_ref[...]
        c, sa, sb = c_ref[...], sa_ref[...], sb_ref[...]
        for h in range(HEADS):
            sl = slice(h * HP, (h + 1) * HP)
            xq = _rms(qraw[:, sl], QK)[0] * gq_ref[...]
            q_out[h] = _rope(xq, c, sa, sb).astype(BF16)
            xk = _rms(kraw[:, sl] + kr, QK)[0] * gk_ref[...]
            k_out[h] = _rope(xk, c, sa, sb).astype(BF16)
            v_out[h] = vraw[:, sl].astype(BF16)

    row = lambda w, j: pl.BlockSpec((tm, w), lambda i: (i, j))
    hspec = pl.BlockSpec((HEADS, tm, HP), lambda i: (0, i, 0))
    hshape = jax.ShapeDtypeStruct((HEADS, s, HP), BF16)
    return pl.pallas_call(
        body, name=name, grid=(s // tm,),
        in_specs=[row(QL, 0), row(KVL, 2), row(HP, 3), row(HP, 0), row(HP, 0), row(HP, 0),
                  _acc((1, QL)), _acc((1, KVL)), _acc((1, HP)), _acc((1, HP)),
                  _acc((QL, HEADS * HP)), _acc((KVL, HEADS * HP)), _acc((KVL, HEADS * HP))],
        out_specs=[hspec] * 3, out_shape=[hshape] * 3,
        compiler_params=_cp(("parallel",)),
    )(z, z, z, *tabs, gql, gkv, gq, gk, wq, wk, wv)


def _causal_mask(s, row0):
    row = lax.broadcasted_iota(jnp.int32, s.shape, 0) + row0
    col = lax.broadcasted_iota(jnp.int32, s.shape, 1)
    return jnp.where(col <= row, s, NEG)


def _attn_fwd(q, k, v, name):
    s = q.shape[1]
    tq = _tile(s, 512)
    wide = ATT_WIDE * tq if s % (ATT_WIDE * tq) == 0 else tq
    rh = tq // ATT_SPLIT

    def body(q_ref, k_ref, v_ref, o_ref, lse_ref):
        i = pl.program_id(1)

        def blk(off, tk, carry, masked):
            off = pl.multiple_of(off, tq)
            kj = k_ref[0, pl.ds(off, tk), :]
            vj = v_ref[0, pl.ds(off, tk), :]
            out = []
            scs = [_dot_nt(q_ref[0, g * rh:(g + 1) * rh, :], kj) for g in range(ATT_SPLIT)]
            for g, (m, l, acc) in enumerate(carry):
                sc = scs[g]
                if masked:
                    sc = _causal_mask(sc, g * rh)
                m_new = jnp.maximum(m, jnp.max(sc, axis=-1, keepdims=True))
                p = jnp.exp2((sc - m_new) * EXP2_C)
                alpha = jnp.exp2((m - m_new) * EXP2_C)
                l = alpha * l + jnp.sum(p, axis=-1, keepdims=True)
                acc = alpha * acc + _dot(p.astype(BF16), vj)
                out.append((m_new, l, acc))
            return tuple(out)

        one = (jnp.full((rh, 1), NEG, F32), jnp.zeros((rh, 1), F32), jnp.zeros((rh, VH), F32))
        nwide = (i * tq) // wide
        carry = lax.fori_loop(0, nwide, lambda j, c: blk(j * wide, wide, c, False), (one,) * ATT_SPLIT)
        carry = lax.fori_loop(nwide * (wide // tq), i, lambda j, c: blk(j * tq, tq, c, False), carry)
        carry = blk(i * tq, tq, carry, True)
        for g, (m, l, acc) in enumerate(carry):
            o_ref[g * rh:(g + 1) * rh, :] = acc / l
            lse_ref[0, g * rh:(g + 1) * rh, :] = jnp.broadcast_to(m * EXP2_C + jnp.log(l) * LOG2E, (rh, LANES))

    return pl.pallas_call(
        body, name=name, grid=(HEADS, s // tq),
        in_specs=[pl.BlockSpec((1, tq, HP), lambda h, i: (h, i, 0)),
                  pl.BlockSpec((1, s, HP), lambda h, i: (h, 0, 0)),
                  pl.BlockSpec((1, s, HP), lambda h, i: (h, 0, 0))],
        out_specs=[pl.BlockSpec((tq, VH), lambda h, i: (i, h)),
                   pl.BlockSpec((1, tq, LANES), lambda h, i: (h, i, 0))],
        out_shape=[jax.ShapeDtypeStruct((s, HEADS * VH), F32), jax.ShapeDtypeStruct((HEADS, s, LANES), F32)],
        compiler_params=_cp(("parallel", "arbitrary"), VMEM_LIMIT),
    )(q, k, v)


def _lane_group(shape, j):
    return (lax.broadcasted_iota(jnp.int32, shape, 1) + j * LANES) // (POOL // 4)


def _pool_win_fwd(z, name):
    s = z.shape[0]
    ch = _tile(s, 512)
    col0 = (IN_P - POOL) // LANES

    def body(p_ref, m_ref):
        j = pl.program_id(0)

        def chunk(r, _):
            off = pl.multiple_of(r * ch, ch)
            cur = p_ref[pl.ds(off, ch), :]
            hoff = pl.multiple_of(jnp.maximum(off - HALO, 0), 8)
            halo = jnp.where(r > 0, p_ref[pl.ds(hoff, HALO), :], 0.0)
            x = jnp.concatenate([halo, cur], axis=0)
            s2 = x + pltpu.roll(x, 1, 0)
            s4 = s2 + pltpu.roll(s2, 2, 0)
            s8 = s4 + pltpu.roll(s4, 4, 0)
            s16 = s8 + pltpu.roll(s8, 8, 0)
            grp = _lane_group((ch, LANES), j)
            sel = jnp.where(grp == 0, s2[HALO:], jnp.where(grp == 1, s4[HALO:], jnp.where(grp == 2, s8[HALO:], s16[HALO:])))
            t1 = (lax.broadcasted_iota(jnp.int32, (ch, LANES), 0) + off + 1).astype(F32)
            win = jnp.where(grp == 0, 2.0, jnp.where(grp == 1, 4.0, jnp.where(grp == 2, 8.0, 16.0)))
            m_ref[pl.ds(off, ch), :] = sel / jnp.minimum(t1, win) - cur
            return 0

        lax.fori_loop(0, s // ch, chunk, 0)

    return pl.pallas_call(
        body, name=name, grid=(POOL // LANES,),
        in_specs=[pl.BlockSpec((s, LANES), lambda j: (0, col0 + j))],
        out_specs=pl.BlockSpec((s, LANES), lambda j: (0, j)),
        out_shape=jax.ShapeDtypeStruct((s, POOL), F32),
        compiler_params=_cp(("parallel",), VMEM_LIMIT),
    )(z)


def _pool_win_bwd(dm, name):
    s = dm.shape[0]
    ch = _tile(s, 512)
    n = s // ch

    def body(dm_ref, dp_ref):
        j = pl.program_id(0)

        def chunk(r, _):
            off = pl.multiple_of(r * ch, ch)
            grp = _lane_group((ch + HALO, LANES), j)
            win = jnp.where(grp == 0, 2.0, jnp.where(grp == 1, 4.0, jnp.where(grp == 2, 8.0, 16.0)))
            cur = dm_ref[pl.ds(off, ch), :]
            hoff = pl.multiple_of(jnp.minimum(off + ch, s - HALO), 8)
            halo = jnp.where(r < n - 1, dm_ref[pl.ds(hoff, HALO), :], 0.0)
            x = jnp.concatenate([cur, halo], axis=0)
            t1 = (lax.broadcasted_iota(jnp.int32, (ch + HALO, LANES), 0) + off + 1).astype(F32)
            e = x / jnp.minimum(t1, win)
            tot = ch + HALO
            r2 = e + pltpu.roll(e, tot - 1, 0)
            r4 = r2 + pltpu.roll(r2, tot - 2, 0)
            r8 = r4 + pltpu.roll(r4, tot - 4, 0)
            r16 = r8 + pltpu.roll(r8, tot - 8, 0)
            g = grp[:ch]
            sel = jnp.where(g == 0, r2[:ch], jnp.where(g == 1, r4[:ch], jnp.where(g == 2, r8[:ch], r16[:ch])))
            dp_ref[pl.ds(off, ch), :] = (sel - cur).astype(BF16)
            return 0

        lax.fori_loop(0, n, chunk, 0)

    return pl.pallas_call(
        body, name=name, grid=(POOL // LANES,),
        in_specs=[pl.BlockSpec((s, LANES), lambda j: (0, j))],
        out_specs=pl.BlockSpec((s, LANES), lambda j: (0, j)),
        out_shape=jax.ShapeDtypeStruct((s, POOL), BF16),
        compiler_params=_cp(("parallel",), VMEM_LIMIT),
    )(dm)


def _head_mask(h):
    lane = lax.broadcasted_iota(jnp.int32, (CHUNK, SGU), 1)
    return (lane // (SGU // HEADS)) == h


def _tril(upper=False):
    row = lax.broadcasted_iota(jnp.int32, (CHUNK, CHUNK), 0)
    col = lax.broadcasted_iota(jnp.int32, (CHUNK, CHUNK), 1)
    return col >= row if upper else col <= row


def _sgu_gate(vn, wsp, bsp):
    out = []
    for cidx in range(vn.shape[0] // CHUNK):
        vc = vn[cidx * CHUNK:(cidx + 1) * CHUNK]
        zc = bsp
        for h in range(HEADS):
            zc = zc + jnp.where(_head_mask(h), _dot(wsp[h], vc), 0.0)
        out.append(zc)
    return jnp.concatenate(out, axis=0)


def _mix_out_fwd(o, z, m, x, wsp, bsp, wbd, psc, gsv, gout, wout, name):
    s = x.shape[0]
    tm = _tile(s, 512)

    def body(o_ref, uv_ref, m_ref, x_ref, wsp_ref, bsp_ref, wbd_ref, psc_ref, gsv_ref, gout_ref, wout_ref,
             x1_ref, mix_ref):
        g = gout_ref[...]
        an = _rms(o_ref[...], HEADS * VH)[0] * g[:, :512]
        uv = uv_ref[...]
        u, v = uv[:, :SGU], uv[:, SGU:]
        vn = (_rms(v, SGU)[0] * gsv_ref[...]).astype(BF16)
        tri = _tril()
        wsp_m = [jnp.where(tri, wsp_ref[h], 0.0).astype(BF16) for h in range(HEADS)]
        gm = u * _sgu_gate(vn, wsp_m, bsp_ref[...])
        gn = _rms(gm, SGU)[0] * g[:, 512:768]
        po = _dot(m_ref[...].astype(BF16), wbd_ref[...]) * psc_ref[...]
        pn = _rms(po, POOL)[0] * g[:, 768:]
        mix = jnp.concatenate([an, gn, pn], axis=1).astype(BF16)
        mix_ref[...] = mix
        x1_ref[...] = x_ref[...] + _dot(mix, wout_ref[...])

    row = lambda w, j: pl.BlockSpec((tm, w), lambda i: (i, j))
    return pl.pallas_call(
        body, name=name, grid=(s // tm,),
        in_specs=[row(512, 0), row(512, 1), row(POOL, 0), row(D, 0),
                  _acc((HEADS, CHUNK, CHUNK)), _acc((CHUNK, SGU)), _acc((POOL, POOL)), _acc((1, POOL)),
                  _acc((1, SGU)), _acc((1, D)), _res((D, D))],
        out_specs=[row(D, 0), row(D, 0)],
        out_shape=[jax.ShapeDtypeStruct((s, D), F32), jax.ShapeDtypeStruct((s, D), BF16)],
        compiler_params=_cp(("parallel",), VMEM_LIMIT),
    )(o, z, m, x, wsp, bsp, wbd, psc, gsv, gout, wout)


def _ffn_fwd(x1, g, wg, wu, wd, name):
    s = x1.shape[0]
    tm = _tile(s, 256)

    def body(x_ref, g_ref, wg_ref, wu_ref, wd_ref, x2_ref, a_ref, b_ref, h_ref):
        x = x_ref[...]
        h = (_rms(x, D)[0] * g_ref[...]).astype(BF16)
        h_ref[...] = h
        acc = jnp.zeros((tm, D), F32)
        for k in range(CHIPS):
            a = _dot(h, wg_ref[k])
            b = _dot(h, wu_ref[k])
            a_ref[k] = a
            b_ref[k] = b
            acc = acc + _dot((a * jax.nn.sigmoid(a) * b).astype(BF16), wd_ref[k])
        x2_ref[...] = x + acc

    row = lambda w: pl.BlockSpec((tm, w), lambda i: (i, 0))
    hrow = pl.BlockSpec((CHIPS, tm, SH), lambda i: (0, i, 0))
    hshape = jax.ShapeDtypeStruct((CHIPS, s, SH), F32)
    return pl.pallas_call(
        body, name=name, grid=(s // tm,),
        in_specs=[row(D), _acc((1, D)), _res((CHIPS, D, SH)), _res((CHIPS, D, SH)), _res((CHIPS, SH, D))],
        out_specs=[row(D), hrow, hrow, row(D)],
        out_shape=[jax.ShapeDtypeStruct((s, D), F32), hshape, hshape, jax.ShapeDtypeStruct((s, D), BF16)],
        compiler_params=_cp(("parallel",), VMEM_LIMIT),
    )(x1, g, wg, wu, wd)


def _loss_grad(y, tgt):
    s = y.shape[0]
    tm = _tile(s, 512)

    def body(y_ref, t_ref, dy_ref, l_ref):
        e = y_ref[...] - t_ref[...]
        dy_ref[...] = e * (1.0 / D)
        sq = jnp.sum(e * e, axis=0, keepdims=True)
        part = sq[:, :LANES]
        for c in range(1, D // LANES):
            part = part + sq[:, c * LANES:(c + 1) * LANES]
        _accumulate(l_ref, part, pl.program_id(0) == 0)

    row = pl.BlockSpec((tm, D), lambda i: (i, 0))
    return pl.pallas_call(
        body, name="loss_grad", grid=(s // tm,),
        in_specs=[row, row], out_specs=[row, _acc((1, LANES))],
        out_shape=[jax.ShapeDtypeStruct((s, D), F32), jax.ShapeDtypeStruct((1, LANES), F32)],
        compiler_params=_cp(("arbitrary",)),
    )(y, tgt)


def _wgrad(a, b, name):
    s, k = a.shape
    n = b.shape[1]
    half = lambda v: v if v <= 1408 else v // 2
    kb, nb, tt = half(k), half(n), _tile(s, 1024)

    def body(a_ref, b_ref, o_ref):
        _accumulate(o_ref, _dot_tn(a_ref[...].astype(BF16), b_ref[...].astype(BF16)), pl.program_id(2) == 0)

    return pl.pallas_call(
        body, name=name, grid=(k // kb, n // nb, s // tt),
        in_specs=[pl.BlockSpec((tt, kb), lambda i, j, t: (t, i)), pl.BlockSpec((tt, nb), lambda i, j, t: (t, j))],
        out_specs=pl.BlockSpec((kb, nb), lambda i, j, t: (i, j)),
        out_shape=jax.ShapeDtypeStruct((k, n), F32),
        compiler_params=_cp(("parallel", "parallel", "arbitrary"), VMEM_LIMIT),
    )(a, b)


def _wgrad_cols(a, b, name):
    s, k = a.shape
    n = b.shape[2]
    tt = _tile(s, 1024)

    def body(a_ref, b_ref, o_ref):
        _accumulate0(o_ref, _dot_tn(a_ref[...].astype(BF16), b_ref[0].astype(BF16)), pl.program_id(1) == 0)

    return pl.pallas_call(
        body, name=name, grid=(CHIPS, s // tt),
        in_specs=[pl.BlockSpec((tt, k), lambda c, t: (t, 0)), pl.BlockSpec((1, tt, n), lambda c, t: (c, t, 0))],
        out_specs=pl.BlockSpec((1, k, n), lambda c, t: (c, 0, 0)),
        out_shape=jax.ShapeDtypeStruct((CHIPS, k, n), F32),
        compiler_params=_cp(("parallel", "arbitrary"), VMEM_LIMIT),
    )(a, b)


def _wgrad_rows(a, b, name):
    s, n = a.shape[1:]
    nn = b.shape[1]
    tt = _tile(s, 1024)

    def body(a_ref, b_ref, o_ref):
        _accumulate0(o_ref, _dot_tn(a_ref[0].astype(BF16), b_ref[...].astype(BF16)), pl.program_id(1) == 0)

    return pl.pallas_call(
        body, name=name, grid=(CHIPS, s // tt),
        in_specs=[pl.BlockSpec((1, tt, n), lambda c, t: (c, t, 0)), pl.BlockSpec((tt, nn), lambda c, t: (t, 0))],
        out_specs=pl.BlockSpec((1, n, nn), lambda c, t: (c, 0, 0)),
        out_shape=jax.ShapeDtypeStruct((CHIPS, n, nn), F32),
        compiler_params=_cp(("parallel", "arbitrary"), VMEM_LIMIT),
    )(a, b)


def _ffn_bwd(dx2, x1, a, b, g, wg, wu, wd, name):
    s = x1.shape[0]
    tm = _tile(s, 256)

    def body(dx2_ref, x_ref, a_ref, b_ref, g_ref, wg_ref, wu_ref, wd_ref,
             dx1_ref, hid_ref, da_ref, db_ref, dg_ref):
        dx2 = dx2_ref[...]
        dyb = dx2.astype(BF16)
        dh = jnp.zeros((tm, D), F32)
        for k in range(CHIPS):
            av, bv = a_ref[k], b_ref[k]
            dhid = _dot_nt(dyb, wd_ref[k])
            sig = jax.nn.sigmoid(av)
            sa = av * sig
            hid_ref[k] = (sa * bv).astype(BF16)
            dbv = (dhid * sa).astype(BF16)
            dav = (dhid * bv * (sig * (1.0 + av * (1.0 - sig)))).astype(BF16)
            db_ref[k] = dbv
            da_ref[k] = dav
            dh = dh + _dot_nt(dav, wg_ref[k]) + _dot_nt(dbv, wu_ref[k])
        xn, r = _rms(x_ref[...], D)
        dxr, dg = _rms_bwd(xn, r, g_ref[...], dh, D)
        dx1_ref[...] = dx2 + dxr
        _accumulate(dg_ref, dg, pl.program_id(0) == 0)

    row = lambda w: pl.BlockSpec((tm, w), lambda i: (i, 0))
    hrow = pl.BlockSpec((CHIPS, tm, SH), lambda i: (0, i, 0))
    hid = jax.ShapeDtypeStruct((CHIPS, s, SH), BF16)
    return pl.pallas_call(
        body, name=name, grid=(s // tm,),
        in_specs=[row(D), row(D), hrow, hrow, _acc((1, D)), _res((CHIPS, D, SH)), _res((CHIPS, D, SH)),
                  _res((CHIPS, SH, D))],
        out_specs=[row(D), hrow, hrow, hrow, _acc((1, D))],
        out_shape=[jax.ShapeDtypeStruct((s, D), F32), hid, hid, hid, jax.ShapeDtypeStruct((1, D), F32)],
        compiler_params=_cp(("arbitrary",), VMEM_LIMIT),
    )(dx2, x1, a, b, g, wg, wu, wd)


def _mix_out_bwd(dx1, o, z, m, wsp, bsp, wbd, psc, gsv, gout, wout, name):
    s = dx1.shape[0]
    tm = _tile(s, 512)

    def body(dx1_ref, o_ref, uv_ref, m_ref, wsp_ref, bsp_ref, wbd_ref, psc_ref, gsv_ref, gout_ref, wout_ref,
             do_ref, dl_ref, duv_ref, dm_ref, dgo_ref, dgsv_ref, dpsc_ref, dwsp_ref, dbsp_ref, dwbd_ref):
        first = pl.program_id(0) == 0
        g = gout_ref[...]
        dmix = _dot_nt(dx1_ref[...].astype(BF16), wout_ref[...])
        o = o_ref[...]
        on, ro = _rms(o, HEADS * VH)
        do, dga = _rms_bwd(on, ro, g[:, :512], dmix[:, :512], HEADS * VH)
        for h in range(HEADS):
            sl = slice(h * VH, (h + 1) * VH)
            do_ref[h] = do[:, sl].astype(BF16)
            dl_ref[h] = jnp.broadcast_to(jnp.sum(do[:, sl] * o[:, sl], axis=-1, keepdims=True), (tm, LANES))
        uv = uv_ref[...]
        u, v = uv[:, :SGU], uv[:, SGU:]
        vx, rv = _rms(v, SGU)
        vn = (vx * gsv_ref[...]).astype(BF16)
        tri = _tril()
        wsp_m = [jnp.where(tri, wsp_ref[h], 0.0).astype(BF16) for h in range(HEADS)]
        zc = _sgu_gate(vn, wsp_m, bsp_ref[...])
        gm = u * zc
        gmn, rg = _rms(gm, SGU)
        dgm, dgg = _rms_bwd(gmn, rg, g[:, 512:768], dmix[:, 512:768], SGU)
        du = dgm * zc
        dzc = dgm * u
        dvn_parts = []
        dbsp = jnp.zeros((CHUNK, SGU), F32)
        dwsp = [jnp.zeros((CHUNK, CHUNK), F32) for _ in range(HEADS)]
        for cidx in range(tm // CHUNK):
            rs = slice(cidx * CHUNK, (cidx + 1) * CHUNK)
            dzc_c = dzc[rs]
            dbsp = dbsp + dzc_c
            dzb = dzc_c.astype(BF16)
            vc = vn[rs]
            dvn_c = jnp.zeros((CHUNK, SGU), F32)
            for h in range(HEADS):
                hm = _head_mask(h)
                dvn_c = dvn_c + jnp.where(hm, _dot_tn(wsp_m[h], dzb), 0.0)
                dwsp[h] = dwsp[h] + _dot_nt(jnp.where(hm, dzc_c, 0.0).astype(BF16), vc)
            dvn_parts.append(dvn_c)
        dvn = jnp.concatenate(dvn_parts, axis=0)
        dv, dgsv = _rms_bwd(vx, rv, gsv_ref[...], dvn, SGU)
        duv_ref[...] = jnp.concatenate([du, dv], axis=1).astype(BF16)
        mb = m_ref[...].astype(BF16)
        pw = _dot(mb, wbd_ref[...])
        po = pw * psc_ref[...]
        pon, rp = _rms(po, POOL)
        dpo, dgp = _rms_bwd(pon, rp, g[:, 768:], dmix[:, 768:], POOL)
        dpw = (dpo * psc_ref[...]).astype(BF16)
        dm_ref[...] = _dot_nt(dpw, wbd_ref[...])
        _accumulate(dgo_ref, jnp.concatenate([dga, dgg, dgp], axis=1), first)
        _accumulate(dgsv_ref, dgsv, first)
        _accumulate(dpsc_ref, jnp.sum(dpo * pw, axis=0, keepdims=True), first)
        _accumulate(dbsp_ref, dbsp, first)
        _accumulate(dwbd_ref, _dot_tn(mb, dpw), first)
        for h in range(HEADS):
            val = jnp.where(tri, dwsp[h], 0.0)

            @pl.when(first)
            def _(val=val, h=h):
                dwsp_ref[h] = val

            @pl.when(jnp.logical_not(first))
            def _(val=val, h=h):
                dwsp_ref[h] += val

    row = lambda w, j: pl.BlockSpec((tm, w), lambda i: (i, j))
    hspec = pl.BlockSpec((HEADS, tm, HP), lambda i: (0, i, 0))
    return pl.pallas_call(
        body, name=name, grid=(s // tm,),
        in_specs=[row(D, 0), row(512, 0), row(512, 1), row(POOL, 0),
                  _acc((HEADS, CHUNK, CHUNK)), _acc((CHUNK, SGU)),
                  _acc((POOL, POOL)), _acc((1, POOL)), _acc((1, SGU)), _acc((1, D)), _res((D, D))],
        out_specs=[hspec, hspec, row(512, 0), row(POOL, 0), _acc((1, D)), _acc((1, SGU)), _acc((1, POOL)),
                   _acc((HEADS, CHUNK, CHUNK)), _acc((CHUNK, SGU)), _acc((POOL, POOL))],
        out_shape=[jax.ShapeDtypeStruct((HEADS, s, HP), BF16), jax.ShapeDtypeStruct((HEADS, s, LANES), F32),
                   jax.ShapeDtypeStruct((s, 512), BF16), jax.ShapeDtypeStruct((s, POOL), F32),
                   jax.ShapeDtypeStruct((1, D), F32), jax.ShapeDtypeStruct((1, SGU), F32),
                   jax.ShapeDtypeStruct((1, POOL), F32), jax.ShapeDtypeStruct((HEADS, CHUNK, CHUNK), F32),
                   jax.ShapeDtypeStruct((CHUNK, SGU), F32), jax.ShapeDtypeStruct((POOL, POOL), F32)],
        compiler_params=_cp(("arbitrary",), VMEM_LIMIT),
    )(dx1, o, z, m, wsp, bsp, wbd, psc, gsv, gout, wout)


def _attn_bwd(q, k, v, do, lse, delta, name):
    s = q.shape[1]
    tq = tk = _tile(s, 512)
    nq = s // tq
    wide = ATT_WIDE * tq if s % (ATT_WIDE * tq) == 0 else tq

    def body(q_ref, k_ref, v_ref, do_ref, lse_ref, dl_ref, dq_ref, dk_ref, dv_ref):
        j = pl.program_id(1)

        @pl.when(j == 0)
        def _():
            dq_ref[...] = jnp.zeros_like(dq_ref)

        kj, vj = k_ref[0], v_ref[0]
        rh = tq // ATT_SPLIT

        def blk(start, rows, dk, dv, masked):
            offs = [pl.multiple_of(start + g * rh, rh) for g in range(rows // rh)]
            qs = [q_ref[0, pl.ds(off, rh), :] for off in offs]
            dos = [do_ref[0, pl.ds(off, rh), :] for off in offs]
            scs = [_dot_nt(qi, kj) for qi in qs]
            dps = [_dot_nt(doi, vj) for doi in dos]
            for g, off in enumerate(offs):
                lse_i = lse_ref[0, pl.ds(off, rh), :][:, :1]
                dl_i = dl_ref[0, pl.ds(off, rh), :][:, :1]
                sc = _causal_mask(scs[g], g * rh) if masked else scs[g]
                p = jnp.exp2(sc * EXP2_C - lse_i)
                ds = (p * (dps[g] - dl_i)).astype(BF16)
                dv = dv + _dot_tn(p.astype(BF16), dos[g])
                dk = dk + _dot_tn(ds, qs[g])
                dq_ref[0, pl.ds(off, rh), :] += _dot(ds, kj) * SCALE
            return dk, dv

        per = wide // tq
        zero = jnp.zeros((tk, HP), F32)
        dk, dv = blk(j * tq, tq, zero, zero, True)
        first_wide = (j + per) // per
        dk, dv = lax.fori_loop(j + 1, jnp.minimum(first_wide * per, nq), lambda i, c: blk(i * tq, tq, *c, False), (dk, dv))
        dk, dv = lax.fori_loop(first_wide, nq // per, lambda i, c: blk(i * wide, wide, *c, False), (dk, dv))
        dk_ref[0] = dk * SCALE
        dv_ref[0] = dv

    full = lambda: pl.BlockSpec((1, s, HP), lambda h, j: (h, 0, 0))
    blk_spec = lambda: pl.BlockSpec((1, tk, HP), lambda h, j: (h, j, 0))
    out = jax.ShapeDtypeStruct((HEADS, s, HP), F32)
    return pl.pallas_call(
        body, name=name, grid=(HEADS, s // tk),
        in_specs=[full(), blk_spec(), blk_spec(), full(), full(), full()],
        out_specs=[full(), blk_spec(), blk_spec()], out_shape=[out] * 3,
        compiler_params=_cp(("parallel", "arbitrary"), VMEM_LIMIT),
    )(q, k, v, do, lse, delta)


def _mla_prep_bwd(dq, dk, dv, z, tabs, gql, gkv, gq, gk, wq, wk, wv, name):
    s = z.shape[0]
    tm = _tile(s, 512)

    def body(dq_ref, dk_ref, dv_ref, ql_ref, kv_ref, kr_ref, c_ref, sa_ref, sb_ref, gql_ref, gkv_ref, gq_ref, gk_ref,
             wq_ref, wk_ref, wv_ref,
             dz_ref, qn_ref, kvn_ref, dqr_ref, dkr_ref, dvr_ref, dgql_ref, dgkv_ref, dgq_ref, dgk_ref):
        first = pl.program_id(0) == 0
        qx, rq = _rms(ql_ref[...], QL)
        qn = (qx * gql_ref[...]).astype(BF16)
        kx, rk = _rms(kv_ref[...], KVL)
        kvn = (kx * gkv_ref[...]).astype(BF16)
        qn_ref[...] = qn
        kvn_ref[...] = kvn
        qraw = _dot(qn, wq_ref[...])
        kraw = _dot(kvn, wk_ref[...])
        kr = kr_ref[...]
        c, sa, sb = c_ref[...], sa_ref[...], sb_ref[...]
        lane = lax.broadcasted_iota(jnp.int32, (tm, HP), 1)
        rope_lanes = (lane >= NOPE) & (lane < QK)
        dkrope = jnp.zeros((tm, HP), F32)
        dgq = jnp.zeros((1, HP), F32)
        dgk = jnp.zeros((1, HP), F32)
        for h in range(HEADS):
            sl = slice(h * HP, (h + 1) * HP)
            xn, r = _rms(qraw[:, sl], QK)
            dx, dg = _rms_bwd(xn, r, gq_ref[...], _rope_t(dq_ref[h], c, sa, sb), QK)
            dqr_ref[:, sl] = dx.astype(BF16)
            dgq = dgq + dg
            xn, r = _rms(kraw[:, sl] + kr, QK)
            dx, dg = _rms_bwd(xn, r, gk_ref[...], _rope_t(dk_ref[h], c, sa, sb), QK)
            dkr_ref[:, sl] = dx.astype(BF16)
            dgk = dgk + dg
            dkrope = dkrope + jnp.where(rope_lanes, dx, 0.0)
            dvr_ref[:, sl] = dv_ref[h].astype(BF16)
        dqn = _dot_nt(dqr_ref[...], wq_ref[...])
        dql, dgql = _rms_bwd(qx, rq, gql_ref[...], dqn, QL)
        dkvn = _dot_nt(dkr_ref[...], wk_ref[...]) + _dot_nt(dvr_ref[...], wv_ref[...])
        dkv, dgkv = _rms_bwd(kx, rk, gkv_ref[...], dkvn, KVL)
        dz_ref[...] = jnp.concatenate([dql, dkv, dkrope], axis=1).astype(BF16)
        _accumulate(dgql_ref, dgql, first)
        _accumulate(dgkv_ref, dgkv, first)
        _accumulate(dgq_ref, dgq, first)
        _accumulate(dgk_ref, dgk, first)

    row = lambda w, j: pl.BlockSpec((tm, w), lambda i: (i, j))
    hspec = pl.BlockSpec((HEADS, tm, HP), lambda i: (0, i, 0))
    sd = lambda w, dt: jax.ShapeDtypeStruct((s, w), dt)
    return pl.pallas_call(
        body, name=name, grid=(s // tm,),
        in_specs=[hspec, hspec, hspec, row(QL, 0), row(KVL, 2), row(HP, 3), row(HP, 0), row(HP, 0), row(HP, 0),
                  _acc((1, QL)), _acc((1, KVL)), _acc((1, HP)), _acc((1, HP)),
                  _acc((QL, HEADS * HP)), _acc((KVL, HEADS * HP)), _acc((KVL, HEADS * HP))],
        out_specs=[row(512, 0), row(QL, 0), row(KVL, 0), row(512, 0), row(512, 0), row(512, 0),
                   _acc((1, QL)), _acc((1, KVL)), _acc((1, HP)), _acc((1, HP))],
        out_shape=[sd(512, BF16), sd(QL, BF16), sd(KVL, BF16), sd(512, BF16), sd(512, BF16), sd(512, BF16),
                   jax.ShapeDtypeStruct((1, QL), F32), jax.ShapeDtypeStruct((1, KVL), F32),
                   jax.ShapeDtypeStruct((1, HP), F32), jax.ShapeDtypeStruct((1, HP), F32)],
        compiler_params=_cp(("arbitrary",), VMEM_LIMIT),
    )(dq, dk, dv, z, z, z, *tabs, gql, gkv, gq, gk, wq, wk, wv)


def _in_proj_bwd(dzm, duv, dp, x, dx1, g, win, name):
    s = x.shape[0]
    tm = _tile(s, 512)

    def body(dzm_ref, duv_ref, dp_ref, x_ref, dx1_ref, g_ref, w_ref, dx_ref, dg_ref):
        dh = _dot_nt(dzm_ref[...], w_ref[:, 0:512]) + _dot_nt(duv_ref[...], w_ref[:, 512:1024]) \
            + _dot_nt(dp_ref[...], w_ref[:, 1024:IN_P])
        xn, r = _rms(x_ref[...], D)
        dxr, dg = _rms_bwd(xn, r, g_ref[...], dh, D)
        dx_ref[...] = dx1_ref[...] + dxr
        _accumulate(dg_ref, dg, pl.program_id(0) == 0)

    row = lambda w: pl.BlockSpec((tm, w), lambda i: (i, 0))
    return pl.pallas_call(
        body, name=name, grid=(s // tm,),
        in_specs=[row(512), row(512), row(POOL), row(D), row(D), _acc((1, D)), _res((D, IN_P))],
        out_specs=[row(D), _acc((1, D))],
        out_shape=[jax.ShapeDtypeStruct((s, D), F32), jax.ShapeDtypeStruct((1, D), F32)],
        compiler_params=_cp(("arbitrary",), VMEM_LIMIT),
    )(dzm, duv, dp, x, dx1, g, win)


def _adamw(w, g0, g1, m, v, name):
    _, r, c = w.shape
    tr = _row_tile(r, 512)
    c1 = 1.0 - B1 ** STEP
    c2 = 1.0 - B2 ** STEP

    def body(w_ref, g0_ref, g1_ref, m_ref, v_ref, g_ref, d_ref, nm_ref, nv_ref):
        gv = jnp.where(pl.program_id(0) == 0, g0_ref[...], g1_ref[...])
        g_ref[0] = gv
        nm = B1 * m_ref[0] + (1.0 - B1) * gv
        nv = B2 * v_ref[0] + (1.0 - B2) * (gv * gv)
        nm_ref[0] = nm
        nv_ref[0] = nv
        d_ref[0] = -LR * ((nm / c1) / (jnp.sqrt(nv / c2) + ADAM_EPS) + WD * w_ref[0])

    spec = pl.BlockSpec((1, tr, c), lambda l, i: (l, i, 0))
    out = jax.ShapeDtypeStruct((DEPTH, r, c), F32)
    return pl.pallas_call(
        body, name=name, grid=(DEPTH, r // tr),
        in_specs=[spec, pl.BlockSpec((tr, c), lambda l, i: (i * (1 - l), 0)), pl.BlockSpec((tr, c), lambda l, i: (i * l, 0)),
                  spec, spec],
        out_specs=[spec] * 4, out_shape=[out] * 4, compiler_params=_cp(("parallel", "parallel")),
    )(w, g0, g1, m, v)


ANY = pl.BlockSpec(memory_space=pl.ANY)


def _place():
    x, y, c = lax.axis_index("x"), lax.axis_index("y"), lax.axis_index("c")
    chips = [(1 - x, y), (x, 1 - y), (1 - x, 1 - y)]
    return x, y, c, chips


def _half_rows(ref, lead, hh, half, align):
    rows = pl.ds(pl.multiple_of(hh * half, align), half)
    return ref.at[rows, :] if lead is None else ref.at[lead, rows, :]


def _row_align(dtype):
    return 16 if dtype == BF16 else 8


def _sems(n):
    return [pltpu.SemaphoreType.DMA((n,)), pltpu.SemaphoreType.DMA((n,)), pltpu.SemaphoreType.DMA((n,))]


def _comm_call(body, ins, out_shapes, nsems, name):
    return pl.pallas_call(
        body, name=name, in_specs=[ANY] * len(ins), out_specs=[ANY] * len(out_shapes), out_shape=out_shapes,
        scratch_shapes=_sems(nsems), compiler_params=pltpu.CompilerParams(has_side_effects=True),
    )(*ins)


def _all_gather_chips(shards, name):
    n = len(shards)
    halves = [a.shape[0] // 2 for a in shards]
    aligns = [_row_align(a.dtype) for a in shards]
    assert all(h % al == 0 for h, al in zip(halves, aligns))

    def body(*refs):
        ins, outs, (send_sems, recv_sems, _) = refs[:n], refs[n:2 * n], refs[2 * n:]
        x, y, c, chips = _place()
        me = 2 * x + y
        sibling = (x, y, 1 - c)

        def copy(sem, src, dst, to):
            return pltpu.make_async_remote_copy(src_ref=src, dst_ref=dst, send_sem=send_sems.at[sem],
                                                recv_sem=recv_sems.at[sem], device_id=to, device_id_type=MESH)

        first, passed = [], []
        for a in range(n):
            my_half = _half_rows(ins[a], None, c, halves[a], aligns[a])
            for j, (cx, cy) in enumerate(chips):
                cp = copy(6 * a + j, my_half, _half_rows(outs[a], me, c, halves[a], aligns[a]), (cx, cy, c))
                cp.start()
                first.append(cp)
        for a in range(n):
            for j, (cx, cy) in enumerate(chips):
                landed = _half_rows(outs[a], 2 * cx + cy, c, halves[a], aligns[a])
                copy(6 * a + j, landed, landed, (cx, cy, c)).wait_recv()
                fwd = copy(6 * a + 3 + j, landed, landed, sibling)
                fwd.start()
                passed.append(fwd)
        for a in range(n):
            for j, (cx, cy) in enumerate(chips):
                other = _half_rows(outs[a], 2 * cx + cy, 1 - c, halves[a], aligns[a])
                copy(6 * a + 3 + j, other, other, sibling).wait_recv()
        for cp in first + passed:
            cp.wait_send()

    lands = _comm_call(body, shards, [jax.ShapeDtypeStruct((CHIPS,) + a.shape, a.dtype) for a in shards], 6 * n, name)
    return _with_own(lands, shards)


def _with_own(lands, shards):
    me = 2 * lax.axis_index("x") + lax.axis_index("y")
    return [lax.dynamic_update_slice(g, a[None], (me, 0, 0)) for g, a in zip(lands, shards)]


def _pair_swap_halves(arrs, name):
    n = len(arrs)
    halves = [a.shape[1] // 2 for a in arrs]

    def body(*refs):
        ins, outs, (send_sems, recv_sems, _) = refs[:n], refs[n:2 * n], refs[2 * n:]
        x, y, c, _ = _place()
        cps = []
        for a in range(n):
            src = ins[a].at[:, pl.ds(pl.multiple_of((1 - c) * halves[a], 8), halves[a]), :]
            cp = pltpu.make_async_remote_copy(src_ref=src, dst_ref=outs[a], send_sem=send_sems.at[a],
                                              recv_sem=recv_sems.at[a], device_id=(x, y, 1 - c), device_id_type=MESH)
            cp.start()
            cps.append(cp)
        for cp in cps:
            cp.wait()

    return _comm_call(body, arrs, [jax.ShapeDtypeStruct((CHIPS, h, a.shape[2]), a.dtype) for a, h in zip(arrs, halves)],
                      n, name)


def _pair_add(full, got, cidx, name):
    _, half, cols = got.shape
    tr = _row_tile(half, 256)
    nt = half // tr

    grid_spec = pltpu.PrefetchScalarGridSpec(
        num_scalar_prefetch=1, grid=(CHIPS, nt),
        in_specs=[pl.BlockSpec((1, tr, cols), lambda k, r, c_ref: (k, c_ref[0] * nt + r, 0)),
                  pl.BlockSpec((1, tr, cols), lambda k, r, c_ref: (k, r, 0))],
        out_specs=pl.BlockSpec((1, tr, cols), lambda k, r, c_ref: (k, r, 0)))

    def body(c_ref, a_ref, b_ref, o_ref):
        o_ref[...] = a_ref[...] + b_ref[...]

    return pl.pallas_call(
        body, name=name, grid_spec=grid_spec, out_shape=jax.ShapeDtypeStruct(got.shape, got.dtype),
        compiler_params=_cp(("parallel", "parallel")),
    )(cidx, full, got)


def _chip_scatter(parts, name):
    n = len(parts)

    def body(*refs):
        ins, outs, (send_sems, recv_sems, _) = refs[:n], refs[n:2 * n], refs[2 * n:]
        cps = _remote_copies(_scatter_pairs, ins, outs, send_sems, recv_sems)
        for cp in cps:
            cp.start()
        for cp in cps:
            cp.wait()

    return _comm_call(body, parts, [jax.ShapeDtypeStruct((3,) + a.shape[1:], a.dtype) for a in parts], 3 * n, name)


def _pair_join(arrs, name):
    n = len(arrs)
    halves = [a.shape[0] // 2 for a in arrs]

    def body(*refs):
        outs, (send_sems, recv_sems, _) = refs[n:2 * n], refs[2 * n:]
        x, y, c, _ = _place()
        cps = []
        for a in range(n):
            mine = _half_rows(outs[a], None, c, halves[a], 8)
            cp = pltpu.make_async_remote_copy(src_ref=mine, dst_ref=mine, send_sem=send_sems.at[a], recv_sem=recv_sems.at[a],
                                              device_id=(x, y, 1 - c), device_id_type=MESH)
            cp.start()
            cps.append(cp)
        for cp in cps:
            cp.wait()

    return pl.pallas_call(
        body, name=name, in_specs=[ANY] * n, out_specs=[ANY] * n,
        out_shape=[jax.ShapeDtypeStruct(a.shape, a.dtype) for a in arrs],
        input_output_aliases={i: i for i in range(n)}, scratch_shapes=_sems(n),
        compiler_params=pltpu.CompilerParams(has_side_effects=True),
    )(*arrs)


HBM = pl.BlockSpec(memory_space=pltpu.HBM)
SEM = pl.BlockSpec(memory_space=pltpu.SEMAPHORE)
DATAFLOW = pltpu.SideEffectType.DATAFLOW_SIDE_EFFECTING


def _remote_copies(pairs, ins, lands, send_sems, recv_sems):
    return [pltpu.make_async_remote_copy(src_ref=src, dst_ref=dst, send_sem=send_sems.at[i], recv_sem=recv_sems.at[i],
                                         device_id=to, device_id_type=MESH)
            for i, (src, dst, to) in enumerate(pairs(ins, lands))]


def _split_start(srcs, land_shapes, ncopies, pairs, name):
    n, m = len(srcs), len(land_shapes)

    def body(*refs):
        ins, lands = refs[:n], refs[n:n + m]
        send_sems, recv_sems, token = refs[n + m], refs[n + m + 1], refs[-1]
        for cp in _remote_copies(pairs, ins, lands, send_sems, recv_sems):
            cp.start()
        token[...] = jnp.zeros_like(token)

    hbm = lambda a: pltpu.with_memory_space_constraint(a, pltpu.HBM)
    lands = [hbm(lax.empty(s.shape, s.dtype)) for s in land_shapes]
    thru = [pltpu.HBM(a.shape, a.dtype) for a in list(srcs) + lands]
    out = pl.pallas_call(
        body, name=name,
        out_shape=(pltpu.SemaphoreType.DMA((ncopies,)), pltpu.SemaphoreType.DMA((ncopies,)), *thru,
                   jax.ShapeDtypeStruct((8, LANES), F32)),
        in_specs=[HBM] * (n + m), out_specs=(SEM, SEM, *[HBM] * (n + m), pl.BlockSpec(memory_space=pltpu.VMEM)),
        input_output_aliases={i: 2 + i for i in range(n + m)},
        compiler_params=pltpu.CompilerParams(has_side_effects=DATAFLOW),
    )(*[hbm(a) for a in srcs], *lands)
    return out[0], out[1], list(out[2:2 + n]), list(out[2 + n:2 + n + m]), out[-1]


def _split_wait(send_sems, recv_sems, srcs, lands, after, pairs, name):
    n, m = len(srcs), len(lands)

    def body(*refs):
        ins, lands_ = refs[:n], refs[n:n + m]
        for cp in _remote_copies(pairs, ins, lands_, refs[n + m], refs[n + m + 1]):
            cp.wait_send()
            cp.wait_recv()

    out = pl.pallas_call(
        body, name=name, out_shape=tuple(pltpu.HBM(a.shape, a.dtype) for a in list(srcs) + list(lands)),
        in_specs=[HBM] * (n + m) + [SEM, SEM, ANY], out_specs=tuple([HBM] * (n + m)),
        input_output_aliases={i: i for i in range(n + m)},
        compiler_params=pltpu.CompilerParams(has_side_effects=DATAFLOW),
    )(*srcs, *lands, send_sems, recv_sems, after)
    return list(out[:n]), list(out[n:])


def _gather_pairs(halves, aligns):
    def pairs(ins, lands):
        x, y, c, chips = _place()
        me = 2 * x + y
        return [(_half_rows(ins[a], None, c, halves[a], aligns[a]), _half_rows(lands[a], me, c, halves[a], aligns[a]),
                 (cx, cy, c)) for a in range(len(ins)) for cx, cy in chips]
    return pairs


def _scatter_pairs(ins, lands):
    x, y, c, chips = _place()
    return [(ins[a].at[2 * cx + cy], lands[a].at[j], (cx, cy, c)) for a in range(len(ins)) for j, (cx, cy) in enumerate(chips)]


def _gather_finish(shards, lands, name):
    n = len(shards)
    halves = [a.shape[0] // 2 for a in shards]
    aligns = [_row_align(a.dtype) for a in shards]

    def body(*refs):
        outs, (send_sems, recv_sems, _) = refs[n:2 * n], refs[2 * n:]
        x, y, c, chips = _place()
        passed = []
        for a in range(n):
            for j, (cx, cy) in enumerate(chips):
                landed = _half_rows(outs[a], 2 * cx + cy, c, halves[a], aligns[a])
                cp = pltpu.make_async_remote_copy(src_ref=landed, dst_ref=landed, send_sem=send_sems.at[3 * a + j],
                                                  recv_sem=recv_sems.at[3 * a + j], device_id=(x, y, 1 - c),
                                                  device_id_type=MESH)
                cp.start()
                passed.append(cp)
        for a in range(n):
            for j, (cx, cy) in enumerate(chips):
                other = _half_rows(outs[a], 2 * cx + cy, 1 - c, halves[a], aligns[a])
                pltpu.make_async_remote_copy(src_ref=other, dst_ref=other, send_sem=send_sems.at[3 * a + j],
                                             recv_sem=recv_sems.at[3 * a + j], device_id=(x, y, 1 - c),
                                             device_id_type=MESH).wait_recv()
        for cp in passed:
            cp.wait_send()

    lands = pl.pallas_call(
        body, name=name, in_specs=[ANY] * n, out_specs=[ANY] * n,
        out_shape=[jax.ShapeDtypeStruct(a.shape, a.dtype) for a in lands],
        input_output_aliases={i: i for i in range(n)}, scratch_shapes=_sems(3 * n),
        compiler_params=pltpu.CompilerParams(has_side_effects=True),
    )(*lands)
    return _with_own(lands, shards)


def _sum_own_and_landed(own, landed, where, name):
    _, half, cols = own.shape
    tr = _row_tile(half, 256)
    nt = half // tr

    grid_spec = pltpu.PrefetchScalarGridSpec(
        num_scalar_prefetch=1, grid=(nt,),
        in_specs=[pl.BlockSpec((1, tr, cols), lambda r, w: (w[0], r, 0)),
                  pl.BlockSpec((3, tr, cols), lambda r, w: (0, r, 0))],
        out_specs=pl.BlockSpec((tr, cols), lambda r, w: (w[1] * nt + r, 0)))

    def body(w_ref, p_ref, q_ref, o_ref):
        o_ref[...] = ((p_ref[0] + q_ref[0]) + q_ref[1]) + q_ref[2]

    return pl.pallas_call(
        body, name=name, grid_spec=grid_spec, out_shape=jax.ShapeDtypeStruct((2 * half, cols), own.dtype),
        compiler_params=_cp(("parallel",)),
    )(where, own, landed)


BIG = [("w_in", (D, IN_W), 1), ("w_q_up", (QL, HEADS * QK), 1), ("w_kv_up", (KVL, HEADS * (NOPE + VH)), 1),
       ("w_out", (D, D), 0), ("w_gate", (D, HID), 1), ("w_up", (D, HID), 1), ("w_down", (HID, D), 0)]
SMALL = [("g_mix_norm", (D,)), ("g_q_lat", (QL,)), ("g_kv_lat", (KVL,)), ("g_q_head", (QK,)), ("g_k_head", (QK,)),
         ("g_sgu_v", (SGU,)), ("w_spatial", (HEADS, CHUNK, CHUNK)), ("b_spatial", (HEADS, CHUNK)),
         ("w_pool", (4, 64, 64)), ("pool_scale", (POOL,)), ("g_out_mla", (512,)), ("g_out_sgu", (SGU,)),
         ("g_out_pool", (POOL,)), ("g_ffn_norm", (D,))]
ORDER = ["g_mix_norm", "w_in", "g_q_lat", "w_q_up", "g_kv_lat", "w_kv_up", "g_q_head", "g_k_head", "g_sgu_v",
         "w_spatial", "b_spatial", "w_pool", "pool_scale", "g_out_mla", "g_out_sgu", "g_out_pool", "w_out",
         "g_ffn_norm", "w_gate", "w_up", "w_down"]
MIX_BIG = ["w_in", "w_q_up", "w_kv_up", "w_out"]
FFN_BIG = ["w_gate", "w_up", "w_down"]
DEPTH = 2
COLS = 1024
SMALL_N = sum(math.prod(s) for _, s in SMALL) * DEPTH
assert SMALL_N % CHIPS == 0
SMALL_ROWS = -(-(SMALL_N // CHIPS) // (16 * COLS)) * 16


def _unsplit_cols(g):
    return g.transpose(1, 0, 2).reshape(g.shape[1], CHIPS * g.shape[2])


def _split_cols(full):
    r, c = full.shape
    return full.reshape(r, CHIPS, c // CHIPS).transpose(1, 0, 2)


def _kernel_weights(g):
    win = _unsplit_cols(g["w_in"])
    zeros = lambda r, c: jnp.zeros((r, c), BF16)
    o2, o3, o4 = QL + KVL, QL + KVL + ROPE, QL + KVL + ROPE + 2 * SGU
    win_p = jnp.concatenate([win[:, :o2], zeros(D, NOPE), win[:, o2:o3], zeros(D, HP - QK), win[:, o3:o4], win[:, o4:]], axis=1)
    wq = _unsplit_cols(g["w_q_up"]).reshape(QL, HEADS, QK)
    wq_p = jnp.pad(wq, ((0, 0), (0, 0), (0, HP - QK))).reshape(QL, HEADS * HP)
    wkv = _unsplit_cols(g["w_kv_up"]).reshape(KVL, HEADS, NOPE + VH)
    wk_p = jnp.pad(wkv[:, :, :NOPE], ((0, 0), (0, 0), (0, HP - NOPE))).reshape(KVL, HEADS * HP)
    wv_p = wkv[:, :, NOPE:].reshape(KVL, HEADS * VH)
    return dict(win=win_p, wq=wq_p, wk=wk_p, wv=wv_p, wout=g["w_out"].reshape(D, D))


def _small_operands(p, l):
    row = lambda v: v.reshape(1, -1)
    pad = lambda v: jnp.pad(v, (0, HP - QK)).reshape(1, HP)
    wpool = p["w_pool"][l]
    wbd = jnp.zeros((POOL, POOL), F32)
    for g in range(4):
        wbd = lax.dynamic_update_slice(wbd, wpool[g], (g * 64, g * 64))
    return dict(
        g_mix=row(p["g_mix_norm"][l]), gql=row(p["g_q_lat"][l]), gkv=row(p["g_kv_lat"][l]),
        gq=pad(p["g_q_head"][l]), gk=pad(p["g_k_head"][l]), gsv=row(p["g_sgu_v"][l]),
        wsp=p["w_spatial"][l], bsp=jnp.repeat(p["b_spatial"][l].T, SGU // HEADS, axis=1),
        wbd=wbd.astype(BF16), psc=row(p["pool_scale"][l]),
        gout=jnp.concatenate([p["g_out_mla"][l], p["g_out_sgu"][l], p["g_out_pool"][l]]).reshape(1, D),
        g_ffn=row(p["g_ffn_norm"][l]))


def _big_grads(g):
    dwin = g["win"]
    o2 = QL + KVL
    gin = jnp.concatenate([dwin[:, :o2], dwin[:, o2 + NOPE:o2 + NOPE + ROPE], dwin[:, 512:]], axis=1)
    gq = g["wq"].reshape(QL, HEADS, HP)[:, :, :QK].reshape(QL, HEADS * QK)
    gk = g["wk"].reshape(KVL, HEADS, HP)[:, :, :NOPE]
    gv = g["wv"].reshape(KVL, HEADS, VH)
    gkv = jnp.concatenate([gk, gv], axis=2).reshape(KVL, HEADS * (NOPE + VH))
    return {"w_in": _split_cols(gin), "w_q_up": _split_cols(gq), "w_kv_up": _split_cols(gkv),
            "w_out": g["wout"].reshape(CHIPS, D // CHIPS, D), "w_gate": g["wg"], "w_up": g["wu"], "w_down": g["wd"]}


def _small_grads(g):
    go = g["gout"].reshape(-1)
    return {"g_mix_norm": g["g_mix"].reshape(-1), "g_q_lat": g["gql"].reshape(-1), "g_kv_lat": g["gkv"].reshape(-1),
            "g_q_head": g["gq"].reshape(-1)[:QK], "g_k_head": g["gk"].reshape(-1)[:QK], "g_sgu_v": g["gsv"].reshape(-1),
            "w_spatial": g["wsp"], "b_spatial": g["bsp"].reshape(CHUNK, HEADS, SGU // HEADS).sum(-1).T,
            "w_pool": jnp.stack([g["wbd"][i * 64:(i + 1) * 64, i * 64:(i + 1) * 64] for i in range(4)]),
            "pool_scale": g["psc"].reshape(-1), "g_out_mla": go[:512], "g_out_sgu": go[512:768],
            "g_out_pool": go[768:], "g_ffn_norm": g["g_ffn"].reshape(-1)}


def _pack_small_grads(small):
    sm = jnp.concatenate([small[l][n].reshape(-1) for l in range(DEPTH) for n, _ in SMALL]).reshape(CHIPS, SMALL_N // CHIPS)
    return jnp.pad(sm, ((0, 0), (0, SMALL_ROWS * COLS - SMALL_N // CHIPS))).reshape(CHIPS, SMALL_ROWS, COLS)


def _unpack_small_grads(gathered):
    flat = gathered.reshape(CHIPS, SMALL_ROWS * COLS)[:, :SMALL_N // CHIPS].reshape(-1)
    out, off = [], 0
    for _ in range(DEPTH):
        layer = {}
        for n, shape in SMALL:
            k = math.prod(shape)
            layer[n] = flat[off:off + k].reshape(shape)
            off += k
        out.append(layer)
    return out


def _layer_fwd(x, tabs, kw, ffn_weights, sp, l):
    t = f"_l{l}"
    z, hb = _in_proj_fwd(x, sp["g_mix"], kw["win"], "in_proj_fwd" + t)
    q, k, v = _mla_prep_fwd(z, tabs, sp["gql"], sp["gkv"], sp["gq"], sp["gk"], kw["wq"], kw["wk"], kw["wv"],
                            "mla_prep_fwd" + t)
    o, lse = _attn_fwd(q, k, v, "attn_fwd" + t)
    m = _pool_win_fwd(z, "pool_win_fwd" + t)
    x1, mix = _mix_out_fwd(o, z, m, x, sp["wsp"], sp["bsp"], sp["wbd"], sp["psc"], sp["gsv"], sp["gout"], kw["wout"],
                           "mix_out_fwd" + t)
    wg, wu, wd = ffn_weights(x1)
    x2, a, b, h2 = _ffn_fwd(x1, sp["g_ffn"], wg, wu, wd, "ffn_fwd" + t)
    saved = dict(x=x, z=z, hb=hb, q=q, k=k, v=v, o=o, lse=lse, m=m, x1=x1, mix=mix, a=a, b=b, h2=h2, wg=wg, wu=wu, wd=wd)
    return x2, saved


def _layer_bwd(dx2, sv, tabs, kw, sp, l, ffn_hook):
    t = f"_l{l}"
    g = {}
    dx1, hid, da, db, g["g_ffn"] = _ffn_bwd(dx2, sv["x1"], sv["a"], sv["b"], sp["g_ffn"], sv["wg"], sv["wu"], sv["wd"],
                                            "ffn_bwd" + t)
    g["wd"] = _wgrad_rows(hid, dx2, "wgrad_down" + t)
    g["wg"] = _wgrad_cols(sv["h2"], da, "wgrad_gate" + t)
    g["wu"] = _wgrad_cols(sv["h2"], db, "wgrad_up" + t)
    gout = sp["gout"] + ffn_hook(g)
    do, delta, duv, dm, g["gout"], g["gsv"], g["psc"], g["wsp"], g["bsp"], g["wbd"] = _mix_out_bwd(
        dx1, sv["o"], sv["z"], sv["m"], sp["wsp"], sp["bsp"], sp["wbd"], sp["psc"], sp["gsv"], gout, kw["wout"],
        "mix_out_bwd" + t)
    g["wout"] = _wgrad(sv["mix"], dx1, "wgrad_out" + t)
    dp = _pool_win_bwd(dm, "pool_win_bwd" + t)
    dq, dk, dv = _attn_bwd(sv["q"], sv["k"], sv["v"], do, sv["lse"], delta, "attn_bwd" + t)
    dzm, qn, kvn, dqr, dkr, dvr, g["gql"], g["gkv"], g["gq"], g["gk"] = _mla_prep_bwd(
        dq, dk, dv, sv["z"], tabs, sp["gql"], sp["gkv"], sp["gq"], sp["gk"], kw["wq"], kw["wk"], kw["wv"],
        "mla_prep_bwd" + t)
    g["wq"] = _wgrad(qn, dqr, "wgrad_q_up" + t)
    g["wk"] = _wgrad(kvn, dkr, "wgrad_k_up" + t)
    g["wv"] = _wgrad(kvn, dvr, "wgrad_v_up" + t)
    dx, g["g_mix"] = _in_proj_bwd(dzm, duv, dp, sv["x"], dx1, sp["g_mix"], kw["win"], "in_proj_bwd" + t)
    g["win"] = jnp.concatenate([_wgrad(sv["hb"], dzm, "wgrad_in_a" + t), _wgrad(sv["hb"], duv, "wgrad_in_b" + t),
                                _wgrad(sv["hb"], dp, "wgrad_in_c" + t)], axis=1)
    return dx, g


def _rope_inv_freq():
    half = ROPE // 2
    inv = 1.0 / (ROPE_THETA ** (jnp.arange(half, dtype=F32) / half))
    return jnp.concatenate([jnp.zeros((NOPE,), F32), inv, inv, jnp.zeros((HP - QK,), F32)]).reshape(1, HP)


def kernel(x, positions, g_mix_norm, w_in, g_q_lat, w_q_up, g_kv_lat, w_kv_up, g_q_head, g_k_head, g_sgu_v, w_spatial, b_spatial, w_pool, pool_scale, g_out_mla, g_out_sgu, g_out_pool, w_out, g_ffn_norm, w_gate, w_up, w_down, loss_target, m_g_mix_norm, m_w_in, m_g_q_lat, m_w_q_up, m_g_kv_lat, m_w_kv_up, m_g_q_head, m_g_k_head, m_g_sgu_v, m_w_spatial, m_b_spatial, m_w_pool, m_pool_scale, m_g_out_mla, m_g_out_sgu, m_g_out_pool, m_w_out, m_g_ffn_norm, m_w_gate, m_w_up, m_w_down, v_g_mix_norm, v_w_in, v_g_q_lat, v_w_q_up, v_g_kv_lat, v_w_kv_up, v_g_q_head, v_g_k_head, v_g_sgu_v, v_w_spatial, v_b_spatial, v_w_pool, v_pool_scale, v_g_out_mla, v_g_out_sgu, v_g_out_pool, v_w_out, v_g_ffn_norm, v_w_gate, v_w_up, v_w_down):
    given = dict(locals())
    p = {n: given[n] for n in ORDER}
    seq = x.shape[1]
    cidx = lax.axis_index("c").astype(jnp.int32).reshape(1)
    where = jnp.stack([2 * lax.axis_index("x") + lax.axis_index("y"), lax.axis_index("c")]).astype(jnp.int32)
    shards = lambda names: [p[n][l].astype(BF16) for l, n in names]
    zero11 = lambda token: token[:1, :1]

    names_0a = [(0, n) for n in MIX_BIG]
    names_0b = [(0, n) for n in FFN_BIG]
    names_1 = [(1, n) for n, _, _ in BIG]
    got_0a = dict(zip(MIX_BIG, _all_gather_chips(shards(names_0a), "all_gather_w0a")))
    started = {}
    for tag, names in (("w0b", names_0b), ("w1", names_1)):
        sh = shards(names)
        pairs = _gather_pairs([a.shape[0] // 2 for a in sh], [_row_align(a.dtype) for a in sh])
        lands = [jax.ShapeDtypeStruct((CHIPS,) + a.shape, a.dtype) for a in sh]
        started[tag] = (sh, pairs) + _split_start(sh, lands, 3 * len(sh), pairs, "gather_start_" + tag)

    def arrived(tag, after):
        _, pairs, send, recv, srcs, lands, _ = started[tag]
        srcs, lands = _split_wait(send, recv, srcs, lands, after, pairs, "gather_wait_" + tag)
        return _gather_finish(srcs, lands, "gather_finish_" + tag)

    layer1 = {}

    def mix_weights(l, h):
        if l == 0:
            return got_0a
        layer1.update(zip([n for _, n in names_1], arrived("w1", h)))
        return layer1

    def ffn_weights(l, x1):
        return arrived("w0b", x1) if l == 0 else [layer1[n] for n in FFN_BIG]

    reducing, last = {}, {}

    def reduce_start(tag, arrs):
        theirs = _pair_swap_halves(arrs, "grad_pair_swap_" + tag)
        pair = [_pair_add(a, t, cidx, f"grad_pair_add_{tag}_{i}") for i, (a, t) in enumerate(zip(arrs, theirs))]
        lands = [jax.ShapeDtypeStruct((3,) + a.shape[1:], a.dtype) for a in pair]
        reducing[tag] = _split_start(pair, lands, 3 * len(pair), _scatter_pairs, "grad_scatter_start_" + tag)
        return zero11(reducing[tag][4])

    def reduce_finish(tag, after):
        send, recv, srcs, lands, _ = reducing[tag]
        srcs, lands = _split_wait(send, recv, srcs, lands, after, _scatter_pairs, "grad_scatter_wait_" + tag)
        return [_sum_own_and_landed(a, q, where, f"grad_sum_{tag}_{i}") for i, (a, q) in enumerate(zip(srcs, lands))]

    def ffn_hook(l, g):
        if l == 1:
            return jnp.zeros((1, 1), F32)
        return reduce_start("g0b", [g["wg"], g["wu"], g["wd"]])

    def layer_hook(l, big, small):
        last[l] = (big, small)
        if l == 1:
            return reduce_start("g1", [big[n] for n, _, _ in BIG])
        return None

    entry = zero11(started["w0b"][6]) + zero11(started["w1"][6])
    loss_part, dx = _step(x.reshape(seq, D), positions.reshape(seq, 1), loss_target.reshape(seq, D), p, entry,
                          mix_weights, ffn_weights, ffn_hook, layer_hook)
    loss = lax.psum(loss_part, ("x", "y", "c"))

    rest = [last[0][0][n] for n in MIX_BIG] + [_pack_small_grads([last[l][1] for l in range(DEPTH)])]
    theirs = _pair_swap_halves(rest, "grad_pair_swap_g0a")
    pair = [_pair_add(a, t, cidx, f"grad_pair_add_g0a_{i}") for i, (a, t) in enumerate(zip(rest, theirs))]
    sums_0a = [_sum_own_and_landed(a, q, where, f"grad_sum_g0a_{i}")
               for i, (a, q) in enumerate(zip(pair, _chip_scatter(pair, "grad_chip_scatter_g0a")))]
    sums_1 = reduce_finish("g1", dx)
    sums_0b = reduce_finish("g0b", dx)
    order = names_1 + names_0b + names_0a + ["small"]
    sums = dict(zip(order, _pair_join(sums_1 + sums_0b + sums_0a, "grad_pair_join")))
    gsmall = _unpack_small_grads(_all_gather_chips([sums["small"]], "all_gather_small_grads")[0])
    grads = {n: [sums[(l, n)] for l in range(DEPTH)] for n, _, _ in BIG}
    grads.update({n: [gsmall[l][n] for l in range(DEPTH)] for n, _ in SMALL})

    out = {}
    for n in ORDER:
        w = p[n]
        three_d = (DEPTH, -1, w.shape[-1])
        two_d = three_d[1:]
        res = _adamw(w.reshape(three_d), grads[n][0].reshape(two_d), grads[n][1].reshape(two_d),
                     given["m_" + n].reshape(three_d), given["v_" + n].reshape(three_d), "adamw_" + n)
        out[n] = [r.reshape(w.shape) for r in res]
    return (loss, dx.reshape(x.shape), *[out[n][i] for i in range(4) for n in ORDER])


def _step(xs, pos, tgt, p, entry, mix_weights, ffn_weights, ffn_hook, layer_hook):
    sps = [_small_operands(p, l) for l in range(DEPTH)]
    sps[0]["g_mix"] = sps[0]["g_mix"] + entry
    tabs = _rope_tables(pos, _rope_inv_freq())
    saved, h = [], xs
    for l in range(DEPTH):
        kw = _kernel_weights(mix_weights(l, h))
        h, sv = _layer_fwd(h, tabs, kw, functools.partial(ffn_weights, l), sps[l], l)
        saved.append(dict(sv, kw=kw))
    dy, lpart = _loss_grad(h, tgt)
    for l in reversed(range(DEPTH)):
        dy, g = _layer_bwd(dy, saved[l], tabs, saved[l]["kw"], sps[l], l, functools.partial(ffn_hook, l))
        zero = layer_hook(l, _big_grads(g), _small_grads(g))
        if zero is not None and l > 0:
            sps[l - 1]["g_ffn"] = sps[l - 1]["g_ffn"] + zero
    return 0.5 / D * jnp.sum(lpart), dy
```

```python
import functools
import math

import jax
import jax.numpy as jnp
from jax import lax
from jax.experimental import pallas as pl
from jax.experimental.pallas import tpu as pltpu

F32 = jnp.float32
BF16 = jnp.bfloat16
MESH = pl.DeviceIdType.MESH

D = 1024
HEADS = 4
QK = 96
NOPE = 64
ROPE = 32
VH = 128
HP = 128
QL = 256
KVL = 128
SGU = 256
POOL = 256
CHUNK = 128
HID = 2816
CHIPS = 4
SH = HID // CHIPS
IN_W = 1184
IN_P = 1280
EPS = 1e-6
ROPE_THETA = 10000.0
SCALE = 1.0 / math.sqrt(QK)
LOG2E = 1.4426950408889634
EXP2_C = SCALE * LOG2E
ATT_SPLIT = 2
ATT_WIDE = 4
NEG = -1e30
HALO = 16

LR, B1, B2, ADAM_EPS, WD, STEP = 0.001, 0.9, 0.999, 1e-08, 0.01, 10

VMEM_LIMIT = 56 * 1024 * 1024
LANES = 128


def _cp(sem, vmem=None):
    return pltpu.CompilerParams(dimension_semantics=sem, vmem_limit_bytes=vmem)


def _res(shape):
    nd = len(shape)
    return pl.BlockSpec(shape, lambda *_: (0,) * nd, pipeline_mode=pl.Buffered(1))


def _acc(shape):
    nd = len(shape)
    return pl.BlockSpec(shape, lambda *_: (0,) * nd)


def _dot(a, b):
    return jnp.dot(a, b, preferred_element_type=F32)


def _dot_nt(a, b):
    return lax.dot_general(a, b, (((1,), (1,)), ((), ())), preferred_element_type=F32)


def _dot_tn(a, b):
    return lax.dot_general(a, b, (((0,), (0,)), ((), ())), preferred_element_type=F32)


def _rms(x, n):
    r = lax.rsqrt(jnp.sum(x * x, axis=-1, keepdims=True) * (1.0 / n) + EPS)
    return x * r, r


def _rms_bwd(xn, r, g, dy, n):
    dn = dy * g
    dx = r * (dn - xn * (jnp.sum(dn * xn, axis=-1, keepdims=True) * (1.0 / n)))
    return dx, jnp.sum(dy * xn, axis=0, keepdims=True)


def _accumulate(ref, val, first):
    @pl.when(first)
    def _():
        ref[...] = val

    @pl.when(jnp.logical_not(first))
    def _():
        ref[...] += val


def _accumulate0(ref, val, first):
    @pl.when(first)
    def _():
        ref[0] = val

    @pl.when(jnp.logical_not(first))
    def _():
        ref[0] += val


def _tile(s, t):
    return min(s, t)


def _row_tile(r, cap):
    if r <= cap:
        return r
    return max(t for t in range(8, cap + 1, 8) if r % t == 0)


def _rope_tables(pos, invf):
    s = pos.shape[0]
    tm = _tile(s, 1024)

    def body(pos_ref, invf_ref, c_ref, sa_ref, sb_ref):
        ang = pos_ref[...].astype(F32) * invf_ref[...]
        c, sn = jnp.cos(ang), jnp.sin(ang)
        lane = lax.broadcasted_iota(jnp.int32, ang.shape, 1)
        first = (lane >= NOPE) & (lane < NOPE + ROPE // 2)
        second = (lane >= NOPE + ROPE // 2) & (lane < QK)
        c_ref[...] = jnp.where(first | second, c, 1.0)
        sa_ref[...] = jnp.where(first, -sn, 0.0)
        sb_ref[...] = jnp.where(second, sn, 0.0)

    out = jax.ShapeDtypeStruct((s, HP), F32)
    return pl.pallas_call(
        body, name="rope_tables", grid=(s // tm,),
        in_specs=[pl.BlockSpec((tm, 1), lambda i: (i, 0)), _acc((1, HP))],
        out_specs=[pl.BlockSpec((tm, HP), lambda i: (i, 0))] * 3,
        out_shape=[out] * 3, compiler_params=_cp(("parallel",)),
    )(pos, invf)


def _rope(x, c, sa, sb):
    return x * c + pltpu.roll(x, HP - ROPE // 2, 1) * sa + pltpu.roll(x, ROPE // 2, 1) * sb


def _rope_t(d, c, sa, sb):
    return d * c + pltpu.roll(d * sa, ROPE // 2, 1) + pltpu.roll(d * sb, HP - ROPE // 2, 1)


def _in_proj_fwd(x, g, w, name):
    s = x.shape[0]
    tm = _tile(s, 512)

    def body(x_ref, g_ref, w_ref, z_ref, h_ref):
        xn, _ = _rms(x_ref[...], D)
        h = (xn * g_ref[...]).astype(BF16)
        h_ref[...] = h
        z_ref[...] = _dot(h, w_ref[...])

    return pl.pallas_call(
        body, name=name, grid=(s // tm,),
        in_specs=[pl.BlockSpec((tm, D), lambda i: (i, 0)), _acc((1, D)), _res((D, IN_P))],
        out_specs=[pl.BlockSpec((tm, IN_P), lambda i: (i, 0)), pl.BlockSpec((tm, D), lambda i: (i, 0))],
        out_shape=[jax.ShapeDtypeStruct((s, IN_P), F32), jax.ShapeDtypeStruct((s, D), BF16)],
        compiler_params=_cp(("parallel",), VMEM_LIMIT),
    )(x, g, w)


def _mla_prep_fwd(z, tabs, gql, gkv, gq, gk, wq, wk, wv, name):
    s = z.shape[0]
    tm = _tile(s, 512)

    def body(ql_ref, kv_ref, kr_ref, c_ref, sa_ref, sb_ref, gql_ref, gkv_ref, gq_ref, gk_ref,
             wq_ref, wk_ref, wv_ref, q_out, k_out, v_out):
        qn = (_rms(ql_ref[...], QL)[0] * gql_ref[...]).astype(BF16)
        kvn = (_rms(kv_ref[...], KVL)[0] * gkv_ref[...]).astype(BF16)
        qraw = _dot(qn, wq_ref[...])
        kraw = _dot(kvn, wk_ref[...])
        vraw = _dot(kvn, wv_ref[...])
        kr = kr_ref[...]
        c, sa, sb = c_ref[...], sa_ref[...], sb_ref[...]
        for h in range(HEADS):
            sl = slice(h * HP, (h + 1) * HP)
            xq = _rms(qraw[:, sl], QK)[0] * gq_ref[...]
            q_out[h] = _rope(xq, c, sa, sb).astype(BF16)
            xk = _rms(kraw[:, sl] + kr, QK)[0] * gk_ref[...]
            k_out[h] = _rope(xk, c, sa, sb).astype(BF16)
            v_out[h] = vraw[:, sl].astype(BF16)

    row = lambda w, j: pl.BlockSpec((tm, w), lambda i: (i, j))
    hspec = pl.BlockSpec((HEADS, tm, HP), lambda i: (0, i, 0))
    hshape = jax.ShapeDtypeStruct((HEADS, s, HP), BF16)
    return pl.pallas_call(
        body, name=name, grid=(s // tm,),
        in_specs=[row(QL, 0), row(KVL, 2), row(HP, 3), row(HP, 0), row(HP, 0), row(HP, 0),
                  _acc((1, QL)), _acc((1, KVL)), _acc((1, HP)), _acc((1, HP)),
                  _acc((QL, HEADS * HP)), _acc((KVL, HEADS * HP)), _acc((KVL, HEADS * HP))],
        out_specs=[hspec] * 3, out_shape=[hshape] * 3,
        compiler_params=_cp(("parallel",)),
    )(z, z, z, *tabs, gql, gkv, gq, gk, wq, wk, wv)


def _causal_mask(s, row0):
    row = lax.broadcasted_iota(jnp.int32, s.shape, 0) + row0
    col = lax.broadcasted_iota(jnp.int32, s.shape, 1)
    return jnp.where(col <= row, s, NEG)


def _attn_fwd(q, k, v, name):
    s = q.shape[1]
    tq = _tile(s, 512)
    wide = ATT_WIDE * tq if s % (ATT_WIDE * tq) == 0 else tq
    rh = tq // ATT_SPLIT

    def body(q_ref, k_ref, v_ref, o_ref, lse_ref):
        i = pl.program_id(1)

        def blk(off, tk, carry, masked):
            off = pl.multiple_of(off, tq)
            kj = k_ref[0, pl.ds(off, tk), :]
            vj = v_ref[0, pl.ds(off, tk), :]
            out = []
            keys = [(g + 1) * rh if masked else tk for g in range(ATT_SPLIT)]
            scs = [_dot_nt(q_ref[0, g * rh:(g + 1) * rh, :], kj[:keys[g]]) for g in range(ATT_SPLIT)]
            for g, (m, l, acc) in enumerate(carry):
                sc = scs[g]
                if masked:
                    sc = _causal_mask(sc, g * rh)
                m_new = jnp.maximum(m, jnp.max(sc, axis=-1, keepdims=True))
                p = jnp.exp2((sc - m_new) * EXP2_C)
                alpha = jnp.exp2((m - m_new) * EXP2_C)
                l = alpha * l + jnp.sum(p, axis=-1, keepdims=True)
                acc = alpha * acc + _dot(p.astype(BF16), vj[:keys[g]])
                out.append((m_new, l, acc))
            return tuple(out)

        one = (jnp.full((rh, 1), NEG, F32), jnp.zeros((rh, 1), F32), jnp.zeros((rh, VH), F32))
        nwide = (i * tq) // wide
        carry = lax.fori_loop(0, nwide, lambda j, c: blk(j * wide, wide, c, False), (one,) * ATT_SPLIT)
        carry = lax.fori_loop(nwide * (wide // tq), i, lambda j, c: blk(j * tq, tq, c, False), carry)
        carry = blk(i * tq, tq, carry, True)
        for g, (m, l, acc) in enumerate(carry):
            o_ref[g * rh:(g + 1) * rh, :] = acc / l
            lse_ref[0, g * rh:(g + 1) * rh, :] = jnp.broadcast_to(m * EXP2_C + jnp.log(l) * LOG2E, (rh, LANES))

    return pl.pallas_call(
        body, name=name, grid=(HEADS, s // tq),
        in_specs=[pl.BlockSpec((1, tq, HP), lambda h, i: (h, i, 0)),
                  pl.BlockSpec((1, s, HP), lambda h, i: (h, 0, 0)),
                  pl.BlockSpec((1, s, HP), lambda h, i: (h, 0, 0))],
        out_specs=[pl.BlockSpec((tq, VH), lambda h, i: (i, h)),
                   pl.BlockSpec((1, tq, LANES), lambda h, i: (h, i, 0))],
        out_shape=[jax.ShapeDtypeStruct((s, HEADS * VH), F32), jax.ShapeDtypeStruct((HEADS, s, LANES), F32)],
        compiler_params=_cp(("parallel", "arbitrary"), VMEM_LIMIT),
    )(q, k, v)


def _lane_group(shape, j):
    return (lax.broadcasted_iota(jnp.int32, shape, 1) + j * LANES) // (POOL // 4)


def _pool_win_fwd(z, name):
    s = z.shape[0]
    ch = _tile(s, 512)
    col0 = (IN_P - POOL) // LANES

    def body(p_ref, m_ref):
        j = pl.program_id(0)

        def chunk(r, _):
            off = pl.multiple_of(r * ch, ch)
            cur = p_ref[pl.ds(off, ch), :]
            hoff = pl.multiple_of(jnp.maximum(off - HALO, 0), 8)
            halo = jnp.where(r > 0, p_ref[pl.ds(hoff, HALO), :], 0.0)
            x = jnp.concatenate([halo, cur], axis=0)
            s2 = x + pltpu.roll(x, 1, 0)
            s4 = s2 + pltpu.roll(s2, 2, 0)
            s8 = s4 + pltpu.roll(s4, 4, 0)
            s16 = s8 + pltpu.roll(s8, 8, 0)
            grp = _lane_group((ch, LANES), j)
            sel = jnp.where(grp == 0, s2[HALO:], jnp.where(grp == 1, s4[HALO:], jnp.where(grp == 2, s8[HALO:], s16[HALO:])))
            t1 = (lax.broadcasted_iota(jnp.int32, (ch, LANES), 0) + off + 1).astype(F32)
            win = jnp.where(grp == 0, 2.0, jnp.where(grp == 1, 4.0, jnp.where(grp == 2, 8.0, 16.0)))
            m_ref[pl.ds(off, ch), :] = sel / jnp.minimum(t1, win) - cur
            return 0

        lax.fori_loop(0, s // ch, chunk, 0)

    return pl.pallas_call(
        body, name=name, grid=(POOL // LANES,),
        in_specs=[pl.BlockSpec((s, LANES), lambda j: (0, col0 + j))],
        out_specs=pl.BlockSpec((s, LANES), lambda j: (0, j)),
        out_shape=jax.ShapeDtypeStruct((s, POOL), F32),
        compiler_params=_cp(("parallel",), VMEM_LIMIT),
    )(z)


def _pool_win_bwd(dm, name):
    s = dm.shape[0]
    ch = _tile(s, 512)
    n = s // ch

    def body(dm_ref, dp_ref):
        j = pl.program_id(0)

        def chunk(r, _):
            off = pl.multiple_of(r * ch, ch)
            grp = _lane_group((ch + HALO, LANES), j)
            win = jnp.where(grp == 0, 2.0, jnp.where(grp == 1, 4.0, jnp.where(grp == 2, 8.0, 16.0)))
            cur = dm_ref[pl.ds(off, ch), :]
            hoff = pl.multiple_of(jnp.minimum(off + ch, s - HALO), 8)
            halo = jnp.where(r < n - 1, dm_ref[pl.ds(hoff, HALO), :], 0.0)
            x = jnp.concatenate([cur, halo], axis=0)
            t1 = (lax.broadcasted_iota(jnp.int32, (ch + HALO, LANES), 0) + off + 1).astype(F32)
            e = x / jnp.minimum(t1, win)
            tot = ch + HALO
            r2 = e + pltpu.roll(e, tot - 1, 0)
            r4 = r2 + pltpu.roll(r2, tot - 2, 0)
            r8 = r4 + pltpu.roll(r4, tot - 4, 0)
            r16 = r8 + pltpu.roll(r8, tot - 8, 0)
            g = grp[:ch]
            sel = jnp.where(g == 0, r2[:ch], jnp.where(g == 1, r4[:ch], jnp.where(g == 2, r8[:ch], r16[:ch])))
            dp_ref[pl.ds(off, ch), :] = (sel - cur).astype(BF16)
            return 0

        lax.fori_loop(0, n, chunk, 0)

    return pl.pallas_call(
        body, name=name, grid=(POOL // LANES,),
        in_specs=[pl.BlockSpec((s, LANES), lambda j: (0, j))],
        out_specs=pl.BlockSpec((s, LANES), lambda j: (0, j)),
        out_shape=jax.ShapeDtypeStruct((s, POOL), BF16),
        compiler_params=_cp(("parallel",), VMEM_LIMIT),
    )(dm)


def _head_mask(h):
    lane = lax.broadcasted_iota(jnp.int32, (CHUNK, SGU), 1)
    return (lane // (SGU // HEADS)) == h


def _tril(upper=False):
    row = lax.broadcasted_iota(jnp.int32, (CHUNK, CHUNK), 0)
    col = lax.broadcasted_iota(jnp.int32, (CHUNK, CHUNK), 1)
    return col >= row if upper else col <= row


def _sgu_gate(vn, wsp, bsp):
    out = []
    for cidx in range(vn.shape[0] // CHUNK):
        vc = vn[cidx * CHUNK:(cidx + 1) * CHUNK]
        zc = bsp
        for h in range(HEADS):
            zc = zc + jnp.where(_head_mask(h), _dot(wsp[h], vc), 0.0)
        out.append(zc)
    return jnp.concatenate(out, axis=0)


def _mix_out_fwd(o, z, m, x, wsp, bsp, wbd, psc, gsv, gout, wout, name):
    s = x.shape[0]
    tm = _tile(s, 512)

    def body(o_ref, uv_ref, m_ref, x_ref, wsp_ref, bsp_ref, wbd_ref, psc_ref, gsv_ref, gout_ref, wout_ref,
             x1_ref, mix_ref):
        g = gout_ref[...]
        an = _rms(o_ref[...], HEADS * VH)[0] * g[:, :512]
        uv = uv_ref[...]
        u, v = uv[:, :SGU], uv[:, SGU:]
        vn = (_rms(v, SGU)[0] * gsv_ref[...]).astype(BF16)
        tri = _tril()
        wsp_m = [jnp.where(tri, wsp_ref[h], 0.0).astype(BF16) for h in range(HEADS)]
        gm = u * _sgu_gate(vn, wsp_m, bsp_ref[...])
        gn = _rms(gm, SGU)[0] * g[:, 512:768]
        po = _dot(m_ref[...].astype(BF16), wbd_ref[...]) * psc_ref[...]
        pn = _rms(po, POOL)[0] * g[:, 768:]
        mix = jnp.concatenate([an, gn, pn], axis=1).astype(BF16)
        mix_ref[...] = mix
        x1_ref[...] = x_ref[...] + _dot(mix, wout_ref[...])

    row = lambda w, j: pl.BlockSpec((tm, w), lambda i: (i, j))
    return pl.pallas_call(
        body, name=name, grid=(s // tm,),
        in_specs=[row(512, 0), row(512, 1), row(POOL, 0), row(D, 0),
                  _acc((HEADS, CHUNK, CHUNK)), _acc((CHUNK, SGU)), _acc((POOL, POOL)), _acc((1, POOL)),
                  _acc((1, SGU)), _acc((1, D)), _res((D, D))],
        out_specs=[row(D, 0), row(D, 0)],
        out_shape=[jax.ShapeDtypeStruct((s, D), F32), jax.ShapeDtypeStruct((s, D), BF16)],
        compiler_params=_cp(("parallel",), VMEM_LIMIT),
    )(o, z, m, x, wsp, bsp, wbd, psc, gsv, gout, wout)


def _ffn_fwd(x1, g, wg, wu, wd, name):
    s = x1.shape[0]
    tm = _tile(s, 256)

    def body(x_ref, g_ref, wg_ref, wu_ref, wd_ref, x2_ref, a_ref, b_ref, h_ref):
        x = x_ref[...]
        h = (_rms(x, D)[0] * g_ref[...]).astype(BF16)
        h_ref[...] = h
        acc = jnp.zeros((tm, D), F32)
        for k in range(CHIPS):
            a = _dot_nt(h, wg_ref[k])
            b = _dot_nt(h, wu_ref[k])
            a_ref[k] = a
            b_ref[k] = b
            acc = acc + _dot((a * jax.nn.sigmoid(a) * b).astype(BF16), wd_ref[k])
        x2_ref[...] = x + acc

    row = lambda w: pl.BlockSpec((tm, w), lambda i: (i, 0))
    hrow = pl.BlockSpec((CHIPS, tm, SH), lambda i: (0, i, 0))
    hshape = jax.ShapeDtypeStruct((CHIPS, s, SH), F32)
    return pl.pallas_call(
        body, name=name, grid=(s // tm,),
        in_specs=[row(D), _acc((1, D)), _res((CHIPS, SH, D)), _res((CHIPS, SH, D)), _res((CHIPS, SH, D))],
        out_specs=[row(D), hrow, hrow, row(D)],
        out_shape=[jax.ShapeDtypeStruct((s, D), F32), hshape, hshape, jax.ShapeDtypeStruct((s, D), BF16)],
        compiler_params=_cp(("parallel",), VMEM_LIMIT),
    )(x1, g, wg, wu, wd)


def _loss_grad(y, tgt):
    s = y.shape[0]
    tm = _tile(s, 512)

    def body(y_ref, t_ref, dy_ref, l_ref):
        e = y_ref[...] - t_ref[...]
        dy_ref[...] = e * (1.0 / D)
        sq = jnp.sum(e * e, axis=0, keepdims=True)
        part = sq[:, :LANES]
        for c in range(1, D // LANES):
            part = part + sq[:, c * LANES:(c + 1) * LANES]
        _accumulate(l_ref, part, pl.program_id(0) == 0)

    row = pl.BlockSpec((tm, D), lambda i: (i, 0))
    return pl.pallas_call(
        body, name="loss_grad", grid=(s // tm,),
        in_specs=[row, row], out_specs=[row, _acc((1, LANES))],
        out_shape=[jax.ShapeDtypeStruct((s, D), F32), jax.ShapeDtypeStruct((1, LANES), F32)],
        compiler_params=_cp(("arbitrary",)),
    )(y, tgt)


def _wgrad(a, b, name):
    s, k = a.shape
    n = b.shape[1]
    half = lambda v: v if v <= 1408 else v // 2
    kb, nb, tt = half(k), half(n), _tile(s, 1024)

    def body(a_ref, b_ref, o_ref):
        _accumulate(o_ref, _dot_tn(a_ref[...].astype(BF16), b_ref[...].astype(BF16)), pl.program_id(2) == 0)

    return pl.pallas_call(
        body, name=name, grid=(k // kb, n // nb, s // tt),
        in_specs=[pl.BlockSpec((tt, kb), lambda i, j, t: (t, i)), pl.BlockSpec((tt, nb), lambda i, j, t: (t, j))],
        out_specs=pl.BlockSpec((kb, nb), lambda i, j, t: (i, j)),
        out_shape=jax.ShapeDtypeStruct((k, n), F32),
        compiler_params=_cp(("parallel", "parallel", "arbitrary"), VMEM_LIMIT),
    )(a, b)


def _wgrad_rows(a, b, name):
    s, n = a.shape[1:]
    nn = b.shape[1]
    tt = _tile(s, 1024)

    def body(a_ref, b_ref, o_ref):
        _accumulate0(o_ref, _dot_tn(a_ref[0].astype(BF16), b_ref[...].astype(BF16)), pl.program_id(1) == 0)

    return pl.pallas_call(
        body, name=name, grid=(CHIPS, s // tt),
        in_specs=[pl.BlockSpec((1, tt, n), lambda c, t: (c, t, 0)), pl.BlockSpec((tt, nn), lambda c, t: (t, 0))],
        out_specs=pl.BlockSpec((1, n, nn), lambda c, t: (c, 0, 0)),
        out_shape=jax.ShapeDtypeStruct((CHIPS, n, nn), F32),
        compiler_params=_cp(("parallel", "arbitrary"), VMEM_LIMIT),
    )(a, b)


def _ffn_bwd(dx2, x1, a, b, g, wg, wu, wd, name):
    s = x1.shape[0]
    tm = _tile(s, 256)

    def body(dx2_ref, x_ref, a_ref, b_ref, g_ref, wg_ref, wu_ref, wd_ref,
             dx1_ref, hid_ref, da_ref, db_ref, dg_ref):
        dx2 = dx2_ref[...]
        dyb = dx2.astype(BF16)
        dh = jnp.zeros((tm, D), F32)
        for k in range(CHIPS):
            av, bv = a_ref[k], b_ref[k]
            dhid = _dot_nt(dyb, wd_ref[k])
            sig = jax.nn.sigmoid(av)
            sa = av * sig
            hid_ref[k] = (sa * bv).astype(BF16)
            dbv = (dhid * sa).astype(BF16)
            dav = (dhid * bv * (sig * (1.0 + av * (1.0 - sig)))).astype(BF16)
            db_ref[k] = dbv
            da_ref[k] = dav
            dh = dh + _dot(dav, wg_ref[k]) + _dot(dbv, wu_ref[k])
        xn, r = _rms(x_ref[...], D)
        dxr, dg = _rms_bwd(xn, r, g_ref[...], dh, D)
        dx1_ref[...] = dx2 + dxr
        _accumulate(dg_ref, dg, pl.program_id(0) == 0)

    row = lambda w: pl.BlockSpec((tm, w), lambda i: (i, 0))
    hrow = pl.BlockSpec((CHIPS, tm, SH), lambda i: (0, i, 0))
    hid = jax.ShapeDtypeStruct((CHIPS, s, SH), BF16)
    return pl.pallas_call(
        body, name=name, grid=(s // tm,),
        in_specs=[row(D), row(D), hrow, hrow, _acc((1, D)), _res((CHIPS, SH, D)), _res((CHIPS, SH, D)),
                  _res((CHIPS, SH, D))],
        out_specs=[row(D), hrow, hrow, hrow, _acc((1, D))],
        out_shape=[jax.ShapeDtypeStruct((s, D), F32), hid, hid, hid, jax.ShapeDtypeStruct((1, D), F32)],
        compiler_params=_cp(("arbitrary",), VMEM_LIMIT),
    )(dx2, x1, a, b, g, wg, wu, wd)


def _mix_out_bwd(dx1, o, z, m, wsp, bsp, wbd, psc, gsv, gout, wout, name):
    s = dx1.shape[0]
    tm = _tile(s, 512)

    def body(dx1_ref, o_ref, uv_ref, m_ref, wsp_ref, bsp_ref, wbd_ref, psc_ref, gsv_ref, gout_ref, wout_ref,
             do_ref, dl_ref, duv_ref, dm_ref, dgo_ref, dgsv_ref, dpsc_ref, dwsp_ref, dbsp_ref, dwbd_ref):
        first = pl.program_id(0) == 0
        g = gout_ref[...]
        dmix = _dot_nt(dx1_ref[...].astype(BF16), wout_ref[...])
        o = o_ref[...]
        on, ro = _rms(o, HEADS * VH)
        do, dga = _rms_bwd(on, ro, g[:, :512], dmix[:, :512], HEADS * VH)
        for h in range(HEADS):
            sl = slice(h * VH, (h + 1) * VH)
            do_ref[h] = do[:, sl].astype(BF16)
            dl_ref[h] = jnp.broadcast_to(jnp.sum(do[:, sl] * o[:, sl], axis=-1, keepdims=True), (tm, LANES))
        uv = uv_ref[...]
        u, v = uv[:, :SGU], uv[:, SGU:]
        vx, rv = _rms(v, SGU)
        vn = (vx * gsv_ref[...]).astype(BF16)
        tri = _tril()
        wsp_m = [jnp.where(tri, wsp_ref[h], 0.0).astype(BF16) for h in range(HEADS)]
        zc = _sgu_gate(vn, wsp_m, bsp_ref[...])
        gm = u * zc
        gmn, rg = _rms(gm, SGU)
        dgm, dgg = _rms_bwd(gmn, rg, g[:, 512:768], dmix[:, 512:768], SGU)
        du = dgm * zc
        dzc = dgm * u
        dvn_parts = []
        dbsp = jnp.zeros((CHUNK, SGU), F32)
        dwsp = [jnp.zeros((CHUNK, CHUNK), F32) for _ in range(HEADS)]
        for cidx in range(tm // CHUNK):
            rs = slice(cidx * CHUNK, (cidx + 1) * CHUNK)
            dzc_c = dzc[rs]
            dbsp = dbsp + dzc_c
            dzb = dzc_c.astype(BF16)
            vc = vn[rs]
            dvn_c = jnp.zeros((CHUNK, SGU), F32)
            for h in range(HEADS):
                hm = _head_mask(h)
                dvn_c = dvn_c + jnp.where(hm, _dot_tn(wsp_m[h], dzb), 0.0)
                dwsp[h] = dwsp[h] + _dot_nt(jnp.where(hm, dzc_c, 0.0).astype(BF16), vc)
            dvn_parts.append(dvn_c)
        dvn = jnp.concatenate(dvn_parts, axis=0)
        dv, dgsv = _rms_bwd(vx, rv, gsv_ref[...], dvn, SGU)
        duv_ref[...] = jnp.concatenate([du, dv], axis=1).astype(BF16)
        mb = m_ref[...].astype(BF16)
        pw = _dot(mb, wbd_ref[...])
        po = pw * psc_ref[...]
        pon, rp = _rms(po, POOL)
        dpo, dgp = _rms_bwd(pon, rp, g[:, 768:], dmix[:, 768:], POOL)
        dpw = (dpo * psc_ref[...]).astype(BF16)
        dm_ref[...] = _dot_nt(dpw, wbd_ref[...])
        _accumulate(dgo_ref, jnp.concatenate([dga, dgg, dgp], axis=1), first)
        _accumulate(dgsv_ref, dgsv, first)
        _accumulate(dpsc_ref, jnp.sum(dpo * pw, axis=0, keepdims=True), first)
        _accumulate(dbsp_ref, dbsp, first)
        _accumulate(dwbd_ref, _dot_tn(mb, dpw), first)
        for h in range(HEADS):
            val = jnp.where(tri, dwsp[h], 0.0)

            @pl.when(first)
            def _(val=val, h=h):
                dwsp_ref[h] = val

            @pl.when(jnp.logical_not(first))
            def _(val=val, h=h):
                dwsp_ref[h] += val

    row = lambda w, j: pl.BlockSpec((tm, w), lambda i: (i, j))
    hspec = pl.BlockSpec((HEADS, tm, HP), lambda i: (0, i, 0))
    return pl.pallas_call(
        body, name=name, grid=(s // tm,),
        in_specs=[row(D, 0), row(512, 0), row(512, 1), row(POOL, 0),
                  _acc((HEADS, CHUNK, CHUNK)), _acc((CHUNK, SGU)),
                  _acc((POOL, POOL)), _acc((1, POOL)), _acc((1, SGU)), _acc((1, D)), _res((D, D))],
        out_specs=[hspec, hspec, row(512, 0), row(POOL, 0), _acc((1, D)), _acc((1, SGU)), _acc((1, POOL)),
                   _acc((HEADS, CHUNK, CHUNK)), _acc((CHUNK, SGU)), _acc((POOL, POOL))],
        out_shape=[jax.ShapeDtypeStruct((HEADS, s, HP), BF16), jax.ShapeDtypeStruct((HEADS, s, LANES), F32),
                   jax.ShapeDtypeStruct((s, 512), BF16), jax.ShapeDtypeStruct((s, POOL), F32),
                   jax.ShapeDtypeStruct((1, D), F32), jax.ShapeDtypeStruct((1, SGU), F32),
                   jax.ShapeDtypeStruct((1, POOL), F32), jax.ShapeDtypeStruct((HEADS, CHUNK, CHUNK), F32),
                   jax.ShapeDtypeStruct((CHUNK, SGU), F32), jax.ShapeDtypeStruct((POOL, POOL), F32)],
        compiler_params=_cp(("arbitrary",), VMEM_LIMIT),
    )(dx1, o, z, m, wsp, bsp, wbd, psc, gsv, gout, wout)


def _attn_bwd(q, k, v, do, lse, delta, name):
    s = q.shape[1]
    tq = tk = _tile(s, 512)
    nq = s // tq
    wide = ATT_WIDE * tq if s % (ATT_WIDE * tq) == 0 else tq

    def body(q_ref, k_ref, v_ref, do_ref, lse_ref, dl_ref, dq_ref, dk_ref, dv_ref):
        j = pl.program_id(1)

        @pl.when(j == 0)
        def _():
            dq_ref[...] = jnp.zeros_like(dq_ref)

        kj, vj = k_ref[0], v_ref[0]
        rh = tq // ATT_SPLIT

        def blk(start, rows, dk, dv, masked):
            offs = [pl.multiple_of(start + g * rh, rh) for g in range(rows // rh)]
            qs = [q_ref[0, pl.ds(off, rh), :] for off in offs]
            dos = [do_ref[0, pl.ds(off, rh), :] for off in offs]
            scs = [_dot_nt(qi, kj) for qi in qs]
            dps = [_dot_nt(doi, vj) for doi in dos]
            for g, off in enumerate(offs):
                lse_i = lse_ref[0, pl.ds(off, rh), :][:, :1]
                dl_i = dl_ref[0, pl.ds(off, rh), :][:, :1]
                sc = _causal_mask(scs[g], g * rh) if masked else scs[g]
                p = jnp.exp2(sc * EXP2_C - lse_i)
                ds = (p * (dps[g] - dl_i)).astype(BF16)
                dv = dv + _dot_tn(p.astype(BF16), dos[g])
                dk = dk + _dot_tn(ds, qs[g])
                dq_ref[0, pl.ds(off, rh), :] += _dot(ds, kj) * SCALE
            return dk, dv

        per = wide // tq
        zero = jnp.zeros((tk, HP), F32)
        dk, dv = blk(j * tq, tq, zero, zero, True)
        first_wide = (j + per) // per
        dk, dv = lax.fori_loop(j + 1, jnp.minimum(first_wide * per, nq), lambda i, c: blk(i * tq, tq, *c, False), (dk, dv))
        dk, dv = lax.fori_loop(first_wide, nq // per, lambda i, c: blk(i * wide, wide, *c, False), (dk, dv))
        dk_ref[0] = dk * SCALE
        dv_ref[0] = dv

    full = lambda: pl.BlockSpec((1, s, HP), lambda h, j: (h, 0, 0))
    blk_spec = lambda: pl.BlockSpec((1, tk, HP), lambda h, j: (h, j, 0))
    out = jax.ShapeDtypeStruct((HEADS, s, HP), F32)
    return pl.pallas_call(
        body, name=name, grid=(HEADS, s // tk),
        in_specs=[full(), blk_spec(), blk_spec(), full(), full(), full()],
        out_specs=[full(), blk_spec(), blk_spec()], out_shape=[out] * 3,
        compiler_params=_cp(("parallel", "arbitrary"), VMEM_LIMIT),
    )(q, k, v, do, lse, delta)


def _mla_prep_bwd(dq, dk, dv, z, tabs, gql, gkv, gq, gk, wq, wk, wv, name):
    s = z.shape[0]
    tm = _tile(s, 512)

    def body(dq_ref, dk_ref, dv_ref, ql_ref, kv_ref, kr_ref, c_ref, sa_ref, sb_ref, gql_ref, gkv_ref, gq_ref, gk_ref,
             wq_ref, wk_ref, wv_ref,
             dz_ref, qn_ref, kvn_ref, dqr_ref, dkr_ref, dvr_ref, dgql_ref, dgkv_ref, dgq_ref, dgk_ref):
        first = pl.program_id(0) == 0
        qx, rq = _rms(ql_ref[...], QL)
        qn = (qx * gql_ref[...]).astype(BF16)
        kx, rk = _rms(kv_ref[...], KVL)
        kvn = (kx * gkv_ref[...]).astype(BF16)
        qn_ref[...] = qn
        kvn_ref[...] = kvn
        qraw = _dot(qn, wq_ref[...])
        kraw = _dot(kvn, wk_ref[...])
        kr = kr_ref[...]
        c, sa, sb = c_ref[...], sa_ref[...], sb_ref[...]
        lane = lax.broadcasted_iota(jnp.int32, (tm, HP), 1)
        rope_lanes = (lane >= NOPE) & (lane < QK)
        dkrope = jnp.zeros((tm, HP), F32)
        dgq = jnp.zeros((1, HP), F32)
        dgk = jnp.zeros((1, HP), F32)
        for h in range(HEADS):
            sl = slice(h * HP, (h + 1) * HP)
            xn, r = _rms(qraw[:, sl], QK)
            dx, dg = _rms_bwd(xn, r, gq_ref[...], _rope_t(dq_ref[h], c, sa, sb), QK)
            dqr_ref[:, sl] = dx.astype(BF16)
            dgq = dgq + dg
            xn, r = _rms(kraw[:, sl] + kr, QK)
            dx, dg = _rms_bwd(xn, r, gk_ref[...], _rope_t(dk_ref[h], c, sa, sb), QK)
            dkr_ref[:, sl] = dx.astype(BF16)
            dgk = dgk + dg
            dkrope = dkrope + jnp.where(rope_lanes, dx, 0.0)
            dvr_ref[:, sl] = dv_ref[h].astype(BF16)
        dqn = _dot_nt(dqr_ref[...], wq_ref[...])
        dql, dgql = _rms_bwd(qx, rq, gql_ref[...], dqn, QL)
        dkvn = _dot_nt(dkr_ref[...], wk_ref[...]) + _dot_nt(dvr_ref[...], wv_ref[...])
        dkv, dgkv = _rms_bwd(kx, rk, gkv_ref[...], dkvn, KVL)
        dz_ref[...] = jnp.concatenate([dql, dkv, dkrope], axis=1).astype(BF16)
        _accumulate(dgql_ref, dgql, first)
        _accumulate(dgkv_ref, dgkv, first)
        _accumulate(dgq_ref, dgq, first)
        _accumulate(dgk_ref, dgk, first)

    row = lambda w, j: pl.BlockSpec((tm, w), lambda i: (i, j))
    hspec = pl.BlockSpec((HEADS, tm, HP), lambda i: (0, i, 0))
    sd = lambda w, dt: jax.ShapeDtypeStruct((s, w), dt)
    return pl.pallas_call(
        body, name=name, grid=(s // tm,),
        in_specs=[hspec, hspec, hspec, row(QL, 0), row(KVL, 2), row(HP, 3), row(HP, 0), row(HP, 0), row(HP, 0),
                  _acc((1, QL)), _acc((1, KVL)), _acc((1, HP)), _acc((1, HP)),
                  _acc((QL, HEADS * HP)), _acc((KVL, HEADS * HP)), _acc((KVL, HEADS * HP))],
        out_specs=[row(512, 0), row(QL, 0), row(KVL, 0), row(512, 0), row(512, 0), row(512, 0),
                   _acc((1, QL)), _acc((1, KVL)), _acc((1, HP)), _acc((1, HP))],
        out_shape=[sd(512, BF16), sd(QL, BF16), sd(KVL, BF16), sd(512, BF16), sd(512, BF16), sd(512, BF16),
                   jax.ShapeDtypeStruct((1, QL), F32), jax.ShapeDtypeStruct((1, KVL), F32),
                   jax.ShapeDtypeStruct((1, HP), F32), jax.ShapeDtypeStruct((1, HP), F32)],
        compiler_params=_cp(("arbitrary",), VMEM_LIMIT),
    )(dq, dk, dv, z, z, z, *tabs, gql, gkv, gq, gk, wq, wk, wv)


def _in_proj_bwd(dzm, duv, dp, x, dx1, g, win, name):
    s = x.shape[0]
    tm = _tile(s, 512)

    def body(dzm_ref, duv_ref, dp_ref, x_ref, dx1_ref, g_ref, w_ref, dx_ref, dg_ref):
        dh = _dot_nt(dzm_ref[...], w_ref[:, 0:512]) + _dot_nt(duv_ref[...], w_ref[:, 512:1024]) \
            + _dot_nt(dp_ref[...], w_ref[:, 1024:IN_P])
        xn, r = _rms(x_ref[...], D)
        dxr, dg = _rms_bwd(xn, r, g_ref[...], dh, D)
        dx_ref[...] = dx1_ref[...] + dxr
        _accumulate(dg_ref, dg, pl.program_id(0) == 0)

    row = lambda w: pl.BlockSpec((tm, w), lambda i: (i, 0))
    return pl.pallas_call(
        body, name=name, grid=(s // tm,),
        in_specs=[row(512), row(512), row(POOL), row(D), row(D), _acc((1, D)), _res((D, IN_P))],
        out_specs=[row(D), _acc((1, D))],
        out_shape=[jax.ShapeDtypeStruct((s, D), F32), jax.ShapeDtypeStruct((1, D), F32)],
        compiler_params=_cp(("arbitrary",), VMEM_LIMIT),
    )(dzm, duv, dp, x, dx1, g, win)


def _adamw(w, g0, g1, m, v, name):
    _, r, c = w.shape
    tr = _row_tile(r, 512)
    c1 = 1.0 - B1 ** STEP
    c2 = 1.0 - B2 ** STEP

    def body(w_ref, g0_ref, g1_ref, m_ref, v_ref, g_ref, d_ref, nm_ref, nv_ref):
        gv = jnp.where(pl.program_id(0) == 0, g0_ref[...], g1_ref[...])
        g_ref[0] = gv
        nm = B1 * m_ref[0] + (1.0 - B1) * gv
        nv = B2 * v_ref[0] + (1.0 - B2) * (gv * gv)
        nm_ref[0] = nm
        nv_ref[0] = nv
        d_ref[0] = -LR * ((nm / c1) / (jnp.sqrt(nv / c2) + ADAM_EPS) + WD * w_ref[0])

    spec = pl.BlockSpec((1, tr, c), lambda l, i: (l, i, 0))
    out = jax.ShapeDtypeStruct((DEPTH, r, c), F32)
    return pl.pallas_call(
        body, name=name, grid=(DEPTH, r // tr),
        in_specs=[spec, pl.BlockSpec((tr, c), lambda l, i: (i * (1 - l), 0)), pl.BlockSpec((tr, c), lambda l, i: (i * l, 0)),
                  spec, spec],
        out_specs=[spec] * 4, out_shape=[out] * 4, compiler_params=_cp(("parallel", "parallel")),
    )(w, g0, g1, m, v)


ANY = pl.BlockSpec(memory_space=pl.ANY)


def _place():
    x, y, c = lax.axis_index("x"), lax.axis_index("y"), lax.axis_index("c")
    chips = [(1 - x, y), (x, 1 - y), (1 - x, 1 - y)]
    return x, y, c, chips


def _half_rows(ref, lead, hh, half, align):
    rows = pl.ds(pl.multiple_of(hh * half, align), half)
    return ref.at[rows, :] if lead is None else ref.at[lead, rows, :]


def _row_align(dtype):
    return 16 if dtype == BF16 else 8


def _sems(n):
    return [pltpu.SemaphoreType.DMA((n,)), pltpu.SemaphoreType.DMA((n,)), pltpu.SemaphoreType.DMA((n,))]


def _comm_call(body, ins, out_shapes, nsems, name):
    return pl.pallas_call(
        body, name=name, in_specs=[ANY] * len(ins), out_specs=[ANY] * len(out_shapes), out_shape=out_shapes,
        scratch_shapes=_sems(nsems), compiler_params=pltpu.CompilerParams(has_side_effects=True),
    )(*ins)


def _all_gather_chips(shards, name):
    n = len(shards)
    halves = [a.shape[0] // 2 for a in shards]
    aligns = [_row_align(a.dtype) for a in shards]
    assert all(h % al == 0 for h, al in zip(halves, aligns))

    def body(*refs):
        ins, outs, (send_sems, recv_sems, _) = refs[:n], refs[n:2 * n], refs[2 * n:]
        x, y, c, chips = _place()
        me = 2 * x + y
        sibling = (x, y, 1 - c)

        def copy(sem, src, dst, to):
            return pltpu.make_async_remote_copy(src_ref=src, dst_ref=dst, send_sem=send_sems.at[sem],
                                                recv_sem=recv_sems.at[sem], device_id=to, device_id_type=MESH)

        first, passed = [], []
        for a in range(n):
            my_half = _half_rows(ins[a], None, c, halves[a], aligns[a])
            for j, (cx, cy) in enumerate(chips):
                cp = copy(6 * a + j, my_half, _half_rows(outs[a], me, c, halves[a], aligns[a]), (cx, cy, c))
                cp.start()
                first.append(cp)
        for a in range(n):
            for j, (cx, cy) in enumerate(chips):
                landed = _half_rows(outs[a], 2 * cx + cy, c, halves[a], aligns[a])
                copy(6 * a + j, landed, landed, (cx, cy, c)).wait_recv()
                fwd = copy(6 * a + 3 + j, landed, landed, sibling)
                fwd.start()
                passed.append(fwd)
        for a in range(n):
            for j, (cx, cy) in enumerate(chips):
                other = _half_rows(outs[a], 2 * cx + cy, 1 - c, halves[a], aligns[a])
                copy(6 * a + 3 + j, other, other, sibling).wait_recv()
        for cp in first + passed:
            cp.wait_send()

    lands = _comm_call(body, shards, [jax.ShapeDtypeStruct((CHIPS,) + a.shape, a.dtype) for a in shards], 6 * n, name)
    return _with_own(lands, shards)


def _with_own(lands, shards):
    me = 2 * lax.axis_index("x") + lax.axis_index("y")
    return [lax.dynamic_update_slice(g, a[None], (me, 0, 0)) for g, a in zip(lands, shards)]


def _pair_swap_halves(arrs, name):
    n = len(arrs)
    halves = [a.shape[1] // 2 for a in arrs]

    def body(*refs):
        ins, outs, (send_sems, recv_sems, _) = refs[:n], refs[n:2 * n], refs[2 * n:]
        x, y, c, _ = _place()
        cps = []
        for a in range(n):
            src = ins[a].at[:, pl.ds(pl.multiple_of((1 - c) * halves[a], 8), halves[a]), :]
            cp = pltpu.make_async_remote_copy(src_ref=src, dst_ref=outs[a], send_sem=send_sems.at[a],
                                              recv_sem=recv_sems.at[a], device_id=(x, y, 1 - c), device_id_type=MESH)
            cp.start()
            cps.append(cp)
        for cp in cps:
            cp.wait()

    return _comm_call(body, arrs, [jax.ShapeDtypeStruct((CHIPS, h, a.shape[2]), a.dtype) for a, h in zip(arrs, halves)],
                      n, name)


def _pair_add(full, got, cidx, name):
    _, half, cols = got.shape
    tr = _row_tile(half, 256)
    nt = half // tr

    grid_spec = pltpu.PrefetchScalarGridSpec(
        num_scalar_prefetch=1, grid=(CHIPS, nt),
        in_specs=[pl.BlockSpec((1, tr, cols), lambda k, r, c_ref: (k, c_ref[0] * nt + r, 0)),
                  pl.BlockSpec((1, tr, cols), lambda k, r, c_ref: (k, r, 0))],
        out_specs=pl.BlockSpec((1, tr, cols), lambda k, r, c_ref: (k, r, 0)))

    def body(c_ref, a_ref, b_ref, o_ref):
        o_ref[...] = a_ref[...] + b_ref[...]

    return pl.pallas_call(
        body, name=name, grid_spec=grid_spec, out_shape=jax.ShapeDtypeStruct(got.shape, got.dtype),
        compiler_params=_cp(("parallel", "parallel")),
    )(cidx, full, got)


def _chip_scatter(parts, name):
    n = len(parts)

    def body(*refs):
        ins, outs, (send_sems, recv_sems, _) = refs[:n], refs[n:2 * n], refs[2 * n:]
        cps = _remote_copies(_scatter_pairs, ins, outs, send_sems, recv_sems)
        for cp in cps:
            cp.start()
        for cp in cps:
            cp.wait()

    return _comm_call(body, parts, [jax.ShapeDtypeStruct((3,) + a.shape[1:], a.dtype) for a in parts], 3 * n, name)


def _pair_join(arrs, name):
    n = len(arrs)
    halves = [a.shape[0] // 2 for a in arrs]

    def body(*refs):
        outs, (send_sems, recv_sems, _) = refs[n:2 * n], refs[2 * n:]
        x, y, c, _ = _place()
        cps = []
        for a in range(n):
            mine = _half_rows(outs[a], None, c, halves[a], 8)
            cp = pltpu.make_async_remote_copy(src_ref=mine, dst_ref=mine, send_sem=send_sems.at[a], recv_sem=recv_sems.at[a],
                                              device_id=(x, y, 1 - c), device_id_type=MESH)
            cp.start()
            cps.append(cp)
        for cp in cps:
            cp.wait()

    return pl.pallas_call(
        body, name=name, in_specs=[ANY] * n, out_specs=[ANY] * n,
        out_shape=[jax.ShapeDtypeStruct(a.shape, a.dtype) for a in arrs],
        input_output_aliases={i: i for i in range(n)}, scratch_shapes=_sems(n),
        compiler_params=pltpu.CompilerParams(has_side_effects=True),
    )(*arrs)


HBM = pl.BlockSpec(memory_space=pltpu.HBM)
SEM = pl.BlockSpec(memory_space=pltpu.SEMAPHORE)
DATAFLOW = pltpu.SideEffectType.DATAFLOW_SIDE_EFFECTING


def _remote_copies(pairs, ins, lands, send_sems, recv_sems):
    return [pltpu.make_async_remote_copy(src_ref=src, dst_ref=dst, send_sem=send_sems.at[i], recv_sem=recv_sems.at[i],
                                         device_id=to, device_id_type=MESH)
            for i, (src, dst, to) in enumerate(pairs(ins, lands))]


def _split_start(srcs, land_shapes, ncopies, pairs, name):
    n, m = len(srcs), len(land_shapes)

    def body(*refs):
        ins, lands = refs[:n], refs[n:n + m]
        send_sems, recv_sems, token = refs[n + m], refs[n + m + 1], refs[-1]
        for cp in _remote_copies(pairs, ins, lands, send_sems, recv_sems):
            cp.start()
        token[...] = jnp.zeros_like(token)

    hbm = lambda a: pltpu.with_memory_space_constraint(a, pltpu.HBM)
    lands = [hbm(lax.empty(s.shape, s.dtype)) for s in land_shapes]
    thru = [pltpu.HBM(a.shape, a.dtype) for a in list(srcs) + lands]
    out = pl.pallas_call(
        body, name=name,
        out_shape=(pltpu.SemaphoreType.DMA((ncopies,)), pltpu.SemaphoreType.DMA((ncopies,)), *thru,
                   jax.ShapeDtypeStruct((8, LANES), F32)),
        in_specs=[HBM] * (n + m), out_specs=(SEM, SEM, *[HBM] * (n + m), pl.BlockSpec(memory_space=pltpu.VMEM)),
        input_output_aliases={i: 2 + i for i in range(n + m)},
        compiler_params=pltpu.CompilerParams(has_side_effects=DATAFLOW),
    )(*[hbm(a) for a in srcs], *lands)
    return out[0], out[1], list(out[2:2 + n]), list(out[2 + n:2 + n + m]), out[-1]


def _split_wait(send_sems, recv_sems, srcs, lands, after, pairs, name):
    n, m = len(srcs), len(lands)

    def body(*refs):
        ins, lands_ = refs[:n], refs[n:n + m]
        for cp in _remote_copies(pairs, ins, lands_, refs[n + m], refs[n + m + 1]):
            cp.wait_send()
            cp.wait_recv()

    out = pl.pallas_call(
        body, name=name, out_shape=tuple(pltpu.HBM(a.shape, a.dtype) for a in list(srcs) + list(lands)),
        in_specs=[HBM] * (n + m) + [SEM, SEM, ANY], out_specs=tuple([HBM] * (n + m)),
        input_output_aliases={i: i for i in range(n + m)},
        compiler_params=pltpu.CompilerParams(has_side_effects=DATAFLOW),
    )(*srcs, *lands, send_sems, recv_sems, after)
    return list(out[:n]), list(out[n:])


def _gather_pairs(halves, aligns):
    def pairs(ins, lands):
        x, y, c, chips = _place()
        me = 2 * x + y
        return [(_half_rows(ins[a], None, c, halves[a], aligns[a]), _half_rows(lands[a], me, c, halves[a], aligns[a]),
                 (cx, cy, c)) for a in range(len(ins)) for cx, cy in chips]
    return pairs


def _scatter_pairs(ins, lands):
    x, y, c, chips = _place()
    return [(ins[a].at[2 * cx + cy], lands[a].at[j], (cx, cy, c)) for a in range(len(ins)) for j, (cx, cy) in enumerate(chips)]


def _gather_finish(shards, lands, name):
    n = len(shards)
    halves = [a.shape[0] // 2 for a in shards]
    aligns = [_row_align(a.dtype) for a in shards]

    def body(*refs):
        outs, (send_sems, recv_sems, _) = refs[n:2 * n], refs[2 * n:]
        x, y, c, chips = _place()
        passed = []
        for a in range(n):
            for j, (cx, cy) in enumerate(chips):
                landed = _half_rows(outs[a], 2 * cx + cy, c, halves[a], aligns[a])
                cp = pltpu.make_async_remote_copy(src_ref=landed, dst_ref=landed, send_sem=send_sems.at[3 * a + j],
                                                  recv_sem=recv_sems.at[3 * a + j], device_id=(x, y, 1 - c),
                                                  device_id_type=MESH)
                cp.start()
                passed.append(cp)
        for a in range(n):
            for j, (cx, cy) in enumerate(chips):
                other = _half_rows(outs[a], 2 * cx + cy, 1 - c, halves[a], aligns[a])
                pltpu.make_async_remote_copy(src_ref=other, dst_ref=other, send_sem=send_sems.at[3 * a + j],
                                             recv_sem=recv_sems.at[3 * a + j], device_id=(x, y, 1 - c),
                                             device_id_type=MESH).wait_recv()
        for cp in passed:
            cp.wait_send()

    lands = pl.pallas_call(
        body, name=name, in_specs=[ANY] * n, out_specs=[ANY] * n,
        out_shape=[jax.ShapeDtypeStruct(a.shape, a.dtype) for a in lands],
        input_output_aliases={i: i for i in range(n)}, scratch_shapes=_sems(3 * n),
        compiler_params=pltpu.CompilerParams(has_side_effects=True),
    )(*lands)
    return _with_own(lands, shards)


def _sum_own_and_landed(own, landed, where, name):
    _, half, cols = own.shape
    tr = _row_tile(half, 256)
    nt = half // tr

    grid_spec = pltpu.PrefetchScalarGridSpec(
        num_scalar_prefetch=1, grid=(nt,),
        in_specs=[pl.BlockSpec((1, tr, cols), lambda r, w: (w[0], r, 0)),
                  pl.BlockSpec((3, tr, cols), lambda r, w: (0, r, 0))],
        out_specs=pl.BlockSpec((tr, cols), lambda r, w: (w[1] * nt + r, 0)))

    def body(w_ref, p_ref, q_ref, o_ref):
        o_ref[...] = ((p_ref[0] + q_ref[0]) + q_ref[1]) + q_ref[2]

    return pl.pallas_call(
        body, name=name, grid_spec=grid_spec, out_shape=jax.ShapeDtypeStruct((2 * half, cols), own.dtype),
        compiler_params=_cp(("parallel",)),
    )(where, own, landed)


BIG = [("w_in", (D, IN_W), 1), ("w_q_up", (QL, HEADS * QK), 1), ("w_kv_up", (KVL, HEADS * (NOPE + VH)), 1),
       ("w_out", (D, D), 0), ("w_gate", (D, HID), 1), ("w_up", (D, HID), 1), ("w_down", (HID, D), 0)]
SMALL = [("g_mix_norm", (D,)), ("g_q_lat", (QL,)), ("g_kv_lat", (KVL,)), ("g_q_head", (QK,)), ("g_k_head", (QK,)),
         ("g_sgu_v", (SGU,)), ("w_spatial", (HEADS, CHUNK, CHUNK)), ("b_spatial", (HEADS, CHUNK)),
         ("w_pool", (4, 64, 64)), ("pool_scale", (POOL,)), ("g_out_mla", (512,)), ("g_out_sgu", (SGU,)),
         ("g_out_pool", (POOL,)), ("g_ffn_norm", (D,))]
ORDER = ["g_mix_norm", "w_in", "g_q_lat", "w_q_up", "g_kv_lat", "w_kv_up", "g_q_head", "g_k_head", "g_sgu_v",
         "w_spatial", "b_spatial", "w_pool", "pool_scale", "g_out_mla", "g_out_sgu", "g_out_pool", "w_out",
         "g_ffn_norm", "w_gate", "w_up", "w_down"]
EARLY_BIG = ["w_in", "w_q_up", "w_kv_up"]
FFN_BIG = ["w_gate", "w_up", "w_down"]
LATE_BIG = ["w_out"] + FFN_BIG
DEPTH = 2
COLS = 1024
SMALL_N = sum(math.prod(s) for _, s in SMALL) * DEPTH
assert SMALL_N % CHIPS == 0
SMALL_ROWS = -(-(SMALL_N // CHIPS) // (16 * COLS)) * 16


def _unsplit_cols(g):
    return g.transpose(1, 0, 2).reshape(g.shape[1], CHIPS * g.shape[2])


def _split_cols(full):
    r, c = full.shape
    return full.reshape(r, CHIPS, c // CHIPS).transpose(1, 0, 2)


def _kernel_weights(g):
    win = _unsplit_cols(g["w_in"])
    zeros = lambda r, c: jnp.zeros((r, c), BF16)
    o2, o3, o4 = QL + KVL, QL + KVL + ROPE, QL + KVL + ROPE + 2 * SGU
    win_p = jnp.concatenate([win[:, :o2], zeros(D, NOPE), win[:, o2:o3], zeros(D, HP - QK), win[:, o3:o4], win[:, o4:]], axis=1)
    wq = _unsplit_cols(g["w_q_up"]).reshape(QL, HEADS, QK)
    wq_p = jnp.pad(wq, ((0, 0), (0, 0), (0, HP - QK))).reshape(QL, HEADS * HP)
    wkv = _unsplit_cols(g["w_kv_up"]).reshape(KVL, HEADS, NOPE + VH)
    wk_p = jnp.pad(wkv[:, :, :NOPE], ((0, 0), (0, 0), (0, HP - NOPE))).reshape(KVL, HEADS * HP)
    wv_p = wkv[:, :, NOPE:].reshape(KVL, HEADS * VH)
    return dict(win=win_p, wq=wq_p, wk=wk_p, wv=wv_p)


def _small_operands(p, l):
    row = lambda v: v.reshape(1, -1)
    pad = lambda v: jnp.pad(v, (0, HP - QK)).reshape(1, HP)
    wpool = p["w_pool"][l]
    wbd = jnp.zeros((POOL, POOL), F32)
    for g in range(4):
        wbd = lax.dynamic_update_slice(wbd, wpool[g], (g * 64, g * 64))
    return dict(
        g_mix=row(p["g_mix_norm"][l]), gql=row(p["g_q_lat"][l]), gkv=row(p["g_kv_lat"][l]),
        gq=pad(p["g_q_head"][l]), gk=pad(p["g_k_head"][l]), gsv=row(p["g_sgu_v"][l]),
        wsp=p["w_spatial"][l], bsp=jnp.repeat(p["b_spatial"][l].T, SGU // HEADS, axis=1),
        wbd=wbd.astype(BF16), psc=row(p["pool_scale"][l]),
        gout=jnp.concatenate([p["g_out_mla"][l], p["g_out_sgu"][l], p["g_out_pool"][l]]).reshape(1, D),
        g_ffn=row(p["g_ffn_norm"][l]))


def _big_grads(g):
    dwin = g["win"]
    o2 = QL + KVL
    gin = jnp.concatenate([dwin[:, :o2], dwin[:, o2 + NOPE:o2 + NOPE + ROPE], dwin[:, 512:]], axis=1)
    gq = g["wq"].reshape(QL, HEADS, HP)[:, :, :QK].reshape(QL, HEADS * QK)
    gk = g["wk"].reshape(KVL, HEADS, HP)[:, :, :NOPE]
    gv = g["wv"].reshape(KVL, HEADS, VH)
    gkv = jnp.concatenate([gk, gv], axis=2).reshape(KVL, HEADS * (NOPE + VH))
    return {"w_in": _split_cols(gin), "w_q_up": _split_cols(gq), "w_kv_up": _split_cols(gkv),
            "w_out": g["wout"].reshape(CHIPS, D // CHIPS, D), "w_gate": g["wg"], "w_up": g["wu"], "w_down": g["wd"]}


TRANSPOSED = ("w_gate", "w_up")


def _small_grads(g):
    go = g["gout"].reshape(-1)
    return {"g_mix_norm": g["g_mix"].reshape(-1), "g_q_lat": g["gql"].reshape(-1), "g_kv_lat": g["gkv"].reshape(-1),
            "g_q_head": g["gq"].reshape(-1)[:QK], "g_k_head": g["gk"].reshape(-1)[:QK], "g_sgu_v": g["gsv"].reshape(-1),
            "w_spatial": g["wsp"], "b_spatial": g["bsp"].reshape(CHUNK, HEADS, SGU // HEADS).sum(-1).T,
            "w_pool": jnp.stack([g["wbd"][i * 64:(i + 1) * 64, i * 64:(i + 1) * 64] for i in range(4)]),
            "pool_scale": g["psc"].reshape(-1), "g_out_mla": go[:512], "g_out_sgu": go[512:768],
            "g_out_pool": go[768:], "g_ffn_norm": g["g_ffn"].reshape(-1)}


def _pack_small_grads(small):
    sm = jnp.concatenate([small[l][n].reshape(-1) for l in range(DEPTH) for n, _ in SMALL]).reshape(CHIPS, SMALL_N // CHIPS)
    return jnp.pad(sm, ((0, 0), (0, SMALL_ROWS * COLS - SMALL_N // CHIPS))).reshape(CHIPS, SMALL_ROWS, COLS)


def _unpack_small_grads(gathered):
    flat = gathered.reshape(CHIPS, SMALL_ROWS * COLS)[:, :SMALL_N // CHIPS].reshape(-1)
    out, off = [], 0
    for _ in range(DEPTH):
        layer = {}
        for n, shape in SMALL:
            k = math.prod(shape)
            layer[n] = flat[off:off + k].reshape(shape)
            off += k
        out.append(layer)
    return out


def _layer_fwd(x, tabs, kw, late_weights, sp, l):
    t = f"_l{l}"
    z, hb = _in_proj_fwd(x, sp["g_mix"], kw["win"], "in_proj_fwd" + t)
    q, k, v = _mla_prep_fwd(z, tabs, sp["gql"], sp["gkv"], sp["gq"], sp["gk"], kw["wq"], kw["wk"], kw["wv"],
                            "mla_prep_fwd" + t)
    o, lse = _attn_fwd(q, k, v, "attn_fwd" + t)
    m = _pool_win_fwd(z, "pool_win_fwd" + t)
    wout, wg, wu, wd = late_weights(o)
    wout = wout.reshape(D, D)
    x1, mix = _mix_out_fwd(o, z, m, x, sp["wsp"], sp["bsp"], sp["wbd"], sp["psc"], sp["gsv"], sp["gout"], wout,
                           "mix_out_fwd" + t)
    x2, a, b, h2 = _ffn_fwd(x1, sp["g_ffn"], wg, wu, wd, "ffn_fwd" + t)
    saved = dict(x=x, z=z, hb=hb, q=q, k=k, v=v, o=o, lse=lse, m=m, x1=x1, mix=mix, a=a, b=b, h2=h2, wg=wg, wu=wu, wd=wd,
                 wout=wout)
    return x2, saved


def _layer_bwd(dx2, sv, tabs, kw, sp, l, ffn_hook):
    t = f"_l{l}"
    g = {}
    dx1, hid, da, db, g["g_ffn"] = _ffn_bwd(dx2, sv["x1"], sv["a"], sv["b"], sp["g_ffn"], sv["wg"], sv["wu"], sv["wd"],
                                            "ffn_bwd" + t)
    g["wd"] = _wgrad_rows(hid, dx2, "wgrad_down" + t)
    g["wg"] = _wgrad_rows(da, sv["h2"], "wgrad_gate" + t)
    g["wu"] = _wgrad_rows(db, sv["h2"], "wgrad_up" + t)
    gout = sp["gout"] + ffn_hook(g)
    do, delta, duv, dm, g["gout"], g["gsv"], g["psc"], g["wsp"], g["bsp"], g["wbd"] = _mix_out_bwd(
        dx1, sv["o"], sv["z"], sv["m"], sp["wsp"], sp["bsp"], sp["wbd"], sp["psc"], sp["gsv"], gout, sv["wout"],
        "mix_out_bwd" + t)
    g["wout"] = _wgrad(sv["mix"], dx1, "wgrad_out" + t)
    dp = _pool_win_bwd(dm, "pool_win_bwd" + t)
    dq, dk, dv = _attn_bwd(sv["q"], sv["k"], sv["v"], do, sv["lse"], delta, "attn_bwd" + t)
    dzm, qn, kvn, dqr, dkr, dvr, g["gql"], g["gkv"], g["gq"], g["gk"] = _mla_prep_bwd(
        dq, dk, dv, sv["z"], tabs, sp["gql"], sp["gkv"], sp["gq"], sp["gk"], kw["wq"], kw["wk"], kw["wv"],
        "mla_prep_bwd" + t)
    g["wq"] = _wgrad(qn, dqr, "wgrad_q_up" + t)
    g["wk"] = _wgrad(kvn, dkr, "wgrad_k_up" + t)
    g["wv"] = _wgrad(kvn, dvr, "wgrad_v_up" + t)
    dx, g["g_mix"] = _in_proj_bwd(dzm, duv, dp, sv["x"], dx1, sp["g_mix"], kw["win"], "in_proj_bwd" + t)
    g["win"] = jnp.concatenate([_wgrad(sv["hb"], dzm, "wgrad_in_a" + t), _wgrad(sv["hb"], duv, "wgrad_in_b" + t),
                                _wgrad(sv["hb"], dp, "wgrad_in_c" + t)], axis=1)
    return dx, g


def _rope_inv_freq():
    half = ROPE // 2
    inv = 1.0 / (ROPE_THETA ** (jnp.arange(half, dtype=F32) / half))
    return jnp.concatenate([jnp.zeros((NOPE,), F32), inv, inv, jnp.zeros((HP - QK,), F32)]).reshape(1, HP)


def kernel(x, positions, g_mix_norm, w_in, g_q_lat, w_q_up, g_kv_lat, w_kv_up, g_q_head, g_k_head, g_sgu_v, w_spatial, b_spatial, w_pool, pool_scale, g_out_mla, g_out_sgu, g_out_pool, w_out, g_ffn_norm, w_gate, w_up, w_down, loss_target, m_g_mix_norm, m_w_in, m_g_q_lat, m_w_q_up, m_g_kv_lat, m_w_kv_up, m_g_q_head, m_g_k_head, m_g_sgu_v, m_w_spatial, m_b_spatial, m_w_pool, m_pool_scale, m_g_out_mla, m_g_out_sgu, m_g_out_pool, m_w_out, m_g_ffn_norm, m_w_gate, m_w_up, m_w_down, v_g_mix_norm, v_w_in, v_g_q_lat, v_w_q_up, v_g_kv_lat, v_w_kv_up, v_g_q_head, v_g_k_head, v_g_sgu_v, v_w_spatial, v_b_spatial, v_w_pool, v_pool_scale, v_g_out_mla, v_g_out_sgu, v_g_out_pool, v_w_out, v_g_ffn_norm, v_w_gate, v_w_up, v_w_down):
    given = dict(locals())
    p = {n: given[n] for n in ORDER}
    view = lambda pre, n: jnp.swapaxes(given[pre + n], 1, 2) if n in TRANSPOSED else given[pre + n]
    seq = x.shape[1]
    cidx = lax.axis_index("c").astype(jnp.int32).reshape(1)
    where = jnp.stack([2 * lax.axis_index("x") + lax.axis_index("y"), lax.axis_index("c")]).astype(jnp.int32)
    shards = lambda names: [view("", n)[l].astype(BF16) for l, n in names]
    zero11 = lambda token: token[:1, :1]

    names_0a = [(0, n) for n in EARLY_BIG]
    names_0b = [(0, n) for n in LATE_BIG]
    names_1 = [(1, n) for n, _, _ in BIG]
    got_0a = dict(zip(EARLY_BIG, _all_gather_chips(shards(names_0a), "all_gather_w0a")))
    started = {}
    for tag, names in (("w0b", names_0b), ("w1", names_1)):
        sh = shards(names)
        pairs = _gather_pairs([a.shape[0] // 2 for a in sh], [_row_align(a.dtype) for a in sh])
        lands = [jax.ShapeDtypeStruct((CHIPS,) + a.shape, a.dtype) for a in sh]
        started[tag] = (sh, pairs) + _split_start(sh, lands, 3 * len(sh), pairs, "gather_start_" + tag)

    def arrived(tag, after):
        _, pairs, send, recv, srcs, lands, _ = started[tag]
        srcs, lands = _split_wait(send, recv, srcs, lands, after, pairs, "gather_wait_" + tag)
        return _gather_finish(srcs, lands, "gather_finish_" + tag)

    layer1 = {}

    def mix_weights(l, h):
        if l == 0:
            return got_0a
        layer1.update(zip([n for _, n in names_1], arrived("w1", h)))
        return layer1

    def late_weights(l, o):
        return arrived("w0b", o) if l == 0 else [layer1[n] for n in LATE_BIG]

    reducing, last = {}, {}

    def reduce_start(tag, arrs):
        theirs = _pair_swap_halves(arrs, "grad_pair_swap_" + tag)
        pair = [_pair_add(a, t, cidx, f"grad_pair_add_{tag}_{i}") for i, (a, t) in enumerate(zip(arrs, theirs))]
        lands = [jax.ShapeDtypeStruct((3,) + a.shape[1:], a.dtype) for a in pair]
        reducing[tag] = _split_start(pair, lands, 3 * len(pair), _scatter_pairs, "grad_scatter_start_" + tag)
        return zero11(reducing[tag][4])

    def reduce_finish(tag, after):
        send, recv, srcs, lands, _ = reducing[tag]
        srcs, lands = _split_wait(send, recv, srcs, lands, after, _scatter_pairs, "grad_scatter_wait_" + tag)
        return [_sum_own_and_landed(a, q, where, f"grad_sum_{tag}_{i}") for i, (a, q) in enumerate(zip(srcs, lands))]

    def ffn_hook(l, g):
        if l == 1:
            return jnp.zeros((1, 1), F32)
        return reduce_start("g0b", [g["wg"], g["wu"], g["wd"]])

    def layer_hook(l, big, small):
        last[l] = (big, small)
        if l == 1:
            return reduce_start("g1", [big[n] for n, _, _ in BIG])
        return None

    entry = zero11(started["w0b"][6]) + zero11(started["w1"][6])
    loss_part, dx = _step(x.reshape(seq, D), positions.reshape(seq, 1), loss_target.reshape(seq, D), p, entry,
                          mix_weights, late_weights, ffn_hook, layer_hook)
    loss = lax.psum(loss_part, ("x", "y", "c"))

    def adamw(n, g0, g1):
        w = view("", n)
        three_d = (DEPTH, -1, w.shape[-1])
        res = _adamw(w.reshape(three_d), g0.reshape(three_d[1:]), g1.reshape(three_d[1:]),
                     view("m_", n).reshape(three_d), view("v_", n).reshape(three_d), "adamw_" + n)
        return [r.reshape(w.shape) for r in res]

    names_rest = [(0, n) for n in EARLY_BIG + ["w_out"]]
    reduce_start("g0a", [last[0][0][n] for _, n in names_rest] + [_pack_small_grads([last[l][1] for l in range(DEPTH)])])
    token = reducing["g0a"][4]
    early = names_1 + [(0, n) for n in FFN_BIG]
    sums = dict(zip(early, _pair_join(reduce_finish("g1", token) + reduce_finish("g0b", token), "grad_pair_join_early")))
    out = {n: adamw(n, sums[(0, n)], sums[(1, n)]) for n in FFN_BIG}
    late = names_rest + ["small"]
    sums.update(zip(late, _pair_join(reduce_finish("g0a", out["w_down"][1]), "grad_pair_join_late")))
    gsmall = _unpack_small_grads(_all_gather_chips([sums["small"]], "all_gather_small_grads")[0])
    for n in ORDER:
        if n not in out:
            g = [sums[(l, n)] for l in range(DEPTH)] if (0, n) in sums else [gsmall[l][n] for l in range(DEPTH)]
            out[n] = adamw(n, *g)
    undo = lambda n, a: jnp.swapaxes(a, 1, 2) if n in TRANSPOSED else a
    return (loss, dx.reshape(x.shape), *[undo(n, out[n][i]) for i in range(4) for n in ORDER])


def _step(xs, pos, tgt, p, entry, mix_weights, late_weights, ffn_hook, layer_hook):
    sps = [_small_operands(p, l) for l in range(DEPTH)]
    sps[0]["g_mix"] = sps[0]["g_mix"] + entry
    tabs = _rope_tables(pos, _rope_inv_freq())
    saved, h = [], xs
    for l in range(DEPTH):
        kw = _kernel_weights(mix_weights(l, h))
        h, sv = _layer_fwd(h, tabs, kw, functools.partial(late_weights, l), sps[l], l)
        saved.append(dict(sv, kw=kw))
    dy, lpart = _loss_grad(h, tgt)
    for l in reversed(range(DEPTH)):
        dy, g = _layer_bwd(dy, saved[l], tabs, saved[l]["kw"], sps[l], l, functools.partial(ffn_hook, l))
        zero = layer_hook(l, _big_grads(g), _small_grads(g))
        if zero is not None and l > 0:
            sps[l - 1]["g_ffn"] = sps[l - 1]["g_ffn"] + zero
    return 0.5 / D * jnp.sum(lpart), dy
```

```python
import functools
import math

import jax
import jax.numpy as jnp
from jax import lax
from jax.experimental import pallas as pl
from jax.experimental.pallas import tpu as pltpu

F32 = jnp.float32
BF16 = jnp.bfloat16
MESH = pl.DeviceIdType.MESH

D = 1024
HEADS = 4
QK = 96
NOPE = 64
ROPE = 32
VH = 128
HP = 128
QL = 256
KVL = 128
SGU = 256
POOL = 256
CHUNK = 128
HID = 2816
CHIPS = 4
SH = HID // CHIPS
IN_W = 1184
IN_P = 1280
EPS = 1e-6
ROPE_THETA = 10000.0
SCALE = 1.0 / math.sqrt(QK)
LOG2E = 1.4426950408889634
EXP2_C = SCALE * LOG2E
ATT_SPLIT = 2
ATT_WIDE = 4
NEG = -1e30
HALO = 16

LR, B1, B2, ADAM_EPS, WD, STEP = 0.001, 0.9, 0.999, 1e-08, 0.01, 10

VMEM_LIMIT = 56 * 1024 * 1024
LANES = 128


def _cp(sem, vmem=None):
    return pltpu.CompilerParams(dimension_semantics=sem, vmem_limit_bytes=vmem)


def _res(shape):
    nd = len(shape)
    return pl.BlockSpec(shape, lambda *_: (0,) * nd, pipeline_mode=pl.Buffered(1))


def _acc(shape):
    nd = len(shape)
    return pl.BlockSpec(shape, lambda *_: (0,) * nd)


def _dot(a, b):
    return jnp.dot(a, b, preferred_element_type=F32)


def _dot_nt(a, b):
    return lax.dot_general(a, b, (((1,), (1,)), ((), ())), preferred_element_type=F32)


def _dot_tn(a, b):
    return lax.dot_general(a, b, (((0,), (0,)), ((), ())), preferred_element_type=F32)


def _rms(x, n):
    r = lax.rsqrt(jnp.sum(x * x, axis=-1, keepdims=True) * (1.0 / n) + EPS)
    return x * r, r


def _rms_bwd(xn, r, g, dy, n):
    dn = dy * g
    dx = r * (dn - xn * (jnp.sum(dn * xn, axis=-1, keepdims=True) * (1.0 / n)))
    return dx, jnp.sum(dy * xn, axis=0, keepdims=True)


def _accumulate(ref, val, first):
    @pl.when(first)
    def _():
        ref[...] = val

    @pl.when(jnp.logical_not(first))
    def _():
        ref[...] += val


def _accumulate0(ref, val, first):
    @pl.when(first)
    def _():
        ref[0] = val

    @pl.when(jnp.logical_not(first))
    def _():
        ref[0] += val


def _tile(s, t):
    return min(s, t)


def _row_tile(r, cap):
    if r <= cap:
        return r
    return max(t for t in range(8, cap + 1, 8) if r % t == 0)


def _rope_tables(pos, invf):
    s = pos.shape[0]
    tm = _tile(s, 1024)

    def body(pos_ref, invf_ref, c_ref, sa_ref, sb_ref):
        ang = pos_ref[...].astype(F32) * invf_ref[...]
        c, sn = jnp.cos(ang), jnp.sin(ang)
        lane = lax.broadcasted_iota(jnp.int32, ang.shape, 1)
        first = (lane >= NOPE) & (lane < NOPE + ROPE // 2)
        second = (lane >= NOPE + ROPE // 2) & (lane < QK)
        c_ref[...] = jnp.where(first | second, c, 1.0)
        sa_ref[...] = jnp.where(first, -sn, 0.0)
        sb_ref[...] = jnp.where(second, sn, 0.0)

    out = jax.ShapeDtypeStruct((s, HP), F32)
    return pl.pallas_call(
        body, name="rope_tables", grid=(s // tm,),
        in_specs=[pl.BlockSpec((tm, 1), lambda i: (i, 0)), _acc((1, HP))],
        out_specs=[pl.BlockSpec((tm, HP), lambda i: (i, 0))] * 3,
        out_shape=[out] * 3, compiler_params=_cp(("parallel",)),
    )(pos, invf)


def _rope(x, c, sa, sb):
    return x * c + pltpu.roll(x, HP - ROPE // 2, 1) * sa + pltpu.roll(x, ROPE // 2, 1) * sb


def _rope_t(d, c, sa, sb):
    return d * c + pltpu.roll(d * sa, ROPE // 2, 1) + pltpu.roll(d * sb, HP - ROPE // 2, 1)


def _in_proj_fwd(x, g, w, name):
    s = x.shape[0]
    tm = _tile(s, 512)

    def body(x_ref, g_ref, w_ref, z_ref, h_ref):
        xn, _ = _rms(x_ref[...], D)
        h = (xn * g_ref[...]).astype(BF16)
        h_ref[...] = h
        z_ref[...] = _dot(h, w_ref[...])

    return pl.pallas_call(
        body, name=name, grid=(s // tm,),
        in_specs=[pl.BlockSpec((tm, D), lambda i: (i, 0)), _acc((1, D)), _res((D, IN_P))],
        out_specs=[pl.BlockSpec((tm, IN_P), lambda i: (i, 0)), pl.BlockSpec((tm, D), lambda i: (i, 0))],
        out_shape=[jax.ShapeDtypeStruct((s, IN_P), F32), jax.ShapeDtypeStruct((s, D), BF16)],
        compiler_params=_cp(("parallel",), VMEM_LIMIT),
    )(x, g, w)


def _mla_prep_fwd(z, tabs, gql, gkv, gq, gk, wq, wk, wv, name):
    s = z.shape[0]
    tm = _tile(s, 512)

    def body(ql_ref, kv_ref, kr_ref, c_ref, sa_ref, sb_ref, gql_ref, gkv_ref, gq_ref, gk_ref,
             wq_ref, wk_ref, wv_ref, q_out, k_out, v_out):
        qn = (_rms(ql_ref[...], QL)[0] * gql_ref[...]).astype(BF16)
        kvn = (_rms(kv_ref[...], KVL)[0] * gkv_ref[...]).astype(BF16)
        qraw = _dot(qn, wq_ref[...])
        kraw = _dot(kvn, wk_ref[...])
        vraw = _dot(kvn, wv_ref[...])
        kr = kr_ref[...]
        c, sa, sb = c_ref[...], sa_ref[...], sb_ref[...]
        for h in range(HEADS):
            sl = slice(h * HP, (h + 1) * HP)
            xq = _rms(qraw[:, sl], QK)[0] * gq_ref[...]
            q_out[h] = _rope(xq, c, sa, sb).astype(BF16)
            xk = _rms(kraw[:, sl] + kr, QK)[0] * gk_ref[...]
            k_out[h] = _rope(xk, c, sa, sb).astype(BF16)
            v_out[h] = vraw[:, sl].astype(BF16)

    row = lambda w, j: pl.BlockSpec((tm, w), lambda i: (i, j))
    hspec = pl.BlockSpec((HEADS, tm, HP), lambda i: (0, i, 0))
    hshape = jax.ShapeDtypeStruct((HEADS, s, HP), BF16)
    return pl.pallas_call(
        body, name=name, grid=(s // tm,),
        in_specs=[row(QL, 0), row(KVL, 2), row(HP, 3), row(HP, 0), row(HP, 0), row(HP, 0),
                  _acc((1, QL)), _acc((1, KVL)), _acc((1, HP)), _acc((1, HP)),
                  _acc((QL, HEADS * HP)), _acc((KVL, HEADS * HP)), _acc((KVL, HEADS * HP))],
        out_specs=[hspec] * 3, out_shape=[hshape] * 3,
        compiler_params=_cp(("parallel",)),
    )(z, z, z, *tabs, gql, gkv, gq, gk, wq, wk, wv)


def _causal_mask(s, row0):
    row = lax.broadcasted_iota(jnp.int32, s.shape, 0) + row0
    col = lax.broadcasted_iota(jnp.int32, s.shape, 1)
    return jnp.where(col <= row, s, NEG)


def _attn_fwd(q, k, v, name):
    s = q.shape[1]
    tq = _tile(s, 512)
    wide = ATT_WIDE * tq if s % (ATT_WIDE * tq) == 0 else tq
    rh = tq // ATT_SPLIT

    def body(q_ref, k_ref, v_ref, o_ref, lse_ref):
        i = pl.program_id(1)

        def blk(off, tk, carry, masked):
            off = pl.multiple_of(off, tq)
            kj = k_ref[0, pl.ds(off, tk), :]
            vj = v_ref[0, pl.ds(off, tk), :]
            out = []
            keys = [(g + 1) * rh if masked else tk for g in range(ATT_SPLIT)]
            scs = [_dot_nt(q_ref[0, g * rh:(g + 1) * rh, :], kj[:keys[g]]) for g in range(ATT_SPLIT)]
            for g, (m, l, acc) in enumerate(carry):
                sc = scs[g]
                if masked:
                    sc = _causal_mask(sc, g * rh)
                m_new = jnp.maximum(m, jnp.max(sc, axis=-1, keepdims=True))
                p = jnp.exp2((sc - m_new) * EXP2_C)
                alpha = jnp.exp2((m - m_new) * EXP2_C)
                l = alpha * l + jnp.sum(p, axis=-1, keepdims=True)
                acc = alpha * acc + _dot(p.astype(BF16), vj[:keys[g]])
                out.append((m_new, l, acc))
            return tuple(out)

        one = (jnp.full((rh, 1), NEG, F32), jnp.zeros((rh, 1), F32), jnp.zeros((rh, VH), F32))
        nwide = (i * tq) // wide
        carry = lax.fori_loop(0, nwide, lambda j, c: blk(j * wide, wide, c, False), (one,) * ATT_SPLIT)
        carry = lax.fori_loop(nwide * (wide // tq), i, lambda j, c: blk(j * tq, tq, c, False), carry)
        carry = blk(i * tq, tq, carry, True)
        for g, (m, l, acc) in enumerate(carry):
            o_ref[g * rh:(g + 1) * rh, :] = acc / l
            lse_ref[0, g * rh:(g + 1) * rh, :] = jnp.broadcast_to(m * EXP2_C + jnp.log(l) * LOG2E, (rh, LANES))

    return pl.pallas_call(
        body, name=name, grid=(HEADS, s // tq),
        in_specs=[pl.BlockSpec((1, tq, HP), lambda h, i: (h, i, 0)),
                  pl.BlockSpec((1, s, HP), lambda h, i: (h, 0, 0)),
                  pl.BlockSpec((1, s, HP), lambda h, i: (h, 0, 0))],
        out_specs=[pl.BlockSpec((tq, VH), lambda h, i: (i, h)),
                   pl.BlockSpec((1, tq, LANES), lambda h, i: (h, i, 0))],
        out_shape=[jax.ShapeDtypeStruct((s, HEADS * VH), F32), jax.ShapeDtypeStruct((HEADS, s, LANES), F32)],
        compiler_params=_cp(("parallel", "arbitrary"), VMEM_LIMIT),
    )(q, k, v)


def _lane_group(shape, j):
    return (lax.broadcasted_iota(jnp.int32, shape, 1) + j * LANES) // (POOL // 4)


def _pool_win_fwd(z, name):
    s = z.shape[0]
    ch = _tile(s, 512)
    col0 = (IN_P - POOL) // LANES

    def body(p_ref, m_ref):
        j = pl.program_id(0)

        def chunk(r, _):
            off = pl.multiple_of(r * ch, ch)
            cur = p_ref[pl.ds(off, ch), :]
            hoff = pl.multiple_of(jnp.maximum(off - HALO, 0), 8)
            halo = jnp.where(r > 0, p_ref[pl.ds(hoff, HALO), :], 0.0)
            x = jnp.concatenate([halo, cur], axis=0)
            s2 = x + pltpu.roll(x, 1, 0)
            s4 = s2 + pltpu.roll(s2, 2, 0)
            s8 = s4 + pltpu.roll(s4, 4, 0)
            s16 = s8 + pltpu.roll(s8, 8, 0)
            grp = _lane_group((ch, LANES), j)
            sel = jnp.where(grp == 0, s2[HALO:], jnp.where(grp == 1, s4[HALO:], jnp.where(grp == 2, s8[HALO:], s16[HALO:])))
            t1 = (lax.broadcasted_iota(jnp.int32, (ch, LANES), 0) + off + 1).astype(F32)
            win = jnp.where(grp == 0, 2.0, jnp.where(grp == 1, 4.0, jnp.where(grp == 2, 8.0, 16.0)))
            m_ref[pl.ds(off, ch), :] = sel / jnp.minimum(t1, win) - cur
            return 0

        lax.fori_loop(0, s // ch, chunk, 0)

    return pl.pallas_call(
        body, name=name, grid=(POOL // LANES,),
        in_specs=[pl.BlockSpec((s, LANES), lambda j: (0, col0 + j))],
        out_specs=pl.BlockSpec((s, LANES), lambda j: (0, j)),
        out_shape=jax.ShapeDtypeStruct((s, POOL), F32),
        compiler_params=_cp(("parallel",), VMEM_LIMIT),
    )(z)


def _pool_win_bwd(dm, name):
    s = dm.shape[0]
    ch = _tile(s, 512)
    n = s // ch

    def body(dm_ref, dp_ref):
        j = pl.program_id(0)

        def chunk(r, _):
            off = pl.multiple_of(r * ch, ch)
            grp = _lane_group((ch + HALO, LANES), j)
            win = jnp.where(grp == 0, 2.0, jnp.where(grp == 1, 4.0, jnp.where(grp == 2, 8.0, 16.0)))
            cur = dm_ref[pl.ds(off, ch), :]
            hoff = pl.multiple_of(jnp.minimum(off + ch, s - HALO), 8)
            halo = jnp.where(r < n - 1, dm_ref[pl.ds(hoff, HALO), :], 0.0)
            x = jnp.concatenate([cur, halo], axis=0)
            t1 = (lax.broadcasted_iota(jnp.int32, (ch + HALO, LANES), 0) + off + 1).astype(F32)
            e = x / jnp.minimum(t1, win)
            tot = ch + HALO
            r2 = e + pltpu.roll(e, tot - 1, 0)
            r4 = r2 + pltpu.roll(r2, tot - 2, 0)
            r8 = r4 + pltpu.roll(r4, tot - 4, 0)
            r16 = r8 + pltpu.roll(r8, tot - 8, 0)
            g = grp[:ch]
            sel = jnp.where(g == 0, r2[:ch], jnp.where(g == 1, r4[:ch], jnp.where(g == 2, r8[:ch], r16[:ch])))
            dp_ref[pl.ds(off, ch), :] = (sel - cur).astype(BF16)
            return 0

        lax.fori_loop(0, n, chunk, 0)

    return pl.pallas_call(
        body, name=name, grid=(POOL // LANES,),
        in_specs=[pl.BlockSpec((s, LANES), lambda j: (0, j))],
        out_specs=pl.BlockSpec((s, LANES), lambda j: (0, j)),
        out_shape=jax.ShapeDtypeStruct((s, POOL), BF16),
        compiler_params=_cp(("parallel",), VMEM_LIMIT),
    )(dm)


def _head_mask(h):
    lane = lax.broadcasted_iota(jnp.int32, (CHUNK, SGU), 1)
    return (lane // (SGU // HEADS)) == h


def _tril(upper=False):
    row = lax.broadcasted_iota(jnp.int32, (CHUNK, CHUNK), 0)
    col = lax.broadcasted_iota(jnp.int32, (CHUNK, CHUNK), 1)
    return col >= row if upper else col <= row


def _sgu_gate(vn, wsp, bsp):
    out = []
    for cidx in range(vn.shape[0] // CHUNK):
        vc = vn[cidx * CHUNK:(cidx + 1) * CHUNK]
        zc = bsp
        for h in range(HEADS):
            zc = zc + jnp.where(_head_mask(h), _dot(wsp[h], vc), 0.0)
        out.append(zc)
    return jnp.concatenate(out, axis=0)


def _mix_out_fwd(o, z, m, x, wsp, bsp, wbd, psc, gsv, gout, wout, name):
    s = x.shape[0]
    tm = _tile(s, 512)

    def body(o_ref, uv_ref, m_ref, x_ref, wsp_ref, bsp_ref, wbd_ref, psc_ref, gsv_ref, gout_ref, wout_ref,
             x1_ref, mix_ref):
        g = gout_ref[...]
        an = _rms(o_ref[...], HEADS * VH)[0] * g[:, :512]
        uv = uv_ref[...]
        u, v = uv[:, :SGU], uv[:, SGU:]
        vn = (_rms(v, SGU)[0] * gsv_ref[...]).astype(BF16)
        tri = _tril()
        wsp_m = [jnp.where(tri, wsp_ref[h], 0.0).astype(BF16) for h in range(HEADS)]
        gm = u * _sgu_gate(vn, wsp_m, bsp_ref[...])
        gn = _rms(gm, SGU)[0] * g[:, 512:768]
        po = _dot(m_ref[...].astype(BF16), wbd_ref[...]) * psc_ref[...]
        pn = _rms(po, POOL)[0] * g[:, 768:]
        mix = jnp.concatenate([an, gn, pn], axis=1).astype(BF16)
        mix_ref[...] = mix
        x1_ref[...] = x_ref[...] + _dot(mix, wout_ref[...])

    row = lambda w, j: pl.BlockSpec((tm, w), lambda i: (i, j))
    return pl.pallas_call(
        body, name=name, grid=(s // tm,),
        in_specs=[row(512, 0), row(512, 1), row(POOL, 0), row(D, 0),
                  _acc((HEADS, CHUNK, CHUNK)), _acc((CHUNK, SGU)), _acc((POOL, POOL)), _acc((1, POOL)),
                  _acc((1, SGU)), _acc((1, D)), _res((D, D))],
        out_specs=[row(D, 0), row(D, 0)],
        out_shape=[jax.ShapeDtypeStruct((s, D), F32), jax.ShapeDtypeStruct((s, D), BF16)],
        compiler_params=_cp(("parallel",), VMEM_LIMIT),
    )(o, z, m, x, wsp, bsp, wbd, psc, gsv, gout, wout)


def _ffn_fwd(x1, g, wg, wu, wd, name):
    s = x1.shape[0]
    tm = _tile(s, 256)

    def body(x_ref, g_ref, wg_ref, wu_ref, wd_ref, x2_ref, a_ref, b_ref, h_ref):
        x = x_ref[...]
        h = (_rms(x, D)[0] * g_ref[...]).astype(BF16)
        h_ref[...] = h
        acc = jnp.zeros((tm, D), F32)
        for k in range(CHIPS):
            a = _dot_nt(h, wg_ref[k])
            b = _dot_nt(h, wu_ref[k])
            a_ref[k] = a
            b_ref[k] = b
            acc = acc + _dot((a * jax.nn.sigmoid(a) * b).astype(BF16), wd_ref[k])
        x2_ref[...] = x + acc

    row = lambda w: pl.BlockSpec((tm, w), lambda i: (i, 0))
    hrow = pl.BlockSpec((CHIPS, tm, SH), lambda i: (0, i, 0))
    hshape = jax.ShapeDtypeStruct((CHIPS, s, SH), F32)
    return pl.pallas_call(
        body, name=name, grid=(s // tm,),
        in_specs=[row(D), _acc((1, D)), _res((CHIPS, SH, D)), _res((CHIPS, SH, D)), _res((CHIPS, SH, D))],
        out_specs=[row(D), hrow, hrow, row(D)],
        out_shape=[jax.ShapeDtypeStruct((s, D), F32), hshape, hshape, jax.ShapeDtypeStruct((s, D), BF16)],
        compiler_params=_cp(("parallel",), VMEM_LIMIT),
    )(x1, g, wg, wu, wd)


def _loss_grad(y, tgt):
    s = y.shape[0]
    tm = _tile(s, 512)

    def body(y_ref, t_ref, dy_ref, l_ref):
        e = y_ref[...] - t_ref[...]
        dy_ref[...] = e * (1.0 / D)
        sq = jnp.sum(e * e, axis=0, keepdims=True)
        part = sq[:, :LANES]
        for c in range(1, D // LANES):
            part = part + sq[:, c * LANES:(c + 1) * LANES]
        _accumulate(l_ref, part, pl.program_id(0) == 0)

    row = pl.BlockSpec((tm, D), lambda i: (i, 0))
    return pl.pallas_call(
        body, name="loss_grad", grid=(s // tm,),
        in_specs=[row, row], out_specs=[row, _acc((1, LANES))],
        out_shape=[jax.ShapeDtypeStruct((s, D), F32), jax.ShapeDtypeStruct((1, LANES), F32)],
        compiler_params=_cp(("arbitrary",)),
    )(y, tgt)


def _wgrad(a, b, name):
    s, k = a.shape
    n = b.shape[1]
    half = lambda v: v if v <= 1408 else v // 2
    kb, nb, tt = half(k), half(n), _tile(s, 2048)

    def body(a_ref, b_ref, o_ref):
        _accumulate(o_ref, _dot_tn(a_ref[...].astype(BF16), b_ref[...].astype(BF16)), pl.program_id(2) == 0)

    return pl.pallas_call(
        body, name=name, grid=(k // kb, n // nb, s // tt),
        in_specs=[pl.BlockSpec((tt, kb), lambda i, j, t: (t, i)), pl.BlockSpec((tt, nb), lambda i, j, t: (t, j))],
        out_specs=pl.BlockSpec((kb, nb), lambda i, j, t: (i, j)),
        out_shape=jax.ShapeDtypeStruct((k, n), F32),
        compiler_params=_cp(("parallel", "parallel", "arbitrary"), VMEM_LIMIT),
    )(a, b)


def _wgrad_rows(a, b, name):
    s, n = a.shape[1:]
    nn = b.shape[1]
    tt = _tile(s, 4096 if b.dtype == BF16 else 2048)

    def body(a_ref, b_ref, o_ref):
        _accumulate0(o_ref, _dot_tn(a_ref[0].astype(BF16), b_ref[...].astype(BF16)), pl.program_id(1) == 0)

    return pl.pallas_call(
        body, name=name, grid=(CHIPS, s // tt),
        in_specs=[pl.BlockSpec((1, tt, n), lambda c, t: (c, t, 0)), pl.BlockSpec((tt, nn), lambda c, t: (t, 0))],
        out_specs=pl.BlockSpec((1, n, nn), lambda c, t: (c, 0, 0)),
        out_shape=jax.ShapeDtypeStruct((CHIPS, n, nn), F32),
        compiler_params=_cp(("parallel", "arbitrary"), VMEM_LIMIT),
    )(a, b)


def _ffn_bwd(dx2, x1, a, b, g, wg, wu, wd, name):
    s = x1.shape[0]
    tm = _tile(s, 256)

    def body(dx2_ref, x_ref, a_ref, b_ref, g_ref, wg_ref, wu_ref, wd_ref,
             dx1_ref, hid_ref, da_ref, db_ref, dg_ref):
        dx2 = dx2_ref[...]
        dyb = dx2.astype(BF16)
        dh = jnp.zeros((tm, D), F32)
        for k in range(CHIPS):
            av, bv = a_ref[k], b_ref[k]
            dhid = _dot_nt(dyb, wd_ref[k])
            sig = jax.nn.sigmoid(av)
            sa = av * sig
            hid_ref[k] = (sa * bv).astype(BF16)
            dbv = (dhid * sa).astype(BF16)
            dav = (dhid * bv * (sig * (1.0 + av * (1.0 - sig)))).astype(BF16)
            db_ref[k] = dbv
            da_ref[k] = dav
            dh = dh + _dot(dav, wg_ref[k]) + _dot(dbv, wu_ref[k])
        xn, r = _rms(x_ref[...], D)
        dxr, dg = _rms_bwd(xn, r, g_ref[...], dh, D)
        dx1_ref[...] = dx2 + dxr
        _accumulate(dg_ref, dg, pl.program_id(0) == 0)

    row = lambda w: pl.BlockSpec((tm, w), lambda i: (i, 0))
    hrow = pl.BlockSpec((CHIPS, tm, SH), lambda i: (0, i, 0))
    hid = jax.ShapeDtypeStruct((CHIPS, s, SH), BF16)
    return pl.pallas_call(
        body, name=name, grid=(s // tm,),
        in_specs=[row(D), row(D), hrow, hrow, _acc((1, D)), _res((CHIPS, SH, D)), _res((CHIPS, SH, D)),
                  _res((CHIPS, SH, D))],
        out_specs=[row(D), hrow, hrow, hrow, _acc((1, D))],
        out_shape=[jax.ShapeDtypeStruct((s, D), F32), hid, hid, hid, jax.ShapeDtypeStruct((1, D), F32)],
        compiler_params=_cp(("arbitrary",), VMEM_LIMIT),
    )(dx2, x1, a, b, g, wg, wu, wd)


def _mix_out_bwd(dx1, o, z, m, wsp, bsp, wbd, psc, gsv, gout, wout, name):
    s = dx1.shape[0]
    tm = _tile(s, 512)

    def body(dx1_ref, o_ref, uv_ref, m_ref, wsp_ref, bsp_ref, wbd_ref, psc_ref, gsv_ref, gout_ref, wout_ref,
             do_ref, dl_ref, duv_ref, dm_ref, dgo_ref, dgsv_ref, dpsc_ref, dwsp_ref, dbsp_ref, dwbd_ref):
        first = pl.program_id(0) == 0
        g = gout_ref[...]
        dmix = _dot_nt(dx1_ref[...].astype(BF16), wout_ref[...])
        o = o_ref[...]
        on, ro = _rms(o, HEADS * VH)
        do, dga = _rms_bwd(on, ro, g[:, :512], dmix[:, :512], HEADS * VH)
        for h in range(HEADS):
            sl = slice(h * VH, (h + 1) * VH)
            do_ref[h] = do[:, sl].astype(BF16)
            dl_ref[h] = jnp.broadcast_to(jnp.sum(do[:, sl] * o[:, sl], axis=-1, keepdims=True), (tm, LANES))
        uv = uv_ref[...]
        u, v = uv[:, :SGU], uv[:, SGU:]
        vx, rv = _rms(v, SGU)
        vn = (vx * gsv_ref[...]).astype(BF16)
        tri = _tril()
        wsp_m = [jnp.where(tri, wsp_ref[h], 0.0).astype(BF16) for h in range(HEADS)]
        zc = _sgu_gate(vn, wsp_m, bsp_ref[...])
        gm = u * zc
        gmn, rg = _rms(gm, SGU)
        dgm, dgg = _rms_bwd(gmn, rg, g[:, 512:768], dmix[:, 512:768], SGU)
        du = dgm * zc
        dzc = dgm * u
        dvn_parts = []
        dbsp = jnp.zeros((CHUNK, SGU), F32)
        dwsp = [jnp.zeros((CHUNK, CHUNK), F32) for _ in range(HEADS)]
        for cidx in range(tm // CHUNK):
            rs = slice(cidx * CHUNK, (cidx + 1) * CHUNK)
            dzc_c = dzc[rs]
            dbsp = dbsp + dzc_c
            dzb = dzc_c.astype(BF16)
            vc = vn[rs]
            dvn_c = jnp.zeros((CHUNK, SGU), F32)
            for h in range(HEADS):
                hm = _head_mask(h)
                dvn_c = dvn_c + jnp.where(hm, _dot_tn(wsp_m[h], dzb), 0.0)
                dwsp[h] = dwsp[h] + _dot_nt(jnp.where(hm, dzc_c, 0.0).astype(BF16), vc)
            dvn_parts.append(dvn_c)
        dvn = jnp.concatenate(dvn_parts, axis=0)
        dv, dgsv = _rms_bwd(vx, rv, gsv_ref[...], dvn, SGU)
        duv_ref[...] = jnp.concatenate([du, dv], axis=1).astype(BF16)
        mb = m_ref[...].astype(BF16)
        pw = _dot(mb, wbd_ref[...])
        po = pw * psc_ref[...]
        pon, rp = _rms(po, POOL)
        dpo, dgp = _rms_bwd(pon, rp, g[:, 768:], dmix[:, 768:], POOL)
        dpw = (dpo * psc_ref[...]).astype(BF16)
        dm_ref[...] = _dot_nt(dpw, wbd_ref[...])
        _accumulate(dgo_ref, jnp.concatenate([dga, dgg, dgp], axis=1), first)
        _accumulate(dgsv_ref, dgsv, first)
        _accumulate(dpsc_ref, jnp.sum(dpo * pw, axis=0, keepdims=True), first)
        _accumulate(dbsp_ref, dbsp, first)
        _accumulate(dwbd_ref, _dot_tn(mb, dpw), first)
        for h in range(HEADS):
            val = jnp.where(tri, dwsp[h], 0.0)

            @pl.when(first)
            def _(val=val, h=h):
                dwsp_ref[h] = val

            @pl.when(jnp.logical_not(first))
            def _(val=val, h=h):
                dwsp_ref[h] += val

    row = lambda w, j: pl.BlockSpec((tm, w), lambda i: (i, j))
    hspec = pl.BlockSpec((HEADS, tm, HP), lambda i: (0, i, 0))
    return pl.pallas_call(
        body, name=name, grid=(s // tm,),
        in_specs=[row(D, 0), row(512, 0), row(512, 1), row(POOL, 0),
                  _acc((HEADS, CHUNK, CHUNK)), _acc((CHUNK, SGU)),
                  _acc((POOL, POOL)), _acc((1, POOL)), _acc((1, SGU)), _acc((1, D)), _res((D, D))],
        out_specs=[hspec, hspec, row(512, 0), row(POOL, 0), _acc((1, D)), _acc((1, SGU)), _acc((1, POOL)),
                   _acc((HEADS, CHUNK, CHUNK)), _acc((CHUNK, SGU)), _acc((POOL, POOL))],
        out_shape=[jax.ShapeDtypeStruct((HEADS, s, HP), BF16), jax.ShapeDtypeStruct((HEADS, s, LANES), F32),
                   jax.ShapeDtypeStruct((s, 512), BF16), jax.ShapeDtypeStruct((s, POOL), F32),
                   jax.ShapeDtypeStruct((1, D), F32), jax.ShapeDtypeStruct((1, SGU), F32),
                   jax.ShapeDtypeStruct((1, POOL), F32), jax.ShapeDtypeStruct((HEADS, CHUNK, CHUNK), F32),
                   jax.ShapeDtypeStruct((CHUNK, SGU), F32), jax.ShapeDtypeStruct((POOL, POOL), F32)],
        compiler_params=_cp(("arbitrary",), VMEM_LIMIT),
    )(dx1, o, z, m, wsp, bsp, wbd, psc, gsv, gout, wout)


def _attn_bwd(q, k, v, do, lse, delta, name):
    s = q.shape[1]
    tq = tk = _tile(s, 512)
    nq = s // tq
    wide = ATT_WIDE * tq if s % (ATT_WIDE * tq) == 0 else tq

    def body(q_ref, k_ref, v_ref, do_ref, lse_ref, dl_ref, dq_ref, dk_ref, dv_ref):
        j = pl.program_id(1)

        @pl.when(j == 0)
        def _():
            dq_ref[...] = jnp.zeros_like(dq_ref)

        kj, vj = k_ref[0], v_ref[0]
        rh = tq // ATT_SPLIT

        def blk(start, rows, dk, dv, masked):
            offs = [pl.multiple_of(start + g * rh, rh) for g in range(rows // rh)]
            qs = [q_ref[0, pl.ds(off, rh), :] for off in offs]
            dos = [do_ref[0, pl.ds(off, rh), :] for off in offs]
            scs = [_dot_nt(qi, kj) for qi in qs]
            dps = [_dot_nt(doi, vj) for doi in dos]
            for g, off in enumerate(offs):
                lse_i = lse_ref[0, pl.ds(off, rh), :][:, :1]
                dl_i = dl_ref[0, pl.ds(off, rh), :][:, :1]
                sc = _causal_mask(scs[g], g * rh) if masked else scs[g]
                p = jnp.exp2(sc * EXP2_C - lse_i)
                ds = (p * (dps[g] - dl_i)).astype(BF16)
                dv = dv + _dot_tn(p.astype(BF16), dos[g])
                dk = dk + _dot_tn(ds, qs[g])
                dq_ref[0, pl.ds(off, rh), :] += _dot(ds, kj) * SCALE
            return dk, dv

        per = wide // tq
        zero = jnp.zeros((tk, HP), F32)
        dk, dv = blk(j * tq, tq, zero, zero, True)
        first_wide = (j + per) // per
        dk, dv = lax.fori_loop(j + 1, jnp.minimum(first_wide * per, nq), lambda i, c: blk(i * tq, tq, *c, False), (dk, dv))
        dk, dv = lax.fori_loop(first_wide, nq // per, lambda i, c: blk(i * wide, wide, *c, False), (dk, dv))
        dk_ref[0] = dk * SCALE
        dv_ref[0] = dv

    full = lambda: pl.BlockSpec((1, s, HP), lambda h, j: (h, 0, 0))
    blk_spec = lambda: pl.BlockSpec((1, tk, HP), lambda h, j: (h, j, 0))
    out = jax.ShapeDtypeStruct((HEADS, s, HP), F32)
    return pl.pallas_call(
        body, name=name, grid=(HEADS, s // tk),
        in_specs=[full(), blk_spec(), blk_spec(), full(), full(), full()],
        out_specs=[full(), blk_spec(), blk_spec()], out_shape=[out] * 3,
        compiler_params=_cp(("parallel", "arbitrary"), VMEM_LIMIT),
    )(q, k, v, do, lse, delta)


def _mla_prep_bwd(dq, dk, dv, z, tabs, gql, gkv, gq, gk, wq, wk, wv, name):
    s = z.shape[0]
    tm = _tile(s, 512)

    def body(dq_ref, dk_ref, dv_ref, ql_ref, kv_ref, kr_ref, c_ref, sa_ref, sb_ref, gql_ref, gkv_ref, gq_ref, gk_ref,
             wq_ref, wk_ref, wv_ref,
             dz_ref, qn_ref, kvn_ref, dqr_ref, dkr_ref, dvr_ref, dgql_ref, dgkv_ref, dgq_ref, dgk_ref):
        first = pl.program_id(0) == 0
        qx, rq = _rms(ql_ref[...], QL)
        qn = (qx * gql_ref[...]).astype(BF16)
        kx, rk = _rms(kv_ref[...], KVL)
        kvn = (kx * gkv_ref[...]).astype(BF16)
        qn_ref[...] = qn
        kvn_ref[...] = kvn
        qraw = _dot(qn, wq_ref[...])
        kraw = _dot(kvn, wk_ref[...])
        kr = kr_ref[...]
        c, sa, sb = c_ref[...], sa_ref[...], sb_ref[...]
        lane = lax.broadcasted_iota(jnp.int32, (tm, HP), 1)
        rope_lanes = (lane >= NOPE) & (lane < QK)
        dkrope = jnp.zeros((tm, HP), F32)
        dgq = jnp.zeros((1, HP), F32)
        dgk = jnp.zeros((1, HP), F32)
        for h in range(HEADS):
            sl = slice(h * HP, (h + 1) * HP)
            xn, r = _rms(qraw[:, sl], QK)
            dx, dg = _rms_bwd(xn, r, gq_ref[...], _rope_t(dq_ref[h], c, sa, sb), QK)
            dqr_ref[:, sl] = dx.astype(BF16)
            dgq = dgq + dg
            xn, r = _rms(kraw[:, sl] + kr, QK)
            dx, dg = _rms_bwd(xn, r, gk_ref[...], _rope_t(dk_ref[h], c, sa, sb), QK)
            dkr_ref[:, sl] = dx.astype(BF16)
            dgk = dgk + dg
            dkrope = dkrope + jnp.where(rope_lanes, dx, 0.0)
            dvr_ref[:, sl] = dv_ref[h].astype(BF16)
        dqn = _dot_nt(dqr_ref[...], wq_ref[...])
        dql, dgql = _rms_bwd(qx, rq, gql_ref[...], dqn, QL)
        dkvn = _dot_nt(dkr_ref[...], wk_ref[...]) + _dot_nt(dvr_ref[...], wv_ref[...])
        dkv, dgkv = _rms_bwd(kx, rk, gkv_ref[...], dkvn, KVL)
        dz_ref[...] = jnp.concatenate([dql, dkv, dkrope], axis=1).astype(BF16)
        _accumulate(dgql_ref, dgql, first)
        _accumulate(dgkv_ref, dgkv, first)
        _accumulate(dgq_ref, dgq, first)
        _accumulate(dgk_ref, dgk, first)

    row = lambda w, j: pl.BlockSpec((tm, w), lambda i: (i, j))
    hspec = pl.BlockSpec((HEADS, tm, HP), lambda i: (0, i, 0))
    sd = lambda w, dt: jax.ShapeDtypeStruct((s, w), dt)
    return pl.pallas_call(
        body, name=name, grid=(s // tm,),
        in_specs=[hspec, hspec, hspec, row(QL, 0), row(KVL, 2), row(HP, 3), row(HP, 0), row(HP, 0), row(HP, 0),
                  _acc((1, QL)), _acc((1, KVL)), _acc((1, HP)), _acc((1, HP)),
                  _acc((QL, HEADS * HP)), _acc((KVL, HEADS * HP)), _acc((KVL, HEADS * HP))],
        out_specs=[row(512, 0), row(QL, 0), row(KVL, 0), row(512, 0), row(512, 0), row(512, 0),
                   _acc((1, QL)), _acc((1, KVL)), _acc((1, HP)), _acc((1, HP))],
        out_shape=[sd(512, BF16), sd(QL, BF16), sd(KVL, BF16), sd(512, BF16), sd(512, BF16), sd(512, BF16),
                   jax.ShapeDtypeStruct((1, QL), F32), jax.ShapeDtypeStruct((1, KVL), F32),
                   jax.ShapeDtypeStruct((1, HP), F32), jax.ShapeDtypeStruct((1, HP), F32)],
        compiler_params=_cp(("arbitrary",), VMEM_LIMIT),
    )(dq, dk, dv, z, z, z, *tabs, gql, gkv, gq, gk, wq, wk, wv)


def _in_proj_bwd(dzm, duv, dp, x, dx1, g, win, name):
    s = x.shape[0]
    tm = _tile(s, 512)

    def body(dzm_ref, duv_ref, dp_ref, x_ref, dx1_ref, g_ref, w_ref, dx_ref, dg_ref):
        dh = _dot_nt(dzm_ref[...], w_ref[:, 0:512]) + _dot_nt(duv_ref[...], w_ref[:, 512:1024]) \
            + _dot_nt(dp_ref[...], w_ref[:, 1024:IN_P])
        xn, r = _rms(x_ref[...], D)
        dxr, dg = _rms_bwd(xn, r, g_ref[...], dh, D)
        dx_ref[...] = dx1_ref[...] + dxr
        _accumulate(dg_ref, dg, pl.program_id(0) == 0)

    row = lambda w: pl.BlockSpec((tm, w), lambda i: (i, 0))
    return pl.pallas_call(
        body, name=name, grid=(s // tm,),
        in_specs=[row(512), row(512), row(POOL), row(D), row(D), _acc((1, D)), _res((D, IN_P))],
        out_specs=[row(D), _acc((1, D))],
        out_shape=[jax.ShapeDtypeStruct((s, D), F32), jax.ShapeDtypeStruct((1, D), F32)],
        compiler_params=_cp(("arbitrary",), VMEM_LIMIT),
    )(dzm, duv, dp, x, dx1, g, win)


def _adamw(w, g0, g1, m, v, name):
    _, r, c = w.shape
    tr = _row_tile(r, 512)
    c1 = 1.0 - B1 ** STEP
    c2 = 1.0 - B2 ** STEP

    def body(w_ref, g0_ref, g1_ref, m_ref, v_ref, g_ref, d_ref, nm_ref, nv_ref):
        gv = jnp.where(pl.program_id(0) == 0, g0_ref[...], g1_ref[...])
        g_ref[0] = gv
        nm = B1 * m_ref[0] + (1.0 - B1) * gv
        nv = B2 * v_ref[0] + (1.0 - B2) * (gv * gv)
        nm_ref[0] = nm
        nv_ref[0] = nv
        d_ref[0] = -LR * ((nm / c1) / (jnp.sqrt(nv / c2) + ADAM_EPS) + WD * w_ref[0])

    spec = pl.BlockSpec((1, tr, c), lambda l, i: (l, i, 0))
    out = jax.ShapeDtypeStruct((DEPTH, r, c), F32)
    return pl.pallas_call(
        body, name=name, grid=(DEPTH, r // tr),
        in_specs=[spec, pl.BlockSpec((tr, c), lambda l, i: (i * (1 - l), 0)), pl.BlockSpec((tr, c), lambda l, i: (i * l, 0)),
                  spec, spec],
        out_specs=[spec] * 4, out_shape=[out] * 4, compiler_params=_cp(("parallel", "parallel")),
    )(w, g0, g1, m, v)


ANY = pl.BlockSpec(memory_space=pl.ANY)


def _place():
    x, y, c = lax.axis_index("x"), lax.axis_index("y"), lax.axis_index("c")
    chips = [(1 - x, y), (x, 1 - y), (1 - x, 1 - y)]
    return x, y, c, chips


def _half_rows(ref, lead, hh, half, align):
    rows = pl.ds(pl.multiple_of(hh * half, align), half)
    return ref.at[rows, :] if lead is None else ref.at[lead, rows, :]


def _row_align(dtype):
    return 16 if dtype == BF16 else 8


def _sems(n):
    return [pltpu.SemaphoreType.DMA((n,)), pltpu.SemaphoreType.DMA((n,)), pltpu.SemaphoreType.DMA((n,))]


def _comm_call(body, ins, out_shapes, nsems, name):
    return pl.pallas_call(
        body, name=name, in_specs=[ANY] * len(ins), out_specs=[ANY] * len(out_shapes), out_shape=out_shapes,
        scratch_shapes=_sems(nsems), compiler_params=pltpu.CompilerParams(has_side_effects=True),
    )(*ins)


def _all_gather_chips(shards, name):
    n = len(shards)
    halves = [a.shape[0] // 2 for a in shards]
    aligns = [_row_align(a.dtype) for a in shards]
    assert all(h % al == 0 for h, al in zip(halves, aligns))

    def body(*refs):
        ins, outs, (send_sems, recv_sems, _) = refs[:n], refs[n:2 * n], refs[2 * n:]
        x, y, c, chips = _place()
        me = 2 * x + y
        sibling = (x, y, 1 - c)

        def copy(sem, src, dst, to):
            return pltpu.make_async_remote_copy(src_ref=src, dst_ref=dst, send_sem=send_sems.at[sem],
                                                recv_sem=recv_sems.at[sem], device_id=to, device_id_type=MESH)

        first, passed = [], []
        for a in range(n):
            my_half = _half_rows(ins[a], None, c, halves[a], aligns[a])
            for j, (cx, cy) in enumerate(chips):
                cp = copy(6 * a + j, my_half, _half_rows(outs[a], me, c, halves[a], aligns[a]), (cx, cy, c))
                cp.start()
                first.append(cp)
        for a in range(n):
            for j, (cx, cy) in enumerate(chips):
                landed = _half_rows(outs[a], 2 * cx + cy, c, halves[a], aligns[a])
                copy(6 * a + j, landed, landed, (cx, cy, c)).wait_recv()
                fwd = copy(6 * a + 3 + j, landed, landed, sibling)
                fwd.start()
                passed.append(fwd)
        for a in range(n):
            for j, (cx, cy) in enumerate(chips):
                other = _half_rows(outs[a], 2 * cx + cy, 1 - c, halves[a], aligns[a])
                copy(6 * a + 3 + j, other, other, sibling).wait_recv()
        for cp in first + passed:
            cp.wait_send()

    lands = _comm_call(body, shards, [jax.ShapeDtypeStruct((CHIPS,) + a.shape, a.dtype) for a in shards], 6 * n, name)
    return _with_own(lands, shards)


def _with_own(lands, shards):
    me = 2 * lax.axis_index("x") + lax.axis_index("y")
    return [lax.dynamic_update_slice(g, a[None], (me, 0, 0)) for g, a in zip(lands, shards)]


def _pair_swap_halves(arrs, name):
    n = len(arrs)
    halves = [a.shape[1] // 2 for a in arrs]

    def body(*refs):
        ins, outs, (send_sems, recv_sems, _) = refs[:n], refs[n:2 * n], refs[2 * n:]
        x, y, c, _ = _place()
        cps = []
        for a in range(n):
            src = ins[a].at[:, pl.ds(pl.multiple_of((1 - c) * halves[a], 8), halves[a]), :]
            cp = pltpu.make_async_remote_copy(src_ref=src, dst_ref=outs[a], send_sem=send_sems.at[a],
                                              recv_sem=recv_sems.at[a], device_id=(x, y, 1 - c), device_id_type=MESH)
            cp.start()
            cps.append(cp)
        for cp in cps:
            cp.wait()

    return _comm_call(body, arrs, [jax.ShapeDtypeStruct((CHIPS, h, a.shape[2]), a.dtype) for a, h in zip(arrs, halves)],
                      n, name)


def _pair_add(full, got, cidx, name):
    _, half, cols = got.shape
    tr = _row_tile(half, 256)
    nt = half // tr

    grid_spec = pltpu.PrefetchScalarGridSpec(
        num_scalar_prefetch=1, grid=(CHIPS, nt),
        in_specs=[pl.BlockSpec((1, tr, cols), lambda k, r, c_ref: (k, c_ref[0] * nt + r, 0)),
                  pl.BlockSpec((1, tr, cols), lambda k, r, c_ref: (k, r, 0))],
        out_specs=pl.BlockSpec((1, tr, cols), lambda k, r, c_ref: (k, r, 0)))

    def body(c_ref, a_ref, b_ref, o_ref):
        o_ref[...] = a_ref[...] + b_ref[...]

    return pl.pallas_call(
        body, name=name, grid_spec=grid_spec, out_shape=jax.ShapeDtypeStruct(got.shape, got.dtype),
        compiler_params=_cp(("parallel", "parallel")),
    )(cidx, full, got)


def _chip_scatter(parts, name):
    n = len(parts)

    def body(*refs):
        ins, outs, (send_sems, recv_sems, _) = refs[:n], refs[n:2 * n], refs[2 * n:]
        cps = _remote_copies(_scatter_pairs, ins, outs, send_sems, recv_sems)
        for cp in cps:
            cp.start()
        for cp in cps:
            cp.wait()

    return _comm_call(body, parts, [jax.ShapeDtypeStruct((3,) + a.shape[1:], a.dtype) for a in parts], 3 * n, name)


def _pair_join(arrs, name):
    n = len(arrs)
    halves = [a.shape[0] // 2 for a in arrs]

    def body(*refs):
        outs, (send_sems, recv_sems, _) = refs[n:2 * n], refs[2 * n:]
        x, y, c, _ = _place()
        cps = []
        for a in range(n):
            mine = _half_rows(outs[a], None, c, halves[a], 8)
            cp = pltpu.make_async_remote_copy(src_ref=mine, dst_ref=mine, send_sem=send_sems.at[a], recv_sem=recv_sems.at[a],
                                              device_id=(x, y, 1 - c), device_id_type=MESH)
            cp.start()
            cps.append(cp)
        for cp in cps:
            cp.wait()

    return pl.pallas_call(
        body, name=name, in_specs=[ANY] * n, out_specs=[ANY] * n,
        out_shape=[jax.ShapeDtypeStruct(a.shape, a.dtype) for a in arrs],
        input_output_aliases={i: i for i in range(n)}, scratch_shapes=_sems(n),
        compiler_params=pltpu.CompilerParams(has_side_effects=True),
    )(*arrs)


HBM = pl.BlockSpec(memory_space=pltpu.HBM)
SEM = pl.BlockSpec(memory_space=pltpu.SEMAPHORE)
DATAFLOW = pltpu.SideEffectType.DATAFLOW_SIDE_EFFECTING


def _remote_copies(pairs, ins, lands, send_sems, recv_sems):
    return [pltpu.make_async_remote_copy(src_ref=src, dst_ref=dst, send_sem=send_sems.at[i], recv_sem=recv_sems.at[i],
                                         device_id=to, device_id_type=MESH)
            for i, (src, dst, to) in enumerate(pairs(ins, lands))]


def _split_start(srcs, land_shapes, ncopies, pairs, name, after):
    n, m = len(srcs), len(land_shapes)

    def body(*refs):
        ins, lands = refs[:n], refs[n:n + m]
        send_sems, recv_sems, token = refs[n + m + 1], refs[n + m + 2], refs[-1]
        for cp in _remote_copies(pairs, ins, lands, send_sems, recv_sems):
            cp.start()
        token[...] = jnp.zeros_like(token)

    hbm = lambda a: pltpu.with_memory_space_constraint(a, pltpu.HBM)
    lands = [hbm(lax.empty(s.shape, s.dtype)) for s in land_shapes]
    thru = [pltpu.HBM(a.shape, a.dtype) for a in list(srcs) + lands]
    out = pl.pallas_call(
        body, name=name,
        out_shape=(pltpu.SemaphoreType.DMA((ncopies,)), pltpu.SemaphoreType.DMA((ncopies,)), *thru,
                   jax.ShapeDtypeStruct((8, LANES), F32)),
        in_specs=[HBM] * (n + m) + [ANY], out_specs=(SEM, SEM, *[HBM] * (n + m), pl.BlockSpec(memory_space=pltpu.VMEM)),
        input_output_aliases={i: 2 + i for i in range(n + m)},
        compiler_params=pltpu.CompilerParams(has_side_effects=DATAFLOW),
    )(*[hbm(a) for a in srcs], *lands, after)
    return out[0], out[1], list(out[2:2 + n]), list(out[2 + n:2 + n + m]), out[-1]


def _split_wait(send_sems, recv_sems, srcs, lands, after, pairs, name):
    n, m = len(srcs), len(lands)

    def body(*refs):
        ins, lands_ = refs[:n], refs[n:n + m]
        for cp in _remote_copies(pairs, ins, lands_, refs[n + m], refs[n + m + 1]):
            cp.wait_send()
            cp.wait_recv()

    out = pl.pallas_call(
        body, name=name, out_shape=tuple(pltpu.HBM(a.shape, a.dtype) for a in list(srcs) + list(lands)),
        in_specs=[HBM] * (n + m) + [SEM, SEM, ANY], out_specs=tuple([HBM] * (n + m)),
        input_output_aliases={i: i for i in range(n + m)},
        compiler_params=pltpu.CompilerParams(has_side_effects=DATAFLOW),
    )(*srcs, *lands, send_sems, recv_sems, after)
    return list(out[:n]), list(out[n:])


def _gather_pairs(halves, aligns):
    def pairs(ins, lands):
        x, y, c, chips = _place()
        me = 2 * x + y
        return [(_half_rows(ins[a], None, c, halves[a], aligns[a]), _half_rows(lands[a], me, c, halves[a], aligns[a]),
                 (cx, cy, c)) for a in range(len(ins)) for cx, cy in chips]
    return pairs


def _scatter_pairs(ins, lands):
    x, y, c, chips = _place()
    return [(ins[a].at[2 * cx + cy], lands[a].at[j], (cx, cy, c)) for a in range(len(ins)) for j, (cx, cy) in enumerate(chips)]


def _gather_finish(shards, lands, name):
    n = len(shards)
    halves = [a.shape[0] // 2 for a in shards]
    aligns = [_row_align(a.dtype) for a in shards]

    def body(*refs):
        outs, (send_sems, recv_sems, _) = refs[n:2 * n], refs[2 * n:]
        x, y, c, chips = _place()
        passed = []
        for a in range(n):
            for j, (cx, cy) in enumerate(chips):
                landed = _half_rows(outs[a], 2 * cx + cy, c, halves[a], aligns[a])
                cp = pltpu.make_async_remote_copy(src_ref=landed, dst_ref=landed, send_sem=send_sems.at[3 * a + j],
                                                  recv_sem=recv_sems.at[3 * a + j], device_id=(x, y, 1 - c),
                                                  device_id_type=MESH)
                cp.start()
                passed.append(cp)
        for a in range(n):
            for j, (cx, cy) in enumerate(chips):
                other = _half_rows(outs[a], 2 * cx + cy, 1 - c, halves[a], aligns[a])
                pltpu.make_async_remote_copy(src_ref=other, dst_ref=other, send_sem=send_sems.at[3 * a + j],
                                             recv_sem=recv_sems.at[3 * a + j], device_id=(x, y, 1 - c),
                                             device_id_type=MESH).wait_recv()
        for cp in passed:
            cp.wait_send()

    lands = pl.pallas_call(
        body, name=name, in_specs=[ANY] * n, out_specs=[ANY] * n,
        out_shape=[jax.ShapeDtypeStruct(a.shape, a.dtype) for a in lands],
        input_output_aliases={i: i for i in range(n)}, scratch_shapes=_sems(3 * n),
        compiler_params=pltpu.CompilerParams(has_side_effects=True),
    )(*lands)
    return _with_own(lands, shards)


def _sum_own_and_landed(own, landed, where, name):
    _, half, cols = own.shape
    tr = _row_tile(half, 256)
    nt = half // tr

    grid_spec = pltpu.PrefetchScalarGridSpec(
        num_scalar_prefetch=1, grid=(nt,),
        in_specs=[pl.BlockSpec((1, tr, cols), lambda r, w: (w[0], r, 0)),
                  pl.BlockSpec((3, tr, cols), lambda r, w: (0, r, 0))],
        out_specs=pl.BlockSpec((tr, cols), lambda r, w: (w[1] * nt + r, 0)))

    def body(w_ref, p_ref, q_ref, o_ref):
        o_ref[...] = ((p_ref[0] + q_ref[0]) + q_ref[1]) + q_ref[2]

    return pl.pallas_call(
        body, name=name, grid_spec=grid_spec, out_shape=jax.ShapeDtypeStruct((2 * half, cols), own.dtype),
        compiler_params=_cp(("parallel",)),
    )(where, own, landed)


BIG = [("w_in", (D, IN_W), 1), ("w_q_up", (QL, HEADS * QK), 1), ("w_kv_up", (KVL, HEADS * (NOPE + VH)), 1),
       ("w_out", (D, D), 0), ("w_gate", (D, HID), 1), ("w_up", (D, HID), 1), ("w_down", (HID, D), 0)]
SMALL = [("g_mix_norm", (D,)), ("g_q_lat", (QL,)), ("g_kv_lat", (KVL,)), ("g_q_head", (QK,)), ("g_k_head", (QK,)),
         ("g_sgu_v", (SGU,)), ("w_spatial", (HEADS, CHUNK, CHUNK)), ("b_spatial", (HEADS, CHUNK)),
         ("w_pool", (4, 64, 64)), ("pool_scale", (POOL,)), ("g_out_mla", (512,)), ("g_out_sgu", (SGU,)),
         ("g_out_pool", (POOL,)), ("g_ffn_norm", (D,))]
ORDER = ["g_mix_norm", "w_in", "g_q_lat", "w_q_up", "g_kv_lat", "w_kv_up", "g_q_head", "g_k_head", "g_sgu_v",
         "w_spatial", "b_spatial", "w_pool", "pool_scale", "g_out_mla", "g_out_sgu", "g_out_pool", "w_out",
         "g_ffn_norm", "w_gate", "w_up", "w_down"]
EARLY_BIG = ["w_in", "w_q_up", "w_kv_up"]
FFN_BIG = ["w_gate", "w_up", "w_down"]
LATE_BIG = ["w_out"] + FFN_BIG
DEPTH = 2
COLS = 1024
SMALL_N = sum(math.prod(s) for _, s in SMALL) * DEPTH
assert SMALL_N % CHIPS == 0
SMALL_ROWS = -(-(SMALL_N // CHIPS) // (16 * COLS)) * 16


def _unsplit_cols(g):
    return g.transpose(1, 0, 2).reshape(g.shape[1], CHIPS * g.shape[2])


def _split_cols(full):
    r, c = full.shape
    return full.reshape(r, CHIPS, c // CHIPS).transpose(1, 0, 2)


def _kernel_weights(g):
    win = _unsplit_cols(g["w_in"])
    zeros = lambda r, c: jnp.zeros((r, c), BF16)
    o2, o3, o4 = QL + KVL, QL + KVL + ROPE, QL + KVL + ROPE + 2 * SGU
    win_p = jnp.concatenate([win[:, :o2], zeros(D, NOPE), win[:, o2:o3], zeros(D, HP - QK), win[:, o3:o4], win[:, o4:]], axis=1)
    wq = _unsplit_cols(g["w_q_up"]).reshape(QL, HEADS, QK)
    wq_p = jnp.pad(wq, ((0, 0), (0, 0), (0, HP - QK))).reshape(QL, HEADS * HP)
    wkv = _unsplit_cols(g["w_kv_up"]).reshape(KVL, HEADS, NOPE + VH)
    wk_p = jnp.pad(wkv[:, :, :NOPE], ((0, 0), (0, 0), (0, HP - NOPE))).reshape(KVL, HEADS * HP)
    wv_p = wkv[:, :, NOPE:].reshape(KVL, HEADS * VH)
    return dict(win=win_p, wq=wq_p, wk=wk_p, wv=wv_p)


def _small_operands(p, l):
    row = lambda v: v.reshape(1, -1)
    pad = lambda v: jnp.pad(v, (0, HP - QK)).reshape(1, HP)
    wpool = p["w_pool"][l]
    wbd = jnp.zeros((POOL, POOL), F32)
    for g in range(4):
        wbd = lax.dynamic_update_slice(wbd, wpool[g], (g * 64, g * 64))
    return dict(
        g_mix=row(p["g_mix_norm"][l]), gql=row(p["g_q_lat"][l]), gkv=row(p["g_kv_lat"][l]),
        gq=pad(p["g_q_head"][l]), gk=pad(p["g_k_head"][l]), gsv=row(p["g_sgu_v"][l]),
        wsp=p["w_spatial"][l], bsp=jnp.repeat(p["b_spatial"][l].T, SGU // HEADS, axis=1),
        wbd=wbd.astype(BF16), psc=row(p["pool_scale"][l]),
        gout=jnp.concatenate([p["g_out_mla"][l], p["g_out_sgu"][l], p["g_out_pool"][l]]).reshape(1, D),
        g_ffn=row(p["g_ffn_norm"][l]))


def _big_grads(g):
    dwin = g["win"]
    o2 = QL + KVL
    gin = jnp.concatenate([dwin[:, :o2], dwin[:, o2 + NOPE:o2 + NOPE + ROPE], dwin[:, 512:]], axis=1)
    gq = g["wq"].reshape(QL, HEADS, HP)[:, :, :QK].reshape(QL, HEADS * QK)
    gk = g["wk"].reshape(KVL, HEADS, HP)[:, :, :NOPE]
    gv = g["wv"].reshape(KVL, HEADS, VH)
    gkv = jnp.concatenate([gk, gv], axis=2).reshape(KVL, HEADS * (NOPE + VH))
    return {"w_in": _split_cols(gin), "w_q_up": _split_cols(gq), "w_kv_up": _split_cols(gkv),
            "w_out": g["wout"].reshape(CHIPS, D // CHIPS, D), "w_gate": g["wg"], "w_up": g["wu"], "w_down": g["wd"]}


TRANSPOSED = ("w_gate", "w_up")


def _small_grads(g):
    go = g["gout"].reshape(-1)
    return {"g_mix_norm": g["g_mix"].reshape(-1), "g_q_lat": g["gql"].reshape(-1), "g_kv_lat": g["gkv"].reshape(-1),
            "g_q_head": g["gq"].reshape(-1)[:QK], "g_k_head": g["gk"].reshape(-1)[:QK], "g_sgu_v": g["gsv"].reshape(-1),
            "w_spatial": g["wsp"], "b_spatial": g["bsp"].reshape(CHUNK, HEADS, SGU // HEADS).sum(-1).T,
            "w_pool": jnp.stack([g["wbd"][i * 64:(i + 1) * 64, i * 64:(i + 1) * 64] for i in range(4)]),
            "pool_scale": g["psc"].reshape(-1), "g_out_mla": go[:512], "g_out_sgu": go[512:768],
            "g_out_pool": go[768:], "g_ffn_norm": g["g_ffn"].reshape(-1)}


def _pack_small_grads(small):
    sm = jnp.concatenate([small[l][n].reshape(-1) for l in range(DEPTH) for n, _ in SMALL]).reshape(CHIPS, SMALL_N // CHIPS)
    return jnp.pad(sm, ((0, 0), (0, SMALL_ROWS * COLS - SMALL_N // CHIPS))).reshape(CHIPS, SMALL_ROWS, COLS)


def _unpack_small_grads(gathered):
    flat = gathered.reshape(CHIPS, SMALL_ROWS * COLS)[:, :SMALL_N // CHIPS].reshape(-1)
    out, off = [], 0
    for _ in range(DEPTH):
        layer = {}
        for n, shape in SMALL:
            k = math.prod(shape)
            layer[n] = flat[off:off + k].reshape(shape)
            off += k
        out.append(layer)
    return out


def _layer_fwd(x, tabs, kw, late_weights, sp, l):
    t = f"_l{l}"
    z, hb = _in_proj_fwd(x, sp["g_mix"], kw["win"], "in_proj_fwd" + t)
    q, k, v = _mla_prep_fwd(z, tabs, sp["gql"], sp["gkv"], sp["gq"], sp["gk"], kw["wq"], kw["wk"], kw["wv"],
                            "mla_prep_fwd" + t)
    o, lse = _attn_fwd(q, k, v, "attn_fwd" + t)
    m = _pool_win_fwd(z, "pool_win_fwd" + t)
    wout, wg, wu, wd = late_weights(o)
    wout = wout.reshape(D, D)
    x1, mix = _mix_out_fwd(o, z, m, x, sp["wsp"], sp["bsp"], sp["wbd"], sp["psc"], sp["gsv"], sp["gout"], wout,
                           "mix_out_fwd" + t)
    x2, a, b, h2 = _ffn_fwd(x1, sp["g_ffn"], wg, wu, wd, "ffn_fwd" + t)
    saved = dict(x=x, z=z, hb=hb, q=q, k=k, v=v, o=o, lse=lse, m=m, x1=x1, mix=mix, a=a, b=b, h2=h2, wg=wg, wu=wu, wd=wd,
                 wout=wout)
    return x2, saved


def _layer_bwd(dx2, sv, tabs, kw, sp, l, ffn_hook):
    t = f"_l{l}"
    g = {}
    dx1, hid, da, db, g["g_ffn"] = _ffn_bwd(dx2, sv["x1"], sv["a"], sv["b"], sp["g_ffn"], sv["wg"], sv["wu"], sv["wd"],
                                            "ffn_bwd" + t)
    g["wd"] = _wgrad_rows(hid, dx2, "wgrad_down" + t)
    g["wg"] = _wgrad_rows(da, sv["h2"], "wgrad_gate" + t)
    g["wu"] = _wgrad_rows(db, sv["h2"], "wgrad_up" + t)
    gout = sp["gout"] + ffn_hook(g)
    do, delta, duv, dm, g["gout"], g["gsv"], g["psc"], g["wsp"], g["bsp"], g["wbd"] = _mix_out_bwd(
        dx1, sv["o"], sv["z"], sv["m"], sp["wsp"], sp["bsp"], sp["wbd"], sp["psc"], sp["gsv"], gout, sv["wout"],
        "mix_out_bwd" + t)
    g["wout"] = _wgrad(sv["mix"], dx1, "wgrad_out" + t)
    dp = _pool_win_bwd(dm, "pool_win_bwd" + t)
    dq, dk, dv = _attn_bwd(sv["q"], sv["k"], sv["v"], do, sv["lse"], delta, "attn_bwd" + t)
    dzm, qn, kvn, dqr, dkr, dvr, g["gql"], g["gkv"], g["gq"], g["gk"] = _mla_prep_bwd(
        dq, dk, dv, sv["z"], tabs, sp["gql"], sp["gkv"], sp["gq"], sp["gk"], kw["wq"], kw["wk"], kw["wv"],
        "mla_prep_bwd" + t)
    g["wq"] = _wgrad(qn, dqr, "wgrad_q_up" + t)
    g["wk"] = _wgrad(kvn, dkr, "wgrad_k_up" + t)
    g["wv"] = _wgrad(kvn, dvr, "wgrad_v_up" + t)
    dx, g["g_mix"] = _in_proj_bwd(dzm, duv, dp, sv["x"], dx1, sp["g_mix"], kw["win"], "in_proj_bwd" + t)
    g["win"] = jnp.concatenate([_wgrad(sv["hb"], dzm, "wgrad_in_a" + t), _wgrad(sv["hb"], duv, "wgrad_in_b" + t),
                                _wgrad(sv["hb"], dp, "wgrad_in_c" + t)], axis=1)
    return dx, g


def _rope_inv_freq():
    half = ROPE // 2
    inv = 1.0 / (ROPE_THETA ** (jnp.arange(half, dtype=F32) / half))
    return jnp.concatenate([jnp.zeros((NOPE,), F32), inv, inv, jnp.zeros((HP - QK,), F32)]).reshape(1, HP)


def kernel(x, positions, g_mix_norm, w_in, g_q_lat, w_q_up, g_kv_lat, w_kv_up, g_q_head, g_k_head, g_sgu_v, w_spatial, b_spatial, w_pool, pool_scale, g_out_mla, g_out_sgu, g_out_pool, w_out, g_ffn_norm, w_gate, w_up, w_down, loss_target, m_g_mix_norm, m_w_in, m_g_q_lat, m_w_q_up, m_g_kv_lat, m_w_kv_up, m_g_q_head, m_g_k_head, m_g_sgu_v, m_w_spatial, m_b_spatial, m_w_pool, m_pool_scale, m_g_out_mla, m_g_out_sgu, m_g_out_pool, m_w_out, m_g_ffn_norm, m_w_gate, m_w_up, m_w_down, v_g_mix_norm, v_w_in, v_g_q_lat, v_w_q_up, v_g_kv_lat, v_w_kv_up, v_g_q_head, v_g_k_head, v_g_sgu_v, v_w_spatial, v_b_spatial, v_w_pool, v_pool_scale, v_g_out_mla, v_g_out_sgu, v_g_out_pool, v_w_out, v_g_ffn_norm, v_w_gate, v_w_up, v_w_down):
    given = dict(locals())
    p = {n: given[n] for n in ORDER}
    view = lambda pre, n: jnp.swapaxes(given[pre + n], 1, 2) if n in TRANSPOSED else given[pre + n]
    seq = x.shape[1]
    cidx = lax.axis_index("c").astype(jnp.int32).reshape(1)
    where = jnp.stack([2 * lax.axis_index("x") + lax.axis_index("y"), lax.axis_index("c")]).astype(jnp.int32)
    shards = lambda names: [view("", n)[l].astype(BF16) for l, n in names]
    zero11 = lambda token: token[:1, :1]

    names_0a = [(0, n) for n in EARLY_BIG]
    names_0b = [(0, n) for n in LATE_BIG]
    names_1 = [(1, n) for n, _, _ in BIG]
    got_0a = dict(zip(EARLY_BIG, _all_gather_chips(shards(names_0a), "all_gather_w0a")))
    started, issued = {}, got_0a["w_in"]
    for tag, names in (("w0b", names_0b), ("w1", names_1)):
        sh = shards(names)
        pairs = _gather_pairs([a.shape[0] // 2 for a in sh], [_row_align(a.dtype) for a in sh])
        lands = [jax.ShapeDtypeStruct((CHIPS,) + a.shape, a.dtype) for a in sh]
        started[tag] = (sh, pairs) + _split_start(sh, lands, 3 * len(sh), pairs, "gather_start_" + tag, issued)
        issued = started[tag][6]

    def arrived(tag, after):
        _, pairs, send, recv, srcs, lands, _ = started[tag]
        srcs, lands = _split_wait(send, recv, srcs, lands, after, pairs, "gather_wait_" + tag)
        return _gather_finish(srcs, lands, "gather_finish_" + tag)

    layer1 = {}

    def mix_weights(l, h):
        if l == 0:
            return got_0a
        layer1.update(zip([n for _, n in names_1], arrived("w1", h)))
        return layer1

    def late_weights(l, o):
        return arrived("w0b", o) if l == 0 else [layer1[n] for n in LATE_BIG]

    reducing, last = {}, {}

    def reduce_start(tag, arrs):
        theirs = _pair_swap_halves(arrs, "grad_pair_swap_" + tag)
        pair = [_pair_add(a, t, cidx, f"grad_pair_add_{tag}_{i}") for i, (a, t) in enumerate(zip(arrs, theirs))]
        lands = [jax.ShapeDtypeStruct((3,) + a.shape[1:], a.dtype) for a in pair]
        reducing[tag] = _split_start(pair, lands, 3 * len(pair), _scatter_pairs, "grad_scatter_start_" + tag, cidx)
        return zero11(reducing[tag][4])

    def reduce_finish(tag, after):
        send, recv, srcs, lands, _ = reducing[tag]
        srcs, lands = _split_wait(send, recv, srcs, lands, after, _scatter_pairs, "grad_scatter_wait_" + tag)
        return [_sum_own_and_landed(a, q, where, f"grad_sum_{tag}_{i}") for i, (a, q) in enumerate(zip(srcs, lands))]

    def ffn_hook(l, g):
        if l == 1:
            return jnp.zeros((1, 1), F32)
        return reduce_start("g0b", [g["wg"], g["wu"], g["wd"]])

    def layer_hook(l, big, small):
        last[l] = (big, small)
        if l == 1:
            return reduce_start("g1", [big[n] for n, _, _ in BIG])
        return None

    entry = zero11(started["w0b"][6]) + zero11(started["w1"][6])
    loss_part, dx = _step(x.reshape(seq, D), positions.reshape(seq, 1), loss_target.reshape(seq, D), p, entry,
                          mix_weights, late_weights, ffn_hook, layer_hook)
    loss = lax.psum(loss_part, ("x", "y", "c"))

    def adamw(n, g0, g1):
        w = view("", n)
        three_d = (DEPTH, -1, w.shape[-1])
        res = _adamw(w.reshape(three_d), g0.reshape(three_d[1:]), g1.reshape(three_d[1:]),
                     view("m_", n).reshape(three_d), view("v_", n).reshape(three_d), "adamw_" + n)
        return [r.reshape(w.shape) for r in res]

    names_rest = [(0, n) for n in EARLY_BIG + ["w_out"]]
    reduce_start("g0a", [last[0][0][n] for _, n in names_rest] + [_pack_small_grads([last[l][1] for l in range(DEPTH)])])
    token = reducing["g0a"][4]
    early = names_1 + [(0, n) for n in FFN_BIG]
    sums = dict(zip(early, _pair_join(reduce_finish("g1", token) + reduce_finish("g0b", token), "grad_pair_join_early")))
    out = {n: adamw(n, sums[(0, n)], sums[(1, n)]) for n in FFN_BIG}
    late = names_rest + ["small"]
    sums.update(zip(late, _pair_join(reduce_finish("g0a", out["w_down"][1]), "grad_pair_join_late")))
    gsmall = _unpack_small_grads(_all_gather_chips([sums["small"]], "all_gather_small_grads")[0])
    for n in ORDER:
        if n not in out:
            g = [sums[(l, n)] for l in range(DEPTH)] if (0, n) in sums else [gsmall[l][n] for l in range(DEPTH)]
            out[n] = adamw(n, *g)
    undo = lambda n, a: jnp.swapaxes(a, 1, 2) if n in TRANSPOSED else a
    return (loss, dx.reshape(x.shape), *[undo(n, out[n][i]) for i in range(4) for n in ORDER])


def _step(xs, pos, tgt, p, entry, mix_weights, late_weights, ffn_hook, layer_hook):
    sps = [_small_operands(p, l) for l in range(DEPTH)]
    sps[0]["g_mix"] = sps[0]["g_mix"] + entry
    tabs = _rope_tables(pos, _rope_inv_freq())
    saved, h = [], xs
    for l in range(DEPTH):
        kw = _kernel_weights(mix_weights(l, h))
        h, sv = _layer_fwd(h, tabs, kw, functools.partial(late_weights, l), sps[l], l)
        saved.append(dict(sv, kw=kw))
    dy, lpart = _loss_grad(h, tgt)
    for l in reversed(range(DEPTH)):
        dy, g = _layer_bwd(dy, saved[l], tabs, saved[l]["kw"], sps[l], l, functools.partial(ffn_hook, l))
        zero = layer_hook(l, _big_grads(g), _small_grads(g))
        if zero is not None and l > 0:
            sps[l - 1]["g_ffn"] = sps[l - 1]["g_ffn"] + zero
    return 0.5 / D * jnp.sum(lpart), dy
```

```python
import functools
import math

import jax
import jax.numpy as jnp
from jax import lax
from jax.experimental import pallas as pl
from jax.experimental.pallas import tpu as pltpu

F32 = jnp.float32
BF16 = jnp.bfloat16
MESH = pl.DeviceIdType.MESH

D = 1024
HEADS = 4
QK = 96
NOPE = 64
ROPE = 32
VH = 128
HP = 128
QL = 256
KVL = 128
SGU = 256
POOL = 256
CHUNK = 128
HID = 2816
CHIPS = 4
SH = HID // CHIPS
IN_W = 1184
IN_P = 1280
EPS = 1e-6
ROPE_THETA = 10000.0
SCALE = 1.0 / math.sqrt(QK)
LOG2E = 1.4426950408889634
EXP2_C = SCALE * LOG2E
ATT_SPLIT = 2
ATT_WIDE = 4
NEG = -1e30
HALO = 16

LR, B1, B2, ADAM_EPS, WD, STEP = 0.001, 0.9, 0.999, 1e-08, 0.01, 10

VMEM_LIMIT = 56 * 1024 * 1024
LANES = 128


def _cp(sem, vmem=None):
    return pltpu.CompilerParams(dimension_semantics=sem, vmem_limit_bytes=vmem)


def _res(shape):
    nd = len(shape)
    return pl.BlockSpec(shape, lambda *_: (0,) * nd, pipeline_mode=pl.Buffered(1))


def _acc(shape):
    nd = len(shape)
    return pl.BlockSpec(shape, lambda *_: (0,) * nd)


def _dot(a, b):
    return jnp.dot(a, b, preferred_element_type=F32)


def _dot_nt(a, b):
    return lax.dot_general(a, b, (((1,), (1,)), ((), ())), preferred_element_type=F32)


def _dot_tn(a, b):
    return lax.dot_general(a, b, (((0,), (0,)), ((), ())), preferred_element_type=F32)


def _rms(x, n):
    r = lax.rsqrt(jnp.sum(x * x, axis=-1, keepdims=True) * (1.0 / n) + EPS)
    return x * r, r


def _rms_bwd(xn, r, g, dy, n):
    dn = dy * g
    dx = r * (dn - xn * (jnp.sum(dn * xn, axis=-1, keepdims=True) * (1.0 / n)))
    return dx, jnp.sum(dy * xn, axis=0, keepdims=True)


def _accumulate(ref, val, first):
    @pl.when(first)
    def _():
        ref[...] = val

    @pl.when(jnp.logical_not(first))
    def _():
        ref[...] += val


def _accumulate0(ref, val, first):
    @pl.when(first)
    def _():
        ref[0] = val

    @pl.when(jnp.logical_not(first))
    def _():
        ref[0] += val


def _tile(s, t):
    return min(s, t)


def _row_tile(r, cap):
    if r <= cap:
        return r
    return max(t for t in range(8, cap + 1, 8) if r % t == 0)


def _rope_tables(pos, invf):
    s = pos.shape[0]
    tm = _tile(s, 1024)

    def body(pos_ref, invf_ref, c_ref, sa_ref, sb_ref):
        ang = pos_ref[...].astype(F32) * invf_ref[...]
        c, sn = jnp.cos(ang), jnp.sin(ang)
        lane = lax.broadcasted_iota(jnp.int32, ang.shape, 1)
        first = (lane >= NOPE) & (lane < NOPE + ROPE // 2)
        second = (lane >= NOPE + ROPE // 2) & (lane < QK)
        c_ref[...] = jnp.where(first | second, c, 1.0)
        sa_ref[...] = jnp.where(first, -sn, 0.0)
        sb_ref[...] = jnp.where(second, sn, 0.0)

    out = jax.ShapeDtypeStruct((s, HP), F32)
    return pl.pallas_call(
        body, name="rope_tables", grid=(s // tm,),
        in_specs=[pl.BlockSpec((tm, 1), lambda i: (i, 0)), _acc((1, HP))],
        out_specs=[pl.BlockSpec((tm, HP), lambda i: (i, 0))] * 3,
        out_shape=[out] * 3, compiler_params=_cp(("parallel",)),
    )(pos, invf)


def _rope(x, c, sa, sb):
    return x * c + pltpu.roll(x, HP - ROPE // 2, 1) * sa + pltpu.roll(x, ROPE // 2, 1) * sb


def _rope_t(d, c, sa, sb):
    return d * c + pltpu.roll(d * sa, ROPE // 2, 1) + pltpu.roll(d * sb, HP - ROPE // 2, 1)


def _in_proj_fwd(x, g, w, name):
    s = x.shape[0]
    tm = _tile(s, 512)

    def body(x_ref, g_ref, w_ref, z_ref, h_ref):
        xn, _ = _rms(x_ref[...], D)
        h = (xn * g_ref[...]).astype(BF16)
        h_ref[...] = h
        z_ref[...] = _dot(h, w_ref[...])

    return pl.pallas_call(
        body, name=name, grid=(s // tm,),
        in_specs=[pl.BlockSpec((tm, D), lambda i: (i, 0)), _acc((1, D)), _res((D, IN_P))],
        out_specs=[pl.BlockSpec((tm, IN_P), lambda i: (i, 0)), pl.BlockSpec((tm, D), lambda i: (i, 0))],
        out_shape=[jax.ShapeDtypeStruct((s, IN_P), F32), jax.ShapeDtypeStruct((s, D), BF16)],
        compiler_params=_cp(("parallel",), VMEM_LIMIT),
    )(x, g, w)


def _mla_prep_fwd(z, tabs, gql, gkv, gq, gk, wq, wk, wv, name):
    s = z.shape[0]
    tm = _tile(s, 512)

    def body(ql_ref, kv_ref, kr_ref, c_ref, sa_ref, sb_ref, gql_ref, gkv_ref, gq_ref, gk_ref,
             wq_ref, wk_ref, wv_ref, q_out, k_out, v_out):
        qn = (_rms(ql_ref[...], QL)[0] * gql_ref[...]).astype(BF16)
        kvn = (_rms(kv_ref[...], KVL)[0] * gkv_ref[...]).astype(BF16)
        qraw = _dot(qn, wq_ref[...])
        kraw = _dot(kvn, wk_ref[...])
        vraw = _dot(kvn, wv_ref[...])
        kr = kr_ref[...]
        c, sa, sb = c_ref[...], sa_ref[...], sb_ref[...]
        for h in range(HEADS):
            sl = slice(h * HP, (h + 1) * HP)
            xq = _rms(qraw[:, sl], QK)[0] * gq_ref[...]
            q_out[h] = _rope(xq, c, sa, sb).astype(BF16)
            xk = _rms(kraw[:, sl] + kr, QK)[0] * gk_ref[...]
            k_out[h] = _rope(xk, c, sa, sb).astype(BF16)
            v_out[h] = vraw[:, sl].astype(BF16)

    row = lambda w, j: pl.BlockSpec((tm, w), lambda i: (i, j))
    hspec = pl.BlockSpec((HEADS, tm, HP), lambda i: (0, i, 0))
    hshape = jax.ShapeDtypeStruct((HEADS, s, HP), BF16)
    return pl.pallas_call(
        body, name=name, grid=(s // tm,),
        in_specs=[row(QL, 0), row(KVL, 2), row(HP, 3), row(HP, 0), row(HP, 0), row(HP, 0),
                  _acc((1, QL)), _acc((1, KVL)), _acc((1, HP)), _acc((1, HP)),
                  _acc((QL, HEADS * HP)), _acc((KVL, HEADS * HP)), _acc((KVL, HEADS * HP))],
        out_specs=[hspec] * 3, out_shape=[hshape] * 3,
        compiler_params=_cp(("parallel",)),
    )(z, z, z, *tabs, gql, gkv, gq, gk, wq, wk, wv)


def _causal_mask(s, row0):
    row = lax.broadcasted_iota(jnp.int32, s.shape, 0) + row0
    col = lax.broadcasted_iota(jnp.int32, s.shape, 1)
    return jnp.where(col <= row, s, NEG)


def _attn_fwd(q, k, v, name):
    s = q.shape[1]
    tq = _tile(s, 512)
    wide = ATT_WIDE * tq if s % (ATT_WIDE * tq) == 0 else tq
    rh = tq // ATT_SPLIT

    def body(q_ref, k_ref, v_ref, o_ref, lse_ref):
        i = pl.program_id(1)

        def blk(off, tk, carry, masked):
            keys = [(g + 1) * rh if masked else tq for g in range(ATT_SPLIT)]
            rows = lambda t: pl.ds(pl.multiple_of(off + t * tq, tq), tq)
            score = lambda g, t: _dot_nt(q_ref[0, g * rh:(g + 1) * rh, :], k_ref[0, rows(t), :][:keys[g]])
            state = list(carry)
            scs = {(g, 0): score(g, 0) for g in range(ATT_SPLIT)}
            for t in range(tk // tq):
                if (t + 1) * tq < tk:
                    scs.update({(g, t + 1): score(g, t + 1) for g in range(ATT_SPLIT)})
                vt = v_ref[0, rows(t), :]
                for g, (m, l, acc) in enumerate(state):
                    sc = scs.pop((g, t))
                    if masked:
                        sc = _causal_mask(sc, g * rh)
                    m_new = jnp.maximum(m, jnp.max(sc, axis=-1, keepdims=True))
                    p = jnp.exp2((sc - m_new) * EXP2_C)
                    alpha = jnp.exp2((m - m_new) * EXP2_C)
                    l = alpha * l + jnp.sum(p, axis=-1, keepdims=True)
                    acc = alpha * acc + _dot(p.astype(BF16), vt[:keys[g]])
                    state[g] = (m_new, l, acc)
            return tuple(state)

        one = (jnp.full((rh, 1), NEG, F32), jnp.zeros((rh, 1), F32), jnp.zeros((rh, VH), F32))
        nwide = (i * tq) // wide
        carry = lax.fori_loop(0, nwide, lambda j, c: blk(j * wide, wide, c, False), (one,) * ATT_SPLIT)
        carry = lax.fori_loop(nwide * (wide // tq), i, lambda j, c: blk(j * tq, tq, c, False), carry)
        carry = blk(i * tq, tq, carry, True)
        for g, (m, l, acc) in enumerate(carry):
            o_ref[g * rh:(g + 1) * rh, :] = acc / l
            lse_ref[0, g * rh:(g + 1) * rh, :] = jnp.broadcast_to(m * EXP2_C + jnp.log(l) * LOG2E, (rh, LANES))

    return pl.pallas_call(
        body, name=name, grid=(HEADS, s // tq),
        in_specs=[pl.BlockSpec((1, tq, HP), lambda h, i: (h, i, 0)),
                  pl.BlockSpec((1, s, HP), lambda h, i: (h, 0, 0)),
                  pl.BlockSpec((1, s, HP), lambda h, i: (h, 0, 0))],
        out_specs=[pl.BlockSpec((tq, VH), lambda h, i: (i, h)),
                   pl.BlockSpec((1, tq, LANES), lambda h, i: (h, i, 0))],
        out_shape=[jax.ShapeDtypeStruct((s, HEADS * VH), F32), jax.ShapeDtypeStruct((HEADS, s, LANES), F32)],
        compiler_params=_cp(("parallel", "arbitrary"), VMEM_LIMIT),
    )(q, k, v)


def _lane_group(shape, j):
    return (lax.broadcasted_iota(jnp.int32, shape, 1) + j * LANES) // (POOL // 4)


def _pool_win_fwd(z, name):
    s = z.shape[0]
    ch = _tile(s, 512)
    col0 = (IN_P - POOL) // LANES

    def body(p_ref, m_ref):
        j = pl.program_id(0)

        def chunk(r, _):
            off = pl.multiple_of(r * ch, ch)
            cur = p_ref[pl.ds(off, ch), :]
            hoff = pl.multiple_of(jnp.maximum(off - HALO, 0), 8)
            halo = jnp.where(r > 0, p_ref[pl.ds(hoff, HALO), :], 0.0)
            x = jnp.concatenate([halo, cur], axis=0)
            s2 = x + pltpu.roll(x, 1, 0)
            s4 = s2 + pltpu.roll(s2, 2, 0)
            s8 = s4 + pltpu.roll(s4, 4, 0)
            s16 = s8 + pltpu.roll(s8, 8, 0)
            grp = _lane_group((ch, LANES), j)
            sel = jnp.where(grp == 0, s2[HALO:], jnp.where(grp == 1, s4[HALO:], jnp.where(grp == 2, s8[HALO:], s16[HALO:])))
            t1 = (lax.broadcasted_iota(jnp.int32, (ch, LANES), 0) + off + 1).astype(F32)
            win = jnp.where(grp == 0, 2.0, jnp.where(grp == 1, 4.0, jnp.where(grp == 2, 8.0, 16.0)))
            m_ref[pl.ds(off, ch), :] = sel / jnp.minimum(t1, win) - cur
            return 0

        lax.fori_loop(0, s // ch, chunk, 0)

    return pl.pallas_call(
        body, name=name, grid=(POOL // LANES,),
        in_specs=[pl.BlockSpec((s, LANES), lambda j: (0, col0 + j))],
        out_specs=pl.BlockSpec((s, LANES), lambda j: (0, j)),
        out_shape=jax.ShapeDtypeStruct((s, POOL), F32),
        compiler_params=_cp(("parallel",), VMEM_LIMIT),
    )(z)


def _pool_win_bwd(dm, name):
    s = dm.shape[0]
    ch = _tile(s, 512)
    n = s // ch

    def body(dm_ref, dp_ref):
        j = pl.program_id(0)

        def chunk(r, _):
            off = pl.multiple_of(r * ch, ch)
            grp = _lane_group((ch + HALO, LANES), j)
            win = jnp.where(grp == 0, 2.0, jnp.where(grp == 1, 4.0, jnp.where(grp == 2, 8.0, 16.0)))
            cur = dm_ref[pl.ds(off, ch), :]
            hoff = pl.multiple_of(jnp.minimum(off + ch, s - HALO), 8)
            halo = jnp.where(r < n - 1, dm_ref[pl.ds(hoff, HALO), :], 0.0)
            x = jnp.concatenate([cur, halo], axis=0)
            t1 = (lax.broadcasted_iota(jnp.int32, (ch + HALO, LANES), 0) + off + 1).astype(F32)
            e = x / jnp.minimum(t1, win)
            tot = ch + HALO
            r2 = e + pltpu.roll(e, tot - 1, 0)
            r4 = r2 + pltpu.roll(r2, tot - 2, 0)
            r8 = r4 + pltpu.roll(r4, tot - 4, 0)
            r16 = r8 + pltpu.roll(r8, tot - 8, 0)
            g = grp[:ch]
            sel = jnp.where(g == 0, r2[:ch], jnp.where(g == 1, r4[:ch], jnp.where(g == 2, r8[:ch], r16[:ch])))
            dp_ref[pl.ds(off, ch), :] = (sel - cur).astype(BF16)
            return 0

        lax.fori_loop(0, n, chunk, 0)

    return pl.pallas_call(
        body, name=name, grid=(POOL // LANES,),
        in_specs=[pl.BlockSpec((s, LANES), lambda j: (0, j))],
        out_specs=pl.BlockSpec((s, LANES), lambda j: (0, j)),
        out_shape=jax.ShapeDtypeStruct((s, POOL), BF16),
        compiler_params=_cp(("parallel",), VMEM_LIMIT),
    )(dm)


def _head_mask(h):
    lane = lax.broadcasted_iota(jnp.int32, (CHUNK, SGU), 1)
    return (lane // (SGU // HEADS)) == h


def _tril(upper=False):
    row = lax.broadcasted_iota(jnp.int32, (CHUNK, CHUNK), 0)
    col = lax.broadcasted_iota(jnp.int32, (CHUNK, CHUNK), 1)
    return col >= row if upper else col <= row


def _sgu_gate(vn, wsp, bsp):
    out = []
    for cidx in range(vn.shape[0] // CHUNK):
        vc = vn[cidx * CHUNK:(cidx + 1) * CHUNK]
        zc = bsp
        for h in range(HEADS):
            zc = zc + jnp.where(_head_mask(h), _dot(wsp[h], vc), 0.0)
        out.append(zc)
    return jnp.concatenate(out, axis=0)


def _mix_out_fwd(o, z, m, x, wsp, bsp, wbd, psc, gsv, gout, wout, name):
    s = x.shape[0]
    tm = _tile(s, 512)

    def body(o_ref, uv_ref, m_ref, x_ref, wsp_ref, bsp_ref, wbd_ref, psc_ref, gsv_ref, gout_ref, wout_ref,
             x1_ref, mix_ref):
        g = gout_ref[...]
        an = _rms(o_ref[...], HEADS * VH)[0] * g[:, :512]
        uv = uv_ref[...]
        u, v = uv[:, :SGU], uv[:, SGU:]
        vn = (_rms(v, SGU)[0] * gsv_ref[...]).astype(BF16)
        tri = _tril()
        wsp_m = [jnp.where(tri, wsp_ref[h], 0.0).astype(BF16) for h in range(HEADS)]
        gm = u * _sgu_gate(vn, wsp_m, bsp_ref[...])
        gn = _rms(gm, SGU)[0] * g[:, 512:768]
        po = _dot(m_ref[...].astype(BF16), wbd_ref[...]) * psc_ref[...]
        pn = _rms(po, POOL)[0] * g[:, 768:]
        mix = jnp.concatenate([an, gn, pn], axis=1).astype(BF16)
        mix_ref[...] = mix
        x1_ref[...] = x_ref[...] + _dot(mix, wout_ref[...])

    row = lambda w, j: pl.BlockSpec((tm, w), lambda i: (i, j))
    return pl.pallas_call(
        body, name=name, grid=(s // tm,),
        in_specs=[row(512, 0), row(512, 1), row(POOL, 0), row(D, 0),
                  _acc((HEADS, CHUNK, CHUNK)), _acc((CHUNK, SGU)), _acc((POOL, POOL)), _acc((1, POOL)),
                  _acc((1, SGU)), _acc((1, D)), _res((D, D))],
        out_specs=[row(D, 0), row(D, 0)],
        out_shape=[jax.ShapeDtypeStruct((s, D), F32), jax.ShapeDtypeStruct((s, D), BF16)],
        compiler_params=_cp(("parallel",), VMEM_LIMIT),
    )(o, z, m, x, wsp, bsp, wbd, psc, gsv, gout, wout)


def _ffn_fwd(x1, g, wg, wu, wd, name):
    s = x1.shape[0]
    tm = _tile(s, 256)

    def body(x_ref, g_ref, wg_ref, wu_ref, wd_ref, x2_ref, a_ref, b_ref, h_ref):
        x = x_ref[...]
        h = (_rms(x, D)[0] * g_ref[...]).astype(BF16)
        h_ref[...] = h
        acc = jnp.zeros((tm, D), F32)
        for k in range(CHIPS):
            a = _dot_nt(h, wg_ref[k])
            b = _dot_nt(h, wu_ref[k])
            a_ref[k] = a
            b_ref[k] = b
            acc = acc + _dot((a * jax.nn.sigmoid(a) * b).astype(BF16), wd_ref[k])
        x2_ref[...] = x + acc

    row = lambda w: pl.BlockSpec((tm, w), lambda i: (i, 0))
    hrow = pl.BlockSpec((CHIPS, tm, SH), lambda i: (0, i, 0))
    hshape = jax.ShapeDtypeStruct((CHIPS, s, SH), F32)
    return pl.pallas_call(
        body, name=name, grid=(s // tm,),
        in_specs=[row(D), _acc((1, D)), _res((CHIPS, SH, D)), _res((CHIPS, SH, D)), _res((CHIPS, SH, D))],
        out_specs=[row(D), hrow, hrow, row(D)],
        out_shape=[jax.ShapeDtypeStruct((s, D), F32), hshape, hshape, jax.ShapeDtypeStruct((s, D), BF16)],
        compiler_params=_cp(("parallel",), VMEM_LIMIT),
    )(x1, g, wg, wu, wd)


def _loss_grad(y, tgt):
    s = y.shape[0]
    tm = _tile(s, 512)

    def body(y_ref, t_ref, dy_ref, l_ref):
        e = y_ref[...] - t_ref[...]
        dy_ref[...] = e * (1.0 / D)
        sq = jnp.sum(e * e, axis=0, keepdims=True)
        part = sq[:, :LANES]
        for c in range(1, D // LANES):
            part = part + sq[:, c * LANES:(c + 1) * LANES]
        _accumulate(l_ref, part, pl.program_id(0) == 0)

    row = pl.BlockSpec((tm, D), lambda i: (i, 0))
    return pl.pallas_call(
        body, name="loss_grad", grid=(s // tm,),
        in_specs=[row, row], out_specs=[row, _acc((1, LANES))],
        out_shape=[jax.ShapeDtypeStruct((s, D), F32), jax.ShapeDtypeStruct((1, LANES), F32)],
        compiler_params=_cp(("arbitrary",)),
    )(y, tgt)


def _wgrad(a, b, name):
    s, k = a.shape
    n = b.shape[1]
    half = lambda v: v if v <= 1408 else v // 2
    kb, nb, tt = half(k), half(n), _tile(s, 2048)

    def body(a_ref, b_ref, o_ref):
        _accumulate(o_ref, _dot_tn(a_ref[...].astype(BF16), b_ref[...].astype(BF16)), pl.program_id(2) == 0)

    return pl.pallas_call(
        body, name=name, grid=(k // kb, n // nb, s // tt),
        in_specs=[pl.BlockSpec((tt, kb), lambda i, j, t: (t, i)), pl.BlockSpec((tt, nb), lambda i, j, t: (t, j))],
        out_specs=pl.BlockSpec((kb, nb), lambda i, j, t: (i, j)),
        out_shape=jax.ShapeDtypeStruct((k, n), F32),
        compiler_params=_cp(("parallel", "parallel", "arbitrary"), VMEM_LIMIT),
    )(a, b)


def _wgrad_in(h, dzm, duv, dp, name):
    s = h.shape[0]
    tt = _tile(s, 2048)

    def body(h_ref, a_ref, b_ref, c_ref, o_ref):
        hv = h_ref[...]
        val = jnp.concatenate([_dot_tn(hv, a_ref[...]), _dot_tn(hv, b_ref[...]), _dot_tn(hv, c_ref[...])], axis=1)
        _accumulate(o_ref, val, pl.program_id(0) == 0)

    row = lambda w: pl.BlockSpec((tt, w), lambda t: (t, 0))
    return pl.pallas_call(
        body, name=name, grid=(s // tt,), in_specs=[row(D), row(512), row(512), row(POOL)], out_specs=_acc((D, IN_P)),
        out_shape=jax.ShapeDtypeStruct((D, IN_P), F32), compiler_params=_cp(("arbitrary",), VMEM_LIMIT),
    )(h, dzm, duv, dp)


def _wgrad_rows(a, b, name):
    s, n = a.shape[1:]
    nn = b.shape[1]
    tt = _tile(s, 4096 if b.dtype == BF16 else 2048)

    def body(a_ref, b_ref, o_ref):
        _accumulate0(o_ref, _dot_tn(a_ref[0].astype(BF16), b_ref[...].astype(BF16)), pl.program_id(1) == 0)

    return pl.pallas_call(
        body, name=name, grid=(CHIPS, s // tt),
        in_specs=[pl.BlockSpec((1, tt, n), lambda c, t: (c, t, 0)), pl.BlockSpec((tt, nn), lambda c, t: (t, 0))],
        out_specs=pl.BlockSpec((1, n, nn), lambda c, t: (c, 0, 0)),
        out_shape=jax.ShapeDtypeStruct((CHIPS, n, nn), F32),
        compiler_params=_cp(("parallel", "arbitrary"), VMEM_LIMIT),
    )(a, b)


def _ffn_bwd(dx2, x1, a, b, g, wg, wu, wd, name):
    s = x1.shape[0]
    tm = _tile(s, 256)

    def body(dx2_ref, x_ref, a_ref, b_ref, g_ref, wg_ref, wu_ref, wd_ref,
             dx1_ref, hid_ref, da_ref, db_ref, dg_ref):
        dx2 = dx2_ref[...]
        dyb = dx2.astype(BF16)
        dh = jnp.zeros((tm, D), F32)
        for k in range(CHIPS):
            av, bv = a_ref[k], b_ref[k]
            dhid = _dot_nt(dyb, wd_ref[k])
            sig = jax.nn.sigmoid(av)
            sa = av * sig
            hid_ref[k] = (sa * bv).astype(BF16)
            dbv = (dhid * sa).astype(BF16)
            dav = (dhid * bv * (sig * (1.0 + av * (1.0 - sig)))).astype(BF16)
            db_ref[k] = dbv
            da_ref[k] = dav
            dh = dh + _dot(dav, wg_ref[k]) + _dot(dbv, wu_ref[k])
        xn, r = _rms(x_ref[...], D)
        dxr, dg = _rms_bwd(xn, r, g_ref[...], dh, D)
        dx1_ref[...] = dx2 + dxr
        _accumulate(dg_ref, dg, pl.program_id(0) == 0)

    row = lambda w: pl.BlockSpec((tm, w), lambda i: (i, 0))
    hrow = pl.BlockSpec((CHIPS, tm, SH), lambda i: (0, i, 0))
    hid = jax.ShapeDtypeStruct((CHIPS, s, SH), BF16)
    return pl.pallas_call(
        body, name=name, grid=(s // tm,),
        in_specs=[row(D), row(D), hrow, hrow, _acc((1, D)), _res((CHIPS, SH, D)), _res((CHIPS, SH, D)),
                  _res((CHIPS, SH, D))],
        out_specs=[row(D), hrow, hrow, hrow, _acc((1, D))],
        out_shape=[jax.ShapeDtypeStruct((s, D), F32), hid, hid, hid, jax.ShapeDtypeStruct((1, D), F32)],
        compiler_params=_cp(("arbitrary",), VMEM_LIMIT),
    )(dx2, x1, a, b, g, wg, wu, wd)


def _mix_out_bwd(dx1, o, z, m, wsp, bsp, wbd, psc, gsv, gout, wout, name):
    s = dx1.shape[0]
    tm = _tile(s, 512)

    def body(dx1_ref, o_ref, uv_ref, m_ref, wsp_ref, bsp_ref, wbd_ref, psc_ref, gsv_ref, gout_ref, wout_ref,
             do_ref, dl_ref, duv_ref, dm_ref, dgo_ref, dgsv_ref, dpsc_ref, dwsp_ref, dbsp_ref, dwbd_ref):
        first = pl.program_id(0) == 0
        g = gout_ref[...]
        dmix = _dot_nt(dx1_ref[...].astype(BF16), wout_ref[...])
        o = o_ref[...]
        on, ro = _rms(o, HEADS * VH)
        do, dga = _rms_bwd(on, ro, g[:, :512], dmix[:, :512], HEADS * VH)
        for h in range(HEADS):
            sl = slice(h * VH, (h + 1) * VH)
            do_ref[h] = do[:, sl].astype(BF16)
            dl_ref[h] = jnp.broadcast_to(jnp.sum(do[:, sl] * o[:, sl], axis=-1, keepdims=True), (tm, LANES))
        uv = uv_ref[...]
        u, v = uv[:, :SGU], uv[:, SGU:]
        vx, rv = _rms(v, SGU)
        vn = (vx * gsv_ref[...]).astype(BF16)
        tri = _tril()
        wsp_m = [jnp.where(tri, wsp_ref[h], 0.0).astype(BF16) for h in range(HEADS)]
        zc = _sgu_gate(vn, wsp_m, bsp_ref[...])
        gm = u * zc
        gmn, rg = _rms(gm, SGU)
        dgm, dgg = _rms_bwd(gmn, rg, g[:, 512:768], dmix[:, 512:768], SGU)
        du = dgm * zc
        dzc = dgm * u
        dvn_parts = []
        dbsp = jnp.zeros((CHUNK, SGU), F32)
        dwsp = [jnp.zeros((CHUNK, CHUNK), F32) for _ in range(HEADS)]
        for cidx in range(tm // CHUNK):
            rs = slice(cidx * CHUNK, (cidx + 1) * CHUNK)
            dzc_c = dzc[rs]
            dbsp = dbsp + dzc_c
            dzb = dzc_c.astype(BF16)
            vc = vn[rs]
            dvn_c = jnp.zeros((CHUNK, SGU), F32)
            for h in range(HEADS):
                hm = _head_mask(h)
                dvn_c = dvn_c + jnp.where(hm, _dot_tn(wsp_m[h], dzb), 0.0)
                dwsp[h] = dwsp[h] + _dot_nt(jnp.where(hm, dzc_c, 0.0).astype(BF16), vc)
            dvn_parts.append(dvn_c)
        dvn = jnp.concatenate(dvn_parts, axis=0)
        dv, dgsv = _rms_bwd(vx, rv, gsv_ref[...], dvn, SGU)
        duv_ref[...] = jnp.concatenate([du, dv], axis=1).astype(BF16)
        mb = m_ref[...].astype(BF16)
        pw = _dot(mb, wbd_ref[...])
        po = pw * psc_ref[...]
        pon, rp = _rms(po, POOL)
        dpo, dgp = _rms_bwd(pon, rp, g[:, 768:], dmix[:, 768:], POOL)
        dpw = (dpo * psc_ref[...]).astype(BF16)
        dm_ref[...] = _dot_nt(dpw, wbd_ref[...])
        _accumulate(dgo_ref, jnp.concatenate([dga, dgg, dgp], axis=1), first)
        _accumulate(dgsv_ref, dgsv, first)
        _accumulate(dpsc_ref, jnp.sum(dpo * pw, axis=0, keepdims=True), first)
        _accumulate(dbsp_ref, dbsp, first)
        _accumulate(dwbd_ref, _dot_tn(mb, dpw), first)
        for h in range(HEADS):
            val = jnp.where(tri, dwsp[h], 0.0)

            @pl.when(first)
            def _(val=val, h=h):
                dwsp_ref[h] = val

            @pl.when(jnp.logical_not(first))
            def _(val=val, h=h):
                dwsp_ref[h] += val

    row = lambda w, j: pl.BlockSpec((tm, w), lambda i: (i, j))
    hspec = pl.BlockSpec((HEADS, tm, HP), lambda i: (0, i, 0))
    return pl.pallas_call(
        body, name=name, grid=(s // tm,),
        in_specs=[row(D, 0), row(512, 0), row(512, 1), row(POOL, 0),
                  _acc((HEADS, CHUNK, CHUNK)), _acc((CHUNK, SGU)),
                  _acc((POOL, POOL)), _acc((1, POOL)), _acc((1, SGU)), _acc((1, D)), _res((D, D))],
        out_specs=[hspec, hspec, row(512, 0), row(POOL, 0), _acc((1, D)), _acc((1, SGU)), _acc((1, POOL)),
                   _acc((HEADS, CHUNK, CHUNK)), _acc((CHUNK, SGU)), _acc((POOL, POOL))],
        out_shape=[jax.ShapeDtypeStruct((HEADS, s, HP), BF16), jax.ShapeDtypeStruct((HEADS, s, LANES), F32),
                   jax.ShapeDtypeStruct((s, 512), BF16), jax.ShapeDtypeStruct((s, POOL), F32),
                   jax.ShapeDtypeStruct((1, D), F32), jax.ShapeDtypeStruct((1, SGU), F32),
                   jax.ShapeDtypeStruct((1, POOL), F32), jax.ShapeDtypeStruct((HEADS, CHUNK, CHUNK), F32),
                   jax.ShapeDtypeStruct((CHUNK, SGU), F32), jax.ShapeDtypeStruct((POOL, POOL), F32)],
        compiler_params=_cp(("arbitrary",), VMEM_LIMIT),
    )(dx1, o, z, m, wsp, bsp, wbd, psc, gsv, gout, wout)


def _attn_bwd(q, k, v, do, lse, delta, name):
    s = q.shape[1]
    tq = tk = _tile(s, 512)
    nq = s // tq
    wide = ATT_WIDE * tq if s % (ATT_WIDE * tq) == 0 else tq

    def body(q_ref, k_ref, v_ref, do_ref, lse_ref, dl_ref, dq_ref, dk_ref, dv_ref):
        j = pl.program_id(1)

        @pl.when(j == 0)
        def _():
            dq_ref[...] = jnp.zeros_like(dq_ref)

        kj, vj = k_ref[0], v_ref[0]
        rh = tq // ATT_SPLIT

        def blk(start, rows, dk, dv, masked):
            offs = [pl.multiple_of(start + g * rh, rh) for g in range(rows // rh)]
            qs = [q_ref[0, pl.ds(off, rh), :] for off in offs]
            dos = [do_ref[0, pl.ds(off, rh), :] for off in offs]
            scs = [_dot_nt(qi, kj) for qi in qs]
            dps = [_dot_nt(doi, vj) for doi in dos]
            for g, off in enumerate(offs):
                lse_i = lse_ref[0, pl.ds(off, rh), :][:, :1]
                dl_i = dl_ref[0, pl.ds(off, rh), :][:, :1]
                sc = _causal_mask(scs[g], g * rh) if masked else scs[g]
                p = jnp.exp2(sc * EXP2_C - lse_i)
                ds = (p * (dps[g] - dl_i)).astype(BF16)
                dv = dv + _dot_tn(p.astype(BF16), dos[g])
                dk = dk + _dot_tn(ds, qs[g])
                dq_ref[0, pl.ds(off, rh), :] += _dot(ds, kj) * SCALE
            return dk, dv

        per = wide // tq
        zero = jnp.zeros((tk, HP), F32)
        dk, dv = blk(j * tq, tq, zero, zero, True)
        first_wide = (j + per) // per
        dk, dv = lax.fori_loop(j + 1, jnp.minimum(first_wide * per, nq), lambda i, c: blk(i * tq, tq, *c, False), (dk, dv))
        dk, dv = lax.fori_loop(first_wide, nq // per, lambda i, c: blk(i * wide, wide, *c, False), (dk, dv))
        dk_ref[0] = dk * SCALE
        dv_ref[0] = dv

    full = lambda: pl.BlockSpec((1, s, HP), lambda h, j: (h, 0, 0))
    blk_spec = lambda: pl.BlockSpec((1, tk, HP), lambda h, j: (h, j, 0))
    out = jax.ShapeDtypeStruct((HEADS, s, HP), F32)
    return pl.pallas_call(
        body, name=name, grid=(HEADS, s // tk),
        in_specs=[full(), blk_spec(), blk_spec(), full(), full(), full()],
        out_specs=[full(), blk_spec(), blk_spec()], out_shape=[out] * 3,
        compiler_params=_cp(("parallel", "arbitrary"), VMEM_LIMIT),
    )(q, k, v, do, lse, delta)


def _mla_prep_bwd(dq, dk, dv, z, tabs, gql, gkv, gq, gk, wq, wk, wv, name):
    s = z.shape[0]
    tm = _tile(s, 512)

    def body(dq_ref, dk_ref, dv_ref, ql_ref, kv_ref, kr_ref, c_ref, sa_ref, sb_ref, gql_ref, gkv_ref, gq_ref, gk_ref,
             wq_ref, wk_ref, wv_ref,
             dz_ref, qn_ref, kvn_ref, dqr_ref, dkr_ref, dvr_ref, dgql_ref, dgkv_ref, dgq_ref, dgk_ref):
        first = pl.program_id(0) == 0
        qx, rq = _rms(ql_ref[...], QL)
        qn = (qx * gql_ref[...]).astype(BF16)
        kx, rk = _rms(kv_ref[...], KVL)
        kvn = (kx * gkv_ref[...]).astype(BF16)
        qn_ref[...] = qn
        kvn_ref[...] = kvn
        qraw = _dot(qn, wq_ref[...])
        kraw = _dot(kvn, wk_ref[...])
        kr = kr_ref[...]
        c, sa, sb = c_ref[...], sa_ref[...], sb_ref[...]
        lane = lax.broadcasted_iota(jnp.int32, (tm, HP), 1)
        rope_lanes = (lane >= NOPE) & (lane < QK)
        dkrope = jnp.zeros((tm, HP), F32)
        dgq = jnp.zeros((1, HP), F32)
        dgk = jnp.zeros((1, HP), F32)
        for h in range(HEADS):
            sl = slice(h * HP, (h + 1) * HP)
            xn, r = _rms(qraw[:, sl], QK)
            dx, dg = _rms_bwd(xn, r, gq_ref[...], _rope_t(dq_ref[h], c, sa, sb), QK)
            dqr_ref[:, sl] = dx.astype(BF16)
            dgq = dgq + dg
            xn, r = _rms(kraw[:, sl] + kr, QK)
            dx, dg = _rms_bwd(xn, r, gk_ref[...], _rope_t(dk_ref[h], c, sa, sb), QK)
            dkr_ref[:, sl] = dx.astype(BF16)
            dgk = dgk + dg
            dkrope = dkrope + jnp.where(rope_lanes, dx, 0.0)
            dvr_ref[:, sl] = dv_ref[h].astype(BF16)
        dqn = _dot_nt(dqr_ref[...], wq_ref[...])
        dql, dgql = _rms_bwd(qx, rq, gql_ref[...], dqn, QL)
        dkvn = _dot_nt(dkr_ref[...], wk_ref[...]) + _dot_nt(dvr_ref[...], wv_ref[...])
        dkv, dgkv = _rms_bwd(kx, rk, gkv_ref[...], dkvn, KVL)
        dz_ref[...] = jnp.concatenate([dql, dkv, dkrope], axis=1).astype(BF16)
        _accumulate(dgql_ref, dgql, first)
        _accumulate(dgkv_ref, dgkv, first)
        _accumulate(dgq_ref, dgq, first)
        _accumulate(dgk_ref, dgk, first)

    row = lambda w, j: pl.BlockSpec((tm, w), lambda i: (i, j))
    hspec = pl.BlockSpec((HEADS, tm, HP), lambda i: (0, i, 0))
    sd = lambda w, dt: jax.ShapeDtypeStruct((s, w), dt)
    return pl.pallas_call(
        body, name=name, grid=(s // tm,),
        in_specs=[hspec, hspec, hspec, row(QL, 0), row(KVL, 2), row(HP, 3), row(HP, 0), row(HP, 0), row(HP, 0),
                  _acc((1, QL)), _acc((1, KVL)), _acc((1, HP)), _acc((1, HP)),
                  _acc((QL, HEADS * HP)), _acc((KVL, HEADS * HP)), _acc((KVL, HEADS * HP))],
        out_specs=[row(512, 0), row(QL, 0), row(KVL, 0), row(512, 0), row(512, 0), row(512, 0),
                   _acc((1, QL)), _acc((1, KVL)), _acc((1, HP)), _acc((1, HP))],
        out_shape=[sd(512, BF16), sd(QL, BF16), sd(KVL, BF16), sd(512, BF16), sd(512, BF16), sd(512, BF16),
                   jax.ShapeDtypeStruct((1, QL), F32), jax.ShapeDtypeStruct((1, KVL), F32),
                   jax.ShapeDtypeStruct((1, HP), F32), jax.ShapeDtypeStruct((1, HP), F32)],
        compiler_params=_cp(("arbitrary",), VMEM_LIMIT),
    )(dq, dk, dv, z, z, z, *tabs, gql, gkv, gq, gk, wq, wk, wv)


def _in_proj_bwd(dzm, duv, dp, x, dx1, g, win, name):
    s = x.shape[0]
    tm = _tile(s, 512)

    def body(dzm_ref, duv_ref, dp_ref, x_ref, dx1_ref, g_ref, w_ref, dx_ref, dg_ref):
        dh = _dot_nt(dzm_ref[...], w_ref[:, 0:512]) + _dot_nt(duv_ref[...], w_ref[:, 512:1024]) \
            + _dot_nt(dp_ref[...], w_ref[:, 1024:IN_P])
        xn, r = _rms(x_ref[...], D)
        dxr, dg = _rms_bwd(xn, r, g_ref[...], dh, D)
        dx_ref[...] = dx1_ref[...] + dxr
        _accumulate(dg_ref, dg, pl.program_id(0) == 0)

    row = lambda w: pl.BlockSpec((tm, w), lambda i: (i, 0))
    return pl.pallas_call(
        body, name=name, grid=(s // tm,),
        in_specs=[row(512), row(512), row(POOL), row(D), row(D), _acc((1, D)), _res((D, IN_P))],
        out_specs=[row(D), _acc((1, D))],
        out_shape=[jax.ShapeDtypeStruct((s, D), F32), jax.ShapeDtypeStruct((1, D), F32)],
        compiler_params=_cp(("arbitrary",), VMEM_LIMIT),
    )(dzm, duv, dp, x, dx1, g, win)


def _adamw(w, g0, g1, m, v, name):
    _, r, c = w.shape
    tr = _row_tile(r, 512)
    c1 = 1.0 - B1 ** STEP
    c2 = 1.0 - B2 ** STEP

    def body(w_ref, g0_ref, g1_ref, m_ref, v_ref, g_ref, d_ref, nm_ref, nv_ref):
        gv = jnp.where(pl.program_id(0) == 0, g0_ref[...], g1_ref[...])
        g_ref[0] = gv
        nm = B1 * m_ref[0] + (1.0 - B1) * gv
        nv = B2 * v_ref[0] + (1.0 - B2) * (gv * gv)
        nm_ref[0] = nm
        nv_ref[0] = nv
        d_ref[0] = -LR * ((nm / c1) / (jnp.sqrt(nv / c2) + ADAM_EPS) + WD * w_ref[0])

    spec = pl.BlockSpec((1, tr, c), lambda l, i: (l, i, 0))
    out = jax.ShapeDtypeStruct((DEPTH, r, c), F32)
    return pl.pallas_call(
        body, name=name, grid=(DEPTH, r // tr),
        in_specs=[spec, pl.BlockSpec((tr, c), lambda l, i: (i * (1 - l), 0)), pl.BlockSpec((tr, c), lambda l, i: (i * l, 0)),
                  spec, spec],
        out_specs=[spec] * 4, out_shape=[out] * 4, compiler_params=_cp(("parallel", "parallel")),
    )(w, g0, g1, m, v)


ANY = pl.BlockSpec(memory_space=pl.ANY)


def _place():
    x, y, c = lax.axis_index("x"), lax.axis_index("y"), lax.axis_index("c")
    chips = [(1 - x, y), (x, 1 - y), (1 - x, 1 - y)]
    return x, y, c, chips


def _half_rows(ref, lead, hh, half, align):
    rows = pl.ds(pl.multiple_of(hh * half, align), half)
    return ref.at[rows, :] if lead is None else ref.at[lead, rows, :]


def _row_align(dtype):
    return 16 if dtype == BF16 else 8


def _sems(n):
    return [pltpu.SemaphoreType.DMA((n,)), pltpu.SemaphoreType.DMA((n,)), pltpu.SemaphoreType.DMA((n,))]


def _comm_call(body, ins, out_shapes, nsems, name):
    return pl.pallas_call(
        body, name=name, in_specs=[ANY] * len(ins), out_specs=[ANY] * len(out_shapes), out_shape=out_shapes,
        scratch_shapes=_sems(nsems), compiler_params=pltpu.CompilerParams(has_side_effects=True),
    )(*ins)


def _all_gather_chips(shards, name):
    n = len(shards)
    halves = [a.shape[0] // 2 for a in shards]
    aligns = [_row_align(a.dtype) for a in shards]
    assert all(h % al == 0 for h, al in zip(halves, aligns))

    def body(*refs):
        ins, outs, (send_sems, recv_sems, _) = refs[:n], refs[n:2 * n], refs[2 * n:]
        x, y, c, chips = _place()
        me = 2 * x + y
        sibling = (x, y, 1 - c)

        def copy(sem, src, dst, to):
            return pltpu.make_async_remote_copy(src_ref=src, dst_ref=dst, send_sem=send_sems.at[sem],
                                                recv_sem=recv_sems.at[sem], device_id=to, device_id_type=MESH)

        first, passed = [], []
        for a in range(n):
            my_half = _half_rows(ins[a], None, c, halves[a], aligns[a])
            for j, (cx, cy) in enumerate(chips):
                cp = copy(6 * a + j, my_half, _half_rows(outs[a], me, c, halves[a], aligns[a]), (cx, cy, c))
                cp.start()
                first.append(cp)
        for a in range(n):
            for j, (cx, cy) in enumerate(chips):
                landed = _half_rows(outs[a], 2 * cx + cy, c, halves[a], aligns[a])
                copy(6 * a + j, landed, landed, (cx, cy, c)).wait_recv()
                fwd = copy(6 * a + 3 + j, landed, landed, sibling)
                fwd.start()
                passed.append(fwd)
        for a in range(n):
            for j, (cx, cy) in enumerate(chips):
                other = _half_rows(outs[a], 2 * cx + cy, 1 - c, halves[a], aligns[a])
                copy(6 * a + 3 + j, other, other, sibling).wait_recv()
        for cp in first + passed:
            cp.wait_send()

    lands = _comm_call(body, shards, [jax.ShapeDtypeStruct((CHIPS,) + a.shape, a.dtype) for a in shards], 6 * n, name)
    return _with_own(lands, shards)


def _with_own(lands, shards):
    me = 2 * lax.axis_index("x") + lax.axis_index("y")
    return [lax.dynamic_update_slice(g, a[None], (me, 0, 0)) for g, a in zip(lands, shards)]


def _pair_join(arrs, name):
    n = len(arrs)
    halves = [a.shape[0] // 2 for a in arrs]

    def body(*refs):
        outs, (send_sems, recv_sems, _) = refs[n:2 * n], refs[2 * n:]
        x, y, c, _ = _place()
        cps = []
        for a in range(n):
            mine = _half_rows(outs[a], None, c, halves[a], 8)
            cp = pltpu.make_async_remote_copy(src_ref=mine, dst_ref=mine, send_sem=send_sems.at[a], recv_sem=recv_sems.at[a],
                                              device_id=(x, y, 1 - c), device_id_type=MESH)
            cp.start()
            cps.append(cp)
        for cp in cps:
            cp.wait()

    return pl.pallas_call(
        body, name=name, in_specs=[ANY] * n, out_specs=[ANY] * n,
        out_shape=[jax.ShapeDtypeStruct(a.shape, a.dtype) for a in arrs],
        input_output_aliases={i: i for i in range(n)}, scratch_shapes=_sems(n),
        compiler_params=pltpu.CompilerParams(has_side_effects=True),
    )(*arrs)


HBM = pl.BlockSpec(memory_space=pltpu.HBM)
SEM = pl.BlockSpec(memory_space=pltpu.SEMAPHORE)
DATAFLOW = pltpu.SideEffectType.DATAFLOW_SIDE_EFFECTING


def _remote_copies(pairs, ins, lands, send_sems, recv_sems):
    return [pltpu.make_async_remote_copy(src_ref=src, dst_ref=dst, send_sem=send_sems.at[i], recv_sem=recv_sems.at[i],
                                         device_id=to, device_id_type=MESH)
            for i, (src, dst, to) in enumerate(pairs(ins, lands))]


def _split_start(srcs, land_shapes, ncopies, pairs, name, after):
    n, m = len(srcs), len(land_shapes)

    def body(*refs):
        ins, lands = refs[:n], refs[n:n + m]
        send_sems, recv_sems, token = refs[n + m + 1], refs[n + m + 2], refs[-1]
        for cp in _remote_copies(pairs, ins, lands, send_sems, recv_sems):
            cp.start()
        token[...] = jnp.zeros_like(token)

    hbm = lambda a: pltpu.with_memory_space_constraint(a, pltpu.HBM)
    lands = [hbm(lax.empty(s.shape, s.dtype)) for s in land_shapes]
    thru = [pltpu.HBM(a.shape, a.dtype) for a in list(srcs) + lands]
    out = pl.pallas_call(
        body, name=name,
        out_shape=(pltpu.SemaphoreType.DMA((ncopies,)), pltpu.SemaphoreType.DMA((ncopies,)), *thru,
                   jax.ShapeDtypeStruct((8, LANES), F32)),
        in_specs=[HBM] * (n + m) + [ANY], out_specs=(SEM, SEM, *[HBM] * (n + m), pl.BlockSpec(memory_space=pltpu.VMEM)),
        input_output_aliases={i: 2 + i for i in range(n + m)},
        compiler_params=pltpu.CompilerParams(has_side_effects=DATAFLOW),
    )(*[hbm(a) for a in srcs], *lands, after)
    return out[0], out[1], list(out[2:2 + n]), list(out[2 + n:2 + n + m]), out[-1]


def _split_wait(send_sems, recv_sems, srcs, lands, after, pairs, name):
    n, m = len(srcs), len(lands)

    def body(*refs):
        ins, lands_ = refs[:n], refs[n:n + m]
        for cp in _remote_copies(pairs, ins, lands_, refs[n + m], refs[n + m + 1]):
            cp.wait_send()
            cp.wait_recv()

    out = pl.pallas_call(
        body, name=name, out_shape=tuple(pltpu.HBM(a.shape, a.dtype) for a in list(srcs) + list(lands)),
        in_specs=[HBM] * (n + m) + [SEM, SEM, ANY], out_specs=tuple([HBM] * (n + m)),
        input_output_aliases={i: i for i in range(n + m)},
        compiler_params=pltpu.CompilerParams(has_side_effects=DATAFLOW),
    )(*srcs, *lands, send_sems, recv_sems, after)
    return list(out[:n]), list(out[n:])


def _gather_pairs(halves, aligns):
    def pairs(ins, lands):
        x, y, c, chips = _place()
        me = 2 * x + y
        return [(_half_rows(ins[a], None, c, halves[a], aligns[a]), _half_rows(lands[a], me, c, halves[a], aligns[a]),
                 (cx, cy, c)) for a in range(len(ins)) for cx, cy in chips]
    return pairs


PEERS = 7


def _scatter_pairs(ins, lands):
    x, y, c, chips = _place()
    to = [(cx, cy, c) for cx, cy in chips] + [(cx, cy, 1 - c) for cx, cy in chips] + [(x, y, 1 - c)]
    out = []
    for a in range(len(ins)):
        half = ins[a].shape[1] // 2
        for i, (tx, ty, tc) in enumerate(to):
            out.append((_half_rows(ins[a], 2 * tx + ty, tc, half, 8), lands[a].at[i], (tx, ty, tc)))
    return out


def _gather_finish(shards, lands, name):
    n = len(shards)
    halves = [a.shape[0] // 2 for a in shards]
    aligns = [_row_align(a.dtype) for a in shards]

    def body(*refs):
        outs, (send_sems, recv_sems, _) = refs[n:2 * n], refs[2 * n:]
        x, y, c, chips = _place()
        passed = []
        for a in range(n):
            for j, (cx, cy) in enumerate(chips):
                landed = _half_rows(outs[a], 2 * cx + cy, c, halves[a], aligns[a])
                cp = pltpu.make_async_remote_copy(src_ref=landed, dst_ref=landed, send_sem=send_sems.at[3 * a + j],
                                                  recv_sem=recv_sems.at[3 * a + j], device_id=(x, y, 1 - c),
                                                  device_id_type=MESH)
                cp.start()
                passed.append(cp)
        for a in range(n):
            for j, (cx, cy) in enumerate(chips):
                other = _half_rows(outs[a], 2 * cx + cy, 1 - c, halves[a], aligns[a])
                pltpu.make_async_remote_copy(src_ref=other, dst_ref=other, send_sem=send_sems.at[3 * a + j],
                                             recv_sem=recv_sems.at[3 * a + j], device_id=(x, y, 1 - c),
                                             device_id_type=MESH).wait_recv()
        for cp in passed:
            cp.wait_send()

    lands = pl.pallas_call(
        body, name=name, in_specs=[ANY] * n, out_specs=[ANY] * n,
        out_shape=[jax.ShapeDtypeStruct(a.shape, a.dtype) for a in lands],
        input_output_aliases={i: i for i in range(n)}, scratch_shapes=_sems(3 * n),
        compiler_params=pltpu.CompilerParams(has_side_effects=True),
    )(*lands)
    return _with_own(lands, shards)


def _sum_own_and_landed(own, landed, where, name):
    _, half, cols = landed.shape
    tr = _row_tile(half, 128)
    nt = half // tr

    grid_spec = pltpu.PrefetchScalarGridSpec(
        num_scalar_prefetch=1, grid=(nt,),
        in_specs=[pl.BlockSpec((1, tr, cols), lambda r, w: (w[0], w[1] * nt + r, 0)),
                  pl.BlockSpec((PEERS, tr, cols), lambda r, w: (0, r, 0))],
        out_specs=pl.BlockSpec((tr, cols), lambda r, w: (w[1] * nt + r, 0)))

    def body(w_ref, p_ref, q_ref, o_ref):
        acc = p_ref[0]
        for i in range(PEERS):
            acc = acc + q_ref[i]
        o_ref[...] = acc

    return pl.pallas_call(
        body, name=name, grid_spec=grid_spec, out_shape=jax.ShapeDtypeStruct((2 * half, cols), own.dtype),
        compiler_params=_cp(("parallel",)),
    )(where, own, landed)


BIG = [("w_in", (D, IN_W), 1), ("w_q_up", (QL, HEADS * QK), 1), ("w_kv_up", (KVL, HEADS * (NOPE + VH)), 1),
       ("w_out", (D, D), 0), ("w_gate", (D, HID), 1), ("w_up", (D, HID), 1), ("w_down", (HID, D), 0)]
SMALL = [("g_mix_norm", (D,)), ("g_q_lat", (QL,)), ("g_kv_lat", (KVL,)), ("g_q_head", (QK,)), ("g_k_head", (QK,)),
         ("g_sgu_v", (SGU,)), ("w_spatial", (HEADS, CHUNK, CHUNK)), ("b_spatial", (HEADS, CHUNK)),
         ("w_pool", (4, 64, 64)), ("pool_scale", (POOL,)), ("g_out_mla", (512,)), ("g_out_sgu", (SGU,)),
         ("g_out_pool", (POOL,)), ("g_ffn_norm", (D,))]
ORDER = ["g_mix_norm", "w_in", "g_q_lat", "w_q_up", "g_kv_lat", "w_kv_up", "g_q_head", "g_k_head", "g_sgu_v",
         "w_spatial", "b_spatial", "w_pool", "pool_scale", "g_out_mla", "g_out_sgu", "g_out_pool", "w_out",
         "g_ffn_norm", "w_gate", "w_up", "w_down"]
EARLY_BIG = ["w_in", "w_q_up", "w_kv_up"]
FFN_BIG = ["w_gate", "w_up", "w_down"]
LATE_BIG = ["w_out"] + FFN_BIG
DEPTH = 2
COLS = 1024
SMALL_N = sum(math.prod(s) for _, s in SMALL) * DEPTH
assert SMALL_N % CHIPS == 0
SMALL_ROWS = -(-(SMALL_N // CHIPS) // (16 * COLS)) * 16


def _unsplit_cols(g):
    return g.transpose(1, 0, 2).reshape(g.shape[1], CHIPS * g.shape[2])


def _split_cols(full):
    r, c = full.shape
    return full.reshape(r, CHIPS, c // CHIPS).transpose(1, 0, 2)


def _kernel_weights(g):
    win = _unsplit_cols(g["w_in"])
    zeros = lambda r, c: jnp.zeros((r, c), BF16)
    o2, o3, o4 = QL + KVL, QL + KVL + ROPE, QL + KVL + ROPE + 2 * SGU
    win_p = jnp.concatenate([win[:, :o2], zeros(D, NOPE), win[:, o2:o3], zeros(D, HP - QK), win[:, o3:o4], win[:, o4:]], axis=1)
    wq = _unsplit_cols(g["w_q_up"]).reshape(QL, HEADS, QK)
    wq_p = jnp.pad(wq, ((0, 0), (0, 0), (0, HP - QK))).reshape(QL, HEADS * HP)
    wkv = _unsplit_cols(g["w_kv_up"]).reshape(KVL, HEADS, NOPE + VH)
    wk_p = jnp.pad(wkv[:, :, :NOPE], ((0, 0), (0, 0), (0, HP - NOPE))).reshape(KVL, HEADS * HP)
    wv_p = wkv[:, :, NOPE:].reshape(KVL, HEADS * VH)
    return dict(win=win_p, wq=wq_p, wk=wk_p, wv=wv_p)


def _small_operands(p, l):
    row = lambda v: v.reshape(1, -1)
    pad = lambda v: jnp.pad(v, (0, HP - QK)).reshape(1, HP)
    wpool = p["w_pool"][l]
    wbd = jnp.zeros((POOL, POOL), F32)
    for g in range(4):
        wbd = lax.dynamic_update_slice(wbd, wpool[g], (g * 64, g * 64))
    return dict(
        g_mix=row(p["g_mix_norm"][l]), gql=row(p["g_q_lat"][l]), gkv=row(p["g_kv_lat"][l]),
        gq=pad(p["g_q_head"][l]), gk=pad(p["g_k_head"][l]), gsv=row(p["g_sgu_v"][l]),
        wsp=p["w_spatial"][l], bsp=jnp.repeat(p["b_spatial"][l].T, SGU // HEADS, axis=1),
        wbd=wbd.astype(BF16), psc=row(p["pool_scale"][l]),
        gout=jnp.concatenate([p["g_out_mla"][l], p["g_out_sgu"][l], p["g_out_pool"][l]]).reshape(1, D),
        g_ffn=row(p["g_ffn_norm"][l]))


def _big_grads(g):
    dwin = g["win"]
    o2 = QL + KVL
    gin = jnp.concatenate([dwin[:, :o2], dwin[:, o2 + NOPE:o2 + NOPE + ROPE], dwin[:, 512:]], axis=1)
    gq = g["wq"].reshape(QL, HEADS, HP)[:, :, :QK].reshape(QL, HEADS * QK)
    gk = g["wk"].reshape(KVL, HEADS, HP)[:, :, :NOPE]
    gv = g["wv"].reshape(KVL, HEADS, VH)
    gkv = jnp.concatenate([gk, gv], axis=2).reshape(KVL, HEADS * (NOPE + VH))
    return {"w_in": _split_cols(gin), "w_q_up": _split_cols(gq), "w_kv_up": _split_cols(gkv),
            "w_out": g["wout"].reshape(CHIPS, D // CHIPS, D), "w_gate": g["wg"], "w_up": g["wu"], "w_down": g["wd"]}


TRANSPOSED = ("w_gate", "w_up")


def _small_grads(g):
    go = g["gout"].reshape(-1)
    return {"g_mix_norm": g["g_mix"].reshape(-1), "g_q_lat": g["gql"].reshape(-1), "g_kv_lat": g["gkv"].reshape(-1),
            "g_q_head": g["gq"].reshape(-1)[:QK], "g_k_head": g["gk"].reshape(-1)[:QK], "g_sgu_v": g["gsv"].reshape(-1),
            "w_spatial": g["wsp"], "b_spatial": g["bsp"].reshape(CHUNK, HEADS, SGU // HEADS).sum(-1).T,
            "w_pool": jnp.stack([g["wbd"][i * 64:(i + 1) * 64, i * 64:(i + 1) * 64] for i in range(4)]),
            "pool_scale": g["psc"].reshape(-1), "g_out_mla": go[:512], "g_out_sgu": go[512:768],
            "g_out_pool": go[768:], "g_ffn_norm": g["g_ffn"].reshape(-1)}


def _pack_small_grads(small):
    sm = jnp.concatenate([small[l][n].reshape(-1) for l in range(DEPTH) for n, _ in SMALL]).reshape(CHIPS, SMALL_N // CHIPS)
    return jnp.pad(sm, ((0, 0), (0, SMALL_ROWS * COLS - SMALL_N // CHIPS))).reshape(CHIPS, SMALL_ROWS, COLS)


def _unpack_small_grads(gathered):
    flat = gathered.reshape(CHIPS, SMALL_ROWS * COLS)[:, :SMALL_N // CHIPS].reshape(-1)
    out, off = [], 0
    for _ in range(DEPTH):
        layer = {}
        for n, shape in SMALL:
            k = math.prod(shape)
            layer[n] = flat[off:off + k].reshape(shape)
            off += k
        out.append(layer)
    return out


def _layer_fwd(x, tabs, kw, late_weights, sp, l):
    t = f"_l{l}"
    z, hb = _in_proj_fwd(x, sp["g_mix"], kw["win"], "in_proj_fwd" + t)
    q, k, v = _mla_prep_fwd(z, tabs, sp["gql"], sp["gkv"], sp["gq"], sp["gk"], kw["wq"], kw["wk"], kw["wv"],
                            "mla_prep_fwd" + t)
    o, lse = _attn_fwd(q, k, v, "attn_fwd" + t)
    m = _pool_win_fwd(z, "pool_win_fwd" + t)
    wout, wg, wu, wd = late_weights(o)
    wout = wout.reshape(D, D)
    x1, mix = _mix_out_fwd(o, z, m, x, sp["wsp"], sp["bsp"], sp["wbd"], sp["psc"], sp["gsv"], sp["gout"], wout,
                           "mix_out_fwd" + t)
    x2, a, b, h2 = _ffn_fwd(x1, sp["g_ffn"], wg, wu, wd, "ffn_fwd" + t)
    saved = dict(x=x, z=z, hb=hb, q=q, k=k, v=v, o=o, lse=lse, m=m, x1=x1, mix=mix, a=a, b=b, h2=h2, wg=wg, wu=wu, wd=wd,
                 wout=wout)
    return x2, saved


def _layer_bwd(dx2, sv, tabs, kw, sp, l, ffn_hook):
    t = f"_l{l}"
    g = {}
    dx1, hid, da, db, g["g_ffn"] = _ffn_bwd(dx2, sv["x1"], sv["a"], sv["b"], sp["g_ffn"], sv["wg"], sv["wu"], sv["wd"],
                                            "ffn_bwd" + t)
    g["wd"] = _wgrad_rows(hid, dx2, "wgrad_down" + t)
    g["wg"] = _wgrad_rows(da, sv["h2"], "wgrad_gate" + t)
    g["wu"] = _wgrad_rows(db, sv["h2"], "wgrad_up" + t)
    gout = sp["gout"] + ffn_hook(g)
    do, delta, duv, dm, g["gout"], g["gsv"], g["psc"], g["wsp"], g["bsp"], g["wbd"] = _mix_out_bwd(
        dx1, sv["o"], sv["z"], sv["m"], sp["wsp"], sp["bsp"], sp["wbd"], sp["psc"], sp["gsv"], gout, sv["wout"],
        "mix_out_bwd" + t)
    g["wout"] = _wgrad(sv["mix"], dx1, "wgrad_out" + t)
    dp = _pool_win_bwd(dm, "pool_win_bwd" + t)
    dq, dk, dv = _attn_bwd(sv["q"], sv["k"], sv["v"], do, sv["lse"], delta, "attn_bwd" + t)
    dzm, qn, kvn, dqr, dkr, dvr, g["gql"], g["gkv"], g["gq"], g["gk"] = _mla_prep_bwd(
        dq, dk, dv, sv["z"], tabs, sp["gql"], sp["gkv"], sp["gq"], sp["gk"], kw["wq"], kw["wk"], kw["wv"],
        "mla_prep_bwd" + t)
    g["wq"] = _wgrad(qn, dqr, "wgrad_q_up" + t)
    g["wk"] = _wgrad(kvn, dkr, "wgrad_k_up" + t)
    g["wv"] = _wgrad(kvn, dvr, "wgrad_v_up" + t)
    dx, g["g_mix"] = _in_proj_bwd(dzm, duv, dp, sv["x"], dx1, sp["g_mix"], kw["win"], "in_proj_bwd" + t)
    g["win"] = _wgrad_in(sv["hb"], dzm, duv, dp, "wgrad_in" + t)
    return dx, g


def _rope_inv_freq():
    half = ROPE // 2
    inv = 1.0 / (ROPE_THETA ** (jnp.arange(half, dtype=F32) / half))
    return jnp.concatenate([jnp.zeros((NOPE,), F32), inv, inv, jnp.zeros((HP - QK,), F32)]).reshape(1, HP)


def kernel(x, positions, g_mix_norm, w_in, g_q_lat, w_q_up, g_kv_lat, w_kv_up, g_q_head, g_k_head, g_sgu_v, w_spatial, b_spatial, w_pool, pool_scale, g_out_mla, g_out_sgu, g_out_pool, w_out, g_ffn_norm, w_gate, w_up, w_down, loss_target, m_g_mix_norm, m_w_in, m_g_q_lat, m_w_q_up, m_g_kv_lat, m_w_kv_up, m_g_q_head, m_g_k_head, m_g_sgu_v, m_w_spatial, m_b_spatial, m_w_pool, m_pool_scale, m_g_out_mla, m_g_out_sgu, m_g_out_pool, m_w_out, m_g_ffn_norm, m_w_gate, m_w_up, m_w_down, v_g_mix_norm, v_w_in, v_g_q_lat, v_w_q_up, v_g_kv_lat, v_w_kv_up, v_g_q_head, v_g_k_head, v_g_sgu_v, v_w_spatial, v_b_spatial, v_w_pool, v_pool_scale, v_g_out_mla, v_g_out_sgu, v_g_out_pool, v_w_out, v_g_ffn_norm, v_w_gate, v_w_up, v_w_down):
    given = dict(locals())
    p = {n: given[n] for n in ORDER}
    view = lambda pre, n: jnp.swapaxes(given[pre + n], 1, 2) if n in TRANSPOSED else given[pre + n]
    seq = x.shape[1]
    where = jnp.stack([2 * lax.axis_index("x") + lax.axis_index("y"), lax.axis_index("c")]).astype(jnp.int32)
    shards = lambda names: [view("", n)[l].astype(BF16) for l, n in names]
    zero11 = lambda token: token[:1, :1]

    names_0a = [(0, n) for n in EARLY_BIG]
    names_0b = [(0, n) for n in LATE_BIG]
    names_1 = [(1, n) for n, _, _ in BIG]
    got_0a = dict(zip(EARLY_BIG, _all_gather_chips(shards(names_0a), "all_gather_w0a")))
    started, issued = {}, got_0a["w_in"]
    for tag, names in (("w0b", names_0b), ("w1", names_1)):
        sh = shards(names)
        pairs = _gather_pairs([a.shape[0] // 2 for a in sh], [_row_align(a.dtype) for a in sh])
        lands = [jax.ShapeDtypeStruct((CHIPS,) + a.shape, a.dtype) for a in sh]
        started[tag] = (sh, pairs) + _split_start(sh, lands, 3 * len(sh), pairs, "gather_start_" + tag, issued)
        issued = started[tag][6]

    def arrived(tag, after):
        _, pairs, send, recv, srcs, lands, _ = started[tag]
        srcs, lands = _split_wait(send, recv, srcs, lands, after, pairs, "gather_wait_" + tag)
        return _gather_finish(srcs, lands, "gather_finish_" + tag)

    layer1 = {}

    def mix_weights(l, h):
        if l == 0:
            return got_0a
        layer1.update(zip([n for _, n in names_1], arrived("w1", h)))
        return layer1

    def late_weights(l, o):
        return arrived("w0b", o) if l == 0 else [layer1[n] for n in LATE_BIG]

    reducing, last = {}, {}

    def reduce_start(tag, arrs):
        lands = [jax.ShapeDtypeStruct((PEERS, a.shape[1] // 2, a.shape[2]), a.dtype) for a in arrs]
        reducing[tag] = _split_start(arrs, lands, PEERS * len(arrs), _scatter_pairs, "grad_scatter_start_" + tag, where)
        return zero11(reducing[tag][4])

    def reduce_finish(tag, after):
        send, recv, srcs, lands, _ = reducing[tag]
        srcs, lands = _split_wait(send, recv, srcs, lands, after, _scatter_pairs, "grad_scatter_wait_" + tag)
        return [_sum_own_and_landed(a, q, where, f"grad_sum_{tag}_{i}") for i, (a, q) in enumerate(zip(srcs, lands))]

    def ffn_hook(l, g):
        if l == 1:
            return jnp.zeros((1, 1), F32)
        return reduce_start("g0b", [g["wg"], g["wu"], g["wd"]])

    def layer_hook(l, big, small):
        last[l] = (big, small)
        if l == 1:
            return reduce_start("g1", [big[n] for n, _, _ in BIG])
        return None

    entry = zero11(started["w0b"][6]) + zero11(started["w1"][6])
    loss_part, dx = _step(x.reshape(seq, D), positions.reshape(seq, 1), loss_target.reshape(seq, D), p, entry,
                          mix_weights, late_weights, ffn_hook, layer_hook)
    loss = lax.psum(loss_part, ("x", "y", "c"))

    def adamw(n, g0, g1):
        w = view("", n)
        three_d = (DEPTH, -1, w.shape[-1])
        res = _adamw(w.reshape(three_d), g0.reshape(three_d[1:]), g1.reshape(three_d[1:]),
                     view("m_", n).reshape(three_d), view("v_", n).reshape(three_d), "adamw_" + n)
        return [r.reshape(w.shape) for r in res]

    names_rest = [(0, n) for n in EARLY_BIG + ["w_out"]]
    reduce_start("g0a", [last[0][0][n] for _, n in names_rest] + [_pack_small_grads([last[l][1] for l in range(DEPTH)])])
    token = reducing["g0a"][4]
    early = names_1 + [(0, n) for n in FFN_BIG]
    sums = dict(zip(early, _pair_join(reduce_finish("g1", token) + reduce_finish("g0b", token), "grad_pair_join_early")))
    out = {n: adamw(n, sums[(0, n)], sums[(1, n)]) for n in FFN_BIG}
    late = names_rest + ["small"]
    sums.update(zip(late, _pair_join(reduce_finish("g0a", out["w_down"][1]), "grad_pair_join_late")))
    gsmall = _unpack_small_grads(_all_gather_chips([sums["small"]], "all_gather_small_grads")[0])
    for n in ORDER:
        if n not in out:
            g = [sums[(l, n)] for l in range(DEPTH)] if (0, n) in sums else [gsmall[l][n] for l in range(DEPTH)]
            out[n] = adamw(n, *g)
    undo = lambda n, a: jnp.swapaxes(a, 1, 2) if n in TRANSPOSED else a
    return (loss, dx.reshape(x.shape), *[undo(n, out[n][i]) for i in range(4) for n in ORDER])


def _step(xs, pos, tgt, p, entry, mix_weights, late_weights, ffn_hook, layer_hook):
    sps = [_small_operands(p, l) for l in range(DEPTH)]
    sps[0]["g_mix"] = sps[0]["g_mix"] + entry
    tabs = _rope_tables(pos, _rope_inv_freq())
    saved, h = [], xs
    for l in range(DEPTH):
        kw = _kernel_weights(mix_weights(l, h))
        h, sv = _layer_fwd(h, tabs, kw, functools.partial(late_weights, l), sps[l], l)
        saved.append(dict(sv, kw=kw))
    dy, lpart = _loss_grad(h, tgt)
    for l in reversed(range(DEPTH)):
        dy, g = _layer_bwd(dy, saved[l], tabs, saved[l]["kw"], sps[l], l, functools.partial(ffn_hook, l))
        zero = layer_hook(l, _big_grads(g), _small_grads(g))
        if zero is not None and l > 0:
            sps[l - 1]["g_ffn"] = sps[l - 1]["g_ffn"] + zero
    return 0.5 / D * jnp.sum(lpart), dy
```

```python
import functools
import math

import jax
import jax.numpy as jnp
from jax import lax
from jax.experimental import pallas as pl
from jax.experimental.pallas import tpu as pltpu

F32 = jnp.float32
BF16 = jnp.bfloat16
MESH = pl.DeviceIdType.MESH

D = 1024
HEADS = 4
QK = 96
NOPE = 64
ROPE = 32
VH = 128
HP = 128
QL = 256
KVL = 128
SGU = 256
POOL = 256
CHUNK = 128
HID = 2816
CHIPS = 4
SH = HID // CHIPS
IN_W = 1184
IN_P = 1280
EPS = 1e-6
ROPE_THETA = 10000.0
SCALE = 1.0 / math.sqrt(QK)
LOG2E = 1.4426950408889634
EXP2_C = SCALE * LOG2E
ATT_SPLIT = 2
ATT_WIDE = 4
NEG = -1e30
HALO = 16

LR, B1, B2, ADAM_EPS, WD, STEP = 0.001, 0.9, 0.999, 1e-08, 0.01, 10

VMEM_LIMIT = 56 * 1024 * 1024
LANES = 128
TOKENS = 1024


def _cp(sem, vmem=None):
    return pltpu.CompilerParams(dimension_semantics=sem, vmem_limit_bytes=vmem)


def _res(shape):
    nd = len(shape)
    return pl.BlockSpec(shape, lambda *_: (0,) * nd, pipeline_mode=pl.Buffered(1))


def _acc(shape):
    nd = len(shape)
    return pl.BlockSpec(shape, lambda *_: (0,) * nd)


def _dot(a, b):
    return jnp.dot(a, b, preferred_element_type=F32)


def _dot_nt(a, b):
    return lax.dot_general(a, b, (((1,), (1,)), ((), ())), preferred_element_type=F32)


def _dot_tn(a, b):
    return lax.dot_general(a, b, (((0,), (0,)), ((), ())), preferred_element_type=F32)


def _rms(x, n):
    r = lax.rsqrt(jnp.sum(x * x, axis=-1, keepdims=True) * (1.0 / n) + EPS)
    return x * r, r


def _rms_bwd(xn, r, g, dy, n):
    dn = dy * g
    dx = r * (dn - xn * (jnp.sum(dn * xn, axis=-1, keepdims=True) * (1.0 / n)))
    return dx, jnp.sum(dy * xn, axis=0, keepdims=True)


def _accumulate(ref, val, first):
    @pl.when(first)
    def _():
        ref[...] = val

    @pl.when(jnp.logical_not(first))
    def _():
        ref[...] += val


def _accumulate0(ref, val, first):
    @pl.when(first)
    def _():
        ref[0] = val

    @pl.when(jnp.logical_not(first))
    def _():
        ref[0] += val


def _tile(s, t):
    return min(s, t)


def _row_tile(r, cap):
    if r <= cap:
        return r
    return max(t for t in range(8, cap + 1, 8) if r % t == 0)


def _rope_tables(pos, invf):
    s = pos.shape[0]
    tm = _tile(s, 1024)

    def body(pos_ref, invf_ref, c_ref, sa_ref, sb_ref):
        ang = pos_ref[...].astype(F32) * invf_ref[...]
        c, sn = jnp.cos(ang), jnp.sin(ang)
        lane = lax.broadcasted_iota(jnp.int32, ang.shape, 1)
        first = (lane >= NOPE) & (lane < NOPE + ROPE // 2)
        second = (lane >= NOPE + ROPE // 2) & (lane < QK)
        c_ref[...] = jnp.where(first | second, c, 1.0)
        sa_ref[...] = jnp.where(first, -sn, 0.0)
        sb_ref[...] = jnp.where(second, sn, 0.0)

    out = jax.ShapeDtypeStruct((s, HP), F32)
    return pl.pallas_call(
        body, name="rope_tables", grid=(s // tm,),
        in_specs=[pl.BlockSpec((tm, 1), lambda i: (i, 0)), _acc((1, HP))],
        out_specs=[pl.BlockSpec((tm, HP), lambda i: (i, 0))] * 3,
        out_shape=[out] * 3, compiler_params=_cp(("parallel",)),
    )(pos, invf)


def _rope(x, c, sa, sb):
    return x * c + pltpu.roll(x, HP - ROPE // 2, 1) * sa + pltpu.roll(x, ROPE // 2, 1) * sb


def _rope_t(d, c, sa, sb):
    return d * c + pltpu.roll(d * sa, ROPE // 2, 1) + pltpu.roll(d * sb, HP - ROPE // 2, 1)


def _in_proj_fwd(x, g, w, name):
    s = x.shape[0]
    tm = _tile(s, TOKENS)

    def body(x_ref, g_ref, w_ref, z_ref, h_ref):
        xn, _ = _rms(x_ref[...], D)
        h = (xn * g_ref[...]).astype(BF16)
        h_ref[...] = h
        z_ref[...] = _dot(h, w_ref[...])

    return pl.pallas_call(
        body, name=name, grid=(s // tm,),
        in_specs=[pl.BlockSpec((tm, D), lambda i: (i, 0)), _acc((1, D)), _res((D, IN_P))],
        out_specs=[pl.BlockSpec((tm, IN_P), lambda i: (i, 0)), pl.BlockSpec((tm, D), lambda i: (i, 0))],
        out_shape=[jax.ShapeDtypeStruct((s, IN_P), F32), jax.ShapeDtypeStruct((s, D), BF16)],
        compiler_params=_cp(("parallel",), VMEM_LIMIT),
    )(x, g, w)


def _mla_prep_fwd(z, tabs, gql, gkv, gq, gk, wq, wk, wv, name):
    s = z.shape[0]
    tm = _tile(s, TOKENS)

    def body(ql_ref, kv_ref, kr_ref, c_ref, sa_ref, sb_ref, gql_ref, gkv_ref, gq_ref, gk_ref,
             wq_ref, wk_ref, wv_ref, q_out, k_out, v_out):
        qn = (_rms(ql_ref[...], QL)[0] * gql_ref[...]).astype(BF16)
        kvn = (_rms(kv_ref[...], KVL)[0] * gkv_ref[...]).astype(BF16)
        qraw = _dot(qn, wq_ref[...])
        kraw = _dot(kvn, wk_ref[...])
        vraw = _dot(kvn, wv_ref[...])
        kr = kr_ref[...]
        c, sa, sb = c_ref[...], sa_ref[...], sb_ref[...]
        for h in range(HEADS):
            sl = slice(h * HP, (h + 1) * HP)
            xq = _rms(qraw[:, sl], QK)[0] * gq_ref[...]
            q_out[h] = _rope(xq, c, sa, sb).astype(BF16)
            xk = _rms(kraw[:, sl] + kr, QK)[0] * gk_ref[...]
            k_out[h] = _rope(xk, c, sa, sb).astype(BF16)
            v_out[h] = vraw[:, sl].astype(BF16)

    row = lambda w, j: pl.BlockSpec((tm, w), lambda i: (i, j))
    hspec = pl.BlockSpec((HEADS, tm, HP), lambda i: (0, i, 0))
    hshape = jax.ShapeDtypeStruct((HEADS, s, HP), BF16)
    return pl.pallas_call(
        body, name=name, grid=(s // tm,),
        in_specs=[row(QL, 0), row(KVL, 2), row(HP, 3), row(HP, 0), row(HP, 0), row(HP, 0),
                  _acc((1, QL)), _acc((1, KVL)), _acc((1, HP)), _acc((1, HP)),
                  _acc((QL, HEADS * HP)), _acc((KVL, HEADS * HP)), _acc((KVL, HEADS * HP))],
        out_specs=[hspec] * 3, out_shape=[hshape] * 3,
        compiler_params=_cp(("parallel",)),
    )(z, z, z, *tabs, gql, gkv, gq, gk, wq, wk, wv)


def _causal_mask(s, row0):
    row = lax.broadcasted_iota(jnp.int32, s.shape, 0) + row0
    col = lax.broadcasted_iota(jnp.int32, s.shape, 1)
    return jnp.where(col <= row, s, NEG)


def _attn_fwd(q, k, v, name):
    s = q.shape[1]
    tq = _tile(s, 512)
    wide = ATT_WIDE * tq if s % (ATT_WIDE * tq) == 0 else tq
    rh = tq // ATT_SPLIT

    def body(q_ref, k_ref, v_ref, o_ref, lse_ref):
        i = pl.program_id(1)

        def blk(off, tk, carry, masked):
            keys = [(g + 1) * rh if masked else tq for g in range(ATT_SPLIT)]
            rows = lambda t: pl.ds(pl.multiple_of(off + t * tq, tq), tq)
            score = lambda g, t: _dot_nt(q_ref[0, g * rh:(g + 1) * rh, :], k_ref[0, rows(t), :][:keys[g]])
            state = list(carry)
            scs = {(g, 0): score(g, 0) for g in range(ATT_SPLIT)}
            for t in range(tk // tq):
                if (t + 1) * tq < tk:
                    scs.update({(g, t + 1): score(g, t + 1) for g in range(ATT_SPLIT)})
                vt = v_ref[0, rows(t), :]
                for g, (m, l, acc) in enumerate(state):
                    sc = scs.pop((g, t))
                    if masked:
                        sc = _causal_mask(sc, g * rh)
                    m_new = jnp.maximum(m, jnp.max(sc, axis=-1, keepdims=True))
                    p = jnp.exp2((sc - m_new) * EXP2_C)
                    alpha = jnp.exp2((m - m_new) * EXP2_C)
                    l = alpha * l + jnp.sum(p, axis=-1, keepdims=True)
                    acc = alpha * acc + _dot(p.astype(BF16), vt[:keys[g]])
                    state[g] = (m_new, l, acc)
            return tuple(state)

        one = (jnp.full((rh, 1), NEG, F32), jnp.zeros((rh, 1), F32), jnp.zeros((rh, VH), F32))
        nwide = (i * tq) // wide
        carry = lax.fori_loop(0, nwide, lambda j, c: blk(j * wide, wide, c, False), (one,) * ATT_SPLIT)
        carry = lax.fori_loop(nwide * (wide // tq), i, lambda j, c: blk(j * tq, tq, c, False), carry)
        carry = blk(i * tq, tq, carry, True)
        for g, (m, l, acc) in enumerate(carry):
            o_ref[g * rh:(g + 1) * rh, :] = acc / l
            lse_ref[0, g * rh:(g + 1) * rh, :] = jnp.broadcast_to(m * EXP2_C + jnp.log(l) * LOG2E, (rh, LANES))

    return pl.pallas_call(
        body, name=name, grid=(HEADS, s // tq),
        in_specs=[pl.BlockSpec((1, tq, HP), lambda h, i: (h, i, 0)),
                  pl.BlockSpec((1, s, HP), lambda h, i: (h, 0, 0)),
                  pl.BlockSpec((1, s, HP), lambda h, i: (h, 0, 0))],
        out_specs=[pl.BlockSpec((tq, VH), lambda h, i: (i, h)),
                   pl.BlockSpec((1, tq, LANES), lambda h, i: (h, i, 0))],
        out_shape=[jax.ShapeDtypeStruct((s, HEADS * VH), F32), jax.ShapeDtypeStruct((HEADS, s, LANES), F32)],
        compiler_params=_cp(("parallel", "arbitrary"), VMEM_LIMIT),
    )(q, k, v)


def _lane_group(shape, j):
    return (lax.broadcasted_iota(jnp.int32, shape, 1) + j * LANES) // (POOL // 4)


def _pool_win_fwd(z, name):
    s = z.shape[0]
    ch = _tile(s, 512)
    col0 = (IN_P - POOL) // LANES

    def body(p_ref, m_ref):
        j = pl.program_id(0)

        def chunk(r, _):
            off = pl.multiple_of(r * ch, ch)
            cur = p_ref[pl.ds(off, ch), :]
            hoff = pl.multiple_of(jnp.maximum(off - HALO, 0), 8)
            halo = jnp.where(r > 0, p_ref[pl.ds(hoff, HALO), :], 0.0)
            x = jnp.concatenate([halo, cur], axis=0)
            s2 = x + pltpu.roll(x, 1, 0)
            s4 = s2 + pltpu.roll(s2, 2, 0)
            s8 = s4 + pltpu.roll(s4, 4, 0)
            s16 = s8 + pltpu.roll(s8, 8, 0)
            grp = _lane_group((ch, LANES), j)
            sel = jnp.where(grp == 0, s2[HALO:], jnp.where(grp == 1, s4[HALO:], jnp.where(grp == 2, s8[HALO:], s16[HALO:])))
            t1 = (lax.broadcasted_iota(jnp.int32, (ch, LANES), 0) + off + 1).astype(F32)
            win = jnp.where(grp == 0, 2.0, jnp.where(grp == 1, 4.0, jnp.where(grp == 2, 8.0, 16.0)))
            m_ref[pl.ds(off, ch), :] = sel / jnp.minimum(t1, win) - cur
            return 0

        lax.fori_loop(0, s // ch, chunk, 0)

    return pl.pallas_call(
        body, name=name, grid=(POOL // LANES,),
        in_specs=[pl.BlockSpec((s, LANES), lambda j: (0, col0 + j))],
        out_specs=pl.BlockSpec((s, LANES), lambda j: (0, j)),
        out_shape=jax.ShapeDtypeStruct((s, POOL), F32),
        compiler_params=_cp(("parallel",), VMEM_LIMIT),
    )(z)


def _pool_win_bwd(dm, name):
    s = dm.shape[0]
    ch = _tile(s, 512)
    n = s // ch

    def body(dm_ref, dp_ref):
        j = pl.program_id(0)

        def chunk(r, _):
            off = pl.multiple_of(r * ch, ch)
            grp = _lane_group((ch + HALO, LANES), j)
            win = jnp.where(grp == 0, 2.0, jnp.where(grp == 1, 4.0, jnp.where(grp == 2, 8.0, 16.0)))
            cur = dm_ref[pl.ds(off, ch), :]
            hoff = pl.multiple_of(jnp.minimum(off + ch, s - HALO), 8)
            halo = jnp.where(r < n - 1, dm_ref[pl.ds(hoff, HALO), :], 0.0)
            x = jnp.concatenate([cur, halo], axis=0)
            t1 = (lax.broadcasted_iota(jnp.int32, (ch + HALO, LANES), 0) + off + 1).astype(F32)
            e = x / jnp.minimum(t1, win)
            tot = ch + HALO
            r2 = e + pltpu.roll(e, tot - 1, 0)
            r4 = r2 + pltpu.roll(r2, tot - 2, 0)
            r8 = r4 + pltpu.roll(r4, tot - 4, 0)
            r16 = r8 + pltpu.roll(r8, tot - 8, 0)
            g = grp[:ch]
            sel = jnp.where(g == 0, r2[:ch], jnp.where(g == 1, r4[:ch], jnp.where(g == 2, r8[:ch], r16[:ch])))
            dp_ref[pl.ds(off, ch), :] = (sel - cur).astype(BF16)
            return 0

        lax.fori_loop(0, n, chunk, 0)

    return pl.pallas_call(
        body, name=name, grid=(POOL // LANES,),
        in_specs=[pl.BlockSpec((s, LANES), lambda j: (0, j))],
        out_specs=pl.BlockSpec((s, LANES), lambda j: (0, j)),
        out_shape=jax.ShapeDtypeStruct((s, POOL), BF16),
        compiler_params=_cp(("parallel",), VMEM_LIMIT),
    )(dm)


def _head_mask(h):
    lane = lax.broadcasted_iota(jnp.int32, (CHUNK, SGU), 1)
    return (lane // (SGU // HEADS)) == h


def _tril(upper=False):
    row = lax.broadcasted_iota(jnp.int32, (CHUNK, CHUNK), 0)
    col = lax.broadcasted_iota(jnp.int32, (CHUNK, CHUNK), 1)
    return col >= row if upper else col <= row


def _sgu_gate(vn, wsp, bsp):
    out = []
    for cidx in range(vn.shape[0] // CHUNK):
        vc = vn[cidx * CHUNK:(cidx + 1) * CHUNK]
        zc = bsp
        for h in range(HEADS):
            zc = zc + jnp.where(_head_mask(h), _dot(wsp[h], vc), 0.0)
        out.append(zc)
    return jnp.concatenate(out, axis=0)


def _mix_out_fwd(o, z, m, x, wsp, bsp, wbd, psc, gsv, gout, wout, name):
    s = x.shape[0]
    tm = _tile(s, TOKENS)

    def body(o_ref, uv_ref, m_ref, x_ref, wsp_ref, bsp_ref, wbd_ref, psc_ref, gsv_ref, gout_ref, wout_ref,
             x1_ref, mix_ref):
        g = gout_ref[...]
        an = _rms(o_ref[...], HEADS * VH)[0] * g[:, :512]
        uv = uv_ref[...]
        u, v = uv[:, :SGU], uv[:, SGU:]
        vn = (_rms(v, SGU)[0] * gsv_ref[...]).astype(BF16)
        tri = _tril()
        wsp_m = [jnp.where(tri, wsp_ref[h], 0.0).astype(BF16) for h in range(HEADS)]
        gm = u * _sgu_gate(vn, wsp_m, bsp_ref[...])
        gn = _rms(gm, SGU)[0] * g[:, 512:768]
        po = _dot(m_ref[...].astype(BF16), wbd_ref[...]) * psc_ref[...]
        pn = _rms(po, POOL)[0] * g[:, 768:]
        mix = jnp.concatenate([an, gn, pn], axis=1).astype(BF16)
        mix_ref[...] = mix
        x1_ref[...] = x_ref[...] + _dot(mix, wout_ref[...])

    row = lambda w, j: pl.BlockSpec((tm, w), lambda i: (i, j))
    return pl.pallas_call(
        body, name=name, grid=(s // tm,),
        in_specs=[row(512, 0), row(512, 1), row(POOL, 0), row(D, 0),
                  _acc((HEADS, CHUNK, CHUNK)), _acc((CHUNK, SGU)), _acc((POOL, POOL)), _acc((1, POOL)),
                  _acc((1, SGU)), _acc((1, D)), _res((D, D))],
        out_specs=[row(D, 0), row(D, 0)],
        out_shape=[jax.ShapeDtypeStruct((s, D), F32), jax.ShapeDtypeStruct((s, D), BF16)],
        compiler_params=_cp(("parallel",), VMEM_LIMIT),
    )(o, z, m, x, wsp, bsp, wbd, psc, gsv, gout, wout)


def _ffn_fwd(x1, g, wg, wu, wd, name):
    s = x1.shape[0]
    tm = _tile(s, 256)

    def body(x_ref, g_ref, wg_ref, wu_ref, wd_ref, x2_ref, a_ref, b_ref, h_ref):
        x = x_ref[...]
        h = (_rms(x, D)[0] * g_ref[...]).astype(BF16)
        h_ref[...] = h
        acc = jnp.zeros((tm, D), F32)
        for k in range(CHIPS):
            a = _dot_nt(h, wg_ref[k])
            b = _dot_nt(h, wu_ref[k])
            a_ref[k] = a
            b_ref[k] = b
            acc = acc + _dot((a * jax.nn.sigmoid(a) * b).astype(BF16), wd_ref[k])
        x2_ref[...] = x + acc

    row = lambda w: pl.BlockSpec((tm, w), lambda i: (i, 0))
    hrow = pl.BlockSpec((CHIPS, tm, SH), lambda i: (0, i, 0))
    hshape = jax.ShapeDtypeStruct((CHIPS, s, SH), F32)
    return pl.pallas_call(
        body, name=name, grid=(s // tm,),
        in_specs=[row(D), _acc((1, D)), _res((CHIPS, SH, D)), _res((CHIPS, SH, D)), _res((CHIPS, SH, D))],
        out_specs=[row(D), hrow, hrow, row(D)],
        out_shape=[jax.ShapeDtypeStruct((s, D), F32), hshape, hshape, jax.ShapeDtypeStruct((s, D), BF16)],
        compiler_params=_cp(("parallel",), VMEM_LIMIT),
    )(x1, g, wg, wu, wd)


def _loss_grad(y, tgt):
    s = y.shape[0]
    tm = _tile(s, TOKENS)

    def body(y_ref, t_ref, dy_ref, l_ref):
        e = y_ref[...] - t_ref[...]
        dy_ref[...] = e * (1.0 / D)
        sq = jnp.sum(e * e, axis=0, keepdims=True)
        part = sq[:, :LANES]
        for c in range(1, D // LANES):
            part = part + sq[:, c * LANES:(c + 1) * LANES]
        _accumulate(l_ref, part, pl.program_id(0) == 0)

    row = pl.BlockSpec((tm, D), lambda i: (i, 0))
    return pl.pallas_call(
        body, name="loss_grad", grid=(s // tm,),
        in_specs=[row, row], out_specs=[row, _acc((1, LANES))],
        out_shape=[jax.ShapeDtypeStruct((s, D), F32), jax.ShapeDtypeStruct((1, LANES), F32)],
        compiler_params=_cp(("arbitrary",)),
    )(y, tgt)


def _wgrad(a, b, name):
    s, k = a.shape
    n = b.shape[1]
    half = lambda v: v if v <= 1408 else v // 2
    kb, nb, tt = half(k), half(n), _tile(s, 2048)

    def body(a_ref, b_ref, o_ref):
        _accumulate(o_ref, _dot_tn(a_ref[...].astype(BF16), b_ref[...].astype(BF16)), pl.program_id(2) == 0)

    return pl.pallas_call(
        body, name=name, grid=(k // kb, n // nb, s // tt),
        in_specs=[pl.BlockSpec((tt, kb), lambda i, j, t: (t, i)), pl.BlockSpec((tt, nb), lambda i, j, t: (t, j))],
        out_specs=pl.BlockSpec((kb, nb), lambda i, j, t: (i, j)),
        out_shape=jax.ShapeDtypeStruct((k, n), F32),
        compiler_params=_cp(("parallel", "parallel", "arbitrary"), VMEM_LIMIT),
    )(a, b)


def _wgrad_in(h, dzm, duv, dp, name):
    s = h.shape[0]
    tt = _tile(s, 2048)

    def body(h_ref, a_ref, b_ref, c_ref, o_ref):
        hv = h_ref[...]
        val = jnp.concatenate([_dot_tn(hv, a_ref[...]), _dot_tn(hv, b_ref[...]), _dot_tn(hv, c_ref[...])], axis=1)
        _accumulate(o_ref, val, pl.program_id(0) == 0)

    row = lambda w: pl.BlockSpec((tt, w), lambda t: (t, 0))
    return pl.pallas_call(
        body, name=name, grid=(s // tt,), in_specs=[row(D), row(512), row(512), row(POOL)], out_specs=_acc((D, IN_P)),
        out_shape=jax.ShapeDtypeStruct((D, IN_P), F32), compiler_params=_cp(("arbitrary",), VMEM_LIMIT),
    )(h, dzm, duv, dp)


def _wgrad_rows(a, b, name):
    s, n = a.shape[1:]
    nn = b.shape[1]
    tt = _tile(s, 4096 if b.dtype == BF16 else 2048)

    def body(a_ref, b_ref, o_ref):
        _accumulate0(o_ref, _dot_tn(a_ref[0].astype(BF16), b_ref[...].astype(BF16)), pl.program_id(1) == 0)

    return pl.pallas_call(
        body, name=name, grid=(CHIPS, s // tt),
        in_specs=[pl.BlockSpec((1, tt, n), lambda c, t: (c, t, 0)), pl.BlockSpec((tt, nn), lambda c, t: (t, 0))],
        out_specs=pl.BlockSpec((1, n, nn), lambda c, t: (c, 0, 0)),
        out_shape=jax.ShapeDtypeStruct((CHIPS, n, nn), F32),
        compiler_params=_cp(("parallel", "arbitrary"), VMEM_LIMIT),
    )(a, b)


def _ffn_bwd(dx2, x1, a, b, g, wg, wu, wd, name):
    s = x1.shape[0]
    tm = _tile(s, 256)

    def body(dx2_ref, x_ref, a_ref, b_ref, g_ref, wg_ref, wu_ref, wd_ref,
             dx1_ref, hid_ref, da_ref, db_ref, dg_ref):
        dx2 = dx2_ref[...]
        dyb = dx2.astype(BF16)
        dh = jnp.zeros((tm, D), F32)
        for k in range(CHIPS):
            av, bv = a_ref[k], b_ref[k]
            dhid = _dot_nt(dyb, wd_ref[k])
            sig = jax.nn.sigmoid(av)
            sa = av * sig
            hid_ref[k] = (sa * bv).astype(BF16)
            dbv = (dhid * sa).astype(BF16)
            dav = (dhid * bv * (sig * (1.0 + av * (1.0 - sig)))).astype(BF16)
            db_ref[k] = dbv
            da_ref[k] = dav
            dh = dh + _dot(dav, wg_ref[k]) + _dot(dbv, wu_ref[k])
        xn, r = _rms(x_ref[...], D)
        dxr, dg = _rms_bwd(xn, r, g_ref[...], dh, D)
        dx1_ref[...] = dx2 + dxr
        _accumulate(dg_ref, dg, pl.program_id(0) == 0)

    row = lambda w: pl.BlockSpec((tm, w), lambda i: (i, 0))
    hrow = pl.BlockSpec((CHIPS, tm, SH), lambda i: (0, i, 0))
    hid = jax.ShapeDtypeStruct((CHIPS, s, SH), BF16)
    return pl.pallas_call(
        body, name=name, grid=(s // tm,),
        in_specs=[row(D), row(D), hrow, hrow, _acc((1, D)), _res((CHIPS, SH, D)), _res((CHIPS, SH, D)),
                  _res((CHIPS, SH, D))],
        out_specs=[row(D), hrow, hrow, hrow, _acc((1, D))],
        out_shape=[jax.ShapeDtypeStruct((s, D), F32), hid, hid, hid, jax.ShapeDtypeStruct((1, D), F32)],
        compiler_params=_cp(("arbitrary",), VMEM_LIMIT),
    )(dx2, x1, a, b, g, wg, wu, wd)


def _mix_out_bwd(dx1, o, z, m, wsp, bsp, wbd, psc, gsv, gout, wout, name):
    s = dx1.shape[0]
    tm = _tile(s, TOKENS)

    def body(dx1_ref, o_ref, uv_ref, m_ref, wsp_ref, bsp_ref, wbd_ref, psc_ref, gsv_ref, gout_ref, wout_ref,
             do_ref, dl_ref, duv_ref, dm_ref, dgo_ref, dgsv_ref, dpsc_ref, dwsp_ref, dbsp_ref, dwbd_ref):
        first = pl.program_id(0) == 0
        g = gout_ref[...]
        dmix = _dot_nt(dx1_ref[...].astype(BF16), wout_ref[...])
        o = o_ref[...]
        on, ro = _rms(o, HEADS * VH)
        do, dga = _rms_bwd(on, ro, g[:, :512], dmix[:, :512], HEADS * VH)
        for h in range(HEADS):
            sl = slice(h * VH, (h + 1) * VH)
            do_ref[h] = do[:, sl].astype(BF16)
            dl_ref[h] = jnp.broadcast_to(jnp.sum(do[:, sl] * o[:, sl], axis=-1, keepdims=True), (tm, LANES))
        uv = uv_ref[...]
        u, v = uv[:, :SGU], uv[:, SGU:]
        vx, rv = _rms(v, SGU)
        vn = (vx * gsv_ref[...]).astype(BF16)
        tri = _tril()
        wsp_m = [jnp.where(tri, wsp_ref[h], 0.0).astype(BF16) for h in range(HEADS)]
        zc = _sgu_gate(vn, wsp_m, bsp_ref[...])
        gm = u * zc
        gmn, rg = _rms(gm, SGU)
        dgm, dgg = _rms_bwd(gmn, rg, g[:, 512:768], dmix[:, 512:768], SGU)
        du = dgm * zc
        dzc = dgm * u
        dvn_parts = []
        dbsp = jnp.zeros((CHUNK, SGU), F32)
        dwsp = [jnp.zeros((CHUNK, CHUNK), F32) for _ in range(HEADS)]
        for cidx in range(tm // CHUNK):
            rs = slice(cidx * CHUNK, (cidx + 1) * CHUNK)
            dzc_c = dzc[rs]
            dbsp = dbsp + dzc_c
            dzb = dzc_c.astype(BF16)
            vc = vn[rs]
            dvn_c = jnp.zeros((CHUNK, SGU), F32)
            for h in range(HEADS):
                hm = _head_mask(h)
                dvn_c = dvn_c + jnp.where(hm, _dot_tn(wsp_m[h], dzb), 0.0)
                dwsp[h] = dwsp[h] + _dot_nt(jnp.where(hm, dzc_c, 0.0).astype(BF16), vc)
            dvn_parts.append(dvn_c)
        dvn = jnp.concatenate(dvn_parts, axis=0)
        dv, dgsv = _rms_bwd(vx, rv, gsv_ref[...], dvn, SGU)
        duv_ref[...] = jnp.concatenate([du, dv], axis=1).astype(BF16)
        mb = m_ref[...].astype(BF16)
        pw = _dot(mb, wbd_ref[...])
        po = pw * psc_ref[...]
        pon, rp = _rms(po, POOL)
        dpo, dgp = _rms_bwd(pon, rp, g[:, 768:], dmix[:, 768:], POOL)
        dpw = (dpo * psc_ref[...]).astype(BF16)
        dm_ref[...] = _dot_nt(dpw, wbd_ref[...])
        _accumulate(dgo_ref, jnp.concatenate([dga, dgg, dgp], axis=1), first)
        _accumulate(dgsv_ref, dgsv, first)
        _accumulate(dpsc_ref, jnp.sum(dpo * pw, axis=0, keepdims=True), first)
        _accumulate(dbsp_ref, dbsp, first)
        _accumulate(dwbd_ref, _dot_tn(mb, dpw), first)
        for h in range(HEADS):
            val = jnp.where(tri, dwsp[h], 0.0)

            @pl.when(first)
            def _(val=val, h=h):
                dwsp_ref[h] = val

            @pl.when(jnp.logical_not(first))
            def _(val=val, h=h):
                dwsp_ref[h] += val

    row = lambda w, j: pl.BlockSpec((tm, w), lambda i: (i, j))
    hspec = pl.BlockSpec((HEADS, tm, HP), lambda i: (0, i, 0))
    return pl.pallas_call(
        body, name=name, grid=(s // tm,),
        in_specs=[row(D, 0), row(512, 0), row(512, 1), row(POOL, 0),
                  _acc((HEADS, CHUNK, CHUNK)), _acc((CHUNK, SGU)),
                  _acc((POOL, POOL)), _acc((1, POOL)), _acc((1, SGU)), _acc((1, D)), _res((D, D))],
        out_specs=[hspec, hspec, row(512, 0), row(POOL, 0), _acc((1, D)), _acc((1, SGU)), _acc((1, POOL)),
                   _acc((HEADS, CHUNK, CHUNK)), _acc((CHUNK, SGU)), _acc((POOL, POOL))],
        out_shape=[jax.ShapeDtypeStruct((HEADS, s, HP), BF16), jax.ShapeDtypeStruct((HEADS, s, LANES), F32),
                   jax.ShapeDtypeStruct((s, 512), BF16), jax.ShapeDtypeStruct((s, POOL), F32),
                   jax.ShapeDtypeStruct((1, D), F32), jax.ShapeDtypeStruct((1, SGU), F32),
                   jax.ShapeDtypeStruct((1, POOL), F32), jax.ShapeDtypeStruct((HEADS, CHUNK, CHUNK), F32),
                   jax.ShapeDtypeStruct((CHUNK, SGU), F32), jax.ShapeDtypeStruct((POOL, POOL), F32)],
        compiler_params=_cp(("arbitrary",), VMEM_LIMIT),
    )(dx1, o, z, m, wsp, bsp, wbd, psc, gsv, gout, wout)


def _attn_bwd(q, k, v, do, lse, delta, after, name):
    s = q.shape[1]
    tq = tk = _tile(s, 512)
    nq = s // tq
    wide = ATT_WIDE * tq if s % (ATT_WIDE * tq) == 0 else tq

    def body(q_ref, k_ref, v_ref, do_ref, lse_ref, dl_ref, after_ref, dq_ref, dk_ref, dv_ref):
        del after_ref
        j = pl.program_id(1)

        @pl.when(j == 0)
        def _():
            dq_ref[...] = jnp.zeros_like(dq_ref)

        kj, vj = k_ref[0], v_ref[0]
        rh = tq // ATT_SPLIT

        def blk(start, rows, dk, dv, masked):
            offs = [pl.multiple_of(start + g * rh, rh) for g in range(rows // rh)]
            qs = [q_ref[0, pl.ds(off, rh), :] for off in offs]
            dos = [do_ref[0, pl.ds(off, rh), :] for off in offs]
            scs = [_dot_nt(qi, kj) for qi in qs]
            dps = [_dot_nt(doi, vj) for doi in dos]
            for g, off in enumerate(offs):
                lse_i = lse_ref[0, pl.ds(off, rh), :][:, :1]
                dl_i = dl_ref[0, pl.ds(off, rh), :][:, :1]
                sc = _causal_mask(scs[g], g * rh) if masked else scs[g]
                p = jnp.exp2(sc * EXP2_C - lse_i)
                ds = (p * (dps[g] - dl_i)).astype(BF16)
                dv = dv + _dot_tn(p.astype(BF16), dos[g])
                dk = dk + _dot_tn(ds, qs[g])
                dq_ref[0, pl.ds(off, rh), :] += _dot(ds, kj) * SCALE
            return dk, dv

        per = wide // tq
        zero = jnp.zeros((tk, HP), F32)
        dk, dv = blk(j * tq, tq, zero, zero, True)
        first_wide = (j + per) // per
        dk, dv = lax.fori_loop(j + 1, jnp.minimum(first_wide * per, nq), lambda i, c: blk(i * tq, tq, *c, False), (dk, dv))
        dk, dv = lax.fori_loop(first_wide, nq // per, lambda i, c: blk(i * wide, wide, *c, False), (dk, dv))
        dk_ref[0] = dk * SCALE
        dv_ref[0] = dv

    full = lambda: pl.BlockSpec((1, s, HP), lambda h, j: (h, 0, 0))
    blk_spec = lambda: pl.BlockSpec((1, tk, HP), lambda h, j: (h, j, 0))
    out = jax.ShapeDtypeStruct((HEADS, s, HP), F32)
    return pl.pallas_call(
        body, name=name, grid=(HEADS, s // tk),
        in_specs=[full(), blk_spec(), blk_spec(), full(), full(), full(), ANY],
        out_specs=[full(), blk_spec(), blk_spec()], out_shape=[out] * 3,
        compiler_params=_cp(("parallel", "arbitrary"), VMEM_LIMIT),
    )(q, k, v, do, lse, delta, after)


def _mla_prep_bwd(dq, dk, dv, z, tabs, gql, gkv, gq, gk, wq, wk, wv, name):
    s = z.shape[0]
    tm = _tile(s, TOKENS)

    def body(dq_ref, dk_ref, dv_ref, ql_ref, kv_ref, kr_ref, c_ref, sa_ref, sb_ref, gql_ref, gkv_ref, gq_ref, gk_ref,
             wq_ref, wk_ref, wv_ref,
             dz_ref, qn_ref, kvn_ref, dqr_ref, dkr_ref, dvr_ref, dgql_ref, dgkv_ref, dgq_ref, dgk_ref):
        first = pl.program_id(0) == 0
        qx, rq = _rms(ql_ref[...], QL)
        qn = (qx * gql_ref[...]).astype(BF16)
        kx, rk = _rms(kv_ref[...], KVL)
        kvn = (kx * gkv_ref[...]).astype(BF16)
        qn_ref[...] = qn
        kvn_ref[...] = kvn
        qraw = _dot(qn, wq_ref[...])
        kraw = _dot(kvn, wk_ref[...])
        kr = kr_ref[...]
        c, sa, sb = c_ref[...], sa_ref[...], sb_ref[...]
        lane = lax.broadcasted_iota(jnp.int32, (tm, HP), 1)
        rope_lanes = (lane >= NOPE) & (lane < QK)
        dkrope = jnp.zeros((tm, HP), F32)
        dgq = jnp.zeros((1, HP), F32)
        dgk = jnp.zeros((1, HP), F32)
        for h in range(HEADS):
            sl = slice(h * HP, (h + 1) * HP)
            xn, r = _rms(qraw[:, sl], QK)
            dx, dg = _rms_bwd(xn, r, gq_ref[...], _rope_t(dq_ref[h], c, sa, sb), QK)
            dqr_ref[:, sl] = dx.astype(BF16)
            dgq = dgq + dg
            xn, r = _rms(kraw[:, sl] + kr, QK)
            dx, dg = _rms_bwd(xn, r, gk_ref[...], _rope_t(dk_ref[h], c, sa, sb), QK)
            dkr_ref[:, sl] = dx.astype(BF16)
            dgk = dgk + dg
            dkrope = dkrope + jnp.where(rope_lanes, dx, 0.0)
            dvr_ref[:, sl] = dv_ref[h].astype(BF16)
        dqn = _dot_nt(dqr_ref[...], wq_ref[...])
        dql, dgql = _rms_bwd(qx, rq, gql_ref[...], dqn, QL)
        dkvn = _dot_nt(dkr_ref[...], wk_ref[...]) + _dot_nt(dvr_ref[...], wv_ref[...])
        dkv, dgkv = _rms_bwd(kx, rk, gkv_ref[...], dkvn, KVL)
        dz_ref[...] = jnp.concatenate([dql, dkv, dkrope], axis=1).astype(BF16)
        _accumulate(dgql_ref, dgql, first)
        _accumulate(dgkv_ref, dgkv, first)
        _accumulate(dgq_ref, dgq, first)
        _accumulate(dgk_ref, dgk, first)

    row = lambda w, j: pl.BlockSpec((tm, w), lambda i: (i, j))
    hspec = pl.BlockSpec((HEADS, tm, HP), lambda i: (0, i, 0))
    sd = lambda w, dt: jax.ShapeDtypeStruct((s, w), dt)
    return pl.pallas_call(
        body, name=name, grid=(s // tm,),
        in_specs=[hspec, hspec, hspec, row(QL, 0), row(KVL, 2), row(HP, 3), row(HP, 0), row(HP, 0), row(HP, 0),
                  _acc((1, QL)), _acc((1, KVL)), _acc((1, HP)), _acc((1, HP)),
                  _acc((QL, HEADS * HP)), _acc((KVL, HEADS * HP)), _acc((KVL, HEADS * HP))],
        out_specs=[row(512, 0), row(QL, 0), row(KVL, 0), row(512, 0), row(512, 0), row(512, 0),
                   _acc((1, QL)), _acc((1, KVL)), _acc((1, HP)), _acc((1, HP))],
        out_shape=[sd(512, BF16), sd(QL, BF16), sd(KVL, BF16), sd(512, BF16), sd(512, BF16), sd(512, BF16),
                   jax.ShapeDtypeStruct((1, QL), F32), jax.ShapeDtypeStruct((1, KVL), F32),
                   jax.ShapeDtypeStruct((1, HP), F32), jax.ShapeDtypeStruct((1, HP), F32)],
        compiler_params=_cp(("arbitrary",), VMEM_LIMIT),
    )(dq, dk, dv, z, z, z, *tabs, gql, gkv, gq, gk, wq, wk, wv)


def _in_proj_bwd(dzm, duv, dp, x, dx1, g, win, name):
    s = x.shape[0]
    tm = _tile(s, TOKENS // 2)

    def body(dzm_ref, duv_ref, dp_ref, x_ref, dx1_ref, g_ref, w_ref, dx_ref, dg_ref):
        dh = _dot_nt(dzm_ref[...], w_ref[:, 0:512]) + _dot_nt(duv_ref[...], w_ref[:, 512:1024]) \
            + _dot_nt(dp_ref[...], w_ref[:, 1024:IN_P])
        xn, r = _rms(x_ref[...], D)
        dxr, dg = _rms_bwd(xn, r, g_ref[...], dh, D)
        dx_ref[...] = dx1_ref[...] + dxr
        _accumulate(dg_ref, dg, pl.program_id(0) == 0)

    row = lambda w: pl.BlockSpec((tm, w), lambda i: (i, 0))
    return pl.pallas_call(
        body, name=name, grid=(s // tm,),
        in_specs=[row(512), row(512), row(POOL), row(D), row(D), _acc((1, D)), _res((D, IN_P))],
        out_specs=[row(D), _acc((1, D))],
        out_shape=[jax.ShapeDtypeStruct((s, D), F32), jax.ShapeDtypeStruct((1, D), F32)],
        compiler_params=_cp(("arbitrary",), VMEM_LIMIT),
    )(dzm, duv, dp, x, dx1, g, win)


def _adamw(w, g0, g1, m, v, name):
    _, r, c = w.shape
    tr = _row_tile(r, 512)
    c1 = 1.0 - B1 ** STEP
    c2 = 1.0 - B2 ** STEP

    def body(w_ref, g0_ref, g1_ref, m_ref, v_ref, g_ref, d_ref, nm_ref, nv_ref):
        gv = jnp.where(pl.program_id(0) == 0, g0_ref[...], g1_ref[...])
        g_ref[0] = gv
        nm = B1 * m_ref[0] + (1.0 - B1) * gv
        nv = B2 * v_ref[0] + (1.0 - B2) * (gv * gv)
        nm_ref[0] = nm
        nv_ref[0] = nv
        d_ref[0] = -LR * ((nm / c1) / (jnp.sqrt(nv / c2) + ADAM_EPS) + WD * w_ref[0])

    spec = pl.BlockSpec((1, tr, c), lambda l, i: (l, i, 0))
    out = jax.ShapeDtypeStruct((DEPTH, r, c), F32)
    return pl.pallas_call(
        body, name=name, grid=(DEPTH, r // tr),
        in_specs=[spec, pl.BlockSpec((tr, c), lambda l, i: (i * (1 - l), 0)), pl.BlockSpec((tr, c), lambda l, i: (i * l, 0)),
                  spec, spec],
        out_specs=[spec] * 4, out_shape=[out] * 4, compiler_params=_cp(("parallel", "parallel")),
    )(w, g0, g1, m, v)


ANY = pl.BlockSpec(memory_space=pl.ANY)


def _place():
    x, y, c = lax.axis_index("x"), lax.axis_index("y"), lax.axis_index("c")
    chips = [(1 - x, y), (x, 1 - y), (1 - x, 1 - y)]
    return x, y, c, chips


def _half_rows(ref, lead, hh, half, align):
    rows = pl.ds(pl.multiple_of(hh * half, align), half)
    return ref.at[rows, :] if lead is None else ref.at[lead, rows, :]


def _row_align(dtype):
    return 16 if dtype == BF16 else 8


def _sems(n):
    return [pltpu.SemaphoreType.DMA((n,)), pltpu.SemaphoreType.DMA((n,)), pltpu.SemaphoreType.DMA((n,))]


def _comm_call(body, ins, out_shapes, nsems, name):
    return pl.pallas_call(
        body, name=name, in_specs=[ANY] * len(ins), out_specs=[ANY] * len(out_shapes), out_shape=out_shapes,
        scratch_shapes=_sems(nsems), compiler_params=pltpu.CompilerParams(has_side_effects=True),
    )(*ins)


def _all_gather_chips(shards, name):
    n = len(shards)
    halves = [a.shape[0] // 2 for a in shards]
    aligns = [_row_align(a.dtype) for a in shards]
    assert all(h % al == 0 for h, al in zip(halves, aligns))

    def body(*refs):
        ins, outs, (send_sems, recv_sems, _) = refs[:n], refs[n:2 * n], refs[2 * n:]
        x, y, c, chips = _place()
        me = 2 * x + y
        sibling = (x, y, 1 - c)

        def copy(sem, src, dst, to):
            return pltpu.make_async_remote_copy(src_ref=src, dst_ref=dst, send_sem=send_sems.at[sem],
                                                recv_sem=recv_sems.at[sem], device_id=to, device_id_type=MESH)

        first, passed = [], []
        for a in range(n):
            my_half = _half_rows(ins[a], None, c, halves[a], aligns[a])
            for j, (cx, cy) in enumerate(chips):
                cp = copy(6 * a + j, my_half, _half_rows(outs[a], me, c, halves[a], aligns[a]), (cx, cy, c))
                cp.start()
                first.append(cp)
        for a in range(n):
            for j, (cx, cy) in enumerate(chips):
                landed = _half_rows(outs[a], 2 * cx + cy, c, halves[a], aligns[a])
                copy(6 * a + j, landed, landed, (cx, cy, c)).wait_recv()
                fwd = copy(6 * a + 3 + j, landed, landed, sibling)
                fwd.start()
                passed.append(fwd)
        for a in range(n):
            for j, (cx, cy) in enumerate(chips):
                other = _half_rows(outs[a], 2 * cx + cy, 1 - c, halves[a], aligns[a])
                copy(6 * a + 3 + j, other, other, sibling).wait_recv()
        for cp in first + passed:
            cp.wait_send()

    lands = _comm_call(body, shards, [jax.ShapeDtypeStruct((CHIPS,) + a.shape, a.dtype) for a in shards], 6 * n, name)
    return _with_own(lands, shards)


def _with_own(lands, shards):
    me = 2 * lax.axis_index("x") + lax.axis_index("y")
    return [lax.dynamic_update_slice(g, a[None], (me, 0, 0)) for g, a in zip(lands, shards)]


def _pair_join(arrs, name):
    n = len(arrs)
    halves = [a.shape[0] // 2 for a in arrs]

    def body(*refs):
        outs, (send_sems, recv_sems, _) = refs[n:2 * n], refs[2 * n:]
        x, y, c, _ = _place()
        cps = []
        for a in range(n):
            mine = _half_rows(outs[a], None, c, halves[a], 8)
            cp = pltpu.make_async_remote_copy(src_ref=mine, dst_ref=mine, send_sem=send_sems.at[a], recv_sem=recv_sems.at[a],
                                              device_id=(x, y, 1 - c), device_id_type=MESH)
            cp.start()
            cps.append(cp)
        for cp in cps:
            cp.wait()

    return pl.pallas_call(
        body, name=name, in_specs=[ANY] * n, out_specs=[ANY] * n,
        out_shape=[jax.ShapeDtypeStruct(a.shape, a.dtype) for a in arrs],
        input_output_aliases={i: i for i in range(n)}, scratch_shapes=_sems(n),
        compiler_params=pltpu.CompilerParams(has_side_effects=True),
    )(*arrs)


HBM = pl.BlockSpec(memory_space=pltpu.HBM)
SEM = pl.BlockSpec(memory_space=pltpu.SEMAPHORE)
DATAFLOW = pltpu.SideEffectType.DATAFLOW_SIDE_EFFECTING


def _remote_copies(pairs, ins, lands, send_sems, recv_sems):
    return [pltpu.make_async_remote_copy(src_ref=src, dst_ref=dst, send_sem=send_sems.at[i], recv_sem=recv_sems.at[i],
                                         device_id=to, device_id_type=MESH)
            for i, (src, dst, to) in enumerate(pairs(ins, lands))]


def _split_start(srcs, land_shapes, ncopies, pairs, name, after):
    n, m = len(srcs), len(land_shapes)

    def body(*refs):
        ins, lands = refs[:n], refs[n:n + m]
        send_sems, recv_sems, token = refs[n + m + 1], refs[n + m + 2], refs[-1]
        for cp in _remote_copies(pairs, ins, lands, send_sems, recv_sems):
            cp.start()
        token[...] = jnp.zeros_like(token)

    hbm = lambda a: pltpu.with_memory_space_constraint(a, pltpu.HBM)
    lands = [hbm(lax.empty(s.shape, s.dtype)) for s in land_shapes]
    thru = [pltpu.HBM(a.shape, a.dtype) for a in list(srcs) + lands]
    out = pl.pallas_call(
        body, name=name,
        out_shape=(pltpu.SemaphoreType.DMA((ncopies,)), pltpu.SemaphoreType.DMA((ncopies,)), *thru,
                   jax.ShapeDtypeStruct((8, LANES), F32)),
        in_specs=[HBM] * (n + m) + [ANY], out_specs=(SEM, SEM, *[HBM] * (n + m), pl.BlockSpec(memory_space=pltpu.VMEM)),
        input_output_aliases={i: 2 + i for i in range(n + m)},
        compiler_params=pltpu.CompilerParams(has_side_effects=DATAFLOW),
    )(*[hbm(a) for a in srcs], *lands, after)
    return out[0], out[1], list(out[2:2 + n]), list(out[2 + n:2 + n + m]), out[-1]


def _split_wait(send_sems, recv_sems, srcs, lands, after, pairs, name):
    n, m = len(srcs), len(lands)

    def body(*refs):
        ins, lands_ = refs[:n], refs[n:n + m]
        for cp in _remote_copies(pairs, ins, lands_, refs[n + m], refs[n + m + 1]):
            cp.wait_send()
            cp.wait_recv()

    out = pl.pallas_call(
        body, name=name, out_shape=tuple(pltpu.HBM(a.shape, a.dtype) for a in list(srcs) + list(lands)),
        in_specs=[HBM] * (n + m) + [SEM, SEM, ANY], out_specs=tuple([HBM] * (n + m)),
        input_output_aliases={i: i for i in range(n + m)},
        compiler_params=pltpu.CompilerParams(has_side_effects=DATAFLOW),
    )(*srcs, *lands, send_sems, recv_sems, after)
    return list(out[:n]), list(out[n:])


def _gather_pairs(halves, aligns):
    def pairs(ins, lands):
        x, y, c, chips = _place()
        me = 2 * x + y
        return [(_half_rows(ins[a], None, c, halves[a], aligns[a]), _half_rows(lands[a], me, c, halves[a], aligns[a]),
                 (cx, cy, c)) for a in range(len(ins)) for cx, cy in chips]
    return pairs


PEERS = 7


def _scatter_pairs(ins, lands):
    x, y, c, chips = _place()
    to = [(cx, cy, c) for cx, cy in chips] + [(cx, cy, 1 - c) for cx, cy in chips] + [(x, y, 1 - c)]
    out = []
    for a in range(len(ins)):
        half = ins[a].shape[1] // 2
        for i, (tx, ty, tc) in enumerate(to):
            out.append((_half_rows(ins[a], 2 * tx + ty, tc, half, 8), lands[a].at[i], (tx, ty, tc)))
    return out


def _gather_finish(shards, lands, name):
    n = len(shards)
    halves = [a.shape[0] // 2 for a in shards]
    aligns = [_row_align(a.dtype) for a in shards]

    def body(*refs):
        outs, (send_sems, recv_sems, _) = refs[n:2 * n], refs[2 * n:]
        x, y, c, chips = _place()
        passed = []
        for a in range(n):
            for j, (cx, cy) in enumerate(chips):
                landed = _half_rows(outs[a], 2 * cx + cy, c, halves[a], aligns[a])
                cp = pltpu.make_async_remote_copy(src_ref=landed, dst_ref=landed, send_sem=send_sems.at[3 * a + j],
                                                  recv_sem=recv_sems.at[3 * a + j], device_id=(x, y, 1 - c),
                                                  device_id_type=MESH)
                cp.start()
                passed.append(cp)
        for a in range(n):
            for j, (cx, cy) in enumerate(chips):
                other = _half_rows(outs[a], 2 * cx + cy, 1 - c, halves[a], aligns[a])
                pltpu.make_async_remote_copy(src_ref=other, dst_ref=other, send_sem=send_sems.at[3 * a + j],
                                             recv_sem=recv_sems.at[3 * a + j], device_id=(x, y, 1 - c),
                                             device_id_type=MESH).wait_recv()
        for cp in passed:
            cp.wait_send()

    lands = pl.pallas_call(
        body, name=name, in_specs=[ANY] * n, out_specs=[ANY] * n,
        out_shape=[jax.ShapeDtypeStruct(a.shape, a.dtype) for a in lands],
        input_output_aliases={i: i for i in range(n)}, scratch_shapes=_sems(3 * n),
        compiler_params=pltpu.CompilerParams(has_side_effects=True),
    )(*lands)
    return _with_own(lands, shards)


def _sum_own_and_landed(own, landed, where, name):
    _, half, cols = landed.shape
    tr = _row_tile(half, 128)
    nt = half // tr

    grid_spec = pltpu.PrefetchScalarGridSpec(
        num_scalar_prefetch=1, grid=(nt,),
        in_specs=[pl.BlockSpec((1, tr, cols), lambda r, w: (w[0], w[1] * nt + r, 0)),
                  pl.BlockSpec((PEERS, tr, cols), lambda r, w: (0, r, 0))],
        out_specs=pl.BlockSpec((tr, cols), lambda r, w: (w[1] * nt + r, 0)))

    def body(w_ref, p_ref, q_ref, o_ref):
        acc = p_ref[0]
        for i in range(PEERS):
            acc = acc + q_ref[i]
        o_ref[...] = acc

    return pl.pallas_call(
        body, name=name, grid_spec=grid_spec, out_shape=jax.ShapeDtypeStruct((2 * half, cols), own.dtype),
        compiler_params=_cp(("parallel",)),
    )(where, own, landed)


BIG = [("w_in", (D, IN_W), 1), ("w_q_up", (QL, HEADS * QK), 1), ("w_kv_up", (KVL, HEADS * (NOPE + VH)), 1),
       ("w_out", (D, D), 0), ("w_gate", (D, HID), 1), ("w_up", (D, HID), 1), ("w_down", (HID, D), 0)]
SMALL = [("g_mix_norm", (D,)), ("g_q_lat", (QL,)), ("g_kv_lat", (KVL,)), ("g_q_head", (QK,)), ("g_k_head", (QK,)),
         ("g_sgu_v", (SGU,)), ("w_spatial", (HEADS, CHUNK, CHUNK)), ("b_spatial", (HEADS, CHUNK)),
         ("w_pool", (4, 64, 64)), ("pool_scale", (POOL,)), ("g_out_mla", (512,)), ("g_out_sgu", (SGU,)),
         ("g_out_pool", (POOL,)), ("g_ffn_norm", (D,))]
ORDER = ["g_mix_norm", "w_in", "g_q_lat", "w_q_up", "g_kv_lat", "w_kv_up", "g_q_head", "g_k_head", "g_sgu_v",
         "w_spatial", "b_spatial", "w_pool", "pool_scale", "g_out_mla", "g_out_sgu", "g_out_pool", "w_out",
         "g_ffn_norm", "w_gate", "w_up", "w_down"]
EARLY_BIG = ["w_in", "w_q_up", "w_kv_up"]
FFN_BIG = ["w_gate", "w_up", "w_down"]
LATE_BIG = ["w_out"] + FFN_BIG
DEPTH = 2
COLS = 1024
SMALL_N = sum(math.prod(s) for _, s in SMALL) * DEPTH
assert SMALL_N % CHIPS == 0
SMALL_ROWS = -(-(SMALL_N // CHIPS) // (16 * COLS)) * 16


def _unsplit_cols(g):
    return g.transpose(1, 0, 2).reshape(g.shape[1], CHIPS * g.shape[2])


def _split_cols(full):
    r, c = full.shape
    return full.reshape(r, CHIPS, c // CHIPS).transpose(1, 0, 2)


def _kernel_weights(g):
    win = _unsplit_cols(g["w_in"])
    zeros = lambda r, c: jnp.zeros((r, c), BF16)
    o2, o3, o4 = QL + KVL, QL + KVL + ROPE, QL + KVL + ROPE + 2 * SGU
    win_p = jnp.concatenate([win[:, :o2], zeros(D, NOPE), win[:, o2:o3], zeros(D, HP - QK), win[:, o3:o4], win[:, o4:]], axis=1)
    wq = _unsplit_cols(g["w_q_up"]).reshape(QL, HEADS, QK)
    wq_p = jnp.pad(wq, ((0, 0), (0, 0), (0, HP - QK))).reshape(QL, HEADS * HP)
    wkv = _unsplit_cols(g["w_kv_up"]).reshape(KVL, HEADS, NOPE + VH)
    wk_p = jnp.pad(wkv[:, :, :NOPE], ((0, 0), (0, 0), (0, HP - NOPE))).reshape(KVL, HEADS * HP)
    wv_p = wkv[:, :, NOPE:].reshape(KVL, HEADS * VH)
    return dict(win=win_p, wq=wq_p, wk=wk_p, wv=wv_p)


def _small_operands(p, l):
    row = lambda v: v.reshape(1, -1)
    pad = lambda v: jnp.pad(v, (0, HP - QK)).reshape(1, HP)
    wpool = p["w_pool"][l]
    wbd = jnp.zeros((POOL, POOL), F32)
    for g in range(4):
        wbd = lax.dynamic_update_slice(wbd, wpool[g], (g * 64, g * 64))
    return dict(
        g_mix=row(p["g_mix_norm"][l]), gql=row(p["g_q_lat"][l]), gkv=row(p["g_kv_lat"][l]),
        gq=pad(p["g_q_head"][l]), gk=pad(p["g_k_head"][l]), gsv=row(p["g_sgu_v"][l]),
        wsp=p["w_spatial"][l], bsp=jnp.repeat(p["b_spatial"][l].T, SGU // HEADS, axis=1),
        wbd=wbd.astype(BF16), psc=row(p["pool_scale"][l]),
        gout=jnp.concatenate([p["g_out_mla"][l], p["g_out_sgu"][l], p["g_out_pool"][l]]).reshape(1, D),
        g_ffn=row(p["g_ffn_norm"][l]))


def _big_grads(g):
    dwin = g["win"]
    o2 = QL + KVL
    gin = jnp.concatenate([dwin[:, :o2], dwin[:, o2 + NOPE:o2 + NOPE + ROPE], dwin[:, 512:]], axis=1)
    gq = g["wq"].reshape(QL, HEADS, HP)[:, :, :QK].reshape(QL, HEADS * QK)
    gk = g["wk"].reshape(KVL, HEADS, HP)[:, :, :NOPE]
    gv = g["wv"].reshape(KVL, HEADS, VH)
    gkv = jnp.concatenate([gk, gv], axis=2).reshape(KVL, HEADS * (NOPE + VH))
    return {"w_in": _split_cols(gin), "w_q_up": _split_cols(gq), "w_kv_up": _split_cols(gkv),
            "w_out": g["wout"].reshape(CHIPS, D // CHIPS, D), "w_gate": g["wg"], "w_up": g["wu"], "w_down": g["wd"]}


TRANSPOSED = ("w_gate", "w_up")


def _small_grads(g):
    go = g["gout"].reshape(-1)
    return {"g_mix_norm": g["g_mix"].reshape(-1), "g_q_lat": g["gql"].reshape(-1), "g_kv_lat": g["gkv"].reshape(-1),
            "g_q_head": g["gq"].reshape(-1)[:QK], "g_k_head": g["gk"].reshape(-1)[:QK], "g_sgu_v": g["gsv"].reshape(-1),
            "w_spatial": g["wsp"], "b_spatial": g["bsp"].reshape(CHUNK, HEADS, SGU // HEADS).sum(-1).T,
            "w_pool": jnp.stack([g["wbd"][i * 64:(i + 1) * 64, i * 64:(i + 1) * 64] for i in range(4)]),
            "pool_scale": g["psc"].reshape(-1), "g_out_mla": go[:512], "g_out_sgu": go[512:768],
            "g_out_pool": go[768:], "g_ffn_norm": g["g_ffn"].reshape(-1)}


def _pack_small_grads(small):
    sm = jnp.concatenate([small[l][n].reshape(-1) for l in range(DEPTH) for n, _ in SMALL]).reshape(CHIPS, SMALL_N // CHIPS)
    return jnp.pad(sm, ((0, 0), (0, SMALL_ROWS * COLS - SMALL_N // CHIPS))).reshape(CHIPS, SMALL_ROWS, COLS)


def _unpack_small_grads(gathered):
    flat = gathered.reshape(CHIPS, SMALL_ROWS * COLS)[:, :SMALL_N // CHIPS].reshape(-1)
    out, off = [], 0
    for _ in range(DEPTH):
        layer = {}
        for n, shape in SMALL:
            k = math.prod(shape)
            layer[n] = flat[off:off + k].reshape(shape)
            off += k
        out.append(layer)
    return out


def _layer_fwd(x, tabs, kw, late_weights, sp, l):
    t = f"_l{l}"
    z, hb = _in_proj_fwd(x, sp["g_mix"], kw["win"], "in_proj_fwd" + t)
    q, k, v = _mla_prep_fwd(z, tabs, sp["gql"], sp["gkv"], sp["gq"], sp["gk"], kw["wq"], kw["wk"], kw["wv"],
                            "mla_prep_fwd" + t)
    o, lse = _attn_fwd(q, k, v, "attn_fwd" + t)
    m = _pool_win_fwd(z, "pool_win_fwd" + t)
    wout, wg, wu, wd = late_weights(o)
    wout = wout.reshape(D, D)
    x1, mix = _mix_out_fwd(o, z, m, x, sp["wsp"], sp["bsp"], sp["wbd"], sp["psc"], sp["gsv"], sp["gout"], wout,
                           "mix_out_fwd" + t)
    x2, a, b, h2 = _ffn_fwd(x1, sp["g_ffn"], wg, wu, wd, "ffn_fwd" + t)
    saved = dict(x=x, z=z, hb=hb, q=q, k=k, v=v, o=o, lse=lse, m=m, x1=x1, mix=mix, a=a, b=b, h2=h2, wg=wg, wu=wu, wd=wd,
                 wout=wout)
    return x2, saved


def _layer_bwd(dx2, sv, tabs, kw, sp, l, ffn_hook, out_hook):
    t = f"_l{l}"
    g = {}
    dx1, hid, da, db, g["g_ffn"] = _ffn_bwd(dx2, sv["x1"], sv["a"], sv["b"], sp["g_ffn"], sv["wg"], sv["wu"], sv["wd"],
                                            "ffn_bwd" + t)
    g["wd"] = _wgrad_rows(hid, dx2, "wgrad_down" + t)
    g["wg"] = _wgrad_rows(da, sv["h2"], "wgrad_gate" + t)
    g["wu"] = _wgrad_rows(db, sv["h2"], "wgrad_up" + t)
    gout = sp["gout"] + ffn_hook(g)
    do, delta, duv, dm, g["gout"], g["gsv"], g["psc"], g["wsp"], g["bsp"], g["wbd"] = _mix_out_bwd(
        dx1, sv["o"], sv["z"], sv["m"], sp["wsp"], sp["bsp"], sp["wbd"], sp["psc"], sp["gsv"], gout, sv["wout"],
        "mix_out_bwd" + t)
    g["wout"] = _wgrad(sv["mix"], dx1, "wgrad_out" + t)
    dp = _pool_win_bwd(dm, "pool_win_bwd" + t)
    dq, dk, dv = _attn_bwd(sv["q"], sv["k"], sv["v"], do, sv["lse"], delta, out_hook(g), "attn_bwd" + t)
    dzm, qn, kvn, dqr, dkr, dvr, g["gql"], g["gkv"], g["gq"], g["gk"] = _mla_prep_bwd(
        dq, dk, dv, sv["z"], tabs, sp["gql"], sp["gkv"], sp["gq"], sp["gk"], kw["wq"], kw["wk"], kw["wv"],
        "mla_prep_bwd" + t)
    g["wq"] = _wgrad(qn, dqr, "wgrad_q_up" + t)
    g["wk"] = _wgrad(kvn, dkr, "wgrad_k_up" + t)
    g["wv"] = _wgrad(kvn, dvr, "wgrad_v_up" + t)
    dx, g["g_mix"] = _in_proj_bwd(dzm, duv, dp, sv["x"], dx1, sp["g_mix"], kw["win"], "in_proj_bwd" + t)
    g["win"] = _wgrad_in(sv["hb"], dzm, duv, dp, "wgrad_in" + t)
    return dx, g


def _rope_inv_freq():
    half = ROPE // 2
    inv = 1.0 / (ROPE_THETA ** (jnp.arange(half, dtype=F32) / half))
    return jnp.concatenate([jnp.zeros((NOPE,), F32), inv, inv, jnp.zeros((HP - QK,), F32)]).reshape(1, HP)


def kernel(x, positions, g_mix_norm, w_in, g_q_lat, w_q_up, g_kv_lat, w_kv_up, g_q_head, g_k_head, g_sgu_v, w_spatial, b_spatial, w_pool, pool_scale, g_out_mla, g_out_sgu, g_out_pool, w_out, g_ffn_norm, w_gate, w_up, w_down, loss_target, m_g_mix_norm, m_w_in, m_g_q_lat, m_w_q_up, m_g_kv_lat, m_w_kv_up, m_g_q_head, m_g_k_head, m_g_sgu_v, m_w_spatial, m_b_spatial, m_w_pool, m_pool_scale, m_g_out_mla, m_g_out_sgu, m_g_out_pool, m_w_out, m_g_ffn_norm, m_w_gate, m_w_up, m_w_down, v_g_mix_norm, v_w_in, v_g_q_lat, v_w_q_up, v_g_kv_lat, v_w_kv_up, v_g_q_head, v_g_k_head, v_g_sgu_v, v_w_spatial, v_b_spatial, v_w_pool, v_pool_scale, v_g_out_mla, v_g_out_sgu, v_g_out_pool, v_w_out, v_g_ffn_norm, v_w_gate, v_w_up, v_w_down):
    given = dict(locals())
    p = {n: given[n] for n in ORDER}
    view = lambda pre, n: jnp.swapaxes(given[pre + n], 1, 2) if n in TRANSPOSED else given[pre + n]
    seq = x.shape[1]
    where = jnp.stack([2 * lax.axis_index("x") + lax.axis_index("y"), lax.axis_index("c")]).astype(jnp.int32)
    shards = lambda names: [view("", n)[l].astype(BF16) for l, n in names]
    zero11 = lambda token: token[:1, :1]

    names_0a = [(0, n) for n in EARLY_BIG]
    names_0b = [(0, n) for n in LATE_BIG]
    names_1 = [(1, n) for n, _, _ in BIG]
    got_0a = dict(zip(EARLY_BIG, _all_gather_chips(shards(names_0a), "all_gather_w0a")))
    started, issued = {}, got_0a["w_in"]
    for tag, names in (("w0b", names_0b), ("w1", names_1)):
        sh = shards(names)
        pairs = _gather_pairs([a.shape[0] // 2 for a in sh], [_row_align(a.dtype) for a in sh])
        lands = [jax.ShapeDtypeStruct((CHIPS,) + a.shape, a.dtype) for a in sh]
        started[tag] = (sh, pairs) + _split_start(sh, lands, 3 * len(sh), pairs, "gather_start_" + tag, issued)
        issued = started[tag][6]

    def arrived(tag, after):
        _, pairs, send, recv, srcs, lands, _ = started[tag]
        srcs, lands = _split_wait(send, recv, srcs, lands, after, pairs, "gather_wait_" + tag)
        return _gather_finish(srcs, lands, "gather_finish_" + tag)

    layer1 = {}

    def mix_weights(l, h):
        if l == 0:
            return got_0a
        layer1.update(zip([n for _, n in names_1], arrived("w1", h)))
        return layer1

    def late_weights(l, o):
        return arrived("w0b", o) if l == 0 else [layer1[n] for n in LATE_BIG]

    reducing, last = {}, {}

    def reduce_start(tag, arrs):
        lands = [jax.ShapeDtypeStruct((PEERS, a.shape[1] // 2, a.shape[2]), a.dtype) for a in arrs]
        reducing[tag] = _split_start(arrs, lands, PEERS * len(arrs), _scatter_pairs, "grad_scatter_start_" + tag, where)
        return zero11(reducing[tag][4])

    def reduce_finish(tag, after):
        send, recv, srcs, lands, _ = reducing[tag]
        srcs, lands = _split_wait(send, recv, srcs, lands, after, _scatter_pairs, "grad_scatter_wait_" + tag)
        return [_sum_own_and_landed(a, q, where, f"grad_sum_{tag}_{i}") for i, (a, q) in enumerate(zip(srcs, lands))]

    def ffn_hook(l, g):
        if l == 1:
            return jnp.zeros((1, 1), F32)
        return reduce_start("g0b", [g["wg"], g["wu"], g["wd"]])

    def out_hook(l, g):
        if l == 1:
            return where
        reduce_start("g0c", [g["wout"].reshape(CHIPS, D // CHIPS, D)])
        return reducing["g0c"][4]

    def layer_hook(l, big, small):
        last[l] = (big, small)
        if l == 1:
            return reduce_start("g1", [big[n] for n, _, _ in BIG])
        return None

    entry = zero11(started["w0b"][6]) + zero11(started["w1"][6])
    loss_part, dx = _step(x.reshape(seq, D), positions.reshape(seq, 1), loss_target.reshape(seq, D), p, entry,
                          mix_weights, late_weights, ffn_hook, out_hook, layer_hook)
    loss = lax.psum(loss_part, ("x", "y", "c"))

    def adamw(n, g0, g1):
        w = view("", n)
        three_d = (DEPTH, -1, w.shape[-1])
        res = _adamw(w.reshape(three_d), g0.reshape(three_d[1:]), g1.reshape(three_d[1:]),
                     view("m_", n).reshape(three_d), view("v_", n).reshape(three_d), "adamw_" + n)
        return [r.reshape(w.shape) for r in res]

    names_rest = [(0, n) for n in EARLY_BIG]
    reduce_start("g0a", [last[0][0][n] for _, n in names_rest] + [_pack_small_grads([last[l][1] for l in range(DEPTH)])])
    token = reducing["g0a"][4]
    early = names_1 + [(0, n) for n in FFN_BIG] + [(0, "w_out")]
    landed = reduce_finish("g1", token) + reduce_finish("g0b", token) + reduce_finish("g0c", token)
    sums = dict(zip(early, _pair_join(landed, "grad_pair_join_early")))
    out = {n: adamw(n, sums[(0, n)], sums[(1, n)]) for n in FFN_BIG}
    late = names_rest + ["small"]
    sums.update(zip(late, _pair_join(reduce_finish("g0a", out["w_down"][1]), "grad_pair_join_late")))
    gsmall = _unpack_small_grads(_all_gather_chips([sums["small"]], "all_gather_small_grads")[0])
    for n in ORDER:
        if n not in out:
            g = [sums[(l, n)] for l in range(DEPTH)] if (0, n) in sums else [gsmall[l][n] for l in range(DEPTH)]
            out[n] = adamw(n, *g)
    undo = lambda n, a: jnp.swapaxes(a, 1, 2) if n in TRANSPOSED else a
    return (loss, dx.reshape(x.shape), *[undo(n, out[n][i]) for i in range(4) for n in ORDER])


def _step(xs, pos, tgt, p, entry, mix_weights, late_weights, ffn_hook, out_hook, layer_hook):
    sps = [_small_operands(p, l) for l in range(DEPTH)]
    sps[0]["g_mix"] = sps[0]["g_mix"] + entry
    tabs = _rope_tables(pos, _rope_inv_freq())
    saved, h = [], xs
    for l in range(DEPTH):
        kw = _kernel_weights(mix_weights(l, h))
        h, sv = _layer_fwd(h, tabs, kw, functools.partial(late_weights, l), sps[l], l)
        saved.append(dict(sv, kw=kw))
    dy, lpart = _loss_grad(h, tgt)
    for l in reversed(range(DEPTH)):
        dy, g = _layer_bwd(dy, saved[l], tabs, saved[l]["kw"], sps[l], l, functools.partial(ffn_hook, l),
                           functools.partial(out_hook, l))
        zero = layer_hook(l, _big_grads(g), _small_grads(g))
        if zero is not None and l > 0:
            sps[l - 1]["g_ffn"] = sps[l - 1]["g_ffn"] + zero
    return 0.5 / D * jnp.sum(lpart), dy
```

```python
import functools
import math

import jax
import jax.numpy as jnp
from jax import lax
from jax.experimental import pallas as pl
from jax.experimental.pallas import tpu as pltpu

F32 = jnp.float32
BF16 = jnp.bfloat16
MESH = pl.DeviceIdType.MESH

D = 1024
HEADS = 4
QK = 96
NOPE = 64
ROPE = 32
VH = 128
HP = 128
QL = 256
KVL = 128
SGU = 256
POOL = 256
CHUNK = 128
HID = 2816
CHIPS = 4
SH = HID // CHIPS
IN_W = 1184
IN_P = 1280
EPS = 1e-6
ROPE_THETA = 10000.0
SCALE = 1.0 / math.sqrt(QK)
LOG2E = 1.4426950408889634
EXP2_C = SCALE * LOG2E
ATT_SPLIT = 2
ATT_WIDE = 4
NEG = -1e30
HALO = 16

LR, B1, B2, ADAM_EPS, WD, STEP = 0.001, 0.9, 0.999, 1e-08, 0.01, 10

VMEM_LIMIT = 56 * 1024 * 1024
LANES = 128
TOKENS = 1024


def _cp(sem, vmem=None):
    return pltpu.CompilerParams(dimension_semantics=sem, vmem_limit_bytes=vmem)


def _res(shape):
    nd = len(shape)
    return pl.BlockSpec(shape, lambda *_: (0,) * nd, pipeline_mode=pl.Buffered(1))


def _acc(shape):
    nd = len(shape)
    return pl.BlockSpec(shape, lambda *_: (0,) * nd)


def _dot(a, b):
    return jnp.dot(a, b, preferred_element_type=F32)


def _dot_nt(a, b):
    return lax.dot_general(a, b, (((1,), (1,)), ((), ())), preferred_element_type=F32)


def _dot_tn(a, b):
    return lax.dot_general(a, b, (((0,), (0,)), ((), ())), preferred_element_type=F32)


def _rms(x, n):
    r = lax.rsqrt(jnp.sum(x * x, axis=-1, keepdims=True) * (1.0 / n) + EPS)
    return x * r, r


def _rms_bwd(xn, r, g, dy, n):
    dn = dy * g
    dx = r * (dn - xn * (jnp.sum(dn * xn, axis=-1, keepdims=True) * (1.0 / n)))
    return dx, jnp.sum(dy * xn, axis=0, keepdims=True)


def _accumulate(ref, val, first):
    @pl.when(first)
    def _():
        ref[...] = val

    @pl.when(jnp.logical_not(first))
    def _():
        ref[...] += val


def _accumulate0(ref, val, first):
    @pl.when(first)
    def _():
        ref[0] = val

    @pl.when(jnp.logical_not(first))
    def _():
        ref[0] += val


def _tile(s, t):
    return min(s, t)


def _row_tile(r, cap):
    if r <= cap:
        return r
    return max(t for t in range(8, cap + 1, 8) if r % t == 0)


def _rope_tables(pos, invf):
    s = pos.shape[0]
    tm = _tile(s, 1024)

    def body(pos_ref, invf_ref, c_ref, sa_ref, sb_ref):
        ang = pos_ref[...].astype(F32) * invf_ref[...]
        c, sn = jnp.cos(ang), jnp.sin(ang)
        lane = lax.broadcasted_iota(jnp.int32, ang.shape, 1)
        first = (lane >= NOPE) & (lane < NOPE + ROPE // 2)
        second = (lane >= NOPE + ROPE // 2) & (lane < QK)
        c_ref[...] = jnp.where(first | second, c, 1.0)
        sa_ref[...] = jnp.where(first, -sn, 0.0)
        sb_ref[...] = jnp.where(second, sn, 0.0)

    out = jax.ShapeDtypeStruct((s, HP), F32)
    return pl.pallas_call(
        body, name="rope_tables", grid=(s // tm,),
        in_specs=[pl.BlockSpec((tm, 1), lambda i: (i, 0)), _acc((1, HP))],
        out_specs=[pl.BlockSpec((tm, HP), lambda i: (i, 0))] * 3,
        out_shape=[out] * 3, compiler_params=_cp(("parallel",)),
    )(pos, invf)


def _rope(x, c, sa, sb):
    return x * c + pltpu.roll(x, HP - ROPE // 2, 1) * sa + pltpu.roll(x, ROPE // 2, 1) * sb


def _rope_t(d, c, sa, sb):
    return d * c + pltpu.roll(d * sa, ROPE // 2, 1) + pltpu.roll(d * sb, HP - ROPE // 2, 1)


def _in_proj_fwd(x, g, w, name):
    s = x.shape[0]
    tm = _tile(s, TOKENS)

    def body(x_ref, g_ref, w_ref, z_ref, h_ref):
        xn, _ = _rms(x_ref[...], D)
        h = (xn * g_ref[...]).astype(BF16)
        h_ref[...] = h
        z_ref[...] = _dot(h, w_ref[...])

    return pl.pallas_call(
        body, name=name, grid=(s // tm,),
        in_specs=[pl.BlockSpec((tm, D), lambda i: (i, 0)), _acc((1, D)), _res((D, IN_P))],
        out_specs=[pl.BlockSpec((tm, IN_P), lambda i: (i, 0)), pl.BlockSpec((tm, D), lambda i: (i, 0))],
        out_shape=[jax.ShapeDtypeStruct((s, IN_P), F32), jax.ShapeDtypeStruct((s, D), BF16)],
        compiler_params=_cp(("parallel",), VMEM_LIMIT),
    )(x, g, w)


def _mla_prep_fwd(z, tabs, gql, gkv, gq, gk, wq, wk, wv, name):
    s = z.shape[0]
    tm = _tile(s, TOKENS)

    def body(ql_ref, kv_ref, kr_ref, c_ref, sa_ref, sb_ref, gql_ref, gkv_ref, gq_ref, gk_ref,
             wq_ref, wk_ref, wv_ref, q_out, k_out, v_out):
        qn = (_rms(ql_ref[...], QL)[0] * gql_ref[...]).astype(BF16)
        kvn = (_rms(kv_ref[...], KVL)[0] * gkv_ref[...]).astype(BF16)
        qraw = _dot(qn, wq_ref[...])
        kraw = _dot(kvn, wk_ref[...])
        vraw = _dot(kvn, wv_ref[...])
        kr = kr_ref[...]
        c, sa, sb = c_ref[...], sa_ref[...], sb_ref[...]
        for h in range(HEADS):
            sl = slice(h * HP, (h + 1) * HP)
            xq = _rms(qraw[:, sl], QK)[0] * gq_ref[...]
            q_out[h] = _rope(xq, c, sa, sb).astype(BF16)
            xk = _rms(kraw[:, sl] + kr, QK)[0] * gk_ref[...]
            k_out[h] = _rope(xk, c, sa, sb).astype(BF16)
            v_out[h] = vraw[:, sl].astype(BF16)

    row = lambda w, j: pl.BlockSpec((tm, w), lambda i: (i, j))
    hspec = pl.BlockSpec((HEADS, tm, HP), lambda i: (0, i, 0))
    hshape = jax.ShapeDtypeStruct((HEADS, s, HP), BF16)
    return pl.pallas_call(
        body, name=name, grid=(s // tm,),
        in_specs=[row(QL, 0), row(KVL, 2), row(HP, 3), row(HP, 0), row(HP, 0), row(HP, 0),
                  _acc((1, QL)), _acc((1, KVL)), _acc((1, HP)), _acc((1, HP)),
                  _acc((QL, HEADS * HP)), _acc((KVL, HEADS * HP)), _acc((KVL, HEADS * HP))],
        out_specs=[hspec] * 3, out_shape=[hshape] * 3,
        compiler_params=_cp(("parallel",)),
    )(z, z, z, *tabs, gql, gkv, gq, gk, wq, wk, wv)


def _causal_mask(s, row0):
    row = lax.broadcasted_iota(jnp.int32, s.shape, 0) + row0
    col = lax.broadcasted_iota(jnp.int32, s.shape, 1)
    return jnp.where(col <= row, s, NEG)


def _attn_fwd(q, k, v, name):
    s = q.shape[1]
    tq = _tile(s, 512)
    wide = ATT_WIDE * tq if s % (ATT_WIDE * tq) == 0 else tq
    rh = tq // ATT_SPLIT

    def body(q_ref, k_ref, v_ref, o_ref, lse_ref):
        i = pl.program_id(1)

        def blk(off, tk, carry, masked):
            keys = [(g + 1) * rh if masked else tq for g in range(ATT_SPLIT)]
            rows = lambda t: pl.ds(pl.multiple_of(off + t * tq, tq), tq)
            score = lambda g, t: _dot_nt(q_ref[0, g * rh:(g + 1) * rh, :], k_ref[0, rows(t), :][:keys[g]])
            state = list(carry)
            scs = {(g, 0): score(g, 0) for g in range(ATT_SPLIT)}
            for t in range(tk // tq):
                if (t + 1) * tq < tk:
                    scs.update({(g, t + 1): score(g, t + 1) for g in range(ATT_SPLIT)})
                vt = v_ref[0, rows(t), :]
                for g, (m, l, acc) in enumerate(state):
                    sc = scs.pop((g, t))
                    if masked:
                        sc = _causal_mask(sc, g * rh)
                    m_new = jnp.maximum(m, jnp.max(sc, axis=-1, keepdims=True))
                    p = jnp.exp2((sc - m_new) * EXP2_C)
                    alpha = jnp.exp2((m - m_new) * EXP2_C)
                    l = alpha * l + jnp.sum(p, axis=-1, keepdims=True)
                    acc = alpha * acc + _dot(p.astype(BF16), vt[:keys[g]])
                    state[g] = (m_new, l, acc)
            return tuple(state)

        one = (jnp.full((rh, 1), NEG, F32), jnp.zeros((rh, 1), F32), jnp.zeros((rh, VH), F32))
        nwide = (i * tq) // wide
        carry = lax.fori_loop(0, nwide, lambda j, c: blk(j * wide, wide, c, False), (one,) * ATT_SPLIT)
        carry = lax.fori_loop(nwide * (wide // tq), i, lambda j, c: blk(j * tq, tq, c, False), carry)
        carry = blk(i * tq, tq, carry, True)
        for g, (m, l, acc) in enumerate(carry):
            o_ref[g * rh:(g + 1) * rh, :] = acc / l
            lse_ref[0, g * rh:(g + 1) * rh, :] = jnp.broadcast_to(m * EXP2_C + jnp.log(l) * LOG2E, (rh, LANES))

    return pl.pallas_call(
        body, name=name, grid=(HEADS, s // tq),
        in_specs=[pl.BlockSpec((1, tq, HP), lambda h, i: (h, i, 0)),
                  pl.BlockSpec((1, s, HP), lambda h, i: (h, 0, 0)),
                  pl.BlockSpec((1, s, HP), lambda h, i: (h, 0, 0))],
        out_specs=[pl.BlockSpec((tq, VH), lambda h, i: (i, h)),
                   pl.BlockSpec((1, tq, LANES), lambda h, i: (h, i, 0))],
        out_shape=[jax.ShapeDtypeStruct((s, HEADS * VH), F32), jax.ShapeDtypeStruct((HEADS, s, LANES), F32)],
        compiler_params=_cp(("parallel", "arbitrary"), VMEM_LIMIT),
    )(q, k, v)


def _lane_group(shape, j):
    return (lax.broadcasted_iota(jnp.int32, shape, 1) + j * LANES) // (POOL // 4)


def _pool_win_fwd(z, name):
    s = z.shape[0]
    ch = _tile(s, 512)
    col0 = (IN_P - POOL) // LANES

    def body(p_ref, m_ref):
        j = pl.program_id(0)

        def chunk(r, _):
            off = pl.multiple_of(r * ch, ch)
            cur = p_ref[pl.ds(off, ch), :]
            hoff = pl.multiple_of(jnp.maximum(off - HALO, 0), 8)
            halo = jnp.where(r > 0, p_ref[pl.ds(hoff, HALO), :], 0.0)
            x = jnp.concatenate([halo, cur], axis=0)
            s2 = x + pltpu.roll(x, 1, 0)
            s4 = s2 + pltpu.roll(s2, 2, 0)
            s8 = s4 + pltpu.roll(s4, 4, 0)
            s16 = s8 + pltpu.roll(s8, 8, 0)
            grp = _lane_group((ch, LANES), j)
            sel = jnp.where(grp == 0, s2[HALO:], jnp.where(grp == 1, s4[HALO:], jnp.where(grp == 2, s8[HALO:], s16[HALO:])))
            t1 = (lax.broadcasted_iota(jnp.int32, (ch, LANES), 0) + off + 1).astype(F32)
            win = jnp.where(grp == 0, 2.0, jnp.where(grp == 1, 4.0, jnp.where(grp == 2, 8.0, 16.0)))
            m_ref[pl.ds(off, ch), :] = sel / jnp.minimum(t1, win) - cur
            return 0

        lax.fori_loop(0, s // ch, chunk, 0)

    return pl.pallas_call(
        body, name=name, grid=(POOL // LANES,),
        in_specs=[pl.BlockSpec((s, LANES), lambda j: (0, col0 + j))],
        out_specs=pl.BlockSpec((s, LANES), lambda j: (0, j)),
        out_shape=jax.ShapeDtypeStruct((s, POOL), F32),
        compiler_params=_cp(("parallel",), VMEM_LIMIT),
    )(z)


def _pool_win_bwd(dm, name):
    s = dm.shape[0]
    ch = _tile(s, 512)
    n = s // ch

    def body(dm_ref, dp_ref):
        j = pl.program_id(0)

        def chunk(r, _):
            off = pl.multiple_of(r * ch, ch)
            grp = _lane_group((ch + HALO, LANES), j)
            win = jnp.where(grp == 0, 2.0, jnp.where(grp == 1, 4.0, jnp.where(grp == 2, 8.0, 16.0)))
            cur = dm_ref[pl.ds(off, ch), :]
            hoff = pl.multiple_of(jnp.minimum(off + ch, s - HALO), 8)
            halo = jnp.where(r < n - 1, dm_ref[pl.ds(hoff, HALO), :], 0.0)
            x = jnp.concatenate([cur, halo], axis=0)
            t1 = (lax.broadcasted_iota(jnp.int32, (ch + HALO, LANES), 0) + off + 1).astype(F32)
            e = x / jnp.minimum(t1, win)
            tot = ch + HALO
            r2 = e + pltpu.roll(e, tot - 1, 0)
            r4 = r2 + pltpu.roll(r2, tot - 2, 0)
            r8 = r4 + pltpu.roll(r4, tot - 4, 0)
            r16 = r8 + pltpu.roll(r8, tot - 8, 0)
            g = grp[:ch]
            sel = jnp.where(g == 0, r2[:ch], jnp.where(g == 1, r4[:ch], jnp.where(g == 2, r8[:ch], r16[:ch])))
            dp_ref[pl.ds(off, ch), :] = (sel - cur).astype(BF16)
            return 0

        lax.fori_loop(0, n, chunk, 0)

    return pl.pallas_call(
        body, name=name, grid=(POOL // LANES,),
        in_specs=[pl.BlockSpec((s, LANES), lambda j: (0, j))],
        out_specs=pl.BlockSpec((s, LANES), lambda j: (0, j)),
        out_shape=jax.ShapeDtypeStruct((s, POOL), BF16),
        compiler_params=_cp(("parallel",), VMEM_LIMIT),
    )(dm)


def _head_mask(h):
    lane = lax.broadcasted_iota(jnp.int32, (CHUNK, SGU), 1)
    return (lane // (SGU // HEADS)) == h


def _tril(upper=False):
    row = lax.broadcasted_iota(jnp.int32, (CHUNK, CHUNK), 0)
    col = lax.broadcasted_iota(jnp.int32, (CHUNK, CHUNK), 1)
    return col >= row if upper else col <= row


def _sgu_gate(vn, wsp, bsp):
    out = []
    for cidx in range(vn.shape[0] // CHUNK):
        vc = vn[cidx * CHUNK:(cidx + 1) * CHUNK]
        zc = bsp
        for h in range(HEADS):
            zc = zc + jnp.where(_head_mask(h), _dot(wsp[h], vc), 0.0)
        out.append(zc)
    return jnp.concatenate(out, axis=0)


def _mix_out_fwd(o, z, m, x, wsp, bsp, wbd, psc, gsv, gout, wout, name):
    s = x.shape[0]
    tm = _tile(s, TOKENS)

    def body(o_ref, uv_ref, m_ref, x_ref, wsp_ref, bsp_ref, wbd_ref, psc_ref, gsv_ref, gout_ref, wout_ref,
             x1_ref, mix_ref):
        g = gout_ref[...]
        an = _rms(o_ref[...], HEADS * VH)[0] * g[:, :512]
        uv = uv_ref[...]
        u, v = uv[:, :SGU], uv[:, SGU:]
        vn = (_rms(v, SGU)[0] * gsv_ref[...]).astype(BF16)
        tri = _tril()
        wsp_m = [jnp.where(tri, wsp_ref[h], 0.0).astype(BF16) for h in range(HEADS)]
        gm = u * _sgu_gate(vn, wsp_m, bsp_ref[...])
        gn = _rms(gm, SGU)[0] * g[:, 512:768]
        po = _dot(m_ref[...].astype(BF16), wbd_ref[...]) * psc_ref[...]
        pn = _rms(po, POOL)[0] * g[:, 768:]
        mix = jnp.concatenate([an, gn, pn], axis=1).astype(BF16)
        mix_ref[...] = mix
        x1_ref[...] = x_ref[...] + _dot(mix, wout_ref[...])

    row = lambda w, j: pl.BlockSpec((tm, w), lambda i: (i, j))
    return pl.pallas_call(
        body, name=name, grid=(s // tm,),
        in_specs=[row(512, 0), row(512, 1), row(POOL, 0), row(D, 0),
                  _acc((HEADS, CHUNK, CHUNK)), _acc((CHUNK, SGU)), _acc((POOL, POOL)), _acc((1, POOL)),
                  _acc((1, SGU)), _acc((1, D)), _res((D, D))],
        out_specs=[row(D, 0), row(D, 0)],
        out_shape=[jax.ShapeDtypeStruct((s, D), F32), jax.ShapeDtypeStruct((s, D), BF16)],
        compiler_params=_cp(("parallel",), VMEM_LIMIT),
    )(o, z, m, x, wsp, bsp, wbd, psc, gsv, gout, wout)


def _ffn_fwd(x1, g, wg, wu, wd, name):
    s = x1.shape[0]
    tm = _tile(s, 256)

    def body(x_ref, g_ref, wg_ref, wu_ref, wd_ref, x2_ref, a_ref, b_ref, h_ref):
        x = x_ref[...]
        h = (_rms(x, D)[0] * g_ref[...]).astype(BF16)
        h_ref[...] = h
        acc = jnp.zeros((tm, D), F32)
        for k in range(CHIPS):
            a = _dot_nt(h, wg_ref[k])
            b = _dot_nt(h, wu_ref[k])
            a_ref[k] = a
            b_ref[k] = b
            acc = acc + _dot((a * jax.nn.sigmoid(a) * b).astype(BF16), wd_ref[k])
        x2_ref[...] = x + acc

    row = lambda w: pl.BlockSpec((tm, w), lambda i: (i, 0))
    hrow = pl.BlockSpec((CHIPS, tm, SH), lambda i: (0, i, 0))
    hshape = jax.ShapeDtypeStruct((CHIPS, s, SH), F32)
    return pl.pallas_call(
        body, name=name, grid=(s // tm,),
        in_specs=[row(D), _acc((1, D)), _res((CHIPS, SH, D)), _res((CHIPS, SH, D)), _res((CHIPS, SH, D))],
        out_specs=[row(D), hrow, hrow, row(D)],
        out_shape=[jax.ShapeDtypeStruct((s, D), F32), hshape, hshape, jax.ShapeDtypeStruct((s, D), BF16)],
        compiler_params=_cp(("parallel",), VMEM_LIMIT),
    )(x1, g, wg, wu, wd)


def _loss_grad(y, tgt):
    s = y.shape[0]
    tm = _tile(s, TOKENS)

    def body(y_ref, t_ref, dy_ref, l_ref):
        e = y_ref[...] - t_ref[...]
        dy_ref[...] = e * (1.0 / D)
        sq = jnp.sum(e * e, axis=0, keepdims=True)
        part = sq[:, :LANES]
        for c in range(1, D // LANES):
            part = part + sq[:, c * LANES:(c + 1) * LANES]
        _accumulate(l_ref, part, pl.program_id(0) == 0)

    row = pl.BlockSpec((tm, D), lambda i: (i, 0))
    return pl.pallas_call(
        body, name="loss_grad", grid=(s // tm,),
        in_specs=[row, row], out_specs=[row, _acc((1, LANES))],
        out_shape=[jax.ShapeDtypeStruct((s, D), F32), jax.ShapeDtypeStruct((1, LANES), F32)],
        compiler_params=_cp(("arbitrary",)),
    )(y, tgt)


def _wgrad(a, b, name):
    s, k = a.shape
    n = b.shape[1]
    half = lambda v: v if v <= 1408 else v // 2
    kb, nb, tt = half(k), half(n), _tile(s, 2048)

    def body(a_ref, b_ref, o_ref):
        _accumulate(o_ref, _dot_tn(a_ref[...].astype(BF16), b_ref[...].astype(BF16)), pl.program_id(2) == 0)

    return pl.pallas_call(
        body, name=name, grid=(k // kb, n // nb, s // tt),
        in_specs=[pl.BlockSpec((tt, kb), lambda i, j, t: (t, i)), pl.BlockSpec((tt, nb), lambda i, j, t: (t, j))],
        out_specs=pl.BlockSpec((kb, nb), lambda i, j, t: (i, j)),
        out_shape=jax.ShapeDtypeStruct((k, n), F32),
        compiler_params=_cp(("parallel", "parallel", "arbitrary"), VMEM_LIMIT),
    )(a, b)


def _wgrad_in(h, dzm, duv, dp, name):
    s = h.shape[0]
    tt = _tile(s, 2048)

    def body(h_ref, a_ref, b_ref, c_ref, o_ref):
        hv = h_ref[...]
        val = jnp.concatenate([_dot_tn(hv, a_ref[...]), _dot_tn(hv, b_ref[...]), _dot_tn(hv, c_ref[...])], axis=1)
        _accumulate(o_ref, val, pl.program_id(0) == 0)

    row = lambda w: pl.BlockSpec((tt, w), lambda t: (t, 0))
    return pl.pallas_call(
        body, name=name, grid=(s // tt,), in_specs=[row(D), row(512), row(512), row(POOL)], out_specs=_acc((D, IN_P)),
        out_shape=jax.ShapeDtypeStruct((D, IN_P), F32), compiler_params=_cp(("arbitrary",), VMEM_LIMIT),
    )(h, dzm, duv, dp)


def _wgrad_rows(a, b, name):
    s, n = a.shape[1:]
    nn = b.shape[1]
    tt = _tile(s, 4096 if b.dtype == BF16 else 2048)

    def body(a_ref, b_ref, o_ref):
        _accumulate0(o_ref, _dot_tn(a_ref[0].astype(BF16), b_ref[...].astype(BF16)), pl.program_id(1) == 0)

    return pl.pallas_call(
        body, name=name, grid=(CHIPS, s // tt),
        in_specs=[pl.BlockSpec((1, tt, n), lambda c, t: (c, t, 0)), pl.BlockSpec((tt, nn), lambda c, t: (t, 0))],
        out_specs=pl.BlockSpec((1, n, nn), lambda c, t: (c, 0, 0)),
        out_shape=jax.ShapeDtypeStruct((CHIPS, n, nn), F32),
        compiler_params=_cp(("parallel", "arbitrary"), VMEM_LIMIT),
    )(a, b)


def _ffn_bwd(dx2, x1, a, b, g, wg, wu, wd, name):
    s = x1.shape[0]
    tm = _tile(s, 256)

    def body(dx2_ref, x_ref, a_ref, b_ref, g_ref, wg_ref, wu_ref, wd_ref,
             dx1_ref, hid_ref, da_ref, db_ref, dyb_ref, dg_ref):
        dx2 = dx2_ref[...]
        dyb = dx2.astype(BF16)
        dyb_ref[...] = dyb
        dh = jnp.zeros((tm, D), F32)
        ahead = _dot_nt(dyb, wd_ref[0])
        for k in range(CHIPS):
            av, bv = a_ref[k], b_ref[k]
            dhid = ahead
            if k + 1 < CHIPS:
                ahead = _dot_nt(dyb, wd_ref[k + 1])
            sig = jax.nn.sigmoid(av)
            sa = av * sig
            hid_ref[k] = (sa * bv).astype(BF16)
            dbv = (dhid * sa).astype(BF16)
            dav = (dhid * bv * (sig * (1.0 + av * (1.0 - sig)))).astype(BF16)
            db_ref[k] = dbv
            da_ref[k] = dav
            dh = dh + _dot(dav, wg_ref[k]) + _dot(dbv, wu_ref[k])
        xn, r = _rms(x_ref[...], D)
        dxr, dg = _rms_bwd(xn, r, g_ref[...], dh, D)
        dx1_ref[...] = dx2 + dxr
        _accumulate(dg_ref, dg, pl.program_id(0) == 0)

    row = lambda w: pl.BlockSpec((tm, w), lambda i: (i, 0))
    hrow = pl.BlockSpec((CHIPS, tm, SH), lambda i: (0, i, 0))
    hid = jax.ShapeDtypeStruct((CHIPS, s, SH), BF16)
    return pl.pallas_call(
        body, name=name, grid=(s // tm,),
        in_specs=[row(D), row(D), hrow, hrow, _acc((1, D)), _res((CHIPS, SH, D)), _res((CHIPS, SH, D)),
                  _res((CHIPS, SH, D))],
        out_specs=[row(D), hrow, hrow, hrow, row(D), _acc((1, D))],
        out_shape=[jax.ShapeDtypeStruct((s, D), F32), hid, hid, hid, jax.ShapeDtypeStruct((s, D), BF16),
                   jax.ShapeDtypeStruct((1, D), F32)],
        compiler_params=_cp(("arbitrary",), VMEM_LIMIT),
    )(dx2, x1, a, b, g, wg, wu, wd)


def _mix_out_bwd(dx1, o, z, m, wsp, bsp, wbd, psc, gsv, gout, wout, name):
    s = dx1.shape[0]
    tm = _tile(s, TOKENS)

    def body(dx1_ref, o_ref, uv_ref, m_ref, wsp_ref, bsp_ref, wbd_ref, psc_ref, gsv_ref, gout_ref, wout_ref,
             do_ref, dl_ref, duv_ref, dm_ref, dgo_ref, dgsv_ref, dpsc_ref, dwsp_ref, dbsp_ref, dwbd_ref):
        first = pl.program_id(0) == 0
        g = gout_ref[...]
        dmix = _dot_nt(dx1_ref[...].astype(BF16), wout_ref[...])
        o = o_ref[...]
        on, ro = _rms(o, HEADS * VH)
        do, dga = _rms_bwd(on, ro, g[:, :512], dmix[:, :512], HEADS * VH)
        for h in range(HEADS):
            sl = slice(h * VH, (h + 1) * VH)
            do_ref[h] = do[:, sl].astype(BF16)
            dl_ref[h] = jnp.broadcast_to(jnp.sum(do[:, sl] * o[:, sl], axis=-1, keepdims=True), (tm, LANES))
        uv = uv_ref[...]
        u, v = uv[:, :SGU], uv[:, SGU:]
        vx, rv = _rms(v, SGU)
        vn = (vx * gsv_ref[...]).astype(BF16)
        tri = _tril()
        wsp_m = [jnp.where(tri, wsp_ref[h], 0.0).astype(BF16) for h in range(HEADS)]
        zc = _sgu_gate(vn, wsp_m, bsp_ref[...])
        gm = u * zc
        gmn, rg = _rms(gm, SGU)
        dgm, dgg = _rms_bwd(gmn, rg, g[:, 512:768], dmix[:, 512:768], SGU)
        du = dgm * zc
        dzc = dgm * u
        dvn_parts = []
        dbsp = jnp.zeros((CHUNK, SGU), F32)
        dwsp = [jnp.zeros((CHUNK, CHUNK), F32) for _ in range(HEADS)]
        for cidx in range(tm // CHUNK):
            rs = slice(cidx * CHUNK, (cidx + 1) * CHUNK)
            dzc_c = dzc[rs]
            dbsp = dbsp + dzc_c
            dzb = dzc_c.astype(BF16)
            vc = vn[rs]
            dvn_c = jnp.zeros((CHUNK, SGU), F32)
            for h in range(HEADS):
                hm = _head_mask(h)
                dvn_c = dvn_c + jnp.where(hm, _dot_tn(wsp_m[h], dzb), 0.0)
                dwsp[h] = dwsp[h] + _dot_nt(jnp.where(hm, dzc_c, 0.0).astype(BF16), vc)
            dvn_parts.append(dvn_c)
        dvn = jnp.concatenate(dvn_parts, axis=0)
        dv, dgsv = _rms_bwd(vx, rv, gsv_ref[...], dvn, SGU)
        duv_ref[...] = jnp.concatenate([du, dv], axis=1).astype(BF16)
        mb = m_ref[...].astype(BF16)
        pw = _dot(mb, wbd_ref[...])
        po = pw * psc_ref[...]
        pon, rp = _rms(po, POOL)
        dpo, dgp = _rms_bwd(pon, rp, g[:, 768:], dmix[:, 768:], POOL)
        dpw = (dpo * psc_ref[...]).astype(BF16)
        dm_ref[...] = _dot_nt(dpw, wbd_ref[...])
        _accumulate(dgo_ref, jnp.concatenate([dga, dgg, dgp], axis=1), first)
        _accumulate(dgsv_ref, dgsv, first)
        _accumulate(dpsc_ref, jnp.sum(dpo * pw, axis=0, keepdims=True), first)
        _accumulate(dbsp_ref, dbsp, first)
        _accumulate(dwbd_ref, _dot_tn(mb, dpw), first)
        for h in range(HEADS):
            val = jnp.where(tri, dwsp[h], 0.0)

            @pl.when(first)
            def _(val=val, h=h):
                dwsp_ref[h] = val

            @pl.when(jnp.logical_not(first))
            def _(val=val, h=h):
                dwsp_ref[h] += val

    row = lambda w, j: pl.BlockSpec((tm, w), lambda i: (i, j))
    hspec = pl.BlockSpec((HEADS, tm, HP), lambda i: (0, i, 0))
    return pl.pallas_call(
        body, name=name, grid=(s // tm,),
        in_specs=[row(D, 0), row(512, 0), row(512, 1), row(POOL, 0),
                  _acc((HEADS, CHUNK, CHUNK)), _acc((CHUNK, SGU)),
                  _acc((POOL, POOL)), _acc((1, POOL)), _acc((1, SGU)), _acc((1, D)), _res((D, D))],
        out_specs=[hspec, hspec, row(512, 0), row(POOL, 0), _acc((1, D)), _acc((1, SGU)), _acc((1, POOL)),
                   _acc((HEADS, CHUNK, CHUNK)), _acc((CHUNK, SGU)), _acc((POOL, POOL))],
        out_shape=[jax.ShapeDtypeStruct((HEADS, s, HP), BF16), jax.ShapeDtypeStruct((HEADS, s, LANES), F32),
                   jax.ShapeDtypeStruct((s, 512), BF16), jax.ShapeDtypeStruct((s, POOL), F32),
                   jax.ShapeDtypeStruct((1, D), F32), jax.ShapeDtypeStruct((1, SGU), F32),
                   jax.ShapeDtypeStruct((1, POOL), F32), jax.ShapeDtypeStruct((HEADS, CHUNK, CHUNK), F32),
                   jax.ShapeDtypeStruct((CHUNK, SGU), F32), jax.ShapeDtypeStruct((POOL, POOL), F32)],
        compiler_params=_cp(("arbitrary",), VMEM_LIMIT),
    )(dx1, o, z, m, wsp, bsp, wbd, psc, gsv, gout, wout)


def _attn_bwd(q, k, v, do, lse, delta, after, name):
    s = q.shape[1]
    tq = tk = _tile(s, 512)
    nq = s // tq
    wide = ATT_WIDE * tq if s % (ATT_WIDE * tq) == 0 else tq

    def body(q_ref, k_ref, v_ref, do_ref, lse_ref, dl_ref, after_ref, dq_ref, dk_ref, dv_ref):
        del after_ref
        j = pl.program_id(1)

        @pl.when(j == 0)
        def _():
            dq_ref[...] = jnp.zeros_like(dq_ref)

        kj, vj = k_ref[0], v_ref[0]
        rh = tq // ATT_SPLIT

        def blk(start, rows, dk, dv, masked):
            offs = [pl.multiple_of(start + g * rh, rh) for g in range(rows // rh)]
            qs = [q_ref[0, pl.ds(off, rh), :] for off in offs]
            dos = [do_ref[0, pl.ds(off, rh), :] for off in offs]
            scs = [_dot_nt(qi, kj) for qi in qs]
            dps = [_dot_nt(doi, vj) for doi in dos]
            for g, off in enumerate(offs):
                lse_i = lse_ref[0, pl.ds(off, rh), :][:, :1]
                dl_i = dl_ref[0, pl.ds(off, rh), :][:, :1]
                sc = _causal_mask(scs[g], g * rh) if masked else scs[g]
                p = jnp.exp2(sc * EXP2_C - lse_i)
                ds = (p * (dps[g] - dl_i)).astype(BF16)
                dv = dv + _dot_tn(p.astype(BF16), dos[g])
                dk = dk + _dot_tn(ds, qs[g])
                dq_ref[0, pl.ds(off, rh), :] += _dot(ds, kj) * SCALE
            return dk, dv

        per = wide // tq
        zero = jnp.zeros((tk, HP), F32)
        dk, dv = blk(j * tq, tq, zero, zero, True)
        first_wide = (j + per) // per
        dk, dv = lax.fori_loop(j + 1, jnp.minimum(first_wide * per, nq), lambda i, c: blk(i * tq, tq, *c, False), (dk, dv))
        dk, dv = lax.fori_loop(first_wide, nq // per, lambda i, c: blk(i * wide, wide, *c, False), (dk, dv))
        dk_ref[0] = dk * SCALE
        dv_ref[0] = dv

    full = lambda: pl.BlockSpec((1, s, HP), lambda h, j: (h, 0, 0))
    blk_spec = lambda: pl.BlockSpec((1, tk, HP), lambda h, j: (h, j, 0))
    out = jax.ShapeDtypeStruct((HEADS, s, HP), F32)
    return pl.pallas_call(
        body, name=name, grid=(HEADS, s // tk),
        in_specs=[full(), blk_spec(), blk_spec(), full(), full(), full(), ANY],
        out_specs=[full(), blk_spec(), blk_spec()], out_shape=[out] * 3,
        compiler_params=_cp(("parallel", "arbitrary"), VMEM_LIMIT),
    )(q, k, v, do, lse, delta, after)


def _mla_prep_bwd(dq, dk, dv, z, tabs, gql, gkv, gq, gk, wq, wk, wv, name):
    s = z.shape[0]
    tm = _tile(s, TOKENS)

    def body(dq_ref, dk_ref, dv_ref, ql_ref, kv_ref, kr_ref, c_ref, sa_ref, sb_ref, gql_ref, gkv_ref, gq_ref, gk_ref,
             wq_ref, wk_ref, wv_ref,
             dz_ref, qn_ref, kvn_ref, dqr_ref, dkr_ref, dvr_ref, dgql_ref, dgkv_ref, dgq_ref, dgk_ref):
        first = pl.program_id(0) == 0
        qx, rq = _rms(ql_ref[...], QL)
        qn = (qx * gql_ref[...]).astype(BF16)
        kx, rk = _rms(kv_ref[...], KVL)
        kvn = (kx * gkv_ref[...]).astype(BF16)
        qn_ref[...] = qn
        kvn_ref[...] = kvn
        qraw = _dot(qn, wq_ref[...])
        kraw = _dot(kvn, wk_ref[...])
        kr = kr_ref[...]
        c, sa, sb = c_ref[...], sa_ref[...], sb_ref[...]
        lane = lax.broadcasted_iota(jnp.int32, (tm, HP), 1)
        rope_lanes = (lane >= NOPE) & (lane < QK)
        dkrope = jnp.zeros((tm, HP), F32)
        dgq = jnp.zeros((1, HP), F32)
        dgk = jnp.zeros((1, HP), F32)
        for h in range(HEADS):
            sl = slice(h * HP, (h + 1) * HP)
            xn, r = _rms(qraw[:, sl], QK)
            dx, dg = _rms_bwd(xn, r, gq_ref[...], _rope_t(dq_ref[h], c, sa, sb), QK)
            dqr_ref[:, sl] = dx.astype(BF16)
            dgq = dgq + dg
            xn, r = _rms(kraw[:, sl] + kr, QK)
            dx, dg = _rms_bwd(xn, r, gk_ref[...], _rope_t(dk_ref[h], c, sa, sb), QK)
            dkr_ref[:, sl] = dx.astype(BF16)
            dgk = dgk + dg
            dkrope = dkrope + jnp.where(rope_lanes, dx, 0.0)
            dvr_ref[:, sl] = dv_ref[h].astype(BF16)
        dqn = _dot_nt(dqr_ref[...], wq_ref[...])
        dql, dgql = _rms_bwd(qx, rq, gql_ref[...], dqn, QL)
        dkvn = _dot_nt(dkr_ref[...], wk_ref[...]) + _dot_nt(dvr_ref[...], wv_ref[...])
        dkv, dgkv = _rms_bwd(kx, rk, gkv_ref[...], dkvn, KVL)
        dz_ref[...] = jnp.concatenate([dql, dkv, dkrope], axis=1).astype(BF16)
        _accumulate(dgql_ref, dgql, first)
        _accumulate(dgkv_ref, dgkv, first)
        _accumulate(dgq_ref, dgq, first)
        _accumulate(dgk_ref, dgk, first)

    row = lambda w, j: pl.BlockSpec((tm, w), lambda i: (i, j))
    hspec = pl.BlockSpec((HEADS, tm, HP), lambda i: (0, i, 0))
    sd = lambda w, dt: jax.ShapeDtypeStruct((s, w), dt)
    return pl.pallas_call(
        body, name=name, grid=(s // tm,),
        in_specs=[hspec, hspec, hspec, row(QL, 0), row(KVL, 2), row(HP, 3), row(HP, 0), row(HP, 0), row(HP, 0),
                  _acc((1, QL)), _acc((1, KVL)), _acc((1, HP)), _acc((1, HP)),
                  _acc((QL, HEADS * HP)), _acc((KVL, HEADS * HP)), _acc((KVL, HEADS * HP))],
        out_specs=[row(512, 0), row(QL, 0), row(KVL, 0), row(512, 0), row(512, 0), row(512, 0),
                   _acc((1, QL)), _acc((1, KVL)), _acc((1, HP)), _acc((1, HP))],
        out_shape=[sd(512, BF16), sd(QL, BF16), sd(KVL, BF16), sd(512, BF16), sd(512, BF16), sd(512, BF16),
                   jax.ShapeDtypeStruct((1, QL), F32), jax.ShapeDtypeStruct((1, KVL), F32),
                   jax.ShapeDtypeStruct((1, HP), F32), jax.ShapeDtypeStruct((1, HP), F32)],
        compiler_params=_cp(("arbitrary",), VMEM_LIMIT),
    )(dq, dk, dv, z, z, z, *tabs, gql, gkv, gq, gk, wq, wk, wv)


def _in_proj_bwd(dzm, duv, dp, x, dx1, g, win, name):
    s = x.shape[0]
    tm = _tile(s, TOKENS // 2)

    def body(dzm_ref, duv_ref, dp_ref, x_ref, dx1_ref, g_ref, w_ref, dx_ref, dg_ref):
        groups = [slice(r0, r0 + tm // 2) for r0 in (0, tm // 2)]
        dhs = [_dot_nt(dzm_ref[rs, :], w_ref[:, 0:512]) + _dot_nt(duv_ref[rs, :], w_ref[:, 512:1024])
               + _dot_nt(dp_ref[rs, :], w_ref[:, 1024:IN_P]) for rs in groups]
        dg = jnp.zeros((1, D), F32)
        for rs, dh in zip(groups, dhs):
            xn, r = _rms(x_ref[rs, :], D)
            dxr, dgr = _rms_bwd(xn, r, g_ref[...], dh, D)
            dx_ref[rs, :] = dx1_ref[rs, :] + dxr
            dg = dg + dgr
        _accumulate(dg_ref, dg, pl.program_id(0) == 0)

    row = lambda w: pl.BlockSpec((tm, w), lambda i: (i, 0))
    return pl.pallas_call(
        body, name=name, grid=(s // tm,),
        in_specs=[row(512), row(512), row(POOL), row(D), row(D), _acc((1, D)), _res((D, IN_P))],
        out_specs=[row(D), _acc((1, D))],
        out_shape=[jax.ShapeDtypeStruct((s, D), F32), jax.ShapeDtypeStruct((1, D), F32)],
        compiler_params=_cp(("arbitrary",), VMEM_LIMIT),
    )(dzm, duv, dp, x, dx1, g, win)


def _adamw(w, g0, g1, m, v, name):
    _, r, c = w.shape
    tr = _row_tile(r, 512)
    c1 = 1.0 - B1 ** STEP
    c2 = 1.0 - B2 ** STEP

    def body(w_ref, g0_ref, g1_ref, m_ref, v_ref, g_ref, d_ref, nm_ref, nv_ref):
        gv = jnp.where(pl.program_id(0) == 0, g0_ref[...], g1_ref[...])
        g_ref[0] = gv
        nm = B1 * m_ref[0] + (1.0 - B1) * gv
        nv = B2 * v_ref[0] + (1.0 - B2) * (gv * gv)
        nm_ref[0] = nm
        nv_ref[0] = nv
        d_ref[0] = -LR * ((nm / c1) / (jnp.sqrt(nv / c2) + ADAM_EPS) + WD * w_ref[0])

    spec = pl.BlockSpec((1, tr, c), lambda l, i: (l, i, 0))
    out = jax.ShapeDtypeStruct((DEPTH, r, c), F32)
    return pl.pallas_call(
        body, name=name, grid=(DEPTH, r // tr),
        in_specs=[spec, pl.BlockSpec((tr, c), lambda l, i: (i * (1 - l), 0)), pl.BlockSpec((tr, c), lambda l, i: (i * l, 0)),
                  spec, spec],
        out_specs=[spec] * 4, out_shape=[out] * 4, compiler_params=_cp(("parallel", "parallel")),
    )(w, g0, g1, m, v)


ANY = pl.BlockSpec(memory_space=pl.ANY)


def _place():
    x, y, c = lax.axis_index("x"), lax.axis_index("y"), lax.axis_index("c")
    chips = [(1 - x, y), (x, 1 - y), (1 - x, 1 - y)]
    return x, y, c, chips


def _half_rows(ref, lead, hh, half, align):
    rows = pl.ds(pl.multiple_of(hh * half, align), half)
    return ref.at[rows, :] if lead is None else ref.at[lead, rows, :]


def _row_align(dtype):
    return 16 if dtype == BF16 else 8


def _sems(n):
    return [pltpu.SemaphoreType.DMA((n,)), pltpu.SemaphoreType.DMA((n,)), pltpu.SemaphoreType.DMA((n,))]


def _comm_call(body, ins, out_shapes, nsems, name):
    return pl.pallas_call(
        body, name=name, in_specs=[ANY] * len(ins), out_specs=[ANY] * len(out_shapes), out_shape=out_shapes,
        scratch_shapes=_sems(nsems), compiler_params=pltpu.CompilerParams(has_side_effects=True),
    )(*ins)


def _all_gather_chips(shards, name):
    n = len(shards)
    halves = [a.shape[0] // 2 for a in shards]
    aligns = [_row_align(a.dtype) for a in shards]
    assert all(h % al == 0 for h, al in zip(halves, aligns))

    def body(*refs):
        ins, outs, (send_sems, recv_sems, _) = refs[:n], refs[n:2 * n], refs[2 * n:]
        x, y, c, chips = _place()
        me = 2 * x + y
        sibling = (x, y, 1 - c)

        def copy(sem, src, dst, to):
            return pltpu.make_async_remote_copy(src_ref=src, dst_ref=dst, send_sem=send_sems.at[sem],
                                                recv_sem=recv_sems.at[sem], device_id=to, device_id_type=MESH)

        first, passed = [], []
        for a in range(n):
            my_half = _half_rows(ins[a], None, c, halves[a], aligns[a])
            for j, (cx, cy) in enumerate(chips):
                cp = copy(6 * a + j, my_half, _half_rows(outs[a], me, c, halves[a], aligns[a]), (cx, cy, c))
                cp.start()
                first.append(cp)
        for a in range(n):
            for j, (cx, cy) in enumerate(chips):
                landed = _half_rows(outs[a], 2 * cx + cy, c, halves[a], aligns[a])
                copy(6 * a + j, landed, landed, (cx, cy, c)).wait_recv()
                fwd = copy(6 * a + 3 + j, landed, landed, sibling)
                fwd.start()
                passed.append(fwd)
        for a in range(n):
            for j, (cx, cy) in enumerate(chips):
                other = _half_rows(outs[a], 2 * cx + cy, 1 - c, halves[a], aligns[a])
                copy(6 * a + 3 + j, other, other, sibling).wait_recv()
        for cp in first + passed:
            cp.wait_send()

    lands = _comm_call(body, shards, [jax.ShapeDtypeStruct((CHIPS,) + a.shape, a.dtype) for a in shards], 6 * n, name)
    return _with_own(lands, shards)


def _with_own(lands, shards):
    me = 2 * lax.axis_index("x") + lax.axis_index("y")
    return [lax.dynamic_update_slice(g, a[None], (me, 0, 0)) for g, a in zip(lands, shards)]


def _pair_join(arrs, name):
    n = len(arrs)
    halves = [a.shape[0] // 2 for a in arrs]

    def body(*refs):
        outs, (send_sems, recv_sems, _) = refs[n:2 * n], refs[2 * n:]
        x, y, c, _ = _place()
        cps = []
        for a in range(n):
            mine = _half_rows(outs[a], None, c, halves[a], 8)
            cp = pltpu.make_async_remote_copy(src_ref=mine, dst_ref=mine, send_sem=send_sems.at[a], recv_sem=recv_sems.at[a],
                                              device_id=(x, y, 1 - c), device_id_type=MESH)
            cp.start()
            cps.append(cp)
        for cp in cps:
            cp.wait()

    return pl.pallas_call(
        body, name=name, in_specs=[ANY] * n, out_specs=[ANY] * n,
        out_shape=[jax.ShapeDtypeStruct(a.shape, a.dtype) for a in arrs],
        input_output_aliases={i: i for i in range(n)}, scratch_shapes=_sems(n),
        compiler_params=pltpu.CompilerParams(has_side_effects=True),
    )(*arrs)


HBM = pl.BlockSpec(memory_space=pltpu.HBM)
SEM = pl.BlockSpec(memory_space=pltpu.SEMAPHORE)
DATAFLOW = pltpu.SideEffectType.DATAFLOW_SIDE_EFFECTING


def _remote_copies(pairs, ins, lands, send_sems, recv_sems):
    return [pltpu.make_async_remote_copy(src_ref=src, dst_ref=dst, send_sem=send_sems.at[i], recv_sem=recv_sems.at[i],
                                         device_id=to, device_id_type=MESH)
            for i, (src, dst, to) in enumerate(pairs(ins, lands))]


def _split_start(srcs, land_shapes, ncopies, pairs, name, after):
    n, m = len(srcs), len(land_shapes)

    def body(*refs):
        ins, lands = refs[:n], refs[n:n + m]
        send_sems, recv_sems, token = refs[n + m + 1], refs[n + m + 2], refs[-1]
        for cp in _remote_copies(pairs, ins, lands, send_sems, recv_sems):
            cp.start()
        token[...] = jnp.zeros_like(token)

    hbm = lambda a: pltpu.with_memory_space_constraint(a, pltpu.HBM)
    lands = [hbm(lax.empty(s.shape, s.dtype)) for s in land_shapes]
    thru = [pltpu.HBM(a.shape, a.dtype) for a in list(srcs) + lands]
    out = pl.pallas_call(
        body, name=name,
        out_shape=(pltpu.SemaphoreType.DMA((ncopies,)), pltpu.SemaphoreType.DMA((ncopies,)), *thru,
                   jax.ShapeDtypeStruct((8, LANES), F32)),
        in_specs=[HBM] * (n + m) + [ANY], out_specs=(SEM, SEM, *[HBM] * (n + m), pl.BlockSpec(memory_space=pltpu.VMEM)),
        input_output_aliases={i: 2 + i for i in range(n + m)},
        compiler_params=pltpu.CompilerParams(has_side_effects=DATAFLOW),
    )(*[hbm(a) for a in srcs], *lands, after)
    return out[0], out[1], list(out[2:2 + n]), list(out[2 + n:2 + n + m]), out[-1]


def _split_wait(send_sems, recv_sems, srcs, lands, after, pairs, name):
    n, m = len(srcs), len(lands)

    def body(*refs):
        ins, lands_ = refs[:n], refs[n:n + m]
        for cp in _remote_copies(pairs, ins, lands_, refs[n + m], refs[n + m + 1]):
            cp.wait_send()
            cp.wait_recv()

    out = pl.pallas_call(
        body, name=name, out_shape=tuple(pltpu.HBM(a.shape, a.dtype) for a in list(srcs) + list(lands)),
        in_specs=[HBM] * (n + m) + [SEM, SEM, ANY], out_specs=tuple([HBM] * (n + m)),
        input_output_aliases={i: i for i in range(n + m)},
        compiler_params=pltpu.CompilerParams(has_side_effects=DATAFLOW),
    )(*srcs, *lands, send_sems, recv_sems, after)
    return list(out[:n]), list(out[n:])


def _gather_pairs(halves, aligns):
    def pairs(ins, lands):
        x, y, c, chips = _place()
        me = 2 * x + y
        return [(_half_rows(ins[a], None, c, halves[a], aligns[a]), _half_rows(lands[a], me, c, halves[a], aligns[a]),
                 (cx, cy, c)) for a in range(len(ins)) for cx, cy in chips]
    return pairs


PEERS = 7


def _scatter_pairs(ins, lands):
    x, y, c, chips = _place()
    to = [(cx, cy, c) for cx, cy in chips] + [(cx, cy, 1 - c) for cx, cy in chips] + [(x, y, 1 - c)]
    out = []
    for a in range(len(ins)):
        half = ins[a].shape[1] // 2
        for i, (tx, ty, tc) in enumerate(to):
            out.append((_half_rows(ins[a], 2 * tx + ty, tc, half, 8), lands[a].at[i], (tx, ty, tc)))
    return out


def _gather_finish(shards, lands, name):
    n = len(shards)
    halves = [a.shape[0] // 2 for a in shards]
    aligns = [_row_align(a.dtype) for a in shards]

    def body(*refs):
        outs, (send_sems, recv_sems, _) = refs[n:2 * n], refs[2 * n:]
        x, y, c, chips = _place()
        passed = []
        for a in range(n):
            for j, (cx, cy) in enumerate(chips):
                landed = _half_rows(outs[a], 2 * cx + cy, c, halves[a], aligns[a])
                cp = pltpu.make_async_remote_copy(src_ref=landed, dst_ref=landed, send_sem=send_sems.at[3 * a + j],
                                                  recv_sem=recv_sems.at[3 * a + j], device_id=(x, y, 1 - c),
                                                  device_id_type=MESH)
                cp.start()
                passed.append(cp)
        for a in range(n):
            for j, (cx, cy) in enumerate(chips):
                other = _half_rows(outs[a], 2 * cx + cy, 1 - c, halves[a], aligns[a])
                pltpu.make_async_remote_copy(src_ref=other, dst_ref=other, send_sem=send_sems.at[3 * a + j],
                                             recv_sem=recv_sems.at[3 * a + j], device_id=(x, y, 1 - c),
                                             device_id_type=MESH).wait_recv()
        for cp in passed:
            cp.wait_send()

    lands = pl.pallas_call(
        body, name=name, in_specs=[ANY] * n, out_specs=[ANY] * n,
        out_shape=[jax.ShapeDtypeStruct(a.shape, a.dtype) for a in lands],
        input_output_aliases={i: i for i in range(n)}, scratch_shapes=_sems(3 * n),
        compiler_params=pltpu.CompilerParams(has_side_effects=True),
    )(*lands)
    return _with_own(lands, shards)


def _sum_own_and_landed(own, landed, where, name):
    _, half, cols = landed.shape
    tr = _row_tile(half, 128)
    nt = half // tr

    grid_spec = pltpu.PrefetchScalarGridSpec(
        num_scalar_prefetch=1, grid=(nt,),
        in_specs=[pl.BlockSpec((1, tr, cols), lambda r, w: (w[0], w[1] * nt + r, 0)),
                  pl.BlockSpec((PEERS, tr, cols), lambda r, w: (0, r, 0))],
        out_specs=pl.BlockSpec((tr, cols), lambda r, w: (w[1] * nt + r, 0)))

    def body(w_ref, p_ref, q_ref, o_ref):
        acc = p_ref[0]
        for i in range(PEERS):
            acc = acc + q_ref[i]
        o_ref[...] = acc

    return pl.pallas_call(
        body, name=name, grid_spec=grid_spec, out_shape=jax.ShapeDtypeStruct((2 * half, cols), own.dtype),
        compiler_params=_cp(("parallel",)),
    )(where, own, landed)


BIG = [("w_in", (D, IN_W), 1), ("w_q_up", (QL, HEADS * QK), 1), ("w_kv_up", (KVL, HEADS * (NOPE + VH)), 1),
       ("w_out", (D, D), 0), ("w_gate", (D, HID), 1), ("w_up", (D, HID), 1), ("w_down", (HID, D), 0)]
SMALL = [("g_mix_norm", (D,)), ("g_q_lat", (QL,)), ("g_kv_lat", (KVL,)), ("g_q_head", (QK,)), ("g_k_head", (QK,)),
         ("g_sgu_v", (SGU,)), ("w_spatial", (HEADS, CHUNK, CHUNK)), ("b_spatial", (HEADS, CHUNK)),
         ("w_pool", (4, 64, 64)), ("pool_scale", (POOL,)), ("g_out_mla", (512,)), ("g_out_sgu", (SGU,)),
         ("g_out_pool", (POOL,)), ("g_ffn_norm", (D,))]
ORDER = ["g_mix_norm", "w_in", "g_q_lat", "w_q_up", "g_kv_lat", "w_kv_up", "g_q_head", "g_k_head", "g_sgu_v",
         "w_spatial", "b_spatial", "w_pool", "pool_scale", "g_out_mla", "g_out_sgu", "g_out_pool", "w_out",
         "g_ffn_norm", "w_gate", "w_up", "w_down"]
EARLY_BIG = ["w_in", "w_q_up", "w_kv_up"]
FFN_BIG = ["w_gate", "w_up", "w_down"]
LATE_BIG = ["w_out"] + FFN_BIG
DEPTH = 2
COLS = 1024
SMALL_N = sum(math.prod(s) for _, s in SMALL) * DEPTH
assert SMALL_N % CHIPS == 0
SMALL_ROWS = -(-(SMALL_N // CHIPS) // (16 * COLS)) * 16


def _unsplit_cols(g):
    return g.transpose(1, 0, 2).reshape(g.shape[1], CHIPS * g.shape[2])


def _split_cols(full):
    r, c = full.shape
    return full.reshape(r, CHIPS, c // CHIPS).transpose(1, 0, 2)


def _kernel_weights(g):
    win = _unsplit_cols(g["w_in"])
    zeros = lambda r, c: jnp.zeros((r, c), BF16)
    o2, o3, o4 = QL + KVL, QL + KVL + ROPE, QL + KVL + ROPE + 2 * SGU
    win_p = jnp.concatenate([win[:, :o2], zeros(D, NOPE), win[:, o2:o3], zeros(D, HP - QK), win[:, o3:o4], win[:, o4:]], axis=1)
    wq = _unsplit_cols(g["w_q_up"]).reshape(QL, HEADS, QK)
    wq_p = jnp.pad(wq, ((0, 0), (0, 0), (0, HP - QK))).reshape(QL, HEADS * HP)
    wkv = _unsplit_cols(g["w_kv_up"]).reshape(KVL, HEADS, NOPE + VH)
    wk_p = jnp.pad(wkv[:, :, :NOPE], ((0, 0), (0, 0), (0, HP - NOPE))).reshape(KVL, HEADS * HP)
    wv_p = wkv[:, :, NOPE:].reshape(KVL, HEADS * VH)
    return dict(win=win_p, wq=wq_p, wk=wk_p, wv=wv_p)


def _small_operands(p, l):
    row = lambda v: v.reshape(1, -1)
    pad = lambda v: jnp.pad(v, (0, HP - QK)).reshape(1, HP)
    wpool = p["w_pool"][l]
    wbd = jnp.zeros((POOL, POOL), F32)
    for g in range(4):
        wbd = lax.dynamic_update_slice(wbd, wpool[g], (g * 64, g * 64))
    return dict(
        g_mix=row(p["g_mix_norm"][l]), gql=row(p["g_q_lat"][l]), gkv=row(p["g_kv_lat"][l]),
        gq=pad(p["g_q_head"][l]), gk=pad(p["g_k_head"][l]), gsv=row(p["g_sgu_v"][l]),
        wsp=p["w_spatial"][l], bsp=jnp.repeat(p["b_spatial"][l].T, SGU // HEADS, axis=1),
        wbd=wbd.astype(BF16), psc=row(p["pool_scale"][l]),
        gout=jnp.concatenate([p["g_out_mla"][l], p["g_out_sgu"][l], p["g_out_pool"][l]]).reshape(1, D),
        g_ffn=row(p["g_ffn_norm"][l]))


def _big_grads(g):
    dwin = g["win"]
    o2 = QL + KVL
    gin = jnp.concatenate([dwin[:, :o2], dwin[:, o2 + NOPE:o2 + NOPE + ROPE], dwin[:, 512:]], axis=1)
    gq = g["wq"].reshape(QL, HEADS, HP)[:, :, :QK].reshape(QL, HEADS * QK)
    gk = g["wk"].reshape(KVL, HEADS, HP)[:, :, :NOPE]
    gv = g["wv"].reshape(KVL, HEADS, VH)
    gkv = jnp.concatenate([gk, gv], axis=2).reshape(KVL, HEADS * (NOPE + VH))
    return {"w_in": _split_cols(gin), "w_q_up": _split_cols(gq), "w_kv_up": _split_cols(gkv),
            "w_out": g["wout"].reshape(CHIPS, D // CHIPS, D), "w_gate": g["wg"], "w_up": g["wu"], "w_down": g["wd"]}


TRANSPOSED = ("w_gate", "w_up")


def _small_grads(g):
    go = g["gout"].reshape(-1)
    return {"g_mix_norm": g["g_mix"].reshape(-1), "g_q_lat": g["gql"].reshape(-1), "g_kv_lat": g["gkv"].reshape(-1),
            "g_q_head": g["gq"].reshape(-1)[:QK], "g_k_head": g["gk"].reshape(-1)[:QK], "g_sgu_v": g["gsv"].reshape(-1),
            "w_spatial": g["wsp"], "b_spatial": g["bsp"].reshape(CHUNK, HEADS, SGU // HEADS).sum(-1).T,
            "w_pool": jnp.stack([g["wbd"][i * 64:(i + 1) * 64, i * 64:(i + 1) * 64] for i in range(4)]),
            "pool_scale": g["psc"].reshape(-1), "g_out_mla": go[:512], "g_out_sgu": go[512:768],
            "g_out_pool": go[768:], "g_ffn_norm": g["g_ffn"].reshape(-1)}


def _pack_small_grads(small):
    sm = jnp.concatenate([small[l][n].reshape(-1) for l in range(DEPTH) for n, _ in SMALL]).reshape(CHIPS, SMALL_N // CHIPS)
    return jnp.pad(sm, ((0, 0), (0, SMALL_ROWS * COLS - SMALL_N // CHIPS))).reshape(CHIPS, SMALL_ROWS, COLS)


def _unpack_small_grads(gathered):
    flat = gathered.reshape(CHIPS, SMALL_ROWS * COLS)[:, :SMALL_N // CHIPS].reshape(-1)
    out, off = [], 0
    for _ in range(DEPTH):
        layer = {}
        for n, shape in SMALL:
            k = math.prod(shape)
            layer[n] = flat[off:off + k].reshape(shape)
            off += k
        out.append(layer)
    return out


def _layer_fwd(x, tabs, kw, late_weights, sp, l):
    t = f"_l{l}"
    z, hb = _in_proj_fwd(x, sp["g_mix"], kw["win"], "in_proj_fwd" + t)
    q, k, v = _mla_prep_fwd(z, tabs, sp["gql"], sp["gkv"], sp["gq"], sp["gk"], kw["wq"], kw["wk"], kw["wv"],
                            "mla_prep_fwd" + t)
    o, lse = _attn_fwd(q, k, v, "attn_fwd" + t)
    m = _pool_win_fwd(z, "pool_win_fwd" + t)
    wout, wg, wu, wd = late_weights(o)
    wout = wout.reshape(D, D)
    x1, mix = _mix_out_fwd(o, z, m, x, sp["wsp"], sp["bsp"], sp["wbd"], sp["psc"], sp["gsv"], sp["gout"], wout,
                           "mix_out_fwd" + t)
    x2, a, b, h2 = _ffn_fwd(x1, sp["g_ffn"], wg, wu, wd, "ffn_fwd" + t)
    saved = dict(x=x, z=z, hb=hb, q=q, k=k, v=v, o=o, lse=lse, m=m, x1=x1, mix=mix, a=a, b=b, h2=h2, wg=wg, wu=wu, wd=wd,
                 wout=wout)
    return x2, saved


def _layer_bwd(dx2, sv, tabs, kw, sp, l, ffn_hook, out_hook):
    t = f"_l{l}"
    g = {}
    dx1, hid, da, db, dyb, g["g_ffn"] = _ffn_bwd(dx2, sv["x1"], sv["a"], sv["b"], sp["g_ffn"], sv["wg"], sv["wu"],
                                                 sv["wd"], "ffn_bwd" + t)
    g["wd"] = _wgrad_rows(hid, dyb, "wgrad_down" + t)
    g["wg"] = _wgrad_rows(da, sv["h2"], "wgrad_gate" + t)
    g["wu"] = _wgrad_rows(db, sv["h2"], "wgrad_up" + t)
    gout = sp["gout"] + ffn_hook(g)
    do, delta, duv, dm, g["gout"], g["gsv"], g["psc"], g["wsp"], g["bsp"], g["wbd"] = _mix_out_bwd(
        dx1, sv["o"], sv["z"], sv["m"], sp["wsp"], sp["bsp"], sp["wbd"], sp["psc"], sp["gsv"], gout, sv["wout"],
        "mix_out_bwd" + t)
    g["wout"] = _wgrad(sv["mix"], dx1, "wgrad_out" + t)
    dp = _pool_win_bwd(dm, "pool_win_bwd" + t)
    dq, dk, dv = _attn_bwd(sv["q"], sv["k"], sv["v"], do, sv["lse"], delta, out_hook(g), "attn_bwd" + t)
    dzm, qn, kvn, dqr, dkr, dvr, g["gql"], g["gkv"], g["gq"], g["gk"] = _mla_prep_bwd(
        dq, dk, dv, sv["z"], tabs, sp["gql"], sp["gkv"], sp["gq"], sp["gk"], kw["wq"], kw["wk"], kw["wv"],
        "mla_prep_bwd" + t)
    g["wq"] = _wgrad(qn, dqr, "wgrad_q_up" + t)
    g["wk"] = _wgrad(kvn, dkr, "wgrad_k_up" + t)
    g["wv"] = _wgrad(kvn, dvr, "wgrad_v_up" + t)
    dx, g["g_mix"] = _in_proj_bwd(dzm, duv, dp, sv["x"], dx1, sp["g_mix"], kw["win"], "in_proj_bwd" + t)
    g["win"] = _wgrad_in(sv["hb"], dzm, duv, dp, "wgrad_in" + t)
    return dx, g


def _rope_inv_freq():
    half = ROPE // 2
    inv = 1.0 / (ROPE_THETA ** (jnp.arange(half, dtype=F32) / half))
    return jnp.concatenate([jnp.zeros((NOPE,), F32), inv, inv, jnp.zeros((HP - QK,), F32)]).reshape(1, HP)


def kernel(x, positions, g_mix_norm, w_in, g_q_lat, w_q_up, g_kv_lat, w_kv_up, g_q_head, g_k_head, g_sgu_v, w_spatial, b_spatial, w_pool, pool_scale, g_out_mla, g_out_sgu, g_out_pool, w_out, g_ffn_norm, w_gate, w_up, w_down, loss_target, m_g_mix_norm, m_w_in, m_g_q_lat, m_w_q_up, m_g_kv_lat, m_w_kv_up, m_g_q_head, m_g_k_head, m_g_sgu_v, m_w_spatial, m_b_spatial, m_w_pool, m_pool_scale, m_g_out_mla, m_g_out_sgu, m_g_out_pool, m_w_out, m_g_ffn_norm, m_w_gate, m_w_up, m_w_down, v_g_mix_norm, v_w_in, v_g_q_lat, v_w_q_up, v_g_kv_lat, v_w_kv_up, v_g_q_head, v_g_k_head, v_g_sgu_v, v_w_spatial, v_b_spatial, v_w_pool, v_pool_scale, v_g_out_mla, v_g_out_sgu, v_g_out_pool, v_w_out, v_g_ffn_norm, v_w_gate, v_w_up, v_w_down):
    given = dict(locals())
    p = {n: given[n] for n in ORDER}
    view = lambda pre, n: jnp.swapaxes(given[pre + n], 1, 2) if n in TRANSPOSED else given[pre + n]
    seq = x.shape[1]
    where = jnp.stack([2 * lax.axis_index("x") + lax.axis_index("y"), lax.axis_index("c")]).astype(jnp.int32)
    shards = lambda names: [view("", n)[l].astype(BF16) for l, n in names]
    zero11 = lambda token: token[:1, :1]

    names_0a = [(0, n) for n in EARLY_BIG]
    names_0b = [(0, n) for n in LATE_BIG]
    names_1 = [(1, n) for n, _, _ in BIG]
    got_0a = dict(zip(EARLY_BIG, _all_gather_chips(shards(names_0a), "all_gather_w0a")))
    started, issued = {}, got_0a["w_in"]
    for tag, names in (("w0b", names_0b), ("w1", names_1)):
        sh = shards(names)
        pairs = _gather_pairs([a.shape[0] // 2 for a in sh], [_row_align(a.dtype) for a in sh])
        lands = [jax.ShapeDtypeStruct((CHIPS,) + a.shape, a.dtype) for a in sh]
        started[tag] = (sh, pairs) + _split_start(sh, lands, 3 * len(sh), pairs, "gather_start_" + tag, issued)
        issued = started[tag][6]

    def arrived(tag, after):
        _, pairs, send, recv, srcs, lands, _ = started[tag]
        srcs, lands = _split_wait(send, recv, srcs, lands, after, pairs, "gather_wait_" + tag)
        return _gather_finish(srcs, lands, "gather_finish_" + tag)

    layer1 = {}

    def mix_weights(l, h):
        if l == 0:
            return got_0a
        layer1.update(zip([n for _, n in names_1], arrived("w1", h)))
        return layer1

    def late_weights(l, o):
        return arrived("w0b", o) if l == 0 else [layer1[n] for n in LATE_BIG]

    reducing, last = {}, {}

    def reduce_start(tag, arrs):
        lands = [jax.ShapeDtypeStruct((PEERS, a.shape[1] // 2, a.shape[2]), a.dtype) for a in arrs]
        reducing[tag] = _split_start(arrs, lands, PEERS * len(arrs), _scatter_pairs, "grad_scatter_start_" + tag, where)
        return zero11(reducing[tag][4])

    def reduce_finish(tag, after):
        send, recv, srcs, lands, _ = reducing[tag]
        srcs, lands = _split_wait(send, recv, srcs, lands, after, _scatter_pairs, "grad_scatter_wait_" + tag)
        return [_sum_own_and_landed(a, q, where, f"grad_sum_{tag}_{i}") for i, (a, q) in enumerate(zip(srcs, lands))]

    def ffn_hook(l, g):
        if l == 1:
            return jnp.zeros((1, 1), F32)
        return reduce_start("g0b", [g["wg"], g["wu"], g["wd"]])

    def out_hook(l, g):
        if l == 1:
            return where
        reduce_start("g0c", [g["wout"].reshape(CHIPS, D // CHIPS, D)])
        return reducing["g0c"][4]

    def layer_hook(l, big, small):
        last[l] = (big, small)
        if l == 1:
            return reduce_start("g1", [big[n] for n, _, _ in BIG])
        return None

    entry = zero11(started["w0b"][6]) + zero11(started["w1"][6])
    loss_part, dx = _step(x.reshape(seq, D), positions.reshape(seq, 1), loss_target.reshape(seq, D), p, entry,
                          mix_weights, late_weights, ffn_hook, out_hook, layer_hook)
    loss = lax.psum(loss_part, ("x", "y", "c"))

    def adamw(n, g0, g1):
        w = view("", n)
        three_d = (DEPTH, -1, w.shape[-1])
        res = _adamw(w.reshape(three_d), g0.reshape(three_d[1:]), g1.reshape(three_d[1:]),
                     view("m_", n).reshape(three_d), view("v_", n).reshape(three_d), "adamw_" + n)
        return [r.reshape(w.shape) for r in res]

    names_rest = [(0, n) for n in EARLY_BIG]
    reduce_start("g0a", [last[0][0][n] for _, n in names_rest] + [_pack_small_grads([last[l][1] for l in range(DEPTH)])])
    token = reducing["g0a"][4]
    early = names_1 + [(0, n) for n in FFN_BIG] + [(0, "w_out")]
    landed = reduce_finish("g1", token) + reduce_finish("g0b", token) + reduce_finish("g0c", token)
    sums = dict(zip(early, _pair_join(landed, "grad_pair_join_early")))
    out = {n: adamw(n, sums[(0, n)], sums[(1, n)]) for n in FFN_BIG}
    late = names_rest + ["small"]
    sums.update(zip(late, _pair_join(reduce_finish("g0a", out["w_down"][1]), "grad_pair_join_late")))
    gsmall = _unpack_small_grads(_all_gather_chips([sums["small"]], "all_gather_small_grads")[0])
    for n in ORDER:
        if n not in out:
            g = [sums[(l, n)] for l in range(DEPTH)] if (0, n) in sums else [gsmall[l][n] for l in range(DEPTH)]
            out[n] = adamw(n, *g)
    undo = lambda n, a: jnp.swapaxes(a, 1, 2) if n in TRANSPOSED else a
    return (loss, dx.reshape(x.shape), *[undo(n, out[n][i]) for i in range(4) for n in ORDER])


def _step(xs, pos, tgt, p, entry, mix_weights, late_weights, ffn_hook, out_hook, layer_hook):
    sps = [_small_operands(p, l) for l in range(DEPTH)]
    sps[0]["g_mix"] = sps[0]["g_mix"] + entry
    tabs = _rope_tables(pos, _rope_inv_freq())
    saved, h = [], xs
    for l in range(DEPTH):
        kw = _kernel_weights(mix_weights(l, h))
        h, sv = _layer_fwd(h, tabs, kw, functools.partial(late_weights, l), sps[l], l)
        saved.append(dict(sv, kw=kw))
    dy, lpart = _loss_grad(h, tgt)
    for l in reversed(range(DEPTH)):
        dy, g = _layer_bwd(dy, saved[l], tabs, saved[l]["kw"], sps[l], l, functools.partial(ffn_hook, l),
                           functools.partial(out_hook, l))
        zero = layer_hook(l, _big_grads(g), _small_grads(g))
        if zero is not None and l > 0:
            sps[l - 1]["g_ffn"] = sps[l - 1]["g_ffn"] + zero
    return 0.5 / D * jnp.sum(lpart), dy
```

```python
import functools
import math

import jax
import jax.numpy as jnp
from jax import lax
from jax.experimental import pallas as pl
from jax.experimental.pallas import tpu as pltpu

F32 = jnp.float32
BF16 = jnp.bfloat16
MESH = pl.DeviceIdType.MESH

D = 1024
HEADS = 4
QK = 96
NOPE = 64
ROPE = 32
VH = 128
HP = 128
QL = 256
KVL = 128
SGU = 256
POOL = 256
CHUNK = 128
HID = 2816
CHIPS = 4
SH = HID // CHIPS
IN_W = 1184
IN_P = 1280
EPS = 1e-6
ROPE_THETA = 10000.0
SCALE = 1.0 / math.sqrt(QK)
LOG2E = 1.4426950408889634
EXP2_C = SCALE * LOG2E
ATT_SPLIT = 2
ATT_WIDE = 4
ATT_FWD_QUERIES = 1024
ATT_PIECE = 512
ATT_ROWS = 256
ATT_KEYS = 1024
ATT_QUERIES = 2048
NEG = -1e30
HALO = 16

LR, B1, B2, ADAM_EPS, WD, STEP = 0.001, 0.9, 0.999, 1e-08, 0.01, 10

VMEM_LIMIT = 56 * 1024 * 1024
LANES = 128
TOKENS = 1024


def _cp(sem, vmem=None):
    return pltpu.CompilerParams(dimension_semantics=sem, vmem_limit_bytes=vmem)


def _res(shape):
    nd = len(shape)
    return pl.BlockSpec(shape, lambda *_: (0,) * nd, pipeline_mode=pl.Buffered(1))


def _acc(shape):
    nd = len(shape)
    return pl.BlockSpec(shape, lambda *_: (0,) * nd)


def _dot(a, b):
    return jnp.dot(a, b, preferred_element_type=F32)


def _dot_nt(a, b):
    return lax.dot_general(a, b, (((1,), (1,)), ((), ())), preferred_element_type=F32)


def _dot_tn(a, b):
    return lax.dot_general(a, b, (((0,), (0,)), ((), ())), preferred_element_type=F32)


def _rms(x, n):
    r = lax.rsqrt(jnp.sum(x * x, axis=-1, keepdims=True) * (1.0 / n) + EPS)
    return x * r, r


def _rms_bwd(xn, r, g, dy, n):
    dn = dy * g
    dx = r * (dn - xn * (jnp.sum(dn * xn, axis=-1, keepdims=True) * (1.0 / n)))
    return dx, jnp.sum(dy * xn, axis=0, keepdims=True)


def _accumulate(ref, val, first):
    @pl.when(first)
    def _():
        ref[...] = val

    @pl.when(jnp.logical_not(first))
    def _():
        ref[...] += val


def _accumulate0(ref, val, first):
    @pl.when(first)
    def _():
        ref[0] = val

    @pl.when(jnp.logical_not(first))
    def _():
        ref[0] += val


def _tile(s, t):
    return min(s, t)


def _row_tile(r, cap):
    if r <= cap:
        return r
    return max(t for t in range(8, cap + 1, 8) if r % t == 0)


def _rope_tables(pos, invf):
    s = pos.shape[0]
    tm = _tile(s, 1024)

    def body(pos_ref, invf_ref, c_ref, sa_ref, sb_ref):
        ang = pos_ref[...].astype(F32) * invf_ref[...]
        c, sn = jnp.cos(ang), jnp.sin(ang)
        lane = lax.broadcasted_iota(jnp.int32, ang.shape, 1)
        first = (lane >= NOPE) & (lane < NOPE + ROPE // 2)
        second = (lane >= NOPE + ROPE // 2) & (lane < QK)
        c_ref[...] = jnp.where(first | second, c, 1.0)
        sa_ref[...] = jnp.where(first, -sn, 0.0)
        sb_ref[...] = jnp.where(second, sn, 0.0)

    out = jax.ShapeDtypeStruct((s, HP), F32)
    return pl.pallas_call(
        body, name="rope_tables", grid=(s // tm,),
        in_specs=[pl.BlockSpec((tm, 1), lambda i: (i, 0)), _acc((1, HP))],
        out_specs=[pl.BlockSpec((tm, HP), lambda i: (i, 0))] * 3,
        out_shape=[out] * 3, compiler_params=_cp(("parallel",)),
    )(pos, invf)


def _rope(x, c, sa, sb):
    return x * c + pltpu.roll(x, HP - ROPE // 2, 1) * sa + pltpu.roll(x, ROPE // 2, 1) * sb


def _rope_t(d, c, sa, sb):
    return d * c + pltpu.roll(d * sa, ROPE // 2, 1) + pltpu.roll(d * sb, HP - ROPE // 2, 1)


def _in_proj_fwd(x, g, w, name):
    s = x.shape[0]
    tm = _tile(s, TOKENS)

    def body(x_ref, g_ref, w_ref, z_ref, h_ref):
        xn, _ = _rms(x_ref[...], D)
        h = (xn * g_ref[...]).astype(BF16)
        h_ref[...] = h
        z_ref[...] = _dot(h, w_ref[...])

    return pl.pallas_call(
        body, name=name, grid=(s // tm,),
        in_specs=[pl.BlockSpec((tm, D), lambda i: (i, 0)), _acc((1, D)), _res((D, IN_P))],
        out_specs=[pl.BlockSpec((tm, IN_P), lambda i: (i, 0)), pl.BlockSpec((tm, D), lambda i: (i, 0))],
        out_shape=[jax.ShapeDtypeStruct((s, IN_P), F32), jax.ShapeDtypeStruct((s, D), BF16)],
        compiler_params=_cp(("parallel",), VMEM_LIMIT),
    )(x, g, w)


def _mla_prep_fwd(z, tabs, gql, gkv, gq, gk, wq, wk, wv, name):
    s = z.shape[0]
    tm = _tile(s, TOKENS)

    def body(ql_ref, kv_ref, kr_ref, c_ref, sa_ref, sb_ref, gql_ref, gkv_ref, gq_ref, gk_ref,
             wq_ref, wk_ref, wv_ref, q_out, k_out, v_out):
        qn = (_rms(ql_ref[...], QL)[0] * gql_ref[...]).astype(BF16)
        kvn = (_rms(kv_ref[...], KVL)[0] * gkv_ref[...]).astype(BF16)
        qraw = _dot(qn, wq_ref[...])
        kraw = _dot(kvn, wk_ref[...])
        vraw = _dot(kvn, wv_ref[...])
        kr = kr_ref[...]
        c, sa, sb = c_ref[...], sa_ref[...], sb_ref[...]
        for h in range(HEADS):
            sl = slice(h * HP, (h + 1) * HP)
            xq = _rms(qraw[:, sl], QK)[0] * gq_ref[...]
            q_out[h] = _rope(xq, c, sa, sb).astype(BF16)
            xk = _rms(kraw[:, sl] + kr, QK)[0] * gk_ref[...]
            k_out[h] = _rope(xk, c, sa, sb).astype(BF16)
            v_out[h] = vraw[:, sl].astype(BF16)

    row = lambda w, j: pl.BlockSpec((tm, w), lambda i: (i, j))
    hspec = pl.BlockSpec((HEADS, tm, HP), lambda i: (0, i, 0))
    hshape = jax.ShapeDtypeStruct((HEADS, s, HP), BF16)
    return pl.pallas_call(
        body, name=name, grid=(s // tm,),
        in_specs=[row(QL, 0), row(KVL, 2), row(HP, 3), row(HP, 0), row(HP, 0), row(HP, 0),
                  _acc((1, QL)), _acc((1, KVL)), _acc((1, HP)), _acc((1, HP)),
                  _acc((QL, HEADS * HP)), _acc((KVL, HEADS * HP)), _acc((KVL, HEADS * HP))],
        out_specs=[hspec] * 3, out_shape=[hshape] * 3,
        compiler_params=_cp(("parallel",)),
    )(z, z, z, *tabs, gql, gkv, gq, gk, wq, wk, wv)


def _causal_mask(s, row0):
    row = lax.broadcasted_iota(jnp.int32, s.shape, 0) + row0
    col = lax.broadcasted_iota(jnp.int32, s.shape, 1)
    return jnp.where(col <= row, s, NEG)


def _attn_fwd(q, k, v, name):
    s = q.shape[1]
    tq = _tile(s, ATT_FWD_QUERIES)
    rh = _tile(s, ATT_ROWS)
    kp = _tile(s, ATT_PIECE)
    wide = ATT_WIDE * kp if s % (ATT_WIDE * kp) == 0 else tq
    groups = tq // rh

    def body(q_ref, k_ref, v_ref, o_ref, lse_ref):
        i = pl.program_id(1)

        def blk(off, tk, carry, diagonal):
            width = lambda g, t: max(0, min(kp, (g + 1) * rh - t * kp)) if diagonal else kp
            rows = lambda t: pl.ds(pl.multiple_of(off + t * kp, kp), kp)
            score = lambda g, t: _dot_nt(q_ref[0, g * rh:(g + 1) * rh, :], k_ref[0, rows(t), :][:width(g, t)])
            live = lambda t: [g for g in range(groups) if width(g, t) > 0]
            state = list(carry)
            scs = {(g, 0): score(g, 0) for g in live(0)}
            for t in range(tk // kp):
                if (t + 1) * kp < tk:
                    scs.update({(g, t + 1): score(g, t + 1) for g in live(t + 1)})
                vt = v_ref[0, rows(t), :]
                for g in live(t):
                    m, l, acc = state[g]
                    sc = scs.pop((g, t))
                    if diagonal and (g + 1) * rh <= (t + 1) * kp:
                        sc = _causal_mask(sc, g * rh - t * kp)
                    m_new = jnp.maximum(m, jnp.max(sc, axis=-1, keepdims=True))
                    p = jnp.exp2((sc - m_new) * EXP2_C)
                    alpha = jnp.exp2((m - m_new) * EXP2_C)
                    l = alpha * l + jnp.sum(p, axis=-1, keepdims=True)
                    acc = alpha * acc + _dot(p.astype(BF16), vt[:width(g, t)])
                    state[g] = (m_new, l, acc)
            return tuple(state)

        one = (jnp.full((rh, 1), NEG, F32), jnp.zeros((rh, 1), F32), jnp.zeros((rh, VH), F32))
        nwide = (i * tq) // wide
        carry = lax.fori_loop(0, nwide, lambda j, c: blk(j * wide, wide, c, False), (one,) * groups)
        carry = lax.fori_loop(nwide * (wide // tq), i, lambda j, c: blk(j * tq, tq, c, False), carry)
        carry = blk(i * tq, tq, carry, True)
        for g, (m, l, acc) in enumerate(carry):
            o_ref[g * rh:(g + 1) * rh, :] = acc / l
            lse_ref[0, g * rh:(g + 1) * rh, :] = jnp.broadcast_to(m * EXP2_C + jnp.log(l) * LOG2E, (rh, LANES))

    return pl.pallas_call(
        body, name=name, grid=(HEADS, s // tq),
        in_specs=[pl.BlockSpec((1, tq, HP), lambda h, i: (h, i, 0)),
                  pl.BlockSpec((1, s, HP), lambda h, i: (h, 0, 0)),
                  pl.BlockSpec((1, s, HP), lambda h, i: (h, 0, 0))],
        out_specs=[pl.BlockSpec((tq, VH), lambda h, i: (i, h)),
                   pl.BlockSpec((1, tq, LANES), lambda h, i: (h, i, 0))],
        out_shape=[jax.ShapeDtypeStruct((s, HEADS * VH), F32), jax.ShapeDtypeStruct((HEADS, s, LANES), F32)],
        compiler_params=_cp(("parallel", "arbitrary"), VMEM_LIMIT),
    )(q, k, v)


def _lane_group(shape, j):
    return (lax.broadcasted_iota(jnp.int32, shape, 1) + j * LANES) // (POOL // 4)


def _pool_win_fwd(z, name):
    s = z.shape[0]
    ch = _tile(s, 512)
    col0 = (IN_P - POOL) // LANES

    def body(p_ref, m_ref):
        j = pl.program_id(0)

        def chunk(r, _):
            off = pl.multiple_of(r * ch, ch)
            cur = p_ref[pl.ds(off, ch), :]
            hoff = pl.multiple_of(jnp.maximum(off - HALO, 0), 8)
            halo = jnp.where(r > 0, p_ref[pl.ds(hoff, HALO), :], 0.0)
            x = jnp.concatenate([halo, cur], axis=0)
            s2 = x + pltpu.roll(x, 1, 0)
            s4 = s2 + pltpu.roll(s2, 2, 0)
            s8 = s4 + pltpu.roll(s4, 4, 0)
            s16 = s8 + pltpu.roll(s8, 8, 0)
            grp = _lane_group((ch, LANES), j)
            sel = jnp.where(grp == 0, s2[HALO:], jnp.where(grp == 1, s4[HALO:], jnp.where(grp == 2, s8[HALO:], s16[HALO:])))
            t1 = (lax.broadcasted_iota(jnp.int32, (ch, LANES), 0) + off + 1).astype(F32)
            win = jnp.where(grp == 0, 2.0, jnp.where(grp == 1, 4.0, jnp.where(grp == 2, 8.0, 16.0)))
            m_ref[pl.ds(off, ch), :] = sel / jnp.minimum(t1, win) - cur
            return 0

        lax.fori_loop(0, s // ch, chunk, 0)

    return pl.pallas_call(
        body, name=name, grid=(POOL // LANES,),
        in_specs=[pl.BlockSpec((s, LANES), lambda j: (0, col0 + j))],
        out_specs=pl.BlockSpec((s, LANES), lambda j: (0, j)),
        out_shape=jax.ShapeDtypeStruct((s, POOL), F32),
        compiler_params=_cp(("parallel",), VMEM_LIMIT),
    )(z)


def _pool_win_bwd(dm, name):
    s = dm.shape[0]
    ch = _tile(s, 512)
    n = s // ch

    def body(dm_ref, dp_ref):
        j = pl.program_id(0)

        def chunk(r, _):
            off = pl.multiple_of(r * ch, ch)
            grp = _lane_group((ch + HALO, LANES), j)
            win = jnp.where(grp == 0, 2.0, jnp.where(grp == 1, 4.0, jnp.where(grp == 2, 8.0, 16.0)))
            cur = dm_ref[pl.ds(off, ch), :]
            hoff = pl.multiple_of(jnp.minimum(off + ch, s - HALO), 8)
            halo = jnp.where(r < n - 1, dm_ref[pl.ds(hoff, HALO), :], 0.0)
            x = jnp.concatenate([cur, halo], axis=0)
            t1 = (lax.broadcasted_iota(jnp.int32, (ch + HALO, LANES), 0) + off + 1).astype(F32)
            e = x / jnp.minimum(t1, win)
            tot = ch + HALO
            r2 = e + pltpu.roll(e, tot - 1, 0)
            r4 = r2 + pltpu.roll(r2, tot - 2, 0)
            r8 = r4 + pltpu.roll(r4, tot - 4, 0)
            r16 = r8 + pltpu.roll(r8, tot - 8, 0)
            g = grp[:ch]
            sel = jnp.where(g == 0, r2[:ch], jnp.where(g == 1, r4[:ch], jnp.where(g == 2, r8[:ch], r16[:ch])))
            dp_ref[pl.ds(off, ch), :] = (sel - cur).astype(BF16)
            return 0

        lax.fori_loop(0, n, chunk, 0)

    return pl.pallas_call(
        body, name=name, grid=(POOL // LANES,),
        in_specs=[pl.BlockSpec((s, LANES), lambda j: (0, j))],
        out_specs=pl.BlockSpec((s, LANES), lambda j: (0, j)),
        out_shape=jax.ShapeDtypeStruct((s, POOL), BF16),
        compiler_params=_cp(("parallel",), VMEM_LIMIT),
    )(dm)


def _head_mask(h):
    lane = lax.broadcasted_iota(jnp.int32, (CHUNK, SGU), 1)
    return (lane // (SGU // HEADS)) == h


def _tril(upper=False):
    row = lax.broadcasted_iota(jnp.int32, (CHUNK, CHUNK), 0)
    col = lax.broadcasted_iota(jnp.int32, (CHUNK, CHUNK), 1)
    return col >= row if upper else col <= row


def _sgu_gate(vn, wsp, bsp):
    out = []
    for cidx in range(vn.shape[0] // CHUNK):
        vc = vn[cidx * CHUNK:(cidx + 1) * CHUNK]
        zc = bsp
        for h in range(HEADS):
            zc = zc + jnp.where(_head_mask(h), _dot(wsp[h], vc), 0.0)
        out.append(zc)
    return jnp.concatenate(out, axis=0)


def _mix_out_fwd(o, z, m, x, wsp, bsp, wbd, psc, gsv, gout, wout, name):
    s = x.shape[0]
    tm = _tile(s, TOKENS)

    def body(o_ref, uv_ref, m_ref, x_ref, wsp_ref, bsp_ref, wbd_ref, psc_ref, gsv_ref, gout_ref, wout_ref,
             x1_ref, mix_ref):
        g = gout_ref[...]
        an = _rms(o_ref[...], HEADS * VH)[0] * g[:, :512]
        uv = uv_ref[...]
        u, v = uv[:, :SGU], uv[:, SGU:]
        vn = (_rms(v, SGU)[0] * gsv_ref[...]).astype(BF16)
        tri = _tril()
        wsp_m = [jnp.where(tri, wsp_ref[h], 0.0).astype(BF16) for h in range(HEADS)]
        gm = u * _sgu_gate(vn, wsp_m, bsp_ref[...])
        gn = _rms(gm, SGU)[0] * g[:, 512:768]
        po = _dot(m_ref[...].astype(BF16), wbd_ref[...]) * psc_ref[...]
        pn = _rms(po, POOL)[0] * g[:, 768:]
        mix = jnp.concatenate([an, gn, pn], axis=1).astype(BF16)
        mix_ref[...] = mix
        x1_ref[...] = x_ref[...] + _dot(mix, wout_ref[...])

    row = lambda w, j: pl.BlockSpec((tm, w), lambda i: (i, j))
    return pl.pallas_call(
        body, name=name, grid=(s // tm,),
        in_specs=[row(512, 0), row(512, 1), row(POOL, 0), row(D, 0),
                  _acc((HEADS, CHUNK, CHUNK)), _acc((CHUNK, SGU)), _acc((POOL, POOL)), _acc((1, POOL)),
                  _acc((1, SGU)), _acc((1, D)), _res((D, D))],
        out_specs=[row(D, 0), row(D, 0)],
        out_shape=[jax.ShapeDtypeStruct((s, D), F32), jax.ShapeDtypeStruct((s, D), BF16)],
        compiler_params=_cp(("parallel",), VMEM_LIMIT),
    )(o, z, m, x, wsp, bsp, wbd, psc, gsv, gout, wout)


def _ffn_fwd(x1, g, wg, wu, wd, name):
    s = x1.shape[0]
    tm = _tile(s, 256)

    def body(x_ref, g_ref, wg_ref, wu_ref, wd_ref, x2_ref, a_ref, b_ref, h_ref):
        x = x_ref[...]
        h = (_rms(x, D)[0] * g_ref[...]).astype(BF16)
        h_ref[...] = h
        acc = jnp.zeros((tm, D), F32)
        for k in range(CHIPS):
            a = _dot_nt(h, wg_ref[k])
            b = _dot_nt(h, wu_ref[k])
            a_ref[k] = a
            b_ref[k] = b
            acc = acc + _dot((a * jax.nn.sigmoid(a) * b).astype(BF16), wd_ref[k])
        x2_ref[...] = x + acc

    row = lambda w: pl.BlockSpec((tm, w), lambda i: (i, 0))
    hrow = pl.BlockSpec((CHIPS, tm, SH), lambda i: (0, i, 0))
    hshape = jax.ShapeDtypeStruct((CHIPS, s, SH), F32)
    return pl.pallas_call(
        body, name=name, grid=(s // tm,),
        in_specs=[row(D), _acc((1, D)), _res((CHIPS, SH, D)), _res((CHIPS, SH, D)), _res((CHIPS, SH, D))],
        out_specs=[row(D), hrow, hrow, row(D)],
        out_shape=[jax.ShapeDtypeStruct((s, D), F32), hshape, hshape, jax.ShapeDtypeStruct((s, D), BF16)],
        compiler_params=_cp(("parallel",), VMEM_LIMIT),
    )(x1, g, wg, wu, wd)


def _loss_grad(y, tgt):
    s = y.shape[0]
    tm = _tile(s, TOKENS)

    def body(y_ref, t_ref, dy_ref, l_ref):
        e = y_ref[...] - t_ref[...]
        dy_ref[...] = e * (1.0 / D)
        sq = jnp.sum(e * e, axis=0, keepdims=True)
        part = sq[:, :LANES]
        for c in range(1, D // LANES):
            part = part + sq[:, c * LANES:(c + 1) * LANES]
        _accumulate(l_ref, part, pl.program_id(0) == 0)

    row = pl.BlockSpec((tm, D), lambda i: (i, 0))
    return pl.pallas_call(
        body, name="loss_grad", grid=(s // tm,),
        in_specs=[row, row], out_specs=[row, _acc((1, LANES))],
        out_shape=[jax.ShapeDtypeStruct((s, D), F32), jax.ShapeDtypeStruct((1, LANES), F32)],
        compiler_params=_cp(("arbitrary",)),
    )(y, tgt)


def _wgrad(a, b, name):
    s, k = a.shape
    n = b.shape[1]
    half = lambda v: v if v <= 1408 else v // 2
    kb, nb, tt = half(k), half(n), _tile(s, 2048)

    def body(a_ref, b_ref, o_ref):
        _accumulate(o_ref, _dot_tn(a_ref[...].astype(BF16), b_ref[...].astype(BF16)), pl.program_id(2) == 0)

    return pl.pallas_call(
        body, name=name, grid=(k // kb, n // nb, s // tt),
        in_specs=[pl.BlockSpec((tt, kb), lambda i, j, t: (t, i)), pl.BlockSpec((tt, nb), lambda i, j, t: (t, j))],
        out_specs=pl.BlockSpec((kb, nb), lambda i, j, t: (i, j)),
        out_shape=jax.ShapeDtypeStruct((k, n), F32),
        compiler_params=_cp(("parallel", "parallel", "arbitrary"), VMEM_LIMIT),
    )(a, b)


def _wgrad_in(h, dzm, duv, dp, name):
    s = h.shape[0]
    tt = _tile(s, 2048)

    def body(h_ref, a_ref, b_ref, c_ref, o_ref):
        hv = h_ref[...]
        val = jnp.concatenate([_dot_tn(hv, a_ref[...]), _dot_tn(hv, b_ref[...]), _dot_tn(hv, c_ref[...])], axis=1)
        _accumulate(o_ref, val, pl.program_id(0) == 0)

    row = lambda w: pl.BlockSpec((tt, w), lambda t: (t, 0))
    return pl.pallas_call(
        body, name=name, grid=(s // tt,), in_specs=[row(D), row(512), row(512), row(POOL)], out_specs=_acc((D, IN_P)),
        out_shape=jax.ShapeDtypeStruct((D, IN_P), F32), compiler_params=_cp(("arbitrary",), VMEM_LIMIT),
    )(h, dzm, duv, dp)


def _wgrad_rows(a, b, name):
    s, n = a.shape[1:]
    nn = b.shape[1]
    tt = _tile(s, 4096 if b.dtype == BF16 else 2048)

    def body(a_ref, b_ref, o_ref):
        _accumulate0(o_ref, _dot_tn(a_ref[0].astype(BF16), b_ref[...].astype(BF16)), pl.program_id(1) == 0)

    return pl.pallas_call(
        body, name=name, grid=(CHIPS, s // tt),
        in_specs=[pl.BlockSpec((1, tt, n), lambda c, t: (c, t, 0)), pl.BlockSpec((tt, nn), lambda c, t: (t, 0))],
        out_specs=pl.BlockSpec((1, n, nn), lambda c, t: (c, 0, 0)),
        out_shape=jax.ShapeDtypeStruct((CHIPS, n, nn), F32),
        compiler_params=_cp(("parallel", "arbitrary"), VMEM_LIMIT),
    )(a, b)


def _ffn_bwd(dx2, x1, a, b, g, wg, wu, wd, name):
    s = x1.shape[0]
    tm = _tile(s, 256)

    def body(dx2_ref, x_ref, a_ref, b_ref, g_ref, wg_ref, wu_ref, wd_ref,
             dx1_ref, hid_ref, da_ref, db_ref, dyb_ref, dg_ref):
        dx2 = dx2_ref[...]
        dyb = dx2.astype(BF16)
        dyb_ref[...] = dyb
        dh = jnp.zeros((tm, D), F32)
        ahead = _dot_nt(dyb, wd_ref[0])
        for k in range(CHIPS):
            av, bv = a_ref[k], b_ref[k]
            dhid = ahead
            if k + 1 < CHIPS:
                ahead = _dot_nt(dyb, wd_ref[k + 1])
            sig = jax.nn.sigmoid(av)
            sa = av * sig
            hid_ref[k] = (sa * bv).astype(BF16)
            dbv = (dhid * sa).astype(BF16)
            dav = (dhid * bv * (sig * (1.0 + av * (1.0 - sig)))).astype(BF16)
            db_ref[k] = dbv
            da_ref[k] = dav
            dh = dh + _dot(dav, wg_ref[k]) + _dot(dbv, wu_ref[k])
        xn, r = _rms(x_ref[...], D)
        dxr, dg = _rms_bwd(xn, r, g_ref[...], dh, D)
        dx1_ref[...] = dx2 + dxr
        _accumulate(dg_ref, dg, pl.program_id(0) == 0)

    row = lambda w: pl.BlockSpec((tm, w), lambda i: (i, 0))
    hrow = pl.BlockSpec((CHIPS, tm, SH), lambda i: (0, i, 0))
    hid = jax.ShapeDtypeStruct((CHIPS, s, SH), BF16)
    return pl.pallas_call(
        body, name=name, grid=(s // tm,),
        in_specs=[row(D), row(D), hrow, hrow, _acc((1, D)), _res((CHIPS, SH, D)), _res((CHIPS, SH, D)),
                  _res((CHIPS, SH, D))],
        out_specs=[row(D), hrow, hrow, hrow, row(D), _acc((1, D))],
        out_shape=[jax.ShapeDtypeStruct((s, D), F32), hid, hid, hid, jax.ShapeDtypeStruct((s, D), BF16),
                   jax.ShapeDtypeStruct((1, D), F32)],
        compiler_params=_cp(("arbitrary",), VMEM_LIMIT),
    )(dx2, x1, a, b, g, wg, wu, wd)


def _mix_out_bwd(dx1, o, z, m, wsp, bsp, wbd, psc, gsv, gout, wout, name):
    s = dx1.shape[0]
    tm = _tile(s, TOKENS)

    def body(dx1_ref, o_ref, uv_ref, m_ref, wsp_ref, bsp_ref, wbd_ref, psc_ref, gsv_ref, gout_ref, wout_ref,
             do_ref, dl_ref, duv_ref, dm_ref, dgo_ref, dgsv_ref, dpsc_ref, dwsp_ref, dbsp_ref, dwbd_ref):
        first = pl.program_id(0) == 0
        g = gout_ref[...]
        dmix = _dot_nt(dx1_ref[...].astype(BF16), wout_ref[...])
        o = o_ref[...]
        on, ro = _rms(o, HEADS * VH)
        do, dga = _rms_bwd(on, ro, g[:, :512], dmix[:, :512], HEADS * VH)
        for h in range(HEADS):
            sl = slice(h * VH, (h + 1) * VH)
            do_ref[h] = do[:, sl].astype(BF16)
            dl_ref[h] = jnp.broadcast_to(jnp.sum(do[:, sl] * o[:, sl], axis=-1, keepdims=True), (tm, LANES))
        uv = uv_ref[...]
        u, v = uv[:, :SGU], uv[:, SGU:]
        vx, rv = _rms(v, SGU)
        vn = (vx * gsv_ref[...]).astype(BF16)
        tri = _tril()
        wsp_m = [jnp.where(tri, wsp_ref[h], 0.0).astype(BF16) for h in range(HEADS)]
        zc = _sgu_gate(vn, wsp_m, bsp_ref[...])
        gm = u * zc
        gmn, rg = _rms(gm, SGU)
        dgm, dgg = _rms_bwd(gmn, rg, g[:, 512:768], dmix[:, 512:768], SGU)
        du = dgm * zc
        dzc = dgm * u
        dvn_parts = []
        dbsp = jnp.zeros((CHUNK, SGU), F32)
        dwsp = [jnp.zeros((CHUNK, CHUNK), F32) for _ in range(HEADS)]
        for cidx in range(tm // CHUNK):
            rs = slice(cidx * CHUNK, (cidx + 1) * CHUNK)
            dzc_c = dzc[rs]
            dbsp = dbsp + dzc_c
            dzb = dzc_c.astype(BF16)
            vc = vn[rs]
            dvn_c = jnp.zeros((CHUNK, SGU), F32)
            for h in range(HEADS):
                hm = _head_mask(h)
                dvn_c = dvn_c + jnp.where(hm, _dot_tn(wsp_m[h], dzb), 0.0)
                dwsp[h] = dwsp[h] + _dot_nt(jnp.where(hm, dzc_c, 0.0).astype(BF16), vc)
            dvn_parts.append(dvn_c)
        dvn = jnp.concatenate(dvn_parts, axis=0)
        dv, dgsv = _rms_bwd(vx, rv, gsv_ref[...], dvn, SGU)
        duv_ref[...] = jnp.concatenate([du, dv], axis=1).astype(BF16)
        mb = m_ref[...].astype(BF16)
        pw = _dot(mb, wbd_ref[...])
        po = pw * psc_ref[...]
        pon, rp = _rms(po, POOL)
        dpo, dgp = _rms_bwd(pon, rp, g[:, 768:], dmix[:, 768:], POOL)
        dpw = (dpo * psc_ref[...]).astype(BF16)
        dm_ref[...] = _dot_nt(dpw, wbd_ref[...])
        _accumulate(dgo_ref, jnp.concatenate([dga, dgg, dgp], axis=1), first)
        _accumulate(dgsv_ref, dgsv, first)
        _accumulate(dpsc_ref, jnp.sum(dpo * pw, axis=0, keepdims=True), first)
        _accumulate(dbsp_ref, dbsp, first)
        _accumulate(dwbd_ref, _dot_tn(mb, dpw), first)
        for h in range(HEADS):
            val = jnp.where(tri, dwsp[h], 0.0)

            @pl.when(first)
            def _(val=val, h=h):
                dwsp_ref[h] = val

            @pl.when(jnp.logical_not(first))
            def _(val=val, h=h):
                dwsp_ref[h] += val

    row = lambda w, j: pl.BlockSpec((tm, w), lambda i: (i, j))
    hspec = pl.BlockSpec((HEADS, tm, HP), lambda i: (0, i, 0))
    return pl.pallas_call(
        body, name=name, grid=(s // tm,),
        in_specs=[row(D, 0), row(512, 0), row(512, 1), row(POOL, 0),
                  _acc((HEADS, CHUNK, CHUNK)), _acc((CHUNK, SGU)),
                  _acc((POOL, POOL)), _acc((1, POOL)), _acc((1, SGU)), _acc((1, D)), _res((D, D))],
        out_specs=[hspec, hspec, row(512, 0), row(POOL, 0), _acc((1, D)), _acc((1, SGU)), _acc((1, POOL)),
                   _acc((HEADS, CHUNK, CHUNK)), _acc((CHUNK, SGU)), _acc((POOL, POOL))],
        out_shape=[jax.ShapeDtypeStruct((HEADS, s, HP), BF16), jax.ShapeDtypeStruct((HEADS, s, LANES), F32),
                   jax.ShapeDtypeStruct((s, 512), BF16), jax.ShapeDtypeStruct((s, POOL), F32),
                   jax.ShapeDtypeStruct((1, D), F32), jax.ShapeDtypeStruct((1, SGU), F32),
                   jax.ShapeDtypeStruct((1, POOL), F32), jax.ShapeDtypeStruct((HEADS, CHUNK, CHUNK), F32),
                   jax.ShapeDtypeStruct((CHUNK, SGU), F32), jax.ShapeDtypeStruct((POOL, POOL), F32)],
        compiler_params=_cp(("arbitrary",), VMEM_LIMIT),
    )(dx1, o, z, m, wsp, bsp, wbd, psc, gsv, gout, wout)


def _attn_bwd(q, k, v, do, lse, delta, after, name):
    s = q.shape[1]
    rh = _tile(s, ATT_ROWS)
    tk = _tile(s, ATT_KEYS)
    nk = s // tk
    wide = ATT_QUERIES if s % ATT_QUERIES == 0 else tk
    pieces = tk // rh

    def body(q_ref, k_ref, v_ref, do_ref, lse_ref, dl_ref, after_ref, dq_ref, dk_ref, dv_ref):
        del after_ref
        j = pl.program_id(1)

        @pl.when(j == 0)
        def _():
            dq_ref[...] = jnp.zeros_like(dq_ref)

        kj, vj = k_ref[0], v_ref[0]

        def blk(start, rows, dks, dvs, diagonal):
            dks, dvs = list(dks), list(dvs)
            offs = [pl.multiple_of(start + g * rh, rh) for g in range(rows // rh)]
            keys = [(g + 1) * rh if diagonal else tk for g in range(rows // rh)]
            qs = [q_ref[0, pl.ds(off, rh), :] for off in offs]
            dos = [do_ref[0, pl.ds(off, rh), :] for off in offs]
            scs = [_dot_nt(qi, kj[:n]) for qi, n in zip(qs, keys)]
            dps = [_dot_nt(doi, vj[:n]) for doi, n in zip(dos, keys)]
            for g, off in enumerate(offs):
                lse_i = lse_ref[0, pl.ds(off, rh), :][:, :1]
                dl_i = dl_ref[0, pl.ds(off, rh), :][:, :1]
                sc = _causal_mask(scs[g], g * rh) if diagonal else scs[g]
                p = jnp.exp2(sc * EXP2_C - lse_i)
                ds = (p * (dps[g] - dl_i)).astype(BF16)
                cv = _dot_tn(p.astype(BF16), dos[g])
                ck = _dot_tn(ds, qs[g])
                for t in range(keys[g] // rh):
                    dvs[t] = dvs[t] + cv[t * rh:(t + 1) * rh]
                    dks[t] = dks[t] + ck[t * rh:(t + 1) * rh]
                dq_ref[0, pl.ds(off, rh), :] += _dot(ds, kj[:keys[g]]) * SCALE
            return tuple(dks), tuple(dvs)

        per = wide // tk
        zero = (jnp.zeros((rh, HP), F32),) * pieces
        acc = blk(j * tk, tk, zero, zero, True)
        first_wide = (j + per) // per
        acc = lax.fori_loop(j + 1, jnp.minimum(first_wide * per, nk), lambda i, c: blk(i * tk, tk, *c, False), acc)
        dks, dvs = lax.fori_loop(first_wide, nk // per, lambda i, c: blk(i * wide, wide, *c, False), acc)
        dk_ref[0] = jnp.concatenate(dks, axis=0) * SCALE
        dv_ref[0] = jnp.concatenate(dvs, axis=0)

    full = lambda: pl.BlockSpec((1, s, HP), lambda h, j: (h, 0, 0))
    blk_spec = lambda: pl.BlockSpec((1, tk, HP), lambda h, j: (h, j, 0))
    out = jax.ShapeDtypeStruct((HEADS, s, HP), F32)
    return pl.pallas_call(
        body, name=name, grid=(HEADS, s // tk),
        in_specs=[full(), blk_spec(), blk_spec(), full(), full(), full(), ANY],
        out_specs=[full(), blk_spec(), blk_spec()], out_shape=[out] * 3,
        compiler_params=_cp(("parallel", "arbitrary"), VMEM_LIMIT),
    )(q, k, v, do, lse, delta, after)


def _mla_prep_bwd(dq, dk, dv, z, tabs, gql, gkv, gq, gk, wq, wk, wv, name):
    s = z.shape[0]
    tm = _tile(s, TOKENS)

    def body(dq_ref, dk_ref, dv_ref, ql_ref, kv_ref, kr_ref, c_ref, sa_ref, sb_ref, gql_ref, gkv_ref, gq_ref, gk_ref,
             wq_ref, wk_ref, wv_ref,
             dz_ref, qn_ref, kvn_ref, dqr_ref, dkr_ref, dvr_ref, dgql_ref, dgkv_ref, dgq_ref, dgk_ref):
        first = pl.program_id(0) == 0
        qx, rq = _rms(ql_ref[...], QL)
        qn = (qx * gql_ref[...]).astype(BF16)
        kx, rk = _rms(kv_ref[...], KVL)
        kvn = (kx * gkv_ref[...]).astype(BF16)
        qn_ref[...] = qn
        kvn_ref[...] = kvn
        qraw = _dot(qn, wq_ref[...])
        kraw = _dot(kvn, wk_ref[...])
        kr = kr_ref[...]
        c, sa, sb = c_ref[...], sa_ref[...], sb_ref[...]
        lane = lax.broadcasted_iota(jnp.int32, (tm, HP), 1)
        rope_lanes = (lane >= NOPE) & (lane < QK)
        dkrope = jnp.zeros((tm, HP), F32)
        dgq = jnp.zeros((1, HP), F32)
        dgk = jnp.zeros((1, HP), F32)
        for h in range(HEADS):
            sl = slice(h * HP, (h + 1) * HP)
            xn, r = _rms(qraw[:, sl], QK)
            dx, dg = _rms_bwd(xn, r, gq_ref[...], _rope_t(dq_ref[h], c, sa, sb), QK)
            dqr_ref[:, sl] = dx.astype(BF16)
            dgq = dgq + dg
            xn, r = _rms(kraw[:, sl] + kr, QK)
            dx, dg = _rms_bwd(xn, r, gk_ref[...], _rope_t(dk_ref[h], c, sa, sb), QK)
            dkr_ref[:, sl] = dx.astype(BF16)
            dgk = dgk + dg
            dkrope = dkrope + jnp.where(rope_lanes, dx, 0.0)
            dvr_ref[:, sl] = dv_ref[h].astype(BF16)
        dqn = _dot_nt(dqr_ref[...], wq_ref[...])
        dql, dgql = _rms_bwd(qx, rq, gql_ref[...], dqn, QL)
        dkvn = _dot_nt(dkr_ref[...], wk_ref[...]) + _dot_nt(dvr_ref[...], wv_ref[...])
        dkv, dgkv = _rms_bwd(kx, rk, gkv_ref[...], dkvn, KVL)
        dz_ref[...] = jnp.concatenate([dql, dkv, dkrope], axis=1).astype(BF16)
        _accumulate(dgql_ref, dgql, first)
        _accumulate(dgkv_ref, dgkv, first)
        _accumulate(dgq_ref, dgq, first)
        _accumulate(dgk_ref, dgk, first)

    row = lambda w, j: pl.BlockSpec((tm, w), lambda i: (i, j))
    hspec = pl.BlockSpec((HEADS, tm, HP), lambda i: (0, i, 0))
    sd = lambda w, dt: jax.ShapeDtypeStruct((s, w), dt)
    return pl.pallas_call(
        body, name=name, grid=(s // tm,),
        in_specs=[hspec, hspec, hspec, row(QL, 0), row(KVL, 2), row(HP, 3), row(HP, 0), row(HP, 0), row(HP, 0),
                  _acc((1, QL)), _acc((1, KVL)), _acc((1, HP)), _acc((1, HP)),
                  _acc((QL, HEADS * HP)), _acc((KVL, HEADS * HP)), _acc((KVL, HEADS * HP))],
        out_specs=[row(512, 0), row(QL, 0), row(KVL, 0), row(512, 0), row(512, 0), row(512, 0),
                   _acc((1, QL)), _acc((1, KVL)), _acc((1, HP)), _acc((1, HP))],
        out_shape=[sd(512, BF16), sd(QL, BF16), sd(KVL, BF16), sd(512, BF16), sd(512, BF16), sd(512, BF16),
                   jax.ShapeDtypeStruct((1, QL), F32), jax.ShapeDtypeStruct((1, KVL), F32),
                   jax.ShapeDtypeStruct((1, HP), F32), jax.ShapeDtypeStruct((1, HP), F32)],
        compiler_params=_cp(("arbitrary",), VMEM_LIMIT),
    )(dq, dk, dv, z, z, z, *tabs, gql, gkv, gq, gk, wq, wk, wv)


def _in_proj_bwd(dzm, duv, dp, x, dx1, g, win, name):
    s = x.shape[0]
    tm = _tile(s, TOKENS // 2)

    def body(dzm_ref, duv_ref, dp_ref, x_ref, dx1_ref, g_ref, w_ref, dx_ref, dg_ref):
        groups = [slice(r0, r0 + tm // 2) for r0 in (0, tm // 2)]
        dhs = [_dot_nt(dzm_ref[rs, :], w_ref[:, 0:512]) + _dot_nt(duv_ref[rs, :], w_ref[:, 512:1024])
               + _dot_nt(dp_ref[rs, :], w_ref[:, 1024:IN_P]) for rs in groups]
        dg = jnp.zeros((1, D), F32)
        for rs, dh in zip(groups, dhs):
            xn, r = _rms(x_ref[rs, :], D)
            dxr, dgr = _rms_bwd(xn, r, g_ref[...], dh, D)
            dx_ref[rs, :] = dx1_ref[rs, :] + dxr
            dg = dg + dgr
        _accumulate(dg_ref, dg, pl.program_id(0) == 0)

    row = lambda w: pl.BlockSpec((tm, w), lambda i: (i, 0))
    return pl.pallas_call(
        body, name=name, grid=(s // tm,),
        in_specs=[row(512), row(512), row(POOL), row(D), row(D), _acc((1, D)), _res((D, IN_P))],
        out_specs=[row(D), _acc((1, D))],
        out_shape=[jax.ShapeDtypeStruct((s, D), F32), jax.ShapeDtypeStruct((1, D), F32)],
        compiler_params=_cp(("arbitrary",), VMEM_LIMIT),
    )(dzm, duv, dp, x, dx1, g, win)


def _adamw(w, g0, g1, m, v, name):
    _, r, c = w.shape
    tr = _row_tile(r, 512)
    c1 = 1.0 - B1 ** STEP
    c2 = 1.0 - B2 ** STEP

    def body(w_ref, g0_ref, g1_ref, m_ref, v_ref, g_ref, d_ref, nm_ref, nv_ref):
        gv = jnp.where(pl.program_id(0) == 0, g0_ref[...], g1_ref[...])
        g_ref[0] = gv
        nm = B1 * m_ref[0] + (1.0 - B1) * gv
        nv = B2 * v_ref[0] + (1.0 - B2) * (gv * gv)
        nm_ref[0] = nm
        nv_ref[0] = nv
        d_ref[0] = -LR * ((nm / c1) / (jnp.sqrt(nv / c2) + ADAM_EPS) + WD * w_ref[0])

    spec = pl.BlockSpec((1, tr, c), lambda l, i: (l, i, 0))
    out = jax.ShapeDtypeStruct((DEPTH, r, c), F32)
    return pl.pallas_call(
        body, name=name, grid=(DEPTH, r // tr),
        in_specs=[spec, pl.BlockSpec((tr, c), lambda l, i: (i * (1 - l), 0)), pl.BlockSpec((tr, c), lambda l, i: (i * l, 0)),
                  spec, spec],
        out_specs=[spec] * 4, out_shape=[out] * 4, compiler_params=_cp(("parallel", "parallel")),
    )(w, g0, g1, m, v)


ANY = pl.BlockSpec(memory_space=pl.ANY)


def _place():
    x, y, c = lax.axis_index("x"), lax.axis_index("y"), lax.axis_index("c")
    chips = [(1 - x, y), (x, 1 - y), (1 - x, 1 - y)]
    return x, y, c, chips


def _half_rows(ref, lead, hh, half, align):
    rows = pl.ds(pl.multiple_of(hh * half, align), half)
    return ref.at[rows, :] if lead is None else ref.at[lead, rows, :]


def _row_align(dtype):
    return 16 if dtype == BF16 else 8


def _sems(n):
    return [pltpu.SemaphoreType.DMA((n,)), pltpu.SemaphoreType.DMA((n,)), pltpu.SemaphoreType.DMA((n,))]


def _comm_call(body, ins, out_shapes, nsems, name):
    return pl.pallas_call(
        body, name=name, in_specs=[ANY] * len(ins), out_specs=[ANY] * len(out_shapes), out_shape=out_shapes,
        scratch_shapes=_sems(nsems), compiler_params=pltpu.CompilerParams(has_side_effects=True),
    )(*ins)


def _all_gather_chips(shards, name):
    n = len(shards)
    halves = [a.shape[0] // 2 for a in shards]
    aligns = [_row_align(a.dtype) for a in shards]
    assert all(h % al == 0 for h, al in zip(halves, aligns))

    def body(*refs):
        ins, outs, (send_sems, recv_sems, _) = refs[:n], refs[n:2 * n], refs[2 * n:]
        x, y, c, chips = _place()
        me = 2 * x + y
        sibling = (x, y, 1 - c)

        def copy(sem, src, dst, to):
            return pltpu.make_async_remote_copy(src_ref=src, dst_ref=dst, send_sem=send_sems.at[sem],
                                                recv_sem=recv_sems.at[sem], device_id=to, device_id_type=MESH)

        first, passed = [], []
        for a in range(n):
            my_half = _half_rows(ins[a], None, c, halves[a], aligns[a])
            for j, (cx, cy) in enumerate(chips):
                cp = copy(6 * a + j, my_half, _half_rows(outs[a], me, c, halves[a], aligns[a]), (cx, cy, c))
                cp.start()
                first.append(cp)
        for a in range(n):
            for j, (cx, cy) in enumerate(chips):
                landed = _half_rows(outs[a], 2 * cx + cy, c, halves[a], aligns[a])
                copy(6 * a + j, landed, landed, (cx, cy, c)).wait_recv()
                fwd = copy(6 * a + 3 + j, landed, landed, sibling)
                fwd.start()
                passed.append(fwd)
        for a in range(n):
            for j, (cx, cy) in enumerate(chips):
                other = _half_rows(outs[a], 2 * cx + cy, 1 - c, halves[a], aligns[a])
                copy(6 * a + 3 + j, other, other, sibling).wait_recv()
        for cp in first + passed:
            cp.wait_send()

    lands = _comm_call(body, shards, [jax.ShapeDtypeStruct((CHIPS,) + a.shape, a.dtype) for a in shards], 6 * n, name)
    return _with_own(lands, shards)


def _with_own(lands, shards):
    me = 2 * lax.axis_index("x") + lax.axis_index("y")
    return [lax.dynamic_update_slice(g, a[None], (me, 0, 0)) for g, a in zip(lands, shards)]


def _pair_join(arrs, name):
    n = len(arrs)
    halves = [a.shape[0] // 2 for a in arrs]

    def body(*refs):
        outs, (send_sems, recv_sems, _) = refs[n:2 * n], refs[2 * n:]
        x, y, c, _ = _place()
        cps = []
        for a in range(n):
            mine = _half_rows(outs[a], None, c, halves[a], 8)
            cp = pltpu.make_async_remote_copy(src_ref=mine, dst_ref=mine, send_sem=send_sems.at[a], recv_sem=recv_sems.at[a],
                                              device_id=(x, y, 1 - c), device_id_type=MESH)
            cp.start()
            cps.append(cp)
        for cp in cps:
            cp.wait()

    return pl.pallas_call(
        body, name=name, in_specs=[ANY] * n, out_specs=[ANY] * n,
        out_shape=[jax.ShapeDtypeStruct(a.shape, a.dtype) for a in arrs],
        input_output_aliases={i: i for i in range(n)}, scratch_shapes=_sems(n),
        compiler_params=pltpu.CompilerParams(has_side_effects=True),
    )(*arrs)


HBM = pl.BlockSpec(memory_space=pltpu.HBM)
SEM = pl.BlockSpec(memory_space=pltpu.SEMAPHORE)
DATAFLOW = pltpu.SideEffectType.DATAFLOW_SIDE_EFFECTING


def _remote_copies(pairs, ins, lands, send_sems, recv_sems):
    return [pltpu.make_async_remote_copy(src_ref=src, dst_ref=dst, send_sem=send_sems.at[i], recv_sem=recv_sems.at[i],
                                         device_id=to, device_id_type=MESH)
            for i, (src, dst, to) in enumerate(pairs(ins, lands))]


def _split_start(srcs, land_shapes, ncopies, pairs, name, after):
    n, m = len(srcs), len(land_shapes)

    def body(*refs):
        ins, lands = refs[:n], refs[n:n + m]
        send_sems, recv_sems, token = refs[n + m + 1], refs[n + m + 2], refs[-1]
        for cp in _remote_copies(pairs, ins, lands, send_sems, recv_sems):
            cp.start()
        token[...] = jnp.zeros_like(token)

    hbm = lambda a: pltpu.with_memory_space_constraint(a, pltpu.HBM)
    lands = [hbm(lax.empty(s.shape, s.dtype)) for s in land_shapes]
    thru = [pltpu.HBM(a.shape, a.dtype) for a in list(srcs) + lands]
    out = pl.pallas_call(
        body, name=name,
        out_shape=(pltpu.SemaphoreType.DMA((ncopies,)), pltpu.SemaphoreType.DMA((ncopies,)), *thru,
                   jax.ShapeDtypeStruct((8, LANES), F32)),
        in_specs=[HBM] * (n + m) + [ANY], out_specs=(SEM, SEM, *[HBM] * (n + m), pl.BlockSpec(memory_space=pltpu.VMEM)),
        input_output_aliases={i: 2 + i for i in range(n + m)},
        compiler_params=pltpu.CompilerParams(has_side_effects=DATAFLOW),
    )(*[hbm(a) for a in srcs], *lands, after)
    return out[0], out[1], list(out[2:2 + n]), list(out[2 + n:2 + n + m]), out[-1]


def _split_wait(send_sems, recv_sems, srcs, lands, after, pairs, name):
    n, m = len(srcs), len(lands)

    def body(*refs):
        ins, lands_ = refs[:n], refs[n:n + m]
        for cp in _remote_copies(pairs, ins, lands_, refs[n + m], refs[n + m + 1]):
            cp.wait_send()
            cp.wait_recv()

    out = pl.pallas_call(
        body, name=name, out_shape=tuple(pltpu.HBM(a.shape, a.dtype) for a in list(srcs) + list(lands)),
        in_specs=[HBM] * (n + m) + [SEM, SEM, ANY], out_specs=tuple([HBM] * (n + m)),
        input_output_aliases={i: i for i in range(n + m)},
        compiler_params=pltpu.CompilerParams(has_side_effects=DATAFLOW),
    )(*srcs, *lands, send_sems, recv_sems, after)
    return list(out[:n]), list(out[n:])


def _gather_pairs(halves, aligns):
    def pairs(ins, lands):
        x, y, c, chips = _place()
        me = 2 * x + y
        return [(_half_rows(ins[a], None, c, halves[a], aligns[a]), _half_rows(lands[a], me, c, halves[a], aligns[a]),
                 (cx, cy, c)) for a in range(len(ins)) for cx, cy in chips]
    return pairs


PEERS = 7


def _scatter_pairs(ins, lands):
    x, y, c, chips = _place()
    to = [(cx, cy, c) for cx, cy in chips] + [(cx, cy, 1 - c) for cx, cy in chips] + [(x, y, 1 - c)]
    out = []
    for a in range(len(ins)):
        half = ins[a].shape[1] // 2
        for i, (tx, ty, tc) in enumerate(to):
            out.append((_half_rows(ins[a], 2 * tx + ty, tc, half, 8), lands[a].at[i], (tx, ty, tc)))
    return out


def _gather_finish(shards, lands, name):
    n = len(shards)
    halves = [a.shape[0] // 2 for a in shards]
    aligns = [_row_align(a.dtype) for a in shards]

    def body(*refs):
        outs, (send_sems, recv_sems, _) = refs[n:2 * n], refs[2 * n:]
        x, y, c, chips = _place()
        passed = []
        for a in range(n):
            for j, (cx, cy) in enumerate(chips):
                landed = _half_rows(outs[a], 2 * cx + cy, c, halves[a], aligns[a])
                cp = pltpu.make_async_remote_copy(src_ref=landed, dst_ref=landed, send_sem=send_sems.at[3 * a + j],
                                                  recv_sem=recv_sems.at[3 * a + j], device_id=(x, y, 1 - c),
                                                  device_id_type=MESH)
                cp.start()
                passed.append(cp)
        for a in range(n):
            for j, (cx, cy) in enumerate(chips):
                other = _half_rows(outs[a], 2 * cx + cy, 1 - c, halves[a], aligns[a])
                pltpu.make_async_remote_copy(src_ref=other, dst_ref=other, send_sem=send_sems.at[3 * a + j],
                                             recv_sem=recv_sems.at[3 * a + j], device_id=(x, y, 1 - c),
                                             device_id_type=MESH).wait_recv()
        for cp in passed:
            cp.wait_send()

    lands = pl.pallas_call(
        body, name=name, in_specs=[ANY] * n, out_specs=[ANY] * n,
        out_shape=[jax.ShapeDtypeStruct(a.shape, a.dtype) for a in lands],
        input_output_aliases={i: i for i in range(n)}, scratch_shapes=_sems(3 * n),
        compiler_params=pltpu.CompilerParams(has_side_effects=True),
    )(*lands)
    return _with_own(lands, shards)


def _sum_own_and_landed(own, landed, where, name):
    _, half, cols = landed.shape
    tr = _row_tile(half, 128)
    nt = half // tr

    grid_spec = pltpu.PrefetchScalarGridSpec(
        num_scalar_prefetch=1, grid=(nt,),
        in_specs=[pl.BlockSpec((1, tr, cols), lambda r, w: (w[0], w[1] * nt + r, 0)),
                  pl.BlockSpec((PEERS, tr, cols), lambda r, w: (0, r, 0))],
        out_specs=pl.BlockSpec((tr, cols), lambda r, w: (w[1] * nt + r, 0)))

    def body(w_ref, p_ref, q_ref, o_ref):
        acc = p_ref[0]
        for i in range(PEERS):
            acc = acc + q_ref[i]
        o_ref[...] = acc

    return pl.pallas_call(
        body, name=name, grid_spec=grid_spec, out_shape=jax.ShapeDtypeStruct((2 * half, cols), own.dtype),
        compiler_params=_cp(("parallel",)),
    )(where, own, landed)


BIG = [("w_in", (D, IN_W), 1), ("w_q_up", (QL, HEADS * QK), 1), ("w_kv_up", (KVL, HEADS * (NOPE + VH)), 1),
       ("w_out", (D, D), 0), ("w_gate", (D, HID), 1), ("w_up", (D, HID), 1), ("w_down", (HID, D), 0)]
SMALL = [("g_mix_norm", (D,)), ("g_q_lat", (QL,)), ("g_kv_lat", (KVL,)), ("g_q_head", (QK,)), ("g_k_head", (QK,)),
         ("g_sgu_v", (SGU,)), ("w_spatial", (HEADS, CHUNK, CHUNK)), ("b_spatial", (HEADS, CHUNK)),
         ("w_pool", (4, 64, 64)), ("pool_scale", (POOL,)), ("g_out_mla", (512,)), ("g_out_sgu", (SGU,)),
         ("g_out_pool", (POOL,)), ("g_ffn_norm", (D,))]
ORDER = ["g_mix_norm", "w_in", "g_q_lat", "w_q_up", "g_kv_lat", "w_kv_up", "g_q_head", "g_k_head", "g_sgu_v",
         "w_spatial", "b_spatial", "w_pool", "pool_scale", "g_out_mla", "g_out_sgu", "g_out_pool", "w_out",
         "g_ffn_norm", "w_gate", "w_up", "w_down"]
EARLY_BIG = ["w_in", "w_q_up", "w_kv_up"]
FFN_BIG = ["w_gate", "w_up", "w_down"]
LATE_BIG = ["w_out"] + FFN_BIG
DEPTH = 2
COLS = 1024
SMALL_N = sum(math.prod(s) for _, s in SMALL) * DEPTH
assert SMALL_N % CHIPS == 0
SMALL_ROWS = -(-(SMALL_N // CHIPS) // (16 * COLS)) * 16


def _unsplit_cols(g):
    return g.transpose(1, 0, 2).reshape(g.shape[1], CHIPS * g.shape[2])


def _split_cols(full):
    r, c = full.shape
    return full.reshape(r, CHIPS, c // CHIPS).transpose(1, 0, 2)


def _kernel_weights(g):
    win = _unsplit_cols(g["w_in"])
    zeros = lambda r, c: jnp.zeros((r, c), BF16)
    o2, o3, o4 = QL + KVL, QL + KVL + ROPE, QL + KVL + ROPE + 2 * SGU
    win_p = jnp.concatenate([win[:, :o2], zeros(D, NOPE), win[:, o2:o3], zeros(D, HP - QK), win[:, o3:o4], win[:, o4:]], axis=1)
    wq = _unsplit_cols(g["w_q_up"]).reshape(QL, HEADS, QK)
    wq_p = jnp.pad(wq, ((0, 0), (0, 0), (0, HP - QK))).reshape(QL, HEADS * HP)
    wkv = _unsplit_cols(g["w_kv_up"]).reshape(KVL, HEADS, NOPE + VH)
    wk_p = jnp.pad(wkv[:, :, :NOPE], ((0, 0), (0, 0), (0, HP - NOPE))).reshape(KVL, HEADS * HP)
    wv_p = wkv[:, :, NOPE:].reshape(KVL, HEADS * VH)
    return dict(win=win_p, wq=wq_p, wk=wk_p, wv=wv_p)


def _small_operands(p, l):
    row = lambda v: v.reshape(1, -1)
    pad = lambda v: jnp.pad(v, (0, HP - QK)).reshape(1, HP)
    wpool = p["w_pool"][l]
    wbd = jnp.zeros((POOL, POOL), F32)
    for g in range(4):
        wbd = lax.dynamic_update_slice(wbd, wpool[g], (g * 64, g * 64))
    return dict(
        g_mix=row(p["g_mix_norm"][l]), gql=row(p["g_q_lat"][l]), gkv=row(p["g_kv_lat"][l]),
        gq=pad(p["g_q_head"][l]), gk=pad(p["g_k_head"][l]), gsv=row(p["g_sgu_v"][l]),
        wsp=p["w_spatial"][l], bsp=jnp.repeat(p["b_spatial"][l].T, SGU // HEADS, axis=1),
        wbd=wbd.astype(BF16), psc=row(p["pool_scale"][l]),
        gout=jnp.concatenate([p["g_out_mla"][l], p["g_out_sgu"][l], p["g_out_pool"][l]]).reshape(1, D),
        g_ffn=row(p["g_ffn_norm"][l]))


def _big_grads(g):
    dwin = g["win"]
    o2 = QL + KVL
    gin = jnp.concatenate([dwin[:, :o2], dwin[:, o2 + NOPE:o2 + NOPE + ROPE], dwin[:, 512:]], axis=1)
    gq = g["wq"].reshape(QL, HEADS, HP)[:, :, :QK].reshape(QL, HEADS * QK)
    gk = g["wk"].reshape(KVL, HEADS, HP)[:, :, :NOPE]
    gv = g["wv"].reshape(KVL, HEADS, VH)
    gkv = jnp.concatenate([gk, gv], axis=2).reshape(KVL, HEADS * (NOPE + VH))
    return {"w_in": _split_cols(gin), "w_q_up": _split_cols(gq), "w_kv_up": _split_cols(gkv),
            "w_out": g["wout"].reshape(CHIPS, D // CHIPS, D), "w_gate": g["wg"], "w_up": g["wu"], "w_down": g["wd"]}


TRANSPOSED = ("w_gate", "w_up")


def _small_grads(g):
    go = g["gout"].reshape(-1)
    return {"g_mix_norm": g["g_mix"].reshape(-1), "g_q_lat": g["gql"].reshape(-1), "g_kv_lat": g["gkv"].reshape(-1),
            "g_q_head": g["gq"].reshape(-1)[:QK], "g_k_head": g["gk"].reshape(-1)[:QK], "g_sgu_v": g["gsv"].reshape(-1),
            "w_spatial": g["wsp"], "b_spatial": g["bsp"].reshape(CHUNK, HEADS, SGU // HEADS).sum(-1).T,
            "w_pool": jnp.stack([g["wbd"][i * 64:(i + 1) * 64, i * 64:(i + 1) * 64] for i in range(4)]),
            "pool_scale": g["psc"].reshape(-1), "g_out_mla": go[:512], "g_out_sgu": go[512:768],
            "g_out_pool": go[768:], "g_ffn_norm": g["g_ffn"].reshape(-1)}


def _pack_small_grads(small):
    sm = jnp.concatenate([small[l][n].reshape(-1) for l in range(DEPTH) for n, _ in SMALL]).reshape(CHIPS, SMALL_N // CHIPS)
    return jnp.pad(sm, ((0, 0), (0, SMALL_ROWS * COLS - SMALL_N // CHIPS))).reshape(CHIPS, SMALL_ROWS, COLS)


def _unpack_small_grads(gathered):
    flat = gathered.reshape(CHIPS, SMALL_ROWS * COLS)[:, :SMALL_N // CHIPS].reshape(-1)
    out, off = [], 0
    for _ in range(DEPTH):
        layer = {}
        for n, shape in SMALL:
            k = math.prod(shape)
            layer[n] = flat[off:off + k].reshape(shape)
            off += k
        out.append(layer)
    return out


def _layer_fwd(x, tabs, kw, late_weights, sp, l):
    t = f"_l{l}"
    z, hb = _in_proj_fwd(x, sp["g_mix"], kw["win"], "in_proj_fwd" + t)
    q, k, v = _mla_prep_fwd(z, tabs, sp["gql"], sp["gkv"], sp["gq"], sp["gk"], kw["wq"], kw["wk"], kw["wv"],
                            "mla_prep_fwd" + t)
    o, lse = _attn_fwd(q, k, v, "attn_fwd" + t)
    m = _pool_win_fwd(z, "pool_win_fwd" + t)
    wout, wg, wu, wd = late_weights(o)
    wout = wout.reshape(D, D)
    x1, mix = _mix_out_fwd(o, z, m, x, sp["wsp"], sp["bsp"], sp["wbd"], sp["psc"], sp["gsv"], sp["gout"], wout,
                           "mix_out_fwd" + t)
    x2, a, b, h2 = _ffn_fwd(x1, sp["g_ffn"], wg, wu, wd, "ffn_fwd" + t)
    saved = dict(x=x, z=z, hb=hb, q=q, k=k, v=v, o=o, lse=lse, m=m, x1=x1, mix=mix, a=a, b=b, h2=h2, wg=wg, wu=wu, wd=wd,
                 wout=wout)
    return x2, saved


def _layer_bwd(dx2, sv, tabs, kw, sp, l, ffn_hook, out_hook):
    t = f"_l{l}"
    g = {}
    dx1, hid, da, db, dyb, g["g_ffn"] = _ffn_bwd(dx2, sv["x1"], sv["a"], sv["b"], sp["g_ffn"], sv["wg"], sv["wu"],
                                                 sv["wd"], "ffn_bwd" + t)
    g["wd"] = _wgrad_rows(hid, dyb, "wgrad_down" + t)
    g["wg"] = _wgrad_rows(da, sv["h2"], "wgrad_gate" + t)
    g["wu"] = _wgrad_rows(db, sv["h2"], "wgrad_up" + t)
    gout = sp["gout"] + ffn_hook(g)
    do, delta, duv, dm, g["gout"], g["gsv"], g["psc"], g["wsp"], g["bsp"], g["wbd"] = _mix_out_bwd(
        dx1, sv["o"], sv["z"], sv["m"], sp["wsp"], sp["bsp"], sp["wbd"], sp["psc"], sp["gsv"], gout, sv["wout"],
        "mix_out_bwd" + t)
    g["wout"] = _wgrad(sv["mix"], dx1, "wgrad_out" + t)
    dp = _pool_win_bwd(dm, "pool_win_bwd" + t)
    dq, dk, dv = _attn_bwd(sv["q"], sv["k"], sv["v"], do, sv["lse"], delta, out_hook(g), "attn_bwd" + t)
    dzm, qn, kvn, dqr, dkr, dvr, g["gql"], g["gkv"], g["gq"], g["gk"] = _mla_prep_bwd(
        dq, dk, dv, sv["z"], tabs, sp["gql"], sp["gkv"], sp["gq"], sp["gk"], kw["wq"], kw["wk"], kw["wv"],
        "mla_prep_bwd" + t)
    g["wq"] = _wgrad(qn, dqr, "wgrad_q_up" + t)
    g["wk"] = _wgrad(kvn, dkr, "wgrad_k_up" + t)
    g["wv"] = _wgrad(kvn, dvr, "wgrad_v_up" + t)
    dx, g["g_mix"] = _in_proj_bwd(dzm, duv, dp, sv["x"], dx1, sp["g_mix"], kw["win"], "in_proj_bwd" + t)
    g["win"] = _wgrad_in(sv["hb"], dzm, duv, dp, "wgrad_in" + t)
    return dx, g


def _rope_inv_freq():
    half = ROPE // 2
    inv = 1.0 / (ROPE_THETA ** (jnp.arange(half, dtype=F32) / half))
    return jnp.concatenate([jnp.zeros((NOPE,), F32), inv, inv, jnp.zeros((HP - QK,), F32)]).reshape(1, HP)


def kernel(x, positions, g_mix_norm, w_in, g_q_lat, w_q_up, g_kv_lat, w_kv_up, g_q_head, g_k_head, g_sgu_v, w_spatial, b_spatial, w_pool, pool_scale, g_out_mla, g_out_sgu, g_out_pool, w_out, g_ffn_norm, w_gate, w_up, w_down, loss_target, m_g_mix_norm, m_w_in, m_g_q_lat, m_w_q_up, m_g_kv_lat, m_w_kv_up, m_g_q_head, m_g_k_head, m_g_sgu_v, m_w_spatial, m_b_spatial, m_w_pool, m_pool_scale, m_g_out_mla, m_g_out_sgu, m_g_out_pool, m_w_out, m_g_ffn_norm, m_w_gate, m_w_up, m_w_down, v_g_mix_norm, v_w_in, v_g_q_lat, v_w_q_up, v_g_kv_lat, v_w_kv_up, v_g_q_head, v_g_k_head, v_g_sgu_v, v_w_spatial, v_b_spatial, v_w_pool, v_pool_scale, v_g_out_mla, v_g_out_sgu, v_g_out_pool, v_w_out, v_g_ffn_norm, v_w_gate, v_w_up, v_w_down):
    given = dict(locals())
    p = {n: given[n] for n in ORDER}
    view = lambda pre, n: jnp.swapaxes(given[pre + n], 1, 2) if n in TRANSPOSED else given[pre + n]
    seq = x.shape[1]
    where = jnp.stack([2 * lax.axis_index("x") + lax.axis_index("y"), lax.axis_index("c")]).astype(jnp.int32)
    shards = lambda names: [view("", n)[l].astype(BF16) for l, n in names]
    zero11 = lambda token: token[:1, :1]

    names_0a = [(0, n) for n in EARLY_BIG]
    names_0b = [(0, n) for n in LATE_BIG]
    names_1 = [(1, n) for n, _, _ in BIG]
    got_0a = dict(zip(EARLY_BIG, _all_gather_chips(shards(names_0a), "all_gather_w0a")))
    started, issued = {}, got_0a["w_in"]
    for tag, names in (("w0b", names_0b), ("w1", names_1)):
        sh = shards(names)
        pairs = _gather_pairs([a.shape[0] // 2 for a in sh], [_row_align(a.dtype) for a in sh])
        lands = [jax.ShapeDtypeStruct((CHIPS,) + a.shape, a.dtype) for a in sh]
        started[tag] = (sh, pairs) + _split_start(sh, lands, 3 * len(sh), pairs, "gather_start_" + tag, issued)
        issued = started[tag][6]

    def arrived(tag, after):
        _, pairs, send, recv, srcs, lands, _ = started[tag]
        srcs, lands = _split_wait(send, recv, srcs, lands, after, pairs, "gather_wait_" + tag)
        return _gather_finish(srcs, lands, "gather_finish_" + tag)

    layer1 = {}

    def mix_weights(l, h):
        if l == 0:
            return got_0a
        layer1.update(zip([n for _, n in names_1], arrived("w1", h)))
        return layer1

    def late_weights(l, o):
        return arrived("w0b", o) if l == 0 else [layer1[n] for n in LATE_BIG]

    reducing, last = {}, {}

    def reduce_start(tag, arrs):
        lands = [jax.ShapeDtypeStruct((PEERS, a.shape[1] // 2, a.shape[2]), a.dtype) for a in arrs]
        reducing[tag] = _split_start(arrs, lands, PEERS * len(arrs), _scatter_pairs, "grad_scatter_start_" + tag, where)
        return zero11(reducing[tag][4])

    def reduce_finish(tag, after):
        send, recv, srcs, lands, _ = reducing[tag]
        srcs, lands = _split_wait(send, recv, srcs, lands, after, _scatter_pairs, "grad_scatter_wait_" + tag)
        return [_sum_own_and_landed(a, q, where, f"grad_sum_{tag}_{i}") for i, (a, q) in enumerate(zip(srcs, lands))]

    def ffn_hook(l, g):
        if l == 1:
            return jnp.zeros((1, 1), F32)
        return reduce_start("g0b", [g["wg"], g["wu"], g["wd"]])

    def out_hook(l, g):
        if l == 1:
            return where
        reduce_start("g0c", [g["wout"].reshape(CHIPS, D // CHIPS, D)])
        return reducing["g0c"][4]

    def layer_hook(l, big, small):
        last[l] = (big, small)
        if l == 1:
            return reduce_start("g1", [big[n] for n, _, _ in BIG])
        return None

    entry = zero11(started["w0b"][6]) + zero11(started["w1"][6])
    loss_part, dx = _step(x.reshape(seq, D), positions.reshape(seq, 1), loss_target.reshape(seq, D), p, entry,
                          mix_weights, late_weights, ffn_hook, out_hook, layer_hook)
    loss = lax.psum(loss_part, ("x", "y", "c"))

    def adamw(n, g0, g1):
        w = view("", n)
        three_d = (DEPTH, -1, w.shape[-1])
        res = _adamw(w.reshape(three_d), g0.reshape(three_d[1:]), g1.reshape(three_d[1:]),
                     view("m_", n).reshape(three_d), view("v_", n).reshape(three_d), "adamw_" + n)
        return [r.reshape(w.shape) for r in res]

    names_rest = [(0, n) for n in EARLY_BIG]
    reduce_start("g0a", [last[0][0][n] for _, n in names_rest] + [_pack_small_grads([last[l][1] for l in range(DEPTH)])])
    token = reducing["g0a"][4]
    early = names_1 + [(0, n) for n in FFN_BIG] + [(0, "w_out")]
    landed = reduce_finish("g1", token) + reduce_finish("g0b", token) + reduce_finish("g0c", token)
    sums = dict(zip(early, _pair_join(landed, "grad_pair_join_early")))
    out = {n: adamw(n, sums[(0, n)], sums[(1, n)]) for n in FFN_BIG}
    late = names_rest + ["small"]
    sums.update(zip(late, _pair_join(reduce_finish("g0a", out["w_down"][1]), "grad_pair_join_late")))
    gsmall = _unpack_small_grads(_all_gather_chips([sums["small"]], "all_gather_small_grads")[0])
    for n in ORDER:
        if n not in out:
            g = [sums[(l, n)] for l in range(DEPTH)] if (0, n) in sums else [gsmall[l][n] for l in range(DEPTH)]
            out[n] = adamw(n, *g)
    undo = lambda n, a: jnp.swapaxes(a, 1, 2) if n in TRANSPOSED else a
    return (loss, dx.reshape(x.shape), *[undo(n, out[n][i]) for i in range(4) for n in ORDER])


def _step(xs, pos, tgt, p, entry, mix_weights, late_weights, ffn_hook, out_hook, layer_hook):
    sps = [_small_operands(p, l) for l in range(DEPTH)]
    sps[0]["g_mix"] = sps[0]["g_mix"] + entry
    tabs = _rope_tables(pos, _rope_inv_freq())
    saved, h = [], xs
    for l in range(DEPTH):
        kw = _kernel_weights(mix_weights(l, h))
        h, sv = _layer_fwd(h, tabs, kw, functools.partial(late_weights, l), sps[l], l)
        saved.append(dict(sv, kw=kw))
    dy, lpart = _loss_grad(h, tgt)
    for l in reversed(range(DEPTH)):
        dy, g = _layer_bwd(dy, saved[l], tabs, saved[l]["kw"], sps[l], l, functools.partial(ffn_hook, l),
                           functools.partial(out_hook, l))
        zero = layer_hook(l, _big_grads(g), _small_grads(g))
        if zero is not None and l > 0:
            sps[l - 1]["g_ffn"] = sps[l - 1]["g_ffn"] + zero
    return 0.5 / D * jnp.sum(lpart), dy
```

```python
import functools
import math

import jax
import jax.numpy as jnp
from jax import lax
from jax.experimental import pallas as pl
from jax.experimental.pallas import tpu as pltpu

F32 = jnp.float32
BF16 = jnp.bfloat16
MESH = pl.DeviceIdType.MESH

D = 1024
HEADS = 4
QK = 96
NOPE = 64
ROPE = 32
VH = 128
HP = 128
QL = 256
KVL = 128
SGU = 256
POOL = 256
CHUNK = 128
HID = 2816
CHIPS = 4
SH = HID // CHIPS
IN_W = 1184
IN_P = 1280
EPS = 1e-6
ROPE_THETA = 10000.0
SCALE = 1.0 / math.sqrt(QK)
LOG2E = 1.4426950408889634
EXP2_C = SCALE * LOG2E
ATT_WIDE = 2
ATT_FWD_QUERIES = 2048
ATT_PIECE = 1024
ATT_ROWS = 256
ATT_KEYS = 1024
ATT_QUERIES = 2048
NEG = -1e30
HALO = 16

LR, B1, B2, ADAM_EPS, WD, STEP = 0.001, 0.9, 0.999, 1e-08, 0.01, 10

VMEM_LIMIT = 56 * 1024 * 1024
LANES = 128
TOKENS = 1024


def _cp(sem, vmem=None):
    return pltpu.CompilerParams(dimension_semantics=sem, vmem_limit_bytes=vmem)


def _res(shape):
    nd = len(shape)
    return pl.BlockSpec(shape, lambda *_: (0,) * nd, pipeline_mode=pl.Buffered(1))


def _acc(shape):
    nd = len(shape)
    return pl.BlockSpec(shape, lambda *_: (0,) * nd)


def _dot(a, b):
    return jnp.dot(a, b, preferred_element_type=F32)


def _dot_nt(a, b):
    return lax.dot_general(a, b, (((1,), (1,)), ((), ())), preferred_element_type=F32)


def _dot_tn(a, b):
    return lax.dot_general(a, b, (((0,), (0,)), ((), ())), preferred_element_type=F32)


def _rms(x, n):
    r = lax.rsqrt(jnp.sum(x * x, axis=-1, keepdims=True) * (1.0 / n) + EPS)
    return x * r, r


def _rms_bwd(xn, r, g, dy, n):
    dn = dy * g
    dx = r * (dn - xn * (jnp.sum(dn * xn, axis=-1, keepdims=True) * (1.0 / n)))
    return dx, jnp.sum(dy * xn, axis=0, keepdims=True)


def _accumulate(ref, val, first):
    @pl.when(first)
    def _():
        ref[...] = val

    @pl.when(jnp.logical_not(first))
    def _():
        ref[...] += val


def _accumulate0(ref, val, first):
    @pl.when(first)
    def _():
        ref[0] = val

    @pl.when(jnp.logical_not(first))
    def _():
        ref[0] += val


def _tile(s, t):
    return min(s, t)


def _row_tile(r, cap):
    if r <= cap:
        return r
    return max(t for t in range(8, cap + 1, 8) if r % t == 0)


def _rope_tables(pos, invf):
    s = pos.shape[0]
    tm = _tile(s, 1024)

    def body(pos_ref, invf_ref, c_ref, sa_ref, sb_ref):
        ang = pos_ref[...].astype(F32) * invf_ref[...]
        c, sn = jnp.cos(ang), jnp.sin(ang)
        lane = lax.broadcasted_iota(jnp.int32, ang.shape, 1)
        first = (lane >= NOPE) & (lane < NOPE + ROPE // 2)
        second = (lane >= NOPE + ROPE // 2) & (lane < QK)
        c_ref[...] = jnp.where(first | second, c, 1.0)
        sa_ref[...] = jnp.where(first, -sn, 0.0)
        sb_ref[...] = jnp.where(second, sn, 0.0)

    out = jax.ShapeDtypeStruct((s, HP), F32)
    return pl.pallas_call(
        body, name="rope_tables", grid=(s // tm,),
        in_specs=[pl.BlockSpec((tm, 1), lambda i: (i, 0)), _acc((1, HP))],
        out_specs=[pl.BlockSpec((tm, HP), lambda i: (i, 0))] * 3,
        out_shape=[out] * 3, compiler_params=_cp(("parallel",)),
    )(pos, invf)


def _rope(x, c, sa, sb):
    return x * c + pltpu.roll(x, HP - ROPE // 2, 1) * sa + pltpu.roll(x, ROPE // 2, 1) * sb


def _rope_t(d, c, sa, sb):
    return d * c + pltpu.roll(d * sa, ROPE // 2, 1) + pltpu.roll(d * sb, HP - ROPE // 2, 1)


def _in_proj_fwd(x, g, w, name):
    s = x.shape[0]
    tm = _tile(s, TOKENS)

    def body(x_ref, g_ref, w_ref, z_ref, h_ref):
        xn, _ = _rms(x_ref[...], D)
        h = (xn * g_ref[...]).astype(BF16)
        h_ref[...] = h
        z_ref[...] = _dot(h, w_ref[...])

    return pl.pallas_call(
        body, name=name, grid=(s // tm,),
        in_specs=[pl.BlockSpec((tm, D), lambda i: (i, 0)), _acc((1, D)), _res((D, IN_P))],
        out_specs=[pl.BlockSpec((tm, IN_P), lambda i: (i, 0)), pl.BlockSpec((tm, D), lambda i: (i, 0))],
        out_shape=[jax.ShapeDtypeStruct((s, IN_P), F32), jax.ShapeDtypeStruct((s, D), BF16)],
        compiler_params=_cp(("parallel",), VMEM_LIMIT),
    )(x, g, w)


def _mla_prep_fwd(z, tabs, gql, gkv, gq, gk, wq, wk, wv, name):
    s = z.shape[0]
    tm = _tile(s, TOKENS)

    def body(ql_ref, kv_ref, kr_ref, c_ref, sa_ref, sb_ref, gql_ref, gkv_ref, gq_ref, gk_ref,
             wq_ref, wk_ref, wv_ref, q_out, k_out, v_out):
        qn = (_rms(ql_ref[...], QL)[0] * gql_ref[...]).astype(BF16)
        kvn = (_rms(kv_ref[...], KVL)[0] * gkv_ref[...]).astype(BF16)
        qraw = _dot(qn, wq_ref[...])
        kraw = _dot(kvn, wk_ref[...])
        vraw = _dot(kvn, wv_ref[...])
        kr = kr_ref[...]
        c, sa, sb = c_ref[...], sa_ref[...], sb_ref[...]
        for h in range(HEADS):
            sl = slice(h * HP, (h + 1) * HP)
            xq = _rms(qraw[:, sl], QK)[0] * gq_ref[...]
            q_out[h] = _rope(xq, c, sa, sb).astype(BF16)
            xk = _rms(kraw[:, sl] + kr, QK)[0] * gk_ref[...]
            k_out[h] = _rope(xk, c, sa, sb).astype(BF16)
            v_out[h] = vraw[:, sl].astype(BF16)

    row = lambda w, j: pl.BlockSpec((tm, w), lambda i: (i, j))
    hspec = pl.BlockSpec((HEADS, tm, HP), lambda i: (0, i, 0))
    hshape = jax.ShapeDtypeStruct((HEADS, s, HP), BF16)
    return pl.pallas_call(
        body, name=name, grid=(s // tm,),
        in_specs=[row(QL, 0), row(KVL, 2), row(HP, 3), row(HP, 0), row(HP, 0), row(HP, 0),
                  _acc((1, QL)), _acc((1, KVL)), _acc((1, HP)), _acc((1, HP)),
                  _acc((QL, HEADS * HP)), _acc((KVL, HEADS * HP)), _acc((KVL, HEADS * HP))],
        out_specs=[hspec] * 3, out_shape=[hshape] * 3,
        compiler_params=_cp(("parallel",)),
    )(z, z, z, *tabs, gql, gkv, gq, gk, wq, wk, wv)


def _causal_mask(s, row0):
    row = lax.broadcasted_iota(jnp.int32, s.shape, 0) + row0
    col = lax.broadcasted_iota(jnp.int32, s.shape, 1)
    return jnp.where(col <= row, s, NEG)


def _attn_fwd(q, k, v, name):
    s = q.shape[1]
    tq = _tile(s, ATT_FWD_QUERIES)
    rh = _tile(s, ATT_ROWS)
    kp = _tile(s, ATT_PIECE)
    wide = ATT_WIDE * kp if s % (ATT_WIDE * kp) == 0 else tq
    groups = tq // rh

    def body(q_ref, k_ref, v_ref, o_ref, lse_ref):
        i = pl.program_id(1)

        def blk(off, tk, carry, diagonal):
            width = lambda g, t: max(0, min(kp, (g + 1) * rh - t * kp)) if diagonal else kp
            rows = lambda t: pl.ds(pl.multiple_of(off + t * kp, kp), kp)
            score = lambda g, t: _dot_nt(q_ref[0, g * rh:(g + 1) * rh, :], k_ref[0, rows(t), :][:width(g, t)])
            live = lambda t: [g for g in range(groups) if width(g, t) > 0]
            state = list(carry)
            scs = {(g, 0): score(g, 0) for g in live(0)}
            for t in range(tk // kp):
                if (t + 1) * kp < tk:
                    scs.update({(g, t + 1): score(g, t + 1) for g in live(t + 1)})
                vt = v_ref[0, rows(t), :]
                for g in live(t):
                    m, l, acc = state[g]
                    sc = scs.pop((g, t))
                    if diagonal and (g + 1) * rh <= (t + 1) * kp:
                        sc = _causal_mask(sc, g * rh - t * kp)
                    m_new = jnp.maximum(m, jnp.max(sc, axis=-1, keepdims=True))
                    p = jnp.exp2((sc - m_new) * EXP2_C)
                    alpha = jnp.exp2((m - m_new) * EXP2_C)
                    l = alpha * l + jnp.sum(p, axis=-1, keepdims=True)
                    acc = alpha * acc + _dot(p.astype(BF16), vt[:width(g, t)])
                    state[g] = (m_new, l, acc)
            return tuple(state)

        one = (jnp.full((rh, 1), NEG, F32), jnp.zeros((rh, 1), F32), jnp.zeros((rh, VH), F32))
        nwide = (i * tq) // wide
        carry = lax.fori_loop(0, nwide, lambda j, c: blk(j * wide, wide, c, False), (one,) * groups)
        carry = lax.fori_loop(nwide * (wide // tq), i, lambda j, c: blk(j * tq, tq, c, False), carry)
        carry = blk(i * tq, tq, carry, True)
        for g, (m, l, acc) in enumerate(carry):
            o_ref[g * rh:(g + 1) * rh, :] = acc / l
            lse_ref[0, g * rh:(g + 1) * rh, :] = jnp.broadcast_to(m * EXP2_C + jnp.log(l) * LOG2E, (rh, LANES))

    return pl.pallas_call(
        body, name=name, grid=(HEADS, s // tq),
        in_specs=[pl.BlockSpec((1, tq, HP), lambda h, i: (h, i, 0)),
                  pl.BlockSpec((1, s, HP), lambda h, i: (h, 0, 0)),
                  pl.BlockSpec((1, s, HP), lambda h, i: (h, 0, 0))],
        out_specs=[pl.BlockSpec((tq, VH), lambda h, i: (i, h)),
                   pl.BlockSpec((1, tq, LANES), lambda h, i: (h, i, 0))],
        out_shape=[jax.ShapeDtypeStruct((s, HEADS * VH), F32), jax.ShapeDtypeStruct((HEADS, s, LANES), F32)],
        compiler_params=_cp(("parallel", "arbitrary"), VMEM_LIMIT),
    )(q, k, v)


def _lane_group(shape, j):
    return (lax.broadcasted_iota(jnp.int32, shape, 1) + j * LANES) // (POOL // 4)


def _pool_win_fwd(z, name):
    s = z.shape[0]
    ch = _tile(s, 512)
    col0 = (IN_P - POOL) // LANES

    def body(p_ref, m_ref):
        j = pl.program_id(0)

        def chunk(r, _):
            off = pl.multiple_of(r * ch, ch)
            cur = p_ref[pl.ds(off, ch), :]
            hoff = pl.multiple_of(jnp.maximum(off - HALO, 0), 8)
            halo = jnp.where(r > 0, p_ref[pl.ds(hoff, HALO), :], 0.0)
            x = jnp.concatenate([halo, cur], axis=0)
            s2 = x + pltpu.roll(x, 1, 0)
            s4 = s2 + pltpu.roll(s2, 2, 0)
            s8 = s4 + pltpu.roll(s4, 4, 0)
            s16 = s8 + pltpu.roll(s8, 8, 0)
            grp = _lane_group((ch, LANES), j)
            sel = jnp.where(grp == 0, s2[HALO:], jnp.where(grp == 1, s4[HALO:], jnp.where(grp == 2, s8[HALO:], s16[HALO:])))
            t1 = (lax.broadcasted_iota(jnp.int32, (ch, LANES), 0) + off + 1).astype(F32)
            win = jnp.where(grp == 0, 2.0, jnp.where(grp == 1, 4.0, jnp.where(grp == 2, 8.0, 16.0)))
            m_ref[pl.ds(off, ch), :] = sel / jnp.minimum(t1, win) - cur
            return 0

        lax.fori_loop(0, s // ch, chunk, 0)

    return pl.pallas_call(
        body, name=name, grid=(POOL // LANES,),
        in_specs=[pl.BlockSpec((s, LANES), lambda j: (0, col0 + j))],
        out_specs=pl.BlockSpec((s, LANES), lambda j: (0, j)),
        out_shape=jax.ShapeDtypeStruct((s, POOL), F32),
        compiler_params=_cp(("parallel",), VMEM_LIMIT),
    )(z)


def _pool_win_bwd(dm, name):
    s = dm.shape[0]
    ch = _tile(s, 512)
    n = s // ch

    def body(dm_ref, dp_ref):
        j = pl.program_id(0)

        def chunk(r, _):
            off = pl.multiple_of(r * ch, ch)
            grp = _lane_group((ch + HALO, LANES), j)
            win = jnp.where(grp == 0, 2.0, jnp.where(grp == 1, 4.0, jnp.where(grp == 2, 8.0, 16.0)))
            cur = dm_ref[pl.ds(off, ch), :]
            hoff = pl.multiple_of(jnp.minimum(off + ch, s - HALO), 8)
            halo = jnp.where(r < n - 1, dm_ref[pl.ds(hoff, HALO), :], 0.0)
            x = jnp.concatenate([cur, halo], axis=0)
            t1 = (lax.broadcasted_iota(jnp.int32, (ch + HALO, LANES), 0) + off + 1).astype(F32)
            e = x / jnp.minimum(t1, win)
            tot = ch + HALO
            r2 = e + pltpu.roll(e, tot - 1, 0)
            r4 = r2 + pltpu.roll(r2, tot - 2, 0)
            r8 = r4 + pltpu.roll(r4, tot - 4, 0)
            r16 = r8 + pltpu.roll(r8, tot - 8, 0)
            g = grp[:ch]
            sel = jnp.where(g == 0, r2[:ch], jnp.where(g == 1, r4[:ch], jnp.where(g == 2, r8[:ch], r16[:ch])))
            dp_ref[pl.ds(off, ch), :] = (sel - cur).astype(BF16)
            return 0

        lax.fori_loop(0, n, chunk, 0)

    return pl.pallas_call(
        body, name=name, grid=(POOL // LANES,),
        in_specs=[pl.BlockSpec((s, LANES), lambda j: (0, j))],
        out_specs=pl.BlockSpec((s, LANES), lambda j: (0, j)),
        out_shape=jax.ShapeDtypeStruct((s, POOL), BF16),
        compiler_params=_cp(("parallel",), VMEM_LIMIT),
    )(dm)


def _head_mask(h):
    lane = lax.broadcasted_iota(jnp.int32, (CHUNK, SGU), 1)
    return (lane // (SGU // HEADS)) == h


def _tril(upper=False):
    row = lax.broadcasted_iota(jnp.int32, (CHUNK, CHUNK), 0)
    col = lax.broadcasted_iota(jnp.int32, (CHUNK, CHUNK), 1)
    return col >= row if upper else col <= row


def _sgu_gate(vn, wsp, bsp):
    out = []
    for cidx in range(vn.shape[0] // CHUNK):
        vc = vn[cidx * CHUNK:(cidx + 1) * CHUNK]
        zc = bsp
        for h in range(HEADS):
            zc = zc + jnp.where(_head_mask(h), _dot(wsp[h], vc), 0.0)
        out.append(zc)
    return jnp.concatenate(out, axis=0)


def _mix_out_fwd(o, z, m, x, wsp, bsp, wbd, psc, gsv, gout, wout, name):
    s = x.shape[0]
    tm = _tile(s, TOKENS)

    def body(o_ref, uv_ref, m_ref, x_ref, wsp_ref, bsp_ref, wbd_ref, psc_ref, gsv_ref, gout_ref, wout_ref,
             x1_ref, mix_ref):
        g = gout_ref[...]
        an = _rms(o_ref[...], HEADS * VH)[0] * g[:, :512]
        uv = uv_ref[...]
        u, v = uv[:, :SGU], uv[:, SGU:]
        vn = (_rms(v, SGU)[0] * gsv_ref[...]).astype(BF16)
        tri = _tril()
        wsp_m = [jnp.where(tri, wsp_ref[h], 0.0).astype(BF16) for h in range(HEADS)]
        gm = u * _sgu_gate(vn, wsp_m, bsp_ref[...])
        gn = _rms(gm, SGU)[0] * g[:, 512:768]
        po = _dot(m_ref[...].astype(BF16), wbd_ref[...]) * psc_ref[...]
        pn = _rms(po, POOL)[0] * g[:, 768:]
        mix = jnp.concatenate([an, gn, pn], axis=1).astype(BF16)
        mix_ref[...] = mix
        x1_ref[...] = x_ref[...] + _dot(mix, wout_ref[...])

    row = lambda w, j: pl.BlockSpec((tm, w), lambda i: (i, j))
    return pl.pallas_call(
        body, name=name, grid=(s // tm,),
        in_specs=[row(512, 0), row(512, 1), row(POOL, 0), row(D, 0),
                  _acc((HEADS, CHUNK, CHUNK)), _acc((CHUNK, SGU)), _acc((POOL, POOL)), _acc((1, POOL)),
                  _acc((1, SGU)), _acc((1, D)), _res((D, D))],
        out_specs=[row(D, 0), row(D, 0)],
        out_shape=[jax.ShapeDtypeStruct((s, D), F32), jax.ShapeDtypeStruct((s, D), BF16)],
        compiler_params=_cp(("parallel",), VMEM_LIMIT),
    )(o, z, m, x, wsp, bsp, wbd, psc, gsv, gout, wout)


def _ffn_fwd(x1, g, wg, wu, wd, name):
    s = x1.shape[0]
    tm = _tile(s, 256)

    def body(x_ref, g_ref, wg_ref, wu_ref, wd_ref, x2_ref, a_ref, b_ref, h_ref):
        x = x_ref[...]
        h = (_rms(x, D)[0] * g_ref[...]).astype(BF16)
        h_ref[...] = h
        acc = jnp.zeros((tm, D), F32)
        for k in range(CHIPS):
            a = _dot_nt(h, wg_ref[k])
            b = _dot_nt(h, wu_ref[k])
            a_ref[k] = a
            b_ref[k] = b
            acc = acc + _dot((a * jax.nn.sigmoid(a) * b).astype(BF16), wd_ref[k])
        x2_ref[...] = x + acc

    row = lambda w: pl.BlockSpec((tm, w), lambda i: (i, 0))
    hrow = pl.BlockSpec((CHIPS, tm, SH), lambda i: (0, i, 0))
    hshape = jax.ShapeDtypeStruct((CHIPS, s, SH), F32)
    return pl.pallas_call(
        body, name=name, grid=(s // tm,),
        in_specs=[row(D), _acc((1, D)), _res((CHIPS, SH, D)), _res((CHIPS, SH, D)), _res((CHIPS, SH, D))],
        out_specs=[row(D), hrow, hrow, row(D)],
        out_shape=[jax.ShapeDtypeStruct((s, D), F32), hshape, hshape, jax.ShapeDtypeStruct((s, D), BF16)],
        compiler_params=_cp(("parallel",), VMEM_LIMIT),
    )(x1, g, wg, wu, wd)


def _loss_grad(y, tgt):
    s = y.shape[0]
    tm = _tile(s, TOKENS)

    def body(y_ref, t_ref, dy_ref, l_ref):
        e = y_ref[...] - t_ref[...]
        dy_ref[...] = e * (1.0 / D)
        sq = jnp.sum(e * e, axis=0, keepdims=True)
        part = sq[:, :LANES]
        for c in range(1, D // LANES):
            part = part + sq[:, c * LANES:(c + 1) * LANES]
        _accumulate(l_ref, part, pl.program_id(0) == 0)

    row = pl.BlockSpec((tm, D), lambda i: (i, 0))
    return pl.pallas_call(
        body, name="loss_grad", grid=(s // tm,),
        in_specs=[row, row], out_specs=[row, _acc((1, LANES))],
        out_shape=[jax.ShapeDtypeStruct((s, D), F32), jax.ShapeDtypeStruct((1, LANES), F32)],
        compiler_params=_cp(("arbitrary",)),
    )(y, tgt)


def _wgrad(a, b, name):
    s, k = a.shape
    n = b.shape[1]
    half = lambda v: v if v <= 1408 else v // 2
    kb, nb, tt = half(k), half(n), _tile(s, 2048)

    def body(a_ref, b_ref, o_ref):
        _accumulate(o_ref, _dot_tn(a_ref[...].astype(BF16), b_ref[...].astype(BF16)), pl.program_id(2) == 0)

    return pl.pallas_call(
        body, name=name, grid=(k // kb, n // nb, s // tt),
        in_specs=[pl.BlockSpec((tt, kb), lambda i, j, t: (t, i)), pl.BlockSpec((tt, nb), lambda i, j, t: (t, j))],
        out_specs=pl.BlockSpec((kb, nb), lambda i, j, t: (i, j)),
        out_shape=jax.ShapeDtypeStruct((k, n), F32),
        compiler_params=_cp(("parallel", "parallel", "arbitrary"), VMEM_LIMIT),
    )(a, b)


def _wgrad_in(h, dzm, duv, dp, name):
    s = h.shape[0]
    tt = _tile(s, 2048)

    def body(h_ref, a_ref, b_ref, c_ref, o_ref):
        hv = h_ref[...]
        val = jnp.concatenate([_dot_tn(hv, a_ref[...]), _dot_tn(hv, b_ref[...]), _dot_tn(hv, c_ref[...])], axis=1)
        _accumulate(o_ref, val, pl.program_id(0) == 0)

    row = lambda w: pl.BlockSpec((tt, w), lambda t: (t, 0))
    return pl.pallas_call(
        body, name=name, grid=(s // tt,), in_specs=[row(D), row(512), row(512), row(POOL)], out_specs=_acc((D, IN_P)),
        out_shape=jax.ShapeDtypeStruct((D, IN_P), F32), compiler_params=_cp(("arbitrary",), VMEM_LIMIT),
    )(h, dzm, duv, dp)


def _wgrad_rows(a, b, name):
    s, n = a.shape[1:]
    nn = b.shape[1]
    tt = _tile(s, 4096 if b.dtype == BF16 else 2048)

    def body(a_ref, b_ref, o_ref):
        _accumulate0(o_ref, _dot_tn(a_ref[0].astype(BF16), b_ref[...].astype(BF16)), pl.program_id(1) == 0)

    return pl.pallas_call(
        body, name=name, grid=(CHIPS, s // tt),
        in_specs=[pl.BlockSpec((1, tt, n), lambda c, t: (c, t, 0)), pl.BlockSpec((tt, nn), lambda c, t: (t, 0))],
        out_specs=pl.BlockSpec((1, n, nn), lambda c, t: (c, 0, 0)),
        out_shape=jax.ShapeDtypeStruct((CHIPS, n, nn), F32),
        compiler_params=_cp(("parallel", "arbitrary"), VMEM_LIMIT),
    )(a, b)


def _ffn_bwd(dx2, x1, a, b, g, wg, wu, wd, name):
    s = x1.shape[0]
    tm = _tile(s, 256)

    def body(dx2_ref, x_ref, a_ref, b_ref, g_ref, wg_ref, wu_ref, wd_ref,
             dx1_ref, hid_ref, da_ref, db_ref, dyb_ref, dg_ref):
        dx2 = dx2_ref[...]
        dyb = dx2.astype(BF16)
        dyb_ref[...] = dyb
        dh = jnp.zeros((tm, D), F32)
        ahead = _dot_nt(dyb, wd_ref[0])
        for k in range(CHIPS):
            av, bv = a_ref[k], b_ref[k]
            dhid = ahead
            if k + 1 < CHIPS:
                ahead = _dot_nt(dyb, wd_ref[k + 1])
            sig = jax.nn.sigmoid(av)
            sa = av * sig
            hid_ref[k] = (sa * bv).astype(BF16)
            dbv = (dhid * sa).astype(BF16)
            dav = (dhid * bv * (sig * (1.0 + av * (1.0 - sig)))).astype(BF16)
            db_ref[k] = dbv
            da_ref[k] = dav
            dh = dh + _dot(dav, wg_ref[k]) + _dot(dbv, wu_ref[k])
        xn, r = _rms(x_ref[...], D)
        dxr, dg = _rms_bwd(xn, r, g_ref[...], dh, D)
        dx1_ref[...] = dx2 + dxr
        _accumulate(dg_ref, dg, pl.program_id(0) == 0)

    row = lambda w: pl.BlockSpec((tm, w), lambda i: (i, 0))
    hrow = pl.BlockSpec((CHIPS, tm, SH), lambda i: (0, i, 0))
    hid = jax.ShapeDtypeStruct((CHIPS, s, SH), BF16)
    return pl.pallas_call(
        body, name=name, grid=(s // tm,),
        in_specs=[row(D), row(D), hrow, hrow, _acc((1, D)), _res((CHIPS, SH, D)), _res((CHIPS, SH, D)),
                  _res((CHIPS, SH, D))],
        out_specs=[row(D), hrow, hrow, hrow, row(D), _acc((1, D))],
        out_shape=[jax.ShapeDtypeStruct((s, D), F32), hid, hid, hid, jax.ShapeDtypeStruct((s, D), BF16),
                   jax.ShapeDtypeStruct((1, D), F32)],
        compiler_params=_cp(("arbitrary",), VMEM_LIMIT),
    )(dx2, x1, a, b, g, wg, wu, wd)


def _mix_out_bwd(dx1, o, z, m, wsp, bsp, wbd, psc, gsv, gout, wout, name):
    s = dx1.shape[0]
    tm = _tile(s, TOKENS)

    def body(dx1_ref, o_ref, uv_ref, m_ref, wsp_ref, bsp_ref, wbd_ref, psc_ref, gsv_ref, gout_ref, wout_ref,
             do_ref, dl_ref, duv_ref, dm_ref, dgo_ref, dgsv_ref, dpsc_ref, dwsp_ref, dbsp_ref, dwbd_ref):
        first = pl.program_id(0) == 0
        g = gout_ref[...]
        dmix = _dot_nt(dx1_ref[...].astype(BF16), wout_ref[...])
        o = o_ref[...]
        on, ro = _rms(o, HEADS * VH)
        do, dga = _rms_bwd(on, ro, g[:, :512], dmix[:, :512], HEADS * VH)
        for h in range(HEADS):
            sl = slice(h * VH, (h + 1) * VH)
            do_ref[h] = do[:, sl].astype(BF16)
            dl_ref[h] = jnp.broadcast_to(jnp.sum(do[:, sl] * o[:, sl], axis=-1, keepdims=True), (tm, LANES))
        uv = uv_ref[...]
        u, v = uv[:, :SGU], uv[:, SGU:]
        vx, rv = _rms(v, SGU)
        vn = (vx * gsv_ref[...]).astype(BF16)
        tri = _tril()
        wsp_m = [jnp.where(tri, wsp_ref[h], 0.0).astype(BF16) for h in range(HEADS)]
        zc = _sgu_gate(vn, wsp_m, bsp_ref[...])
        gm = u * zc
        gmn, rg = _rms(gm, SGU)
        dgm, dgg = _rms_bwd(gmn, rg, g[:, 512:768], dmix[:, 512:768], SGU)
        du = dgm * zc
        dzc = dgm * u
        dvn_parts = []
        dbsp = jnp.zeros((CHUNK, SGU), F32)
        dwsp = [jnp.zeros((CHUNK, CHUNK), F32) for _ in range(HEADS)]
        for cidx in range(tm // CHUNK):
            rs = slice(cidx * CHUNK, (cidx + 1) * CHUNK)
            dzc_c = dzc[rs]
            dbsp = dbsp + dzc_c
            dzb = dzc_c.astype(BF16)
            vc = vn[rs]
            dvn_c = jnp.zeros((CHUNK, SGU), F32)
            for h in range(HEADS):
                hm = _head_mask(h)
                dvn_c = dvn_c + jnp.where(hm, _dot_tn(wsp_m[h], dzb), 0.0)
                dwsp[h] = dwsp[h] + _dot_nt(jnp.where(hm, dzc_c, 0.0).astype(BF16), vc)
            dvn_parts.append(dvn_c)
        dvn = jnp.concatenate(dvn_parts, axis=0)
        dv, dgsv = _rms_bwd(vx, rv, gsv_ref[...], dvn, SGU)
        duv_ref[...] = jnp.concatenate([du, dv], axis=1).astype(BF16)
        mb = m_ref[...].astype(BF16)
        pw = _dot(mb, wbd_ref[...])
        po = pw * psc_ref[...]
        pon, rp = _rms(po, POOL)
        dpo, dgp = _rms_bwd(pon, rp, g[:, 768:], dmix[:, 768:], POOL)
        dpw = (dpo * psc_ref[...]).astype(BF16)
        dm_ref[...] = _dot_nt(dpw, wbd_ref[...])
        _accumulate(dgo_ref, jnp.concatenate([dga, dgg, dgp], axis=1), first)
        _accumulate(dgsv_ref, dgsv, first)
        _accumulate(dpsc_ref, jnp.sum(dpo * pw, axis=0, keepdims=True), first)
        _accumulate(dbsp_ref, dbsp, first)
        _accumulate(dwbd_ref, _dot_tn(mb, dpw), first)
        for h in range(HEADS):
            val = jnp.where(tri, dwsp[h], 0.0)

            @pl.when(first)
            def _(val=val, h=h):
                dwsp_ref[h] = val

            @pl.when(jnp.logical_not(first))
            def _(val=val, h=h):
                dwsp_ref[h] += val

    row = lambda w, j: pl.BlockSpec((tm, w), lambda i: (i, j))
    hspec = pl.BlockSpec((HEADS, tm, HP), lambda i: (0, i, 0))
    return pl.pallas_call(
        body, name=name, grid=(s // tm,),
        in_specs=[row(D, 0), row(512, 0), row(512, 1), row(POOL, 0),
                  _acc((HEADS, CHUNK, CHUNK)), _acc((CHUNK, SGU)),
                  _acc((POOL, POOL)), _acc((1, POOL)), _acc((1, SGU)), _acc((1, D)), _res((D, D))],
        out_specs=[hspec, hspec, row(512, 0), row(POOL, 0), _acc((1, D)), _acc((1, SGU)), _acc((1, POOL)),
                   _acc((HEADS, CHUNK, CHUNK)), _acc((CHUNK, SGU)), _acc((POOL, POOL))],
        out_shape=[jax.ShapeDtypeStruct((HEADS, s, HP), BF16), jax.ShapeDtypeStruct((HEADS, s, LANES), F32),
                   jax.ShapeDtypeStruct((s, 512), BF16), jax.ShapeDtypeStruct((s, POOL), F32),
                   jax.ShapeDtypeStruct((1, D), F32), jax.ShapeDtypeStruct((1, SGU), F32),
                   jax.ShapeDtypeStruct((1, POOL), F32), jax.ShapeDtypeStruct((HEADS, CHUNK, CHUNK), F32),
                   jax.ShapeDtypeStruct((CHUNK, SGU), F32), jax.ShapeDtypeStruct((POOL, POOL), F32)],
        compiler_params=_cp(("arbitrary",), VMEM_LIMIT),
    )(dx1, o, z, m, wsp, bsp, wbd, psc, gsv, gout, wout)


def _attn_bwd(q, k, v, do, lse, delta, after, name):
    s = q.shape[1]
    rh = _tile(s, ATT_ROWS)
    tk = _tile(s, ATT_KEYS)
    nk = s // tk
    wide = ATT_QUERIES if s % ATT_QUERIES == 0 else tk
    pieces = tk // rh

    def body(q_ref, k_ref, v_ref, do_ref, lse_ref, dl_ref, after_ref, dq_ref, dk_ref, dv_ref):
        del after_ref
        j = pl.program_id(1)

        @pl.when(j == 0)
        def _():
            dq_ref[...] = jnp.zeros_like(dq_ref)

        kj, vj = k_ref[0], v_ref[0]

        def blk(start, rows, dks, dvs, diagonal):
            dks, dvs = list(dks), list(dvs)
            offs = [pl.multiple_of(start + g * rh, rh) for g in range(rows // rh)]
            keys = [(g + 1) * rh if diagonal else tk for g in range(rows // rh)]
            qs = [q_ref[0, pl.ds(off, rh), :] for off in offs]
            dos = [do_ref[0, pl.ds(off, rh), :] for off in offs]
            scs = [_dot_nt(qi, kj[:n]) for qi, n in zip(qs, keys)]
            dps = [_dot_nt(doi, vj[:n]) for doi, n in zip(dos, keys)]
            for g, off in enumerate(offs):
                lse_i = lse_ref[0, pl.ds(off, rh), :][:, :1]
                dl_i = dl_ref[0, pl.ds(off, rh), :][:, :1]
                sc = _causal_mask(scs[g], g * rh) if diagonal else scs[g]
                p = jnp.exp2(sc * EXP2_C - lse_i)
                ds = (p * (dps[g] - dl_i)).astype(BF16)
                cv = _dot_tn(p.astype(BF16), dos[g])
                ck = _dot_tn(ds, qs[g])
                for t in range(keys[g] // rh):
                    dvs[t] = dvs[t] + cv[t * rh:(t + 1) * rh]
                    dks[t] = dks[t] + ck[t * rh:(t + 1) * rh]
                dq_ref[0, pl.ds(off, rh), :] += _dot(ds, kj[:keys[g]]) * SCALE
            return tuple(dks), tuple(dvs)

        per = wide // tk
        zero = (jnp.zeros((rh, HP), F32),) * pieces
        acc = blk(j * tk, tk, zero, zero, True)
        first_wide = (j + per) // per
        acc = lax.fori_loop(j + 1, jnp.minimum(first_wide * per, nk), lambda i, c: blk(i * tk, tk, *c, False), acc)
        dks, dvs = lax.fori_loop(first_wide, nk // per, lambda i, c: blk(i * wide, wide, *c, False), acc)
        dk_ref[0] = jnp.concatenate(dks, axis=0) * SCALE
        dv_ref[0] = jnp.concatenate(dvs, axis=0)

    full = lambda: pl.BlockSpec((1, s, HP), lambda h, j: (h, 0, 0))
    blk_spec = lambda: pl.BlockSpec((1, tk, HP), lambda h, j: (h, j, 0))
    out = jax.ShapeDtypeStruct((HEADS, s, HP), F32)
    return pl.pallas_call(
        body, name=name, grid=(HEADS, s // tk),
        in_specs=[full(), blk_spec(), blk_spec(), full(), full(), full(), ANY],
        out_specs=[full(), blk_spec(), blk_spec()], out_shape=[out] * 3,
        compiler_params=_cp(("parallel", "arbitrary"), VMEM_LIMIT),
    )(q, k, v, do, lse, delta, after)


def _mla_prep_bwd(dq, dk, dv, z, tabs, gql, gkv, gq, gk, wq, wk, wv, name):
    s = z.shape[0]
    tm = _tile(s, TOKENS)

    def body(dq_ref, dk_ref, dv_ref, ql_ref, kv_ref, kr_ref, c_ref, sa_ref, sb_ref, gql_ref, gkv_ref, gq_ref, gk_ref,
             wq_ref, wk_ref, wv_ref,
             dz_ref, qn_ref, kvn_ref, dqr_ref, dkr_ref, dvr_ref, dgql_ref, dgkv_ref, dgq_ref, dgk_ref):
        first = pl.program_id(0) == 0
        qx, rq = _rms(ql_ref[...], QL)
        qn = (qx * gql_ref[...]).astype(BF16)
        kx, rk = _rms(kv_ref[...], KVL)
        kvn = (kx * gkv_ref[...]).astype(BF16)
        qn_ref[...] = qn
        kvn_ref[...] = kvn
        qraw = _dot(qn, wq_ref[...])
        kraw = _dot(kvn, wk_ref[...])
        kr = kr_ref[...]
        c, sa, sb = c_ref[...], sa_ref[...], sb_ref[...]
        lane = lax.broadcasted_iota(jnp.int32, (tm, HP), 1)
        rope_lanes = (lane >= NOPE) & (lane < QK)
        dkrope = jnp.zeros((tm, HP), F32)
        dgq = jnp.zeros((1, HP), F32)
        dgk = jnp.zeros((1, HP), F32)
        for h in range(HEADS):
            sl = slice(h * HP, (h + 1) * HP)
            xn, r = _rms(qraw[:, sl], QK)
            dx, dg = _rms_bwd(xn, r, gq_ref[...], _rope_t(dq_ref[h], c, sa, sb), QK)
            dqr_ref[:, sl] = dx.astype(BF16)
            dgq = dgq + dg
            xn, r = _rms(kraw[:, sl] + kr, QK)
            dx, dg = _rms_bwd(xn, r, gk_ref[...], _rope_t(dk_ref[h], c, sa, sb), QK)
            dkr_ref[:, sl] = dx.astype(BF16)
            dgk = dgk + dg
            dkrope = dkrope + jnp.where(rope_lanes, dx, 0.0)
            dvr_ref[:, sl] = dv_ref[h].astype(BF16)
        dqn = _dot_nt(dqr_ref[...], wq_ref[...])
        dql, dgql = _rms_bwd(qx, rq, gql_ref[...], dqn, QL)
        dkvn = _dot_nt(dkr_ref[...], wk_ref[...]) + _dot_nt(dvr_ref[...], wv_ref[...])
        dkv, dgkv = _rms_bwd(kx, rk, gkv_ref[...], dkvn, KVL)
        dz_ref[...] = jnp.concatenate([dql, dkv, dkrope], axis=1).astype(BF16)
        _accumulate(dgql_ref, dgql, first)
        _accumulate(dgkv_ref, dgkv, first)
        _accumulate(dgq_ref, dgq, first)
        _accumulate(dgk_ref, dgk, first)

    row = lambda w, j: pl.BlockSpec((tm, w), lambda i: (i, j))
    hspec = pl.BlockSpec((HEADS, tm, HP), lambda i: (0, i, 0))
    sd = lambda w, dt: jax.ShapeDtypeStruct((s, w), dt)
    return pl.pallas_call(
        body, name=name, grid=(s // tm,),
        in_specs=[hspec, hspec, hspec, row(QL, 0), row(KVL, 2), row(HP, 3), row(HP, 0), row(HP, 0), row(HP, 0),
                  _acc((1, QL)), _acc((1, KVL)), _acc((1, HP)), _acc((1, HP)),
                  _acc((QL, HEADS * HP)), _acc((KVL, HEADS * HP)), _acc((KVL, HEADS * HP))],
        out_specs=[row(512, 0), row(QL, 0), row(KVL, 0), row(512, 0), row(512, 0), row(512, 0),
                   _acc((1, QL)), _acc((1, KVL)), _acc((1, HP)), _acc((1, HP))],
        out_shape=[sd(512, BF16), sd(QL, BF16), sd(KVL, BF16), sd(512, BF16), sd(512, BF16), sd(512, BF16),
                   jax.ShapeDtypeStruct((1, QL), F32), jax.ShapeDtypeStruct((1, KVL), F32),
                   jax.ShapeDtypeStruct((1, HP), F32), jax.ShapeDtypeStruct((1, HP), F32)],
        compiler_params=_cp(("arbitrary",), VMEM_LIMIT),
    )(dq, dk, dv, z, z, z, *tabs, gql, gkv, gq, gk, wq, wk, wv)


def _in_proj_bwd(dzm, duv, dp, x, dx1, g, win, name):
    s = x.shape[0]
    tm = _tile(s, TOKENS // 2)

    def body(dzm_ref, duv_ref, dp_ref, x_ref, dx1_ref, g_ref, w_ref, dx_ref, dg_ref):
        groups = [slice(r0, r0 + tm // 2) for r0 in (0, tm // 2)]
        dhs = [_dot_nt(dzm_ref[rs, :], w_ref[:, 0:512]) + _dot_nt(duv_ref[rs, :], w_ref[:, 512:1024])
               + _dot_nt(dp_ref[rs, :], w_ref[:, 1024:IN_P]) for rs in groups]
        dg = jnp.zeros((1, D), F32)
        for rs, dh in zip(groups, dhs):
            xn, r = _rms(x_ref[rs, :], D)
            dxr, dgr = _rms_bwd(xn, r, g_ref[...], dh, D)
            dx_ref[rs, :] = dx1_ref[rs, :] + dxr
            dg = dg + dgr
        _accumulate(dg_ref, dg, pl.program_id(0) == 0)

    row = lambda w: pl.BlockSpec((tm, w), lambda i: (i, 0))
    return pl.pallas_call(
        body, name=name, grid=(s // tm,),
        in_specs=[row(512), row(512), row(POOL), row(D), row(D), _acc((1, D)), _res((D, IN_P))],
        out_specs=[row(D), _acc((1, D))],
        out_shape=[jax.ShapeDtypeStruct((s, D), F32), jax.ShapeDtypeStruct((1, D), F32)],
        compiler_params=_cp(("arbitrary",), VMEM_LIMIT),
    )(dzm, duv, dp, x, dx1, g, win)


def _adamw(w, g0, g1, m, v, name):
    _, r, c = w.shape
    tr = _row_tile(r, 512)
    c1 = 1.0 - B1 ** STEP
    c2 = 1.0 - B2 ** STEP

    def body(w_ref, g0_ref, g1_ref, m_ref, v_ref, g_ref, d_ref, nm_ref, nv_ref):
        gv = jnp.where(pl.program_id(0) == 0, g0_ref[...], g1_ref[...])
        g_ref[0] = gv
        nm = B1 * m_ref[0] + (1.0 - B1) * gv
        nv = B2 * v_ref[0] + (1.0 - B2) * (gv * gv)
        nm_ref[0] = nm
        nv_ref[0] = nv
        d_ref[0] = -LR * ((nm / c1) / (jnp.sqrt(nv / c2) + ADAM_EPS) + WD * w_ref[0])

    spec = pl.BlockSpec((1, tr, c), lambda l, i: (l, i, 0))
    out = jax.ShapeDtypeStruct((DEPTH, r, c), F32)
    return pl.pallas_call(
        body, name=name, grid=(DEPTH, r // tr),
        in_specs=[spec, pl.BlockSpec((tr, c), lambda l, i: (i * (1 - l), 0)), pl.BlockSpec((tr, c), lambda l, i: (i * l, 0)),
                  spec, spec],
        out_specs=[spec] * 4, out_shape=[out] * 4, compiler_params=_cp(("parallel", "parallel")),
    )(w, g0, g1, m, v)


ANY = pl.BlockSpec(memory_space=pl.ANY)


def _place():
    x, y, c = lax.axis_index("x"), lax.axis_index("y"), lax.axis_index("c")
    chips = [(1 - x, y), (x, 1 - y), (1 - x, 1 - y)]
    return x, y, c, chips


def _half_rows(ref, lead, hh, half, align):
    rows = pl.ds(pl.multiple_of(hh * half, align), half)
    return ref.at[rows, :] if lead is None else ref.at[lead, rows, :]


def _row_align(dtype):
    return 16 if dtype == BF16 else 8


def _sems(n):
    return [pltpu.SemaphoreType.DMA((n,)), pltpu.SemaphoreType.DMA((n,)), pltpu.SemaphoreType.DMA((n,))]


def _comm_call(body, ins, out_shapes, nsems, name):
    return pl.pallas_call(
        body, name=name, in_specs=[ANY] * len(ins), out_specs=[ANY] * len(out_shapes), out_shape=out_shapes,
        scratch_shapes=_sems(nsems), compiler_params=pltpu.CompilerParams(has_side_effects=True),
    )(*ins)


def _all_gather_chips(shards, name):
    n = len(shards)
    halves = [a.shape[0] // 2 for a in shards]
    aligns = [_row_align(a.dtype) for a in shards]
    assert all(h % al == 0 for h, al in zip(halves, aligns))

    def body(*refs):
        ins, outs, (send_sems, recv_sems, _) = refs[:n], refs[n:2 * n], refs[2 * n:]
        x, y, c, chips = _place()
        me = 2 * x + y
        sibling = (x, y, 1 - c)

        def copy(sem, src, dst, to):
            return pltpu.make_async_remote_copy(src_ref=src, dst_ref=dst, send_sem=send_sems.at[sem],
                                                recv_sem=recv_sems.at[sem], device_id=to, device_id_type=MESH)

        first, passed = [], []
        for a in range(n):
            my_half = _half_rows(ins[a], None, c, halves[a], aligns[a])
            for j, (cx, cy) in enumerate(chips):
                cp = copy(6 * a + j, my_half, _half_rows(outs[a], me, c, halves[a], aligns[a]), (cx, cy, c))
                cp.start()
                first.append(cp)
        for a in range(n):
            for j, (cx, cy) in enumerate(chips):
                landed = _half_rows(outs[a], 2 * cx + cy, c, halves[a], aligns[a])
                copy(6 * a + j, landed, landed, (cx, cy, c)).wait_recv()
                fwd = copy(6 * a + 3 + j, landed, landed, sibling)
                fwd.start()
                passed.append(fwd)
        for a in range(n):
            for j, (cx, cy) in enumerate(chips):
                other = _half_rows(outs[a], 2 * cx + cy, 1 - c, halves[a], aligns[a])
                copy(6 * a + 3 + j, other, other, sibling).wait_recv()
        for cp in first + passed:
            cp.wait_send()

    lands = _comm_call(body, shards, [jax.ShapeDtypeStruct((CHIPS,) + a.shape, a.dtype) for a in shards], 6 * n, name)
    return _with_own(lands, shards)


def _with_own(lands, shards):
    me = 2 * lax.axis_index("x") + lax.axis_index("y")
    return [lax.dynamic_update_slice(g, a[None], (me, 0, 0)) for g, a in zip(lands, shards)]


def _pair_join(arrs, name):
    n = len(arrs)
    halves = [a.shape[0] // 2 for a in arrs]

    def body(*refs):
        outs, (send_sems, recv_sems, _) = refs[n:2 * n], refs[2 * n:]
        x, y, c, _ = _place()
        cps = []
        for a in range(n):
            mine = _half_rows(outs[a], None, c, halves[a], 8)
            cp = pltpu.make_async_remote_copy(src_ref=mine, dst_ref=mine, send_sem=send_sems.at[a], recv_sem=recv_sems.at[a],
                                              device_id=(x, y, 1 - c), device_id_type=MESH)
            cp.start()
            cps.append(cp)
        for cp in cps:
            cp.wait()

    return pl.pallas_call(
        body, name=name, in_specs=[ANY] * n, out_specs=[ANY] * n,
        out_shape=[jax.ShapeDtypeStruct(a.shape, a.dtype) for a in arrs],
        input_output_aliases={i: i for i in range(n)}, scratch_shapes=_sems(n),
        compiler_params=pltpu.CompilerParams(has_side_effects=True),
    )(*arrs)


HBM = pl.BlockSpec(memory_space=pltpu.HBM)
SEM = pl.BlockSpec(memory_space=pltpu.SEMAPHORE)
DATAFLOW = pltpu.SideEffectType.DATAFLOW_SIDE_EFFECTING


def _remote_copies(pairs, ins, lands, send_sems, recv_sems):
    return [pltpu.make_async_remote_copy(src_ref=src, dst_ref=dst, send_sem=send_sems.at[i], recv_sem=recv_sems.at[i],
                                         device_id=to, device_id_type=MESH)
            for i, (src, dst, to) in enumerate(pairs(ins, lands))]


def _split_start(srcs, land_shapes, ncopies, pairs, name, after):
    n, m = len(srcs), len(land_shapes)

    def body(*refs):
        ins, lands = refs[:n], refs[n:n + m]
        send_sems, recv_sems, token = refs[n + m + 1], refs[n + m + 2], refs[-1]
        for cp in _remote_copies(pairs, ins, lands, send_sems, recv_sems):
            cp.start()
        token[...] = jnp.zeros_like(token)

    hbm = lambda a: pltpu.with_memory_space_constraint(a, pltpu.HBM)
    lands = [hbm(lax.empty(s.shape, s.dtype)) for s in land_shapes]
    thru = [pltpu.HBM(a.shape, a.dtype) for a in list(srcs) + lands]
    out = pl.pallas_call(
        body, name=name,
        out_shape=(pltpu.SemaphoreType.DMA((ncopies,)), pltpu.SemaphoreType.DMA((ncopies,)), *thru,
                   jax.ShapeDtypeStruct((8, LANES), F32)),
        in_specs=[HBM] * (n + m) + [ANY], out_specs=(SEM, SEM, *[HBM] * (n + m), pl.BlockSpec(memory_space=pltpu.VMEM)),
        input_output_aliases={i: 2 + i for i in range(n + m)},
        compiler_params=pltpu.CompilerParams(has_side_effects=DATAFLOW),
    )(*[hbm(a) for a in srcs], *lands, after)
    return out[0], out[1], list(out[2:2 + n]), list(out[2 + n:2 + n + m]), out[-1]


def _split_wait(send_sems, recv_sems, srcs, lands, after, pairs, name):
    n, m = len(srcs), len(lands)

    def body(*refs):
        ins, lands_ = refs[:n], refs[n:n + m]
        for cp in _remote_copies(pairs, ins, lands_, refs[n + m], refs[n + m + 1]):
            cp.wait_send()
            cp.wait_recv()

    out = pl.pallas_call(
        body, name=name, out_shape=tuple(pltpu.HBM(a.shape, a.dtype) for a in list(srcs) + list(lands)),
        in_specs=[HBM] * (n + m) + [SEM, SEM, ANY], out_specs=tuple([HBM] * (n + m)),
        input_output_aliases={i: i for i in range(n + m)},
        compiler_params=pltpu.CompilerParams(has_side_effects=DATAFLOW),
    )(*srcs, *lands, send_sems, recv_sems, after)
    return list(out[:n]), list(out[n:])


def _gather_pairs(halves, aligns):
    def pairs(ins, lands):
        x, y, c, chips = _place()
        me = 2 * x + y
        return [(_half_rows(ins[a], None, c, halves[a], aligns[a]), _half_rows(lands[a], me, c, halves[a], aligns[a]),
                 (cx, cy, c)) for a in range(len(ins)) for cx, cy in chips]
    return pairs


PEERS = 7


def _scatter_pairs(ins, lands):
    x, y, c, chips = _place()
    to = [(cx, cy, c) for cx, cy in chips] + [(cx, cy, 1 - c) for cx, cy in chips] + [(x, y, 1 - c)]
    out = []
    for a in range(len(ins)):
        half = ins[a].shape[1] // 2
        for i, (tx, ty, tc) in enumerate(to):
            out.append((_half_rows(ins[a], 2 * tx + ty, tc, half, 8), lands[a].at[i], (tx, ty, tc)))
    return out


def _gather_finish(shards, lands, name):
    n = len(shards)
    halves = [a.shape[0] // 2 for a in shards]
    aligns = [_row_align(a.dtype) for a in shards]

    def body(*refs):
        outs, (send_sems, recv_sems, _) = refs[n:2 * n], refs[2 * n:]
        x, y, c, chips = _place()
        passed = []
        for a in range(n):
            for j, (cx, cy) in enumerate(chips):
                landed = _half_rows(outs[a], 2 * cx + cy, c, halves[a], aligns[a])
                cp = pltpu.make_async_remote_copy(src_ref=landed, dst_ref=landed, send_sem=send_sems.at[3 * a + j],
                                                  recv_sem=recv_sems.at[3 * a + j], device_id=(x, y, 1 - c),
                                                  device_id_type=MESH)
                cp.start()
                passed.append(cp)
        for a in range(n):
            for j, (cx, cy) in enumerate(chips):
                other = _half_rows(outs[a], 2 * cx + cy, 1 - c, halves[a], aligns[a])
                pltpu.make_async_remote_copy(src_ref=other, dst_ref=other, send_sem=send_sems.at[3 * a + j],
                                             recv_sem=recv_sems.at[3 * a + j], device_id=(x, y, 1 - c),
                                             device_id_type=MESH).wait_recv()
        for cp in passed:
            cp.wait_send()

    lands = pl.pallas_call(
        body, name=name, in_specs=[ANY] * n, out_specs=[ANY] * n,
        out_shape=[jax.ShapeDtypeStruct(a.shape, a.dtype) for a in lands],
        input_output_aliases={i: i for i in range(n)}, scratch_shapes=_sems(3 * n),
        compiler_params=pltpu.CompilerParams(has_side_effects=True),
    )(*lands)
    return _with_own(lands, shards)


def _sum_own_and_landed(own, landed, where, name):
    _, half, cols = landed.shape
    tr = _row_tile(half, 128)
    nt = half // tr

    grid_spec = pltpu.PrefetchScalarGridSpec(
        num_scalar_prefetch=1, grid=(nt,),
        in_specs=[pl.BlockSpec((1, tr, cols), lambda r, w: (w[0], w[1] * nt + r, 0)),
                  pl.BlockSpec((PEERS, tr, cols), lambda r, w: (0, r, 0))],
        out_specs=pl.BlockSpec((tr, cols), lambda r, w: (w[1] * nt + r, 0)))

    def body(w_ref, p_ref, q_ref, o_ref):
        acc = p_ref[0]
        for i in range(PEERS):
            acc = acc + q_ref[i]
        o_ref[...] = acc

    return pl.pallas_call(
        body, name=name, grid_spec=grid_spec, out_shape=jax.ShapeDtypeStruct((2 * half, cols), own.dtype),
        compiler_params=_cp(("parallel",)),
    )(where, own, landed)


BIG = [("w_in", (D, IN_W), 1), ("w_q_up", (QL, HEADS * QK), 1), ("w_kv_up", (KVL, HEADS * (NOPE + VH)), 1),
       ("w_out", (D, D), 0), ("w_gate", (D, HID), 1), ("w_up", (D, HID), 1), ("w_down", (HID, D), 0)]
SMALL = [("g_mix_norm", (D,)), ("g_q_lat", (QL,)), ("g_kv_lat", (KVL,)), ("g_q_head", (QK,)), ("g_k_head", (QK,)),
         ("g_sgu_v", (SGU,)), ("w_spatial", (HEADS, CHUNK, CHUNK)), ("b_spatial", (HEADS, CHUNK)),
         ("w_pool", (4, 64, 64)), ("pool_scale", (POOL,)), ("g_out_mla", (512,)), ("g_out_sgu", (SGU,)),
         ("g_out_pool", (POOL,)), ("g_ffn_norm", (D,))]
ORDER = ["g_mix_norm", "w_in", "g_q_lat", "w_q_up", "g_kv_lat", "w_kv_up", "g_q_head", "g_k_head", "g_sgu_v",
         "w_spatial", "b_spatial", "w_pool", "pool_scale", "g_out_mla", "g_out_sgu", "g_out_pool", "w_out",
         "g_ffn_norm", "w_gate", "w_up", "w_down"]
EARLY_BIG = ["w_in", "w_q_up", "w_kv_up"]
FFN_BIG = ["w_gate", "w_up", "w_down"]
LATE_BIG = ["w_out"] + FFN_BIG
DEPTH = 2
COLS = 1024
SMALL_N = sum(math.prod(s) for _, s in SMALL) * DEPTH
assert SMALL_N % CHIPS == 0
SMALL_ROWS = -(-(SMALL_N // CHIPS) // (16 * COLS)) * 16


def _unsplit_cols(g):
    return g.transpose(1, 0, 2).reshape(g.shape[1], CHIPS * g.shape[2])


def _split_cols(full):
    r, c = full.shape
    return full.reshape(r, CHIPS, c // CHIPS).transpose(1, 0, 2)


def _kernel_weights(g):
    win = _unsplit_cols(g["w_in"])
    zeros = lambda r, c: jnp.zeros((r, c), BF16)
    o2, o3, o4 = QL + KVL, QL + KVL + ROPE, QL + KVL + ROPE + 2 * SGU
    win_p = jnp.concatenate([win[:, :o2], zeros(D, NOPE), win[:, o2:o3], zeros(D, HP - QK), win[:, o3:o4], win[:, o4:]], axis=1)
    wq = _unsplit_cols(g["w_q_up"]).reshape(QL, HEADS, QK)
    wq_p = jnp.pad(wq, ((0, 0), (0, 0), (0, HP - QK))).reshape(QL, HEADS * HP)
    wkv = _unsplit_cols(g["w_kv_up"]).reshape(KVL, HEADS, NOPE + VH)
    wk_p = jnp.pad(wkv[:, :, :NOPE], ((0, 0), (0, 0), (0, HP - NOPE))).reshape(KVL, HEADS * HP)
    wv_p = wkv[:, :, NOPE:].reshape(KVL, HEADS * VH)
    return dict(win=win_p, wq=wq_p, wk=wk_p, wv=wv_p)


def _small_operands(p, l):
    row = lambda v: v.reshape(1, -1)
    pad = lambda v: jnp.pad(v, (0, HP - QK)).reshape(1, HP)
    wpool = p["w_pool"][l]
    wbd = jnp.zeros((POOL, POOL), F32)
    for g in range(4):
        wbd = lax.dynamic_update_slice(wbd, wpool[g], (g * 64, g * 64))
    return dict(
        g_mix=row(p["g_mix_norm"][l]), gql=row(p["g_q_lat"][l]), gkv=row(p["g_kv_lat"][l]),
        gq=pad(p["g_q_head"][l]), gk=pad(p["g_k_head"][l]), gsv=row(p["g_sgu_v"][l]),
        wsp=p["w_spatial"][l], bsp=jnp.repeat(p["b_spatial"][l].T, SGU // HEADS, axis=1),
        wbd=wbd.astype(BF16), psc=row(p["pool_scale"][l]),
        gout=jnp.concatenate([p["g_out_mla"][l], p["g_out_sgu"][l], p["g_out_pool"][l]]).reshape(1, D),
        g_ffn=row(p["g_ffn_norm"][l]))


def _big_grads(g):
    dwin = g["win"]
    o2 = QL + KVL
    gin = jnp.concatenate([dwin[:, :o2], dwin[:, o2 + NOPE:o2 + NOPE + ROPE], dwin[:, 512:]], axis=1)
    gq = g["wq"].reshape(QL, HEADS, HP)[:, :, :QK].reshape(QL, HEADS * QK)
    gk = g["wk"].reshape(KVL, HEADS, HP)[:, :, :NOPE]
    gv = g["wv"].reshape(KVL, HEADS, VH)
    gkv = jnp.concatenate([gk, gv], axis=2).reshape(KVL, HEADS * (NOPE + VH))
    return {"w_in": _split_cols(gin), "w_q_up": _split_cols(gq), "w_kv_up": _split_cols(gkv),
            "w_out": g["wout"].reshape(CHIPS, D // CHIPS, D), "w_gate": g["wg"], "w_up": g["wu"], "w_down": g["wd"]}


TRANSPOSED = ("w_gate", "w_up")


def _small_grads(g):
    go = g["gout"].reshape(-1)
    return {"g_mix_norm": g["g_mix"].reshape(-1), "g_q_lat": g["gql"].reshape(-1), "g_kv_lat": g["gkv"].reshape(-1),
            "g_q_head": g["gq"].reshape(-1)[:QK], "g_k_head": g["gk"].reshape(-1)[:QK], "g_sgu_v": g["gsv"].reshape(-1),
            "w_spatial": g["wsp"], "b_spatial": g["bsp"].reshape(CHUNK, HEADS, SGU // HEADS).sum(-1).T,
            "w_pool": jnp.stack([g["wbd"][i * 64:(i + 1) * 64, i * 64:(i + 1) * 64] for i in range(4)]),
            "pool_scale": g["psc"].reshape(-1), "g_out_mla": go[:512], "g_out_sgu": go[512:768],
            "g_out_pool": go[768:], "g_ffn_norm": g["g_ffn"].reshape(-1)}


def _pack_small_grads(small):
    sm = jnp.concatenate([small[l][n].reshape(-1) for l in range(DEPTH) for n, _ in SMALL]).reshape(CHIPS, SMALL_N // CHIPS)
    return jnp.pad(sm, ((0, 0), (0, SMALL_ROWS * COLS - SMALL_N // CHIPS))).reshape(CHIPS, SMALL_ROWS, COLS)


def _unpack_small_grads(gathered):
    flat = gathered.reshape(CHIPS, SMALL_ROWS * COLS)[:, :SMALL_N // CHIPS].reshape(-1)
    out, off = [], 0
    for _ in range(DEPTH):
        layer = {}
        for n, shape in SMALL:
            k = math.prod(shape)
            layer[n] = flat[off:off + k].reshape(shape)
            off += k
        out.append(layer)
    return out


def _layer_fwd(x, tabs, kw, late_weights, sp, l):
    t = f"_l{l}"
    z, hb = _in_proj_fwd(x, sp["g_mix"], kw["win"], "in_proj_fwd" + t)
    q, k, v = _mla_prep_fwd(z, tabs, sp["gql"], sp["gkv"], sp["gq"], sp["gk"], kw["wq"], kw["wk"], kw["wv"],
                            "mla_prep_fwd" + t)
    o, lse = _attn_fwd(q, k, v, "attn_fwd" + t)
    m = _pool_win_fwd(z, "pool_win_fwd" + t)
    wout, wg, wu, wd = late_weights(o)
    wout = wout.reshape(D, D)
    x1, mix = _mix_out_fwd(o, z, m, x, sp["wsp"], sp["bsp"], sp["wbd"], sp["psc"], sp["gsv"], sp["gout"], wout,
                           "mix_out_fwd" + t)
    x2, a, b, h2 = _ffn_fwd(x1, sp["g_ffn"], wg, wu, wd, "ffn_fwd" + t)
    saved = dict(x=x, z=z, hb=hb, q=q, k=k, v=v, o=o, lse=lse, m=m, x1=x1, mix=mix, a=a, b=b, h2=h2, wg=wg, wu=wu, wd=wd,
                 wout=wout)
    return x2, saved


def _layer_bwd(dx2, sv, tabs, kw, sp, l, ffn_hook, out_hook):
    t = f"_l{l}"
    g = {}
    dx1, hid, da, db, dyb, g["g_ffn"] = _ffn_bwd(dx2, sv["x1"], sv["a"], sv["b"], sp["g_ffn"], sv["wg"], sv["wu"],
                                                 sv["wd"], "ffn_bwd" + t)
    g["wd"] = _wgrad_rows(hid, dyb, "wgrad_down" + t)
    g["wg"] = _wgrad_rows(da, sv["h2"], "wgrad_gate" + t)
    g["wu"] = _wgrad_rows(db, sv["h2"], "wgrad_up" + t)
    gout = sp["gout"] + ffn_hook(g)
    do, delta, duv, dm, g["gout"], g["gsv"], g["psc"], g["wsp"], g["bsp"], g["wbd"] = _mix_out_bwd(
        dx1, sv["o"], sv["z"], sv["m"], sp["wsp"], sp["bsp"], sp["wbd"], sp["psc"], sp["gsv"], gout, sv["wout"],
        "mix_out_bwd" + t)
    g["wout"] = _wgrad(sv["mix"], dx1, "wgrad_out" + t)
    dp = _pool_win_bwd(dm, "pool_win_bwd" + t)
    dq, dk, dv = _attn_bwd(sv["q"], sv["k"], sv["v"], do, sv["lse"], delta, out_hook(g), "attn_bwd" + t)
    dzm, qn, kvn, dqr, dkr, dvr, g["gql"], g["gkv"], g["gq"], g["gk"] = _mla_prep_bwd(
        dq, dk, dv, sv["z"], tabs, sp["gql"], sp["gkv"], sp["gq"], sp["gk"], kw["wq"], kw["wk"], kw["wv"],
        "mla_prep_bwd" + t)
    g["wq"] = _wgrad(qn, dqr, "wgrad_q_up" + t)
    g["wk"] = _wgrad(kvn, dkr, "wgrad_k_up" + t)
    g["wv"] = _wgrad(kvn, dvr, "wgrad_v_up" + t)
    dx, g["g_mix"] = _in_proj_bwd(dzm, duv, dp, sv["x"], dx1, sp["g_mix"], kw["win"], "in_proj_bwd" + t)
    g["win"] = _wgrad_in(sv["hb"], dzm, duv, dp, "wgrad_in" + t)
    return dx, g


def _rope_inv_freq():
    half = ROPE // 2
    inv = 1.0 / (ROPE_THETA ** (jnp.arange(half, dtype=F32) / half))
    return jnp.concatenate([jnp.zeros((NOPE,), F32), inv, inv, jnp.zeros((HP - QK,), F32)]).reshape(1, HP)


def kernel(x, positions, g_mix_norm, w_in, g_q_lat, w_q_up, g_kv_lat, w_kv_up, g_q_head, g_k_head, g_sgu_v, w_spatial, b_spatial, w_pool, pool_scale, g_out_mla, g_out_sgu, g_out_pool, w_out, g_ffn_norm, w_gate, w_up, w_down, loss_target, m_g_mix_norm, m_w_in, m_g_q_lat, m_w_q_up, m_g_kv_lat, m_w_kv_up, m_g_q_head, m_g_k_head, m_g_sgu_v, m_w_spatial, m_b_spatial, m_w_pool, m_pool_scale, m_g_out_mla, m_g_out_sgu, m_g_out_pool, m_w_out, m_g_ffn_norm, m_w_gate, m_w_up, m_w_down, v_g_mix_norm, v_w_in, v_g_q_lat, v_w_q_up, v_g_kv_lat, v_w_kv_up, v_g_q_head, v_g_k_head, v_g_sgu_v, v_w_spatial, v_b_spatial, v_w_pool, v_pool_scale, v_g_out_mla, v_g_out_sgu, v_g_out_pool, v_w_out, v_g_ffn_norm, v_w_gate, v_w_up, v_w_down):
    given = dict(locals())
    p = {n: given[n] for n in ORDER}
    view = lambda pre, n: jnp.swapaxes(given[pre + n], 1, 2) if n in TRANSPOSED else given[pre + n]
    seq = x.shape[1]
    where = jnp.stack([2 * lax.axis_index("x") + lax.axis_index("y"), lax.axis_index("c")]).astype(jnp.int32)
    shards = lambda names: [view("", n)[l].astype(BF16) for l, n in names]
    zero11 = lambda token: token[:1, :1]

    names_0a = [(0, n) for n in EARLY_BIG]
    names_0b = [(0, n) for n in LATE_BIG]
    names_1 = [(1, n) for n, _, _ in BIG]
    got_0a = dict(zip(EARLY_BIG, _all_gather_chips(shards(names_0a), "all_gather_w0a")))
    started, issued = {}, got_0a["w_in"]
    for tag, names in (("w0b", names_0b), ("w1", names_1)):
        sh = shards(names)
        pairs = _gather_pairs([a.shape[0] // 2 for a in sh], [_row_align(a.dtype) for a in sh])
        lands = [jax.ShapeDtypeStruct((CHIPS,) + a.shape, a.dtype) for a in sh]
        started[tag] = (sh, pairs) + _split_start(sh, lands, 3 * len(sh), pairs, "gather_start_" + tag, issued)
        issued = started[tag][6]

    def arrived(tag, after):
        _, pairs, send, recv, srcs, lands, _ = started[tag]
        srcs, lands = _split_wait(send, recv, srcs, lands, after, pairs, "gather_wait_" + tag)
        return _gather_finish(srcs, lands, "gather_finish_" + tag)

    layer1 = {}

    def mix_weights(l, h):
        if l == 0:
            return got_0a
        layer1.update(zip([n for _, n in names_1], arrived("w1", h)))
        return layer1

    def late_weights(l, o):
        return arrived("w0b", o) if l == 0 else [layer1[n] for n in LATE_BIG]

    reducing, last = {}, {}

    def reduce_start(tag, arrs):
        lands = [jax.ShapeDtypeStruct((PEERS, a.shape[1] // 2, a.shape[2]), a.dtype) for a in arrs]
        reducing[tag] = _split_start(arrs, lands, PEERS * len(arrs), _scatter_pairs, "grad_scatter_start_" + tag, where)
        return zero11(reducing[tag][4])

    def reduce_finish(tag, after):
        send, recv, srcs, lands, _ = reducing[tag]
        srcs, lands = _split_wait(send, recv, srcs, lands, after, _scatter_pairs, "grad_scatter_wait_" + tag)
        return [_sum_own_and_landed(a, q, where, f"grad_sum_{tag}_{i}") for i, (a, q) in enumerate(zip(srcs, lands))]

    def ffn_hook(l, g):
        if l == 1:
            return jnp.zeros((1, 1), F32)
        return reduce_start("g0b", [g["wg"], g["wu"], g["wd"]])

    def out_hook(l, g):
        if l == 1:
            return where
        reduce_start("g0c", [g["wout"].reshape(CHIPS, D // CHIPS, D)])
        return reducing["g0c"][4]

    def layer_hook(l, big, small):
        last[l] = (big, small)
        if l == 1:
            return reduce_start("g1", [big[n] for n, _, _ in BIG])
        return None

    entry = zero11(started["w0b"][6]) + zero11(started["w1"][6])
    loss_part, dx = _step(x.reshape(seq, D), positions.reshape(seq, 1), loss_target.reshape(seq, D), p, entry,
                          mix_weights, late_weights, ffn_hook, out_hook, layer_hook)
    loss = lax.psum(loss_part, ("x", "y", "c"))

    def adamw(n, g0, g1):
        w = view("", n)
        three_d = (DEPTH, -1, w.shape[-1])
        res = _adamw(w.reshape(three_d), g0.reshape(three_d[1:]), g1.reshape(three_d[1:]),
                     view("m_", n).reshape(three_d), view("v_", n).reshape(three_d), "adamw_" + n)
        return [r.reshape(w.shape) for r in res]

    names_rest = [(0, n) for n in EARLY_BIG]
    reduce_start("g0a", [last[0][0][n] for _, n in names_rest] + [_pack_small_grads([last[l][1] for l in range(DEPTH)])])
    token = reducing["g0a"][4]
    early = names_1 + [(0, n) for n in FFN_BIG] + [(0, "w_out")]
    landed = reduce_finish("g1", token) + reduce_finish("g0b", token) + reduce_finish("g0c", token)
    sums = dict(zip(early, _pair_join(landed, "grad_pair_join_early")))
    out = {n: adamw(n, sums[(0, n)], sums[(1, n)]) for n in FFN_BIG}
    late = names_rest + ["small"]
    sums.update(zip(late, _pair_join(reduce_finish("g0a", out["w_down"][1]), "grad_pair_join_late")))
    gsmall = _unpack_small_grads(_all_gather_chips([sums["small"]], "all_gather_small_grads")[0])
    for n in ORDER:
        if n not in out:
            g = [sums[(l, n)] for l in range(DEPTH)] if (0, n) in sums else [gsmall[l][n] for l in range(DEPTH)]
            out[n] = adamw(n, *g)
    undo = lambda n, a: jnp.swapaxes(a, 1, 2) if n in TRANSPOSED else a
    return (loss, dx.reshape(x.shape), *[undo(n, out[n][i]) for i in range(4) for n in ORDER])


def _step(xs, pos, tgt, p, entry, mix_weights, late_weights, ffn_hook, out_hook, layer_hook):
    sps = [_small_operands(p, l) for l in range(DEPTH)]
    sps[0]["g_mix"] = sps[0]["g_mix"] + entry
    tabs = _rope_tables(pos, _rope_inv_freq())
    saved, h = [], xs
    for l in range(DEPTH):
        kw = _kernel_weights(mix_weights(l, h))
        h, sv = _layer_fwd(h, tabs, kw, functools.partial(late_weights, l), sps[l], l)
        saved.append(dict(sv, kw=kw))
    dy, lpart = _loss_grad(h, tgt)
    for l in reversed(range(DEPTH)):
        dy, g = _layer_bwd(dy, saved[l], tabs, saved[l]["kw"], sps[l], l, functools.partial(ffn_hook, l),
                           functools.partial(out_hook, l))
        zero = layer_hook(l, _big_grads(g), _small_grads(g))
        if zero is not None and l > 0:
            sps[l - 1]["g_ffn"] = sps[l - 1]["g_ffn"] + zero
    return 0.5 / D * jnp.sum(lpart), dy
```

```python
import functools
import math

import jax
import jax.numpy as jnp
from jax import lax
from jax.experimental import pallas as pl
from jax.experimental.pallas import tpu as pltpu

F32 = jnp.float32
BF16 = jnp.bfloat16
MESH = pl.DeviceIdType.MESH

D = 1024
HEADS = 4
QK = 96
NOPE = 64
ROPE = 32
VH = 128
HP = 128
QL = 256
KVL = 128
SGU = 256
POOL = 256
CHUNK = 128
HID = 2816
CHIPS = 4
SH = HID // CHIPS
IN_W = 1184
IN_P = 1280
EPS = 1e-6
ROPE_THETA = 10000.0
SCALE = 1.0 / math.sqrt(QK)
LOG2E = 1.4426950408889634
EXP2_C = SCALE * LOG2E
ATT_WIDE = 2
ATT_FWD_QUERIES = 2048
ATT_PIECE = 1024
ATT_ROWS = 256
ATT_KEYS = 1024
ATT_QUERIES = 2048
NEG = -1e30
HALO = 16

LR, B1, B2, ADAM_EPS, WD, STEP = 0.001, 0.9, 0.999, 1e-08, 0.01, 10

VMEM_LIMIT = 56 * 1024 * 1024
LANES = 128
TOKENS = 1024


def _cp(sem, vmem=None):
    return pltpu.CompilerParams(dimension_semantics=sem, vmem_limit_bytes=vmem)


def _res(shape):
    nd = len(shape)
    return pl.BlockSpec(shape, lambda *_: (0,) * nd, pipeline_mode=pl.Buffered(1))


def _acc(shape):
    nd = len(shape)
    return pl.BlockSpec(shape, lambda *_: (0,) * nd)


def _dot(a, b):
    return jnp.dot(a, b, preferred_element_type=F32)


def _dot_nt(a, b):
    return lax.dot_general(a, b, (((1,), (1,)), ((), ())), preferred_element_type=F32)


def _dot_tn(a, b):
    return lax.dot_general(a, b, (((0,), (0,)), ((), ())), preferred_element_type=F32)


def _rms(x, n):
    r = lax.rsqrt(jnp.sum(x * x, axis=-1, keepdims=True) * (1.0 / n) + EPS)
    return x * r, r


def _rms_bwd(xn, r, g, dy, n):
    dn = dy * g
    dx = r * (dn - xn * (jnp.sum(dn * xn, axis=-1, keepdims=True) * (1.0 / n)))
    return dx, jnp.sum(dy * xn, axis=0, keepdims=True)


def _accumulate(ref, val, first):
    @pl.when(first)
    def _():
        ref[...] = val

    @pl.when(jnp.logical_not(first))
    def _():
        ref[...] += val


def _accumulate0(ref, val, first):
    @pl.when(first)
    def _():
        ref[0] = val

    @pl.when(jnp.logical_not(first))
    def _():
        ref[0] += val


def _tile(s, t):
    return min(s, t)


def _row_tile(r, cap):
    if r <= cap:
        return r
    return max(t for t in range(8, cap + 1, 8) if r % t == 0)


def _rope_tables(pos, invf):
    s = pos.shape[0]
    tm = _tile(s, 1024)

    def body(pos_ref, invf_ref, c_ref, sa_ref, sb_ref):
        ang = pos_ref[...].astype(F32) * invf_ref[...]
        c, sn = jnp.cos(ang), jnp.sin(ang)
        lane = lax.broadcasted_iota(jnp.int32, ang.shape, 1)
        first = (lane >= NOPE) & (lane < NOPE + ROPE // 2)
        second = (lane >= NOPE + ROPE // 2) & (lane < QK)
        c_ref[...] = jnp.where(first | second, c, 1.0)
        sa_ref[...] = jnp.where(first, -sn, 0.0)
        sb_ref[...] = jnp.where(second, sn, 0.0)

    out = jax.ShapeDtypeStruct((s, HP), F32)
    return pl.pallas_call(
        body, name="rope_tables", grid=(s // tm,),
        in_specs=[pl.BlockSpec((tm, 1), lambda i: (i, 0)), _acc((1, HP))],
        out_specs=[pl.BlockSpec((tm, HP), lambda i: (i, 0))] * 3,
        out_shape=[out] * 3, compiler_params=_cp(("parallel",)),
    )(pos, invf)


def _rope(x, c, sa, sb):
    return x * c + pltpu.roll(x, HP - ROPE // 2, 1) * sa + pltpu.roll(x, ROPE // 2, 1) * sb


def _rope_t(d, c, sa, sb):
    return d * c + pltpu.roll(d * sa, ROPE // 2, 1) + pltpu.roll(d * sb, HP - ROPE // 2, 1)


def _in_proj_fwd(x, g, w, name):
    s = x.shape[0]
    tm = _tile(s, TOKENS)

    def body(x_ref, g_ref, w_ref, z_ref, h_ref):
        xn, _ = _rms(x_ref[...], D)
        h = (xn * g_ref[...]).astype(BF16)
        h_ref[...] = h
        z_ref[...] = _dot(h, w_ref[...])

    return pl.pallas_call(
        body, name=name, grid=(s // tm,),
        in_specs=[pl.BlockSpec((tm, D), lambda i: (i, 0)), _acc((1, D)), _res((D, IN_P))],
        out_specs=[pl.BlockSpec((tm, IN_P), lambda i: (i, 0)), pl.BlockSpec((tm, D), lambda i: (i, 0))],
        out_shape=[jax.ShapeDtypeStruct((s, IN_P), F32), jax.ShapeDtypeStruct((s, D), BF16)],
        compiler_params=_cp(("parallel",), VMEM_LIMIT),
    )(x, g, w)


def _mla_prep_fwd(z, tabs, gql, gkv, gq, gk, wq, wk, wv, name):
    s = z.shape[0]
    tm = _tile(s, TOKENS)

    def body(ql_ref, kv_ref, kr_ref, c_ref, sa_ref, sb_ref, gql_ref, gkv_ref, gq_ref, gk_ref,
             wq_ref, wk_ref, wv_ref, q_out, k_out, v_out):
        qn = (_rms(ql_ref[...], QL)[0] * gql_ref[...]).astype(BF16)
        kvn = (_rms(kv_ref[...], KVL)[0] * gkv_ref[...]).astype(BF16)
        qraw = _dot(qn, wq_ref[...])
        kraw = _dot(kvn, wk_ref[...])
        vraw = _dot(kvn, wv_ref[...])
        kr = kr_ref[...]
        c, sa, sb = c_ref[...], sa_ref[...], sb_ref[...]
        for h in range(HEADS):
            sl = slice(h * HP, (h + 1) * HP)
            xq = _rms(qraw[:, sl], QK)[0] * gq_ref[...]
            q_out[h] = (_rope(xq, c, sa, sb) * EXP2_C).astype(BF16)
            xk = _rms(kraw[:, sl] + kr, QK)[0] * gk_ref[...]
            k_out[h] = _rope(xk, c, sa, sb).astype(BF16)
            v_out[h] = vraw[:, sl].astype(BF16)

    row = lambda w, j: pl.BlockSpec((tm, w), lambda i: (i, j))
    hspec = pl.BlockSpec((HEADS, tm, HP), lambda i: (0, i, 0))
    hshape = jax.ShapeDtypeStruct((HEADS, s, HP), BF16)
    return pl.pallas_call(
        body, name=name, grid=(s // tm,),
        in_specs=[row(QL, 0), row(KVL, 2), row(HP, 3), row(HP, 0), row(HP, 0), row(HP, 0),
                  _acc((1, QL)), _acc((1, KVL)), _acc((1, HP)), _acc((1, HP)),
                  _acc((QL, HEADS * HP)), _acc((KVL, HEADS * HP)), _acc((KVL, HEADS * HP))],
        out_specs=[hspec] * 3, out_shape=[hshape] * 3,
        compiler_params=_cp(("parallel",)),
    )(z, z, z, *tabs, gql, gkv, gq, gk, wq, wk, wv)


def _causal_mask(s, row0):
    row = lax.broadcasted_iota(jnp.int32, s.shape, 0) + row0
    col = lax.broadcasted_iota(jnp.int32, s.shape, 1)
    return jnp.where(col <= row, s, NEG)


def _attn_fwd(q, k, v, name):
    s = q.shape[1]
    tq = _tile(s, ATT_FWD_QUERIES)
    rh = _tile(s, ATT_ROWS)
    kp = _tile(s, ATT_PIECE)
    wide = ATT_WIDE * kp if s % (ATT_WIDE * kp) == 0 else tq
    groups = tq // rh

    def body(q_ref, k_ref, v_ref, o_ref, lse_ref):
        i = pl.program_id(1)

        def blk(off, tk, carry, diagonal):
            width = lambda g, t: max(0, min(kp, (g + 1) * rh - t * kp)) if diagonal else kp
            rows = lambda t: pl.ds(pl.multiple_of(off + t * kp, kp), kp)
            score = lambda g, t: _dot_nt(q_ref[0, g * rh:(g + 1) * rh, :], k_ref[0, rows(t), :][:width(g, t)])
            live = lambda t: [g for g in range(groups) if width(g, t) > 0]
            state = list(carry)
            scs = {(g, 0): score(g, 0) for g in live(0)}
            for t in range(tk // kp):
                if (t + 1) * kp < tk:
                    scs.update({(g, t + 1): score(g, t + 1) for g in live(t + 1)})
                vt = v_ref[0, rows(t), :]
                for g in live(t):
                    m, l, acc = state[g]
                    sc = scs.pop((g, t))
                    if diagonal and (g + 1) * rh <= (t + 1) * kp:
                        sc = _causal_mask(sc, g * rh - t * kp)
                    m_new = jnp.maximum(m, jnp.max(sc, axis=-1, keepdims=True))
                    p = jnp.exp2(sc - m_new)
                    alpha = jnp.exp2(m - m_new)
                    l = alpha * l + jnp.sum(p, axis=-1, keepdims=True)
                    acc = alpha * acc + _dot(p.astype(BF16), vt[:width(g, t)])
                    state[g] = (m_new, l, acc)
            return tuple(state)

        one = (jnp.full((rh, 1), NEG, F32), jnp.zeros((rh, 1), F32), jnp.zeros((rh, VH), F32))
        nwide = (i * tq) // wide
        carry = lax.fori_loop(0, nwide, lambda j, c: blk(j * wide, wide, c, False), (one,) * groups)
        carry = lax.fori_loop(nwide * (wide // tq), i, lambda j, c: blk(j * tq, tq, c, False), carry)
        carry = blk(i * tq, tq, carry, True)
        for g, (m, l, acc) in enumerate(carry):
            o_ref[g * rh:(g + 1) * rh, :] = acc / l
            lse_ref[0, g * rh:(g + 1) * rh, :] = jnp.broadcast_to(m + jnp.log(l) * LOG2E, (rh, LANES))

    return pl.pallas_call(
        body, name=name, grid=(HEADS, s // tq),
        in_specs=[pl.BlockSpec((1, tq, HP), lambda h, i: (h, i, 0)),
                  pl.BlockSpec((1, s, HP), lambda h, i: (h, 0, 0)),
                  pl.BlockSpec((1, s, HP), lambda h, i: (h, 0, 0))],
        out_specs=[pl.BlockSpec((tq, VH), lambda h, i: (i, h)),
                   pl.BlockSpec((1, tq, LANES), lambda h, i: (h, i, 0))],
        out_shape=[jax.ShapeDtypeStruct((s, HEADS * VH), F32), jax.ShapeDtypeStruct((HEADS, s, LANES), F32)],
        compiler_params=_cp(("parallel", "arbitrary"), VMEM_LIMIT),
    )(q, k, v)


def _lane_group(shape, j):
    return (lax.broadcasted_iota(jnp.int32, shape, 1) + j * LANES) // (POOL // 4)


def _pool_win_fwd(z, name):
    s = z.shape[0]
    ch = _tile(s, 512)
    col0 = (IN_P - POOL) // LANES

    def body(p_ref, m_ref):
        j = pl.program_id(0)

        def chunk(r, _):
            off = pl.multiple_of(r * ch, ch)
            cur = p_ref[pl.ds(off, ch), :]
            hoff = pl.multiple_of(jnp.maximum(off - HALO, 0), 8)
            halo = jnp.where(r > 0, p_ref[pl.ds(hoff, HALO), :], 0.0)
            x = jnp.concatenate([halo, cur], axis=0)
            s2 = x + pltpu.roll(x, 1, 0)
            s4 = s2 + pltpu.roll(s2, 2, 0)
            s8 = s4 + pltpu.roll(s4, 4, 0)
            s16 = s8 + pltpu.roll(s8, 8, 0)
            grp = _lane_group((ch, LANES), j)
            sel = jnp.where(grp == 0, s2[HALO:], jnp.where(grp == 1, s4[HALO:], jnp.where(grp == 2, s8[HALO:], s16[HALO:])))
            t1 = (lax.broadcasted_iota(jnp.int32, (ch, LANES), 0) + off + 1).astype(F32)
            win = jnp.where(grp == 0, 2.0, jnp.where(grp == 1, 4.0, jnp.where(grp == 2, 8.0, 16.0)))
            m_ref[pl.ds(off, ch), :] = sel / jnp.minimum(t1, win) - cur
            return 0

        lax.fori_loop(0, s // ch, chunk, 0)

    return pl.pallas_call(
        body, name=name, grid=(POOL // LANES,),
        in_specs=[pl.BlockSpec((s, LANES), lambda j: (0, col0 + j))],
        out_specs=pl.BlockSpec((s, LANES), lambda j: (0, j)),
        out_shape=jax.ShapeDtypeStruct((s, POOL), F32),
        compiler_params=_cp(("parallel",), VMEM_LIMIT),
    )(z)


def _pool_win_bwd(dm, name):
    s = dm.shape[0]
    ch = _tile(s, 512)
    n = s // ch

    def body(dm_ref, dp_ref):
        j = pl.program_id(0)

        def chunk(r, _):
            off = pl.multiple_of(r * ch, ch)
            grp = _lane_group((ch + HALO, LANES), j)
            win = jnp.where(grp == 0, 2.0, jnp.where(grp == 1, 4.0, jnp.where(grp == 2, 8.0, 16.0)))
            cur = dm_ref[pl.ds(off, ch), :]
            hoff = pl.multiple_of(jnp.minimum(off + ch, s - HALO), 8)
            halo = jnp.where(r < n - 1, dm_ref[pl.ds(hoff, HALO), :], 0.0)
            x = jnp.concatenate([cur, halo], axis=0)
            t1 = (lax.broadcasted_iota(jnp.int32, (ch + HALO, LANES), 0) + off + 1).astype(F32)
            e = x / jnp.minimum(t1, win)
            tot = ch + HALO
            r2 = e + pltpu.roll(e, tot - 1, 0)
            r4 = r2 + pltpu.roll(r2, tot - 2, 0)
            r8 = r4 + pltpu.roll(r4, tot - 4, 0)
            r16 = r8 + pltpu.roll(r8, tot - 8, 0)
            g = grp[:ch]
            sel = jnp.where(g == 0, r2[:ch], jnp.where(g == 1, r4[:ch], jnp.where(g == 2, r8[:ch], r16[:ch])))
            dp_ref[pl.ds(off, ch), :] = (sel - cur).astype(BF16)
            return 0

        lax.fori_loop(0, n, chunk, 0)

    return pl.pallas_call(
        body, name=name, grid=(POOL // LANES,),
        in_specs=[pl.BlockSpec((s, LANES), lambda j: (0, j))],
        out_specs=pl.BlockSpec((s, LANES), lambda j: (0, j)),
        out_shape=jax.ShapeDtypeStruct((s, POOL), BF16),
        compiler_params=_cp(("parallel",), VMEM_LIMIT),
    )(dm)


def _head_mask(h):
    lane = lax.broadcasted_iota(jnp.int32, (CHUNK, SGU), 1)
    return (lane // (SGU // HEADS)) == h


def _tril(upper=False):
    row = lax.broadcasted_iota(jnp.int32, (CHUNK, CHUNK), 0)
    col = lax.broadcasted_iota(jnp.int32, (CHUNK, CHUNK), 1)
    return col >= row if upper else col <= row


def _sgu_gate(vn, wsp, bsp):
    out = []
    for cidx in range(vn.shape[0] // CHUNK):
        vc = vn[cidx * CHUNK:(cidx + 1) * CHUNK]
        zc = bsp
        for h in range(HEADS):
            zc = zc + jnp.where(_head_mask(h), _dot(wsp[h], vc), 0.0)
        out.append(zc)
    return jnp.concatenate(out, axis=0)


def _mix_out_fwd(o, z, m, x, wsp, bsp, wbd, psc, gsv, gout, wout, name):
    s = x.shape[0]
    tm = _tile(s, TOKENS)

    def body(o_ref, uv_ref, m_ref, x_ref, wsp_ref, bsp_ref, wbd_ref, psc_ref, gsv_ref, gout_ref, wout_ref,
             x1_ref, mix_ref):
        g = gout_ref[...]
        an = _rms(o_ref[...], HEADS * VH)[0] * g[:, :512]
        uv = uv_ref[...]
        u, v = uv[:, :SGU], uv[:, SGU:]
        vn = (_rms(v, SGU)[0] * gsv_ref[...]).astype(BF16)
        tri = _tril()
        wsp_m = [jnp.where(tri, wsp_ref[h], 0.0).astype(BF16) for h in range(HEADS)]
        gm = u * _sgu_gate(vn, wsp_m, bsp_ref[...])
        gn = _rms(gm, SGU)[0] * g[:, 512:768]
        po = _dot(m_ref[...].astype(BF16), wbd_ref[...]) * psc_ref[...]
        pn = _rms(po, POOL)[0] * g[:, 768:]
        mix = jnp.concatenate([an, gn, pn], axis=1).astype(BF16)
        mix_ref[...] = mix
        x1_ref[...] = x_ref[...] + _dot(mix, wout_ref[...])

    row = lambda w, j: pl.BlockSpec((tm, w), lambda i: (i, j))
    return pl.pallas_call(
        body, name=name, grid=(s // tm,),
        in_specs=[row(512, 0), row(512, 1), row(POOL, 0), row(D, 0),
                  _acc((HEADS, CHUNK, CHUNK)), _acc((CHUNK, SGU)), _acc((POOL, POOL)), _acc((1, POOL)),
                  _acc((1, SGU)), _acc((1, D)), _res((D, D))],
        out_specs=[row(D, 0), row(D, 0)],
        out_shape=[jax.ShapeDtypeStruct((s, D), F32), jax.ShapeDtypeStruct((s, D), BF16)],
        compiler_params=_cp(("parallel",), VMEM_LIMIT),
    )(o, z, m, x, wsp, bsp, wbd, psc, gsv, gout, wout)


def _ffn_fwd(x1, g, wg, wu, wd, name):
    s = x1.shape[0]
    tm = _tile(s, 256)

    def body(x_ref, g_ref, wg_ref, wu_ref, wd_ref, x2_ref, a_ref, b_ref, h_ref):
        x = x_ref[...]
        h = (_rms(x, D)[0] * g_ref[...]).astype(BF16)
        h_ref[...] = h
        acc = jnp.zeros((tm, D), F32)
        for k in range(CHIPS):
            a = _dot_nt(h, wg_ref[k])
            b = _dot_nt(h, wu_ref[k])
            a_ref[k] = a.astype(BF16)
            b_ref[k] = b.astype(BF16)
            acc = acc + _dot((a * jax.nn.sigmoid(a) * b).astype(BF16), wd_ref[k])
        x2_ref[...] = x + acc

    row = lambda w: pl.BlockSpec((tm, w), lambda i: (i, 0))
    hrow = pl.BlockSpec((CHIPS, tm, SH), lambda i: (0, i, 0))
    hshape = jax.ShapeDtypeStruct((CHIPS, s, SH), BF16)
    return pl.pallas_call(
        body, name=name, grid=(s // tm,),
        in_specs=[row(D), _acc((1, D)), _res((CHIPS, SH, D)), _res((CHIPS, SH, D)), _res((CHIPS, SH, D))],
        out_specs=[row(D), hrow, hrow, row(D)],
        out_shape=[jax.ShapeDtypeStruct((s, D), F32), hshape, hshape, jax.ShapeDtypeStruct((s, D), BF16)],
        compiler_params=_cp(("parallel",), VMEM_LIMIT),
    )(x1, g, wg, wu, wd)


def _loss_grad(y, tgt):
    s = y.shape[0]
    tm = _tile(s, TOKENS)

    def body(y_ref, t_ref, dy_ref, l_ref):
        e = y_ref[...] - t_ref[...]
        dy_ref[...] = e * (1.0 / D)
        sq = jnp.sum(e * e, axis=0, keepdims=True)
        part = sq[:, :LANES]
        for c in range(1, D // LANES):
            part = part + sq[:, c * LANES:(c + 1) * LANES]
        _accumulate(l_ref, part, pl.program_id(0) == 0)

    row = pl.BlockSpec((tm, D), lambda i: (i, 0))
    return pl.pallas_call(
        body, name="loss_grad", grid=(s // tm,),
        in_specs=[row, row], out_specs=[row, _acc((1, LANES))],
        out_shape=[jax.ShapeDtypeStruct((s, D), F32), jax.ShapeDtypeStruct((1, LANES), F32)],
        compiler_params=_cp(("arbitrary",)),
    )(y, tgt)


def _wgrad(a, b, name):
    s, k = a.shape
    n = b.shape[1]
    half = lambda v: v if v <= 1408 else v // 2
    kb, nb, tt = half(k), half(n), _tile(s, 2048)

    def body(a_ref, b_ref, o_ref):
        _accumulate(o_ref, _dot_tn(a_ref[...].astype(BF16), b_ref[...].astype(BF16)), pl.program_id(2) == 0)

    return pl.pallas_call(
        body, name=name, grid=(k // kb, n // nb, s // tt),
        in_specs=[pl.BlockSpec((tt, kb), lambda i, j, t: (t, i)), pl.BlockSpec((tt, nb), lambda i, j, t: (t, j))],
        out_specs=pl.BlockSpec((kb, nb), lambda i, j, t: (i, j)),
        out_shape=jax.ShapeDtypeStruct((k, n), F32),
        compiler_params=_cp(("parallel", "parallel", "arbitrary"), VMEM_LIMIT),
    )(a, b)


def _wgrad_in(h, dzm, duv, dp, name):
    s = h.shape[0]
    tt = _tile(s, 2048)

    def body(h_ref, a_ref, b_ref, c_ref, o_ref):
        hv = h_ref[...]
        val = jnp.concatenate([_dot_tn(hv, a_ref[...]), _dot_tn(hv, b_ref[...]), _dot_tn(hv, c_ref[...])], axis=1)
        _accumulate(o_ref, val, pl.program_id(0) == 0)

    row = lambda w: pl.BlockSpec((tt, w), lambda t: (t, 0))
    return pl.pallas_call(
        body, name=name, grid=(s // tt,), in_specs=[row(D), row(512), row(512), row(POOL)], out_specs=_acc((D, IN_P)),
        out_shape=jax.ShapeDtypeStruct((D, IN_P), F32), compiler_params=_cp(("arbitrary",), VMEM_LIMIT),
    )(h, dzm, duv, dp)


def _wgrad_rows(a, b, name):
    s, n = a.shape[1:]
    nn = b.shape[1]
    tt = _tile(s, 4096 if b.dtype == BF16 else 2048)

    def body(a_ref, b_ref, o_ref):
        _accumulate0(o_ref, _dot_tn(a_ref[0].astype(BF16), b_ref[...].astype(BF16)), pl.program_id(1) == 0)

    return pl.pallas_call(
        body, name=name, grid=(CHIPS, s // tt),
        in_specs=[pl.BlockSpec((1, tt, n), lambda c, t: (c, t, 0)), pl.BlockSpec((tt, nn), lambda c, t: (t, 0))],
        out_specs=pl.BlockSpec((1, n, nn), lambda c, t: (c, 0, 0)),
        out_shape=jax.ShapeDtypeStruct((CHIPS, n, nn), F32),
        compiler_params=_cp(("parallel", "arbitrary"), VMEM_LIMIT),
    )(a, b)


def _ffn_bwd(dx2, x1, a, b, g, wg, wu, wd, name):
    s = x1.shape[0]
    tm = _tile(s, 256)

    def body(dx2_ref, x_ref, a_ref, b_ref, g_ref, wg_ref, wu_ref, wd_ref,
             dx1_ref, hid_ref, da_ref, db_ref, dyb_ref, dg_ref):
        dx2 = dx2_ref[...]
        dyb = dx2.astype(BF16)
        dyb_ref[...] = dyb
        dh = jnp.zeros((tm, D), F32)
        ahead = _dot_nt(dyb, wd_ref[0])
        for k in range(CHIPS):
            av, bv = a_ref[k].astype(F32), b_ref[k].astype(F32)
            dhid = ahead
            if k + 1 < CHIPS:
                ahead = _dot_nt(dyb, wd_ref[k + 1])
            sig = jax.nn.sigmoid(av)
            sa = av * sig
            hid_ref[k] = (sa * bv).astype(BF16)
            dbv = (dhid * sa).astype(BF16)
            dav = (dhid * bv * (sig * (1.0 + av * (1.0 - sig)))).astype(BF16)
            db_ref[k] = dbv
            da_ref[k] = dav
            dh = dh + _dot(dav, wg_ref[k]) + _dot(dbv, wu_ref[k])
        xn, r = _rms(x_ref[...], D)
        dxr, dg = _rms_bwd(xn, r, g_ref[...], dh, D)
        dx1_ref[...] = dx2 + dxr
        _accumulate(dg_ref, dg, pl.program_id(0) == 0)

    row = lambda w: pl.BlockSpec((tm, w), lambda i: (i, 0))
    hrow = pl.BlockSpec((CHIPS, tm, SH), lambda i: (0, i, 0))
    hid = jax.ShapeDtypeStruct((CHIPS, s, SH), BF16)
    return pl.pallas_call(
        body, name=name, grid=(s // tm,),
        in_specs=[row(D), row(D), hrow, hrow, _acc((1, D)), _res((CHIPS, SH, D)), _res((CHIPS, SH, D)),
                  _res((CHIPS, SH, D))],
        out_specs=[row(D), hrow, hrow, hrow, row(D), _acc((1, D))],
        out_shape=[jax.ShapeDtypeStruct((s, D), F32), hid, hid, hid, jax.ShapeDtypeStruct((s, D), BF16),
                   jax.ShapeDtypeStruct((1, D), F32)],
        compiler_params=_cp(("arbitrary",), VMEM_LIMIT),
    )(dx2, x1, a, b, g, wg, wu, wd)


def _mix_out_bwd(dx1, o, z, m, wsp, bsp, wbd, psc, gsv, gout, wout, name):
    s = dx1.shape[0]
    tm = _tile(s, TOKENS)

    def body(dx1_ref, o_ref, uv_ref, m_ref, wsp_ref, bsp_ref, wbd_ref, psc_ref, gsv_ref, gout_ref, wout_ref,
             do_ref, dl_ref, duv_ref, dm_ref, dgo_ref, dgsv_ref, dpsc_ref, dwsp_ref, dbsp_ref, dwbd_ref):
        first = pl.program_id(0) == 0
        g = gout_ref[...]
        dmix = _dot_nt(dx1_ref[...].astype(BF16), wout_ref[...])
        o = o_ref[...]
        on, ro = _rms(o, HEADS * VH)
        do, dga = _rms_bwd(on, ro, g[:, :512], dmix[:, :512], HEADS * VH)
        for h in range(HEADS):
            sl = slice(h * VH, (h + 1) * VH)
            do_ref[h] = do[:, sl].astype(BF16)
            dl_ref[h] = jnp.broadcast_to(jnp.sum(do[:, sl] * o[:, sl], axis=-1, keepdims=True), (tm, LANES))
        uv = uv_ref[...]
        u, v = uv[:, :SGU], uv[:, SGU:]
        vx, rv = _rms(v, SGU)
        vn = (vx * gsv_ref[...]).astype(BF16)
        tri = _tril()
        wsp_m = [jnp.where(tri, wsp_ref[h], 0.0).astype(BF16) for h in range(HEADS)]
        zc = _sgu_gate(vn, wsp_m, bsp_ref[...])
        gm = u * zc
        gmn, rg = _rms(gm, SGU)
        dgm, dgg = _rms_bwd(gmn, rg, g[:, 512:768], dmix[:, 512:768], SGU)
        du = dgm * zc
        dzc = dgm * u
        dvn_parts = []
        dbsp = jnp.zeros((CHUNK, SGU), F32)
        dwsp = [jnp.zeros((CHUNK, CHUNK), F32) for _ in range(HEADS)]
        for cidx in range(tm // CHUNK):
            rs = slice(cidx * CHUNK, (cidx + 1) * CHUNK)
            dzc_c = dzc[rs]
            dbsp = dbsp + dzc_c
            dzb = dzc_c.astype(BF16)
            vc = vn[rs]
            dvn_c = jnp.zeros((CHUNK, SGU), F32)
            for h in range(HEADS):
                hm = _head_mask(h)
                dvn_c = dvn_c + jnp.where(hm, _dot_tn(wsp_m[h], dzb), 0.0)
                dwsp[h] = dwsp[h] + _dot_nt(jnp.where(hm, dzc_c, 0.0).astype(BF16), vc)
            dvn_parts.append(dvn_c)
        dvn = jnp.concatenate(dvn_parts, axis=0)
        dv, dgsv = _rms_bwd(vx, rv, gsv_ref[...], dvn, SGU)
        duv_ref[...] = jnp.concatenate([du, dv], axis=1).astype(BF16)
        mb = m_ref[...].astype(BF16)
        pw = _dot(mb, wbd_ref[...])
        po = pw * psc_ref[...]
        pon, rp = _rms(po, POOL)
        dpo, dgp = _rms_bwd(pon, rp, g[:, 768:], dmix[:, 768:], POOL)
        dpw = (dpo * psc_ref[...]).astype(BF16)
        dm_ref[...] = _dot_nt(dpw, wbd_ref[...])
        _accumulate(dgo_ref, jnp.concatenate([dga, dgg, dgp], axis=1), first)
        _accumulate(dgsv_ref, dgsv, first)
        _accumulate(dpsc_ref, jnp.sum(dpo * pw, axis=0, keepdims=True), first)
        _accumulate(dbsp_ref, dbsp, first)
        _accumulate(dwbd_ref, _dot_tn(mb, dpw), first)
        for h in range(HEADS):
            val = jnp.where(tri, dwsp[h], 0.0)

            @pl.when(first)
            def _(val=val, h=h):
                dwsp_ref[h] = val

            @pl.when(jnp.logical_not(first))
            def _(val=val, h=h):
                dwsp_ref[h] += val

    row = lambda w, j: pl.BlockSpec((tm, w), lambda i: (i, j))
    hspec = pl.BlockSpec((HEADS, tm, HP), lambda i: (0, i, 0))
    return pl.pallas_call(
        body, name=name, grid=(s // tm,),
        in_specs=[row(D, 0), row(512, 0), row(512, 1), row(POOL, 0),
                  _acc((HEADS, CHUNK, CHUNK)), _acc((CHUNK, SGU)),
                  _acc((POOL, POOL)), _acc((1, POOL)), _acc((1, SGU)), _acc((1, D)), _res((D, D))],
        out_specs=[hspec, hspec, row(512, 0), row(POOL, 0), _acc((1, D)), _acc((1, SGU)), _acc((1, POOL)),
                   _acc((HEADS, CHUNK, CHUNK)), _acc((CHUNK, SGU)), _acc((POOL, POOL))],
        out_shape=[jax.ShapeDtypeStruct((HEADS, s, HP), BF16), jax.ShapeDtypeStruct((HEADS, s, LANES), F32),
                   jax.ShapeDtypeStruct((s, 512), BF16), jax.ShapeDtypeStruct((s, POOL), F32),
                   jax.ShapeDtypeStruct((1, D), F32), jax.ShapeDtypeStruct((1, SGU), F32),
                   jax.ShapeDtypeStruct((1, POOL), F32), jax.ShapeDtypeStruct((HEADS, CHUNK, CHUNK), F32),
                   jax.ShapeDtypeStruct((CHUNK, SGU), F32), jax.ShapeDtypeStruct((POOL, POOL), F32)],
        compiler_params=_cp(("arbitrary",), VMEM_LIMIT),
    )(dx1, o, z, m, wsp, bsp, wbd, psc, gsv, gout, wout)


def _attn_bwd(q, k, v, do, lse, delta, after, name):
    s = q.shape[1]
    rh = _tile(s, ATT_ROWS)
    tk = _tile(s, ATT_KEYS)
    nk = s // tk
    wide = ATT_QUERIES if s % ATT_QUERIES == 0 else tk
    pieces = tk // rh

    def body(q_ref, k_ref, v_ref, do_ref, lse_ref, dl_ref, after_ref, dq_ref, dk_ref, dv_ref):
        del after_ref
        j = pl.program_id(1)

        @pl.when(j == 0)
        def _():
            dq_ref[...] = jnp.zeros_like(dq_ref)

        kj, vj = k_ref[0], v_ref[0]

        def blk(start, rows, dks, dvs, diagonal):
            dks, dvs = list(dks), list(dvs)
            offs = [pl.multiple_of(start + g * rh, rh) for g in range(rows // rh)]
            keys = [(g + 1) * rh if diagonal else tk for g in range(rows // rh)]
            qs = [q_ref[0, pl.ds(off, rh), :] for off in offs]
            dos = [do_ref[0, pl.ds(off, rh), :] for off in offs]
            scs = [_dot_nt(qi, kj[:n]) for qi, n in zip(qs, keys)]
            dps = [_dot_nt(doi, vj[:n]) for doi, n in zip(dos, keys)]
            for g, off in enumerate(offs):
                lse_i = lse_ref[0, pl.ds(off, rh), :][:, :1]
                dl_i = dl_ref[0, pl.ds(off, rh), :][:, :1]
                sc = _causal_mask(scs[g], g * rh) if diagonal else scs[g]
                p = jnp.exp2(sc - lse_i)
                ds = (p * (dps[g] - dl_i)).astype(BF16)
                cv = _dot_tn(p.astype(BF16), dos[g])
                ck = _dot_tn(ds, qs[g])
                for t in range(keys[g] // rh):
                    dvs[t] = dvs[t] + cv[t * rh:(t + 1) * rh]
                    dks[t] = dks[t] + ck[t * rh:(t + 1) * rh]
                dq_ref[0, pl.ds(off, rh), :] += _dot(ds, kj[:keys[g]]) * SCALE
            return tuple(dks), tuple(dvs)

        per = wide // tk
        zero = (jnp.zeros((rh, HP), F32),) * pieces
        acc = blk(j * tk, tk, zero, zero, True)
        first_wide = (j + per) // per
        acc = lax.fori_loop(j + 1, jnp.minimum(first_wide * per, nk), lambda i, c: blk(i * tk, tk, *c, False), acc)
        dks, dvs = lax.fori_loop(first_wide, nk // per, lambda i, c: blk(i * wide, wide, *c, False), acc)
        dk_ref[0] = jnp.concatenate(dks, axis=0) * (SCALE / EXP2_C)
        dv_ref[0] = jnp.concatenate(dvs, axis=0)

    full = lambda: pl.BlockSpec((1, s, HP), lambda h, j: (h, 0, 0))
    blk_spec = lambda: pl.BlockSpec((1, tk, HP), lambda h, j: (h, j, 0))
    out = jax.ShapeDtypeStruct((HEADS, s, HP), F32)
    return pl.pallas_call(
        body, name=name, grid=(HEADS, s // tk),
        in_specs=[full(), blk_spec(), blk_spec(), full(), full(), full(), ANY],
        out_specs=[full(), blk_spec(), blk_spec()], out_shape=[out] * 3,
        compiler_params=_cp(("parallel", "arbitrary"), VMEM_LIMIT),
    )(q, k, v, do, lse, delta, after)


def _mla_prep_bwd(dq, dk, dv, z, tabs, gql, gkv, gq, gk, wq, wk, wv, name):
    s = z.shape[0]
    tm = _tile(s, TOKENS)

    def body(dq_ref, dk_ref, dv_ref, ql_ref, kv_ref, kr_ref, c_ref, sa_ref, sb_ref, gql_ref, gkv_ref, gq_ref, gk_ref,
             wq_ref, wk_ref, wv_ref,
             dz_ref, qn_ref, kvn_ref, dqr_ref, dkr_ref, dvr_ref, dgql_ref, dgkv_ref, dgq_ref, dgk_ref):
        first = pl.program_id(0) == 0
        qx, rq = _rms(ql_ref[...], QL)
        qn = (qx * gql_ref[...]).astype(BF16)
        kx, rk = _rms(kv_ref[...], KVL)
        kvn = (kx * gkv_ref[...]).astype(BF16)
        qn_ref[...] = qn
        kvn_ref[...] = kvn
        qraw = _dot(qn, wq_ref[...])
        kraw = _dot(kvn, wk_ref[...])
        kr = kr_ref[...]
        c, sa, sb = c_ref[...], sa_ref[...], sb_ref[...]
        lane = lax.broadcasted_iota(jnp.int32, (tm, HP), 1)
        rope_lanes = (lane >= NOPE) & (lane < QK)
        dkrope = jnp.zeros((tm, HP), F32)
        dgq = jnp.zeros((1, HP), F32)
        dgk = jnp.zeros((1, HP), F32)
        for h in range(HEADS):
            sl = slice(h * HP, (h + 1) * HP)
            xn, r = _rms(qraw[:, sl], QK)
            dx, dg = _rms_bwd(xn, r, gq_ref[...], _rope_t(dq_ref[h], c, sa, sb), QK)
            dqr_ref[:, sl] = dx.astype(BF16)
            dgq = dgq + dg
            xn, r = _rms(kraw[:, sl] + kr, QK)
            dx, dg = _rms_bwd(xn, r, gk_ref[...], _rope_t(dk_ref[h], c, sa, sb), QK)
            dkr_ref[:, sl] = dx.astype(BF16)
            dgk = dgk + dg
            dkrope = dkrope + jnp.where(rope_lanes, dx, 0.0)
            dvr_ref[:, sl] = dv_ref[h].astype(BF16)
        dqn = _dot_nt(dqr_ref[...], wq_ref[...])
        dql, dgql = _rms_bwd(qx, rq, gql_ref[...], dqn, QL)
        dkvn = _dot_nt(dkr_ref[...], wk_ref[...]) + _dot_nt(dvr_ref[...], wv_ref[...])
        dkv, dgkv = _rms_bwd(kx, rk, gkv_ref[...], dkvn, KVL)
        dz_ref[...] = jnp.concatenate([dql, dkv, dkrope], axis=1).astype(BF16)
        _accumulate(dgql_ref, dgql, first)
        _accumulate(dgkv_ref, dgkv, first)
        _accumulate(dgq_ref, dgq, first)
        _accumulate(dgk_ref, dgk, first)

    row = lambda w, j: pl.BlockSpec((tm, w), lambda i: (i, j))
    hspec = pl.BlockSpec((HEADS, tm, HP), lambda i: (0, i, 0))
    sd = lambda w, dt: jax.ShapeDtypeStruct((s, w), dt)
    return pl.pallas_call(
        body, name=name, grid=(s // tm,),
        in_specs=[hspec, hspec, hspec, row(QL, 0), row(KVL, 2), row(HP, 3), row(HP, 0), row(HP, 0), row(HP, 0),
                  _acc((1, QL)), _acc((1, KVL)), _acc((1, HP)), _acc((1, HP)),
                  _acc((QL, HEADS * HP)), _acc((KVL, HEADS * HP)), _acc((KVL, HEADS * HP))],
        out_specs=[row(512, 0), row(QL, 0), row(KVL, 0), row(512, 0), row(512, 0), row(512, 0),
                   _acc((1, QL)), _acc((1, KVL)), _acc((1, HP)), _acc((1, HP))],
        out_shape=[sd(512, BF16), sd(QL, BF16), sd(KVL, BF16), sd(512, BF16), sd(512, BF16), sd(512, BF16),
                   jax.ShapeDtypeStruct((1, QL), F32), jax.ShapeDtypeStruct((1, KVL), F32),
                   jax.ShapeDtypeStruct((1, HP), F32), jax.ShapeDtypeStruct((1, HP), F32)],
        compiler_params=_cp(("arbitrary",), VMEM_LIMIT),
    )(dq, dk, dv, z, z, z, *tabs, gql, gkv, gq, gk, wq, wk, wv)


def _in_proj_bwd(dzm, duv, dp, x, dx1, g, win, name):
    s = x.shape[0]
    tm = _tile(s, TOKENS // 2)

    def body(dzm_ref, duv_ref, dp_ref, x_ref, dx1_ref, g_ref, w_ref, dx_ref, dg_ref):
        groups = [slice(r0, r0 + tm // 2) for r0 in (0, tm // 2)]
        dhs = [_dot_nt(dzm_ref[rs, :], w_ref[:, 0:512]) + _dot_nt(duv_ref[rs, :], w_ref[:, 512:1024])
               + _dot_nt(dp_ref[rs, :], w_ref[:, 1024:IN_P]) for rs in groups]
        dg = jnp.zeros((1, D), F32)
        for rs, dh in zip(groups, dhs):
            xn, r = _rms(x_ref[rs, :], D)
            dxr, dgr = _rms_bwd(xn, r, g_ref[...], dh, D)
            dx_ref[rs, :] = dx1_ref[rs, :] + dxr
            dg = dg + dgr
        _accumulate(dg_ref, dg, pl.program_id(0) == 0)

    row = lambda w: pl.BlockSpec((tm, w), lambda i: (i, 0))
    return pl.pallas_call(
        body, name=name, grid=(s // tm,),
        in_specs=[row(512), row(512), row(POOL), row(D), row(D), _acc((1, D)), _res((D, IN_P))],
        out_specs=[row(D), _acc((1, D))],
        out_shape=[jax.ShapeDtypeStruct((s, D), F32), jax.ShapeDtypeStruct((1, D), F32)],
        compiler_params=_cp(("arbitrary",), VMEM_LIMIT),
    )(dzm, duv, dp, x, dx1, g, win)


def _adamw(w, g0, g1, m, v, name):
    _, r, c = w.shape
    tr = _row_tile(r, 512)
    c1 = 1.0 - B1 ** STEP
    c2 = 1.0 - B2 ** STEP

    def body(w_ref, g0_ref, g1_ref, m_ref, v_ref, g_ref, d_ref, nm_ref, nv_ref):
        gv = jnp.where(pl.program_id(0) == 0, g0_ref[...], g1_ref[...])
        g_ref[0] = gv
        nm = B1 * m_ref[0] + (1.0 - B1) * gv
        nv = B2 * v_ref[0] + (1.0 - B2) * (gv * gv)
        nm_ref[0] = nm
        nv_ref[0] = nv
        d_ref[0] = -LR * ((nm / c1) / (jnp.sqrt(nv / c2) + ADAM_EPS) + WD * w_ref[0])

    spec = pl.BlockSpec((1, tr, c), lambda l, i: (l, i, 0))
    out = jax.ShapeDtypeStruct((DEPTH, r, c), F32)
    return pl.pallas_call(
        body, name=name, grid=(DEPTH, r // tr),
        in_specs=[spec, pl.BlockSpec((tr, c), lambda l, i: (i * (1 - l), 0)), pl.BlockSpec((tr, c), lambda l, i: (i * l, 0)),
                  spec, spec],
        out_specs=[spec] * 4, out_shape=[out] * 4, compiler_params=_cp(("parallel", "parallel")),
    )(w, g0, g1, m, v)


ANY = pl.BlockSpec(memory_space=pl.ANY)


def _place():
    x, y, c = lax.axis_index("x"), lax.axis_index("y"), lax.axis_index("c")
    chips = [(1 - x, y), (x, 1 - y), (1 - x, 1 - y)]
    return x, y, c, chips


def _half_rows(ref, lead, hh, half, align):
    rows = pl.ds(pl.multiple_of(hh * half, align), half)
    return ref.at[rows, :] if lead is None else ref.at[lead, rows, :]


def _row_align(dtype):
    return 16 if dtype == BF16 else 8


def _sems(n):
    return [pltpu.SemaphoreType.DMA((n,)), pltpu.SemaphoreType.DMA((n,)), pltpu.SemaphoreType.DMA((n,))]


def _comm_call(body, ins, out_shapes, nsems, name):
    return pl.pallas_call(
        body, name=name, in_specs=[ANY] * len(ins), out_specs=[ANY] * len(out_shapes), out_shape=out_shapes,
        scratch_shapes=_sems(nsems), compiler_params=pltpu.CompilerParams(has_side_effects=True),
    )(*ins)


def _all_gather_chips(shards, name):
    n = len(shards)
    halves = [a.shape[0] // 2 for a in shards]
    aligns = [_row_align(a.dtype) for a in shards]
    assert all(h % al == 0 for h, al in zip(halves, aligns))

    def body(*refs):
        ins, outs, (send_sems, recv_sems, _) = refs[:n], refs[n:2 * n], refs[2 * n:]
        x, y, c, chips = _place()
        me = 2 * x + y
        sibling = (x, y, 1 - c)

        def copy(sem, src, dst, to):
            return pltpu.make_async_remote_copy(src_ref=src, dst_ref=dst, send_sem=send_sems.at[sem],
                                                recv_sem=recv_sems.at[sem], device_id=to, device_id_type=MESH)

        first, passed = [], []
        for a in range(n):
            my_half = _half_rows(ins[a], None, c, halves[a], aligns[a])
            for j, (cx, cy) in enumerate(chips):
                cp = copy(6 * a + j, my_half, _half_rows(outs[a], me, c, halves[a], aligns[a]), (cx, cy, c))
                cp.start()
                first.append(cp)
        for a in range(n):
            for j, (cx, cy) in enumerate(chips):
                landed = _half_rows(outs[a], 2 * cx + cy, c, halves[a], aligns[a])
                copy(6 * a + j, landed, landed, (cx, cy, c)).wait_recv()
                fwd = copy(6 * a + 3 + j, landed, landed, sibling)
                fwd.start()
                passed.append(fwd)
        for a in range(n):
            for j, (cx, cy) in enumerate(chips):
                other = _half_rows(outs[a], 2 * cx + cy, 1 - c, halves[a], aligns[a])
                copy(6 * a + 3 + j, other, other, sibling).wait_recv()
        for cp in first + passed:
            cp.wait_send()

    lands = _comm_call(body, shards, [jax.ShapeDtypeStruct((CHIPS,) + a.shape, a.dtype) for a in shards], 6 * n, name)
    return _with_own(lands, shards)


def _with_own(lands, shards):
    me = 2 * lax.axis_index("x") + lax.axis_index("y")
    return [lax.dynamic_update_slice(g, a[None], (me, 0, 0)) for g, a in zip(lands, shards)]


def _pair_join(arrs, name):
    n = len(arrs)
    halves = [a.shape[0] // 2 for a in arrs]

    def body(*refs):
        outs, (send_sems, recv_sems, _) = refs[n:2 * n], refs[2 * n:]
        x, y, c, _ = _place()
        cps = []
        for a in range(n):
            mine = _half_rows(outs[a], None, c, halves[a], 8)
            cp = pltpu.make_async_remote_copy(src_ref=mine, dst_ref=mine, send_sem=send_sems.at[a], recv_sem=recv_sems.at[a],
                                              device_id=(x, y, 1 - c), device_id_type=MESH)
            cp.start()
            cps.append(cp)
        for cp in cps:
            cp.wait()

    return pl.pallas_call(
        body, name=name, in_specs=[ANY] * n, out_specs=[ANY] * n,
        out_shape=[jax.ShapeDtypeStruct(a.shape, a.dtype) for a in arrs],
        input_output_aliases={i: i for i in range(n)}, scratch_shapes=_sems(n),
        compiler_params=pltpu.CompilerParams(has_side_effects=True),
    )(*arrs)


HBM = pl.BlockSpec(memory_space=pltpu.HBM)
SEM = pl.BlockSpec(memory_space=pltpu.SEMAPHORE)
DATAFLOW = pltpu.SideEffectType.DATAFLOW_SIDE_EFFECTING


def _remote_copies(pairs, ins, lands, send_sems, recv_sems):
    return [pltpu.make_async_remote_copy(src_ref=src, dst_ref=dst, send_sem=send_sems.at[i], recv_sem=recv_sems.at[i],
                                         device_id=to, device_id_type=MESH)
            for i, (src, dst, to) in enumerate(pairs(ins, lands))]


def _split_start(srcs, land_shapes, ncopies, pairs, name, after):
    n, m = len(srcs), len(land_shapes)

    def body(*refs):
        ins, lands = refs[:n], refs[n:n + m]
        send_sems, recv_sems, token = refs[n + m + 1], refs[n + m + 2], refs[-1]
        for cp in _remote_copies(pairs, ins, lands, send_sems, recv_sems):
            cp.start()
        token[...] = jnp.zeros_like(token)

    hbm = lambda a: pltpu.with_memory_space_constraint(a, pltpu.HBM)
    lands = [hbm(lax.empty(s.shape, s.dtype)) for s in land_shapes]
    thru = [pltpu.HBM(a.shape, a.dtype) for a in list(srcs) + lands]
    out = pl.pallas_call(
        body, name=name,
        out_shape=(pltpu.SemaphoreType.DMA((ncopies,)), pltpu.SemaphoreType.DMA((ncopies,)), *thru,
                   jax.ShapeDtypeStruct((8, LANES), F32)),
        in_specs=[HBM] * (n + m) + [ANY], out_specs=(SEM, SEM, *[HBM] * (n + m), pl.BlockSpec(memory_space=pltpu.VMEM)),
        input_output_aliases={i: 2 + i for i in range(n + m)},
        compiler_params=pltpu.CompilerParams(has_side_effects=DATAFLOW),
    )(*[hbm(a) for a in srcs], *lands, after)
    return out[0], out[1], list(out[2:2 + n]), list(out[2 + n:2 + n + m]), out[-1]


def _split_wait(send_sems, recv_sems, srcs, lands, after, pairs, name):
    n, m = len(srcs), len(lands)

    def body(*refs):
        ins, lands_ = refs[:n], refs[n:n + m]
        for cp in _remote_copies(pairs, ins, lands_, refs[n + m], refs[n + m + 1]):
            cp.wait_send()
            cp.wait_recv()

    out = pl.pallas_call(
        body, name=name, out_shape=tuple(pltpu.HBM(a.shape, a.dtype) for a in list(srcs) + list(lands)),
        in_specs=[HBM] * (n + m) + [SEM, SEM, ANY], out_specs=tuple([HBM] * (n + m)),
        input_output_aliases={i: i for i in range(n + m)},
        compiler_params=pltpu.CompilerParams(has_side_effects=DATAFLOW),
    )(*srcs, *lands, send_sems, recv_sems, after)
    return list(out[:n]), list(out[n:])


def _gather_pairs(halves, aligns):
    def pairs(ins, lands):
        x, y, c, chips = _place()
        me = 2 * x + y
        return [(_half_rows(ins[a], None, c, halves[a], aligns[a]), _half_rows(lands[a], me, c, halves[a], aligns[a]),
                 (cx, cy, c)) for a in range(len(ins)) for cx, cy in chips]
    return pairs


PEERS = 7


def _scatter_pairs(ins, lands):
    x, y, c, chips = _place()
    to = [(cx, cy, c) for cx, cy in chips] + [(cx, cy, 1 - c) for cx, cy in chips] + [(x, y, 1 - c)]
    out = []
    for a in range(len(ins)):
        half = ins[a].shape[1] // 2
        for i, (tx, ty, tc) in enumerate(to):
            out.append((_half_rows(ins[a], 2 * tx + ty, tc, half, 8), lands[a].at[i], (tx, ty, tc)))
    return out


def _gather_finish(shards, lands, name):
    n = len(shards)
    halves = [a.shape[0] // 2 for a in shards]
    aligns = [_row_align(a.dtype) for a in shards]

    def body(*refs):
        outs, (send_sems, recv_sems, _) = refs[n:2 * n], refs[2 * n:]
        x, y, c, chips = _place()
        passed = []
        for a in range(n):
            for j, (cx, cy) in enumerate(chips):
                landed = _half_rows(outs[a], 2 * cx + cy, c, halves[a], aligns[a])
                cp = pltpu.make_async_remote_copy(src_ref=landed, dst_ref=landed, send_sem=send_sems.at[3 * a + j],
                                                  recv_sem=recv_sems.at[3 * a + j], device_id=(x, y, 1 - c),
                                                  device_id_type=MESH)
                cp.start()
                passed.append(cp)
        for a in range(n):
            for j, (cx, cy) in enumerate(chips):
                other = _half_rows(outs[a], 2 * cx + cy, 1 - c, halves[a], aligns[a])
                pltpu.make_async_remote_copy(src_ref=other, dst_ref=other, send_sem=send_sems.at[3 * a + j],
                                             recv_sem=recv_sems.at[3 * a + j], device_id=(x, y, 1 - c),
                                             device_id_type=MESH).wait_recv()
        for cp in passed:
            cp.wait_send()

    lands = pl.pallas_call(
        body, name=name, in_specs=[ANY] * n, out_specs=[ANY] * n,
        out_shape=[jax.ShapeDtypeStruct(a.shape, a.dtype) for a in lands],
        input_output_aliases={i: i for i in range(n)}, scratch_shapes=_sems(3 * n),
        compiler_params=pltpu.CompilerParams(has_side_effects=True),
    )(*lands)
    return _with_own(lands, shards)


def _sum_own_and_landed(own, landed, where, name):
    _, half, cols = landed.shape
    tr = _row_tile(half, 128)
    nt = half // tr

    grid_spec = pltpu.PrefetchScalarGridSpec(
        num_scalar_prefetch=1, grid=(nt,),
        in_specs=[pl.BlockSpec((1, tr, cols), lambda r, w: (w[0], w[1] * nt + r, 0)),
                  pl.BlockSpec((PEERS, tr, cols), lambda r, w: (0, r, 0))],
        out_specs=pl.BlockSpec((tr, cols), lambda r, w: (w[1] * nt + r, 0)))

    def body(w_ref, p_ref, q_ref, o_ref):
        acc = p_ref[0]
        for i in range(PEERS):
            acc = acc + q_ref[i]
        o_ref[...] = acc

    return pl.pallas_call(
        body, name=name, grid_spec=grid_spec, out_shape=jax.ShapeDtypeStruct((2 * half, cols), own.dtype),
        compiler_params=_cp(("parallel",)),
    )(where, own, landed)


BIG = [("w_in", (D, IN_W), 1), ("w_q_up", (QL, HEADS * QK), 1), ("w_kv_up", (KVL, HEADS * (NOPE + VH)), 1),
       ("w_out", (D, D), 0), ("w_gate", (D, HID), 1), ("w_up", (D, HID), 1), ("w_down", (HID, D), 0)]
SMALL = [("g_mix_norm", (D,)), ("g_q_lat", (QL,)), ("g_kv_lat", (KVL,)), ("g_q_head", (QK,)), ("g_k_head", (QK,)),
         ("g_sgu_v", (SGU,)), ("w_spatial", (HEADS, CHUNK, CHUNK)), ("b_spatial", (HEADS, CHUNK)),
         ("w_pool", (4, 64, 64)), ("pool_scale", (POOL,)), ("g_out_mla", (512,)), ("g_out_sgu", (SGU,)),
         ("g_out_pool", (POOL,)), ("g_ffn_norm", (D,))]
ORDER = ["g_mix_norm", "w_in", "g_q_lat", "w_q_up", "g_kv_lat", "w_kv_up", "g_q_head", "g_k_head", "g_sgu_v",
         "w_spatial", "b_spatial", "w_pool", "pool_scale", "g_out_mla", "g_out_sgu", "g_out_pool", "w_out",
         "g_ffn_norm", "w_gate", "w_up", "w_down"]
EARLY_BIG = ["w_in", "w_q_up", "w_kv_up"]
FFN_BIG = ["w_gate", "w_up", "w_down"]
LATE_BIG = ["w_out"] + FFN_BIG
DEPTH = 2
COLS = 1024
SMALL_N = sum(math.prod(s) for _, s in SMALL) * DEPTH
assert SMALL_N % CHIPS == 0
SMALL_ROWS = -(-(SMALL_N // CHIPS) // (16 * COLS)) * 16


def _unsplit_cols(g):
    return g.transpose(1, 0, 2).reshape(g.shape[1], CHIPS * g.shape[2])


def _split_cols(full):
    r, c = full.shape
    return full.reshape(r, CHIPS, c // CHIPS).transpose(1, 0, 2)


def _kernel_weights(g):
    win = _unsplit_cols(g["w_in"])
    zeros = lambda r, c: jnp.zeros((r, c), BF16)
    o2, o3, o4 = QL + KVL, QL + KVL + ROPE, QL + KVL + ROPE + 2 * SGU
    win_p = jnp.concatenate([win[:, :o2], zeros(D, NOPE), win[:, o2:o3], zeros(D, HP - QK), win[:, o3:o4], win[:, o4:]], axis=1)
    wq = _unsplit_cols(g["w_q_up"]).reshape(QL, HEADS, QK)
    wq_p = jnp.pad(wq, ((0, 0), (0, 0), (0, HP - QK))).reshape(QL, HEADS * HP)
    wkv = _unsplit_cols(g["w_kv_up"]).reshape(KVL, HEADS, NOPE + VH)
    wk_p = jnp.pad(wkv[:, :, :NOPE], ((0, 0), (0, 0), (0, HP - NOPE))).reshape(KVL, HEADS * HP)
    wv_p = wkv[:, :, NOPE:].reshape(KVL, HEADS * VH)
    return dict(win=win_p, wq=wq_p, wk=wk_p, wv=wv_p)


def _small_operands(p, l):
    row = lambda v: v.reshape(1, -1)
    pad = lambda v: jnp.pad(v, (0, HP - QK)).reshape(1, HP)
    wpool = p["w_pool"][l]
    wbd = jnp.zeros((POOL, POOL), F32)
    for g in range(4):
        wbd = lax.dynamic_update_slice(wbd, wpool[g], (g * 64, g * 64))
    return dict(
        g_mix=row(p["g_mix_norm"][l]), gql=row(p["g_q_lat"][l]), gkv=row(p["g_kv_lat"][l]),
        gq=pad(p["g_q_head"][l]), gk=pad(p["g_k_head"][l]), gsv=row(p["g_sgu_v"][l]),
        wsp=p["w_spatial"][l], bsp=jnp.repeat(p["b_spatial"][l].T, SGU // HEADS, axis=1),
        wbd=wbd.astype(BF16), psc=row(p["pool_scale"][l]),
        gout=jnp.concatenate([p["g_out_mla"][l], p["g_out_sgu"][l], p["g_out_pool"][l]]).reshape(1, D),
        g_ffn=row(p["g_ffn_norm"][l]))


def _big_grads(g):
    dwin = g["win"]
    o2 = QL + KVL
    gin = jnp.concatenate([dwin[:, :o2], dwin[:, o2 + NOPE:o2 + NOPE + ROPE], dwin[:, 512:]], axis=1)
    gq = g["wq"].reshape(QL, HEADS, HP)[:, :, :QK].reshape(QL, HEADS * QK)
    gk = g["wk"].reshape(KVL, HEADS, HP)[:, :, :NOPE]
    gv = g["wv"].reshape(KVL, HEADS, VH)
    gkv = jnp.concatenate([gk, gv], axis=2).reshape(KVL, HEADS * (NOPE + VH))
    return {"w_in": _split_cols(gin), "w_q_up": _split_cols(gq), "w_kv_up": _split_cols(gkv),
            "w_out": g["wout"].reshape(CHIPS, D // CHIPS, D), "w_gate": g["wg"], "w_up": g["wu"], "w_down": g["wd"]}


TRANSPOSED = ("w_gate", "w_up")


def _small_grads(g):
    go = g["gout"].reshape(-1)
    return {"g_mix_norm": g["g_mix"].reshape(-1), "g_q_lat": g["gql"].reshape(-1), "g_kv_lat": g["gkv"].reshape(-1),
            "g_q_head": g["gq"].reshape(-1)[:QK], "g_k_head": g["gk"].reshape(-1)[:QK], "g_sgu_v": g["gsv"].reshape(-1),
            "w_spatial": g["wsp"], "b_spatial": g["bsp"].reshape(CHUNK, HEADS, SGU // HEADS).sum(-1).T,
            "w_pool": jnp.stack([g["wbd"][i * 64:(i + 1) * 64, i * 64:(i + 1) * 64] for i in range(4)]),
            "pool_scale": g["psc"].reshape(-1), "g_out_mla": go[:512], "g_out_sgu": go[512:768],
            "g_out_pool": go[768:], "g_ffn_norm": g["g_ffn"].reshape(-1)}


def _pack_small_grads(small):
    sm = jnp.concatenate([small[l][n].reshape(-1) for l in range(DEPTH) for n, _ in SMALL]).reshape(CHIPS, SMALL_N // CHIPS)
    return jnp.pad(sm, ((0, 0), (0, SMALL_ROWS * COLS - SMALL_N // CHIPS))).reshape(CHIPS, SMALL_ROWS, COLS)


def _unpack_small_grads(gathered):
    flat = gathered.reshape(CHIPS, SMALL_ROWS * COLS)[:, :SMALL_N // CHIPS].reshape(-1)
    out, off = [], 0
    for _ in range(DEPTH):
        layer = {}
        for n, shape in SMALL:
            k = math.prod(shape)
            layer[n] = flat[off:off + k].reshape(shape)
            off += k
        out.append(layer)
    return out


def _layer_fwd(x, tabs, kw, late_weights, sp, l):
    t = f"_l{l}"
    z, hb = _in_proj_fwd(x, sp["g_mix"], kw["win"], "in_proj_fwd" + t)
    q, k, v = _mla_prep_fwd(z, tabs, sp["gql"], sp["gkv"], sp["gq"], sp["gk"], kw["wq"], kw["wk"], kw["wv"],
                            "mla_prep_fwd" + t)
    o, lse = _attn_fwd(q, k, v, "attn_fwd" + t)
    m = _pool_win_fwd(z, "pool_win_fwd" + t)
    wout, wg, wu, wd = late_weights(o)
    wout = wout.reshape(D, D)
    x1, mix = _mix_out_fwd(o, z, m, x, sp["wsp"], sp["bsp"], sp["wbd"], sp["psc"], sp["gsv"], sp["gout"], wout,
                           "mix_out_fwd" + t)
    x2, a, b, h2 = _ffn_fwd(x1, sp["g_ffn"], wg, wu, wd, "ffn_fwd" + t)
    saved = dict(x=x, z=z, hb=hb, q=q, k=k, v=v, o=o, lse=lse, m=m, x1=x1, mix=mix, a=a, b=b, h2=h2, wg=wg, wu=wu, wd=wd,
                 wout=wout)
    return x2, saved


def _layer_bwd(dx2, sv, tabs, kw, sp, l, ffn_hook, out_hook):
    t = f"_l{l}"
    g = {}
    dx1, hid, da, db, dyb, g["g_ffn"] = _ffn_bwd(dx2, sv["x1"], sv["a"], sv["b"], sp["g_ffn"], sv["wg"], sv["wu"],
                                                 sv["wd"], "ffn_bwd" + t)
    g["wd"] = _wgrad_rows(hid, dyb, "wgrad_down" + t)
    g["wg"] = _wgrad_rows(da, sv["h2"], "wgrad_gate" + t)
    g["wu"] = _wgrad_rows(db, sv["h2"], "wgrad_up" + t)
    gout = sp["gout"] + ffn_hook(g)
    do, delta, duv, dm, g["gout"], g["gsv"], g["psc"], g["wsp"], g["bsp"], g["wbd"] = _mix_out_bwd(
        dx1, sv["o"], sv["z"], sv["m"], sp["wsp"], sp["bsp"], sp["wbd"], sp["psc"], sp["gsv"], gout, sv["wout"],
        "mix_out_bwd" + t)
    g["wout"] = _wgrad(sv["mix"], dx1, "wgrad_out" + t)
    dp = _pool_win_bwd(dm, "pool_win_bwd" + t)
    dq, dk, dv = _attn_bwd(sv["q"], sv["k"], sv["v"], do, sv["lse"], delta, out_hook(g), "attn_bwd" + t)
    dzm, qn, kvn, dqr, dkr, dvr, g["gql"], g["gkv"], g["gq"], g["gk"] = _mla_prep_bwd(
        dq, dk, dv, sv["z"], tabs, sp["gql"], sp["gkv"], sp["gq"], sp["gk"], kw["wq"], kw["wk"], kw["wv"],
        "mla_prep_bwd" + t)
    g["wq"] = _wgrad(qn, dqr, "wgrad_q_up" + t)
    g["wk"] = _wgrad(kvn, dkr, "wgrad_k_up" + t)
    g["wv"] = _wgrad(kvn, dvr, "wgrad_v_up" + t)
    dx, g["g_mix"] = _in_proj_bwd(dzm, duv, dp, sv["x"], dx1, sp["g_mix"], kw["win"], "in_proj_bwd" + t)
    g["win"] = _wgrad_in(sv["hb"], dzm, duv, dp, "wgrad_in" + t)
    return dx, g


def _rope_inv_freq():
    half = ROPE // 2
    inv = 1.0 / (ROPE_THETA ** (jnp.arange(half, dtype=F32) / half))
    return jnp.concatenate([jnp.zeros((NOPE,), F32), inv, inv, jnp.zeros((HP - QK,), F32)]).reshape(1, HP)


def kernel(x, positions, g_mix_norm, w_in, g_q_lat, w_q_up, g_kv_lat, w_kv_up, g_q_head, g_k_head, g_sgu_v, w_spatial, b_spatial, w_pool, pool_scale, g_out_mla, g_out_sgu, g_out_pool, w_out, g_ffn_norm, w_gate, w_up, w_down, loss_target, m_g_mix_norm, m_w_in, m_g_q_lat, m_w_q_up, m_g_kv_lat, m_w_kv_up, m_g_q_head, m_g_k_head, m_g_sgu_v, m_w_spatial, m_b_spatial, m_w_pool, m_pool_scale, m_g_out_mla, m_g_out_sgu, m_g_out_pool, m_w_out, m_g_ffn_norm, m_w_gate, m_w_up, m_w_down, v_g_mix_norm, v_w_in, v_g_q_lat, v_w_q_up, v_g_kv_lat, v_w_kv_up, v_g_q_head, v_g_k_head, v_g_sgu_v, v_w_spatial, v_b_spatial, v_w_pool, v_pool_scale, v_g_out_mla, v_g_out_sgu, v_g_out_pool, v_w_out, v_g_ffn_norm, v_w_gate, v_w_up, v_w_down):
    given = dict(locals())
    p = {n: given[n] for n in ORDER}
    view = lambda pre, n: jnp.swapaxes(given[pre + n], 1, 2) if n in TRANSPOSED else given[pre + n]
    seq = x.shape[1]
    where = jnp.stack([2 * lax.axis_index("x") + lax.axis_index("y"), lax.axis_index("c")]).astype(jnp.int32)
    shards = lambda names: [view("", n)[l].astype(BF16) for l, n in names]
    zero11 = lambda token: token[:1, :1]

    names_0a = [(0, n) for n in EARLY_BIG]
    names_0b = [(0, n) for n in LATE_BIG]
    names_1 = [(1, n) for n, _, _ in BIG]
    got_0a = dict(zip(EARLY_BIG, _all_gather_chips(shards(names_0a), "all_gather_w0a")))
    started, issued = {}, got_0a["w_in"]
    for tag, names in (("w0b", names_0b), ("w1", names_1)):
        sh = shards(names)
        pairs = _gather_pairs([a.shape[0] // 2 for a in sh], [_row_align(a.dtype) for a in sh])
        lands = [jax.ShapeDtypeStruct((CHIPS,) + a.shape, a.dtype) for a in sh]
        started[tag] = (sh, pairs) + _split_start(sh, lands, 3 * len(sh), pairs, "gather_start_" + tag, issued)
        issued = started[tag][6]

    def arrived(tag, after):
        _, pairs, send, recv, srcs, lands, _ = started[tag]
        srcs, lands = _split_wait(send, recv, srcs, lands, after, pairs, "gather_wait_" + tag)
        return _gather_finish(srcs, lands, "gather_finish_" + tag)

    layer1 = {}

    def mix_weights(l, h):
        if l == 0:
            return got_0a
        layer1.update(zip([n for _, n in names_1], arrived("w1", h)))
        return layer1

    def late_weights(l, o):
        return arrived("w0b", o) if l == 0 else [layer1[n] for n in LATE_BIG]

    reducing, last = {}, {}

    def reduce_start(tag, arrs):
        lands = [jax.ShapeDtypeStruct((PEERS, a.shape[1] // 2, a.shape[2]), a.dtype) for a in arrs]
        reducing[tag] = _split_start(arrs, lands, PEERS * len(arrs), _scatter_pairs, "grad_scatter_start_" + tag, where)
        return zero11(reducing[tag][4])

    def reduce_finish(tag, after):
        send, recv, srcs, lands, _ = reducing[tag]
        srcs, lands = _split_wait(send, recv, srcs, lands, after, _scatter_pairs, "grad_scatter_wait_" + tag)
        return [_sum_own_and_landed(a, q, where, f"grad_sum_{tag}_{i}") for i, (a, q) in enumerate(zip(srcs, lands))]

    def ffn_hook(l, g):
        if l == 1:
            return jnp.zeros((1, 1), F32)
        return reduce_start("g0b", [g["wg"], g["wu"], g["wd"]])

    def out_hook(l, g):
        if l == 1:
            return where
        reduce_start("g0c", [g["wout"].reshape(CHIPS, D // CHIPS, D)])
        return reducing["g0c"][4]

    def layer_hook(l, big, small):
        last[l] = (big, small)
        if l == 1:
            return reduce_start("g1", [big[n] for n, _, _ in BIG])
        return None

    entry = zero11(started["w0b"][6]) + zero11(started["w1"][6])
    loss_part, dx = _step(x.reshape(seq, D), positions.reshape(seq, 1), loss_target.reshape(seq, D), p, entry,
                          mix_weights, late_weights, ffn_hook, out_hook, layer_hook)
    loss = lax.psum(loss_part, ("x", "y", "c"))

    def adamw(n, g0, g1):
        flip = n in EARLY_BIG
        pick = lambda pre: jnp.swapaxes(given[pre + n], 1, 2) if flip else view(pre, n)
        w = pick("")
        three_d = (DEPTH, -1, w.shape[-1])
        g0, g1 = (g.T if flip else g for g in (g0, g1))
        res = _adamw(w.reshape(three_d), g0.reshape(three_d[1:]), g1.reshape(three_d[1:]),
                     pick("m_").reshape(three_d), pick("v_").reshape(three_d), "adamw_" + n)
        return [jnp.swapaxes(r.reshape(w.shape), 1, 2) if flip else r.reshape(w.shape) for r in res]

    names_rest = [(0, n) for n in EARLY_BIG]
    reduce_start("g0a", [last[0][0][n] for _, n in names_rest] + [_pack_small_grads([last[l][1] for l in range(DEPTH)])])
    token = reducing["g0a"][4]
    early = names_1 + [(0, n) for n in FFN_BIG] + [(0, "w_out")]
    landed = reduce_finish("g1", token) + reduce_finish("g0b", token) + reduce_finish("g0c", token)
    sums = dict(zip(early, _pair_join(landed, "grad_pair_join_early")))
    out = {n: adamw(n, sums[(0, n)], sums[(1, n)]) for n in FFN_BIG}
    late = names_rest + ["small"]
    sums.update(zip(late, _pair_join(reduce_finish("g0a", out["w_down"][1]), "grad_pair_join_late")))
    gsmall = _unpack_small_grads(_all_gather_chips([sums["small"]], "all_gather_small_grads")[0])
    for n in ORDER:
        if n not in out:
            g = [sums[(l, n)] for l in range(DEPTH)] if (0, n) in sums else [gsmall[l][n] for l in range(DEPTH)]
            out[n] = adamw(n, *g)
    undo = lambda n, a: jnp.swapaxes(a, 1, 2) if n in TRANSPOSED else a
    return (loss, dx.reshape(x.shape), *[undo(n, out[n][i]) for i in range(4) for n in ORDER])


def _step(xs, pos, tgt, p, entry, mix_weights, late_weights, ffn_hook, out_hook, layer_hook):
    sps = [_small_operands(p, l) for l in range(DEPTH)]
    sps[0]["g_mix"] = sps[0]["g_mix"] + entry
    tabs = _rope_tables(pos, _rope_inv_freq())
    saved, h = [], xs
    for l in range(DEPTH):
        kw = _kernel_weights(mix_weights(l, h))
        h, sv = _layer_fwd(h, tabs, kw, functools.partial(late_weights, l), sps[l], l)
        saved.append(dict(sv, kw=kw))
    dy, lpart = _loss_grad(h, tgt)
    for l in reversed(range(DEPTH)):
        dy, g = _layer_bwd(dy, saved[l], tabs, saved[l]["kw"], sps[l], l, functools.partial(ffn_hook, l),
                           functools.partial(out_hook, l))
        zero = layer_hook(l, _big_grads(g), _small_grads(g))
        if zero is not None and l > 0:
            sps[l - 1]["g_ffn"] = sps[l - 1]["g_ffn"] + zero
    return 0.5 / D * jnp.sum(lpart), dy
```

```python
import functools
import math

import jax
import jax.numpy as jnp
from jax import lax
from jax.experimental import pallas as pl
from jax.experimental.pallas import tpu as pltpu

F32 = jnp.float32
BF16 = jnp.bfloat16
MESH = pl.DeviceIdType.MESH

D = 1024
HEADS = 4
QK = 96
NOPE = 64
ROPE = 32
VH = 128
HP = 128
QL = 256
KVL = 128
SGU = 256
POOL = 256
CHUNK = 128
HID = 2816
CHIPS = 4
SH = HID // CHIPS
IN_W = 1184
IN_P = 1280
EPS = 1e-6
ROPE_THETA = 10000.0
SCALE = 1.0 / math.sqrt(QK)
LOG2E = 1.4426950408889634
EXP2_C = SCALE * LOG2E
ATT_WIDE = 2
ATT_FWD_QUERIES = 2048
ATT_PIECE = 1024
ATT_ROWS = 256
ATT_KEYS = 1024
ATT_QUERIES = 2048
NEG = -1e30
HALO = 16

LR, B1, B2, ADAM_EPS, WD, STEP = 0.001, 0.9, 0.999, 1e-08, 0.01, 10

VMEM_LIMIT = 56 * 1024 * 1024
LANES = 128
TOKENS = 1024


def _cp(sem, vmem=None):
    return pltpu.CompilerParams(dimension_semantics=sem, vmem_limit_bytes=vmem)


def _res(shape):
    nd = len(shape)
    return pl.BlockSpec(shape, lambda *_: (0,) * nd, pipeline_mode=pl.Buffered(1))


def _acc(shape):
    nd = len(shape)
    return pl.BlockSpec(shape, lambda *_: (0,) * nd)


def _dot(a, b):
    return jnp.dot(a, b, preferred_element_type=F32)


def _dot_nt(a, b):
    return lax.dot_general(a, b, (((1,), (1,)), ((), ())), preferred_element_type=F32)


def _dot_tn(a, b):
    return lax.dot_general(a, b, (((0,), (0,)), ((), ())), preferred_element_type=F32)


def _rms(x, n):
    r = lax.rsqrt(jnp.sum(x * x, axis=-1, keepdims=True) * (1.0 / n) + EPS)
    return x * r, r


def _rms_bwd(xn, r, g, dy, n):
    dn = dy * g
    dx = r * (dn - xn * (jnp.sum(dn * xn, axis=-1, keepdims=True) * (1.0 / n)))
    return dx, jnp.sum(dy * xn, axis=0, keepdims=True)


def _accumulate(ref, val, first):
    @pl.when(first)
    def _():
        ref[...] = val

    @pl.when(jnp.logical_not(first))
    def _():
        ref[...] += val


def _accumulate0(ref, val, first):
    @pl.when(first)
    def _():
        ref[0] = val

    @pl.when(jnp.logical_not(first))
    def _():
        ref[0] += val


def _tile(s, t):
    return min(s, t)


def _row_tile(r, cap):
    if r <= cap:
        return r
    return max(t for t in range(8, cap + 1, 8) if r % t == 0)


def _rope_tables(pos, invf):
    s = pos.shape[0]
    tm = _tile(s, 1024)

    def body(pos_ref, invf_ref, c_ref, sa_ref, sb_ref):
        ang = pos_ref[...].astype(F32) * invf_ref[...]
        c, sn = jnp.cos(ang), jnp.sin(ang)
        lane = lax.broadcasted_iota(jnp.int32, ang.shape, 1)
        first = (lane >= NOPE) & (lane < NOPE + ROPE // 2)
        second = (lane >= NOPE + ROPE // 2) & (lane < QK)
        c_ref[...] = jnp.where(first | second, c, 1.0)
        sa_ref[...] = jnp.where(first, -sn, 0.0)
        sb_ref[...] = jnp.where(second, sn, 0.0)

    out = jax.ShapeDtypeStruct((s, HP), F32)
    return pl.pallas_call(
        body, name="rope_tables", grid=(s // tm,),
        in_specs=[pl.BlockSpec((tm, 1), lambda i: (i, 0)), _acc((1, HP))],
        out_specs=[pl.BlockSpec((tm, HP), lambda i: (i, 0))] * 3,
        out_shape=[out] * 3, compiler_params=_cp(("parallel",)),
    )(pos, invf)


def _rope(x, c, sa, sb):
    return x * c + pltpu.roll(x, HP - ROPE // 2, 1) * sa + pltpu.roll(x, ROPE // 2, 1) * sb


def _rope_t(d, c, sa, sb):
    return d * c + pltpu.roll(d * sa, ROPE // 2, 1) + pltpu.roll(d * sb, HP - ROPE // 2, 1)


def _in_proj_fwd(x, g, w, name):
    s = x.shape[0]
    tm = _tile(s, TOKENS)

    def body(x_ref, g_ref, w_ref, z_ref, h_ref):
        xn, _ = _rms(x_ref[...], D)
        h = (xn * g_ref[...]).astype(BF16)
        h_ref[...] = h
        z_ref[...] = _dot(h, w_ref[...])

    return pl.pallas_call(
        body, name=name, grid=(s // tm,),
        in_specs=[pl.BlockSpec((tm, D), lambda i: (i, 0)), _acc((1, D)), _res((D, IN_P))],
        out_specs=[pl.BlockSpec((tm, IN_P), lambda i: (i, 0)), pl.BlockSpec((tm, D), lambda i: (i, 0))],
        out_shape=[jax.ShapeDtypeStruct((s, IN_P), F32), jax.ShapeDtypeStruct((s, D), BF16)],
        compiler_params=_cp(("parallel",), VMEM_LIMIT),
    )(x, g, w)


def _mla_prep_fwd(z, tabs, gql, gkv, gq, gk, wq, wk, wv, name):
    s = z.shape[0]
    tm = _tile(s, TOKENS)

    def body(ql_ref, kv_ref, kr_ref, c_ref, sa_ref, sb_ref, gql_ref, gkv_ref, gq_ref, gk_ref,
             wq_ref, wk_ref, wv_ref, q_out, k_out, v_out):
        qn = (_rms(ql_ref[...], QL)[0] * gql_ref[...]).astype(BF16)
        kvn = (_rms(kv_ref[...], KVL)[0] * gkv_ref[...]).astype(BF16)
        qraw = _dot(qn, wq_ref[...])
        kraw = _dot(kvn, wk_ref[...])
        vraw = _dot(kvn, wv_ref[...])
        kr = kr_ref[...]
        c, sa, sb = c_ref[...], sa_ref[...], sb_ref[...]
        for h in range(HEADS):
            sl = slice(h * HP, (h + 1) * HP)
            xq = _rms(qraw[:, sl], QK)[0] * gq_ref[...]
            q_out[h] = (_rope(xq, c, sa, sb) * EXP2_C).astype(BF16)
            xk = _rms(kraw[:, sl] + kr, QK)[0] * gk_ref[...]
            k_out[h] = _rope(xk, c, sa, sb).astype(BF16)
            v_out[h] = vraw[:, sl].astype(BF16)

    row = lambda w, j: pl.BlockSpec((tm, w), lambda i: (i, j))
    hspec = pl.BlockSpec((HEADS, tm, HP), lambda i: (0, i, 0))
    hshape = jax.ShapeDtypeStruct((HEADS, s, HP), BF16)
    return pl.pallas_call(
        body, name=name, grid=(s // tm,),
        in_specs=[row(QL, 0), row(KVL, 2), row(HP, 3), row(HP, 0), row(HP, 0), row(HP, 0),
                  _acc((1, QL)), _acc((1, KVL)), _acc((1, HP)), _acc((1, HP)),
                  _acc((QL, HEADS * HP)), _acc((KVL, HEADS * HP)), _acc((KVL, HEADS * HP))],
        out_specs=[hspec] * 3, out_shape=[hshape] * 3,
        compiler_params=_cp(("parallel",)),
    )(z, z, z, *tabs, gql, gkv, gq, gk, wq, wk, wv)


def _causal_mask(s, row0):
    row = lax.broadcasted_iota(jnp.int32, s.shape, 0) + row0
    col = lax.broadcasted_iota(jnp.int32, s.shape, 1)
    return jnp.where(col <= row, s, NEG)


def _attn_fwd(q, k, v, name):
    s = q.shape[1]
    tq = _tile(s, ATT_FWD_QUERIES)
    rh = _tile(s, ATT_ROWS)
    kp = _tile(s, ATT_PIECE)
    wide = ATT_WIDE * kp if s % (ATT_WIDE * kp) == 0 else tq
    groups = tq // rh

    def body(q_ref, k_ref, v_ref, o_ref, lse_ref):
        i = pl.program_id(1)

        def blk(off, tk, carry, diagonal):
            width = lambda g, t: max(0, min(kp, (g + 1) * rh - t * kp)) if diagonal else kp
            rows = lambda t: pl.ds(pl.multiple_of(off + t * kp, kp), kp)
            score = lambda g, t: _dot_nt(q_ref[0, g * rh:(g + 1) * rh, :], k_ref[0, rows(t), :][:width(g, t)])
            live = lambda t: [g for g in range(groups) if width(g, t) > 0]
            state = list(carry)
            scs = {(g, 0): score(g, 0) for g in live(0)}
            for t in range(tk // kp):
                if (t + 1) * kp < tk:
                    scs.update({(g, t + 1): score(g, t + 1) for g in live(t + 1)})
                vt = v_ref[0, rows(t), :]
                for g in live(t):
                    m, l, acc = state[g]
                    sc = scs.pop((g, t))
                    if diagonal and (g + 1) * rh <= (t + 1) * kp:
                        sc = _causal_mask(sc, g * rh - t * kp)
                    m_new = jnp.maximum(m, jnp.max(sc, axis=-1, keepdims=True))
                    p = jnp.exp2(sc - m_new)
                    alpha = jnp.exp2(m - m_new)
                    l = alpha * l + jnp.sum(p, axis=-1, keepdims=True)
                    acc = alpha * acc + _dot(p.astype(BF16), vt[:width(g, t)])
                    state[g] = (m_new, l, acc)
            return tuple(state)

        one = (jnp.full((rh, 1), NEG, F32), jnp.zeros((rh, 1), F32), jnp.zeros((rh, VH), F32))
        nwide = (i * tq) // wide
        carry = lax.fori_loop(0, nwide, lambda j, c: blk(j * wide, wide, c, False), (one,) * groups)
        carry = lax.fori_loop(nwide * (wide // tq), i, lambda j, c: blk(j * tq, tq, c, False), carry)
        carry = blk(i * tq, tq, carry, True)
        for g, (m, l, acc) in enumerate(carry):
            o_ref[g * rh:(g + 1) * rh, :] = acc / l
            lse_ref[0, g * rh:(g + 1) * rh, :] = jnp.broadcast_to(m + jnp.log(l) * LOG2E, (rh, LANES))

    return pl.pallas_call(
        body, name=name, grid=(HEADS, s // tq),
        in_specs=[pl.BlockSpec((1, tq, HP), lambda h, i: (h, i, 0)),
                  pl.BlockSpec((1, s, HP), lambda h, i: (h, 0, 0)),
                  pl.BlockSpec((1, s, HP), lambda h, i: (h, 0, 0))],
        out_specs=[pl.BlockSpec((tq, VH), lambda h, i: (i, h)),
                   pl.BlockSpec((1, tq, LANES), lambda h, i: (h, i, 0))],
        out_shape=[jax.ShapeDtypeStruct((s, HEADS * VH), F32), jax.ShapeDtypeStruct((HEADS, s, LANES), F32)],
        compiler_params=_cp(("parallel", "arbitrary"), VMEM_LIMIT),
    )(q, k, v)


def _lane_group(shape, j):
    return (lax.broadcasted_iota(jnp.int32, shape, 1) + j * LANES) // (POOL // 4)


def _pool_win_fwd(z, name):
    s = z.shape[0]
    ch = _tile(s, 512)
    col0 = (IN_P - POOL) // LANES

    def body(p_ref, m_ref):
        j = pl.program_id(0)

        def chunk(r, _):
            off = pl.multiple_of(r * ch, ch)
            cur = p_ref[pl.ds(off, ch), :]
            hoff = pl.multiple_of(jnp.maximum(off - HALO, 0), 8)
            halo = jnp.where(r > 0, p_ref[pl.ds(hoff, HALO), :], 0.0)
            x = jnp.concatenate([halo, cur], axis=0)
            s2 = x + pltpu.roll(x, 1, 0)
            s4 = s2 + pltpu.roll(s2, 2, 0)
            s8 = s4 + pltpu.roll(s4, 4, 0)
            s16 = s8 + pltpu.roll(s8, 8, 0)
            grp = _lane_group((ch, LANES), j)
            sel = jnp.where(grp == 0, s2[HALO:], jnp.where(grp == 1, s4[HALO:], jnp.where(grp == 2, s8[HALO:], s16[HALO:])))
            t1 = (lax.broadcasted_iota(jnp.int32, (ch, LANES), 0) + off + 1).astype(F32)
            win = jnp.where(grp == 0, 2.0, jnp.where(grp == 1, 4.0, jnp.where(grp == 2, 8.0, 16.0)))
            m_ref[pl.ds(off, ch), :] = sel / jnp.minimum(t1, win) - cur
            return 0

        lax.fori_loop(0, s // ch, chunk, 0)

    return pl.pallas_call(
        body, name=name, grid=(POOL // LANES,),
        in_specs=[pl.BlockSpec((s, LANES), lambda j: (0, col0 + j))],
        out_specs=pl.BlockSpec((s, LANES), lambda j: (0, j)),
        out_shape=jax.ShapeDtypeStruct((s, POOL), F32),
        compiler_params=_cp(("parallel",), VMEM_LIMIT),
    )(z)


def _pool_win_bwd(dm, name):
    s = dm.shape[0]
    ch = _tile(s, 512)
    n = s // ch

    def body(dm_ref, dp_ref):
        j = pl.program_id(0)

        def chunk(r, _):
            off = pl.multiple_of(r * ch, ch)
            grp = _lane_group((ch + HALO, LANES), j)
            win = jnp.where(grp == 0, 2.0, jnp.where(grp == 1, 4.0, jnp.where(grp == 2, 8.0, 16.0)))
            cur = dm_ref[pl.ds(off, ch), :]
            hoff = pl.multiple_of(jnp.minimum(off + ch, s - HALO), 8)
            halo = jnp.where(r < n - 1, dm_ref[pl.ds(hoff, HALO), :], 0.0)
            x = jnp.concatenate([cur, halo], axis=0)
            t1 = (lax.broadcasted_iota(jnp.int32, (ch + HALO, LANES), 0) + off + 1).astype(F32)
            e = x / jnp.minimum(t1, win)
            tot = ch + HALO
            r2 = e + pltpu.roll(e, tot - 1, 0)
            r4 = r2 + pltpu.roll(r2, tot - 2, 0)
            r8 = r4 + pltpu.roll(r4, tot - 4, 0)
            r16 = r8 + pltpu.roll(r8, tot - 8, 0)
            g = grp[:ch]
            sel = jnp.where(g == 0, r2[:ch], jnp.where(g == 1, r4[:ch], jnp.where(g == 2, r8[:ch], r16[:ch])))
            dp_ref[pl.ds(off, ch), :] = (sel - cur).astype(BF16)
            return 0

        lax.fori_loop(0, n, chunk, 0)

    return pl.pallas_call(
        body, name=name, grid=(POOL // LANES,),
        in_specs=[pl.BlockSpec((s, LANES), lambda j: (0, j))],
        out_specs=pl.BlockSpec((s, LANES), lambda j: (0, j)),
        out_shape=jax.ShapeDtypeStruct((s, POOL), BF16),
        compiler_params=_cp(("parallel",), VMEM_LIMIT),
    )(dm)


def _head_mask(h):
    lane = lax.broadcasted_iota(jnp.int32, (CHUNK, SGU), 1)
    return (lane // (SGU // HEADS)) == h


def _tril(upper=False):
    row = lax.broadcasted_iota(jnp.int32, (CHUNK, CHUNK), 0)
    col = lax.broadcasted_iota(jnp.int32, (CHUNK, CHUNK), 1)
    return col >= row if upper else col <= row


def _sgu_gate(vn, wsp, bsp):
    out = []
    for cidx in range(vn.shape[0] // CHUNK):
        vc = vn[cidx * CHUNK:(cidx + 1) * CHUNK]
        zc = bsp
        for h in range(HEADS):
            zc = zc + jnp.where(_head_mask(h), _dot(wsp[h], vc), 0.0)
        out.append(zc)
    return jnp.concatenate(out, axis=0)


def _mix_out_fwd(o, z, m, x, wsp, bsp, wbd, psc, gsv, gout, wout, name):
    s = x.shape[0]
    tm = _tile(s, TOKENS)

    def body(o_ref, uv_ref, m_ref, x_ref, wsp_ref, bsp_ref, wbd_ref, psc_ref, gsv_ref, gout_ref, wout_ref,
             x1_ref, mix_ref):
        g = gout_ref[...]
        an = _rms(o_ref[...], HEADS * VH)[0] * g[:, :512]
        uv = uv_ref[...]
        u, v = uv[:, :SGU], uv[:, SGU:]
        vn = (_rms(v, SGU)[0] * gsv_ref[...]).astype(BF16)
        tri = _tril()
        wsp_m = [jnp.where(tri, wsp_ref[h], 0.0).astype(BF16) for h in range(HEADS)]
        gm = u * _sgu_gate(vn, wsp_m, bsp_ref[...])
        gn = _rms(gm, SGU)[0] * g[:, 512:768]
        po = _dot(m_ref[...].astype(BF16), wbd_ref[...]) * psc_ref[...]
        pn = _rms(po, POOL)[0] * g[:, 768:]
        mix = jnp.concatenate([an, gn, pn], axis=1).astype(BF16)
        mix_ref[...] = mix
        x1_ref[...] = x_ref[...] + _dot(mix, wout_ref[...])

    row = lambda w, j: pl.BlockSpec((tm, w), lambda i: (i, j))
    return pl.pallas_call(
        body, name=name, grid=(s // tm,),
        in_specs=[row(512, 0), row(512, 1), row(POOL, 0), row(D, 0),
                  _acc((HEADS, CHUNK, CHUNK)), _acc((CHUNK, SGU)), _acc((POOL, POOL)), _acc((1, POOL)),
                  _acc((1, SGU)), _acc((1, D)), _res((D, D))],
        out_specs=[row(D, 0), row(D, 0)],
        out_shape=[jax.ShapeDtypeStruct((s, D), F32), jax.ShapeDtypeStruct((s, D), BF16)],
        compiler_params=_cp(("parallel",), VMEM_LIMIT),
    )(o, z, m, x, wsp, bsp, wbd, psc, gsv, gout, wout)


def _ffn_fwd(x1, g, wg, wu, wd, tgt, name):
    s = x1.shape[0]
    tm = _tile(s, 256)
    last = tgt is not None

    def body(x_ref, g_ref, wg_ref, wu_ref, wd_ref, *rest):
        t_ref = rest[0] if last else None
        outs = rest[1:] if last else rest
        a_ref, b_ref, h_ref = outs[-3:]
        x = x_ref[...]
        h = (_rms(x, D)[0] * g_ref[...]).astype(BF16)
        h_ref[...] = h
        acc = jnp.zeros((tm, D), F32)
        for k in range(CHIPS):
            a = _dot_nt(h, wg_ref[k])
            b = _dot_nt(h, wu_ref[k])
            a_ref[k] = a
            b_ref[k] = b
            acc = acc + _dot((a * jax.nn.sigmoid(a) * b).astype(BF16), wd_ref[k])
        if not last:
            outs[0][...] = x + acc
            return
        dy_ref, l_ref = outs[:2]
        e = (x + acc) - t_ref[...]
        dy_ref[...] = e * (1.0 / D)
        sq = jnp.sum(e * e, axis=0, keepdims=True)
        part = sq[:, :LANES]
        for c in range(1, D // LANES):
            part = part + sq[:, c * LANES:(c + 1) * LANES]
        _accumulate(l_ref, part, pl.program_id(0) == 0)

    row = lambda w: pl.BlockSpec((tm, w), lambda i: (i, 0))
    hrow = pl.BlockSpec((CHIPS, tm, SH), lambda i: (0, i, 0))
    hshape = jax.ShapeDtypeStruct((CHIPS, s, SH), F32)
    tail_specs = [hrow, hrow, row(D)]
    tail_shapes = [hshape, hshape, jax.ShapeDtypeStruct((s, D), BF16)]
    head_specs = [row(D), _acc((1, LANES))] if last else [row(D)]
    head_shapes = [jax.ShapeDtypeStruct((s, D), F32)] + ([jax.ShapeDtypeStruct((1, LANES), F32)] if last else [])
    res = pl.pallas_call(
        body, name=name, grid=(s // tm,),
        in_specs=[row(D), _acc((1, D)), _res((CHIPS, SH, D)), _res((CHIPS, SH, D)), _res((CHIPS, SH, D))]
        + ([row(D)] if last else []),
        out_specs=head_specs + tail_specs, out_shape=head_shapes + tail_shapes,
        compiler_params=_cp(("arbitrary",), VMEM_LIMIT),
    )(x1, g, wg, wu, wd, *([tgt] if last else []))
    return (tuple(res[:2]) if last else res[0]), res[-3], res[-2], res[-1]


def _wgrad(a, b, name):
    s, k = a.shape
    n = b.shape[1]
    half = lambda v: v if v <= 1408 else v // 2
    kb, nb, tt = half(k), half(n), _tile(s, 2048)

    def body(a_ref, b_ref, o_ref):
        _accumulate(o_ref, _dot_tn(a_ref[...].astype(BF16), b_ref[...].astype(BF16)), pl.program_id(2) == 0)

    return pl.pallas_call(
        body, name=name, grid=(k // kb, n // nb, s // tt),
        in_specs=[pl.BlockSpec((tt, kb), lambda i, j, t: (t, i)), pl.BlockSpec((tt, nb), lambda i, j, t: (t, j))],
        out_specs=pl.BlockSpec((kb, nb), lambda i, j, t: (i, j)),
        out_shape=jax.ShapeDtypeStruct((k, n), F32),
        compiler_params=_cp(("parallel", "parallel", "arbitrary"), VMEM_LIMIT),
    )(a, b)


def _wgrad_in(h, dzm, duv, dp, name):
    s = h.shape[0]
    tt = _tile(s, 2048)

    def body(h_ref, a_ref, b_ref, c_ref, o_ref):
        hv = h_ref[...]
        val = jnp.concatenate([_dot_tn(hv, a_ref[...]), _dot_tn(hv, b_ref[...]), _dot_tn(hv, c_ref[...])], axis=1)
        _accumulate(o_ref, val, pl.program_id(0) == 0)

    row = lambda w: pl.BlockSpec((tt, w), lambda t: (t, 0))
    return pl.pallas_call(
        body, name=name, grid=(s // tt,), in_specs=[row(D), row(512), row(512), row(POOL)], out_specs=_acc((D, IN_P)),
        out_shape=jax.ShapeDtypeStruct((D, IN_P), F32), compiler_params=_cp(("arbitrary",), VMEM_LIMIT),
    )(h, dzm, duv, dp)


def _wgrad_rows(a, b, name):
    s, n = a.shape[1:]
    nn = b.shape[1]
    tt = _tile(s, 4096 if b.dtype == BF16 else 2048)

    def body(a_ref, b_ref, o_ref):
        _accumulate0(o_ref, _dot_tn(a_ref[0].astype(BF16), b_ref[...].astype(BF16)), pl.program_id(1) == 0)

    return pl.pallas_call(
        body, name=name, grid=(CHIPS, s // tt),
        in_specs=[pl.BlockSpec((1, tt, n), lambda c, t: (c, t, 0)), pl.BlockSpec((tt, nn), lambda c, t: (t, 0))],
        out_specs=pl.BlockSpec((1, n, nn), lambda c, t: (c, 0, 0)),
        out_shape=jax.ShapeDtypeStruct((CHIPS, n, nn), F32),
        compiler_params=_cp(("parallel", "arbitrary"), VMEM_LIMIT),
    )(a, b)


def _ffn_bwd(dx2, x1, a, b, g, wg, wu, wd, name):
    s = x1.shape[0]
    tm = _tile(s, 256)

    def body(dx2_ref, x_ref, a_ref, b_ref, g_ref, wg_ref, wu_ref, wd_ref,
             dx1_ref, hid_ref, da_ref, db_ref, dyb_ref, dg_ref):
        dx2 = dx2_ref[...]
        dyb = dx2.astype(BF16)
        dyb_ref[...] = dyb
        dh = jnp.zeros((tm, D), F32)
        ahead = _dot_nt(dyb, wd_ref[0])
        for k in range(CHIPS):
            av, bv = a_ref[k], b_ref[k]
            dhid = ahead
            if k + 1 < CHIPS:
                ahead = _dot_nt(dyb, wd_ref[k + 1])
            sig = jax.nn.sigmoid(av)
            sa = av * sig
            hid_ref[k] = (sa * bv).astype(BF16)
            dbv = (dhid * sa).astype(BF16)
            dav = (dhid * bv * (sig * (1.0 + av * (1.0 - sig)))).astype(BF16)
            db_ref[k] = dbv
            da_ref[k] = dav
            dh = dh + _dot(dav, wg_ref[k]) + _dot(dbv, wu_ref[k])
        xn, r = _rms(x_ref[...], D)
        dxr, dg = _rms_bwd(xn, r, g_ref[...], dh, D)
        dx1_ref[...] = dx2 + dxr
        _accumulate(dg_ref, dg, pl.program_id(0) == 0)

    row = lambda w: pl.BlockSpec((tm, w), lambda i: (i, 0))
    hrow = pl.BlockSpec((CHIPS, tm, SH), lambda i: (0, i, 0))
    hid = jax.ShapeDtypeStruct((CHIPS, s, SH), BF16)
    return pl.pallas_call(
        body, name=name, grid=(s // tm,),
        in_specs=[row(D), row(D), hrow, hrow, _acc((1, D)), _res((CHIPS, SH, D)), _res((CHIPS, SH, D)),
                  _res((CHIPS, SH, D))],
        out_specs=[row(D), hrow, hrow, hrow, row(D), _acc((1, D))],
        out_shape=[jax.ShapeDtypeStruct((s, D), F32), hid, hid, hid, jax.ShapeDtypeStruct((s, D), BF16),
                   jax.ShapeDtypeStruct((1, D), F32)],
        compiler_params=_cp(("arbitrary",), VMEM_LIMIT),
    )(dx2, x1, a, b, g, wg, wu, wd)


def _mix_out_bwd(dx1, o, z, m, wsp, bsp, wbd, psc, gsv, gout, wout, name):
    s = dx1.shape[0]
    tm = _tile(s, TOKENS)

    def body(dx1_ref, o_ref, uv_ref, m_ref, wsp_ref, bsp_ref, wbd_ref, psc_ref, gsv_ref, gout_ref, wout_ref,
             do_ref, dl_ref, duv_ref, dm_ref, dgo_ref, dgsv_ref, dpsc_ref, dwsp_ref, dbsp_ref, dwbd_ref):
        first = pl.program_id(0) == 0
        g = gout_ref[...]
        dmix = _dot_nt(dx1_ref[...].astype(BF16), wout_ref[...])
        o = o_ref[...]
        on, ro = _rms(o, HEADS * VH)
        do, dga = _rms_bwd(on, ro, g[:, :512], dmix[:, :512], HEADS * VH)
        for h in range(HEADS):
            sl = slice(h * VH, (h + 1) * VH)
            do_ref[h] = do[:, sl].astype(BF16)
            dl_ref[h] = jnp.broadcast_to(jnp.sum(do[:, sl] * o[:, sl], axis=-1, keepdims=True), (tm, LANES))
        uv = uv_ref[...]
        u, v = uv[:, :SGU], uv[:, SGU:]
        vx, rv = _rms(v, SGU)
        vn = (vx * gsv_ref[...]).astype(BF16)
        tri = _tril()
        wsp_m = [jnp.where(tri, wsp_ref[h], 0.0).astype(BF16) for h in range(HEADS)]
        zc = _sgu_gate(vn, wsp_m, bsp_ref[...])
        gm = u * zc
        gmn, rg = _rms(gm, SGU)
        dgm, dgg = _rms_bwd(gmn, rg, g[:, 512:768], dmix[:, 512:768], SGU)
        du = dgm * zc
        dzc = dgm * u
        dvn_parts = []
        dbsp = jnp.zeros((CHUNK, SGU), F32)
        dwsp = [jnp.zeros((CHUNK, CHUNK), F32) for _ in range(HEADS)]
        for cidx in range(tm // CHUNK):
            rs = slice(cidx * CHUNK, (cidx + 1) * CHUNK)
            dzc_c = dzc[rs]
            dbsp = dbsp + dzc_c
            dzb = dzc_c.astype(BF16)
            vc = vn[rs]
            dvn_c = jnp.zeros((CHUNK, SGU), F32)
            for h in range(HEADS):
                hm = _head_mask(h)
                dvn_c = dvn_c + jnp.where(hm, _dot_tn(wsp_m[h], dzb), 0.0)
                dwsp[h] = dwsp[h] + _dot_nt(jnp.where(hm, dzc_c, 0.0).astype(BF16), vc)
            dvn_parts.append(dvn_c)
        dvn = jnp.concatenate(dvn_parts, axis=0)
        dv, dgsv = _rms_bwd(vx, rv, gsv_ref[...], dvn, SGU)
        duv_ref[...] = jnp.concatenate([du, dv], axis=1).astype(BF16)
        mb = m_ref[...].astype(BF16)
        pw = _dot(mb, wbd_ref[...])
        po = pw * psc_ref[...]
        pon, rp = _rms(po, POOL)
        dpo, dgp = _rms_bwd(pon, rp, g[:, 768:], dmix[:, 768:], POOL)
        dpw = (dpo * psc_ref[...]).astype(BF16)
        dm_ref[...] = _dot_nt(dpw, wbd_ref[...])
        _accumulate(dgo_ref, jnp.concatenate([dga, dgg, dgp], axis=1), first)
        _accumulate(dgsv_ref, dgsv, first)
        _accumulate(dpsc_ref, jnp.sum(dpo * pw, axis=0, keepdims=True), first)
        _accumulate(dbsp_ref, dbsp, first)
        _accumulate(dwbd_ref, _dot_tn(mb, dpw), first)
        for h in range(HEADS):
            val = jnp.where(tri, dwsp[h], 0.0)

            @pl.when(first)
            def _(val=val, h=h):
                dwsp_ref[h] = val

            @pl.when(jnp.logical_not(first))
            def _(val=val, h=h):
                dwsp_ref[h] += val

    row = lambda w, j: pl.BlockSpec((tm, w), lambda i: (i, j))
    hspec = pl.BlockSpec((HEADS, tm, HP), lambda i: (0, i, 0))
    return pl.pallas_call(
        body, name=name, grid=(s // tm,),
        in_specs=[row(D, 0), row(512, 0), row(512, 1), row(POOL, 0),
                  _acc((HEADS, CHUNK, CHUNK)), _acc((CHUNK, SGU)),
                  _acc((POOL, POOL)), _acc((1, POOL)), _acc((1, SGU)), _acc((1, D)), _res((D, D))],
        out_specs=[hspec, hspec, row(512, 0), row(POOL, 0), _acc((1, D)), _acc((1, SGU)), _acc((1, POOL)),
                   _acc((HEADS, CHUNK, CHUNK)), _acc((CHUNK, SGU)), _acc((POOL, POOL))],
        out_shape=[jax.ShapeDtypeStruct((HEADS, s, HP), BF16), jax.ShapeDtypeStruct((HEADS, s, LANES), F32),
                   jax.ShapeDtypeStruct((s, 512), BF16), jax.ShapeDtypeStruct((s, POOL), F32),
                   jax.ShapeDtypeStruct((1, D), F32), jax.ShapeDtypeStruct((1, SGU), F32),
                   jax.ShapeDtypeStruct((1, POOL), F32), jax.ShapeDtypeStruct((HEADS, CHUNK, CHUNK), F32),
                   jax.ShapeDtypeStruct((CHUNK, SGU), F32), jax.ShapeDtypeStruct((POOL, POOL), F32)],
        compiler_params=_cp(("arbitrary",), VMEM_LIMIT),
    )(dx1, o, z, m, wsp, bsp, wbd, psc, gsv, gout, wout)


def _attn_bwd(q, k, v, do, lse, delta, after, name):
    s = q.shape[1]
    rh = _tile(s, ATT_ROWS)
    tk = _tile(s, ATT_KEYS)
    nk = s // tk
    wide = ATT_QUERIES if s % ATT_QUERIES == 0 else tk
    pieces = tk // rh

    def body(q_ref, k_ref, v_ref, do_ref, lse_ref, dl_ref, after_ref, dq_ref, dk_ref, dv_ref):
        del after_ref
        j = pl.program_id(1)

        @pl.when(j == 0)
        def _():
            dq_ref[...] = jnp.zeros_like(dq_ref)

        kj, vj = k_ref[0], v_ref[0]

        def blk(start, rows, dks, dvs, diagonal):
            dks, dvs = list(dks), list(dvs)
            offs = [pl.multiple_of(start + g * rh, rh) for g in range(rows // rh)]
            keys = [(g + 1) * rh if diagonal else tk for g in range(rows // rh)]
            qs = [q_ref[0, pl.ds(off, rh), :] for off in offs]
            dos = [do_ref[0, pl.ds(off, rh), :] for off in offs]
            scs = [_dot_nt(qi, kj[:n]) for qi, n in zip(qs, keys)]
            dps = [_dot_nt(doi, vj[:n]) for doi, n in zip(dos, keys)]
            for g, off in enumerate(offs):
                lse_i = lse_ref[0, pl.ds(off, rh), :][:, :1]
                dl_i = dl_ref[0, pl.ds(off, rh), :][:, :1]
                sc = _causal_mask(scs[g], g * rh) if diagonal else scs[g]
                p = jnp.exp2(sc - lse_i)
                ds = (p * (dps[g] - dl_i)).astype(BF16)
                cv = _dot_tn(p.astype(BF16), dos[g])
                ck = _dot_tn(ds, qs[g])
                for t in range(keys[g] // rh):
                    dvs[t] = dvs[t] + cv[t * rh:(t + 1) * rh]
                    dks[t] = dks[t] + ck[t * rh:(t + 1) * rh]
                dq_ref[0, pl.ds(off, rh), :] += _dot(ds, kj[:keys[g]]) * SCALE
            return tuple(dks), tuple(dvs)

        per = wide // tk
        zero = (jnp.zeros((rh, HP), F32),) * pieces
        acc = blk(j * tk, tk, zero, zero, True)
        first_wide = (j + per) // per
        acc = lax.fori_loop(j + 1, jnp.minimum(first_wide * per, nk), lambda i, c: blk(i * tk, tk, *c, False), acc)
        dks, dvs = lax.fori_loop(first_wide, nk // per, lambda i, c: blk(i * wide, wide, *c, False), acc)
        dk_ref[0] = jnp.concatenate(dks, axis=0) * (SCALE / EXP2_C)
        dv_ref[0] = jnp.concatenate(dvs, axis=0)

    full = lambda: pl.BlockSpec((1, s, HP), lambda h, j: (h, 0, 0))
    blk_spec = lambda: pl.BlockSpec((1, tk, HP), lambda h, j: (h, j, 0))
    out = jax.ShapeDtypeStruct((HEADS, s, HP), F32)
    return pl.pallas_call(
        body, name=name, grid=(HEADS, s // tk),
        in_specs=[full(), blk_spec(), blk_spec(), full(), full(), full(), ANY],
        out_specs=[full(), blk_spec(), blk_spec()], out_shape=[out] * 3,
        compiler_params=_cp(("parallel", "arbitrary"), VMEM_LIMIT),
    )(q, k, v, do, lse, delta, after)


def _mla_prep_bwd(dq, dk, dv, z, tabs, gql, gkv, gq, gk, wq, wk, wv, name):
    s = z.shape[0]
    tm = _tile(s, TOKENS)

    def body(dq_ref, dk_ref, dv_ref, ql_ref, kv_ref, kr_ref, c_ref, sa_ref, sb_ref, gql_ref, gkv_ref, gq_ref, gk_ref,
             wq_ref, wk_ref, wv_ref,
             dz_ref, qn_ref, kvn_ref, dqr_ref, dkr_ref, dvr_ref, dgql_ref, dgkv_ref, dgq_ref, dgk_ref):
        first = pl.program_id(0) == 0
        qx, rq = _rms(ql_ref[...], QL)
        qn = (qx * gql_ref[...]).astype(BF16)
        kx, rk = _rms(kv_ref[...], KVL)
        kvn = (kx * gkv_ref[...]).astype(BF16)
        qn_ref[...] = qn
        kvn_ref[...] = kvn
        qraw = _dot(qn, wq_ref[...])
        kraw = _dot(kvn, wk_ref[...])
        kr = kr_ref[...]
        c, sa, sb = c_ref[...], sa_ref[...], sb_ref[...]
        lane = lax.broadcasted_iota(jnp.int32, (tm, HP), 1)
        rope_lanes = (lane >= NOPE) & (lane < QK)
        dkrope = jnp.zeros((tm, HP), F32)
        dgq = jnp.zeros((1, HP), F32)
        dgk = jnp.zeros((1, HP), F32)
        for h in range(HEADS):
            sl = slice(h * HP, (h + 1) * HP)
            xn, r = _rms(qraw[:, sl], QK)
            dx, dg = _rms_bwd(xn, r, gq_ref[...], _rope_t(dq_ref[h], c, sa, sb), QK)
            dqr_ref[:, sl] = dx.astype(BF16)
            dgq = dgq + dg
            xn, r = _rms(kraw[:, sl] + kr, QK)
            dx, dg = _rms_bwd(xn, r, gk_ref[...], _rope_t(dk_ref[h], c, sa, sb), QK)
            dkr_ref[:, sl] = dx.astype(BF16)
            dgk = dgk + dg
            dkrope = dkrope + jnp.where(rope_lanes, dx, 0.0)
            dvr_ref[:, sl] = dv_ref[h].astype(BF16)
        dqn = _dot_nt(dqr_ref[...], wq_ref[...])
        dql, dgql = _rms_bwd(qx, rq, gql_ref[...], dqn, QL)
        dkvn = _dot_nt(dkr_ref[...], wk_ref[...]) + _dot_nt(dvr_ref[...], wv_ref[...])
        dkv, dgkv = _rms_bwd(kx, rk, gkv_ref[...], dkvn, KVL)
        dz_ref[...] = jnp.concatenate([dql, dkv, dkrope], axis=1).astype(BF16)
        _accumulate(dgql_ref, dgql, first)
        _accumulate(dgkv_ref, dgkv, first)
        _accumulate(dgq_ref, dgq, first)
        _accumulate(dgk_ref, dgk, first)

    row = lambda w, j: pl.BlockSpec((tm, w), lambda i: (i, j))
    hspec = pl.BlockSpec((HEADS, tm, HP), lambda i: (0, i, 0))
    sd = lambda w, dt: jax.ShapeDtypeStruct((s, w), dt)
    return pl.pallas_call(
        body, name=name, grid=(s // tm,),
        in_specs=[hspec, hspec, hspec, row(QL, 0), row(KVL, 2), row(HP, 3), row(HP, 0), row(HP, 0), row(HP, 0),
                  _acc((1, QL)), _acc((1, KVL)), _acc((1, HP)), _acc((1, HP)),
                  _acc((QL, HEADS * HP)), _acc((KVL, HEADS * HP)), _acc((KVL, HEADS * HP))],
        out_specs=[row(512, 0), row(QL, 0), row(KVL, 0), row(512, 0), row(512, 0), row(512, 0),
                   _acc((1, QL)), _acc((1, KVL)), _acc((1, HP)), _acc((1, HP))],
        out_shape=[sd(512, BF16), sd(QL, BF16), sd(KVL, BF16), sd(512, BF16), sd(512, BF16), sd(512, BF16),
                   jax.ShapeDtypeStruct((1, QL), F32), jax.ShapeDtypeStruct((1, KVL), F32),
                   jax.ShapeDtypeStruct((1, HP), F32), jax.ShapeDtypeStruct((1, HP), F32)],
        compiler_params=_cp(("arbitrary",), VMEM_LIMIT),
    )(dq, dk, dv, z, z, z, *tabs, gql, gkv, gq, gk, wq, wk, wv)


def _in_proj_bwd(dzm, duv, dp, x, dx1, g, win, name):
    s = x.shape[0]
    tm = _tile(s, TOKENS // 2)

    def body(dzm_ref, duv_ref, dp_ref, x_ref, dx1_ref, g_ref, w_ref, dx_ref, dg_ref):
        groups = [slice(r0, r0 + tm // 2) for r0 in (0, tm // 2)]
        dhs = [_dot_nt(dzm_ref[rs, :], w_ref[:, 0:512]) + _dot_nt(duv_ref[rs, :], w_ref[:, 512:1024])
               + _dot_nt(dp_ref[rs, :], w_ref[:, 1024:IN_P]) for rs in groups]
        dg = jnp.zeros((1, D), F32)
        for rs, dh in zip(groups, dhs):
            xn, r = _rms(x_ref[rs, :], D)
            dxr, dgr = _rms_bwd(xn, r, g_ref[...], dh, D)
            dx_ref[rs, :] = dx1_ref[rs, :] + dxr
            dg = dg + dgr
        _accumulate(dg_ref, dg, pl.program_id(0) == 0)

    row = lambda w: pl.BlockSpec((tm, w), lambda i: (i, 0))
    return pl.pallas_call(
        body, name=name, grid=(s // tm,),
        in_specs=[row(512), row(512), row(POOL), row(D), row(D), _acc((1, D)), _res((D, IN_P))],
        out_specs=[row(D), _acc((1, D))],
        out_shape=[jax.ShapeDtypeStruct((s, D), F32), jax.ShapeDtypeStruct((1, D), F32)],
        compiler_params=_cp(("arbitrary",), VMEM_LIMIT),
    )(dzm, duv, dp, x, dx1, g, win)


def _adamw(w, g0, g1, m, v, name):
    _, r, c = w.shape
    tr = _row_tile(r, 512)
    c1 = 1.0 - B1 ** STEP
    c2 = 1.0 - B2 ** STEP

    def body(w_ref, g0_ref, g1_ref, m_ref, v_ref, g_ref, d_ref, nm_ref, nv_ref):
        gv = jnp.where(pl.program_id(0) == 0, g0_ref[...], g1_ref[...])
        g_ref[0] = gv
        nm = B1 * m_ref[0] + (1.0 - B1) * gv
        nv = B2 * v_ref[0] + (1.0 - B2) * (gv * gv)
        nm_ref[0] = nm
        nv_ref[0] = nv
        d_ref[0] = -LR * ((nm / c1) / (jnp.sqrt(nv / c2) + ADAM_EPS) + WD * w_ref[0])

    spec = pl.BlockSpec((1, tr, c), lambda l, i: (l, i, 0))
    out = jax.ShapeDtypeStruct((DEPTH, r, c), F32)
    return pl.pallas_call(
        body, name=name, grid=(DEPTH, r // tr),
        in_specs=[spec, pl.BlockSpec((tr, c), lambda l, i: (i * (1 - l), 0)), pl.BlockSpec((tr, c), lambda l, i: (i * l, 0)),
                  spec, spec],
        out_specs=[spec] * 4, out_shape=[out] * 4, compiler_params=_cp(("parallel", "parallel")),
    )(w, g0, g1, m, v)


ANY = pl.BlockSpec(memory_space=pl.ANY)


def _place():
    x, y, c = lax.axis_index("x"), lax.axis_index("y"), lax.axis_index("c")
    chips = [(1 - x, y), (x, 1 - y), (1 - x, 1 - y)]
    return x, y, c, chips


def _half_rows(ref, lead, hh, half, align):
    rows = pl.ds(pl.multiple_of(hh * half, align), half)
    return ref.at[rows, :] if lead is None else ref.at[lead, rows, :]


def _row_align(dtype):
    return 16 if dtype == BF16 else 8


def _sems(n):
    return [pltpu.SemaphoreType.DMA((n,)), pltpu.SemaphoreType.DMA((n,)), pltpu.SemaphoreType.DMA((n,))]


def _comm_call(body, ins, out_shapes, nsems, name):
    return pl.pallas_call(
        body, name=name, in_specs=[ANY] * len(ins), out_specs=[ANY] * len(out_shapes), out_shape=out_shapes,
        scratch_shapes=_sems(nsems), compiler_params=pltpu.CompilerParams(has_side_effects=True),
    )(*ins)


def _all_gather_chips(shards, name):
    n = len(shards)
    halves = [a.shape[0] // 2 for a in shards]
    aligns = [_row_align(a.dtype) for a in shards]
    assert all(h % al == 0 for h, al in zip(halves, aligns))

    def body(*refs):
        ins, outs, (send_sems, recv_sems, _) = refs[:n], refs[n:2 * n], refs[2 * n:]
        x, y, c, chips = _place()
        me = 2 * x + y
        sibling = (x, y, 1 - c)

        def copy(sem, src, dst, to):
            return pltpu.make_async_remote_copy(src_ref=src, dst_ref=dst, send_sem=send_sems.at[sem],
                                                recv_sem=recv_sems.at[sem], device_id=to, device_id_type=MESH)

        first, passed = [], []
        for a in range(n):
            my_half = _half_rows(ins[a], None, c, halves[a], aligns[a])
            for j, (cx, cy) in enumerate(chips):
                cp = copy(6 * a + j, my_half, _half_rows(outs[a], me, c, halves[a], aligns[a]), (cx, cy, c))
                cp.start()
                first.append(cp)
        for a in range(n):
            for j, (cx, cy) in enumerate(chips):
                landed = _half_rows(outs[a], 2 * cx + cy, c, halves[a], aligns[a])
                copy(6 * a + j, landed, landed, (cx, cy, c)).wait_recv()
                fwd = copy(6 * a + 3 + j, landed, landed, sibling)
                fwd.start()
                passed.append(fwd)
        for a in range(n):
            for j, (cx, cy) in enumerate(chips):
                other = _half_rows(outs[a], 2 * cx + cy, 1 - c, halves[a], aligns[a])
                copy(6 * a + 3 + j, other, other, sibling).wait_recv()
        for cp in first + passed:
            cp.wait_send()

    lands = _comm_call(body, shards, [jax.ShapeDtypeStruct((CHIPS,) + a.shape, a.dtype) for a in shards], 6 * n, name)
    return _with_own(lands, shards)


def _with_own(lands, shards):
    me = 2 * lax.axis_index("x") + lax.axis_index("y")
    return [lax.dynamic_update_slice(g, a[None], (me, 0, 0)) for g, a in zip(lands, shards)]


def _pair_join(arrs, name):
    n = len(arrs)
    halves = [a.shape[0] // 2 for a in arrs]

    def body(*refs):
        outs, (send_sems, recv_sems, _) = refs[n:2 * n], refs[2 * n:]
        x, y, c, _ = _place()
        cps = []
        for a in range(n):
            mine = _half_rows(outs[a], None, c, halves[a], 8)
            cp = pltpu.make_async_remote_copy(src_ref=mine, dst_ref=mine, send_sem=send_sems.at[a], recv_sem=recv_sems.at[a],
                                              device_id=(x, y, 1 - c), device_id_type=MESH)
            cp.start()
            cps.append(cp)
        for cp in cps:
            cp.wait()

    return pl.pallas_call(
        body, name=name, in_specs=[ANY] * n, out_specs=[ANY] * n,
        out_shape=[jax.ShapeDtypeStruct(a.shape, a.dtype) for a in arrs],
        input_output_aliases={i: i for i in range(n)}, scratch_shapes=_sems(n),
        compiler_params=pltpu.CompilerParams(has_side_effects=True),
    )(*arrs)


HBM = pl.BlockSpec(memory_space=pltpu.HBM)
SEM = pl.BlockSpec(memory_space=pltpu.SEMAPHORE)
DATAFLOW = pltpu.SideEffectType.DATAFLOW_SIDE_EFFECTING


def _remote_copies(pairs, ins, lands, send_sems, recv_sems):
    return [pltpu.make_async_remote_copy(src_ref=src, dst_ref=dst, send_sem=send_sems.at[i], recv_sem=recv_sems.at[i],
                                         device_id=to, device_id_type=MESH)
            for i, (src, dst, to) in enumerate(pairs(ins, lands))]


def _split_start(srcs, land_shapes, ncopies, pairs, name, after):
    n, m = len(srcs), len(land_shapes)

    def body(*refs):
        ins, lands = refs[:n], refs[n:n + m]
        send_sems, recv_sems, token = refs[n + m + 1], refs[n + m + 2], refs[-1]
        for cp in _remote_copies(pairs, ins, lands, send_sems, recv_sems):
            cp.start()
        token[...] = jnp.zeros_like(token)

    hbm = lambda a: pltpu.with_memory_space_constraint(a, pltpu.HBM)
    lands = [hbm(lax.empty(s.shape, s.dtype)) for s in land_shapes]
    thru = [pltpu.HBM(a.shape, a.dtype) for a in list(srcs) + lands]
    out = pl.pallas_call(
        body, name=name,
        out_shape=(pltpu.SemaphoreType.DMA((ncopies,)), pltpu.SemaphoreType.DMA((ncopies,)), *thru,
                   jax.ShapeDtypeStruct((8, LANES), F32)),
        in_specs=[HBM] * (n + m) + [ANY], out_specs=(SEM, SEM, *[HBM] * (n + m), pl.BlockSpec(memory_space=pltpu.VMEM)),
        input_output_aliases={i: 2 + i for i in range(n + m)},
        compiler_params=pltpu.CompilerParams(has_side_effects=DATAFLOW),
    )(*[hbm(a) for a in srcs], *lands, after)
    return out[0], out[1], list(out[2:2 + n]), list(out[2 + n:2 + n + m]), out[-1]


def _split_wait(send_sems, recv_sems, srcs, lands, after, pairs, name):
    n, m = len(srcs), len(lands)

    def body(*refs):
        ins, lands_ = refs[:n], refs[n:n + m]
        for cp in _remote_copies(pairs, ins, lands_, refs[n + m], refs[n + m + 1]):
            cp.wait_send()
            cp.wait_recv()

    out = pl.pallas_call(
        body, name=name, out_shape=tuple(pltpu.HBM(a.shape, a.dtype) for a in list(srcs) + list(lands)),
        in_specs=[HBM] * (n + m) + [SEM, SEM, ANY], out_specs=tuple([HBM] * (n + m)),
        input_output_aliases={i: i for i in range(n + m)},
        compiler_params=pltpu.CompilerParams(has_side_effects=DATAFLOW),
    )(*srcs, *lands, send_sems, recv_sems, after)
    return list(out[:n]), list(out[n:])


def _gather_pairs(halves, aligns):
    def pairs(ins, lands):
        x, y, c, chips = _place()
        me = 2 * x + y
        return [(_half_rows(ins[a], None, c, halves[a], aligns[a]), _half_rows(lands[a], me, c, halves[a], aligns[a]),
                 (cx, cy, c)) for a in range(len(ins)) for cx, cy in chips]
    return pairs


PEERS = 7


def _scatter_pairs(ins, lands):
    x, y, c, chips = _place()
    to = [(cx, cy, c) for cx, cy in chips] + [(cx, cy, 1 - c) for cx, cy in chips] + [(x, y, 1 - c)]
    out = []
    for a in range(len(ins)):
        half = ins[a].shape[1] // 2
        for i, (tx, ty, tc) in enumerate(to):
            out.append((_half_rows(ins[a], 2 * tx + ty, tc, half, 8), lands[a].at[i], (tx, ty, tc)))
    return out


def _gather_finish(shards, lands, name):
    n = len(shards)
    halves = [a.shape[0] // 2 for a in shards]
    aligns = [_row_align(a.dtype) for a in shards]

    def body(*refs):
        outs, (send_sems, recv_sems, _) = refs[n:2 * n], refs[2 * n:]
        x, y, c, chips = _place()
        passed = []
        for a in range(n):
            for j, (cx, cy) in enumerate(chips):
                landed = _half_rows(outs[a], 2 * cx + cy, c, halves[a], aligns[a])
                cp = pltpu.make_async_remote_copy(src_ref=landed, dst_ref=landed, send_sem=send_sems.at[3 * a + j],
                                                  recv_sem=recv_sems.at[3 * a + j], device_id=(x, y, 1 - c),
                                                  device_id_type=MESH)
                cp.start()
                passed.append(cp)
        for a in range(n):
            for j, (cx, cy) in enumerate(chips):
                other = _half_rows(outs[a], 2 * cx + cy, 1 - c, halves[a], aligns[a])
                pltpu.make_async_remote_copy(src_ref=other, dst_ref=other, send_sem=send_sems.at[3 * a + j],
                                             recv_sem=recv_sems.at[3 * a + j], device_id=(x, y, 1 - c),
                                             device_id_type=MESH).wait_recv()
        for cp in passed:
            cp.wait_send()

    lands = pl.pallas_call(
        body, name=name, in_specs=[ANY] * n, out_specs=[ANY] * n,
        out_shape=[jax.ShapeDtypeStruct(a.shape, a.dtype) for a in lands],
        input_output_aliases={i: i for i in range(n)}, scratch_shapes=_sems(3 * n),
        compiler_params=pltpu.CompilerParams(has_side_effects=True),
    )(*lands)
    return _with_own(lands, shards)


def _sum_own_and_landed(own, landed, where, name):
    _, half, cols = landed.shape
    tr = _row_tile(half, 128)
    nt = half // tr

    grid_spec = pltpu.PrefetchScalarGridSpec(
        num_scalar_prefetch=1, grid=(nt,),
        in_specs=[pl.BlockSpec((1, tr, cols), lambda r, w: (w[0], w[1] * nt + r, 0)),
                  pl.BlockSpec((PEERS, tr, cols), lambda r, w: (0, r, 0))],
        out_specs=pl.BlockSpec((tr, cols), lambda r, w: (w[1] * nt + r, 0)))

    def body(w_ref, p_ref, q_ref, o_ref):
        acc = p_ref[0]
        for i in range(PEERS):
            acc = acc + q_ref[i]
        o_ref[...] = acc

    return pl.pallas_call(
        body, name=name, grid_spec=grid_spec, out_shape=jax.ShapeDtypeStruct((2 * half, cols), own.dtype),
        compiler_params=_cp(("parallel",)),
    )(where, own, landed)


BIG = [("w_in", (D, IN_W), 1), ("w_q_up", (QL, HEADS * QK), 1), ("w_kv_up", (KVL, HEADS * (NOPE + VH)), 1),
       ("w_out", (D, D), 0), ("w_gate", (D, HID), 1), ("w_up", (D, HID), 1), ("w_down", (HID, D), 0)]
SMALL = [("g_mix_norm", (D,)), ("g_q_lat", (QL,)), ("g_kv_lat", (KVL,)), ("g_q_head", (QK,)), ("g_k_head", (QK,)),
         ("g_sgu_v", (SGU,)), ("w_spatial", (HEADS, CHUNK, CHUNK)), ("b_spatial", (HEADS, CHUNK)),
         ("w_pool", (4, 64, 64)), ("pool_scale", (POOL,)), ("g_out_mla", (512,)), ("g_out_sgu", (SGU,)),
         ("g_out_pool", (POOL,)), ("g_ffn_norm", (D,))]
ORDER = ["g_mix_norm", "w_in", "g_q_lat", "w_q_up", "g_kv_lat", "w_kv_up", "g_q_head", "g_k_head", "g_sgu_v",
         "w_spatial", "b_spatial", "w_pool", "pool_scale", "g_out_mla", "g_out_sgu", "g_out_pool", "w_out",
         "g_ffn_norm", "w_gate", "w_up", "w_down"]
EARLY_BIG = ["w_in", "w_q_up", "w_kv_up"]
FFN_BIG = ["w_gate", "w_up", "w_down"]
LATE_BIG = ["w_out"] + FFN_BIG
DEPTH = 2
COLS = 1024
SMALL_N = sum(math.prod(s) for _, s in SMALL) * DEPTH
assert SMALL_N % CHIPS == 0
SMALL_ROWS = -(-(SMALL_N // CHIPS + 1) // (16 * COLS)) * 16


def _unsplit_cols(g):
    return g.transpose(1, 0, 2).reshape(g.shape[1], CHIPS * g.shape[2])


def _split_cols(full):
    r, c = full.shape
    return full.reshape(r, CHIPS, c // CHIPS).transpose(1, 0, 2)


def _kernel_weights(g):
    win = _unsplit_cols(g["w_in"])
    zeros = lambda r, c: jnp.zeros((r, c), BF16)
    o2, o3, o4 = QL + KVL, QL + KVL + ROPE, QL + KVL + ROPE + 2 * SGU
    win_p = jnp.concatenate([win[:, :o2], zeros(D, NOPE), win[:, o2:o3], zeros(D, HP - QK), win[:, o3:o4], win[:, o4:]], axis=1)
    wq = _unsplit_cols(g["w_q_up"]).reshape(QL, HEADS, QK)
    wq_p = jnp.pad(wq, ((0, 0), (0, 0), (0, HP - QK))).reshape(QL, HEADS * HP)
    wkv = _unsplit_cols(g["w_kv_up"]).reshape(KVL, HEADS, NOPE + VH)
    wk_p = jnp.pad(wkv[:, :, :NOPE], ((0, 0), (0, 0), (0, HP - NOPE))).reshape(KVL, HEADS * HP)
    wv_p = wkv[:, :, NOPE:].reshape(KVL, HEADS * VH)
    return dict(win=win_p, wq=wq_p, wk=wk_p, wv=wv_p)


def _small_operands(p, l):
    row = lambda v: v.reshape(1, -1)
    pad = lambda v: jnp.pad(v, (0, HP - QK)).reshape(1, HP)
    wpool = p["w_pool"][l]
    wbd = jnp.zeros((POOL, POOL), F32)
    for g in range(4):
        wbd = lax.dynamic_update_slice(wbd, wpool[g], (g * 64, g * 64))
    return dict(
        g_mix=row(p["g_mix_norm"][l]), gql=row(p["g_q_lat"][l]), gkv=row(p["g_kv_lat"][l]),
        gq=pad(p["g_q_head"][l]), gk=pad(p["g_k_head"][l]), gsv=row(p["g_sgu_v"][l]),
        wsp=p["w_spatial"][l], bsp=jnp.repeat(p["b_spatial"][l].T, SGU // HEADS, axis=1),
        wbd=wbd.astype(BF16), psc=row(p["pool_scale"][l]),
        gout=jnp.concatenate([p["g_out_mla"][l], p["g_out_sgu"][l], p["g_out_pool"][l]]).reshape(1, D),
        g_ffn=row(p["g_ffn_norm"][l]))


def _big_grads(g):
    dwin = g["win"]
    o2 = QL + KVL
    gin = jnp.concatenate([dwin[:, :o2], dwin[:, o2 + NOPE:o2 + NOPE + ROPE], dwin[:, 512:]], axis=1)
    gq = g["wq"].reshape(QL, HEADS, HP)[:, :, :QK].reshape(QL, HEADS * QK)
    gk = g["wk"].reshape(KVL, HEADS, HP)[:, :, :NOPE]
    gv = g["wv"].reshape(KVL, HEADS, VH)
    gkv = jnp.concatenate([gk, gv], axis=2).reshape(KVL, HEADS * (NOPE + VH))
    return {"w_in": _split_cols(gin), "w_q_up": _split_cols(gq), "w_kv_up": _split_cols(gkv),
            "w_out": g["wout"].reshape(CHIPS, D // CHIPS, D), "w_gate": g["wg"], "w_up": g["wu"], "w_down": g["wd"]}


TRANSPOSED = ("w_gate", "w_up")


def _small_grads(g):
    go = g["gout"].reshape(-1)
    return {"g_mix_norm": g["g_mix"].reshape(-1), "g_q_lat": g["gql"].reshape(-1), "g_kv_lat": g["gkv"].reshape(-1),
            "g_q_head": g["gq"].reshape(-1)[:QK], "g_k_head": g["gk"].reshape(-1)[:QK], "g_sgu_v": g["gsv"].reshape(-1),
            "w_spatial": g["wsp"], "b_spatial": g["bsp"].reshape(CHUNK, HEADS, SGU // HEADS).sum(-1).T,
            "w_pool": jnp.stack([g["wbd"][i * 64:(i + 1) * 64, i * 64:(i + 1) * 64] for i in range(4)]),
            "pool_scale": g["psc"].reshape(-1), "g_out_mla": go[:512], "g_out_sgu": go[512:768],
            "g_out_pool": go[768:], "g_ffn_norm": g["g_ffn"].reshape(-1)}


def _pack_small_grads(small, loss):
    sm = jnp.concatenate([small[l][n].reshape(-1) for l in range(DEPTH) for n, _ in SMALL]).reshape(CHIPS, SMALL_N // CHIPS)
    sm = jnp.pad(sm, ((0, 0), (0, SMALL_ROWS * COLS - SMALL_N // CHIPS)))
    return sm.at[0, SMALL_N // CHIPS].set(loss).reshape(CHIPS, SMALL_ROWS, COLS)


def _unpack_small_grads(gathered):
    rows = gathered.reshape(CHIPS, SMALL_ROWS * COLS)
    loss = rows[0, SMALL_N // CHIPS]
    flat = rows[:, :SMALL_N // CHIPS].reshape(-1)
    out, off = [], 0
    for _ in range(DEPTH):
        layer = {}
        for n, shape in SMALL:
            k = math.prod(shape)
            layer[n] = flat[off:off + k].reshape(shape)
            off += k
        out.append(layer)
    return out, loss


def _layer_fwd(x, tabs, kw, late_weights, sp, l, tgt):
    t = f"_l{l}"
    z, hb = _in_proj_fwd(x, sp["g_mix"], kw["win"], "in_proj_fwd" + t)
    q, k, v = _mla_prep_fwd(z, tabs, sp["gql"], sp["gkv"], sp["gq"], sp["gk"], kw["wq"], kw["wk"], kw["wv"],
                            "mla_prep_fwd" + t)
    o, lse = _attn_fwd(q, k, v, "attn_fwd" + t)
    m = _pool_win_fwd(z, "pool_win_fwd" + t)
    wout, wg, wu, wd = late_weights(o)
    wout = wout.reshape(D, D)
    x1, mix = _mix_out_fwd(o, z, m, x, sp["wsp"], sp["bsp"], sp["wbd"], sp["psc"], sp["gsv"], sp["gout"], wout,
                           "mix_out_fwd" + t)
    x2, a, b, h2 = _ffn_fwd(x1, sp["g_ffn"], wg, wu, wd, tgt, "ffn_fwd" + t)
    saved = dict(x=x, z=z, hb=hb, q=q, k=k, v=v, o=o, lse=lse, m=m, x1=x1, mix=mix, a=a, b=b, h2=h2, wg=wg, wu=wu, wd=wd,
                 wout=wout)
    return x2, saved


def _layer_bwd(dx2, sv, tabs, kw, sp, l, ffn_hook, out_hook):
    t = f"_l{l}"
    g = {}
    dx1, hid, da, db, dyb, g["g_ffn"] = _ffn_bwd(dx2, sv["x1"], sv["a"], sv["b"], sp["g_ffn"], sv["wg"], sv["wu"],
                                                 sv["wd"], "ffn_bwd" + t)
    g["wd"] = _wgrad_rows(hid, dyb, "wgrad_down" + t)
    g["wg"] = _wgrad_rows(da, sv["h2"], "wgrad_gate" + t)
    g["wu"] = _wgrad_rows(db, sv["h2"], "wgrad_up" + t)
    gout = sp["gout"] + ffn_hook(g)
    do, delta, duv, dm, g["gout"], g["gsv"], g["psc"], g["wsp"], g["bsp"], g["wbd"] = _mix_out_bwd(
        dx1, sv["o"], sv["z"], sv["m"], sp["wsp"], sp["bsp"], sp["wbd"], sp["psc"], sp["gsv"], gout, sv["wout"],
        "mix_out_bwd" + t)
    g["wout"] = _wgrad(sv["mix"], dx1, "wgrad_out" + t)
    dp = _pool_win_bwd(dm, "pool_win_bwd" + t)
    dq, dk, dv = _attn_bwd(sv["q"], sv["k"], sv["v"], do, sv["lse"], delta, out_hook(g), "attn_bwd" + t)
    dzm, qn, kvn, dqr, dkr, dvr, g["gql"], g["gkv"], g["gq"], g["gk"] = _mla_prep_bwd(
        dq, dk, dv, sv["z"], tabs, sp["gql"], sp["gkv"], sp["gq"], sp["gk"], kw["wq"], kw["wk"], kw["wv"],
        "mla_prep_bwd" + t)
    g["wq"] = _wgrad(qn, dqr, "wgrad_q_up" + t)
    g["wk"] = _wgrad(kvn, dkr, "wgrad_k_up" + t)
    g["wv"] = _wgrad(kvn, dvr, "wgrad_v_up" + t)
    dx, g["g_mix"] = _in_proj_bwd(dzm, duv, dp, sv["x"], dx1, sp["g_mix"], kw["win"], "in_proj_bwd" + t)
    g["win"] = _wgrad_in(sv["hb"], dzm, duv, dp, "wgrad_in" + t)
    return dx, g


def _rope_inv_freq():
    half = ROPE // 2
    inv = 1.0 / (ROPE_THETA ** (jnp.arange(half, dtype=F32) / half))
    return jnp.concatenate([jnp.zeros((NOPE,), F32), inv, inv, jnp.zeros((HP - QK,), F32)]).reshape(1, HP)


def kernel(x, positions, g_mix_norm, w_in, g_q_lat, w_q_up, g_kv_lat, w_kv_up, g_q_head, g_k_head, g_sgu_v, w_spatial, b_spatial, w_pool, pool_scale, g_out_mla, g_out_sgu, g_out_pool, w_out, g_ffn_norm, w_gate, w_up, w_down, loss_target, m_g_mix_norm, m_w_in, m_g_q_lat, m_w_q_up, m_g_kv_lat, m_w_kv_up, m_g_q_head, m_g_k_head, m_g_sgu_v, m_w_spatial, m_b_spatial, m_w_pool, m_pool_scale, m_g_out_mla, m_g_out_sgu, m_g_out_pool, m_w_out, m_g_ffn_norm, m_w_gate, m_w_up, m_w_down, v_g_mix_norm, v_w_in, v_g_q_lat, v_w_q_up, v_g_kv_lat, v_w_kv_up, v_g_q_head, v_g_k_head, v_g_sgu_v, v_w_spatial, v_b_spatial, v_w_pool, v_pool_scale, v_g_out_mla, v_g_out_sgu, v_g_out_pool, v_w_out, v_g_ffn_norm, v_w_gate, v_w_up, v_w_down):
    given = dict(locals())
    p = {n: given[n] for n in ORDER}
    view = lambda pre, n: jnp.swapaxes(given[pre + n], 1, 2) if n in TRANSPOSED else given[pre + n]
    seq = x.shape[1]
    where = jnp.stack([2 * lax.axis_index("x") + lax.axis_index("y"), lax.axis_index("c")]).astype(jnp.int32)
    shards = lambda names: [view("", n)[l].astype(BF16) for l, n in names]
    zero11 = lambda token: token[:1, :1]

    names_0a = [(0, n) for n in EARLY_BIG]
    names_0b = [(0, n) for n in LATE_BIG]
    names_1 = [(1, n) for n, _, _ in BIG]
    got_0a = dict(zip(EARLY_BIG, _all_gather_chips(shards(names_0a), "all_gather_w0a")))
    started, issued = {}, got_0a["w_in"]
    for tag, names in (("w0b", names_0b), ("w1", names_1)):
        sh = shards(names)
        pairs = _gather_pairs([a.shape[0] // 2 for a in sh], [_row_align(a.dtype) for a in sh])
        lands = [jax.ShapeDtypeStruct((CHIPS,) + a.shape, a.dtype) for a in sh]
        started[tag] = (sh, pairs) + _split_start(sh, lands, 3 * len(sh), pairs, "gather_start_" + tag, issued)
        issued = started[tag][6]

    def arrived(tag, after):
        _, pairs, send, recv, srcs, lands, _ = started[tag]
        srcs, lands = _split_wait(send, recv, srcs, lands, after, pairs, "gather_wait_" + tag)
        return _gather_finish(srcs, lands, "gather_finish_" + tag)

    layer1 = {}

    def mix_weights(l, h):
        if l == 0:
            return got_0a
        layer1.update(zip([n for _, n in names_1], arrived("w1", h)))
        return layer1

    def late_weights(l, o):
        return arrived("w0b", o) if l == 0 else [layer1[n] for n in LATE_BIG]

    reducing, last = {}, {}

    def reduce_start(tag, arrs):
        lands = [jax.ShapeDtypeStruct((PEERS, a.shape[1] // 2, a.shape[2]), a.dtype) for a in arrs]
        reducing[tag] = _split_start(arrs, lands, PEERS * len(arrs), _scatter_pairs, "grad_scatter_start_" + tag, where)
        return zero11(reducing[tag][4])

    def reduce_finish(tag, after):
        send, recv, srcs, lands, _ = reducing[tag]
        srcs, lands = _split_wait(send, recv, srcs, lands, after, _scatter_pairs, "grad_scatter_wait_" + tag)
        return [_sum_own_and_landed(a, q, where, f"grad_sum_{tag}_{i}") for i, (a, q) in enumerate(zip(srcs, lands))]

    def ffn_hook(l, g):
        if l == 1:
            return jnp.zeros((1, 1), F32)
        return reduce_start("g0b", [g["wg"], g["wu"], g["wd"]])

    def out_hook(l, g):
        if l == 1:
            return where
        reduce_start("g0c", [g["wout"].reshape(CHIPS, D // CHIPS, D)])
        return reducing["g0c"][4]

    def layer_hook(l, big, small):
        last[l] = (big, small)
        if l == 1:
            return reduce_start("g1", [big[n] for n, _, _ in BIG])
        return None

    entry = zero11(started["w0b"][6]) + zero11(started["w1"][6])
    loss_part, dx = _step(x.reshape(seq, D), positions.reshape(seq, 1), loss_target.reshape(seq, D), p, entry,
                          mix_weights, late_weights, ffn_hook, out_hook, layer_hook)

    def adamw(n, g0, g1):
        flip = n in EARLY_BIG
        pick = lambda pre: jnp.swapaxes(given[pre + n], 1, 2) if flip else view(pre, n)
        w = pick("")
        three_d = (DEPTH, -1, w.shape[-1])
        g0, g1 = (g.T if flip else g for g in (g0, g1))
        res = _adamw(w.reshape(three_d), g0.reshape(three_d[1:]), g1.reshape(three_d[1:]),
                     pick("m_").reshape(three_d), pick("v_").reshape(three_d), "adamw_" + n)
        return [jnp.swapaxes(r.reshape(w.shape), 1, 2) if flip else r.reshape(w.shape) for r in res]

    names_rest = [(0, n) for n in EARLY_BIG]
    reduce_start("g0a", [last[0][0][n] for _, n in names_rest]
                 + [_pack_small_grads([last[l][1] for l in range(DEPTH)], loss_part)])
    token = reducing["g0a"][4]
    early = names_1 + [(0, n) for n in FFN_BIG] + [(0, "w_out")]
    landed = reduce_finish("g1", token) + reduce_finish("g0b", token) + reduce_finish("g0c", token)
    sums = dict(zip(early, _pair_join(landed, "grad_pair_join_early")))
    out = {n: adamw(n, sums[(0, n)], sums[(1, n)]) for n in FFN_BIG}
    late = names_rest + ["small"]
    sums.update(zip(late, _pair_join(reduce_finish("g0a", out["w_down"][1]), "grad_pair_join_late")))
    gsmall, loss = _unpack_small_grads(_all_gather_chips([sums["small"]], "all_gather_small_grads")[0])
    for n in ORDER:
        if n not in out:
            g = [sums[(l, n)] for l in range(DEPTH)] if (0, n) in sums else [gsmall[l][n] for l in range(DEPTH)]
            out[n] = adamw(n, *g)
    undo = lambda n, a: jnp.swapaxes(a, 1, 2) if n in TRANSPOSED else a
    return (loss, dx.reshape(x.shape), *[undo(n, out[n][i]) for i in range(4) for n in ORDER])


def _step(xs, pos, tgt, p, entry, mix_weights, late_weights, ffn_hook, out_hook, layer_hook):
    sps = [_small_operands(p, l) for l in range(DEPTH)]
    sps[0]["g_mix"] = sps[0]["g_mix"] + entry
    tabs = _rope_tables(pos, _rope_inv_freq())
    saved, h = [], xs
    for l in range(DEPTH):
        kw = _kernel_weights(mix_weights(l, h))
        h, sv = _layer_fwd(h, tabs, kw, functools.partial(late_weights, l), sps[l], l, tgt if l == DEPTH - 1 else None)
        saved.append(dict(sv, kw=kw))
    dy, lpart = h
    for l in reversed(range(DEPTH)):
        dy, g = _layer_bwd(dy, saved[l], tabs, saved[l]["kw"], sps[l], l, functools.partial(ffn_hook, l),
                           functools.partial(out_hook, l))
        zero = layer_hook(l, _big_grads(g), _small_grads(g))
        if zero is not None and l > 0:
            sps[l - 1]["g_ffn"] = sps[l - 1]["g_ffn"] + zero
    return 0.5 / D * jnp.sum(lpart), dy
```

```python
import functools
import math

import jax
import jax.numpy as jnp
from jax import lax
from jax.experimental import pallas as pl
from jax.experimental.pallas import tpu as pltpu

F32 = jnp.float32
BF16 = jnp.bfloat16
MESH = pl.DeviceIdType.MESH

D = 1024
HEADS = 4
QK = 96
NOPE = 64
ROPE = 32
VH = 128
HP = 128
QL = 256
KVL = 128
SGU = 256
POOL = 256
CHUNK = 128
HID = 2816
CHIPS = 4
SH = HID // CHIPS
IN_W = 1184
IN_P = 1280
EPS = 1e-6
ROPE_THETA = 10000.0
SCALE = 1.0 / math.sqrt(QK)
LOG2E = 1.4426950408889634
EXP2_C = SCALE * LOG2E
ATT_WIDE = 1
ATT_FWD_QUERIES = 2048
ATT_PIECE = 2048
ATT_ROWS = 256
ATT_KEYS = 1024
ATT_QUERIES = 2048
NEG = -1e30
HALO = 16

LR, B1, B2, ADAM_EPS, WD, STEP = 0.001, 0.9, 0.999, 1e-08, 0.01, 10

VMEM_LIMIT = 56 * 1024 * 1024
LANES = 128
TOKENS = 1024


def _cp(sem, vmem=None):
    return pltpu.CompilerParams(dimension_semantics=sem, vmem_limit_bytes=vmem)


def _res(shape):
    nd = len(shape)
    return pl.BlockSpec(shape, lambda *_: (0,) * nd, pipeline_mode=pl.Buffered(1))


def _acc(shape):
    nd = len(shape)
    return pl.BlockSpec(shape, lambda *_: (0,) * nd)


def _dot(a, b):
    return jnp.dot(a, b, preferred_element_type=F32)


def _dot_nt(a, b):
    return lax.dot_general(a, b, (((1,), (1,)), ((), ())), preferred_element_type=F32)


def _dot_tn(a, b):
    return lax.dot_general(a, b, (((0,), (0,)), ((), ())), preferred_element_type=F32)


def _rms(x, n):
    r = lax.rsqrt(jnp.sum(x * x, axis=-1, keepdims=True) * (1.0 / n) + EPS)
    return x * r, r


def _rms_bwd(xn, r, g, dy, n):
    dn = dy * g
    dx = r * (dn - xn * (jnp.sum(dn * xn, axis=-1, keepdims=True) * (1.0 / n)))
    return dx, jnp.sum(dy * xn, axis=0, keepdims=True)


def _accumulate(ref, val, first):
    @pl.when(first)
    def _():
        ref[...] = val

    @pl.when(jnp.logical_not(first))
    def _():
        ref[...] += val


def _accumulate0(ref, val, first):
    @pl.when(first)
    def _():
        ref[0] = val

    @pl.when(jnp.logical_not(first))
    def _():
        ref[0] += val


def _tile(s, t):
    return min(s, t)


def _row_tile(r, cap):
    if r <= cap:
        return r
    return max(t for t in range(8, cap + 1, 8) if r % t == 0)


def _rope_tables(pos, invf):
    s = pos.shape[0]
    tm = _tile(s, 1024)

    def body(pos_ref, invf_ref, c_ref, sa_ref, sb_ref):
        ang = pos_ref[...].astype(F32) * invf_ref[...]
        c, sn = jnp.cos(ang), jnp.sin(ang)
        lane = lax.broadcasted_iota(jnp.int32, ang.shape, 1)
        first = (lane >= NOPE) & (lane < NOPE + ROPE // 2)
        second = (lane >= NOPE + ROPE // 2) & (lane < QK)
        c_ref[...] = jnp.where(first | second, c, 1.0)
        sa_ref[...] = jnp.where(first, -sn, 0.0)
        sb_ref[...] = jnp.where(second, sn, 0.0)

    out = jax.ShapeDtypeStruct((s, HP), F32)
    return pl.pallas_call(
        body, name="rope_tables", grid=(s // tm,),
        in_specs=[pl.BlockSpec((tm, 1), lambda i: (i, 0)), _acc((1, HP))],
        out_specs=[pl.BlockSpec((tm, HP), lambda i: (i, 0))] * 3,
        out_shape=[out] * 3, compiler_params=_cp(("parallel",)),
    )(pos, invf)


def _rope(x, c, sa, sb):
    return x * c + pltpu.roll(x, HP - ROPE // 2, 1) * sa + pltpu.roll(x, ROPE // 2, 1) * sb


def _rope_t(d, c, sa, sb):
    return d * c + pltpu.roll(d * sa, ROPE // 2, 1) + pltpu.roll(d * sb, HP - ROPE // 2, 1)


def _in_proj_fwd(x, g, w, name):
    s = x.shape[0]
    tm = _tile(s, TOKENS)

    def body(x_ref, g_ref, w_ref, z_ref, h_ref):
        xn, _ = _rms(x_ref[...], D)
        h = (xn * g_ref[...]).astype(BF16)
        h_ref[...] = h
        z_ref[...] = _dot(h, w_ref[...])

    return pl.pallas_call(
        body, name=name, grid=(s // tm,),
        in_specs=[pl.BlockSpec((tm, D), lambda i: (i, 0)), _acc((1, D)), _res((D, IN_P))],
        out_specs=[pl.BlockSpec((tm, IN_P), lambda i: (i, 0)), pl.BlockSpec((tm, D), lambda i: (i, 0))],
        out_shape=[jax.ShapeDtypeStruct((s, IN_P), F32), jax.ShapeDtypeStruct((s, D), BF16)],
        compiler_params=_cp(("parallel",), VMEM_LIMIT),
    )(x, g, w)


def _mla_prep_fwd(z, tabs, gql, gkv, gq, gk, wq, wk, wv, name):
    s = z.shape[0]
    tm = _tile(s, TOKENS)

    def body(ql_ref, kv_ref, kr_ref, c_ref, sa_ref, sb_ref, gql_ref, gkv_ref, gq_ref, gk_ref,
             wq_ref, wk_ref, wv_ref, q_out, k_out, v_out):
        qn = (_rms(ql_ref[...], QL)[0] * gql_ref[...]).astype(BF16)
        kvn = (_rms(kv_ref[...], KVL)[0] * gkv_ref[...]).astype(BF16)
        qraw = _dot(qn, wq_ref[...])
        kraw = _dot(kvn, wk_ref[...])
        vraw = _dot(kvn, wv_ref[...])
        kr = kr_ref[...]
        c, sa, sb = c_ref[...], sa_ref[...], sb_ref[...]
        for h in range(HEADS):
            sl = slice(h * HP, (h + 1) * HP)
            xq = _rms(qraw[:, sl], QK)[0] * gq_ref[...]
            q_out[h] = (_rope(xq, c, sa, sb) * EXP2_C).astype(BF16)
            xk = _rms(kraw[:, sl] + kr, QK)[0] * gk_ref[...]
            k_out[h] = _rope(xk, c, sa, sb).astype(BF16)
            v_out[h] = vraw[:, sl].astype(BF16)

    row = lambda w, j: pl.BlockSpec((tm, w), lambda i: (i, j))
    hspec = pl.BlockSpec((HEADS, tm, HP), lambda i: (0, i, 0))
    hshape = jax.ShapeDtypeStruct((HEADS, s, HP), BF16)
    return pl.pallas_call(
        body, name=name, grid=(s // tm,),
        in_specs=[row(QL, 0), row(KVL, 2), row(HP, 3), row(HP, 0), row(HP, 0), row(HP, 0),
                  _acc((1, QL)), _acc((1, KVL)), _acc((1, HP)), _acc((1, HP)),
                  _acc((QL, HEADS * HP)), _acc((KVL, HEADS * HP)), _acc((KVL, HEADS * HP))],
        out_specs=[hspec] * 3, out_shape=[hshape] * 3,
        compiler_params=_cp(("parallel",)),
    )(z, z, z, *tabs, gql, gkv, gq, gk, wq, wk, wv)


def _causal_mask(s, row0):
    row = lax.broadcasted_iota(jnp.int32, s.shape, 0) + row0
    col = lax.broadcasted_iota(jnp.int32, s.shape, 1)
    return jnp.where(col <= row, s, NEG)


def _attn_fwd(q, k, v, name):
    s = q.shape[1]
    tq = _tile(s, ATT_FWD_QUERIES)
    rh = _tile(s, ATT_ROWS)
    kp = _tile(s, ATT_PIECE)
    wide = ATT_WIDE * kp if s % (ATT_WIDE * kp) == 0 else tq
    groups = tq // rh

    def body(q_ref, k_ref, v_ref, o_ref, lse_ref):
        i = pl.program_id(1)

        def blk(off, tk, carry, diagonal):
            width = lambda g, t: max(0, min(kp, (g + 1) * rh - t * kp)) if diagonal else kp
            rows = lambda t: pl.ds(pl.multiple_of(off + t * kp, kp), kp)
            score = lambda g, t: _dot_nt(q_ref[0, g * rh:(g + 1) * rh, :], k_ref[0, rows(t), :][:width(g, t)])
            live = lambda t: [g for g in range(groups) if width(g, t) > 0]
            state = list(carry)
            scs = {(g, 0): score(g, 0) for g in live(0)}
            for t in range(tk // kp):
                if (t + 1) * kp < tk:
                    scs.update({(g, t + 1): score(g, t + 1) for g in live(t + 1)})
                vt = v_ref[0, rows(t), :]
                for g in live(t):
                    m, l, acc = state[g]
                    sc = scs.pop((g, t))
                    if diagonal and (g + 1) * rh <= (t + 1) * kp:
                        sc = _causal_mask(sc, g * rh - t * kp)
                    m_new = jnp.maximum(m, jnp.max(sc, axis=-1, keepdims=True))
                    p = jnp.exp2(sc - m_new)
                    alpha = jnp.exp2(m - m_new)
                    l = alpha * l + jnp.sum(p, axis=-1, keepdims=True)
                    acc = alpha * acc + _dot(p.astype(BF16), vt[:width(g, t)])
                    state[g] = (m_new, l, acc)
            return tuple(state)

        one = (jnp.full((rh, 1), NEG, F32), jnp.zeros((rh, 1), F32), jnp.zeros((rh, VH), F32))
        nwide = (i * tq) // wide
        carry = lax.fori_loop(0, nwide, lambda j, c: blk(j * wide, wide, c, False), (one,) * groups)
        carry = lax.fori_loop(nwide * (wide // tq), i, lambda j, c: blk(j * tq, tq, c, False), carry)
        carry = blk(i * tq, tq, carry, True)
        for g, (m, l, acc) in enumerate(carry):
            o_ref[g * rh:(g + 1) * rh, :] = acc / l
            lse_ref[0, g * rh:(g + 1) * rh, :] = jnp.broadcast_to(m + jnp.log(l) * LOG2E, (rh, LANES))

    return pl.pallas_call(
        body, name=name, grid=(HEADS, s // tq),
        in_specs=[pl.BlockSpec((1, tq, HP), lambda h, i: (h, i, 0)),
                  pl.BlockSpec((1, s, HP), lambda h, i: (h, 0, 0)),
                  pl.BlockSpec((1, s, HP), lambda h, i: (h, 0, 0))],
        out_specs=[pl.BlockSpec((tq, VH), lambda h, i: (i, h)),
                   pl.BlockSpec((1, tq, LANES), lambda h, i: (h, i, 0))],
        out_shape=[jax.ShapeDtypeStruct((s, HEADS * VH), F32), jax.ShapeDtypeStruct((HEADS, s, LANES), F32)],
        compiler_params=_cp(("parallel", "arbitrary"), VMEM_LIMIT),
    )(q, k, v)


def _lane_group(shape, j):
    return (lax.broadcasted_iota(jnp.int32, shape, 1) + j * LANES) // (POOL // 4)


def _pool_win_fwd(z, name):
    s = z.shape[0]
    ch = _tile(s, 512)
    col0 = (IN_P - POOL) // LANES

    def body(p_ref, m_ref):
        j = pl.program_id(0)

        def chunk(r, _):
            off = pl.multiple_of(r * ch, ch)
            cur = p_ref[pl.ds(off, ch), :]
            hoff = pl.multiple_of(jnp.maximum(off - HALO, 0), 8)
            halo = jnp.where(r > 0, p_ref[pl.ds(hoff, HALO), :], 0.0)
            x = jnp.concatenate([halo, cur], axis=0)
            s2 = x + pltpu.roll(x, 1, 0)
            s4 = s2 + pltpu.roll(s2, 2, 0)
            s8 = s4 + pltpu.roll(s4, 4, 0)
            s16 = s8 + pltpu.roll(s8, 8, 0)
            grp = _lane_group((ch, LANES), j)
            sel = jnp.where(grp == 0, s2[HALO:], jnp.where(grp == 1, s4[HALO:], jnp.where(grp == 2, s8[HALO:], s16[HALO:])))
            t1 = (lax.broadcasted_iota(jnp.int32, (ch, LANES), 0) + off + 1).astype(F32)
            win = jnp.where(grp == 0, 2.0, jnp.where(grp == 1, 4.0, jnp.where(grp == 2, 8.0, 16.0)))
            m_ref[pl.ds(off, ch), :] = sel / jnp.minimum(t1, win) - cur
            return 0

        lax.fori_loop(0, s // ch, chunk, 0)

    return pl.pallas_call(
        body, name=name, grid=(POOL // LANES,),
        in_specs=[pl.BlockSpec((s, LANES), lambda j: (0, col0 + j))],
        out_specs=pl.BlockSpec((s, LANES), lambda j: (0, j)),
        out_shape=jax.ShapeDtypeStruct((s, POOL), F32),
        compiler_params=_cp(("parallel",), VMEM_LIMIT),
    )(z)


def _pool_win_bwd(dm, name):
    s = dm.shape[0]
    ch = _tile(s, 512)
    n = s // ch

    def body(dm_ref, dp_ref):
        j = pl.program_id(0)

        def chunk(r, _):
            off = pl.multiple_of(r * ch, ch)
            grp = _lane_group((ch + HALO, LANES), j)
            win = jnp.where(grp == 0, 2.0, jnp.where(grp == 1, 4.0, jnp.where(grp == 2, 8.0, 16.0)))
            cur = dm_ref[pl.ds(off, ch), :]
            hoff = pl.multiple_of(jnp.minimum(off + ch, s - HALO), 8)
            halo = jnp.where(r < n - 1, dm_ref[pl.ds(hoff, HALO), :], 0.0)
            x = jnp.concatenate([cur, halo], axis=0)
            t1 = (lax.broadcasted_iota(jnp.int32, (ch + HALO, LANES), 0) + off + 1).astype(F32)
            e = x / jnp.minimum(t1, win)
            tot = ch + HALO
            r2 = e + pltpu.roll(e, tot - 1, 0)
            r4 = r2 + pltpu.roll(r2, tot - 2, 0)
            r8 = r4 + pltpu.roll(r4, tot - 4, 0)
            r16 = r8 + pltpu.roll(r8, tot - 8, 0)
            g = grp[:ch]
            sel = jnp.where(g == 0, r2[:ch], jnp.where(g == 1, r4[:ch], jnp.where(g == 2, r8[:ch], r16[:ch])))
            dp_ref[pl.ds(off, ch), :] = (sel - cur).astype(BF16)
            return 0

        lax.fori_loop(0, n, chunk, 0)

    return pl.pallas_call(
        body, name=name, grid=(POOL // LANES,),
        in_specs=[pl.BlockSpec((s, LANES), lambda j: (0, j))],
        out_specs=pl.BlockSpec((s, LANES), lambda j: (0, j)),
        out_shape=jax.ShapeDtypeStruct((s, POOL), BF16),
        compiler_params=_cp(("parallel",), VMEM_LIMIT),
    )(dm)


def _head_mask(h):
    lane = lax.broadcasted_iota(jnp.int32, (CHUNK, SGU), 1)
    return (lane // (SGU // HEADS)) == h


def _tril(upper=False):
    row = lax.broadcasted_iota(jnp.int32, (CHUNK, CHUNK), 0)
    col = lax.broadcasted_iota(jnp.int32, (CHUNK, CHUNK), 1)
    return col >= row if upper else col <= row


def _sgu_gate(vn, wsp, bsp):
    out = []
    for cidx in range(vn.shape[0] // CHUNK):
        vc = vn[cidx * CHUNK:(cidx + 1) * CHUNK]
        zc = bsp
        for h in range(HEADS):
            zc = zc + jnp.where(_head_mask(h), _dot(wsp[h], vc), 0.0)
        out.append(zc)
    return jnp.concatenate(out, axis=0)


def _mix_out_fwd(o, z, m, x, wsp, bsp, wbd, psc, gsv, gout, wout, name):
    s = x.shape[0]
    tm = _tile(s, TOKENS)

    def body(o_ref, uv_ref, m_ref, x_ref, wsp_ref, bsp_ref, wbd_ref, psc_ref, gsv_ref, gout_ref, wout_ref,
             x1_ref, mix_ref):
        g = gout_ref[...]
        an = _rms(o_ref[...], HEADS * VH)[0] * g[:, :512]
        uv = uv_ref[...]
        u, v = uv[:, :SGU], uv[:, SGU:]
        vn = (_rms(v, SGU)[0] * gsv_ref[...]).astype(BF16)
        tri = _tril()
        wsp_m = [jnp.where(tri, wsp_ref[h], 0.0).astype(BF16) for h in range(HEADS)]
        gm = u * _sgu_gate(vn, wsp_m, bsp_ref[...])
        gn = _rms(gm, SGU)[0] * g[:, 512:768]
        po = _dot(m_ref[...].astype(BF16), wbd_ref[...]) * psc_ref[...]
        pn = _rms(po, POOL)[0] * g[:, 768:]
        mix = jnp.concatenate([an, gn, pn], axis=1).astype(BF16)
        mix_ref[...] = mix
        x1_ref[...] = x_ref[...] + _dot(mix, wout_ref[...])

    row = lambda w, j: pl.BlockSpec((tm, w), lambda i: (i, j))
    return pl.pallas_call(
        body, name=name, grid=(s // tm,),
        in_specs=[row(512, 0), row(512, 1), row(POOL, 0), row(D, 0),
                  _acc((HEADS, CHUNK, CHUNK)), _acc((CHUNK, SGU)), _acc((POOL, POOL)), _acc((1, POOL)),
                  _acc((1, SGU)), _acc((1, D)), _res((D, D))],
        out_specs=[row(D, 0), row(D, 0)],
        out_shape=[jax.ShapeDtypeStruct((s, D), F32), jax.ShapeDtypeStruct((s, D), BF16)],
        compiler_params=_cp(("parallel",), VMEM_LIMIT),
    )(o, z, m, x, wsp, bsp, wbd, psc, gsv, gout, wout)


def _ffn_fwd(x1, g, wg, wu, wd, tgt, name):
    s = x1.shape[0]
    tm = _tile(s, 256)
    last = tgt is not None

    def body(x_ref, g_ref, wg_ref, wu_ref, wd_ref, *rest):
        t_ref = rest[0] if last else None
        outs = rest[1:] if last else rest
        a_ref, b_ref, h_ref = outs[-3:]
        x = x_ref[...]
        h = (_rms(x, D)[0] * g_ref[...]).astype(BF16)
        h_ref[...] = h
        acc = jnp.zeros((tm, D), F32)
        for k in range(CHIPS):
            a = _dot_nt(h, wg_ref[k])
            b = _dot_nt(h, wu_ref[k])
            a_ref[k] = a
            b_ref[k] = b
            acc = acc + _dot((a * jax.nn.sigmoid(a) * b).astype(BF16), wd_ref[k])
        if not last:
            outs[0][...] = x + acc
            return
        dy_ref, l_ref = outs[:2]
        e = (x + acc) - t_ref[...]
        dy_ref[...] = e * (1.0 / D)
        sq = jnp.sum(e * e, axis=0, keepdims=True)
        part = sq[:, :LANES]
        for c in range(1, D // LANES):
            part = part + sq[:, c * LANES:(c + 1) * LANES]
        _accumulate(l_ref, part, pl.program_id(0) == 0)

    row = lambda w: pl.BlockSpec((tm, w), lambda i: (i, 0))
    hrow = pl.BlockSpec((CHIPS, tm, SH), lambda i: (0, i, 0))
    hshape = jax.ShapeDtypeStruct((CHIPS, s, SH), F32)
    tail_specs = [hrow, hrow, row(D)]
    tail_shapes = [hshape, hshape, jax.ShapeDtypeStruct((s, D), BF16)]
    head_specs = [row(D), _acc((1, LANES))] if last else [row(D)]
    head_shapes = [jax.ShapeDtypeStruct((s, D), F32)] + ([jax.ShapeDtypeStruct((1, LANES), F32)] if last else [])
    res = pl.pallas_call(
        body, name=name, grid=(s // tm,),
        in_specs=[row(D), _acc((1, D)), _res((CHIPS, SH, D)), _res((CHIPS, SH, D)), _res((CHIPS, SH, D))]
        + ([row(D)] if last else []),
        out_specs=head_specs + tail_specs, out_shape=head_shapes + tail_shapes,
        compiler_params=_cp(("arbitrary",), VMEM_LIMIT),
    )(x1, g, wg, wu, wd, *([tgt] if last else []))
    return (tuple(res[:2]) if last else res[0]), res[-3], res[-2], res[-1]


def _wgrad(a, b, name):
    s, k = a.shape
    n = b.shape[1]
    half = lambda v: v if v <= 1408 else v // 2
    kb, nb, tt = half(k), half(n), _tile(s, 2048)

    def body(a_ref, b_ref, o_ref):
        _accumulate(o_ref, _dot_tn(a_ref[...].astype(BF16), b_ref[...].astype(BF16)), pl.program_id(2) == 0)

    return pl.pallas_call(
        body, name=name, grid=(k // kb, n // nb, s // tt),
        in_specs=[pl.BlockSpec((tt, kb), lambda i, j, t: (t, i)), pl.BlockSpec((tt, nb), lambda i, j, t: (t, j))],
        out_specs=pl.BlockSpec((kb, nb), lambda i, j, t: (i, j)),
        out_shape=jax.ShapeDtypeStruct((k, n), F32),
        compiler_params=_cp(("parallel", "parallel", "arbitrary"), VMEM_LIMIT),
    )(a, b)


def _wgrad_in(h, dzm, duv, dp, name):
    s = h.shape[0]
    tt = _tile(s, 2048)

    def body(h_ref, a_ref, b_ref, c_ref, o_ref):
        hv = h_ref[...]
        val = jnp.concatenate([_dot_tn(hv, a_ref[...]), _dot_tn(hv, b_ref[...]), _dot_tn(hv, c_ref[...])], axis=1)
        _accumulate(o_ref, val, pl.program_id(0) == 0)

    row = lambda w: pl.BlockSpec((tt, w), lambda t: (t, 0))
    return pl.pallas_call(
        body, name=name, grid=(s // tt,), in_specs=[row(D), row(512), row(512), row(POOL)], out_specs=_acc((D, IN_P)),
        out_shape=jax.ShapeDtypeStruct((D, IN_P), F32), compiler_params=_cp(("arbitrary",), VMEM_LIMIT),
    )(h, dzm, duv, dp)


def _wgrad_rows(a, b, name):
    s, n = a.shape[1:]
    nn = b.shape[1]
    tt = _tile(s, 4096 if b.dtype == BF16 else 2048)

    def body(a_ref, b_ref, o_ref):
        _accumulate0(o_ref, _dot_tn(a_ref[0].astype(BF16), b_ref[...].astype(BF16)), pl.program_id(1) == 0)

    return pl.pallas_call(
        body, name=name, grid=(CHIPS, s // tt),
        in_specs=[pl.BlockSpec((1, tt, n), lambda c, t: (c, t, 0)), pl.BlockSpec((tt, nn), lambda c, t: (t, 0))],
        out_specs=pl.BlockSpec((1, n, nn), lambda c, t: (c, 0, 0)),
        out_shape=jax.ShapeDtypeStruct((CHIPS, n, nn), F32),
        compiler_params=_cp(("parallel", "arbitrary"), VMEM_LIMIT),
    )(a, b)


def _ffn_bwd(dx2, x1, a, b, g, wg, wu, wd, name):
    s = x1.shape[0]
    tm = _tile(s, 256)

    def body(dx2_ref, x_ref, a_ref, b_ref, g_ref, wg_ref, wu_ref, wd_ref,
             dx1_ref, hid_ref, da_ref, db_ref, dyb_ref, dg_ref):
        dx2 = dx2_ref[...]
        dyb = dx2.astype(BF16)
        dyb_ref[...] = dyb
        dh = jnp.zeros((tm, D), F32)
        ahead = _dot_nt(dyb, wd_ref[0])
        for k in range(CHIPS):
            av, bv = a_ref[k], b_ref[k]
            dhid = ahead
            if k + 1 < CHIPS:
                ahead = _dot_nt(dyb, wd_ref[k + 1])
            sig = jax.nn.sigmoid(av)
            sa = av * sig
            hid_ref[k] = (sa * bv).astype(BF16)
            dbv = (dhid * sa).astype(BF16)
            dav = (dhid * bv * (sig * (1.0 + av * (1.0 - sig)))).astype(BF16)
            db_ref[k] = dbv
            da_ref[k] = dav
            dh = dh + _dot(dav, wg_ref[k]) + _dot(dbv, wu_ref[k])
        xn, r = _rms(x_ref[...], D)
        dxr, dg = _rms_bwd(xn, r, g_ref[...], dh, D)
        dx1_ref[...] = dx2 + dxr
        _accumulate(dg_ref, dg, pl.program_id(0) == 0)

    row = lambda w: pl.BlockSpec((tm, w), lambda i: (i, 0))
    hrow = pl.BlockSpec((CHIPS, tm, SH), lambda i: (0, i, 0))
    hid = jax.ShapeDtypeStruct((CHIPS, s, SH), BF16)
    return pl.pallas_call(
        body, name=name, grid=(s // tm,),
        in_specs=[row(D), row(D), hrow, hrow, _acc((1, D)), _res((CHIPS, SH, D)), _res((CHIPS, SH, D)),
                  _res((CHIPS, SH, D))],
        out_specs=[row(D), hrow, hrow, hrow, row(D), _acc((1, D))],
        out_shape=[jax.ShapeDtypeStruct((s, D), F32), hid, hid, hid, jax.ShapeDtypeStruct((s, D), BF16),
                   jax.ShapeDtypeStruct((1, D), F32)],
        compiler_params=_cp(("arbitrary",), VMEM_LIMIT),
    )(dx2, x1, a, b, g, wg, wu, wd)


def _mix_out_bwd(dx1, o, z, m, wsp, bsp, wbd, psc, gsv, gout, wout, name):
    s = dx1.shape[0]
    tm = _tile(s, TOKENS)

    def body(dx1_ref, o_ref, uv_ref, m_ref, wsp_ref, bsp_ref, wbd_ref, psc_ref, gsv_ref, gout_ref, wout_ref,
             do_ref, dl_ref, duv_ref, dm_ref, dgo_ref, dgsv_ref, dpsc_ref, dwsp_ref, dbsp_ref, dwbd_ref):
        first = pl.program_id(0) == 0
        g = gout_ref[...]
        dmix = _dot_nt(dx1_ref[...].astype(BF16), wout_ref[...])
        o = o_ref[...]
        on, ro = _rms(o, HEADS * VH)
        do, dga = _rms_bwd(on, ro, g[:, :512], dmix[:, :512], HEADS * VH)
        for h in range(HEADS):
            sl = slice(h * VH, (h + 1) * VH)
            do_ref[h] = do[:, sl].astype(BF16)
            dl_ref[h] = jnp.broadcast_to(jnp.sum(do[:, sl] * o[:, sl], axis=-1, keepdims=True), (tm, LANES))
        uv = uv_ref[...]
        u, v = uv[:, :SGU], uv[:, SGU:]
        vx, rv = _rms(v, SGU)
        vn = (vx * gsv_ref[...]).astype(BF16)
        tri = _tril()
        wsp_m = [jnp.where(tri, wsp_ref[h], 0.0).astype(BF16) for h in range(HEADS)]
        zc = _sgu_gate(vn, wsp_m, bsp_ref[...])
        gm = u * zc
        gmn, rg = _rms(gm, SGU)
        dgm, dgg = _rms_bwd(gmn, rg, g[:, 512:768], dmix[:, 512:768], SGU)
        du = dgm * zc
        dzc = dgm * u
        dvn_parts = []
        dbsp = jnp.zeros((CHUNK, SGU), F32)
        dwsp = [jnp.zeros((CHUNK, CHUNK), F32) for _ in range(HEADS)]
        for cidx in range(tm // CHUNK):
            rs = slice(cidx * CHUNK, (cidx + 1) * CHUNK)
            dzc_c = dzc[rs]
            dbsp = dbsp + dzc_c
            dzb = dzc_c.astype(BF16)
            vc = vn[rs]
            dvn_c = jnp.zeros((CHUNK, SGU), F32)
            for h in range(HEADS):
                hm = _head_mask(h)
                dvn_c = dvn_c + jnp.where(hm, _dot_tn(wsp_m[h], dzb), 0.0)
                dwsp[h] = dwsp[h] + _dot_nt(jnp.where(hm, dzc_c, 0.0).astype(BF16), vc)
            dvn_parts.append(dvn_c)
        dvn = jnp.concatenate(dvn_parts, axis=0)
        dv, dgsv = _rms_bwd(vx, rv, gsv_ref[...], dvn, SGU)
        duv_ref[...] = jnp.concatenate([du, dv], axis=1).astype(BF16)
        mb = m_ref[...].astype(BF16)
        pw = _dot(mb, wbd_ref[...])
        po = pw * psc_ref[...]
        pon, rp = _rms(po, POOL)
        dpo, dgp = _rms_bwd(pon, rp, g[:, 768:], dmix[:, 768:], POOL)
        dpw = (dpo * psc_ref[...]).astype(BF16)
        dm_ref[...] = _dot_nt(dpw, wbd_ref[...])
        _accumulate(dgo_ref, jnp.concatenate([dga, dgg, dgp], axis=1), first)
        _accumulate(dgsv_ref, dgsv, first)
        _accumulate(dpsc_ref, jnp.sum(dpo * pw, axis=0, keepdims=True), first)
        _accumulate(dbsp_ref, dbsp, first)
        _accumulate(dwbd_ref, _dot_tn(mb, dpw), first)
        for h in range(HEADS):
            val = jnp.where(tri, dwsp[h], 0.0)

            @pl.when(first)
            def _(val=val, h=h):
                dwsp_ref[h] = val

            @pl.when(jnp.logical_not(first))
            def _(val=val, h=h):
                dwsp_ref[h] += val

    row = lambda w, j: pl.BlockSpec((tm, w), lambda i: (i, j))
    hspec = pl.BlockSpec((HEADS, tm, HP), lambda i: (0, i, 0))
    return pl.pallas_call(
        body, name=name, grid=(s // tm,),
        in_specs=[row(D, 0), row(512, 0), row(512, 1), row(POOL, 0),
                  _acc((HEADS, CHUNK, CHUNK)), _acc((CHUNK, SGU)),
                  _acc((POOL, POOL)), _acc((1, POOL)), _acc((1, SGU)), _acc((1, D)), _res((D, D))],
        out_specs=[hspec, hspec, row(512, 0), row(POOL, 0), _acc((1, D)), _acc((1, SGU)), _acc((1, POOL)),
                   _acc((HEADS, CHUNK, CHUNK)), _acc((CHUNK, SGU)), _acc((POOL, POOL))],
        out_shape=[jax.ShapeDtypeStruct((HEADS, s, HP), BF16), jax.ShapeDtypeStruct((HEADS, s, LANES), F32),
                   jax.ShapeDtypeStruct((s, 512), BF16), jax.ShapeDtypeStruct((s, POOL), F32),
                   jax.ShapeDtypeStruct((1, D), F32), jax.ShapeDtypeStruct((1, SGU), F32),
                   jax.ShapeDtypeStruct((1, POOL), F32), jax.ShapeDtypeStruct((HEADS, CHUNK, CHUNK), F32),
                   jax.ShapeDtypeStruct((CHUNK, SGU), F32), jax.ShapeDtypeStruct((POOL, POOL), F32)],
        compiler_params=_cp(("arbitrary",), VMEM_LIMIT),
    )(dx1, o, z, m, wsp, bsp, wbd, psc, gsv, gout, wout)


def _attn_bwd(q, k, v, do, lse, delta, after, name):
    s = q.shape[1]
    rh = _tile(s, ATT_ROWS)
    tk = _tile(s, ATT_KEYS)
    nk = s // tk
    wide = ATT_QUERIES if s % ATT_QUERIES == 0 else tk
    pieces = tk // rh

    def body(q_ref, k_ref, v_ref, do_ref, lse_ref, dl_ref, after_ref, dq_ref, dk_ref, dv_ref):
        del after_ref
        j = pl.program_id(1)

        @pl.when(j == 0)
        def _():
            dq_ref[...] = jnp.zeros_like(dq_ref)

        kj, vj = k_ref[0], v_ref[0]

        def blk(start, rows, dks, dvs, diagonal):
            dks, dvs = list(dks), list(dvs)
            offs = [pl.multiple_of(start + g * rh, rh) for g in range(rows // rh)]
            keys = [(g + 1) * rh if diagonal else tk for g in range(rows // rh)]
            qs = [q_ref[0, pl.ds(off, rh), :] for off in offs]
            dos = [do_ref[0, pl.ds(off, rh), :] for off in offs]
            scs = [_dot_nt(qi, kj[:n]) for qi, n in zip(qs, keys)]
            dps = [_dot_nt(doi, vj[:n]) for doi, n in zip(dos, keys)]
            for g, off in enumerate(offs):
                lse_i = lse_ref[0, pl.ds(off, rh), :][:, :1]
                dl_i = dl_ref[0, pl.ds(off, rh), :][:, :1]
                sc = _causal_mask(scs[g], g * rh) if diagonal else scs[g]
                p = jnp.exp2(sc - lse_i)
                ds = (p * (dps[g] - dl_i)).astype(BF16)
                cv = _dot_tn(p.astype(BF16), dos[g])
                ck = _dot_tn(ds, qs[g])
                for t in range(keys[g] // rh):
                    dvs[t] = dvs[t] + cv[t * rh:(t + 1) * rh]
                    dks[t] = dks[t] + ck[t * rh:(t + 1) * rh]
                dq_ref[0, pl.ds(off, rh), :] += _dot(ds, kj[:keys[g]]) * SCALE
            return tuple(dks), tuple(dvs)

        per = wide // tk
        zero = (jnp.zeros((rh, HP), F32),) * pieces
        acc = blk(j * tk, tk, zero, zero, True)
        first_wide = (j + per) // per
        acc = lax.fori_loop(j + 1, jnp.minimum(first_wide * per, nk), lambda i, c: blk(i * tk, tk, *c, False), acc)
        dks, dvs = lax.fori_loop(first_wide, nk // per, lambda i, c: blk(i * wide, wide, *c, False), acc)
        dk_ref[0] = jnp.concatenate(dks, axis=0) * (SCALE / EXP2_C)
        dv_ref[0] = jnp.concatenate(dvs, axis=0)

    full = lambda: pl.BlockSpec((1, s, HP), lambda h, j: (h, 0, 0))
    blk_spec = lambda: pl.BlockSpec((1, tk, HP), lambda h, j: (h, j, 0))
    out = jax.ShapeDtypeStruct((HEADS, s, HP), F32)
    return pl.pallas_call(
        body, name=name, grid=(HEADS, s // tk),
        in_specs=[full(), blk_spec(), blk_spec(), full(), full(), full(), ANY],
        out_specs=[full(), blk_spec(), blk_spec()], out_shape=[out] * 3,
        compiler_params=_cp(("parallel", "arbitrary"), VMEM_LIMIT),
    )(q, k, v, do, lse, delta, after)


def _mla_prep_bwd(dq, dk, dv, z, tabs, gql, gkv, gq, gk, wq, wk, wv, name):
    s = z.shape[0]
    tm = _tile(s, TOKENS)

    def body(dq_ref, dk_ref, dv_ref, ql_ref, kv_ref, kr_ref, c_ref, sa_ref, sb_ref, gql_ref, gkv_ref, gq_ref, gk_ref,
             wq_ref, wk_ref, wv_ref,
             dz_ref, qn_ref, kvn_ref, dqr_ref, dkr_ref, dvr_ref, dgql_ref, dgkv_ref, dgq_ref, dgk_ref):
        first = pl.program_id(0) == 0
        qx, rq = _rms(ql_ref[...], QL)
        qn = (qx * gql_ref[...]).astype(BF16)
        kx, rk = _rms(kv_ref[...], KVL)
        kvn = (kx * gkv_ref[...]).astype(BF16)
        qn_ref[...] = qn
        kvn_ref[...] = kvn
        qraw = _dot(qn, wq_ref[...])
        kraw = _dot(kvn, wk_ref[...])
        kr = kr_ref[...]
        c, sa, sb = c_ref[...], sa_ref[...], sb_ref[...]
        lane = lax.broadcasted_iota(jnp.int32, (tm, HP), 1)
        rope_lanes = (lane >= NOPE) & (lane < QK)
        dkrope = jnp.zeros((tm, HP), F32)
        dgq = jnp.zeros((1, HP), F32)
        dgk = jnp.zeros((1, HP), F32)
        for h in range(HEADS):
            sl = slice(h * HP, (h + 1) * HP)
            xn, r = _rms(qraw[:, sl], QK)
            dx, dg = _rms_bwd(xn, r, gq_ref[...], _rope_t(dq_ref[h], c, sa, sb), QK)
            dqr_ref[:, sl] = dx.astype(BF16)
            dgq = dgq + dg
            xn, r = _rms(kraw[:, sl] + kr, QK)
            dx, dg = _rms_bwd(xn, r, gk_ref[...], _rope_t(dk_ref[h], c, sa, sb), QK)
            dkr_ref[:, sl] = dx.astype(BF16)
            dgk = dgk + dg
            dkrope = dkrope + jnp.where(rope_lanes, dx, 0.0)
            dvr_ref[:, sl] = dv_ref[h].astype(BF16)
        dqn = _dot_nt(dqr_ref[...], wq_ref[...])
        dql, dgql = _rms_bwd(qx, rq, gql_ref[...], dqn, QL)
        dkvn = _dot_nt(dkr_ref[...], wk_ref[...]) + _dot_nt(dvr_ref[...], wv_ref[...])
        dkv, dgkv = _rms_bwd(kx, rk, gkv_ref[...], dkvn, KVL)
        dz_ref[...] = jnp.concatenate([dql, dkv, dkrope], axis=1).astype(BF16)
        _accumulate(dgql_ref, dgql, first)
        _accumulate(dgkv_ref, dgkv, first)
        _accumulate(dgq_ref, dgq, first)
        _accumulate(dgk_ref, dgk, first)

    row = lambda w, j: pl.BlockSpec((tm, w), lambda i: (i, j))
    hspec = pl.BlockSpec((HEADS, tm, HP), lambda i: (0, i, 0))
    sd = lambda w, dt: jax.ShapeDtypeStruct((s, w), dt)
    return pl.pallas_call(
        body, name=name, grid=(s // tm,),
        in_specs=[hspec, hspec, hspec, row(QL, 0), row(KVL, 2), row(HP, 3), row(HP, 0), row(HP, 0), row(HP, 0),
                  _acc((1, QL)), _acc((1, KVL)), _acc((1, HP)), _acc((1, HP)),
                  _acc((QL, HEADS * HP)), _acc((KVL, HEADS * HP)), _acc((KVL, HEADS * HP))],
        out_specs=[row(512, 0), row(QL, 0), row(KVL, 0), row(512, 0), row(512, 0), row(512, 0),
                   _acc((1, QL)), _acc((1, KVL)), _acc((1, HP)), _acc((1, HP))],
        out_shape=[sd(512, BF16), sd(QL, BF16), sd(KVL, BF16), sd(512, BF16), sd(512, BF16), sd(512, BF16),
                   jax.ShapeDtypeStruct((1, QL), F32), jax.ShapeDtypeStruct((1, KVL), F32),
                   jax.ShapeDtypeStruct((1, HP), F32), jax.ShapeDtypeStruct((1, HP), F32)],
        compiler_params=_cp(("arbitrary",), VMEM_LIMIT),
    )(dq, dk, dv, z, z, z, *tabs, gql, gkv, gq, gk, wq, wk, wv)


def _in_proj_bwd(dzm, duv, dp, x, dx1, g, win, name):
    s = x.shape[0]
    tm = _tile(s, TOKENS // 2)

    def body(dzm_ref, duv_ref, dp_ref, x_ref, dx1_ref, g_ref, w_ref, dx_ref, dg_ref):
        groups = [slice(r0, r0 + tm // 2) for r0 in (0, tm // 2)]
        dhs = [_dot_nt(dzm_ref[rs, :], w_ref[:, 0:512]) + _dot_nt(duv_ref[rs, :], w_ref[:, 512:1024])
               + _dot_nt(dp_ref[rs, :], w_ref[:, 1024:IN_P]) for rs in groups]
        dg = jnp.zeros((1, D), F32)
        for rs, dh in zip(groups, dhs):
            xn, r = _rms(x_ref[rs, :], D)
            dxr, dgr = _rms_bwd(xn, r, g_ref[...], dh, D)
            dx_ref[rs, :] = dx1_ref[rs, :] + dxr
            dg = dg + dgr
        _accumulate(dg_ref, dg, pl.program_id(0) == 0)

    row = lambda w: pl.BlockSpec((tm, w), lambda i: (i, 0))
    return pl.pallas_call(
        body, name=name, grid=(s // tm,),
        in_specs=[row(512), row(512), row(POOL), row(D), row(D), _acc((1, D)), _res((D, IN_P))],
        out_specs=[row(D), _acc((1, D))],
        out_shape=[jax.ShapeDtypeStruct((s, D), F32), jax.ShapeDtypeStruct((1, D), F32)],
        compiler_params=_cp(("arbitrary",), VMEM_LIMIT),
    )(dzm, duv, dp, x, dx1, g, win)


def _adamw(w, g0, g1, m, v, name):
    _, r, c = w.shape
    tr = _row_tile(r, 512)
    c1 = 1.0 - B1 ** STEP
    c2 = 1.0 - B2 ** STEP

    def body(w_ref, g0_ref, g1_ref, m_ref, v_ref, g_ref, d_ref, nm_ref, nv_ref):
        gv = jnp.where(pl.program_id(0) == 0, g0_ref[...], g1_ref[...])
        g_ref[0] = gv
        nm = B1 * m_ref[0] + (1.0 - B1) * gv
        nv = B2 * v_ref[0] + (1.0 - B2) * (gv * gv)
        nm_ref[0] = nm
        nv_ref[0] = nv
        d_ref[0] = -LR * ((nm / c1) / (jnp.sqrt(nv / c2) + ADAM_EPS) + WD * w_ref[0])

    spec = pl.BlockSpec((1, tr, c), lambda l, i: (l, i, 0))
    out = jax.ShapeDtypeStruct((DEPTH, r, c), F32)
    return pl.pallas_call(
        body, name=name, grid=(DEPTH, r // tr),
        in_specs=[spec, pl.BlockSpec((tr, c), lambda l, i: (i * (1 - l), 0)), pl.BlockSpec((tr, c), lambda l, i: (i * l, 0)),
                  spec, spec],
        out_specs=[spec] * 4, out_shape=[out] * 4, compiler_params=_cp(("parallel", "parallel")),
    )(w, g0, g1, m, v)


ANY = pl.BlockSpec(memory_space=pl.ANY)


def _place():
    x, y, c = lax.axis_index("x"), lax.axis_index("y"), lax.axis_index("c")
    chips = [(1 - x, y), (x, 1 - y), (1 - x, 1 - y)]
    return x, y, c, chips


def _half_rows(ref, lead, hh, half, align):
    rows = pl.ds(pl.multiple_of(hh * half, align), half)
    return ref.at[rows, :] if lead is None else ref.at[lead, rows, :]


def _row_align(dtype):
    return 16 if dtype == BF16 else 8


def _sems(n):
    return [pltpu.SemaphoreType.DMA((n,)), pltpu.SemaphoreType.DMA((n,)), pltpu.SemaphoreType.DMA((n,))]


def _comm_call(body, ins, out_shapes, nsems, name):
    return pl.pallas_call(
        body, name=name, in_specs=[ANY] * len(ins), out_specs=[ANY] * len(out_shapes), out_shape=out_shapes,
        scratch_shapes=_sems(nsems), compiler_params=pltpu.CompilerParams(has_side_effects=True),
    )(*ins)


def _all_gather_chips(shards, name):
    n = len(shards)
    halves = [a.shape[0] // 2 for a in shards]
    aligns = [_row_align(a.dtype) for a in shards]
    assert all(h % al == 0 for h, al in zip(halves, aligns))

    def body(*refs):
        ins, outs, (send_sems, recv_sems, _) = refs[:n], refs[n:2 * n], refs[2 * n:]
        x, y, c, chips = _place()
        me = 2 * x + y
        sibling = (x, y, 1 - c)

        def copy(sem, src, dst, to):
            return pltpu.make_async_remote_copy(src_ref=src, dst_ref=dst, send_sem=send_sems.at[sem],
                                                recv_sem=recv_sems.at[sem], device_id=to, device_id_type=MESH)

        first, passed = [], []
        for a in range(n):
            my_half = _half_rows(ins[a], None, c, halves[a], aligns[a])
            for j, (cx, cy) in enumerate(chips):
                cp = copy(6 * a + j, my_half, _half_rows(outs[a], me, c, halves[a], aligns[a]), (cx, cy, c))
                cp.start()
                first.append(cp)
        for a in range(n):
            for j, (cx, cy) in enumerate(chips):
                landed = _half_rows(outs[a], 2 * cx + cy, c, halves[a], aligns[a])
                copy(6 * a + j, landed, landed, (cx, cy, c)).wait_recv()
                fwd = copy(6 * a + 3 + j, landed, landed, sibling)
                fwd.start()
                passed.append(fwd)
        for a in range(n):
            for j, (cx, cy) in enumerate(chips):
                other = _half_rows(outs[a], 2 * cx + cy, 1 - c, halves[a], aligns[a])
                copy(6 * a + 3 + j, other, other, sibling).wait_recv()
        for cp in first + passed:
            cp.wait_send()

    lands = _comm_call(body, shards, [jax.ShapeDtypeStruct((CHIPS,) + a.shape, a.dtype) for a in shards], 6 * n, name)
    return _with_own(lands, shards)


def _with_own(lands, shards):
    me = 2 * lax.axis_index("x") + lax.axis_index("y")
    return [lax.dynamic_update_slice(g, a[None], (me, 0, 0)) for g, a in zip(lands, shards)]


def _pair_join(arrs, name):
    n = len(arrs)
    halves = [a.shape[0] // 2 for a in arrs]

    def body(*refs):
        outs, (send_sems, recv_sems, _) = refs[n:2 * n], refs[2 * n:]
        x, y, c, _ = _place()
        cps = []
        for a in range(n):
            mine = _half_rows(outs[a], None, c, halves[a], 8)
            cp = pltpu.make_async_remote_copy(src_ref=mine, dst_ref=mine, send_sem=send_sems.at[a], recv_sem=recv_sems.at[a],
                                              device_id=(x, y, 1 - c), device_id_type=MESH)
            cp.start()
            cps.append(cp)
        for cp in cps:
            cp.wait()

    return pl.pallas_call(
        body, name=name, in_specs=[ANY] * n, out_specs=[ANY] * n,
        out_shape=[jax.ShapeDtypeStruct(a.shape, a.dtype) for a in arrs],
        input_output_aliases={i: i for i in range(n)}, scratch_shapes=_sems(n),
        compiler_params=pltpu.CompilerParams(has_side_effects=True),
    )(*arrs)


HBM = pl.BlockSpec(memory_space=pltpu.HBM)
SEM = pl.BlockSpec(memory_space=pltpu.SEMAPHORE)
DATAFLOW = pltpu.SideEffectType.DATAFLOW_SIDE_EFFECTING


def _remote_copies(pairs, ins, lands, send_sems, recv_sems):
    return [pltpu.make_async_remote_copy(src_ref=src, dst_ref=dst, send_sem=send_sems.at[i], recv_sem=recv_sems.at[i],
                                         device_id=to, device_id_type=MESH)
            for i, (src, dst, to) in enumerate(pairs(ins, lands))]


def _split_start(srcs, land_shapes, ncopies, pairs, name, after):
    n, m = len(srcs), len(land_shapes)

    def body(*refs):
        ins, lands = refs[:n], refs[n:n + m]
        send_sems, recv_sems, token = refs[n + m + 1], refs[n + m + 2], refs[-1]
        for cp in _remote_copies(pairs, ins, lands, send_sems, recv_sems):
            cp.start()
        token[...] = jnp.zeros_like(token)

    hbm = lambda a: pltpu.with_memory_space_constraint(a, pltpu.HBM)
    lands = [hbm(lax.empty(s.shape, s.dtype)) for s in land_shapes]
    thru = [pltpu.HBM(a.shape, a.dtype) for a in list(srcs) + lands]
    out = pl.pallas_call(
        body, name=name,
        out_shape=(pltpu.SemaphoreType.DMA((ncopies,)), pltpu.SemaphoreType.DMA((ncopies,)), *thru,
                   jax.ShapeDtypeStruct((8, LANES), F32)),
        in_specs=[HBM] * (n + m) + [ANY], out_specs=(SEM, SEM, *[HBM] * (n + m), pl.BlockSpec(memory_space=pltpu.VMEM)),
        input_output_aliases={i: 2 + i for i in range(n + m)},
        compiler_params=pltpu.CompilerParams(has_side_effects=DATAFLOW),
    )(*[hbm(a) for a in srcs], *lands, after)
    return out[0], out[1], list(out[2:2 + n]), list(out[2 + n:2 + n + m]), out[-1]


def _split_wait(send_sems, recv_sems, srcs, lands, after, pairs, name):
    n, m = len(srcs), len(lands)

    def body(*refs):
        ins, lands_ = refs[:n], refs[n:n + m]
        for cp in _remote_copies(pairs, ins, lands_, refs[n + m], refs[n + m + 1]):
            cp.wait_send()
            cp.wait_recv()

    out = pl.pallas_call(
        body, name=name, out_shape=tuple(pltpu.HBM(a.shape, a.dtype) for a in list(srcs) + list(lands)),
        in_specs=[HBM] * (n + m) + [SEM, SEM, ANY], out_specs=tuple([HBM] * (n + m)),
        input_output_aliases={i: i for i in range(n + m)},
        compiler_params=pltpu.CompilerParams(has_side_effects=DATAFLOW),
    )(*srcs, *lands, send_sems, recv_sems, after)
    return list(out[:n]), list(out[n:])


def _gather_pairs(halves, aligns):
    def pairs(ins, lands):
        x, y, c, chips = _place()
        me = 2 * x + y
        return [(_half_rows(ins[a], None, c, halves[a], aligns[a]), _half_rows(lands[a], me, c, halves[a], aligns[a]),
                 (cx, cy, c)) for a in range(len(ins)) for cx, cy in chips]
    return pairs


PEERS = 7


def _scatter_pairs(ins, lands):
    x, y, c, chips = _place()
    to = [(cx, cy, c) for cx, cy in chips] + [(cx, cy, 1 - c) for cx, cy in chips] + [(x, y, 1 - c)]
    out = []
    for a in range(len(ins)):
        half = ins[a].shape[1] // 2
        for i, (tx, ty, tc) in enumerate(to):
            out.append((_half_rows(ins[a], 2 * tx + ty, tc, half, 8), lands[a].at[i], (tx, ty, tc)))
    return out


def _gather_finish(shards, lands, name):
    n = len(shards)
    halves = [a.shape[0] // 2 for a in shards]
    aligns = [_row_align(a.dtype) for a in shards]

    def body(*refs):
        outs, (send_sems, recv_sems, _) = refs[n:2 * n], refs[2 * n:]
        x, y, c, chips = _place()
        passed = []
        for a in range(n):
            for j, (cx, cy) in enumerate(chips):
                landed = _half_rows(outs[a], 2 * cx + cy, c, halves[a], aligns[a])
                cp = pltpu.make_async_remote_copy(src_ref=landed, dst_ref=landed, send_sem=send_sems.at[3 * a + j],
                                                  recv_sem=recv_sems.at[3 * a + j], device_id=(x, y, 1 - c),
                                                  device_id_type=MESH)
                cp.start()
                passed.append(cp)
        for a in range(n):
            for j, (cx, cy) in enumerate(chips):
                other = _half_rows(outs[a], 2 * cx + cy, 1 - c, halves[a], aligns[a])
                pltpu.make_async_remote_copy(src_ref=other, dst_ref=other, send_sem=send_sems.at[3 * a + j],
                                             recv_sem=recv_sems.at[3 * a + j], device_id=(x, y, 1 - c),
                                             device_id_type=MESH).wait_recv()
        for cp in passed:
            cp.wait_send()

    lands = pl.pallas_call(
        body, name=name, in_specs=[ANY] * n, out_specs=[ANY] * n,
        out_shape=[jax.ShapeDtypeStruct(a.shape, a.dtype) for a in lands],
        input_output_aliases={i: i for i in range(n)}, scratch_shapes=_sems(3 * n),
        compiler_params=pltpu.CompilerParams(has_side_effects=True),
    )(*lands)
    return _with_own(lands, shards)


def _sum_own_and_landed(own, landed, where, name):
    _, half, cols = landed.shape
    tr = _row_tile(half, 128)
    nt = half // tr

    grid_spec = pltpu.PrefetchScalarGridSpec(
        num_scalar_prefetch=1, grid=(nt,),
        in_specs=[pl.BlockSpec((1, tr, cols), lambda r, w: (w[0], w[1] * nt + r, 0)),
                  pl.BlockSpec((PEERS, tr, cols), lambda r, w: (0, r, 0))],
        out_specs=pl.BlockSpec((tr, cols), lambda r, w: (w[1] * nt + r, 0)))

    def body(w_ref, p_ref, q_ref, o_ref):
        acc = p_ref[0]
        for i in range(PEERS):
            acc = acc + q_ref[i]
        o_ref[...] = acc

    return pl.pallas_call(
        body, name=name, grid_spec=grid_spec, out_shape=jax.ShapeDtypeStruct((2 * half, cols), own.dtype),
        compiler_params=_cp(("parallel",)),
    )(where, own, landed)


BIG = [("w_in", (D, IN_W), 1), ("w_q_up", (QL, HEADS * QK), 1), ("w_kv_up", (KVL, HEADS * (NOPE + VH)), 1),
       ("w_out", (D, D), 0), ("w_gate", (D, HID), 1), ("w_up", (D, HID), 1), ("w_down", (HID, D), 0)]
SMALL = [("g_mix_norm", (D,)), ("g_q_lat", (QL,)), ("g_kv_lat", (KVL,)), ("g_q_head", (QK,)), ("g_k_head", (QK,)),
         ("g_sgu_v", (SGU,)), ("w_spatial", (HEADS, CHUNK, CHUNK)), ("b_spatial", (HEADS, CHUNK)),
         ("w_pool", (4, 64, 64)), ("pool_scale", (POOL,)), ("g_out_mla", (512,)), ("g_out_sgu", (SGU,)),
         ("g_out_pool", (POOL,)), ("g_ffn_norm", (D,))]
ORDER = ["g_mix_norm", "w_in", "g_q_lat", "w_q_up", "g_kv_lat", "w_kv_up", "g_q_head", "g_k_head", "g_sgu_v",
         "w_spatial", "b_spatial", "w_pool", "pool_scale", "g_out_mla", "g_out_sgu", "g_out_pool", "w_out",
         "g_ffn_norm", "w_gate", "w_up", "w_down"]
EARLY_BIG = ["w_in", "w_q_up", "w_kv_up"]
FFN_BIG = ["w_gate", "w_up", "w_down"]
LATE_BIG = ["w_out"] + FFN_BIG
DEPTH = 2
COLS = 1024
SMALL_N = sum(math.prod(s) for _, s in SMALL) * DEPTH
assert SMALL_N % CHIPS == 0
SMALL_ROWS = -(-(SMALL_N // CHIPS + 1) // (16 * COLS)) * 16


def _unsplit_cols(g):
    return g.transpose(1, 0, 2).reshape(g.shape[1], CHIPS * g.shape[2])


def _split_cols(full):
    r, c = full.shape
    return full.reshape(r, CHIPS, c // CHIPS).transpose(1, 0, 2)


def _kernel_weights(g):
    win = _unsplit_cols(g["w_in"])
    zeros = lambda r, c: jnp.zeros((r, c), BF16)
    o2, o3, o4 = QL + KVL, QL + KVL + ROPE, QL + KVL + ROPE + 2 * SGU
    win_p = jnp.concatenate([win[:, :o2], zeros(D, NOPE), win[:, o2:o3], zeros(D, HP - QK), win[:, o3:o4], win[:, o4:]], axis=1)
    wq = _unsplit_cols(g["w_q_up"]).reshape(QL, HEADS, QK)
    wq_p = jnp.pad(wq, ((0, 0), (0, 0), (0, HP - QK))).reshape(QL, HEADS * HP)
    wkv = _unsplit_cols(g["w_kv_up"]).reshape(KVL, HEADS, NOPE + VH)
    wk_p = jnp.pad(wkv[:, :, :NOPE], ((0, 0), (0, 0), (0, HP - NOPE))).reshape(KVL, HEADS * HP)
    wv_p = wkv[:, :, NOPE:].reshape(KVL, HEADS * VH)
    return dict(win=win_p, wq=wq_p, wk=wk_p, wv=wv_p)


def _small_operands(p, l):
    row = lambda v: v.reshape(1, -1)
    pad = lambda v: jnp.pad(v, (0, HP - QK)).reshape(1, HP)
    wpool = p["w_pool"][l]
    wbd = jnp.zeros((POOL, POOL), F32)
    for g in range(4):
        wbd = lax.dynamic_update_slice(wbd, wpool[g], (g * 64, g * 64))
    return dict(
        g_mix=row(p["g_mix_norm"][l]), gql=row(p["g_q_lat"][l]), gkv=row(p["g_kv_lat"][l]),
        gq=pad(p["g_q_head"][l]), gk=pad(p["g_k_head"][l]), gsv=row(p["g_sgu_v"][l]),
        wsp=p["w_spatial"][l], bsp=jnp.repeat(p["b_spatial"][l].T, SGU // HEADS, axis=1),
        wbd=wbd.astype(BF16), psc=row(p["pool_scale"][l]),
        gout=jnp.concatenate([p["g_out_mla"][l], p["g_out_sgu"][l], p["g_out_pool"][l]]).reshape(1, D),
        g_ffn=row(p["g_ffn_norm"][l]))


def _big_grads(g):
    dwin = g["win"]
    o2 = QL + KVL
    gin = jnp.concatenate([dwin[:, :o2], dwin[:, o2 + NOPE:o2 + NOPE + ROPE], dwin[:, 512:]], axis=1)
    gq = g["wq"].reshape(QL, HEADS, HP)[:, :, :QK].reshape(QL, HEADS * QK)
    gk = g["wk"].reshape(KVL, HEADS, HP)[:, :, :NOPE]
    gv = g["wv"].reshape(KVL, HEADS, VH)
    gkv = jnp.concatenate([gk, gv], axis=2).reshape(KVL, HEADS * (NOPE + VH))
    return {"w_in": _split_cols(gin), "w_q_up": _split_cols(gq), "w_kv_up": _split_cols(gkv),
            "w_out": g["wout"].reshape(CHIPS, D // CHIPS, D), "w_gate": g["wg"], "w_up": g["wu"], "w_down": g["wd"]}


TRANSPOSED = ("w_gate", "w_up")


def _small_grads(g):
    go = g["gout"].reshape(-1)
    return {"g_mix_norm": g["g_mix"].reshape(-1), "g_q_lat": g["gql"].reshape(-1), "g_kv_lat": g["gkv"].reshape(-1),
            "g_q_head": g["gq"].reshape(-1)[:QK], "g_k_head": g["gk"].reshape(-1)[:QK], "g_sgu_v": g["gsv"].reshape(-1),
            "w_spatial": g["wsp"], "b_spatial": g["bsp"].reshape(CHUNK, HEADS, SGU // HEADS).sum(-1).T,
            "w_pool": jnp.stack([g["wbd"][i * 64:(i + 1) * 64, i * 64:(i + 1) * 64] for i in range(4)]),
            "pool_scale": g["psc"].reshape(-1), "g_out_mla": go[:512], "g_out_sgu": go[512:768],
            "g_out_pool": go[768:], "g_ffn_norm": g["g_ffn"].reshape(-1)}


def _pack_small_grads(small, loss):
    sm = jnp.concatenate([small[l][n].reshape(-1) for l in range(DEPTH) for n, _ in SMALL]).reshape(CHIPS, SMALL_N // CHIPS)
    sm = jnp.pad(sm, ((0, 0), (0, SMALL_ROWS * COLS - SMALL_N // CHIPS)))
    return sm.at[0, SMALL_N // CHIPS].set(loss).reshape(CHIPS, SMALL_ROWS, COLS)


def _unpack_small_grads(gathered):
    rows = gathered.reshape(CHIPS, SMALL_ROWS * COLS)
    loss = rows[0, SMALL_N // CHIPS]
    flat = rows[:, :SMALL_N // CHIPS].reshape(-1)
    out, off = [], 0
    for _ in range(DEPTH):
        layer = {}
        for n, shape in SMALL:
            k = math.prod(shape)
            layer[n] = flat[off:off + k].reshape(shape)
            off += k
        out.append(layer)
    return out, loss


def _layer_fwd(x, tabs, kw, late_weights, sp, l, tgt):
    t = f"_l{l}"
    z, hb = _in_proj_fwd(x, sp["g_mix"], kw["win"], "in_proj_fwd" + t)
    q, k, v = _mla_prep_fwd(z, tabs, sp["gql"], sp["gkv"], sp["gq"], sp["gk"], kw["wq"], kw["wk"], kw["wv"],
                            "mla_prep_fwd" + t)
    o, lse = _attn_fwd(q, k, v, "attn_fwd" + t)
    m = _pool_win_fwd(z, "pool_win_fwd" + t)
    wout, wg, wu, wd = late_weights(o)
    wout = wout.reshape(D, D)
    x1, mix = _mix_out_fwd(o, z, m, x, sp["wsp"], sp["bsp"], sp["wbd"], sp["psc"], sp["gsv"], sp["gout"], wout,
                           "mix_out_fwd" + t)
    x2, a, b, h2 = _ffn_fwd(x1, sp["g_ffn"], wg, wu, wd, tgt, "ffn_fwd" + t)
    saved = dict(x=x, z=z, hb=hb, q=q, k=k, v=v, o=o, lse=lse, m=m, x1=x1, mix=mix, a=a, b=b, h2=h2, wg=wg, wu=wu, wd=wd,
                 wout=wout)
    return x2, saved


def _layer_bwd(dx2, sv, tabs, kw, sp, l, ffn_hook, out_hook):
    t = f"_l{l}"
    g = {}
    dx1, hid, da, db, dyb, g["g_ffn"] = _ffn_bwd(dx2, sv["x1"], sv["a"], sv["b"], sp["g_ffn"], sv["wg"], sv["wu"],
                                                 sv["wd"], "ffn_bwd" + t)
    g["wd"] = _wgrad_rows(hid, dyb, "wgrad_down" + t)
    g["wg"] = _wgrad_rows(da, sv["h2"], "wgrad_gate" + t)
    g["wu"] = _wgrad_rows(db, sv["h2"], "wgrad_up" + t)
    gout = sp["gout"] + ffn_hook(g)
    do, delta, duv, dm, g["gout"], g["gsv"], g["psc"], g["wsp"], g["bsp"], g["wbd"] = _mix_out_bwd(
        dx1, sv["o"], sv["z"], sv["m"], sp["wsp"], sp["bsp"], sp["wbd"], sp["psc"], sp["gsv"], gout, sv["wout"],
        "mix_out_bwd" + t)
    g["wout"] = _wgrad(sv["mix"], dx1, "wgrad_out" + t)
    dp = _pool_win_bwd(dm, "pool_win_bwd" + t)
    dq, dk, dv = _attn_bwd(sv["q"], sv["k"], sv["v"], do, sv["lse"], delta, out_hook(g), "attn_bwd" + t)
    dzm, qn, kvn, dqr, dkr, dvr, g["gql"], g["gkv"], g["gq"], g["gk"] = _mla_prep_bwd(
        dq, dk, dv, sv["z"], tabs, sp["gql"], sp["gkv"], sp["gq"], sp["gk"], kw["wq"], kw["wk"], kw["wv"],
        "mla_prep_bwd" + t)
    g["wq"] = _wgrad(qn, dqr, "wgrad_q_up" + t)
    g["wk"] = _wgrad(kvn, dkr, "wgrad_k_up" + t)
    g["wv"] = _wgrad(kvn, dvr, "wgrad_v_up" + t)
    dx, g["g_mix"] = _in_proj_bwd(dzm, duv, dp, sv["x"], dx1, sp["g_mix"], kw["win"], "in_proj_bwd" + t)
    g["win"] = _wgrad_in(sv["hb"], dzm, duv, dp, "wgrad_in" + t)
    return dx, g


def _rope_inv_freq():
    half = ROPE // 2
    inv = 1.0 / (ROPE_THETA ** (jnp.arange(half, dtype=F32) / half))
    return jnp.concatenate([jnp.zeros((NOPE,), F32), inv, inv, jnp.zeros((HP - QK,), F32)]).reshape(1, HP)


def kernel(x, positions, g_mix_norm, w_in, g_q_lat, w_q_up, g_kv_lat, w_kv_up, g_q_head, g_k_head, g_sgu_v, w_spatial, b_spatial, w_pool, pool_scale, g_out_mla, g_out_sgu, g_out_pool, w_out, g_ffn_norm, w_gate, w_up, w_down, loss_target, m_g_mix_norm, m_w_in, m_g_q_lat, m_w_q_up, m_g_kv_lat, m_w_kv_up, m_g_q_head, m_g_k_head, m_g_sgu_v, m_w_spatial, m_b_spatial, m_w_pool, m_pool_scale, m_g_out_mla, m_g_out_sgu, m_g_out_pool, m_w_out, m_g_ffn_norm, m_w_gate, m_w_up, m_w_down, v_g_mix_norm, v_w_in, v_g_q_lat, v_w_q_up, v_g_kv_lat, v_w_kv_up, v_g_q_head, v_g_k_head, v_g_sgu_v, v_w_spatial, v_b_spatial, v_w_pool, v_pool_scale, v_g_out_mla, v_g_out_sgu, v_g_out_pool, v_w_out, v_g_ffn_norm, v_w_gate, v_w_up, v_w_down):
    given = dict(locals())
    p = {n: given[n] for n in ORDER}
    view = lambda pre, n: jnp.swapaxes(given[pre + n], 1, 2) if n in TRANSPOSED else given[pre + n]
    seq = x.shape[1]
    where = jnp.stack([2 * lax.axis_index("x") + lax.axis_index("y"), lax.axis_index("c")]).astype(jnp.int32)
    shards = lambda names: [view("", n)[l].astype(BF16) for l, n in names]
    zero11 = lambda token: token[:1, :1]

    names_0a = [(0, n) for n in EARLY_BIG]
    names_0b = [(0, n) for n in LATE_BIG]
    names_1 = [(1, n) for n, _, _ in BIG]
    got_0a = dict(zip(EARLY_BIG, _all_gather_chips(shards(names_0a), "all_gather_w0a")))
    started, issued = {}, got_0a["w_in"]
    for tag, names in (("w0b", names_0b), ("w1", names_1)):
        sh = shards(names)
        pairs = _gather_pairs([a.shape[0] // 2 for a in sh], [_row_align(a.dtype) for a in sh])
        lands = [jax.ShapeDtypeStruct((CHIPS,) + a.shape, a.dtype) for a in sh]
        started[tag] = (sh, pairs) + _split_start(sh, lands, 3 * len(sh), pairs, "gather_start_" + tag, issued)
        issued = started[tag][6]

    def arrived(tag, after):
        _, pairs, send, recv, srcs, lands, _ = started[tag]
        srcs, lands = _split_wait(send, recv, srcs, lands, after, pairs, "gather_wait_" + tag)
        return _gather_finish(srcs, lands, "gather_finish_" + tag)

    layer1 = {}

    def mix_weights(l, h):
        if l == 0:
            return got_0a
        layer1.update(zip([n for _, n in names_1], arrived("w1", h)))
        return layer1

    def late_weights(l, o):
        return arrived("w0b", o) if l == 0 else [layer1[n] for n in LATE_BIG]

    reducing, last = {}, {}

    def reduce_start(tag, arrs):
        lands = [jax.ShapeDtypeStruct((PEERS, a.shape[1] // 2, a.shape[2]), a.dtype) for a in arrs]
        reducing[tag] = _split_start(arrs, lands, PEERS * len(arrs), _scatter_pairs, "grad_scatter_start_" + tag, where)
        return zero11(reducing[tag][4])

    def reduce_finish(tag, after):
        send, recv, srcs, lands, _ = reducing[tag]
        srcs, lands = _split_wait(send, recv, srcs, lands, after, _scatter_pairs, "grad_scatter_wait_" + tag)
        return [_sum_own_and_landed(a, q, where, f"grad_sum_{tag}_{i}") for i, (a, q) in enumerate(zip(srcs, lands))]

    def ffn_hook(l, g):
        if l == 1:
            return jnp.zeros((1, 1), F32)
        return reduce_start("g0b", [g["wg"], g["wu"], g["wd"]])

    def out_hook(l, g):
        if l == 1:
            return where
        reduce_start("g0c", [g["wout"].reshape(CHIPS, D // CHIPS, D)])
        return reducing["g0c"][4]

    def layer_hook(l, big, small):
        last[l] = (big, small)
        if l == 1:
            return reduce_start("g1", [big[n] for n, _, _ in BIG])
        return None

    entry = zero11(started["w0b"][6]) + zero11(started["w1"][6])
    loss_part, dx = _step(x.reshape(seq, D), positions.reshape(seq, 1), loss_target.reshape(seq, D), p, entry,
                          mix_weights, late_weights, ffn_hook, out_hook, layer_hook)

    def adamw(n, g0, g1):
        flip = n in EARLY_BIG
        pick = lambda pre: jnp.swapaxes(given[pre + n], 1, 2) if flip else view(pre, n)
        w = pick("")
        three_d = (DEPTH, -1, w.shape[-1])
        g0, g1 = (g.T if flip else g for g in (g0, g1))
        res = _adamw(w.reshape(three_d), g0.reshape(three_d[1:]), g1.reshape(three_d[1:]),
                     pick("m_").reshape(three_d), pick("v_").reshape(three_d), "adamw_" + n)
        return [jnp.swapaxes(r.reshape(w.shape), 1, 2) if flip else r.reshape(w.shape) for r in res]

    names_rest = [(0, n) for n in EARLY_BIG]
    reduce_start("g0a", [last[0][0][n] for _, n in names_rest]
                 + [_pack_small_grads([last[l][1] for l in range(DEPTH)], loss_part)])
    token = reducing["g0a"][4]
    early = names_1 + [(0, n) for n in FFN_BIG] + [(0, "w_out")]
    landed = reduce_finish("g1", token) + reduce_finish("g0b", token) + reduce_finish("g0c", token)
    sums = dict(zip(early, _pair_join(landed, "grad_pair_join_early")))
    out = {n: adamw(n, sums[(0, n)], sums[(1, n)]) for n in FFN_BIG}
    late = names_rest + ["small"]
    sums.update(zip(late, _pair_join(reduce_finish("g0a", out["w_down"][1]), "grad_pair_join_late")))
    gsmall, loss = _unpack_small_grads(_all_gather_chips([sums["small"]], "all_gather_small_grads")[0])
    for n in ORDER:
        if n not in out:
            g = [sums[(l, n)] for l in range(DEPTH)] if (0, n) in sums else [gsmall[l][n] for l in range(DEPTH)]
            out[n] = adamw(n, *g)
    undo = lambda n, a: jnp.swapaxes(a, 1, 2) if n in TRANSPOSED else a
    return (loss, dx.reshape(x.shape), *[undo(n, out[n][i]) for i in range(4) for n in ORDER])


def _step(xs, pos, tgt, p, entry, mix_weights, late_weights, ffn_hook, out_hook, layer_hook):
    sps = [_small_operands(p, l) for l in range(DEPTH)]
    sps[0]["g_mix"] = sps[0]["g_mix"] + entry
    tabs = _rope_tables(pos, _rope_inv_freq())
    saved, h = [], xs
    for l in range(DEPTH):
        kw = _kernel_weights(mix_weights(l, h))
        h, sv = _layer_fwd(h, tabs, kw, functools.partial(late_weights, l), sps[l], l, tgt if l == DEPTH - 1 else None)
        saved.append(dict(sv, kw=kw))
    dy, lpart = h
    for l in reversed(range(DEPTH)):
        dy, g = _layer_bwd(dy, saved[l], tabs, saved[l]["kw"], sps[l], l, functools.partial(ffn_hook, l),
                           functools.partial(out_hook, l))
        zero = layer_hook(l, _big_grads(g), _small_grads(g))
        if zero is not None and l > 0:
            sps[l - 1]["g_ffn"] = sps[l - 1]["g_ffn"] + zero
    return 0.5 / D * jnp.sum(lpart), dy
```

```python
import functools
import math

import jax
import jax.numpy as jnp
from jax import lax
from jax.experimental import pallas as pl
from jax.experimental.pallas import tpu as pltpu

F32 = jnp.float32
BF16 = jnp.bfloat16
MESH = pl.DeviceIdType.MESH

D = 1024
HEADS = 4
QK = 96
NOPE = 64
ROPE = 32
VH = 128
HP = 128
QL = 256
KVL = 128
SGU = 256
POOL = 256
CHUNK = 128
HID = 2816
CHIPS = 4
SH = HID // CHIPS
IN_W = 1184
IN_P = 1280
EPS = 1e-6
ROPE_THETA = 10000.0
SCALE = 1.0 / math.sqrt(QK)
LOG2E = 1.4426950408889634
EXP2_C = SCALE * LOG2E
ATT_WIDE = 2
ATT_FWD_QUERIES = 2048
ATT_PIECE = 1024
ATT_ROWS = 256
ATT_KEYS = 1024
ATT_QUERIES = 2048
NEG = -1e30
HALO = 16

LR, B1, B2, ADAM_EPS, WD, STEP = 0.001, 0.9, 0.999, 1e-08, 0.01, 10

VMEM_LIMIT = 56 * 1024 * 1024
LANES = 128
TOKENS = 1024


def _cp(sem, vmem=None):
    return pltpu.CompilerParams(dimension_semantics=sem, vmem_limit_bytes=vmem)


def _res(shape):
    nd = len(shape)
    return pl.BlockSpec(shape, lambda *_: (0,) * nd, pipeline_mode=pl.Buffered(1))


def _acc(shape):
    nd = len(shape)
    return pl.BlockSpec(shape, lambda *_: (0,) * nd)


def _dot(a, b):
    return jnp.dot(a, b, preferred_element_type=F32)


def _dot_nt(a, b):
    return lax.dot_general(a, b, (((1,), (1,)), ((), ())), preferred_element_type=F32)


def _dot_tn(a, b):
    return lax.dot_general(a, b, (((0,), (0,)), ((), ())), preferred_element_type=F32)


def _rms(x, n):
    r = lax.rsqrt(jnp.sum(x * x, axis=-1, keepdims=True) * (1.0 / n) + EPS)
    return x * r, r


def _rms_bwd(xn, r, g, dy, n):
    dn = dy * g
    dx = r * (dn - xn * (jnp.sum(dn * xn, axis=-1, keepdims=True) * (1.0 / n)))
    return dx, jnp.sum(dy * xn, axis=0, keepdims=True)


def _accumulate(ref, val, first):
    @pl.when(first)
    def _():
        ref[...] = val

    @pl.when(jnp.logical_not(first))
    def _():
        ref[...] += val


def _accumulate0(ref, val, first):
    @pl.when(first)
    def _():
        ref[0] = val

    @pl.when(jnp.logical_not(first))
    def _():
        ref[0] += val


def _tile(s, t):
    return min(s, t)


def _row_tile(r, cap):
    if r <= cap:
        return r
    return max(t for t in range(8, cap + 1, 8) if r % t == 0)


def _rope_tables(pos, invf):
    s = pos.shape[0]
    tm = _tile(s, 1024)

    def body(pos_ref, invf_ref, c_ref, sa_ref, sb_ref):
        ang = pos_ref[...].astype(F32) * invf_ref[...]
        c, sn = jnp.cos(ang), jnp.sin(ang)
        lane = lax.broadcasted_iota(jnp.int32, ang.shape, 1)
        first = (lane >= NOPE) & (lane < NOPE + ROPE // 2)
        second = (lane >= NOPE + ROPE // 2) & (lane < QK)
        c_ref[...] = jnp.where(first | second, c, 1.0)
        sa_ref[...] = jnp.where(first, -sn, 0.0)
        sb_ref[...] = jnp.where(second, sn, 0.0)

    out = jax.ShapeDtypeStruct((s, HP), F32)
    return pl.pallas_call(
        body, name="rope_tables", grid=(s // tm,),
        in_specs=[pl.BlockSpec((tm, 1), lambda i: (i, 0)), _acc((1, HP))],
        out_specs=[pl.BlockSpec((tm, HP), lambda i: (i, 0))] * 3,
        out_shape=[out] * 3, compiler_params=_cp(("parallel",)),
    )(pos, invf)


def _rope(x, c, sa, sb):
    return x * c + pltpu.roll(x, HP - ROPE // 2, 1) * sa + pltpu.roll(x, ROPE // 2, 1) * sb


def _rope_t(d, c, sa, sb):
    return d * c + pltpu.roll(d * sa, ROPE // 2, 1) + pltpu.roll(d * sb, HP - ROPE // 2, 1)


def _in_proj_fwd(x, g, w, name):
    s = x.shape[0]
    tm = _tile(s, TOKENS)

    def body(x_ref, g_ref, w_ref, z_ref, h_ref):
        xn, _ = _rms(x_ref[...], D)
        h = (xn * g_ref[...]).astype(BF16)
        h_ref[...] = h
        z_ref[...] = _dot(h, w_ref[...])

    return pl.pallas_call(
        body, name=name, grid=(s // tm,),
        in_specs=[pl.BlockSpec((tm, D), lambda i: (i, 0)), _acc((1, D)), _res((D, IN_P))],
        out_specs=[pl.BlockSpec((tm, IN_P), lambda i: (i, 0)), pl.BlockSpec((tm, D), lambda i: (i, 0))],
        out_shape=[jax.ShapeDtypeStruct((s, IN_P), F32), jax.ShapeDtypeStruct((s, D), BF16)],
        compiler_params=_cp(("parallel",), VMEM_LIMIT),
    )(x, g, w)


def _mla_prep_fwd(z, tabs, gql, gkv, gq, gk, wq, wk, wv, name):
    s = z.shape[0]
    tm = _tile(s, TOKENS)

    def body(ql_ref, kv_ref, kr_ref, c_ref, sa_ref, sb_ref, gql_ref, gkv_ref, gq_ref, gk_ref,
             wq_ref, wk_ref, wv_ref, q_out, k_out, v_out):
        qn = (_rms(ql_ref[...], QL)[0] * gql_ref[...]).astype(BF16)
        kvn = (_rms(kv_ref[...], KVL)[0] * gkv_ref[...]).astype(BF16)
        qraw = _dot(qn, wq_ref[...])
        kraw = _dot(kvn, wk_ref[...])
        vraw = _dot(kvn, wv_ref[...])
        kr = kr_ref[...]
        c, sa, sb = c_ref[...], sa_ref[...], sb_ref[...]
        for h in range(HEADS):
            sl = slice(h * HP, (h + 1) * HP)
            xq = _rms(qraw[:, sl], QK)[0] * gq_ref[...]
            q_out[h] = (_rope(xq, c, sa, sb) * EXP2_C).astype(BF16)
            xk = _rms(kraw[:, sl] + kr, QK)[0] * gk_ref[...]
            k_out[h] = _rope(xk, c, sa, sb).astype(BF16)
            v_out[h] = vraw[:, sl].astype(BF16)

    row = lambda w, j: pl.BlockSpec((tm, w), lambda i: (i, j))
    hspec = pl.BlockSpec((HEADS, tm, HP), lambda i: (0, i, 0))
    hshape = jax.ShapeDtypeStruct((HEADS, s, HP), BF16)
    return pl.pallas_call(
        body, name=name, grid=(s // tm,),
        in_specs=[row(QL, 0), row(KVL, 2), row(HP, 3), row(HP, 0), row(HP, 0), row(HP, 0),
                  _acc((1, QL)), _acc((1, KVL)), _acc((1, HP)), _acc((1, HP)),
                  _acc((QL, HEADS * HP)), _acc((KVL, HEADS * HP)), _acc((KVL, HEADS * HP))],
        out_specs=[hspec] * 3, out_shape=[hshape] * 3,
        compiler_params=_cp(("parallel",)),
    )(z, z, z, *tabs, gql, gkv, gq, gk, wq, wk, wv)


def _causal_mask(s, row0):
    row = lax.broadcasted_iota(jnp.int32, s.shape, 0) + row0
    col = lax.broadcasted_iota(jnp.int32, s.shape, 1)
    return jnp.where(col <= row, s, NEG)


def _attn_fwd(q, k, v, name):
    s = q.shape[1]
    tq = _tile(s, ATT_FWD_QUERIES)
    rh = _tile(s, ATT_ROWS)
    kp = _tile(s, ATT_PIECE)
    wide = ATT_WIDE * kp if s % (ATT_WIDE * kp) == 0 else tq
    groups = tq // rh

    def body(q_ref, k_ref, v_ref, o_ref, lse_ref):
        i = pl.program_id(1)

        def blk(off, tk, carry, diagonal):
            width = lambda g, t: max(0, min(kp, (g + 1) * rh - t * kp)) if diagonal else kp
            rows = lambda t: pl.ds(pl.multiple_of(off + t * kp, kp), kp)
            score = lambda g, t: _dot_nt(q_ref[0, g * rh:(g + 1) * rh, :], k_ref[0, rows(t), :][:width(g, t)])
            live = lambda t: [g for g in range(groups) if width(g, t) > 0]
            state = list(carry)
            scs = {(g, 0): score(g, 0) for g in live(0)}
            for t in range(tk // kp):
                if (t + 1) * kp < tk:
                    scs.update({(g, t + 1): score(g, t + 1) for g in live(t + 1)})
                vt = v_ref[0, rows(t), :]
                for g in live(t):
                    m, l, acc = state[g]
                    sc = scs.pop((g, t))
                    if diagonal and (g + 1) * rh <= (t + 1) * kp:
                        sc = _causal_mask(sc, g * rh - t * kp)
                    m_new = jnp.maximum(m, jnp.max(sc, axis=-1, keepdims=True))
                    p = jnp.exp2(sc - m_new)
                    alpha = jnp.exp2(m - m_new)
                    l = alpha * l + jnp.sum(p, axis=-1, keepdims=True)
                    acc = alpha * acc + _dot(p.astype(BF16), vt[:width(g, t)])
                    state[g] = (m_new, l, acc)
            return tuple(state)

        one = (jnp.full((rh, 1), NEG, F32), jnp.zeros((rh, 1), F32), jnp.zeros((rh, VH), F32))
        nwide = (i * tq) // wide
        carry = lax.fori_loop(0, nwide, lambda j, c: blk(j * wide, wide, c, False), (one,) * groups)
        carry = lax.fori_loop(nwide * (wide // tq), i, lambda j, c: blk(j * tq, tq, c, False), carry)
        carry = blk(i * tq, tq, carry, True)
        for g, (m, l, acc) in enumerate(carry):
            o_ref[g * rh:(g + 1) * rh, :] = acc / l
            lse_ref[0, g * rh:(g + 1) * rh, :] = jnp.broadcast_to(m + jnp.log(l) * LOG2E, (rh, LANES))

    return pl.pallas_call(
        body, name=name, grid=(HEADS, s // tq),
        in_specs=[pl.BlockSpec((1, tq, HP), lambda h, i: (h, i, 0)),
                  pl.BlockSpec((1, s, HP), lambda h, i: (h, 0, 0)),
                  pl.BlockSpec((1, s, HP), lambda h, i: (h, 0, 0))],
        out_specs=[pl.BlockSpec((tq, VH), lambda h, i: (i, h)),
                   pl.BlockSpec((1, tq, LANES), lambda h, i: (h, i, 0))],
        out_shape=[jax.ShapeDtypeStruct((s, HEADS * VH), F32), jax.ShapeDtypeStruct((HEADS, s, LANES), F32)],
        compiler_params=_cp(("parallel", "arbitrary"), VMEM_LIMIT),
    )(q, k, v)


def _lane_group(shape, j):
    return (lax.broadcasted_iota(jnp.int32, shape, 1) + j * LANES) // (POOL // 4)


def _pool_win_fwd(z, name):
    s = z.shape[0]
    ch = _tile(s, 512)
    col0 = (IN_P - POOL) // LANES

    def body(p_ref, m_ref):
        j = pl.program_id(0)

        def chunk(r, _):
            off = pl.multiple_of(r * ch, ch)
            cur = p_ref[pl.ds(off, ch), :]
            hoff = pl.multiple_of(jnp.maximum(off - HALO, 0), 8)
            halo = jnp.where(r > 0, p_ref[pl.ds(hoff, HALO), :], 0.0)
            x = jnp.concatenate([halo, cur], axis=0)
            s2 = x + pltpu.roll(x, 1, 0)
            s4 = s2 + pltpu.roll(s2, 2, 0)
            s8 = s4 + pltpu.roll(s4, 4, 0)
            s16 = s8 + pltpu.roll(s8, 8, 0)
            grp = _lane_group((ch, LANES), j)
            sel = jnp.where(grp == 0, s2[HALO:], jnp.where(grp == 1, s4[HALO:], jnp.where(grp == 2, s8[HALO:], s16[HALO:])))
            t1 = (lax.broadcasted_iota(jnp.int32, (ch, LANES), 0) + off + 1).astype(F32)
            win = jnp.where(grp == 0, 2.0, jnp.where(grp == 1, 4.0, jnp.where(grp == 2, 8.0, 16.0)))
            m_ref[pl.ds(off, ch), :] = sel / jnp.minimum(t1, win) - cur
            return 0

        lax.fori_loop(0, s // ch, chunk, 0)

    return pl.pallas_call(
        body, name=name, grid=(POOL // LANES,),
        in_specs=[pl.BlockSpec((s, LANES), lambda j: (0, col0 + j))],
        out_specs=pl.BlockSpec((s, LANES), lambda j: (0, j)),
        out_shape=jax.ShapeDtypeStruct((s, POOL), F32),
        compiler_params=_cp(("parallel",), VMEM_LIMIT),
    )(z)


def _pool_win_bwd(dm, name):
    s = dm.shape[0]
    ch = _tile(s, 512)
    n = s // ch

    def body(dm_ref, dp_ref):
        j = pl.program_id(0)

        def chunk(r, _):
            off = pl.multiple_of(r * ch, ch)
            grp = _lane_group((ch + HALO, LANES), j)
            win = jnp.where(grp == 0, 2.0, jnp.where(grp == 1, 4.0, jnp.where(grp == 2, 8.0, 16.0)))
            cur = dm_ref[pl.ds(off, ch), :]
            hoff = pl.multiple_of(jnp.minimum(off + ch, s - HALO), 8)
            halo = jnp.where(r < n - 1, dm_ref[pl.ds(hoff, HALO), :], 0.0)
            x = jnp.concatenate([cur, halo], axis=0)
            t1 = (lax.broadcasted_iota(jnp.int32, (ch + HALO, LANES), 0) + off + 1).astype(F32)
            e = x / jnp.minimum(t1, win)
            tot = ch + HALO
            r2 = e + pltpu.roll(e, tot - 1, 0)
            r4 = r2 + pltpu.roll(r2, tot - 2, 0)
            r8 = r4 + pltpu.roll(r4, tot - 4, 0)
            r16 = r8 + pltpu.roll(r8, tot - 8, 0)
            g = grp[:ch]
            sel = jnp.where(g == 0, r2[:ch], jnp.where(g == 1, r4[:ch], jnp.where(g == 2, r8[:ch], r16[:ch])))
            dp_ref[pl.ds(off, ch), :] = (sel - cur).astype(BF16)
            return 0

        lax.fori_loop(0, n, chunk, 0)

    return pl.pallas_call(
        body, name=name, grid=(POOL // LANES,),
        in_specs=[pl.BlockSpec((s, LANES), lambda j: (0, j))],
        out_specs=pl.BlockSpec((s, LANES), lambda j: (0, j)),
        out_shape=jax.ShapeDtypeStruct((s, POOL), BF16),
        compiler_params=_cp(("parallel",), VMEM_LIMIT),
    )(dm)


def _head_mask(h):
    lane = lax.broadcasted_iota(jnp.int32, (CHUNK, SGU), 1)
    return (lane // (SGU // HEADS)) == h


def _tril(upper=False):
    row = lax.broadcasted_iota(jnp.int32, (CHUNK, CHUNK), 0)
    col = lax.broadcasted_iota(jnp.int32, (CHUNK, CHUNK), 1)
    return col >= row if upper else col <= row


def _sgu_gate(vn, wsp, bsp):
    out = []
    for cidx in range(vn.shape[0] // CHUNK):
        vc = vn[cidx * CHUNK:(cidx + 1) * CHUNK]
        zc = bsp
        for h in range(HEADS):
            zc = zc + jnp.where(_head_mask(h), _dot(wsp[h], vc), 0.0)
        out.append(zc)
    return jnp.concatenate(out, axis=0)


def _mix_out_fwd(o, z, m, x, wsp, bsp, wbd, psc, gsv, gout, wout, name):
    s = x.shape[0]
    tm = _tile(s, TOKENS)

    def body(o_ref, uv_ref, m_ref, x_ref, wsp_ref, bsp_ref, wbd_ref, psc_ref, gsv_ref, gout_ref, wout_ref,
             x1_ref, mix_ref):
        g = gout_ref[...]
        an = _rms(o_ref[...], HEADS * VH)[0] * g[:, :512]
        uv = uv_ref[...]
        u, v = uv[:, :SGU], uv[:, SGU:]
        vn = (_rms(v, SGU)[0] * gsv_ref[...]).astype(BF16)
        tri = _tril()
        wsp_m = [jnp.where(tri, wsp_ref[h], 0.0).astype(BF16) for h in range(HEADS)]
        gm = u * _sgu_gate(vn, wsp_m, bsp_ref[...])
        gn = _rms(gm, SGU)[0] * g[:, 512:768]
        po = _dot(m_ref[...].astype(BF16), wbd_ref[...]) * psc_ref[...]
        pn = _rms(po, POOL)[0] * g[:, 768:]
        mix = jnp.concatenate([an, gn, pn], axis=1).astype(BF16)
        mix_ref[...] = mix
        x1_ref[...] = x_ref[...] + _dot(mix, wout_ref[...])

    row = lambda w, j: pl.BlockSpec((tm, w), lambda i: (i, j))
    return pl.pallas_call(
        body, name=name, grid=(s // tm,),
        in_specs=[row(512, 0), row(512, 1), row(POOL, 0), row(D, 0),
                  _acc((HEADS, CHUNK, CHUNK)), _acc((CHUNK, SGU)), _acc((POOL, POOL)), _acc((1, POOL)),
                  _acc((1, SGU)), _acc((1, D)), _res((D, D))],
        out_specs=[row(D, 0), row(D, 0)],
        out_shape=[jax.ShapeDtypeStruct((s, D), F32), jax.ShapeDtypeStruct((s, D), BF16)],
        compiler_params=_cp(("parallel",), VMEM_LIMIT),
    )(o, z, m, x, wsp, bsp, wbd, psc, gsv, gout, wout)


def _ffn_fwd(x1, g, wg, wu, wd, tgt, name):
    s = x1.shape[0]
    tm = _tile(s, 256)
    last = tgt is not None

    def body(x_ref, g_ref, wg_ref, wu_ref, wd_ref, *rest):
        t_ref = rest[0] if last else None
        outs = rest[1:] if last else rest
        a_ref, b_ref, h_ref = outs[-3:]
        x = x_ref[...]
        h = (_rms(x, D)[0] * g_ref[...]).astype(BF16)
        h_ref[...] = h
        acc = jnp.zeros((tm, D), F32)
        for k in range(CHIPS):
            a = _dot_nt(h, wg_ref[k])
            b = _dot_nt(h, wu_ref[k])
            a_ref[k] = a
            b_ref[k] = b
            acc = acc + _dot((a * jax.nn.sigmoid(a) * b).astype(BF16), wd_ref[k])
        if not last:
            outs[0][...] = x + acc
            return
        dy_ref, l_ref = outs[:2]
        e = (x + acc) - t_ref[...]
        dy_ref[...] = e * (1.0 / D)
        sq = jnp.sum(e * e, axis=0, keepdims=True)
        part = sq[:, :LANES]
        for c in range(1, D // LANES):
            part = part + sq[:, c * LANES:(c + 1) * LANES]
        _accumulate(l_ref, part, pl.program_id(0) == 0)

    row = lambda w: pl.BlockSpec((tm, w), lambda i: (i, 0))
    hrow = pl.BlockSpec((CHIPS, tm, SH), lambda i: (0, i, 0))
    hshape = jax.ShapeDtypeStruct((CHIPS, s, SH), F32)
    tail_specs = [hrow, hrow, row(D)]
    tail_shapes = [hshape, hshape, jax.ShapeDtypeStruct((s, D), BF16)]
    head_specs = [row(D), _acc((1, LANES))] if last else [row(D)]
    head_shapes = [jax.ShapeDtypeStruct((s, D), F32)] + ([jax.ShapeDtypeStruct((1, LANES), F32)] if last else [])
    res = pl.pallas_call(
        body, name=name, grid=(s // tm,),
        in_specs=[row(D), _acc((1, D)), _res((CHIPS, SH, D)), _res((CHIPS, SH, D)), _res((CHIPS, SH, D))]
        + ([row(D)] if last else []),
        out_specs=head_specs + tail_specs, out_shape=head_shapes + tail_shapes,
        compiler_params=_cp(("arbitrary",), VMEM_LIMIT),
    )(x1, g, wg, wu, wd, *([tgt] if last else []))
    return (tuple(res[:2]) if last else res[0]), res[-3], res[-2], res[-1]


def _wgrad(a, b, name):
    s, k = a.shape
    n = b.shape[1]
    half = lambda v: v if v <= 1408 else v // 2
    kb, nb, tt = half(k), half(n), _tile(s, 2048)

    def body(a_ref, b_ref, o_ref):
        _accumulate(o_ref, _dot_tn(a_ref[...].astype(BF16), b_ref[...].astype(BF16)), pl.program_id(2) == 0)

    return pl.pallas_call(
        body, name=name, grid=(k // kb, n // nb, s // tt),
        in_specs=[pl.BlockSpec((tt, kb), lambda i, j, t: (t, i)), pl.BlockSpec((tt, nb), lambda i, j, t: (t, j))],
        out_specs=pl.BlockSpec((kb, nb), lambda i, j, t: (i, j)),
        out_shape=jax.ShapeDtypeStruct((k, n), F32),
        compiler_params=_cp(("parallel", "parallel", "arbitrary"), VMEM_LIMIT),
    )(a, b)


def _wgrad_in(h, dzm, duv, dp, name):
    s = h.shape[0]
    tt = _tile(s, 2048)

    def body(h_ref, a_ref, b_ref, c_ref, o_ref):
        hv = h_ref[...]
        val = jnp.concatenate([_dot_tn(hv, a_ref[...]), _dot_tn(hv, b_ref[...]), _dot_tn(hv, c_ref[...])], axis=1)
        _accumulate(o_ref, val, pl.program_id(0) == 0)

    row = lambda w: pl.BlockSpec((tt, w), lambda t: (t, 0))
    return pl.pallas_call(
        body, name=name, grid=(s // tt,), in_specs=[row(D), row(512), row(512), row(POOL)], out_specs=_acc((D, IN_P)),
        out_shape=jax.ShapeDtypeStruct((D, IN_P), F32), compiler_params=_cp(("arbitrary",), VMEM_LIMIT),
    )(h, dzm, duv, dp)


def _wgrad_rows(a, b, name):
    s, n = a.shape[1:]
    nn = b.shape[1]
    tt = _tile(s, 4096 if b.dtype == BF16 else 2048)

    def body(a_ref, b_ref, o_ref):
        _accumulate0(o_ref, _dot_tn(a_ref[0].astype(BF16), b_ref[...].astype(BF16)), pl.program_id(1) == 0)

    return pl.pallas_call(
        body, name=name, grid=(CHIPS, s // tt),
        in_specs=[pl.BlockSpec((1, tt, n), lambda c, t: (c, t, 0)), pl.BlockSpec((tt, nn), lambda c, t: (t, 0))],
        out_specs=pl.BlockSpec((1, n, nn), lambda c, t: (c, 0, 0)),
        out_shape=jax.ShapeDtypeStruct((CHIPS, n, nn), F32),
        compiler_params=_cp(("parallel", "arbitrary"), VMEM_LIMIT),
    )(a, b)


def _ffn_bwd(dx2, x1, a, b, g, wg, wu, wd, name):
    s = x1.shape[0]
    tm = _tile(s, 256)

    def body(dx2_ref, x_ref, a_ref, b_ref, g_ref, wg_ref, wu_ref, wd_ref,
             dx1_ref, hid_ref, da_ref, db_ref, dyb_ref, dg_ref):
        dx2 = dx2_ref[...]
        dyb = dx2.astype(BF16)
        dyb_ref[...] = dyb
        dh = jnp.zeros((tm, D), F32)
        ahead = _dot_nt(dyb, wd_ref[0])
        for k in range(CHIPS):
            av, bv = a_ref[k], b_ref[k]
            dhid = ahead
            if k + 1 < CHIPS:
                ahead = _dot_nt(dyb, wd_ref[k + 1])
            sig = jax.nn.sigmoid(av)
            sa = av * sig
            hid_ref[k] = (sa * bv).astype(BF16)
            dbv = (dhid * sa).astype(BF16)
            dav = (dhid * bv * (sig * (1.0 + av * (1.0 - sig)))).astype(BF16)
            db_ref[k] = dbv
            da_ref[k] = dav
            dh = dh + _dot(dav, wg_ref[k]) + _dot(dbv, wu_ref[k])
        xn, r = _rms(x_ref[...], D)
        dxr, dg = _rms_bwd(xn, r, g_ref[...], dh, D)
        dx1_ref[...] = dx2 + dxr
        _accumulate(dg_ref, dg, pl.program_id(0) == 0)

    row = lambda w: pl.BlockSpec((tm, w), lambda i: (i, 0))
    hrow = pl.BlockSpec((CHIPS, tm, SH), lambda i: (0, i, 0))
    hid = jax.ShapeDtypeStruct((CHIPS, s, SH), BF16)
    return pl.pallas_call(
        body, name=name, grid=(s // tm,),
        in_specs=[row(D), row(D), hrow, hrow, _acc((1, D)), _res((CHIPS, SH, D)), _res((CHIPS, SH, D)),
                  _res((CHIPS, SH, D))],
        out_specs=[row(D), hrow, hrow, hrow, row(D), _acc((1, D))],
        out_shape=[jax.ShapeDtypeStruct((s, D), F32), hid, hid, hid, jax.ShapeDtypeStruct((s, D), BF16),
                   jax.ShapeDtypeStruct((1, D), F32)],
        compiler_params=_cp(("arbitrary",), VMEM_LIMIT),
    )(dx2, x1, a, b, g, wg, wu, wd)


def _mix_out_bwd(dx1, o, z, m, wsp, bsp, wbd, psc, gsv, gout, wout, name):
    s = dx1.shape[0]
    tm = _tile(s, TOKENS)

    def body(dx1_ref, o_ref, uv_ref, m_ref, wsp_ref, bsp_ref, wbd_ref, psc_ref, gsv_ref, gout_ref, wout_ref,
             do_ref, dl_ref, duv_ref, dm_ref, dgo_ref, dgsv_ref, dpsc_ref, dwsp_ref, dbsp_ref, dwbd_ref):
        first = pl.program_id(0) == 0
        g = gout_ref[...]
        dmix = _dot_nt(dx1_ref[...].astype(BF16), wout_ref[...])
        o = o_ref[...]
        on, ro = _rms(o, HEADS * VH)
        do, dga = _rms_bwd(on, ro, g[:, :512], dmix[:, :512], HEADS * VH)
        for h in range(HEADS):
            sl = slice(h * VH, (h + 1) * VH)
            do_ref[h] = do[:, sl].astype(BF16)
            dl_ref[h] = jnp.broadcast_to(jnp.sum(do[:, sl] * o[:, sl], axis=-1, keepdims=True), (tm, LANES))
        uv = uv_ref[...]
        u, v = uv[:, :SGU], uv[:, SGU:]
        vx, rv = _rms(v, SGU)
        vn = (vx * gsv_ref[...]).astype(BF16)
        tri = _tril()
        wsp_m = [jnp.where(tri, wsp_ref[h], 0.0).astype(BF16) for h in range(HEADS)]
        zc = _sgu_gate(vn, wsp_m, bsp_ref[...])
        gm = u * zc
        gmn, rg = _rms(gm, SGU)
        dgm, dgg = _rms_bwd(gmn, rg, g[:, 512:768], dmix[:, 512:768], SGU)
        du = dgm * zc
        dzc = dgm * u
        dvn_parts = []
        dbsp = jnp.zeros((CHUNK, SGU), F32)
        dwsp = [jnp.zeros((CHUNK, CHUNK), F32) for _ in range(HEADS)]
        for cidx in range(tm // CHUNK):
            rs = slice(cidx * CHUNK, (cidx + 1) * CHUNK)
            dzc_c = dzc[rs]
            dbsp = dbsp + dzc_c
            dzb = dzc_c.astype(BF16)
            vc = vn[rs]
            dvn_c = jnp.zeros((CHUNK, SGU), F32)
            for h in range(HEADS):
                hm = _head_mask(h)
                dvn_c = dvn_c + jnp.where(hm, _dot_tn(wsp_m[h], dzb), 0.0)
                dwsp[h] = dwsp[h] + _dot_nt(jnp.where(hm, dzc_c, 0.0).astype(BF16), vc)
            dvn_parts.append(dvn_c)
        dvn = jnp.concatenate(dvn_parts, axis=0)
        dv, dgsv = _rms_bwd(vx, rv, gsv_ref[...], dvn, SGU)
        duv_ref[...] = jnp.concatenate([du, dv], axis=1).astype(BF16)
        mb = m_ref[...].astype(BF16)
        pw = _dot(mb, wbd_ref[...])
        po = pw * psc_ref[...]
        pon, rp = _rms(po, POOL)
        dpo, dgp = _rms_bwd(pon, rp, g[:, 768:], dmix[:, 768:], POOL)
        dpw = (dpo * psc_ref[...]).astype(BF16)
        dm_ref[...] = _dot_nt(dpw, wbd_ref[...])
        _accumulate(dgo_ref, jnp.concatenate([dga, dgg, dgp], axis=1), first)
        _accumulate(dgsv_ref, dgsv, first)
        _accumulate(dpsc_ref, jnp.sum(dpo * pw, axis=0, keepdims=True), first)
        _accumulate(dbsp_ref, dbsp, first)
        _accumulate(dwbd_ref, _dot_tn(mb, dpw), first)
        for h in range(HEADS):
            val = jnp.where(tri, dwsp[h], 0.0)

            @pl.when(first)
            def _(val=val, h=h):
                dwsp_ref[h] = val

            @pl.when(jnp.logical_not(first))
            def _(val=val, h=h):
                dwsp_ref[h] += val

    row = lambda w, j: pl.BlockSpec((tm, w), lambda i: (i, j))
    hspec = pl.BlockSpec((HEADS, tm, HP), lambda i: (0, i, 0))
    return pl.pallas_call(
        body, name=name, grid=(s // tm,),
        in_specs=[row(D, 0), row(512, 0), row(512, 1), row(POOL, 0),
                  _acc((HEADS, CHUNK, CHUNK)), _acc((CHUNK, SGU)),
                  _acc((POOL, POOL)), _acc((1, POOL)), _acc((1, SGU)), _acc((1, D)), _res((D, D))],
        out_specs=[hspec, hspec, row(512, 0), row(POOL, 0), _acc((1, D)), _acc((1, SGU)), _acc((1, POOL)),
                   _acc((HEADS, CHUNK, CHUNK)), _acc((CHUNK, SGU)), _acc((POOL, POOL))],
        out_shape=[jax.ShapeDtypeStruct((HEADS, s, HP), BF16), jax.ShapeDtypeStruct((HEADS, s, LANES), F32),
                   jax.ShapeDtypeStruct((s, 512), BF16), jax.ShapeDtypeStruct((s, POOL), F32),
                   jax.ShapeDtypeStruct((1, D), F32), jax.ShapeDtypeStruct((1, SGU), F32),
                   jax.ShapeDtypeStruct((1, POOL), F32), jax.ShapeDtypeStruct((HEADS, CHUNK, CHUNK), F32),
                   jax.ShapeDtypeStruct((CHUNK, SGU), F32), jax.ShapeDtypeStruct((POOL, POOL), F32)],
        compiler_params=_cp(("arbitrary",), VMEM_LIMIT),
    )(dx1, o, z, m, wsp, bsp, wbd, psc, gsv, gout, wout)


def _attn_bwd(q, k, v, do, lse, delta, after, name):
    s = q.shape[1]
    rh = _tile(s, ATT_ROWS)
    tk = _tile(s, ATT_KEYS)
    nk = s // tk
    wide = ATT_QUERIES if s % ATT_QUERIES == 0 else tk
    pieces = tk // rh

    def body(q_ref, k_ref, v_ref, do_ref, lse_ref, dl_ref, after_ref, dq_ref, dk_ref, dv_ref):
        del after_ref
        j = pl.program_id(1)

        @pl.when(j == 0)
        def _():
            dq_ref[...] = jnp.zeros_like(dq_ref)

        kj, vj = k_ref[0], v_ref[0]

        def blk(start, rows, dks, dvs, diagonal):
            dks, dvs = list(dks), list(dvs)
            offs = [pl.multiple_of(start + g * rh, rh) for g in range(rows // rh)]
            keys = [(g + 1) * rh if diagonal else tk for g in range(rows // rh)]
            qs = [q_ref[0, pl.ds(off, rh), :] for off in offs]
            dos = [do_ref[0, pl.ds(off, rh), :] for off in offs]
            scs = [_dot_nt(qi, kj[:n]) for qi, n in zip(qs, keys)]
            dps = [_dot_nt(doi, vj[:n]) for doi, n in zip(dos, keys)]
            for g, off in enumerate(offs):
                lse_i = lse_ref[0, pl.ds(off, rh), :][:, :1]
                dl_i = dl_ref[0, pl.ds(off, rh), :][:, :1]
                sc = _causal_mask(scs[g], g * rh) if diagonal else scs[g]
                p = jnp.exp2(sc - lse_i)
                ds = (p * (dps[g] - dl_i)).astype(BF16)
                cv = _dot_tn(p.astype(BF16), dos[g])
                ck = _dot_tn(ds, qs[g])
                for t in range(keys[g] // rh):
                    dvs[t] = dvs[t] + cv[t * rh:(t + 1) * rh]
                    dks[t] = dks[t] + ck[t * rh:(t + 1) * rh]
                dq_ref[0, pl.ds(off, rh), :] += _dot(ds, kj[:keys[g]]) * SCALE
            return tuple(dks), tuple(dvs)

        per = wide // tk
        zero = (jnp.zeros((rh, HP), F32),) * pieces
        acc = blk(j * tk, tk, zero, zero, True)
        first_wide = (j + per) // per
        acc = lax.fori_loop(j + 1, jnp.minimum(first_wide * per, nk), lambda i, c: blk(i * tk, tk, *c, False), acc)
        dks, dvs = lax.fori_loop(first_wide, nk // per, lambda i, c: blk(i * wide, wide, *c, False), acc)
        dk_ref[0] = jnp.concatenate(dks, axis=0) * (SCALE / EXP2_C)
        dv_ref[0] = jnp.concatenate(dvs, axis=0)

    full = lambda: pl.BlockSpec((1, s, HP), lambda h, j: (h, 0, 0))
    blk_spec = lambda: pl.BlockSpec((1, tk, HP), lambda h, j: (h, j, 0))
    out = jax.ShapeDtypeStruct((HEADS, s, HP), F32)
    return pl.pallas_call(
        body, name=name, grid=(HEADS, s // tk),
        in_specs=[full(), blk_spec(), blk_spec(), full(), full(), full(), ANY],
        out_specs=[full(), blk_spec(), blk_spec()], out_shape=[out] * 3,
        compiler_params=_cp(("parallel", "arbitrary"), VMEM_LIMIT),
    )(q, k, v, do, lse, delta, after)


def _mla_prep_bwd(dq, dk, dv, z, tabs, gql, gkv, gq, gk, wq, wk, wv, name):
    s = z.shape[0]
    tm = _tile(s, TOKENS)

    def body(dq_ref, dk_ref, dv_ref, ql_ref, kv_ref, kr_ref, c_ref, sa_ref, sb_ref, gql_ref, gkv_ref, gq_ref, gk_ref,
             wq_ref, wk_ref, wv_ref,
             dz_ref, dwq_ref, dwk_ref, dwv_ref, dgql_ref, dgkv_ref, dgq_ref, dgk_ref, dqr_ref, dkr_ref, dvr_ref):
        first = pl.program_id(0) == 0
        qx, rq = _rms(ql_ref[...], QL)
        qn = (qx * gql_ref[...]).astype(BF16)
        kx, rk = _rms(kv_ref[...], KVL)
        kvn = (kx * gkv_ref[...]).astype(BF16)
        qraw = _dot(qn, wq_ref[...])
        kraw = _dot(kvn, wk_ref[...])
        kr = kr_ref[...]
        c, sa, sb = c_ref[...], sa_ref[...], sb_ref[...]
        lane = lax.broadcasted_iota(jnp.int32, (tm, HP), 1)
        rope_lanes = (lane >= NOPE) & (lane < QK)
        dkrope = jnp.zeros((tm, HP), F32)
        dgq = jnp.zeros((1, HP), F32)
        dgk = jnp.zeros((1, HP), F32)
        for h in range(HEADS):
            sl = slice(h * HP, (h + 1) * HP)
            xn, r = _rms(qraw[:, sl], QK)
            dx, dg = _rms_bwd(xn, r, gq_ref[...], _rope_t(dq_ref[h], c, sa, sb), QK)
            dqr_ref[:, sl] = dx.astype(BF16)
            dgq = dgq + dg
            xn, r = _rms(kraw[:, sl] + kr, QK)
            dx, dg = _rms_bwd(xn, r, gk_ref[...], _rope_t(dk_ref[h], c, sa, sb), QK)
            dkr_ref[:, sl] = dx.astype(BF16)
            dgk = dgk + dg
            dkrope = dkrope + jnp.where(rope_lanes, dx, 0.0)
            dvr_ref[:, sl] = dv_ref[h].astype(BF16)
        dqn = _dot_nt(dqr_ref[...], wq_ref[...])
        dql, dgql = _rms_bwd(qx, rq, gql_ref[...], dqn, QL)
        dkvn = _dot_nt(dkr_ref[...], wk_ref[...]) + _dot_nt(dvr_ref[...], wv_ref[...])
        dkv, dgkv = _rms_bwd(kx, rk, gkv_ref[...], dkvn, KVL)
        dz_ref[...] = jnp.concatenate([dql, dkv, dkrope], axis=1).astype(BF16)
        _accumulate(dwq_ref, _dot_tn(qn, dqr_ref[...]), first)
        _accumulate(dwk_ref, _dot_tn(kvn, dkr_ref[...]), first)
        _accumulate(dwv_ref, _dot_tn(kvn, dvr_ref[...]), first)
        _accumulate(dgql_ref, dgql, first)
        _accumulate(dgkv_ref, dgkv, first)
        _accumulate(dgq_ref, dgq, first)
        _accumulate(dgk_ref, dgk, first)

    row = lambda w, j: pl.BlockSpec((tm, w), lambda i: (i, j))
    hspec = pl.BlockSpec((HEADS, tm, HP), lambda i: (0, i, 0))
    acc = lambda r, c: (_acc((r, c)), jax.ShapeDtypeStruct((r, c), F32))
    outs = [(row(512, 0), jax.ShapeDtypeStruct((s, 512), BF16)), acc(QL, HEADS * HP), acc(KVL, HEADS * HP),
            acc(KVL, HEADS * HP), acc(1, QL), acc(1, KVL), acc(1, HP), acc(1, HP)]
    return pl.pallas_call(
        body, name=name, grid=(s // tm,),
        in_specs=[hspec, hspec, hspec, row(QL, 0), row(KVL, 2), row(HP, 3), row(HP, 0), row(HP, 0), row(HP, 0),
                  _acc((1, QL)), _acc((1, KVL)), _acc((1, HP)), _acc((1, HP)),
                  _acc((QL, HEADS * HP)), _acc((KVL, HEADS * HP)), _acc((KVL, HEADS * HP))],
        out_specs=[o[0] for o in outs], out_shape=[o[1] for o in outs],
        scratch_shapes=[pltpu.VMEM((tm, HEADS * HP), BF16)] * 3,
        compiler_params=_cp(("arbitrary",), VMEM_LIMIT),
    )(dq, dk, dv, z, z, z, *tabs, gql, gkv, gq, gk, wq, wk, wv)


def _in_proj_bwd(dzm, duv, dp, x, dx1, g, win, name):
    s = x.shape[0]
    tm = _tile(s, TOKENS // 2)

    def body(dzm_ref, duv_ref, dp_ref, x_ref, dx1_ref, g_ref, w_ref, dx_ref, dg_ref):
        groups = [slice(r0, r0 + tm // 2) for r0 in (0, tm // 2)]
        dhs = [_dot_nt(dzm_ref[rs, :], w_ref[:, 0:512]) + _dot_nt(duv_ref[rs, :], w_ref[:, 512:1024])
               + _dot_nt(dp_ref[rs, :], w_ref[:, 1024:IN_P]) for rs in groups]
        dg = jnp.zeros((1, D), F32)
        for rs, dh in zip(groups, dhs):
            xn, r = _rms(x_ref[rs, :], D)
            dxr, dgr = _rms_bwd(xn, r, g_ref[...], dh, D)
            dx_ref[rs, :] = dx1_ref[rs, :] + dxr
            dg = dg + dgr
        _accumulate(dg_ref, dg, pl.program_id(0) == 0)

    row = lambda w: pl.BlockSpec((tm, w), lambda i: (i, 0))
    return pl.pallas_call(
        body, name=name, grid=(s // tm,),
        in_specs=[row(512), row(512), row(POOL), row(D), row(D), _acc((1, D)), _res((D, IN_P))],
        out_specs=[row(D), _acc((1, D))],
        out_shape=[jax.ShapeDtypeStruct((s, D), F32), jax.ShapeDtypeStruct((1, D), F32)],
        compiler_params=_cp(("arbitrary",), VMEM_LIMIT),
    )(dzm, duv, dp, x, dx1, g, win)


def _adamw(w, g0, g1, m, v, name):
    _, r, c = w.shape
    tr = _row_tile(r, 512)
    c1 = 1.0 - B1 ** STEP
    c2 = 1.0 - B2 ** STEP

    def body(w_ref, g0_ref, g1_ref, m_ref, v_ref, g_ref, d_ref, nm_ref, nv_ref):
        gv = jnp.where(pl.program_id(0) == 0, g0_ref[...], g1_ref[...])
        g_ref[0] = gv
        nm = B1 * m_ref[0] + (1.0 - B1) * gv
        nv = B2 * v_ref[0] + (1.0 - B2) * (gv * gv)
        nm_ref[0] = nm
        nv_ref[0] = nv
        d_ref[0] = -LR * ((nm / c1) / (jnp.sqrt(nv / c2) + ADAM_EPS) + WD * w_ref[0])

    spec = pl.BlockSpec((1, tr, c), lambda l, i: (l, i, 0))
    out = jax.ShapeDtypeStruct((DEPTH, r, c), F32)
    return pl.pallas_call(
        body, name=name, grid=(DEPTH, r // tr),
        in_specs=[spec, pl.BlockSpec((tr, c), lambda l, i: (i * (1 - l), 0)), pl.BlockSpec((tr, c), lambda l, i: (i * l, 0)),
                  spec, spec],
        out_specs=[spec] * 4, out_shape=[out] * 4, compiler_params=_cp(("parallel", "parallel")),
    )(w, g0, g1, m, v)


ANY = pl.BlockSpec(memory_space=pl.ANY)


def _place():
    x, y, c = lax.axis_index("x"), lax.axis_index("y"), lax.axis_index("c")
    chips = [(1 - x, y), (x, 1 - y), (1 - x, 1 - y)]
    return x, y, c, chips


def _half_rows(ref, lead, hh, half, align):
    rows = pl.ds(pl.multiple_of(hh * half, align), half)
    return ref.at[rows, :] if lead is None else ref.at[lead, rows, :]


def _row_align(dtype):
    return 16 if dtype == BF16 else 8


def _sems(n):
    return [pltpu.SemaphoreType.DMA((n,)), pltpu.SemaphoreType.DMA((n,)), pltpu.SemaphoreType.DMA((n,))]


def _comm_call(body, ins, out_shapes, nsems, name):
    return pl.pallas_call(
        body, name=name, in_specs=[ANY] * len(ins), out_specs=[ANY] * len(out_shapes), out_shape=out_shapes,
        scratch_shapes=_sems(nsems), compiler_params=pltpu.CompilerParams(has_side_effects=True),
    )(*ins)


def _all_gather_chips(shards, name):
    n = len(shards)
    halves = [a.shape[0] // 2 for a in shards]
    aligns = [_row_align(a.dtype) for a in shards]
    assert all(h % al == 0 for h, al in zip(halves, aligns))

    def body(*refs):
        ins, outs, (send_sems, recv_sems, _) = refs[:n], refs[n:2 * n], refs[2 * n:]
        x, y, c, chips = _place()
        me = 2 * x + y
        sibling = (x, y, 1 - c)

        def copy(sem, src, dst, to):
            return pltpu.make_async_remote_copy(src_ref=src, dst_ref=dst, send_sem=send_sems.at[sem],
                                                recv_sem=recv_sems.at[sem], device_id=to, device_id_type=MESH)

        first, passed = [], []
        for a in range(n):
            my_half = _half_rows(ins[a], None, c, halves[a], aligns[a])
            for j, (cx, cy) in enumerate(chips):
                cp = copy(6 * a + j, my_half, _half_rows(outs[a], me, c, halves[a], aligns[a]), (cx, cy, c))
                cp.start()
                first.append(cp)
        for a in range(n):
            for j, (cx, cy) in enumerate(chips):
                landed = _half_rows(outs[a], 2 * cx + cy, c, halves[a], aligns[a])
                copy(6 * a + j, landed, landed, (cx, cy, c)).wait_recv()
                fwd = copy(6 * a + 3 + j, landed, landed, sibling)
                fwd.start()
                passed.append(fwd)
        for a in range(n):
            for j, (cx, cy) in enumerate(chips):
                other = _half_rows(outs[a], 2 * cx + cy, 1 - c, halves[a], aligns[a])
                copy(6 * a + 3 + j, other, other, sibling).wait_recv()
        for cp in first + passed:
            cp.wait_send()

    lands = _comm_call(body, shards, [jax.ShapeDtypeStruct((CHIPS,) + a.shape, a.dtype) for a in shards], 6 * n, name)
    return _with_own(lands, shards)


def _with_own(lands, shards):
    me = 2 * lax.axis_index("x") + lax.axis_index("y")
    return [lax.dynamic_update_slice(g, a[None], (me, 0, 0)) for g, a in zip(lands, shards)]


def _pair_join(arrs, name):
    n = len(arrs)
    halves = [a.shape[0] // 2 for a in arrs]

    def body(*refs):
        outs, (send_sems, recv_sems, _) = refs[n:2 * n], refs[2 * n:]
        x, y, c, _ = _place()
        cps = []
        for a in range(n):
            mine = _half_rows(outs[a], None, c, halves[a], 8)
            cp = pltpu.make_async_remote_copy(src_ref=mine, dst_ref=mine, send_sem=send_sems.at[a], recv_sem=recv_sems.at[a],
                                              device_id=(x, y, 1 - c), device_id_type=MESH)
            cp.start()
            cps.append(cp)
        for cp in cps:
            cp.wait()

    return pl.pallas_call(
        body, name=name, in_specs=[ANY] * n, out_specs=[ANY] * n,
        out_shape=[jax.ShapeDtypeStruct(a.shape, a.dtype) for a in arrs],
        input_output_aliases={i: i for i in range(n)}, scratch_shapes=_sems(n),
        compiler_params=pltpu.CompilerParams(has_side_effects=True),
    )(*arrs)


HBM = pl.BlockSpec(memory_space=pltpu.HBM)
SEM = pl.BlockSpec(memory_space=pltpu.SEMAPHORE)
DATAFLOW = pltpu.SideEffectType.DATAFLOW_SIDE_EFFECTING


def _remote_copies(pairs, ins, lands, send_sems, recv_sems):
    return [pltpu.make_async_remote_copy(src_ref=src, dst_ref=dst, send_sem=send_sems.at[i], recv_sem=recv_sems.at[i],
                                         device_id=to, device_id_type=MESH)
            for i, (src, dst, to) in enumerate(pairs(ins, lands))]


def _split_start(srcs, land_shapes, ncopies, pairs, name, after):
    n, m = len(srcs), len(land_shapes)

    def body(*refs):
        ins, lands = refs[:n], refs[n:n + m]
        send_sems, recv_sems, token = refs[n + m + 1], refs[n + m + 2], refs[-1]
        for cp in _remote_copies(pairs, ins, lands, send_sems, recv_sems):
            cp.start()
        token[...] = jnp.zeros_like(token)

    hbm = lambda a: pltpu.with_memory_space_constraint(a, pltpu.HBM)
    lands = [hbm(lax.empty(s.shape, s.dtype)) for s in land_shapes]
    thru = [pltpu.HBM(a.shape, a.dtype) for a in list(srcs) + lands]
    out = pl.pallas_call(
        body, name=name,
        out_shape=(pltpu.SemaphoreType.DMA((ncopies,)), pltpu.SemaphoreType.DMA((ncopies,)), *thru,
                   jax.ShapeDtypeStruct((8, LANES), F32)),
        in_specs=[HBM] * (n + m) + [ANY], out_specs=(SEM, SEM, *[HBM] * (n + m), pl.BlockSpec(memory_space=pltpu.VMEM)),
        input_output_aliases={i: 2 + i for i in range(n + m)},
        compiler_params=pltpu.CompilerParams(has_side_effects=DATAFLOW),
    )(*[hbm(a) for a in srcs], *lands, after)
    return out[0], out[1], list(out[2:2 + n]), list(out[2 + n:2 + n + m]), out[-1]


def _split_wait(send_sems, recv_sems, srcs, lands, after, pairs, name):
    n, m = len(srcs), len(lands)

    def body(*refs):
        ins, lands_ = refs[:n], refs[n:n + m]
        for cp in _remote_copies(pairs, ins, lands_, refs[n + m], refs[n + m + 1]):
            cp.wait_send()
            cp.wait_recv()

    out = pl.pallas_call(
        body, name=name, out_shape=tuple(pltpu.HBM(a.shape, a.dtype) for a in list(srcs) + list(lands)),
        in_specs=[HBM] * (n + m) + [SEM, SEM, ANY], out_specs=tuple([HBM] * (n + m)),
        input_output_aliases={i: i for i in range(n + m)},
        compiler_params=pltpu.CompilerParams(has_side_effects=DATAFLOW),
    )(*srcs, *lands, send_sems, recv_sems, after)
    return list(out[:n]), list(out[n:])


def _gather_pairs(halves, aligns):
    def pairs(ins, lands):
        x, y, c, chips = _place()
        me = 2 * x + y
        return [(_half_rows(ins[a], None, c, halves[a], aligns[a]), _half_rows(lands[a], me, c, halves[a], aligns[a]),
                 (cx, cy, c)) for a in range(len(ins)) for cx, cy in chips]
    return pairs


PEERS = 7


def _scatter_pairs(ins, lands):
    x, y, c, chips = _place()
    to = [(cx, cy, c) for cx, cy in chips] + [(cx, cy, 1 - c) for cx, cy in chips] + [(x, y, 1 - c)]
    out = []
    for a in range(len(ins)):
        half = ins[a].shape[1] // 2
        for i, (tx, ty, tc) in enumerate(to):
            out.append((_half_rows(ins[a], 2 * tx + ty, tc, half, 8), lands[a].at[i], (tx, ty, tc)))
    return out


def _gather_finish(shards, lands, name):
    n = len(shards)
    halves = [a.shape[0] // 2 for a in shards]
    aligns = [_row_align(a.dtype) for a in shards]

    def body(*refs):
        outs, (send_sems, recv_sems, _) = refs[n:2 * n], refs[2 * n:]
        x, y, c, chips = _place()
        passed = []
        for a in range(n):
            for j, (cx, cy) in enumerate(chips):
                landed = _half_rows(outs[a], 2 * cx + cy, c, halves[a], aligns[a])
                cp = pltpu.make_async_remote_copy(src_ref=landed, dst_ref=landed, send_sem=send_sems.at[3 * a + j],
                                                  recv_sem=recv_sems.at[3 * a + j], device_id=(x, y, 1 - c),
                                                  device_id_type=MESH)
                cp.start()
                passed.append(cp)
        for a in range(n):
            for j, (cx, cy) in enumerate(chips):
                other = _half_rows(outs[a], 2 * cx + cy, 1 - c, halves[a], aligns[a])
                pltpu.make_async_remote_copy(src_ref=other, dst_ref=other, send_sem=send_sems.at[3 * a + j],
                                             recv_sem=recv_sems.at[3 * a + j], device_id=(x, y, 1 - c),
                                             device_id_type=MESH).wait_recv()
        for cp in passed:
            cp.wait_send()

    lands = pl.pallas_call(
        body, name=name, in_specs=[ANY] * n, out_specs=[ANY] * n,
        out_shape=[jax.ShapeDtypeStruct(a.shape, a.dtype) for a in lands],
        input_output_aliases={i: i for i in range(n)}, scratch_shapes=_sems(3 * n),
        compiler_params=pltpu.CompilerParams(has_side_effects=True),
    )(*lands)
    return _with_own(lands, shards)


def _sum_own_and_landed(own, landed, where, name):
    _, half, cols = landed.shape
    tr = _row_tile(half, 128)
    nt = half // tr

    grid_spec = pltpu.PrefetchScalarGridSpec(
        num_scalar_prefetch=1, grid=(nt,),
        in_specs=[pl.BlockSpec((1, tr, cols), lambda r, w: (w[0], w[1] * nt + r, 0)),
                  pl.BlockSpec((PEERS, tr, cols), lambda r, w: (0, r, 0))],
        out_specs=pl.BlockSpec((tr, cols), lambda r, w: (w[1] * nt + r, 0)))

    def body(w_ref, p_ref, q_ref, o_ref):
        acc = p_ref[0]
        for i in range(PEERS):
            acc = acc + q_ref[i]
        o_ref[...] = acc

    return pl.pallas_call(
        body, name=name, grid_spec=grid_spec, out_shape=jax.ShapeDtypeStruct((2 * half, cols), own.dtype),
        compiler_params=_cp(("parallel",)),
    )(where, own, landed)


BIG = [("w_in", (D, IN_W), 1), ("w_q_up", (QL, HEADS * QK), 1), ("w_kv_up", (KVL, HEADS * (NOPE + VH)), 1),
       ("w_out", (D, D), 0), ("w_gate", (D, HID), 1), ("w_up", (D, HID), 1), ("w_down", (HID, D), 0)]
SMALL = [("g_mix_norm", (D,)), ("g_q_lat", (QL,)), ("g_kv_lat", (KVL,)), ("g_q_head", (QK,)), ("g_k_head", (QK,)),
         ("g_sgu_v", (SGU,)), ("w_spatial", (HEADS, CHUNK, CHUNK)), ("b_spatial", (HEADS, CHUNK)),
         ("w_pool", (4, 64, 64)), ("pool_scale", (POOL,)), ("g_out_mla", (512,)), ("g_out_sgu", (SGU,)),
         ("g_out_pool", (POOL,)), ("g_ffn_norm", (D,))]
ORDER = ["g_mix_norm", "w_in", "g_q_lat", "w_q_up", "g_kv_lat", "w_kv_up", "g_q_head", "g_k_head", "g_sgu_v",
         "w_spatial", "b_spatial", "w_pool", "pool_scale", "g_out_mla", "g_out_sgu", "g_out_pool", "w_out",
         "g_ffn_norm", "w_gate", "w_up", "w_down"]
EARLY_BIG = ["w_in", "w_q_up", "w_kv_up"]
FFN_BIG = ["w_gate", "w_up", "w_down"]
LATE_BIG = ["w_out"] + FFN_BIG
DEPTH = 2
COLS = 1024
SMALL_N = sum(math.prod(s) for _, s in SMALL) * DEPTH
assert SMALL_N % CHIPS == 0
SMALL_ROWS = -(-(SMALL_N // CHIPS + 1) // (16 * COLS)) * 16


def _unsplit_cols(g):
    return g.transpose(1, 0, 2).reshape(g.shape[1], CHIPS * g.shape[2])


def _split_cols(full):
    r, c = full.shape
    return full.reshape(r, CHIPS, c // CHIPS).transpose(1, 0, 2)


def _kernel_weights(g):
    win = _unsplit_cols(g["w_in"])
    zeros = lambda r, c: jnp.zeros((r, c), BF16)
    o2, o3, o4 = QL + KVL, QL + KVL + ROPE, QL + KVL + ROPE + 2 * SGU
    win_p = jnp.concatenate([win[:, :o2], zeros(D, NOPE), win[:, o2:o3], zeros(D, HP - QK), win[:, o3:o4], win[:, o4:]], axis=1)
    wq = _unsplit_cols(g["w_q_up"]).reshape(QL, HEADS, QK)
    wq_p = jnp.pad(wq, ((0, 0), (0, 0), (0, HP - QK))).reshape(QL, HEADS * HP)
    wkv = _unsplit_cols(g["w_kv_up"]).reshape(KVL, HEADS, NOPE + VH)
    wk_p = jnp.pad(wkv[:, :, :NOPE], ((0, 0), (0, 0), (0, HP - NOPE))).reshape(KVL, HEADS * HP)
    wv_p = wkv[:, :, NOPE:].reshape(KVL, HEADS * VH)
    return dict(win=win_p, wq=wq_p, wk=wk_p, wv=wv_p)


def _small_operands(p, l):
    row = lambda v: v.reshape(1, -1)
    pad = lambda v: jnp.pad(v, (0, HP - QK)).reshape(1, HP)
    wpool = p["w_pool"][l]
    wbd = jnp.zeros((POOL, POOL), F32)
    for g in range(4):
        wbd = lax.dynamic_update_slice(wbd, wpool[g], (g * 64, g * 64))
    return dict(
        g_mix=row(p["g_mix_norm"][l]), gql=row(p["g_q_lat"][l]), gkv=row(p["g_kv_lat"][l]),
        gq=pad(p["g_q_head"][l]), gk=pad(p["g_k_head"][l]), gsv=row(p["g_sgu_v"][l]),
        wsp=p["w_spatial"][l], bsp=jnp.repeat(p["b_spatial"][l].T, SGU // HEADS, axis=1),
        wbd=wbd.astype(BF16), psc=row(p["pool_scale"][l]),
        gout=jnp.concatenate([p["g_out_mla"][l], p["g_out_sgu"][l], p["g_out_pool"][l]]).reshape(1, D),
        g_ffn=row(p["g_ffn_norm"][l]))


def _big_grads(g):
    dwin = g["win"]
    o2 = QL + KVL
    gin = jnp.concatenate([dwin[:, :o2], dwin[:, o2 + NOPE:o2 + NOPE + ROPE], dwin[:, 512:]], axis=1)
    gq = g["wq"].reshape(QL, HEADS, HP)[:, :, :QK].reshape(QL, HEADS * QK)
    gk = g["wk"].reshape(KVL, HEADS, HP)[:, :, :NOPE]
    gv = g["wv"].reshape(KVL, HEADS, VH)
    gkv = jnp.concatenate([gk, gv], axis=2).reshape(KVL, HEADS * (NOPE + VH))
    return {"w_in": _split_cols(gin), "w_q_up": _split_cols(gq), "w_kv_up": _split_cols(gkv),
            "w_out": g["wout"].reshape(CHIPS, D // CHIPS, D), "w_gate": g["wg"], "w_up": g["wu"], "w_down": g["wd"]}


TRANSPOSED = ("w_gate", "w_up")


def _small_grads(g):
    go = g["gout"].reshape(-1)
    return {"g_mix_norm": g["g_mix"].reshape(-1), "g_q_lat": g["gql"].reshape(-1), "g_kv_lat": g["gkv"].reshape(-1),
            "g_q_head": g["gq"].reshape(-1)[:QK], "g_k_head": g["gk"].reshape(-1)[:QK], "g_sgu_v": g["gsv"].reshape(-1),
            "w_spatial": g["wsp"], "b_spatial": g["bsp"].reshape(CHUNK, HEADS, SGU // HEADS).sum(-1).T,
            "w_pool": jnp.stack([g["wbd"][i * 64:(i + 1) * 64, i * 64:(i + 1) * 64] for i in range(4)]),
            "pool_scale": g["psc"].reshape(-1), "g_out_mla": go[:512], "g_out_sgu": go[512:768],
            "g_out_pool": go[768:], "g_ffn_norm": g["g_ffn"].reshape(-1)}


def _pack_small_grads(small, loss):
    sm = jnp.concatenate([small[l][n].reshape(-1) for l in range(DEPTH) for n, _ in SMALL]).reshape(CHIPS, SMALL_N // CHIPS)
    sm = jnp.pad(sm, ((0, 0), (0, SMALL_ROWS * COLS - SMALL_N // CHIPS)))
    return sm.at[0, SMALL_N // CHIPS].set(loss).reshape(CHIPS, SMALL_ROWS, COLS)


def _unpack_small_grads(gathered):
    rows = gathered.reshape(CHIPS, SMALL_ROWS * COLS)
    loss = rows[0, SMALL_N // CHIPS]
    flat = rows[:, :SMALL_N // CHIPS].reshape(-1)
    out, off = [], 0
    for _ in range(DEPTH):
        layer = {}
        for n, shape in SMALL:
            k = math.prod(shape)
            layer[n] = flat[off:off + k].reshape(shape)
            off += k
        out.append(layer)
    return out, loss


def _layer_fwd(x, tabs, kw, late_weights, sp, l, tgt):
    t = f"_l{l}"
    z, hb = _in_proj_fwd(x, sp["g_mix"], kw["win"], "in_proj_fwd" + t)
    q, k, v = _mla_prep_fwd(z, tabs, sp["gql"], sp["gkv"], sp["gq"], sp["gk"], kw["wq"], kw["wk"], kw["wv"],
                            "mla_prep_fwd" + t)
    o, lse = _attn_fwd(q, k, v, "attn_fwd" + t)
    m = _pool_win_fwd(z, "pool_win_fwd" + t)
    wout, wg, wu, wd = late_weights(o)
    wout = wout.reshape(D, D)
    x1, mix = _mix_out_fwd(o, z, m, x, sp["wsp"], sp["bsp"], sp["wbd"], sp["psc"], sp["gsv"], sp["gout"], wout,
                           "mix_out_fwd" + t)
    x2, a, b, h2 = _ffn_fwd(x1, sp["g_ffn"], wg, wu, wd, tgt, "ffn_fwd" + t)
    saved = dict(x=x, z=z, hb=hb, q=q, k=k, v=v, o=o, lse=lse, m=m, x1=x1, mix=mix, a=a, b=b, h2=h2, wg=wg, wu=wu, wd=wd,
                 wout=wout)
    return x2, saved


def _layer_bwd(dx2, sv, tabs, kw, sp, l, ffn_hook, out_hook):
    t = f"_l{l}"
    g = {}
    dx1, hid, da, db, dyb, g["g_ffn"] = _ffn_bwd(dx2, sv["x1"], sv["a"], sv["b"], sp["g_ffn"], sv["wg"], sv["wu"],
                                                 sv["wd"], "ffn_bwd" + t)
    g["wd"] = _wgrad_rows(hid, dyb, "wgrad_down" + t)
    g["wg"] = _wgrad_rows(da, sv["h2"], "wgrad_gate" + t)
    g["wu"] = _wgrad_rows(db, sv["h2"], "wgrad_up" + t)
    gout = sp["gout"] + ffn_hook(g)
    do, delta, duv, dm, g["gout"], g["gsv"], g["psc"], g["wsp"], g["bsp"], g["wbd"] = _mix_out_bwd(
        dx1, sv["o"], sv["z"], sv["m"], sp["wsp"], sp["bsp"], sp["wbd"], sp["psc"], sp["gsv"], gout, sv["wout"],
        "mix_out_bwd" + t)
    g["wout"] = _wgrad(sv["mix"], dx1, "wgrad_out" + t)
    dp = _pool_win_bwd(dm, "pool_win_bwd" + t)
    dq, dk, dv = _attn_bwd(sv["q"], sv["k"], sv["v"], do, sv["lse"], delta, out_hook(g), "attn_bwd" + t)
    dzm, g["wq"], g["wk"], g["wv"], g["gql"], g["gkv"], g["gq"], g["gk"] = _mla_prep_bwd(
        dq, dk, dv, sv["z"], tabs, sp["gql"], sp["gkv"], sp["gq"], sp["gk"], kw["wq"], kw["wk"], kw["wv"],
        "mla_prep_bwd" + t)
    dx, g["g_mix"] = _in_proj_bwd(dzm, duv, dp, sv["x"], dx1, sp["g_mix"], kw["win"], "in_proj_bwd" + t)
    g["win"] = _wgrad_in(sv["hb"], dzm, duv, dp, "wgrad_in" + t)
    return dx, g


def _rope_inv_freq():
    half = ROPE // 2
    inv = 1.0 / (ROPE_THETA ** (jnp.arange(half, dtype=F32) / half))
    return jnp.concatenate([jnp.zeros((NOPE,), F32), inv, inv, jnp.zeros((HP - QK,), F32)]).reshape(1, HP)


def kernel(x, positions, g_mix_norm, w_in, g_q_lat, w_q_up, g_kv_lat, w_kv_up, g_q_head, g_k_head, g_sgu_v, w_spatial, b_spatial, w_pool, pool_scale, g_out_mla, g_out_sgu, g_out_pool, w_out, g_ffn_norm, w_gate, w_up, w_down, loss_target, m_g_mix_norm, m_w_in, m_g_q_lat, m_w_q_up, m_g_kv_lat, m_w_kv_up, m_g_q_head, m_g_k_head, m_g_sgu_v, m_w_spatial, m_b_spatial, m_w_pool, m_pool_scale, m_g_out_mla, m_g_out_sgu, m_g_out_pool, m_w_out, m_g_ffn_norm, m_w_gate, m_w_up, m_w_down, v_g_mix_norm, v_w_in, v_g_q_lat, v_w_q_up, v_g_kv_lat, v_w_kv_up, v_g_q_head, v_g_k_head, v_g_sgu_v, v_w_spatial, v_b_spatial, v_w_pool, v_pool_scale, v_g_out_mla, v_g_out_sgu, v_g_out_pool, v_w_out, v_g_ffn_norm, v_w_gate, v_w_up, v_w_down):
    given = dict(locals())
    p = {n: given[n] for n in ORDER}
    view = lambda pre, n: jnp.swapaxes(given[pre + n], 1, 2) if n in TRANSPOSED else given[pre + n]
    seq = x.shape[1]
    where = jnp.stack([2 * lax.axis_index("x") + lax.axis_index("y"), lax.axis_index("c")]).astype(jnp.int32)
    shards = lambda names: [view("", n)[l].astype(BF16) for l, n in names]
    zero11 = lambda token: token[:1, :1]

    names_0a = [(0, n) for n in EARLY_BIG]
    names_0b = [(0, n) for n in LATE_BIG]
    names_1 = [(1, n) for n, _, _ in BIG]
    got_0a = dict(zip(EARLY_BIG, _all_gather_chips(shards(names_0a), "all_gather_w0a")))
    started, issued = {}, got_0a["w_in"]
    for tag, names in (("w0b", names_0b), ("w1", names_1)):
        sh = shards(names)
        pairs = _gather_pairs([a.shape[0] // 2 for a in sh], [_row_align(a.dtype) for a in sh])
        lands = [jax.ShapeDtypeStruct((CHIPS,) + a.shape, a.dtype) for a in sh]
        started[tag] = (sh, pairs) + _split_start(sh, lands, 3 * len(sh), pairs, "gather_start_" + tag, issued)
        issued = started[tag][6]

    def arrived(tag, after):
        _, pairs, send, recv, srcs, lands, _ = started[tag]
        srcs, lands = _split_wait(send, recv, srcs, lands, after, pairs, "gather_wait_" + tag)
        return _gather_finish(srcs, lands, "gather_finish_" + tag)

    layer1 = {}

    def mix_weights(l, h):
        if l == 0:
            return got_0a
        layer1.update(zip([n for _, n in names_1], arrived("w1", h)))
        return layer1

    def late_weights(l, o):
        return arrived("w0b", o) if l == 0 else [layer1[n] for n in LATE_BIG]

    reducing, last = {}, {}

    def reduce_start(tag, arrs):
        lands = [jax.ShapeDtypeStruct((PEERS, a.shape[1] // 2, a.shape[2]), a.dtype) for a in arrs]
        reducing[tag] = _split_start(arrs, lands, PEERS * len(arrs), _scatter_pairs, "grad_scatter_start_" + tag, where)
        return zero11(reducing[tag][4])

    def reduce_finish(tag, after):
        send, recv, srcs, lands, _ = reducing[tag]
        srcs, lands = _split_wait(send, recv, srcs, lands, after, _scatter_pairs, "grad_scatter_wait_" + tag)
        return [_sum_own_and_landed(a, q, where, f"grad_sum_{tag}_{i}") for i, (a, q) in enumerate(zip(srcs, lands))]

    def ffn_hook(l, g):
        if l == 1:
            return jnp.zeros((1, 1), F32)
        return reduce_start("g0b", [g["wg"], g["wu"], g["wd"]])

    def out_hook(l, g):
        if l == 1:
            return where
        reduce_start("g0c", [g["wout"].reshape(CHIPS, D // CHIPS, D)])
        return reducing["g0c"][4]

    def layer_hook(l, big, small):
        last[l] = (big, small)
        if l == 1:
            return reduce_start("g1", [big[n] for n, _, _ in BIG])
        return None

    entry = zero11(started["w0b"][6]) + zero11(started["w1"][6])
    loss_part, dx = _step(x.reshape(seq, D), positions.reshape(seq, 1), loss_target.reshape(seq, D), p, entry,
                          mix_weights, late_weights, ffn_hook, out_hook, layer_hook)

    def adamw(n, g0, g1):
        flip = n in EARLY_BIG
        pick = lambda pre: jnp.swapaxes(given[pre + n], 1, 2) if flip else view(pre, n)
        w = pick("")
        three_d = (DEPTH, -1, w.shape[-1])
        g0, g1 = (g.T if flip else g for g in (g0, g1))
        res = _adamw(w.reshape(three_d), g0.reshape(three_d[1:]), g1.reshape(three_d[1:]),
                     pick("m_").reshape(three_d), pick("v_").reshape(three_d), "adamw_" + n)
        return [jnp.swapaxes(r.reshape(w.shape), 1, 2) if flip else r.reshape(w.shape) for r in res]

    names_rest = [(0, n) for n in EARLY_BIG]
    reduce_start("g0a", [last[0][0][n] for _, n in names_rest]
                 + [_pack_small_grads([last[l][1] for l in range(DEPTH)], loss_part)])
    token = reducing["g0a"][4]
    early = names_1 + [(0, n) for n in FFN_BIG] + [(0, "w_out")]
    landed = reduce_finish("g1", token) + reduce_finish("g0b", token) + reduce_finish("g0c", token)
    sums = dict(zip(early, _pair_join(landed, "grad_pair_join_early")))
    out = {n: adamw(n, sums[(0, n)], sums[(1, n)]) for n in FFN_BIG}
    late = names_rest + ["small"]
    sums.update(zip(late, _pair_join(reduce_finish("g0a", out["w_down"][1]), "grad_pair_join_late")))
    gsmall, loss = _unpack_small_grads(_all_gather_chips([sums["small"]], "all_gather_small_grads")[0])
    for n in ORDER:
        if n not in out:
            g = [sums[(l, n)] for l in range(DEPTH)] if (0, n) in sums else [gsmall[l][n] for l in range(DEPTH)]
            out[n] = adamw(n, *g)
    undo = lambda n, a: jnp.swapaxes(a, 1, 2) if n in TRANSPOSED else a
    return (loss, dx.reshape(x.shape), *[undo(n, out[n][i]) for i in range(4) for n in ORDER])


def _step(xs, pos, tgt, p, entry, mix_weights, late_weights, ffn_hook, out_hook, layer_hook):
    sps = [_small_operands(p, l) for l in range(DEPTH)]
    sps[0]["g_mix"] = sps[0]["g_mix"] + entry
    tabs = _rope_tables(pos, _rope_inv_freq())
    saved, h = [], xs
    for l in range(DEPTH):
        kw = _kernel_weights(mix_weights(l, h))
        h, sv = _layer_fwd(h, tabs, kw, functools.partial(late_weights, l), sps[l], l, tgt if l == DEPTH - 1 else None)
        saved.append(dict(sv, kw=kw))
    dy, lpart = h
    for l in reversed(range(DEPTH)):
        dy, g = _layer_bwd(dy, saved[l], tabs, saved[l]["kw"], sps[l], l, functools.partial(ffn_hook, l),
                           functools.partial(out_hook, l))
        zero = layer_hook(l, _big_grads(g), _small_grads(g))
        if zero is not None and l > 0:
            sps[l - 1]["g_ffn"] = sps[l - 1]["g_ffn"] + zero
    return 0.5 / D * jnp.sum(lpart), dy
```

```python
import functools
import math

import jax
import jax.numpy as jnp
from jax import lax
from jax.experimental import pallas as pl
from jax.experimental.pallas import tpu as pltpu

F32 = jnp.float32
BF16 = jnp.bfloat16
MESH = pl.DeviceIdType.MESH

D = 1024
HEADS = 4
QK = 96
NOPE = 64
ROPE = 32
VH = 128
HP = 128
QL = 256
KVL = 128
SGU = 256
POOL = 256
CHUNK = 128
HID = 2816
CHIPS = 4
SH = HID // CHIPS
IN_W = 1184
IN_P = 1280
EPS = 1e-6
ROPE_THETA = 10000.0
SCALE = 1.0 / math.sqrt(QK)
LOG2E = 1.4426950408889634
EXP2_C = SCALE * LOG2E
ATT_WIDE = 2
ATT_FWD_QUERIES = 2048
ATT_PIECE = 1024
ATT_ROWS = 256
ATT_KEYS = 1024
ATT_QUERIES = 2048
NEG = -1e30
HALO = 16

LR, B1, B2, ADAM_EPS, WD, STEP = 0.001, 0.9, 0.999, 1e-08, 0.01, 10

VMEM_LIMIT = 56 * 1024 * 1024
LANES = 128
TOKENS = 1024


def _cp(sem, vmem=None):
    return pltpu.CompilerParams(dimension_semantics=sem, vmem_limit_bytes=vmem)


def _res(shape):
    nd = len(shape)
    return pl.BlockSpec(shape, lambda *_: (0,) * nd, pipeline_mode=pl.Buffered(1))


def _acc(shape):
    nd = len(shape)
    return pl.BlockSpec(shape, lambda *_: (0,) * nd)


def _dot(a, b):
    return jnp.dot(a, b, preferred_element_type=F32)


def _dot_nt(a, b):
    return lax.dot_general(a, b, (((1,), (1,)), ((), ())), preferred_element_type=F32)


def _dot_tn(a, b):
    return lax.dot_general(a, b, (((0,), (0,)), ((), ())), preferred_element_type=F32)


def _rms(x, n):
    r = lax.rsqrt(jnp.sum(x * x, axis=-1, keepdims=True) * (1.0 / n) + EPS)
    return x * r, r


def _rms_bwd(xn, r, g, dy, n):
    dn = dy * g
    dx = r * (dn - xn * (jnp.sum(dn * xn, axis=-1, keepdims=True) * (1.0 / n)))
    return dx, jnp.sum(dy * xn, axis=0, keepdims=True)


def _accumulate(ref, val, first):
    @pl.when(first)
    def _():
        ref[...] = val

    @pl.when(jnp.logical_not(first))
    def _():
        ref[...] += val


def _accumulate0(ref, val, first):
    @pl.when(first)
    def _():
        ref[0] = val

    @pl.when(jnp.logical_not(first))
    def _():
        ref[0] += val


def _tile(s, t):
    return min(s, t)


def _row_tile(r, cap):
    if r <= cap:
        return r
    return max(t for t in range(8, cap + 1, 8) if r % t == 0)


def _rope_tables(pos, invf):
    s = pos.shape[0]
    tm = _tile(s, 1024)

    def body(pos_ref, invf_ref, c_ref, sa_ref, sb_ref):
        ang = pos_ref[...].astype(F32) * invf_ref[...]
        c, sn = jnp.cos(ang), jnp.sin(ang)
        lane = lax.broadcasted_iota(jnp.int32, ang.shape, 1)
        first = (lane >= NOPE) & (lane < NOPE + ROPE // 2)
        second = (lane >= NOPE + ROPE // 2) & (lane < QK)
        c_ref[...] = jnp.where(first | second, c, 1.0)
        sa_ref[...] = jnp.where(first, -sn, 0.0)
        sb_ref[...] = jnp.where(second, sn, 0.0)

    out = jax.ShapeDtypeStruct((s, HP), F32)
    return pl.pallas_call(
        body, name="rope_tables", grid=(s // tm,),
        in_specs=[pl.BlockSpec((tm, 1), lambda i: (i, 0)), _acc((1, HP))],
        out_specs=[pl.BlockSpec((tm, HP), lambda i: (i, 0))] * 3,
        out_shape=[out] * 3, compiler_params=_cp(("parallel",)),
    )(pos, invf)


def _rope(x, c, sa, sb):
    return x * c + pltpu.roll(x, HP - ROPE // 2, 1) * sa + pltpu.roll(x, ROPE // 2, 1) * sb


def _rope_t(d, c, sa, sb):
    return d * c + pltpu.roll(d * sa, ROPE // 2, 1) + pltpu.roll(d * sb, HP - ROPE // 2, 1)


def _in_proj_fwd(x, g, w, name):
    s = x.shape[0]
    tm = _tile(s, TOKENS)

    def body(x_ref, g_ref, w_ref, z_ref, h_ref):
        xn, _ = _rms(x_ref[...], D)
        h = (xn * g_ref[...]).astype(BF16)
        h_ref[...] = h
        z_ref[...] = _dot(h, w_ref[...])

    return pl.pallas_call(
        body, name=name, grid=(s // tm,),
        in_specs=[pl.BlockSpec((tm, D), lambda i: (i, 0)), _acc((1, D)), _res((D, IN_P))],
        out_specs=[pl.BlockSpec((tm, IN_P), lambda i: (i, 0)), pl.BlockSpec((tm, D), lambda i: (i, 0))],
        out_shape=[jax.ShapeDtypeStruct((s, IN_P), F32), jax.ShapeDtypeStruct((s, D), BF16)],
        compiler_params=_cp(("parallel",), VMEM_LIMIT),
    )(x, g, w)


def _mla_prep_fwd(z, tabs, gql, gkv, gq, gk, wq, wk, wv, name):
    s = z.shape[0]
    tm = _tile(s, TOKENS)

    def body(ql_ref, kv_ref, kr_ref, c_ref, sa_ref, sb_ref, gql_ref, gkv_ref, gq_ref, gk_ref,
             wq_ref, wk_ref, wv_ref, q_out, k_out, v_out):
        qn = (_rms(ql_ref[...], QL)[0] * gql_ref[...]).astype(BF16)
        kvn = (_rms(kv_ref[...], KVL)[0] * gkv_ref[...]).astype(BF16)
        qraw = _dot(qn, wq_ref[...])
        kraw = _dot(kvn, wk_ref[...])
        vraw = _dot(kvn, wv_ref[...])
        kr = kr_ref[...]
        c, sa, sb = c_ref[...], sa_ref[...], sb_ref[...]
        for h in range(HEADS):
            sl = slice(h * HP, (h + 1) * HP)
            xq = _rms(qraw[:, sl], QK)[0] * gq_ref[...]
            q_out[h] = (_rope(xq, c, sa, sb) * EXP2_C).astype(BF16)
            xk = _rms(kraw[:, sl] + kr, QK)[0] * gk_ref[...]
            k_out[h] = _rope(xk, c, sa, sb).astype(BF16)
            v_out[h] = vraw[:, sl].astype(BF16)

    row = lambda w, j: pl.BlockSpec((tm, w), lambda i: (i, j))
    hspec = pl.BlockSpec((HEADS, tm, HP), lambda i: (0, i, 0))
    hshape = jax.ShapeDtypeStruct((HEADS, s, HP), BF16)
    return pl.pallas_call(
        body, name=name, grid=(s // tm,),
        in_specs=[row(QL, 0), row(KVL, 2), row(HP, 3), row(HP, 0), row(HP, 0), row(HP, 0),
                  _acc((1, QL)), _acc((1, KVL)), _acc((1, HP)), _acc((1, HP)),
                  _acc((QL, HEADS * HP)), _acc((KVL, HEADS * HP)), _acc((KVL, HEADS * HP))],
        out_specs=[hspec] * 3, out_shape=[hshape] * 3,
        compiler_params=_cp(("parallel",)),
    )(z, z, z, *tabs, gql, gkv, gq, gk, wq, wk, wv)


def _causal_mask(s, row0):
    row = lax.broadcasted_iota(jnp.int32, s.shape, 0) + row0
    col = lax.broadcasted_iota(jnp.int32, s.shape, 1)
    return jnp.where(col <= row, s, NEG)


def _attn_fwd(q, k, v, name):
    s = q.shape[1]
    tq = _tile(s, ATT_FWD_QUERIES)
    rh = _tile(s, ATT_ROWS)
    kp = _tile(s, ATT_PIECE)
    wide = ATT_WIDE * kp if s % (ATT_WIDE * kp) == 0 else tq
    groups = tq // rh

    def body(q_ref, k_ref, v_ref, o_ref, lse_ref):
        i = pl.program_id(1)

        def blk(off, tk, carry, diagonal):
            width = lambda g, t: max(0, min(kp, (g + 1) * rh - t * kp)) if diagonal else kp
            rows = lambda t: pl.ds(pl.multiple_of(off + t * kp, kp), kp)
            score = lambda g, t: _dot_nt(q_ref[0, g * rh:(g + 1) * rh, :], k_ref[0, rows(t), :][:width(g, t)])
            live = lambda t: [g for g in range(groups) if width(g, t) > 0]
            state = list(carry)
            scs = {(g, 0): score(g, 0) for g in live(0)}
            for t in range(tk // kp):
                if (t + 1) * kp < tk:
                    scs.update({(g, t + 1): score(g, t + 1) for g in live(t + 1)})
                vt = v_ref[0, rows(t), :]
                for g in live(t):
                    m, l, acc = state[g]
                    sc = scs.pop((g, t))
                    if diagonal and (g + 1) * rh <= (t + 1) * kp:
                        sc = _causal_mask(sc, g * rh - t * kp)
                    m_new = jnp.maximum(m, jnp.max(sc, axis=-1, keepdims=True))
                    p = jnp.exp2(sc - m_new)
                    alpha = jnp.exp2(m - m_new)
                    l = alpha * l + jnp.sum(p, axis=-1, keepdims=True)
                    acc = alpha * acc + _dot(p.astype(BF16), vt[:width(g, t)])
                    state[g] = (m_new, l, acc)
            return tuple(state)

        one = (jnp.full((rh, 1), NEG, F32), jnp.zeros((rh, 1), F32), jnp.zeros((rh, VH), F32))
        nwide = (i * tq) // wide
        carry = lax.fori_loop(0, nwide, lambda j, c: blk(j * wide, wide, c, False), (one,) * groups)
        carry = lax.fori_loop(nwide * (wide // tq), i, lambda j, c: blk(j * tq, tq, c, False), carry)
        carry = blk(i * tq, tq, carry, True)
        for g, (m, l, acc) in enumerate(carry):
            o_ref[g * rh:(g + 1) * rh, :] = acc / l
            lse_ref[0, g * rh:(g + 1) * rh, :] = jnp.broadcast_to(m + jnp.log(l) * LOG2E, (rh, LANES))

    return pl.pallas_call(
        body, name=name, grid=(HEADS, s // tq),
        in_specs=[pl.BlockSpec((1, tq, HP), lambda h, i: (h, i, 0)),
                  pl.BlockSpec((1, s, HP), lambda h, i: (h, 0, 0)),
                  pl.BlockSpec((1, s, HP), lambda h, i: (h, 0, 0))],
        out_specs=[pl.BlockSpec((tq, VH), lambda h, i: (i, h)),
                   pl.BlockSpec((1, tq, LANES), lambda h, i: (h, i, 0))],
        out_shape=[jax.ShapeDtypeStruct((s, HEADS * VH), F32), jax.ShapeDtypeStruct((HEADS, s, LANES), F32)],
        compiler_params=_cp(("parallel", "arbitrary"), VMEM_LIMIT),
    )(q, k, v)


def _lane_group(shape, j):
    return (lax.broadcasted_iota(jnp.int32, shape, 1) + j * LANES) // (POOL // 4)


def _pool_win_fwd(z, name):
    s = z.shape[0]
    ch = _tile(s, 512)
    col0 = (IN_P - POOL) // LANES

    def body(p_ref, m_ref):
        j = pl.program_id(0)

        def chunk(r, _):
            off = pl.multiple_of(r * ch, ch)
            cur = p_ref[pl.ds(off, ch), :]
            hoff = pl.multiple_of(jnp.maximum(off - HALO, 0), 8)
            halo = jnp.where(r > 0, p_ref[pl.ds(hoff, HALO), :], 0.0)
            x = jnp.concatenate([halo, cur], axis=0)
            s2 = x + pltpu.roll(x, 1, 0)
            s4 = s2 + pltpu.roll(s2, 2, 0)
            s8 = s4 + pltpu.roll(s4, 4, 0)
            s16 = s8 + pltpu.roll(s8, 8, 0)
            grp = _lane_group((ch, LANES), j)
            sel = jnp.where(grp == 0, s2[HALO:], jnp.where(grp == 1, s4[HALO:], jnp.where(grp == 2, s8[HALO:], s16[HALO:])))
            t1 = (lax.broadcasted_iota(jnp.int32, (ch, LANES), 0) + off + 1).astype(F32)
            win = jnp.where(grp == 0, 2.0, jnp.where(grp == 1, 4.0, jnp.where(grp == 2, 8.0, 16.0)))
            m_ref[pl.ds(off, ch), :] = sel / jnp.minimum(t1, win) - cur
            return 0

        lax.fori_loop(0, s // ch, chunk, 0)

    return pl.pallas_call(
        body, name=name, grid=(POOL // LANES,),
        in_specs=[pl.BlockSpec((s, LANES), lambda j: (0, col0 + j))],
        out_specs=pl.BlockSpec((s, LANES), lambda j: (0, j)),
        out_shape=jax.ShapeDtypeStruct((s, POOL), F32),
        compiler_params=_cp(("parallel",), VMEM_LIMIT),
    )(z)


def _pool_win_bwd(dm, name):
    s = dm.shape[0]
    ch = _tile(s, 512)
    n = s // ch

    def body(dm_ref, dp_ref):
        j = pl.program_id(0)

        def chunk(r, _):
            off = pl.multiple_of(r * ch, ch)
            grp = _lane_group((ch + HALO, LANES), j)
            win = jnp.where(grp == 0, 2.0, jnp.where(grp == 1, 4.0, jnp.where(grp == 2, 8.0, 16.0)))
            cur = dm_ref[pl.ds(off, ch), :]
            hoff = pl.multiple_of(jnp.minimum(off + ch, s - HALO), 8)
            halo = jnp.where(r < n - 1, dm_ref[pl.ds(hoff, HALO), :], 0.0)
            x = jnp.concatenate([cur, halo], axis=0)
            t1 = (lax.broadcasted_iota(jnp.int32, (ch + HALO, LANES), 0) + off + 1).astype(F32)
            e = x / jnp.minimum(t1, win)
            tot = ch + HALO
            r2 = e + pltpu.roll(e, tot - 1, 0)
            r4 = r2 + pltpu.roll(r2, tot - 2, 0)
            r8 = r4 + pltpu.roll(r4, tot - 4, 0)
            r16 = r8 + pltpu.roll(r8, tot - 8, 0)
            g = grp[:ch]
            sel = jnp.where(g == 0, r2[:ch], jnp.where(g == 1, r4[:ch], jnp.where(g == 2, r8[:ch], r16[:ch])))
            dp_ref[pl.ds(off, ch), :] = (sel - cur).astype(BF16)
            return 0

        lax.fori_loop(0, n, chunk, 0)

    return pl.pallas_call(
        body, name=name, grid=(POOL // LANES,),
        in_specs=[pl.BlockSpec((s, LANES), lambda j: (0, j))],
        out_specs=pl.BlockSpec((s, LANES), lambda j: (0, j)),
        out_shape=jax.ShapeDtypeStruct((s, POOL), BF16),
        compiler_params=_cp(("parallel",), VMEM_LIMIT),
    )(dm)


def _head_mask(h):
    lane = lax.broadcasted_iota(jnp.int32, (CHUNK, SGU), 1)
    return (lane // (SGU // HEADS)) == h


def _tril(upper=False):
    row = lax.broadcasted_iota(jnp.int32, (CHUNK, CHUNK), 0)
    col = lax.broadcasted_iota(jnp.int32, (CHUNK, CHUNK), 1)
    return col >= row if upper else col <= row


def _sgu_gate(vn, wsp, bsp):
    out = []
    for cidx in range(vn.shape[0] // CHUNK):
        vc = vn[cidx * CHUNK:(cidx + 1) * CHUNK]
        zc = bsp
        for h in range(HEADS):
            zc = zc + jnp.where(_head_mask(h), _dot(wsp[h], vc), 0.0)
        out.append(zc)
    return jnp.concatenate(out, axis=0)


def _mix_out_fwd(o, z, m, x, wsp, bsp, wbd, psc, gsv, gout, wout, name):
    s = x.shape[0]
    tm = _tile(s, TOKENS)

    def body(o_ref, uv_ref, m_ref, x_ref, wsp_ref, bsp_ref, wbd_ref, psc_ref, gsv_ref, gout_ref, wout_ref,
             x1_ref, mix_ref):
        g = gout_ref[...]
        an = _rms(o_ref[...], HEADS * VH)[0] * g[:, :512]
        uv = uv_ref[...]
        u, v = uv[:, :SGU], uv[:, SGU:]
        vn = (_rms(v, SGU)[0] * gsv_ref[...]).astype(BF16)
        tri = _tril()
        wsp_m = [jnp.where(tri, wsp_ref[h], 0.0).astype(BF16) for h in range(HEADS)]
        gm = u * _sgu_gate(vn, wsp_m, bsp_ref[...])
        gn = _rms(gm, SGU)[0] * g[:, 512:768]
        po = _dot(m_ref[...].astype(BF16), wbd_ref[...]) * psc_ref[...]
        pn = _rms(po, POOL)[0] * g[:, 768:]
        mix = jnp.concatenate([an, gn, pn], axis=1).astype(BF16)
        mix_ref[...] = mix
        x1_ref[...] = x_ref[...] + _dot(mix, wout_ref[...])

    row = lambda w, j: pl.BlockSpec((tm, w), lambda i: (i, j))
    return pl.pallas_call(
        body, name=name, grid=(s // tm,),
        in_specs=[row(512, 0), row(512, 1), row(POOL, 0), row(D, 0),
                  _acc((HEADS, CHUNK, CHUNK)), _acc((CHUNK, SGU)), _acc((POOL, POOL)), _acc((1, POOL)),
                  _acc((1, SGU)), _acc((1, D)), _res((D, D))],
        out_specs=[row(D, 0), row(D, 0)],
        out_shape=[jax.ShapeDtypeStruct((s, D), F32), jax.ShapeDtypeStruct((s, D), BF16)],
        compiler_params=_cp(("parallel",), VMEM_LIMIT),
    )(o, z, m, x, wsp, bsp, wbd, psc, gsv, gout, wout)


def _ffn_fwd(x1, g, wg, wu, wd, tgt, name):
    s = x1.shape[0]
    tm = _tile(s, 256)
    last = tgt is not None

    def body(x_ref, g_ref, wg_ref, wu_ref, wd_ref, *rest):
        t_ref = rest[0] if last else None
        outs = rest[1:] if last else rest
        a_ref, b_ref, h_ref = outs[-3:]
        x = x_ref[...]
        h = (_rms(x, D)[0] * g_ref[...]).astype(BF16)
        h_ref[...] = h
        acc = jnp.zeros((tm, D), F32)
        for k in range(CHIPS):
            a = _dot_nt(h, wg_ref[k])
            b = _dot_nt(h, wu_ref[k])
            a_ref[k] = a
            b_ref[k] = b
            acc = acc + _dot((a * jax.nn.sigmoid(a) * b).astype(BF16), wd_ref[k])
        if not last:
            outs[0][...] = x + acc
            return
        dy_ref, l_ref = outs[:2]
        e = (x + acc) - t_ref[...]
        dy_ref[...] = e * (1.0 / D)
        sq = jnp.sum(e * e, axis=0, keepdims=True)
        part = sq[:, :LANES]
        for c in range(1, D // LANES):
            part = part + sq[:, c * LANES:(c + 1) * LANES]
        _accumulate(l_ref, part, pl.program_id(0) == 0)

    row = lambda w: pl.BlockSpec((tm, w), lambda i: (i, 0))
    hrow = pl.BlockSpec((CHIPS, tm, SH), lambda i: (0, i, 0))
    hshape = jax.ShapeDtypeStruct((CHIPS, s, SH), F32)
    tail_specs = [hrow, hrow, row(D)]
    tail_shapes = [hshape, hshape, jax.ShapeDtypeStruct((s, D), BF16)]
    head_specs = [row(D), _acc((1, LANES))] if last else [row(D)]
    head_shapes = [jax.ShapeDtypeStruct((s, D), F32)] + ([jax.ShapeDtypeStruct((1, LANES), F32)] if last else [])
    res = pl.pallas_call(
        body, name=name, grid=(s // tm,),
        in_specs=[row(D), _acc((1, D)), _res((CHIPS, SH, D)), _res((CHIPS, SH, D)), _res((CHIPS, SH, D))]
        + ([row(D)] if last else []),
        out_specs=head_specs + tail_specs, out_shape=head_shapes + tail_shapes,
        compiler_params=_cp(("arbitrary",), VMEM_LIMIT),
    )(x1, g, wg, wu, wd, *([tgt] if last else []))
    return (tuple(res[:2]) if last else res[0]), res[-3], res[-2], res[-1]


def _wgrad(a, b, name):
    s, k = a.shape
    n = b.shape[1]
    half = lambda v: v if v <= 1408 else v // 2
    kb, nb, tt = half(k), half(n), _tile(s, 2048)

    def body(a_ref, b_ref, o_ref):
        _accumulate(o_ref, _dot_tn(a_ref[...].astype(BF16), b_ref[...].astype(BF16)), pl.program_id(2) == 0)

    return pl.pallas_call(
        body, name=name, grid=(k // kb, n // nb, s // tt),
        in_specs=[pl.BlockSpec((tt, kb), lambda i, j, t: (t, i)), pl.BlockSpec((tt, nb), lambda i, j, t: (t, j))],
        out_specs=pl.BlockSpec((kb, nb), lambda i, j, t: (i, j)),
        out_shape=jax.ShapeDtypeStruct((k, n), F32),
        compiler_params=_cp(("parallel", "parallel", "arbitrary"), VMEM_LIMIT),
    )(a, b)


def _wgrad_in(h, dzm, duv, dp, name):
    s = h.shape[0]
    tt = _tile(s, 2048)

    def body(h_ref, a_ref, b_ref, c_ref, o_ref):
        hv = h_ref[...]
        val = jnp.concatenate([_dot_tn(hv, a_ref[...]), _dot_tn(hv, b_ref[...]), _dot_tn(hv, c_ref[...])], axis=1)
        _accumulate(o_ref, val, pl.program_id(0) == 0)

    row = lambda w: pl.BlockSpec((tt, w), lambda t: (t, 0))
    return pl.pallas_call(
        body, name=name, grid=(s // tt,), in_specs=[row(D), row(512), row(512), row(POOL)], out_specs=_acc((D, IN_P)),
        out_shape=jax.ShapeDtypeStruct((D, IN_P), F32), compiler_params=_cp(("arbitrary",), VMEM_LIMIT),
    )(h, dzm, duv, dp)


def _wgrad_rows(a, b, name):
    s, n = a.shape[1:]
    nn = b.shape[1]
    tt = _tile(s, 4096 if b.dtype == BF16 else 2048)

    def body(a_ref, b_ref, o_ref):
        _accumulate0(o_ref, _dot_tn(a_ref[0].astype(BF16), b_ref[...].astype(BF16)), pl.program_id(1) == 0)

    return pl.pallas_call(
        body, name=name, grid=(CHIPS, s // tt),
        in_specs=[pl.BlockSpec((1, tt, n), lambda c, t: (c, t, 0)), pl.BlockSpec((tt, nn), lambda c, t: (t, 0))],
        out_specs=pl.BlockSpec((1, n, nn), lambda c, t: (c, 0, 0)),
        out_shape=jax.ShapeDtypeStruct((CHIPS, n, nn), F32),
        compiler_params=_cp(("parallel", "arbitrary"), VMEM_LIMIT),
    )(a, b)


def _ffn_bwd(dx2, x1, a, b, g, wg, wu, wd, name):
    s = x1.shape[0]
    tm = _tile(s, 256)

    def body(dx2_ref, x_ref, a_ref, b_ref, g_ref, wg_ref, wu_ref, wd_ref,
             dx1_ref, hid_ref, da_ref, db_ref, dyb_ref, dg_ref):
        dx2 = dx2_ref[...]
        dyb = dx2.astype(BF16)
        dyb_ref[...] = dyb
        dh = jnp.zeros((tm, D), F32)
        ahead = _dot_nt(dyb, wd_ref[0])
        for k in range(CHIPS):
            av, bv = a_ref[k], b_ref[k]
            dhid = ahead
            if k + 1 < CHIPS:
                ahead = _dot_nt(dyb, wd_ref[k + 1])
            sig = jax.nn.sigmoid(av)
            sa = av * sig
            hid_ref[k] = (sa * bv).astype(BF16)
            dbv = (dhid * sa).astype(BF16)
            dav = (dhid * bv * (sig * (1.0 + av * (1.0 - sig)))).astype(BF16)
            db_ref[k] = dbv
            da_ref[k] = dav
            dh = dh + _dot(dav, wg_ref[k]) + _dot(dbv, wu_ref[k])
        xn, r = _rms(x_ref[...], D)
        dxr, dg = _rms_bwd(xn, r, g_ref[...], dh, D)
        dx1_ref[...] = dx2 + dxr
        _accumulate(dg_ref, dg, pl.program_id(0) == 0)

    row = lambda w: pl.BlockSpec((tm, w), lambda i: (i, 0))
    hrow = pl.BlockSpec((CHIPS, tm, SH), lambda i: (0, i, 0))
    hid = jax.ShapeDtypeStruct((CHIPS, s, SH), BF16)
    return pl.pallas_call(
        body, name=name, grid=(s // tm,),
        in_specs=[row(D), row(D), hrow, hrow, _acc((1, D)), _res((CHIPS, SH, D)), _res((CHIPS, SH, D)),
                  _res((CHIPS, SH, D))],
        out_specs=[row(D), hrow, hrow, hrow, row(D), _acc((1, D))],
        out_shape=[jax.ShapeDtypeStruct((s, D), F32), hid, hid, hid, jax.ShapeDtypeStruct((s, D), BF16),
                   jax.ShapeDtypeStruct((1, D), F32)],
        compiler_params=_cp(("arbitrary",), VMEM_LIMIT),
    )(dx2, x1, a, b, g, wg, wu, wd)


def _mix_out_bwd(dx1, o, z, m, wsp, bsp, wbd, psc, gsv, gout, wout, name):
    s = dx1.shape[0]
    tm = _tile(s, TOKENS)

    def body(dx1_ref, o_ref, uv_ref, m_ref, wsp_ref, bsp_ref, wbd_ref, psc_ref, gsv_ref, gout_ref, wout_ref,
             do_ref, dl_ref, duv_ref, dm_ref, dgo_ref, dgsv_ref, dpsc_ref, dwsp_ref, dbsp_ref, dwbd_ref):
        first = pl.program_id(0) == 0
        g = gout_ref[...]
        dmix = _dot_nt(dx1_ref[...].astype(BF16), wout_ref[...])
        o = o_ref[...]
        on, ro = _rms(o, HEADS * VH)
        do, dga = _rms_bwd(on, ro, g[:, :512], dmix[:, :512], HEADS * VH)
        for h in range(HEADS):
            sl = slice(h * VH, (h + 1) * VH)
            do_ref[h] = do[:, sl].astype(BF16)
            dl_ref[h] = jnp.broadcast_to(jnp.sum(do[:, sl] * o[:, sl], axis=-1, keepdims=True), (tm, LANES))
        uv = uv_ref[...]
        u, v = uv[:, :SGU], uv[:, SGU:]
        vx, rv = _rms(v, SGU)
        vn = (vx * gsv_ref[...]).astype(BF16)
        tri = _tril()
        wsp_m = [jnp.where(tri, wsp_ref[h], 0.0).astype(BF16) for h in range(HEADS)]
        zc = _sgu_gate(vn, wsp_m, bsp_ref[...])
        gm = u * zc
        gmn, rg = _rms(gm, SGU)
        dgm, dgg = _rms_bwd(gmn, rg, g[:, 512:768], dmix[:, 512:768], SGU)
        du = dgm * zc
        dzc = dgm * u
        dvn_parts = []
        dbsp = jnp.zeros((CHUNK, SGU), F32)
        dwsp = [jnp.zeros((CHUNK, CHUNK), F32) for _ in range(HEADS)]
        for cidx in range(tm // CHUNK):
            rs = slice(cidx * CHUNK, (cidx + 1) * CHUNK)
            dzc_c = dzc[rs]
            dbsp = dbsp + dzc_c
            dzb = dzc_c.astype(BF16)
            vc = vn[rs]
            dvn_c = jnp.zeros((CHUNK, SGU), F32)
            for h in range(HEADS):
                hm = _head_mask(h)
                dvn_c = dvn_c + jnp.where(hm, _dot_tn(wsp_m[h], dzb), 0.0)
                dwsp[h] = dwsp[h] + _dot_nt(jnp.where(hm, dzc_c, 0.0).astype(BF16), vc)
            dvn_parts.append(dvn_c)
        dvn = jnp.concatenate(dvn_parts, axis=0)
        dv, dgsv = _rms_bwd(vx, rv, gsv_ref[...], dvn, SGU)
        duv_ref[...] = jnp.concatenate([du, dv], axis=1).astype(BF16)
        mb = m_ref[...].astype(BF16)
        pw = _dot(mb, wbd_ref[...])
        po = pw * psc_ref[...]
        pon, rp = _rms(po, POOL)
        dpo, dgp = _rms_bwd(pon, rp, g[:, 768:], dmix[:, 768:], POOL)
        dpw = (dpo * psc_ref[...]).astype(BF16)
        dm_ref[...] = _dot_nt(dpw, wbd_ref[...])
        _accumulate(dgo_ref, jnp.concatenate([dga, dgg, dgp], axis=1), first)
        _accumulate(dgsv_ref, dgsv, first)
        _accumulate(dpsc_ref, jnp.sum(dpo * pw, axis=0, keepdims=True), first)
        _accumulate(dbsp_ref, dbsp, first)
        _accumulate(dwbd_ref, _dot_tn(mb, dpw), first)
        for h in range(HEADS):
            val = jnp.where(tri, dwsp[h], 0.0)

            @pl.when(first)
            def _(val=val, h=h):
                dwsp_ref[h] = val

            @pl.when(jnp.logical_not(first))
            def _(val=val, h=h):
                dwsp_ref[h] += val

    row = lambda w, j: pl.BlockSpec((tm, w), lambda i: (i, j))
    hspec = pl.BlockSpec((HEADS, tm, HP), lambda i: (0, i, 0))
    return pl.pallas_call(
        body, name=name, grid=(s // tm,),
        in_specs=[row(D, 0), row(512, 0), row(512, 1), row(POOL, 0),
                  _acc((HEADS, CHUNK, CHUNK)), _acc((CHUNK, SGU)),
                  _acc((POOL, POOL)), _acc((1, POOL)), _acc((1, SGU)), _acc((1, D)), _res((D, D))],
        out_specs=[hspec, hspec, row(512, 0), row(POOL, 0), _acc((1, D)), _acc((1, SGU)), _acc((1, POOL)),
                   _acc((HEADS, CHUNK, CHUNK)), _acc((CHUNK, SGU)), _acc((POOL, POOL))],
        out_shape=[jax.ShapeDtypeStruct((HEADS, s, HP), BF16), jax.ShapeDtypeStruct((HEADS, s, LANES), F32),
                   jax.ShapeDtypeStruct((s, 512), BF16), jax.ShapeDtypeStruct((s, POOL), F32),
                   jax.ShapeDtypeStruct((1, D), F32), jax.ShapeDtypeStruct((1, SGU), F32),
                   jax.ShapeDtypeStruct((1, POOL), F32), jax.ShapeDtypeStruct((HEADS, CHUNK, CHUNK), F32),
                   jax.ShapeDtypeStruct((CHUNK, SGU), F32), jax.ShapeDtypeStruct((POOL, POOL), F32)],
        compiler_params=_cp(("arbitrary",), VMEM_LIMIT),
    )(dx1, o, z, m, wsp, bsp, wbd, psc, gsv, gout, wout)


def _attn_bwd(q, k, v, do, lse, delta, after, name):
    s = q.shape[1]
    rh = _tile(s, ATT_ROWS)
    tk = _tile(s, ATT_KEYS)
    nk = s // tk
    wide = ATT_QUERIES if s % ATT_QUERIES == 0 else tk
    pieces = tk // rh

    def body(q_ref, k_ref, v_ref, do_ref, lse_ref, dl_ref, after_ref, dq_ref, dk_ref, dv_ref):
        del after_ref
        j = pl.program_id(1)

        @pl.when(j == 0)
        def _():
            dq_ref[...] = jnp.zeros_like(dq_ref)

        kj, vj = k_ref[0], v_ref[0]

        def blk(start, rows, dks, dvs, diagonal):
            dks, dvs = list(dks), list(dvs)
            offs = [pl.multiple_of(start + g * rh, rh) for g in range(rows // rh)]
            keys = [(g + 1) * rh if diagonal else tk for g in range(rows // rh)]
            qs = [q_ref[0, pl.ds(off, rh), :] for off in offs]
            dos = [do_ref[0, pl.ds(off, rh), :] for off in offs]
            scs = [_dot_nt(qi, kj[:n]) for qi, n in zip(qs, keys)]
            dps = [_dot_nt(doi, vj[:n]) for doi, n in zip(dos, keys)]
            for g, off in enumerate(offs):
                lse_i = lse_ref[0, pl.ds(off, rh), :][:, :1]
                dl_i = dl_ref[0, pl.ds(off, rh), :][:, :1]
                sc = _causal_mask(scs[g], g * rh) if diagonal else scs[g]
                p = jnp.exp2(sc - lse_i)
                ds = (p * (dps[g] - dl_i)).astype(BF16)
                cv = _dot_tn(p.astype(BF16), dos[g])
                ck = _dot_tn(ds, qs[g])
                for t in range(keys[g] // rh):
                    dvs[t] = dvs[t] + cv[t * rh:(t + 1) * rh]
                    dks[t] = dks[t] + ck[t * rh:(t + 1) * rh]
                dq_ref[0, pl.ds(off, rh), :] += _dot(ds, kj[:keys[g]]) * SCALE
            return tuple(dks), tuple(dvs)

        per = wide // tk
        zero = (jnp.zeros((rh, HP), F32),) * pieces
        acc = blk(j * tk, tk, zero, zero, True)
        first_wide = (j + per) // per
        acc = lax.fori_loop(j + 1, jnp.minimum(first_wide * per, nk), lambda i, c: blk(i * tk, tk, *c, False), acc)
        dks, dvs = lax.fori_loop(first_wide, nk // per, lambda i, c: blk(i * wide, wide, *c, False), acc)
        dk_ref[0] = jnp.concatenate(dks, axis=0) * (SCALE / EXP2_C)
        dv_ref[0] = jnp.concatenate(dvs, axis=0)

    full = lambda: pl.BlockSpec((1, s, HP), lambda h, j: (h, 0, 0))
    blk_spec = lambda: pl.BlockSpec((1, tk, HP), lambda h, j: (h, j, 0))
    out = jax.ShapeDtypeStruct((HEADS, s, HP), F32)
    return pl.pallas_call(
        body, name=name, grid=(HEADS, s // tk),
        in_specs=[full(), blk_spec(), blk_spec(), full(), full(), full(), ANY],
        out_specs=[full(), blk_spec(), blk_spec()], out_shape=[out] * 3,
        compiler_params=_cp(("parallel", "arbitrary"), VMEM_LIMIT),
    )(q, k, v, do, lse, delta, after)


def _mla_prep_bwd(dq, dk, dv, z, tabs, gql, gkv, gq, gk, wq, wk, wv, name):
    s = z.shape[0]
    tm = _tile(s, TOKENS)

    def body(dq_ref, dk_ref, dv_ref, ql_ref, kv_ref, kr_ref, c_ref, sa_ref, sb_ref, gql_ref, gkv_ref, gq_ref, gk_ref,
             wq_ref, wk_ref, wv_ref,
             dz_ref, dwq_ref, dwk_ref, dwv_ref, dgql_ref, dgkv_ref, dgq_ref, dgk_ref, dqr_ref, dkr_ref, dvr_ref):
        first = pl.program_id(0) == 0
        qx, rq = _rms(ql_ref[...], QL)
        qn = (qx * gql_ref[...]).astype(BF16)
        kx, rk = _rms(kv_ref[...], KVL)
        kvn = (kx * gkv_ref[...]).astype(BF16)
        qraw = _dot(qn, wq_ref[...])
        kraw = _dot(kvn, wk_ref[...])
        kr = kr_ref[...]
        c, sa, sb = c_ref[...], sa_ref[...], sb_ref[...]
        lane = lax.broadcasted_iota(jnp.int32, (tm, HP), 1)
        rope_lanes = (lane >= NOPE) & (lane < QK)
        dkrope = jnp.zeros((tm, HP), F32)
        dgq = jnp.zeros((1, HP), F32)
        dgk = jnp.zeros((1, HP), F32)
        for h in range(HEADS):
            sl = slice(h * HP, (h + 1) * HP)
            xn, r = _rms(qraw[:, sl], QK)
            dx, dg = _rms_bwd(xn, r, gq_ref[...], _rope_t(dq_ref[h], c, sa, sb), QK)
            dqr_ref[:, sl] = dx.astype(BF16)
            dgq = dgq + dg
            xn, r = _rms(kraw[:, sl] + kr, QK)
            dx, dg = _rms_bwd(xn, r, gk_ref[...], _rope_t(dk_ref[h], c, sa, sb), QK)
            dkr_ref[:, sl] = dx.astype(BF16)
            dgk = dgk + dg
            dkrope = dkrope + jnp.where(rope_lanes, dx, 0.0)
            dvr_ref[:, sl] = dv_ref[h].astype(BF16)
        dqn = _dot_nt(dqr_ref[...], wq_ref[...])
        dql, dgql = _rms_bwd(qx, rq, gql_ref[...], dqn, QL)
        dkvn = _dot_nt(dkr_ref[...], wk_ref[...]) + _dot_nt(dvr_ref[...], wv_ref[...])
        dkv, dgkv = _rms_bwd(kx, rk, gkv_ref[...], dkvn, KVL)
        dz_ref[...] = jnp.concatenate([dql, dkv, dkrope], axis=1).astype(BF16)
        _accumulate(dwq_ref, _dot_tn(qn, dqr_ref[...]), first)
        _accumulate(dwk_ref, _dot_tn(kvn, dkr_ref[...]), first)
        _accumulate(dwv_ref, _dot_tn(kvn, dvr_ref[...]), first)
        _accumulate(dgql_ref, dgql, first)
        _accumulate(dgkv_ref, dgkv, first)
        _accumulate(dgq_ref, dgq, first)
        _accumulate(dgk_ref, dgk, first)

    row = lambda w, j: pl.BlockSpec((tm, w), lambda i: (i, j))
    hspec = pl.BlockSpec((HEADS, tm, HP), lambda i: (0, i, 0))
    acc = lambda r, c: (_acc((r, c)), jax.ShapeDtypeStruct((r, c), F32))
    outs = [(row(512, 0), jax.ShapeDtypeStruct((s, 512), BF16)), acc(QL, HEADS * HP), acc(KVL, HEADS * HP),
            acc(KVL, HEADS * HP), acc(1, QL), acc(1, KVL), acc(1, HP), acc(1, HP)]
    return pl.pallas_call(
        body, name=name, grid=(s // tm,),
        in_specs=[hspec, hspec, hspec, row(QL, 0), row(KVL, 2), row(HP, 3), row(HP, 0), row(HP, 0), row(HP, 0),
                  _acc((1, QL)), _acc((1, KVL)), _acc((1, HP)), _acc((1, HP)),
                  _acc((QL, HEADS * HP)), _acc((KVL, HEADS * HP)), _acc((KVL, HEADS * HP))],
        out_specs=[o[0] for o in outs], out_shape=[o[1] for o in outs],
        scratch_shapes=[pltpu.VMEM((tm, HEADS * HP), BF16)] * 3,
        compiler_params=_cp(("arbitrary",), VMEM_LIMIT),
    )(dq, dk, dv, z, z, z, *tabs, gql, gkv, gq, gk, wq, wk, wv)


def _in_proj_bwd(dzm, duv, dp, x, dx1, g, win, name):
    s = x.shape[0]
    tm = _tile(s, TOKENS // 2)

    def body(dzm_ref, duv_ref, dp_ref, x_ref, dx1_ref, g_ref, w_ref, dx_ref, dg_ref):
        groups = [slice(r0, r0 + tm // 2) for r0 in (0, tm // 2)]
        dhs = [_dot_nt(dzm_ref[rs, :], w_ref[:, 0:512]) + _dot_nt(duv_ref[rs, :], w_ref[:, 512:1024])
               + _dot_nt(dp_ref[rs, :], w_ref[:, 1024:IN_P]) for rs in groups]
        dg = jnp.zeros((1, D), F32)
        for rs, dh in zip(groups, dhs):
            xn, r = _rms(x_ref[rs, :], D)
            dxr, dgr = _rms_bwd(xn, r, g_ref[...], dh, D)
            dx_ref[rs, :] = dx1_ref[rs, :] + dxr
            dg = dg + dgr
        _accumulate(dg_ref, dg, pl.program_id(0) == 0)

    row = lambda w: pl.BlockSpec((tm, w), lambda i: (i, 0))
    return pl.pallas_call(
        body, name=name, grid=(s // tm,),
        in_specs=[row(512), row(512), row(POOL), row(D), row(D), _acc((1, D)), _res((D, IN_P))],
        out_specs=[row(D), _acc((1, D))],
        out_shape=[jax.ShapeDtypeStruct((s, D), F32), jax.ShapeDtypeStruct((1, D), F32)],
        compiler_params=_cp(("arbitrary",), VMEM_LIMIT),
    )(dzm, duv, dp, x, dx1, g, win)


def _adamw(w, g0, g1, m, v, name):
    _, r, c = w.shape
    tr = _row_tile(r, 512)
    c1 = 1.0 - B1 ** STEP
    c2 = 1.0 - B2 ** STEP

    def body(w_ref, g0_ref, g1_ref, m_ref, v_ref, g_ref, d_ref, nm_ref, nv_ref):
        gv = jnp.where(pl.program_id(0) == 0, g0_ref[...], g1_ref[...])
        g_ref[0] = gv
        nm = B1 * m_ref[0] + (1.0 - B1) * gv
        nv = B2 * v_ref[0] + (1.0 - B2) * (gv * gv)
        nm_ref[0] = nm
        nv_ref[0] = nv
        d_ref[0] = -LR * ((nm / c1) / (jnp.sqrt(nv / c2) + ADAM_EPS) + WD * w_ref[0])

    spec = pl.BlockSpec((1, tr, c), lambda l, i: (l, i, 0))
    out = jax.ShapeDtypeStruct((DEPTH, r, c), F32)
    return pl.pallas_call(
        body, name=name, grid=(DEPTH, r // tr),
        in_specs=[spec, pl.BlockSpec((tr, c), lambda l, i: (i * (1 - l), 0)), pl.BlockSpec((tr, c), lambda l, i: (i * l, 0)),
                  spec, spec],
        out_specs=[spec] * 4, out_shape=[out] * 4, compiler_params=_cp(("parallel", "parallel")),
    )(w, g0, g1, m, v)


ANY = pl.BlockSpec(memory_space=pl.ANY)


def _place():
    x, y, c = lax.axis_index("x"), lax.axis_index("y"), lax.axis_index("c")
    chips = [(1 - x, y), (x, 1 - y), (1 - x, 1 - y)]
    return x, y, c, chips


def _half_rows(ref, lead, hh, half, align):
    rows = pl.ds(pl.multiple_of(hh * half, align), half)
    return ref.at[rows, :] if lead is None else ref.at[lead, rows, :]


def _row_align(dtype):
    return 16 if dtype == BF16 else 8


def _sems(n):
    return [pltpu.SemaphoreType.DMA((n,)), pltpu.SemaphoreType.DMA((n,)), pltpu.SemaphoreType.DMA((n,))]


def _comm_call(body, ins, out_shapes, nsems, name):
    return pl.pallas_call(
        body, name=name, in_specs=[ANY] * len(ins), out_specs=[ANY] * len(out_shapes), out_shape=out_shapes,
        scratch_shapes=_sems(nsems), compiler_params=pltpu.CompilerParams(has_side_effects=True),
    )(*ins)


def _all_gather_chips(shards, name):
    n = len(shards)
    halves = [a.shape[0] // 2 for a in shards]
    aligns = [_row_align(a.dtype) for a in shards]
    assert all(h % al == 0 for h, al in zip(halves, aligns))

    def body(*refs):
        ins, outs, (send_sems, recv_sems, _) = refs[:n], refs[n:2 * n], refs[2 * n:]
        x, y, c, chips = _place()
        me = 2 * x + y
        sibling = (x, y, 1 - c)

        def copy(sem, src, dst, to):
            return pltpu.make_async_remote_copy(src_ref=src, dst_ref=dst, send_sem=send_sems.at[sem],
                                                recv_sem=recv_sems.at[sem], device_id=to, device_id_type=MESH)

        first, passed = [], []
        for a in range(n):
            my_half = _half_rows(ins[a], None, c, halves[a], aligns[a])
            for j, (cx, cy) in enumerate(chips):
                cp = copy(6 * a + j, my_half, _half_rows(outs[a], me, c, halves[a], aligns[a]), (cx, cy, c))
                cp.start()
                first.append(cp)
        for a in range(n):
            for j, (cx, cy) in enumerate(chips):
                landed = _half_rows(outs[a], 2 * cx + cy, c, halves[a], aligns[a])
                copy(6 * a + j, landed, landed, (cx, cy, c)).wait_recv()
                fwd = copy(6 * a + 3 + j, landed, landed, sibling)
                fwd.start()
                passed.append(fwd)
        for a in range(n):
            for j, (cx, cy) in enumerate(chips):
                other = _half_rows(outs[a], 2 * cx + cy, 1 - c, halves[a], aligns[a])
                copy(6 * a + 3 + j, other, other, sibling).wait_recv()
        for cp in first + passed:
            cp.wait_send()

    lands = _comm_call(body, shards, [jax.ShapeDtypeStruct((CHIPS,) + a.shape, a.dtype) for a in shards], 6 * n, name)
    return _with_own(lands, shards)


def _with_own(lands, shards):
    me = 2 * lax.axis_index("x") + lax.axis_index("y")
    return [lax.dynamic_update_slice(g, a[None], (me, 0, 0)) for g, a in zip(lands, shards)]


def _pair_join(arrs, name):
    n = len(arrs)
    halves = [a.shape[0] // 2 for a in arrs]

    def body(*refs):
        outs, (send_sems, recv_sems, _) = refs[n:2 * n], refs[2 * n:]
        x, y, c, _ = _place()
        cps = []
        for a in range(n):
            mine = _half_rows(outs[a], None, c, halves[a], 8)
            cp = pltpu.make_async_remote_copy(src_ref=mine, dst_ref=mine, send_sem=send_sems.at[a], recv_sem=recv_sems.at[a],
                                              device_id=(x, y, 1 - c), device_id_type=MESH)
            cp.start()
            cps.append(cp)
        for cp in cps:
            cp.wait()

    return pl.pallas_call(
        body, name=name, in_specs=[ANY] * n, out_specs=[ANY] * n,
        out_shape=[jax.ShapeDtypeStruct(a.shape, a.dtype) for a in arrs],
        input_output_aliases={i: i for i in range(n)}, scratch_shapes=_sems(n),
        compiler_params=pltpu.CompilerParams(has_side_effects=True),
    )(*arrs)


HBM = pl.BlockSpec(memory_space=pltpu.HBM)
SEM = pl.BlockSpec(memory_space=pltpu.SEMAPHORE)
DATAFLOW = pltpu.SideEffectType.DATAFLOW_SIDE_EFFECTING


def _remote_copies(pairs, ins, lands, send_sems, recv_sems):
    return [pltpu.make_async_remote_copy(src_ref=src, dst_ref=dst, send_sem=send_sems.at[i], recv_sem=recv_sems.at[i],
                                         device_id=to, device_id_type=MESH)
            for i, (src, dst, to) in enumerate(pairs(ins, lands))]


def _split_start(srcs, land_shapes, ncopies, pairs, name, after):
    n, m = len(srcs), len(land_shapes)

    def body(*refs):
        ins, lands = refs[:n], refs[n:n + m]
        send_sems, recv_sems, token = refs[n + m + 1], refs[n + m + 2], refs[-1]
        for cp in _remote_copies(pairs, ins, lands, send_sems, recv_sems):
            cp.start()
        token[...] = jnp.zeros_like(token)

    hbm = lambda a: pltpu.with_memory_space_constraint(a, pltpu.HBM)
    lands = [hbm(lax.empty(s.shape, s.dtype)) for s in land_shapes]
    thru = [pltpu.HBM(a.shape, a.dtype) for a in list(srcs) + lands]
    out = pl.pallas_call(
        body, name=name,
        out_shape=(pltpu.SemaphoreType.DMA((ncopies,)), pltpu.SemaphoreType.DMA((ncopies,)), *thru,
                   jax.ShapeDtypeStruct((8, LANES), F32)),
        in_specs=[HBM] * (n + m) + [ANY], out_specs=(SEM, SEM, *[HBM] * (n + m), pl.BlockSpec(memory_space=pltpu.VMEM)),
        input_output_aliases={i: 2 + i for i in range(n + m)},
        compiler_params=pltpu.CompilerParams(has_side_effects=DATAFLOW),
    )(*[hbm(a) for a in srcs], *lands, after)
    return out[0], out[1], list(out[2:2 + n]), list(out[2 + n:2 + n + m]), out[-1]


def _split_wait(send_sems, recv_sems, srcs, lands, after, pairs, name):
    n, m = len(srcs), len(lands)

    def body(*refs):
        ins, lands_ = refs[:n], refs[n:n + m]
        for cp in _remote_copies(pairs, ins, lands_, refs[n + m], refs[n + m + 1]):
            cp.wait_send()
            cp.wait_recv()

    out = pl.pallas_call(
        body, name=name, out_shape=tuple(pltpu.HBM(a.shape, a.dtype) for a in list(srcs) + list(lands)),
        in_specs=[HBM] * (n + m) + [SEM, SEM, ANY], out_specs=tuple([HBM] * (n + m)),
        input_output_aliases={i: i for i in range(n + m)},
        compiler_params=pltpu.CompilerParams(has_side_effects=DATAFLOW),
    )(*srcs, *lands, send_sems, recv_sems, after)
    return list(out[:n]), list(out[n:])


def _gather_pairs(halves, aligns):
    def pairs(ins, lands):
        x, y, c, chips = _place()
        me = 2 * x + y
        return [(_half_rows(ins[a], None, c, halves[a], aligns[a]), _half_rows(lands[a], me, c, halves[a], aligns[a]),
                 (cx, cy, c)) for a in range(len(ins)) for cx, cy in chips]
    return pairs


PEERS = 7


def _scatter_pairs(ins, lands):
    x, y, c, chips = _place()
    to = [(cx, cy, c) for cx, cy in chips] + [(cx, cy, 1 - c) for cx, cy in chips] + [(x, y, 1 - c)]
    out = []
    for a in range(len(ins)):
        half = ins[a].shape[1] // 2
        for i, (tx, ty, tc) in enumerate(to):
            out.append((_half_rows(ins[a], 2 * tx + ty, tc, half, 8), lands[a].at[i], (tx, ty, tc)))
    return out


def _gather_finish(shards, lands, name):
    n = len(shards)
    halves = [a.shape[0] // 2 for a in shards]
    aligns = [_row_align(a.dtype) for a in shards]

    def body(*refs):
        outs, (send_sems, recv_sems, _) = refs[n:2 * n], refs[2 * n:]
        x, y, c, chips = _place()
        passed = []
        for a in range(n):
            for j, (cx, cy) in enumerate(chips):
                landed = _half_rows(outs[a], 2 * cx + cy, c, halves[a], aligns[a])
                cp = pltpu.make_async_remote_copy(src_ref=landed, dst_ref=landed, send_sem=send_sems.at[3 * a + j],
                                                  recv_sem=recv_sems.at[3 * a + j], device_id=(x, y, 1 - c),
                                                  device_id_type=MESH)
                cp.start()
                passed.append(cp)
        for a in range(n):
            for j, (cx, cy) in enumerate(chips):
                other = _half_rows(outs[a], 2 * cx + cy, 1 - c, halves[a], aligns[a])
                pltpu.make_async_remote_copy(src_ref=other, dst_ref=other, send_sem=send_sems.at[3 * a + j],
                                             recv_sem=recv_sems.at[3 * a + j], device_id=(x, y, 1 - c),
                                             device_id_type=MESH).wait_recv()
        for cp in passed:
            cp.wait_send()

    lands = pl.pallas_call(
        body, name=name, in_specs=[ANY] * n, out_specs=[ANY] * n,
        out_shape=[jax.ShapeDtypeStruct(a.shape, a.dtype) for a in lands],
        input_output_aliases={i: i for i in range(n)}, scratch_shapes=_sems(3 * n),
        compiler_params=pltpu.CompilerParams(has_side_effects=True),
    )(*lands)
    return _with_own(lands, shards)


def _sum_own_and_landed(owns, landeds, where, name):
    n = len(owns)
    _, half, cols = landeds[0].shape
    tr = _row_tile(half, 128)
    nt = half // tr

    grid_spec = pltpu.PrefetchScalarGridSpec(
        num_scalar_prefetch=1, grid=(nt,),
        in_specs=[pl.BlockSpec((1, tr, cols), lambda r, w: (w[0], w[1] * nt + r, 0))] * n
        + [pl.BlockSpec((PEERS, tr, cols), lambda r, w: (0, r, 0))] * n,
        out_specs=[pl.BlockSpec((tr, cols), lambda r, w: (w[1] * nt + r, 0))] * n)

    def body(w_ref, *refs):
        for p_ref, q_ref, o_ref in zip(refs[:n], refs[n:2 * n], refs[2 * n:]):
            acc = p_ref[0]
            for i in range(PEERS):
                acc = acc + q_ref[i]
            o_ref[...] = acc

    return pl.pallas_call(
        body, name=name, grid_spec=grid_spec, out_shape=[jax.ShapeDtypeStruct((2 * half, cols), owns[0].dtype)] * n,
        compiler_params=_cp(("parallel",), VMEM_LIMIT),
    )(where, *owns, *landeds)


BIG = [("w_in", (D, IN_W), 1), ("w_q_up", (QL, HEADS * QK), 1), ("w_kv_up", (KVL, HEADS * (NOPE + VH)), 1),
       ("w_out", (D, D), 0), ("w_gate", (D, HID), 1), ("w_up", (D, HID), 1), ("w_down", (HID, D), 0)]
SMALL = [("g_mix_norm", (D,)), ("g_q_lat", (QL,)), ("g_kv_lat", (KVL,)), ("g_q_head", (QK,)), ("g_k_head", (QK,)),
         ("g_sgu_v", (SGU,)), ("w_spatial", (HEADS, CHUNK, CHUNK)), ("b_spatial", (HEADS, CHUNK)),
         ("w_pool", (4, 64, 64)), ("pool_scale", (POOL,)), ("g_out_mla", (512,)), ("g_out_sgu", (SGU,)),
         ("g_out_pool", (POOL,)), ("g_ffn_norm", (D,))]
ORDER = ["g_mix_norm", "w_in", "g_q_lat", "w_q_up", "g_kv_lat", "w_kv_up", "g_q_head", "g_k_head", "g_sgu_v",
         "w_spatial", "b_spatial", "w_pool", "pool_scale", "g_out_mla", "g_out_sgu", "g_out_pool", "w_out",
         "g_ffn_norm", "w_gate", "w_up", "w_down"]
EARLY_BIG = ["w_in", "w_q_up", "w_kv_up"]
FFN_BIG = ["w_gate", "w_up", "w_down"]
LATE_BIG = ["w_out"] + FFN_BIG
DEPTH = 2
COLS = 1024
SMALL_N = sum(math.prod(s) for _, s in SMALL) * DEPTH
assert SMALL_N % CHIPS == 0
SMALL_ROWS = -(-(SMALL_N // CHIPS + 1) // (16 * COLS)) * 16


def _unsplit_cols(g):
    return g.transpose(1, 0, 2).reshape(g.shape[1], CHIPS * g.shape[2])


def _split_cols(full):
    r, c = full.shape
    return full.reshape(r, CHIPS, c // CHIPS).transpose(1, 0, 2)


def _kernel_weights(g):
    win = _unsplit_cols(g["w_in"])
    zeros = lambda r, c: jnp.zeros((r, c), BF16)
    o2, o3, o4 = QL + KVL, QL + KVL + ROPE, QL + KVL + ROPE + 2 * SGU
    win_p = jnp.concatenate([win[:, :o2], zeros(D, NOPE), win[:, o2:o3], zeros(D, HP - QK), win[:, o3:o4], win[:, o4:]], axis=1)
    wq = _unsplit_cols(g["w_q_up"]).reshape(QL, HEADS, QK)
    wq_p = jnp.pad(wq, ((0, 0), (0, 0), (0, HP - QK))).reshape(QL, HEADS * HP)
    wkv = _unsplit_cols(g["w_kv_up"]).reshape(KVL, HEADS, NOPE + VH)
    wk_p = jnp.pad(wkv[:, :, :NOPE], ((0, 0), (0, 0), (0, HP - NOPE))).reshape(KVL, HEADS * HP)
    wv_p = wkv[:, :, NOPE:].reshape(KVL, HEADS * VH)
    return dict(win=win_p, wq=wq_p, wk=wk_p, wv=wv_p)


def _small_operands(p, l):
    row = lambda v: v.reshape(1, -1)
    pad = lambda v: jnp.pad(v, (0, HP - QK)).reshape(1, HP)
    wpool = p["w_pool"][l]
    wbd = jnp.zeros((POOL, POOL), F32)
    for g in range(4):
        wbd = lax.dynamic_update_slice(wbd, wpool[g], (g * 64, g * 64))
    return dict(
        g_mix=row(p["g_mix_norm"][l]), gql=row(p["g_q_lat"][l]), gkv=row(p["g_kv_lat"][l]),
        gq=pad(p["g_q_head"][l]), gk=pad(p["g_k_head"][l]), gsv=row(p["g_sgu_v"][l]),
        wsp=p["w_spatial"][l], bsp=jnp.repeat(p["b_spatial"][l].T, SGU // HEADS, axis=1),
        wbd=wbd.astype(BF16), psc=row(p["pool_scale"][l]),
        gout=jnp.concatenate([p["g_out_mla"][l], p["g_out_sgu"][l], p["g_out_pool"][l]]).reshape(1, D),
        g_ffn=row(p["g_ffn_norm"][l]))


def _big_grads(g):
    dwin = g["win"]
    o2 = QL + KVL
    gin = jnp.concatenate([dwin[:, :o2], dwin[:, o2 + NOPE:o2 + NOPE + ROPE], dwin[:, 512:]], axis=1)
    gq = g["wq"].reshape(QL, HEADS, HP)[:, :, :QK].reshape(QL, HEADS * QK)
    gk = g["wk"].reshape(KVL, HEADS, HP)[:, :, :NOPE]
    gv = g["wv"].reshape(KVL, HEADS, VH)
    gkv = jnp.concatenate([gk, gv], axis=2).reshape(KVL, HEADS * (NOPE + VH))
    return {"w_in": _split_cols(gin), "w_q_up": _split_cols(gq), "w_kv_up": _split_cols(gkv),
            "w_out": g["wout"].reshape(CHIPS, D // CHIPS, D), "w_gate": g["wg"], "w_up": g["wu"], "w_down": g["wd"]}


TRANSPOSED = ("w_gate", "w_up")


def _small_grads(g):
    go = g["gout"].reshape(-1)
    return {"g_mix_norm": g["g_mix"].reshape(-1), "g_q_lat": g["gql"].reshape(-1), "g_kv_lat": g["gkv"].reshape(-1),
            "g_q_head": g["gq"].reshape(-1)[:QK], "g_k_head": g["gk"].reshape(-1)[:QK], "g_sgu_v": g["gsv"].reshape(-1),
            "w_spatial": g["wsp"], "b_spatial": g["bsp"].reshape(CHUNK, HEADS, SGU // HEADS).sum(-1).T,
            "w_pool": jnp.stack([g["wbd"][i * 64:(i + 1) * 64, i * 64:(i + 1) * 64] for i in range(4)]),
            "pool_scale": g["psc"].reshape(-1), "g_out_mla": go[:512], "g_out_sgu": go[512:768],
            "g_out_pool": go[768:], "g_ffn_norm": g["g_ffn"].reshape(-1)}


def _pack_small_grads(small, loss):
    sm = jnp.concatenate([small[l][n].reshape(-1) for l in range(DEPTH) for n, _ in SMALL]).reshape(CHIPS, SMALL_N // CHIPS)
    sm = jnp.pad(sm, ((0, 0), (0, SMALL_ROWS * COLS - SMALL_N // CHIPS)))
    return sm.at[0, SMALL_N // CHIPS].set(loss).reshape(CHIPS, SMALL_ROWS, COLS)


def _unpack_small_grads(gathered):
    rows = gathered.reshape(CHIPS, SMALL_ROWS * COLS)
    loss = rows[0, SMALL_N // CHIPS]
    flat = rows[:, :SMALL_N // CHIPS].reshape(-1)
    out, off = [], 0
    for _ in range(DEPTH):
        layer = {}
        for n, shape in SMALL:
            k = math.prod(shape)
            layer[n] = flat[off:off + k].reshape(shape)
            off += k
        out.append(layer)
    return out, loss


def _layer_fwd(x, tabs, kw, late_weights, sp, l, tgt):
    t = f"_l{l}"
    z, hb = _in_proj_fwd(x, sp["g_mix"], kw["win"], "in_proj_fwd" + t)
    q, k, v = _mla_prep_fwd(z, tabs, sp["gql"], sp["gkv"], sp["gq"], sp["gk"], kw["wq"], kw["wk"], kw["wv"],
                            "mla_prep_fwd" + t)
    o, lse = _attn_fwd(q, k, v, "attn_fwd" + t)
    m = _pool_win_fwd(z, "pool_win_fwd" + t)
    wout, wg, wu, wd = late_weights(o)
    wout = wout.reshape(D, D)
    x1, mix = _mix_out_fwd(o, z, m, x, sp["wsp"], sp["bsp"], sp["wbd"], sp["psc"], sp["gsv"], sp["gout"], wout,
                           "mix_out_fwd" + t)
    x2, a, b, h2 = _ffn_fwd(x1, sp["g_ffn"], wg, wu, wd, tgt, "ffn_fwd" + t)
    saved = dict(x=x, z=z, hb=hb, q=q, k=k, v=v, o=o, lse=lse, m=m, x1=x1, mix=mix, a=a, b=b, h2=h2, wg=wg, wu=wu, wd=wd,
                 wout=wout)
    return x2, saved


def _layer_bwd(dx2, sv, tabs, kw, sp, l, ffn_hook, out_hook):
    t = f"_l{l}"
    g = {}
    dx1, hid, da, db, dyb, g["g_ffn"] = _ffn_bwd(dx2, sv["x1"], sv["a"], sv["b"], sp["g_ffn"], sv["wg"], sv["wu"],
                                                 sv["wd"], "ffn_bwd" + t)
    g["wd"] = _wgrad_rows(hid, dyb, "wgrad_down" + t)
    g["wg"] = _wgrad_rows(da, sv["h2"], "wgrad_gate" + t)
    g["wu"] = _wgrad_rows(db, sv["h2"], "wgrad_up" + t)
    gout = sp["gout"] + ffn_hook(g)
    do, delta, duv, dm, g["gout"], g["gsv"], g["psc"], g["wsp"], g["bsp"], g["wbd"] = _mix_out_bwd(
        dx1, sv["o"], sv["z"], sv["m"], sp["wsp"], sp["bsp"], sp["wbd"], sp["psc"], sp["gsv"], gout, sv["wout"],
        "mix_out_bwd" + t)
    g["wout"] = _wgrad(sv["mix"], dx1, "wgrad_out" + t)
    dp = _pool_win_bwd(dm, "pool_win_bwd" + t)
    dq, dk, dv = _attn_bwd(sv["q"], sv["k"], sv["v"], do, sv["lse"], delta, out_hook(g), "attn_bwd" + t)
    dzm, g["wq"], g["wk"], g["wv"], g["gql"], g["gkv"], g["gq"], g["gk"] = _mla_prep_bwd(
        dq, dk, dv, sv["z"], tabs, sp["gql"], sp["gkv"], sp["gq"], sp["gk"], kw["wq"], kw["wk"], kw["wv"],
        "mla_prep_bwd" + t)
    dx, g["g_mix"] = _in_proj_bwd(dzm, duv, dp, sv["x"], dx1, sp["g_mix"], kw["win"], "in_proj_bwd" + t)
    g["win"] = _wgrad_in(sv["hb"], dzm, duv, dp, "wgrad_in" + t)
    return dx, g


def _rope_inv_freq():
    half = ROPE // 2
    inv = 1.0 / (ROPE_THETA ** (jnp.arange(half, dtype=F32) / half))
    return jnp.concatenate([jnp.zeros((NOPE,), F32), inv, inv, jnp.zeros((HP - QK,), F32)]).reshape(1, HP)


def kernel(x, positions, g_mix_norm, w_in, g_q_lat, w_q_up, g_kv_lat, w_kv_up, g_q_head, g_k_head, g_sgu_v, w_spatial, b_spatial, w_pool, pool_scale, g_out_mla, g_out_sgu, g_out_pool, w_out, g_ffn_norm, w_gate, w_up, w_down, loss_target, m_g_mix_norm, m_w_in, m_g_q_lat, m_w_q_up, m_g_kv_lat, m_w_kv_up, m_g_q_head, m_g_k_head, m_g_sgu_v, m_w_spatial, m_b_spatial, m_w_pool, m_pool_scale, m_g_out_mla, m_g_out_sgu, m_g_out_pool, m_w_out, m_g_ffn_norm, m_w_gate, m_w_up, m_w_down, v_g_mix_norm, v_w_in, v_g_q_lat, v_w_q_up, v_g_kv_lat, v_w_kv_up, v_g_q_head, v_g_k_head, v_g_sgu_v, v_w_spatial, v_b_spatial, v_w_pool, v_pool_scale, v_g_out_mla, v_g_out_sgu, v_g_out_pool, v_w_out, v_g_ffn_norm, v_w_gate, v_w_up, v_w_down):
    given = dict(locals())
    p = {n: given[n] for n in ORDER}
    view = lambda pre, n: jnp.swapaxes(given[pre + n], 1, 2) if n in TRANSPOSED else given[pre + n]
    seq = x.shape[1]
    where = jnp.stack([2 * lax.axis_index("x") + lax.axis_index("y"), lax.axis_index("c")]).astype(jnp.int32)
    shards = lambda names: [view("", n)[l].astype(BF16) for l, n in names]
    zero11 = lambda token: token[:1, :1]

    names_0a = [(0, n) for n in EARLY_BIG]
    names_0b = [(0, n) for n in LATE_BIG]
    names_1 = [(1, n) for n, _, _ in BIG]
    got_0a = dict(zip(EARLY_BIG, _all_gather_chips(shards(names_0a), "all_gather_w0a")))
    started, issued = {}, got_0a["w_in"]
    for tag, names in (("w0b", names_0b), ("w1", names_1)):
        sh = shards(names)
        pairs = _gather_pairs([a.shape[0] // 2 for a in sh], [_row_align(a.dtype) for a in sh])
        lands = [jax.ShapeDtypeStruct((CHIPS,) + a.shape, a.dtype) for a in sh]
        started[tag] = (sh, pairs) + _split_start(sh, lands, 3 * len(sh), pairs, "gather_start_" + tag, issued)
        issued = started[tag][6]

    def arrived(tag, after):
        _, pairs, send, recv, srcs, lands, _ = started[tag]
        srcs, lands = _split_wait(send, recv, srcs, lands, after, pairs, "gather_wait_" + tag)
        return _gather_finish(srcs, lands, "gather_finish_" + tag)

    layer1 = {}

    def mix_weights(l, h):
        if l == 0:
            return got_0a
        layer1.update(zip([n for _, n in names_1], arrived("w1", h)))
        return layer1

    def late_weights(l, o):
        return arrived("w0b", o) if l == 0 else [layer1[n] for n in LATE_BIG]

    reducing, last = {}, {}

    def reduce_start(tag, arrs):
        lands = [jax.ShapeDtypeStruct((PEERS, a.shape[1] // 2, a.shape[2]), a.dtype) for a in arrs]
        reducing[tag] = _split_start(arrs, lands, PEERS * len(arrs), _scatter_pairs, "grad_scatter_start_" + tag, where)
        return zero11(reducing[tag][4])

    def reduce_finish(tag, after):
        send, recv, srcs, lands, _ = reducing[tag]
        srcs, lands = _split_wait(send, recv, srcs, lands, after, _scatter_pairs, "grad_scatter_wait_" + tag)
        sums = [None] * len(srcs)
        for shape in dict.fromkeys(a.shape for a in srcs):
            idx = [i for i, a in enumerate(srcs) if a.shape == shape]
            res = _sum_own_and_landed([srcs[i] for i in idx], [lands[i] for i in idx], where, f"grad_sum_{tag}_{idx[0]}")
            for i, r in zip(idx, res):
                sums[i] = r
        return sums

    def ffn_hook(l, g):
        if l == 1:
            return jnp.zeros((1, 1), F32)
        return reduce_start("g0b", [g["wg"], g["wu"], g["wd"]])

    def out_hook(l, g):
        if l == 1:
            return where
        reduce_start("g0c", [g["wout"].reshape(CHIPS, D // CHIPS, D)])
        return reducing["g0c"][4]

    def layer_hook(l, big, small):
        last[l] = (big, small)
        if l == 1:
            return reduce_start("g1", [big[n] for n, _, _ in BIG])
        return None

    entry = zero11(started["w0b"][6]) + zero11(started["w1"][6])
    loss_part, dx = _step(x.reshape(seq, D), positions.reshape(seq, 1), loss_target.reshape(seq, D), p, entry,
                          mix_weights, late_weights, ffn_hook, out_hook, layer_hook)

    def adamw(n, g0, g1):
        flip = n in EARLY_BIG
        pick = lambda pre: jnp.swapaxes(given[pre + n], 1, 2) if flip else view(pre, n)
        w = pick("")
        three_d = (DEPTH, -1, w.shape[-1])
        g0, g1 = (g.T if flip else g for g in (g0, g1))
        res = _adamw(w.reshape(three_d), g0.reshape(three_d[1:]), g1.reshape(three_d[1:]),
                     pick("m_").reshape(three_d), pick("v_").reshape(three_d), "adamw_" + n)
        return [jnp.swapaxes(r.reshape(w.shape), 1, 2) if flip else r.reshape(w.shape) for r in res]

    names_rest = [(0, n) for n in EARLY_BIG]
    reduce_start("g0a", [last[0][0][n] for _, n in names_rest]
                 + [_pack_small_grads([last[l][1] for l in range(DEPTH)], loss_part)])
    token = reducing["g0a"][4]
    early = names_1 + [(0, n) for n in FFN_BIG] + [(0, "w_out")]
    landed = reduce_finish("g1", token) + reduce_finish("g0b", token) + reduce_finish("g0c", token)
    sums = dict(zip(early, _pair_join(landed, "grad_pair_join_early")))
    out = {n: adamw(n, sums[(0, n)], sums[(1, n)]) for n in FFN_BIG}
    late = names_rest + ["small"]
    sums.update(zip(late, _pair_join(reduce_finish("g0a", out["w_down"][1]), "grad_pair_join_late")))
    gsmall, loss = _unpack_small_grads(_all_gather_chips([sums["small"]], "all_gather_small_grads")[0])
    for n in ORDER:
        if n not in out:
            g = [sums[(l, n)] for l in range(DEPTH)] if (0, n) in sums else [gsmall[l][n] for l in range(DEPTH)]
            out[n] = adamw(n, *g)
    undo = lambda n, a: jnp.swapaxes(a, 1, 2) if n in TRANSPOSED else a
    return (loss, dx.reshape(x.shape), *[undo(n, out[n][i]) for i in range(4) for n in ORDER])


def _step(xs, pos, tgt, p, entry, mix_weights, late_weights, ffn_hook, out_hook, layer_hook):
    sps = [_small_operands(p, l) for l in range(DEPTH)]
    sps[0]["g_mix"] = sps[0]["g_mix"] + entry
    tabs = _rope_tables(pos, _rope_inv_freq())
    saved, h = [], xs
    for l in range(DEPTH):
        kw = _kernel_weights(mix_weights(l, h))
        h, sv = _layer_fwd(h, tabs, kw, functools.partial(late_weights, l), sps[l], l, tgt if l == DEPTH - 1 else None)
        saved.append(dict(sv, kw=kw))
    dy, lpart = h
    for l in reversed(range(DEPTH)):
        dy, g = _layer_bwd(dy, saved[l], tabs, saved[l]["kw"], sps[l], l, functools.partial(ffn_hook, l),
                           functools.partial(out_hook, l))
        zero = layer_hook(l, _big_grads(g), _small_grads(g))
        if zero is not None and l > 0:
            sps[l - 1]["g_ffn"] = sps[l - 1]["g_ffn"] + zero
    return 0.5 / D * jnp.sum(lpart), dy
```

```python
import functools
import math

import jax
import jax.numpy as jnp
from jax import lax
from jax.experimental import pallas as pl
from jax.experimental.pallas import tpu as pltpu

F32 = jnp.float32
BF16 = jnp.bfloat16
MESH = pl.DeviceIdType.MESH

D = 1024
HEADS = 4
QK = 96
NOPE = 64
ROPE = 32
VH = 128
HP = 128
QL = 256
KVL = 128
SGU = 256
POOL = 256
CHUNK = 128
HID = 2816
CHIPS = 4
SH = HID // CHIPS
IN_W = 1184
IN_P = 1280
EPS = 1e-6
ROPE_THETA = 10000.0
SCALE = 1.0 / math.sqrt(QK)
LOG2E = 1.4426950408889634
EXP2_C = SCALE * LOG2E
ATT_WIDE = 2
ATT_FWD_QUERIES = 2048
ATT_PIECE = 1024
ATT_ROWS = 256
ATT_KEYS = 1024
ATT_QUERIES = 2048
NEG = -1e30
HALO = 16

LR, B1, B2, ADAM_EPS, WD, STEP = 0.001, 0.9, 0.999, 1e-08, 0.01, 10

VMEM_LIMIT = 56 * 1024 * 1024
LANES = 128
TOKENS = 1024


def _cp(sem, vmem=None):
    return pltpu.CompilerParams(dimension_semantics=sem, vmem_limit_bytes=vmem)


def _res(shape):
    nd = len(shape)
    return pl.BlockSpec(shape, lambda *_: (0,) * nd, pipeline_mode=pl.Buffered(1))


def _acc(shape):
    nd = len(shape)
    return pl.BlockSpec(shape, lambda *_: (0,) * nd)


def _dot(a, b):
    return jnp.dot(a, b, preferred_element_type=F32)


def _dot_nt(a, b):
    return lax.dot_general(a, b, (((1,), (1,)), ((), ())), preferred_element_type=F32)


def _dot_tn(a, b):
    return lax.dot_general(a, b, (((0,), (0,)), ((), ())), preferred_element_type=F32)


def _rms(x, n):
    r = lax.rsqrt(jnp.sum(x * x, axis=-1, keepdims=True) * (1.0 / n) + EPS)
    return x * r, r


def _head_ones():
    row = lax.broadcasted_iota(jnp.int32, (HEADS * HP, HEADS * HP), 0) // HP
    col = lax.broadcasted_iota(jnp.int32, (HEADS * HP, HEADS * HP), 1) // HP
    return (row == col).astype(BF16)


def _head_sum(x, ones):
    return _dot(x.astype(BF16), ones)


def _head_rms(x, ones):
    r = lax.rsqrt(_head_sum(x * x, ones) * (1.0 / QK) + EPS)
    return x * r, r


def _rms_bwd(xn, r, g, dy, n):
    dn = dy * g
    dx = r * (dn - xn * (jnp.sum(dn * xn, axis=-1, keepdims=True) * (1.0 / n)))
    return dx, jnp.sum(dy * xn, axis=0, keepdims=True)


def _accumulate(ref, val, first):
    @pl.when(first)
    def _():
        ref[...] = val

    @pl.when(jnp.logical_not(first))
    def _():
        ref[...] += val


def _accumulate0(ref, val, first):
    @pl.when(first)
    def _():
        ref[0] = val

    @pl.when(jnp.logical_not(first))
    def _():
        ref[0] += val


def _tile(s, t):
    return min(s, t)


def _row_tile(r, cap):
    if r <= cap:
        return r
    return max(t for t in range(8, cap + 1, 8) if r % t == 0)


def _rope_tables(pos, invf):
    s = pos.shape[0]
    tm = _tile(s, 1024)

    def body(pos_ref, invf_ref, c_ref, sa_ref, sb_ref):
        ang = pos_ref[...].astype(F32) * invf_ref[...]
        c, sn = jnp.cos(ang), jnp.sin(ang)
        lane = lax.broadcasted_iota(jnp.int32, ang.shape, 1)
        first = (lane >= NOPE) & (lane < NOPE + ROPE // 2)
        second = (lane >= NOPE + ROPE // 2) & (lane < QK)
        c_ref[...] = jnp.where(first | second, c, 1.0)
        sa_ref[...] = jnp.where(first, -sn, 0.0)
        sb_ref[...] = jnp.where(second, sn, 0.0)

    out = jax.ShapeDtypeStruct((s, HP), F32)
    return pl.pallas_call(
        body, name="rope_tables", grid=(s // tm,),
        in_specs=[pl.BlockSpec((tm, 1), lambda i: (i, 0)), _acc((1, HP))],
        out_specs=[pl.BlockSpec((tm, HP), lambda i: (i, 0))] * 3,
        out_shape=[out] * 3, compiler_params=_cp(("parallel",)),
    )(pos, invf)


def _rope(x, c, sa, sb):
    return x * c + pltpu.roll(x, HP - ROPE // 2, 1) * sa + pltpu.roll(x, ROPE // 2, 1) * sb


def _rope_t(d, c, sa, sb):
    return d * c + pltpu.roll(d * sa, ROPE // 2, 1) + pltpu.roll(d * sb, HP - ROPE // 2, 1)


def _in_proj_fwd(x, g, w, name):
    s = x.shape[0]
    tm = _tile(s, TOKENS)

    def body(x_ref, g_ref, w_ref, z_ref, h_ref):
        xn, _ = _rms(x_ref[...], D)
        h = (xn * g_ref[...]).astype(BF16)
        h_ref[...] = h
        z_ref[...] = _dot(h, w_ref[...])

    return pl.pallas_call(
        body, name=name, grid=(s // tm,),
        in_specs=[pl.BlockSpec((tm, D), lambda i: (i, 0)), _acc((1, D)), _res((D, IN_P))],
        out_specs=[pl.BlockSpec((tm, IN_P), lambda i: (i, 0)), pl.BlockSpec((tm, D), lambda i: (i, 0))],
        out_shape=[jax.ShapeDtypeStruct((s, IN_P), F32), jax.ShapeDtypeStruct((s, D), BF16)],
        compiler_params=_cp(("parallel",), VMEM_LIMIT),
    )(x, g, w)


def _mla_prep_fwd(z, tabs, gql, gkv, gq, gk, wq, wk, wv, name):
    s = z.shape[0]
    tm = _tile(s, TOKENS)

    def body(ql_ref, kv_ref, kr_ref, c_ref, sa_ref, sb_ref, gql_ref, gkv_ref, gq_ref, gk_ref,
             wq_ref, wk_ref, wv_ref, q_out, k_out, v_out):
        qn = (_rms(ql_ref[...], QL)[0] * gql_ref[...]).astype(BF16)
        kvn = (_rms(kv_ref[...], KVL)[0] * gkv_ref[...]).astype(BF16)
        qraw = _dot(qn, wq_ref[...])
        kraw = _dot(kvn, wk_ref[...])
        vraw = _dot(kvn, wv_ref[...])
        kr = kr_ref[...]
        c, sa, sb = c_ref[...], sa_ref[...], sb_ref[...]
        ones = _head_ones()
        xq_all = _head_rms(qraw, ones)[0]
        xk_all = _head_rms(kraw + jnp.concatenate([kr] * HEADS, axis=1), ones)[0]
        for h in range(HEADS):
            sl = slice(h * HP, (h + 1) * HP)
            q_out[h] = (_rope(xq_all[:, sl] * gq_ref[...], c, sa, sb) * EXP2_C).astype(BF16)
            k_out[h] = _rope(xk_all[:, sl] * gk_ref[...], c, sa, sb).astype(BF16)
            v_out[h] = vraw[:, sl].astype(BF16)

    row = lambda w, j: pl.BlockSpec((tm, w), lambda i: (i, j))
    hspec = pl.BlockSpec((HEADS, tm, HP), lambda i: (0, i, 0))
    hshape = jax.ShapeDtypeStruct((HEADS, s, HP), BF16)
    return pl.pallas_call(
        body, name=name, grid=(s // tm,),
        in_specs=[row(QL, 0), row(KVL, 2), row(HP, 3), row(HP, 0), row(HP, 0), row(HP, 0),
                  _acc((1, QL)), _acc((1, KVL)), _acc((1, HP)), _acc((1, HP)),
                  _acc((QL, HEADS * HP)), _acc((KVL, HEADS * HP)), _acc((KVL, HEADS * HP))],
        out_specs=[hspec] * 3, out_shape=[hshape] * 3,
        compiler_params=_cp(("parallel",)),
    )(z, z, z, *tabs, gql, gkv, gq, gk, wq, wk, wv)


def _causal_mask(s, row0):
    row = lax.broadcasted_iota(jnp.int32, s.shape, 0) + row0
    col = lax.broadcasted_iota(jnp.int32, s.shape, 1)
    return jnp.where(col <= row, s, NEG)


def _attn_fwd(q, k, v, name):
    s = q.shape[1]
    tq = _tile(s, ATT_FWD_QUERIES)
    rh = _tile(s, ATT_ROWS)
    kp = _tile(s, ATT_PIECE)
    wide = ATT_WIDE * kp if s % (ATT_WIDE * kp) == 0 else tq
    groups = tq // rh

    def body(q_ref, k_ref, v_ref, o_ref, lse_ref):
        i = pl.program_id(1)

        def blk(off, tk, carry, diagonal):
            width = lambda g, t: max(0, min(kp, (g + 1) * rh - t * kp)) if diagonal else kp
            rows = lambda t: pl.ds(pl.multiple_of(off + t * kp, kp), kp)
            score = lambda g, t: _dot_nt(q_ref[0, g * rh:(g + 1) * rh, :], k_ref[0, rows(t), :][:width(g, t)])
            live = lambda t: [g for g in range(groups) if width(g, t) > 0]
            state = list(carry)
            scs = {(g, 0): score(g, 0) for g in live(0)}
            for t in range(tk // kp):
                if (t + 1) * kp < tk:
                    scs.update({(g, t + 1): score(g, t + 1) for g in live(t + 1)})
                vt = v_ref[0, rows(t), :]
                for g in live(t):
                    m, l, acc = state[g]
                    sc = scs.pop((g, t))
                    if diagonal and (g + 1) * rh <= (t + 1) * kp:
                        sc = _causal_mask(sc, g * rh - t * kp)
                    m_new = jnp.maximum(m, jnp.max(sc, axis=-1, keepdims=True))
                    p = jnp.exp2(sc - m_new)
                    alpha = jnp.exp2(m - m_new)
                    l = alpha * l + jnp.sum(p, axis=-1, keepdims=True)
                    acc = alpha * acc + _dot(p.astype(BF16), vt[:width(g, t)])
                    state[g] = (m_new, l, acc)
            return tuple(state)

        one = (jnp.full((rh, 1), NEG, F32), jnp.zeros((rh, 1), F32), jnp.zeros((rh, VH), F32))
        nwide = (i * tq) // wide
        carry = lax.fori_loop(0, nwide, lambda j, c: blk(j * wide, wide, c, False), (one,) * groups)
        carry = lax.fori_loop(nwide * (wide // tq), i, lambda j, c: blk(j * tq, tq, c, False), carry)
        carry = blk(i * tq, tq, carry, True)
        for g, (m, l, acc) in enumerate(carry):
            o_ref[g * rh:(g + 1) * rh, :] = acc / l
            lse_ref[0, g * rh:(g + 1) * rh, :] = jnp.broadcast_to(m + jnp.log(l) * LOG2E, (rh, LANES))

    return pl.pallas_call(
        body, name=name, grid=(HEADS, s // tq),
        in_specs=[pl.BlockSpec((1, tq, HP), lambda h, i: (h, i, 0)),
                  pl.BlockSpec((1, s, HP), lambda h, i: (h, 0, 0)),
                  pl.BlockSpec((1, s, HP), lambda h, i: (h, 0, 0))],
        out_specs=[pl.BlockSpec((tq, VH), lambda h, i: (i, h)),
                   pl.BlockSpec((1, tq, LANES), lambda h, i: (h, i, 0))],
        out_shape=[jax.ShapeDtypeStruct((s, HEADS * VH), F32), jax.ShapeDtypeStruct((HEADS, s, LANES), F32)],
        compiler_params=_cp(("parallel", "arbitrary"), VMEM_LIMIT),
    )(q, k, v)


def _lane_group(shape, j):
    return (lax.broadcasted_iota(jnp.int32, shape, 1) + j * LANES) // (POOL // 4)


def _pool_win_fwd(z, name):
    s = z.shape[0]
    ch = _tile(s, 512)
    col0 = (IN_P - POOL) // LANES

    def body(p_ref, m_ref):
        j = pl.program_id(0)

        def chunk(r, _):
            off = pl.multiple_of(r * ch, ch)
            cur = p_ref[pl.ds(off, ch), :]
            hoff = pl.multiple_of(jnp.maximum(off - HALO, 0), 8)
            halo = jnp.where(r > 0, p_ref[pl.ds(hoff, HALO), :], 0.0)
            x = jnp.concatenate([halo, cur], axis=0)
            s2 = x + pltpu.roll(x, 1, 0)
            s4 = s2 + pltpu.roll(s2, 2, 0)
            s8 = s4 + pltpu.roll(s4, 4, 0)
            s16 = s8 + pltpu.roll(s8, 8, 0)
            grp = _lane_group((ch, LANES), j)
            sel = jnp.where(grp == 0, s2[HALO:], jnp.where(grp == 1, s4[HALO:], jnp.where(grp == 2, s8[HALO:], s16[HALO:])))
            t1 = (lax.broadcasted_iota(jnp.int32, (ch, LANES), 0) + off + 1).astype(F32)
            win = jnp.where(grp == 0, 2.0, jnp.where(grp == 1, 4.0, jnp.where(grp == 2, 8.0, 16.0)))
            m_ref[pl.ds(off, ch), :] = sel / jnp.minimum(t1, win) - cur
            return 0

        lax.fori_loop(0, s // ch, chunk, 0)

    return pl.pallas_call(
        body, name=name, grid=(POOL // LANES,),
        in_specs=[pl.BlockSpec((s, LANES), lambda j: (0, col0 + j))],
        out_specs=pl.BlockSpec((s, LANES), lambda j: (0, j)),
        out_shape=jax.ShapeDtypeStruct((s, POOL), F32),
        compiler_params=_cp(("parallel",), VMEM_LIMIT),
    )(z)


def _pool_win_bwd(dm, name):
    s = dm.shape[0]
    ch = _tile(s, 512)
    n = s // ch

    def body(dm_ref, dp_ref):
        j = pl.program_id(0)

        def chunk(r, _):
            off = pl.multiple_of(r * ch, ch)
            grp = _lane_group((ch + HALO, LANES), j)
            win = jnp.where(grp == 0, 2.0, jnp.where(grp == 1, 4.0, jnp.where(grp == 2, 8.0, 16.0)))
            cur = dm_ref[pl.ds(off, ch), :]
            hoff = pl.multiple_of(jnp.minimum(off + ch, s - HALO), 8)
            halo = jnp.where(r < n - 1, dm_ref[pl.ds(hoff, HALO), :], 0.0)
            x = jnp.concatenate([cur, halo], axis=0)
            t1 = (lax.broadcasted_iota(jnp.int32, (ch + HALO, LANES), 0) + off + 1).astype(F32)
            e = x / jnp.minimum(t1, win)
            tot = ch + HALO
            r2 = e + pltpu.roll(e, tot - 1, 0)
            r4 = r2 + pltpu.roll(r2, tot - 2, 0)
            r8 = r4 + pltpu.roll(r4, tot - 4, 0)
            r16 = r8 + pltpu.roll(r8, tot - 8, 0)
            g = grp[:ch]
            sel = jnp.where(g == 0, r2[:ch], jnp.where(g == 1, r4[:ch], jnp.where(g == 2, r8[:ch], r16[:ch])))
            dp_ref[pl.ds(off, ch), :] = (sel - cur).astype(BF16)
            return 0

        lax.fori_loop(0, n, chunk, 0)

    return pl.pallas_call(
        body, name=name, grid=(POOL // LANES,),
        in_specs=[pl.BlockSpec((s, LANES), lambda j: (0, j))],
        out_specs=pl.BlockSpec((s, LANES), lambda j: (0, j)),
        out_shape=jax.ShapeDtypeStruct((s, POOL), BF16),
        compiler_params=_cp(("parallel",), VMEM_LIMIT),
    )(dm)


def _head_mask(h):
    lane = lax.broadcasted_iota(jnp.int32, (CHUNK, SGU), 1)
    return (lane // (SGU // HEADS)) == h


def _tril(upper=False):
    row = lax.broadcasted_iota(jnp.int32, (CHUNK, CHUNK), 0)
    col = lax.broadcasted_iota(jnp.int32, (CHUNK, CHUNK), 1)
    return col >= row if upper else col <= row


def _sgu_gate(vn, wsp, bsp):
    out = []
    for cidx in range(vn.shape[0] // CHUNK):
        vc = vn[cidx * CHUNK:(cidx + 1) * CHUNK]
        zc = bsp
        for h in range(HEADS):
            zc = zc + jnp.where(_head_mask(h), _dot(wsp[h], vc), 0.0)
        out.append(zc)
    return jnp.concatenate(out, axis=0)


def _mix_out_fwd(o, z, m, x, wsp, bsp, wbd, psc, gsv, gout, wout, name):
    s = x.shape[0]
    tm = _tile(s, TOKENS)

    def body(o_ref, uv_ref, m_ref, x_ref, wsp_ref, bsp_ref, wbd_ref, psc_ref, gsv_ref, gout_ref, wout_ref,
             x1_ref, mix_ref):
        g = gout_ref[...]
        an = _rms(o_ref[...], HEADS * VH)[0] * g[:, :512]
        uv = uv_ref[...]
        u, v = uv[:, :SGU], uv[:, SGU:]
        vn = (_rms(v, SGU)[0] * gsv_ref[...]).astype(BF16)
        tri = _tril()
        wsp_m = [jnp.where(tri, wsp_ref[h], 0.0).astype(BF16) for h in range(HEADS)]
        gm = u * _sgu_gate(vn, wsp_m, bsp_ref[...])
        gn = _rms(gm, SGU)[0] * g[:, 512:768]
        po = _dot(m_ref[...].astype(BF16), wbd_ref[...]) * psc_ref[...]
        pn = _rms(po, POOL)[0] * g[:, 768:]
        mix = jnp.concatenate([an, gn, pn], axis=1).astype(BF16)
        mix_ref[...] = mix
        x1_ref[...] = x_ref[...] + _dot(mix, wout_ref[...])

    row = lambda w, j: pl.BlockSpec((tm, w), lambda i: (i, j))
    return pl.pallas_call(
        body, name=name, grid=(s // tm,),
        in_specs=[row(512, 0), row(512, 1), row(POOL, 0), row(D, 0),
                  _acc((HEADS, CHUNK, CHUNK)), _acc((CHUNK, SGU)), _acc((POOL, POOL)), _acc((1, POOL)),
                  _acc((1, SGU)), _acc((1, D)), _res((D, D))],
        out_specs=[row(D, 0), row(D, 0)],
        out_shape=[jax.ShapeDtypeStruct((s, D), F32), jax.ShapeDtypeStruct((s, D), BF16)],
        compiler_params=_cp(("parallel",), VMEM_LIMIT),
    )(o, z, m, x, wsp, bsp, wbd, psc, gsv, gout, wout)


def _ffn_fwd(x1, g, wg, wu, wd, tgt, name):
    s = x1.shape[0]
    tm = _tile(s, 256)
    last = tgt is not None

    def body(x_ref, g_ref, wg_ref, wu_ref, wd_ref, *rest):
        t_ref = rest[0] if last else None
        outs = rest[1:] if last else rest
        a_ref, b_ref, h_ref = outs[-3:]
        x = x_ref[...]
        h = (_rms(x, D)[0] * g_ref[...]).astype(BF16)
        h_ref[...] = h
        acc = jnp.zeros((tm, D), F32)
        for k in range(CHIPS):
            a = _dot_nt(h, wg_ref[k])
            b = _dot_nt(h, wu_ref[k])
            a_ref[k] = a
            b_ref[k] = b
            acc = acc + _dot((a * jax.nn.sigmoid(a) * b).astype(BF16), wd_ref[k])
        if not last:
            outs[0][...] = x + acc
            return
        dy_ref, l_ref = outs[:2]
        e = (x + acc) - t_ref[...]
        dy_ref[...] = e * (1.0 / D)
        sq = jnp.sum(e * e, axis=0, keepdims=True)
        part = sq[:, :LANES]
        for c in range(1, D // LANES):
            part = part + sq[:, c * LANES:(c + 1) * LANES]
        _accumulate(l_ref, part, pl.program_id(0) == 0)

    row = lambda w: pl.BlockSpec((tm, w), lambda i: (i, 0))
    hrow = pl.BlockSpec((CHIPS, tm, SH), lambda i: (0, i, 0))
    hshape = jax.ShapeDtypeStruct((CHIPS, s, SH), F32)
    tail_specs = [hrow, hrow, row(D)]
    tail_shapes = [hshape, hshape, jax.ShapeDtypeStruct((s, D), BF16)]
    head_specs = [row(D), _acc((1, LANES))] if last else [row(D)]
    head_shapes = [jax.ShapeDtypeStruct((s, D), F32)] + ([jax.ShapeDtypeStruct((1, LANES), F32)] if last else [])
    res = pl.pallas_call(
        body, name=name, grid=(s // tm,),
        in_specs=[row(D), _acc((1, D)), _res((CHIPS, SH, D)), _res((CHIPS, SH, D)), _res((CHIPS, SH, D))]
        + ([row(D)] if last else []),
        out_specs=head_specs + tail_specs, out_shape=head_shapes + tail_shapes,
        compiler_params=_cp(("arbitrary",), VMEM_LIMIT),
    )(x1, g, wg, wu, wd, *([tgt] if last else []))
    return (tuple(res[:2]) if last else res[0]), res[-3], res[-2], res[-1]


def _wgrad(a, b, name):
    s, k = a.shape
    n = b.shape[1]
    half = lambda v: v if v <= 1408 else v // 2
    kb, nb, tt = half(k), half(n), _tile(s, 2048)

    def body(a_ref, b_ref, o_ref):
        _accumulate(o_ref, _dot_tn(a_ref[...].astype(BF16), b_ref[...].astype(BF16)), pl.program_id(2) == 0)

    return pl.pallas_call(
        body, name=name, grid=(k // kb, n // nb, s // tt),
        in_specs=[pl.BlockSpec((tt, kb), lambda i, j, t: (t, i)), pl.BlockSpec((tt, nb), lambda i, j, t: (t, j))],
        out_specs=pl.BlockSpec((kb, nb), lambda i, j, t: (i, j)),
        out_shape=jax.ShapeDtypeStruct((k, n), F32),
        compiler_params=_cp(("parallel", "parallel", "arbitrary"), VMEM_LIMIT),
    )(a, b)


def _wgrad_in(h, dzm, duv, dp, name):
    s = h.shape[0]
    tt = _tile(s, 2048)

    def body(h_ref, a_ref, b_ref, c_ref, o_ref):
        hv = h_ref[...]
        val = jnp.concatenate([_dot_tn(hv, a_ref[...]), _dot_tn(hv, b_ref[...]), _dot_tn(hv, c_ref[...])], axis=1)
        _accumulate(o_ref, val, pl.program_id(0) == 0)

    row = lambda w: pl.BlockSpec((tt, w), lambda t: (t, 0))
    return pl.pallas_call(
        body, name=name, grid=(s // tt,), in_specs=[row(D), row(512), row(512), row(POOL)], out_specs=_acc((D, IN_P)),
        out_shape=jax.ShapeDtypeStruct((D, IN_P), F32), compiler_params=_cp(("arbitrary",), VMEM_LIMIT),
    )(h, dzm, duv, dp)


def _wgrad_rows(a, b, name):
    s, n = a.shape[1:]
    nn = b.shape[1]
    tt = _tile(s, 4096 if b.dtype == BF16 else 2048)

    def body(a_ref, b_ref, o_ref):
        _accumulate0(o_ref, _dot_tn(a_ref[0].astype(BF16), b_ref[...].astype(BF16)), pl.program_id(1) == 0)

    return pl.pallas_call(
        body, name=name, grid=(CHIPS, s // tt),
        in_specs=[pl.BlockSpec((1, tt, n), lambda c, t: (c, t, 0)), pl.BlockSpec((tt, nn), lambda c, t: (t, 0))],
        out_specs=pl.BlockSpec((1, n, nn), lambda c, t: (c, 0, 0)),
        out_shape=jax.ShapeDtypeStruct((CHIPS, n, nn), F32),
        compiler_params=_cp(("parallel", "arbitrary"), VMEM_LIMIT),
    )(a, b)


def _ffn_bwd(dx2, x1, a, b, g, wg, wu, wd, name):
    s = x1.shape[0]
    tm = _tile(s, 256)

    def body(dx2_ref, x_ref, a_ref, b_ref, g_ref, wg_ref, wu_ref, wd_ref,
             dx1_ref, hid_ref, da_ref, db_ref, dyb_ref, dg_ref):
        dx2 = dx2_ref[...]
        dyb = dx2.astype(BF16)
        dyb_ref[...] = dyb
        dh = jnp.zeros((tm, D), F32)
        ahead = _dot_nt(dyb, wd_ref[0])
        for k in range(CHIPS):
            av, bv = a_ref[k], b_ref[k]
            dhid = ahead
            if k + 1 < CHIPS:
                ahead = _dot_nt(dyb, wd_ref[k + 1])
            sig = jax.nn.sigmoid(av)
            sa = av * sig
            hid_ref[k] = (sa * bv).astype(BF16)
            dbv = (dhid * sa).astype(BF16)
            dav = (dhid * bv * (sig * (1.0 + av * (1.0 - sig)))).astype(BF16)
            db_ref[k] = dbv
            da_ref[k] = dav
            dh = dh + _dot(dav, wg_ref[k]) + _dot(dbv, wu_ref[k])
        xn, r = _rms(x_ref[...], D)
        dxr, dg = _rms_bwd(xn, r, g_ref[...], dh, D)
        dx1_ref[...] = dx2 + dxr
        _accumulate(dg_ref, dg, pl.program_id(0) == 0)

    row = lambda w: pl.BlockSpec((tm, w), lambda i: (i, 0))
    hrow = pl.BlockSpec((CHIPS, tm, SH), lambda i: (0, i, 0))
    hid = jax.ShapeDtypeStruct((CHIPS, s, SH), BF16)
    return pl.pallas_call(
        body, name=name, grid=(s // tm,),
        in_specs=[row(D), row(D), hrow, hrow, _acc((1, D)), _res((CHIPS, SH, D)), _res((CHIPS, SH, D)),
                  _res((CHIPS, SH, D))],
        out_specs=[row(D), hrow, hrow, hrow, row(D), _acc((1, D))],
        out_shape=[jax.ShapeDtypeStruct((s, D), F32), hid, hid, hid, jax.ShapeDtypeStruct((s, D), BF16),
                   jax.ShapeDtypeStruct((1, D), F32)],
        compiler_params=_cp(("arbitrary",), VMEM_LIMIT),
    )(dx2, x1, a, b, g, wg, wu, wd)


def _mix_out_bwd(dx1, o, z, m, wsp, bsp, wbd, psc, gsv, gout, wout, name):
    s = dx1.shape[0]
    tm = _tile(s, TOKENS)

    def body(dx1_ref, o_ref, uv_ref, m_ref, wsp_ref, bsp_ref, wbd_ref, psc_ref, gsv_ref, gout_ref, wout_ref,
             do_ref, dl_ref, duv_ref, dm_ref, dgo_ref, dgsv_ref, dpsc_ref, dwsp_ref, dbsp_ref, dwbd_ref):
        first = pl.program_id(0) == 0
        g = gout_ref[...]
        dmix = _dot_nt(dx1_ref[...].astype(BF16), wout_ref[...])
        o = o_ref[...]
        on, ro = _rms(o, HEADS * VH)
        do, dga = _rms_bwd(on, ro, g[:, :512], dmix[:, :512], HEADS * VH)
        for h in range(HEADS):
            sl = slice(h * VH, (h + 1) * VH)
            do_ref[h] = do[:, sl].astype(BF16)
            dl_ref[h] = jnp.broadcast_to(jnp.sum(do[:, sl] * o[:, sl], axis=-1, keepdims=True), (tm, LANES))
        uv = uv_ref[...]
        u, v = uv[:, :SGU], uv[:, SGU:]
        vx, rv = _rms(v, SGU)
        vn = (vx * gsv_ref[...]).astype(BF16)
        tri = _tril()
        wsp_m = [jnp.where(tri, wsp_ref[h], 0.0).astype(BF16) for h in range(HEADS)]
        zc = _sgu_gate(vn, wsp_m, bsp_ref[...])
        gm = u * zc
        gmn, rg = _rms(gm, SGU)
        dgm, dgg = _rms_bwd(gmn, rg, g[:, 512:768], dmix[:, 512:768], SGU)
        du = dgm * zc
        dzc = dgm * u
        dvn_parts = []
        dbsp = jnp.zeros((CHUNK, SGU), F32)
        dwsp = [jnp.zeros((CHUNK, CHUNK), F32) for _ in range(HEADS)]
        for cidx in range(tm // CHUNK):
            rs = slice(cidx * CHUNK, (cidx + 1) * CHUNK)
            dzc_c = dzc[rs]
            dbsp = dbsp + dzc_c
            dzb = dzc_c.astype(BF16)
            vc = vn[rs]
            dvn_c = jnp.zeros((CHUNK, SGU), F32)
            for h in range(HEADS):
                hm = _head_mask(h)
                dvn_c = dvn_c + jnp.where(hm, _dot_tn(wsp_m[h], dzb), 0.0)
                dwsp[h] = dwsp[h] + _dot_nt(jnp.where(hm, dzc_c, 0.0).astype(BF16), vc)
            dvn_parts.append(dvn_c)
        dvn = jnp.concatenate(dvn_parts, axis=0)
        dv, dgsv = _rms_bwd(vx, rv, gsv_ref[...], dvn, SGU)
        duv_ref[...] = jnp.concatenate([du, dv], axis=1).astype(BF16)
        mb = m_ref[...].astype(BF16)
        pw = _dot(mb, wbd_ref[...])
        po = pw * psc_ref[...]
        pon, rp = _rms(po, POOL)
        dpo, dgp = _rms_bwd(pon, rp, g[:, 768:], dmix[:, 768:], POOL)
        dpw = (dpo * psc_ref[...]).astype(BF16)
        dm_ref[...] = _dot_nt(dpw, wbd_ref[...])
        _accumulate(dgo_ref, jnp.concatenate([dga, dgg, dgp], axis=1), first)
        _accumulate(dgsv_ref, dgsv, first)
        _accumulate(dpsc_ref, jnp.sum(dpo * pw, axis=0, keepdims=True), first)
        _accumulate(dbsp_ref, dbsp, first)
        _accumulate(dwbd_ref, _dot_tn(mb, dpw), first)
        for h in range(HEADS):
            val = jnp.where(tri, dwsp[h], 0.0)

            @pl.when(first)
            def _(val=val, h=h):
                dwsp_ref[h] = val

            @pl.when(jnp.logical_not(first))
            def _(val=val, h=h):
                dwsp_ref[h] += val

    row = lambda w, j: pl.BlockSpec((tm, w), lambda i: (i, j))
    hspec = pl.BlockSpec((HEADS, tm, HP), lambda i: (0, i, 0))
    return pl.pallas_call(
        body, name=name, grid=(s // tm,),
        in_specs=[row(D, 0), row(512, 0), row(512, 1), row(POOL, 0),
                  _acc((HEADS, CHUNK, CHUNK)), _acc((CHUNK, SGU)),
                  _acc((POOL, POOL)), _acc((1, POOL)), _acc((1, SGU)), _acc((1, D)), _res((D, D))],
        out_specs=[hspec, hspec, row(512, 0), row(POOL, 0), _acc((1, D)), _acc((1, SGU)), _acc((1, POOL)),
                   _acc((HEADS, CHUNK, CHUNK)), _acc((CHUNK, SGU)), _acc((POOL, POOL))],
        out_shape=[jax.ShapeDtypeStruct((HEADS, s, HP), BF16), jax.ShapeDtypeStruct((HEADS, s, LANES), F32),
                   jax.ShapeDtypeStruct((s, 512), BF16), jax.ShapeDtypeStruct((s, POOL), F32),
                   jax.ShapeDtypeStruct((1, D), F32), jax.ShapeDtypeStruct((1, SGU), F32),
                   jax.ShapeDtypeStruct((1, POOL), F32), jax.ShapeDtypeStruct((HEADS, CHUNK, CHUNK), F32),
                   jax.ShapeDtypeStruct((CHUNK, SGU), F32), jax.ShapeDtypeStruct((POOL, POOL), F32)],
        compiler_params=_cp(("arbitrary",), VMEM_LIMIT),
    )(dx1, o, z, m, wsp, bsp, wbd, psc, gsv, gout, wout)


def _attn_bwd(q, k, v, do, lse, delta, after, name):
    s = q.shape[1]
    rh = _tile(s, ATT_ROWS)
    tk = _tile(s, ATT_KEYS)
    nk = s // tk
    wide = ATT_QUERIES if s % ATT_QUERIES == 0 else tk
    pieces = tk // rh

    def body(q_ref, k_ref, v_ref, do_ref, lse_ref, dl_ref, after_ref, dq_ref, dk_ref, dv_ref):
        del after_ref
        j = pl.program_id(1)

        @pl.when(j == 0)
        def _():
            dq_ref[...] = jnp.zeros_like(dq_ref)

        kj, vj = k_ref[0], v_ref[0]

        def blk(start, rows, dks, dvs, diagonal):
            dks, dvs = list(dks), list(dvs)
            offs = [pl.multiple_of(start + g * rh, rh) for g in range(rows // rh)]
            keys = [(g + 1) * rh if diagonal else tk for g in range(rows // rh)]
            qs = [q_ref[0, pl.ds(off, rh), :] for off in offs]
            dos = [do_ref[0, pl.ds(off, rh), :] for off in offs]
            scs = [_dot_nt(qi, kj[:n]) for qi, n in zip(qs, keys)]
            dps = [_dot_nt(doi, vj[:n]) for doi, n in zip(dos, keys)]
            for g, off in enumerate(offs):
                lse_i = lse_ref[0, pl.ds(off, rh), :][:, :1]
                dl_i = dl_ref[0, pl.ds(off, rh), :][:, :1]
                sc = _causal_mask(scs[g], g * rh) if diagonal else scs[g]
                p = jnp.exp2(sc - lse_i)
                ds = (p * (dps[g] - dl_i)).astype(BF16)
                cv = _dot_tn(p.astype(BF16), dos[g])
                ck = _dot_tn(ds, qs[g])
                for t in range(keys[g] // rh):
                    dvs[t] = dvs[t] + cv[t * rh:(t + 1) * rh]
                    dks[t] = dks[t] + ck[t * rh:(t + 1) * rh]
                dq_ref[0, pl.ds(off, rh), :] += _dot(ds, kj[:keys[g]]) * SCALE
            return tuple(dks), tuple(dvs)

        per = wide // tk
        zero = (jnp.zeros((rh, HP), F32),) * pieces
        acc = blk(j * tk, tk, zero, zero, True)
        first_wide = (j + per) // per
        acc = lax.fori_loop(j + 1, jnp.minimum(first_wide * per, nk), lambda i, c: blk(i * tk, tk, *c, False), acc)
        dks, dvs = lax.fori_loop(first_wide, nk // per, lambda i, c: blk(i * wide, wide, *c, False), acc)
        dk_ref[0] = jnp.concatenate(dks, axis=0) * (SCALE / EXP2_C)
        dv_ref[0] = jnp.concatenate(dvs, axis=0)

    full = lambda: pl.BlockSpec((1, s, HP), lambda h, j: (h, 0, 0))
    blk_spec = lambda: pl.BlockSpec((1, tk, HP), lambda h, j: (h, j, 0))
    out = jax.ShapeDtypeStruct((HEADS, s, HP), F32)
    return pl.pallas_call(
        body, name=name, grid=(HEADS, s // tk),
        in_specs=[full(), blk_spec(), blk_spec(), full(), full(), full(), ANY],
        out_specs=[full(), blk_spec(), blk_spec()], out_shape=[out] * 3,
        compiler_params=_cp(("parallel", "arbitrary"), VMEM_LIMIT),
    )(q, k, v, do, lse, delta, after)


def _mla_prep_bwd(dq, dk, dv, z, tabs, gql, gkv, gq, gk, wq, wk, wv, name):
    s = z.shape[0]
    tm = _tile(s, TOKENS)

    def body(dq_ref, dk_ref, dv_ref, ql_ref, kv_ref, kr_ref, c_ref, sa_ref, sb_ref, gql_ref, gkv_ref, gq_ref, gk_ref,
             wq_ref, wk_ref, wv_ref,
             dz_ref, dwq_ref, dwk_ref, dwv_ref, dgql_ref, dgkv_ref, dgq_ref, dgk_ref, dqr_ref, dkr_ref, dvr_ref):
        first = pl.program_id(0) == 0
        qx, rq = _rms(ql_ref[...], QL)
        qn = (qx * gql_ref[...]).astype(BF16)
        kx, rk = _rms(kv_ref[...], KVL)
        kvn = (kx * gkv_ref[...]).astype(BF16)
        qraw = _dot(qn, wq_ref[...])
        kraw = _dot(kvn, wk_ref[...])
        kr = kr_ref[...]
        c, sa, sb = c_ref[...], sa_ref[...], sb_ref[...]
        lane = lax.broadcasted_iota(jnp.int32, (tm, HP), 1)
        rope_lanes = (lane >= NOPE) & (lane < QK)
        ones = _head_ones()
        heads = lambda f: jnp.concatenate([f(h) for h in range(HEADS)], axis=1)
        fold = lambda v: sum(v[:, h * HP:(h + 1) * HP] for h in range(HEADS))

        def head_rms_bwd(x, g_ref, d_ref):
            xn, r = _head_rms(x, ones)
            dy = heads(lambda h: _rope_t(d_ref[h], c, sa, sb))
            dn = dy * jnp.concatenate([g_ref[...]] * HEADS, axis=1)
            dx = r * (dn - xn * (_head_sum(dn * xn, ones) * (1.0 / QK)))
            return dx, fold(jnp.sum(dy * xn, axis=0, keepdims=True))

        dxq, dgq = head_rms_bwd(qraw, gq_ref, dq_ref)
        dxk, dgk = head_rms_bwd(kraw + jnp.concatenate([kr] * HEADS, axis=1), gk_ref, dk_ref)
        dqr_ref[...] = dxq.astype(BF16)
        dkr_ref[...] = dxk.astype(BF16)
        dvr_ref[...] = heads(lambda h: dv_ref[h]).astype(BF16)
        dkrope = jnp.where(rope_lanes, fold(dxk), 0.0)
        dqn = _dot_nt(dqr_ref[...], wq_ref[...])
        dql, dgql = _rms_bwd(qx, rq, gql_ref[...], dqn, QL)
        dkvn = _dot_nt(dkr_ref[...], wk_ref[...]) + _dot_nt(dvr_ref[...], wv_ref[...])
        dkv, dgkv = _rms_bwd(kx, rk, gkv_ref[...], dkvn, KVL)
        dz_ref[...] = jnp.concatenate([dql, dkv, dkrope], axis=1).astype(BF16)
        _accumulate(dwq_ref, _dot_tn(qn, dqr_ref[...]), first)
        _accumulate(dwk_ref, _dot_tn(kvn, dkr_ref[...]), first)
        _accumulate(dwv_ref, _dot_tn(kvn, dvr_ref[...]), first)
        _accumulate(dgql_ref, dgql, first)
        _accumulate(dgkv_ref, dgkv, first)
        _accumulate(dgq_ref, dgq, first)
        _accumulate(dgk_ref, dgk, first)

    row = lambda w, j: pl.BlockSpec((tm, w), lambda i: (i, j))
    hspec = pl.BlockSpec((HEADS, tm, HP), lambda i: (0, i, 0))
    acc = lambda r, c: (_acc((r, c)), jax.ShapeDtypeStruct((r, c), F32))
    outs = [(row(512, 0), jax.ShapeDtypeStruct((s, 512), BF16)), acc(QL, HEADS * HP), acc(KVL, HEADS * HP),
            acc(KVL, HEADS * HP), acc(1, QL), acc(1, KVL), acc(1, HP), acc(1, HP)]
    return pl.pallas_call(
        body, name=name, grid=(s // tm,),
        in_specs=[hspec, hspec, hspec, row(QL, 0), row(KVL, 2), row(HP, 3), row(HP, 0), row(HP, 0), row(HP, 0),
                  _acc((1, QL)), _acc((1, KVL)), _acc((1, HP)), _acc((1, HP)),
                  _acc((QL, HEADS * HP)), _acc((KVL, HEADS * HP)), _acc((KVL, HEADS * HP))],
        out_specs=[o[0] for o in outs], out_shape=[o[1] for o in outs],
        scratch_shapes=[pltpu.VMEM((tm, HEADS * HP), BF16)] * 3,
        compiler_params=_cp(("arbitrary",), VMEM_LIMIT),
    )(dq, dk, dv, z, z, z, *tabs, gql, gkv, gq, gk, wq, wk, wv)


def _in_proj_bwd(dzm, duv, dp, x, dx1, g, win, name):
    s = x.shape[0]
    tm = _tile(s, TOKENS // 2)

    def body(dzm_ref, duv_ref, dp_ref, x_ref, dx1_ref, g_ref, w_ref, dx_ref, dg_ref):
        groups = [slice(r0, r0 + tm // 2) for r0 in (0, tm // 2)]
        dhs = [_dot_nt(dzm_ref[rs, :], w_ref[:, 0:512]) + _dot_nt(duv_ref[rs, :], w_ref[:, 512:1024])
               + _dot_nt(dp_ref[rs, :], w_ref[:, 1024:IN_P]) for rs in groups]
        dg = jnp.zeros((1, D), F32)
        for rs, dh in zip(groups, dhs):
            xn, r = _rms(x_ref[rs, :], D)
            dxr, dgr = _rms_bwd(xn, r, g_ref[...], dh, D)
            dx_ref[rs, :] = dx1_ref[rs, :] + dxr
            dg = dg + dgr
        _accumulate(dg_ref, dg, pl.program_id(0) == 0)

    row = lambda w: pl.BlockSpec((tm, w), lambda i: (i, 0))
    return pl.pallas_call(
        body, name=name, grid=(s // tm,),
        in_specs=[row(512), row(512), row(POOL), row(D), row(D), _acc((1, D)), _res((D, IN_P))],
        out_specs=[row(D), _acc((1, D))],
        out_shape=[jax.ShapeDtypeStruct((s, D), F32), jax.ShapeDtypeStruct((1, D), F32)],
        compiler_params=_cp(("arbitrary",), VMEM_LIMIT),
    )(dzm, duv, dp, x, dx1, g, win)


def _adamw(w, g0, g1, m, v, name):
    _, r, c = w.shape
    tr = _row_tile(r, 512)
    c1 = 1.0 - B1 ** STEP
    c2 = 1.0 - B2 ** STEP

    def body(w_ref, g0_ref, g1_ref, m_ref, v_ref, g_ref, d_ref, nm_ref, nv_ref):
        gv = jnp.where(pl.program_id(0) == 0, g0_ref[...], g1_ref[...])
        g_ref[0] = gv
        nm = B1 * m_ref[0] + (1.0 - B1) * gv
        nv = B2 * v_ref[0] + (1.0 - B2) * (gv * gv)
        nm_ref[0] = nm
        nv_ref[0] = nv
        d_ref[0] = -LR * ((nm / c1) / (jnp.sqrt(nv / c2) + ADAM_EPS) + WD * w_ref[0])

    spec = pl.BlockSpec((1, tr, c), lambda l, i: (l, i, 0))
    out = jax.ShapeDtypeStruct((DEPTH, r, c), F32)
    return pl.pallas_call(
        body, name=name, grid=(DEPTH, r // tr),
        in_specs=[spec, pl.BlockSpec((tr, c), lambda l, i: (i * (1 - l), 0)), pl.BlockSpec((tr, c), lambda l, i: (i * l, 0)),
                  spec, spec],
        out_specs=[spec] * 4, out_shape=[out] * 4, compiler_params=_cp(("parallel", "parallel")),
    )(w, g0, g1, m, v)


ANY = pl.BlockSpec(memory_space=pl.ANY)


def _place():
    x, y, c = lax.axis_index("x"), lax.axis_index("y"), lax.axis_index("c")
    chips = [(1 - x, y), (x, 1 - y), (1 - x, 1 - y)]
    return x, y, c, chips


def _half_rows(ref, lead, hh, half, align):
    rows = pl.ds(pl.multiple_of(hh * half, align), half)
    return ref.at[rows, :] if lead is None else ref.at[lead, rows, :]


def _row_align(dtype):
    return 16 if dtype == BF16 else 8


def _sems(n):
    return [pltpu.SemaphoreType.DMA((n,)), pltpu.SemaphoreType.DMA((n,)), pltpu.SemaphoreType.DMA((n,))]


def _comm_call(body, ins, out_shapes, nsems, name):
    return pl.pallas_call(
        body, name=name, in_specs=[ANY] * len(ins), out_specs=[ANY] * len(out_shapes), out_shape=out_shapes,
        scratch_shapes=_sems(nsems), compiler_params=pltpu.CompilerParams(has_side_effects=True),
    )(*ins)


def _all_gather_chips(shards, name):
    n = len(shards)
    halves = [a.shape[0] // 2 for a in shards]
    aligns = [_row_align(a.dtype) for a in shards]
    assert all(h % al == 0 for h, al in zip(halves, aligns))

    def body(*refs):
        ins, outs, (send_sems, recv_sems, _) = refs[:n], refs[n:2 * n], refs[2 * n:]
        x, y, c, chips = _place()
        me = 2 * x + y
        sibling = (x, y, 1 - c)

        def copy(sem, src, dst, to):
            return pltpu.make_async_remote_copy(src_ref=src, dst_ref=dst, send_sem=send_sems.at[sem],
                                                recv_sem=recv_sems.at[sem], device_id=to, device_id_type=MESH)

        first, passed = [], []
        for a in range(n):
            my_half = _half_rows(ins[a], None, c, halves[a], aligns[a])
            for j, (cx, cy) in enumerate(chips):
                cp = copy(6 * a + j, my_half, _half_rows(outs[a], me, c, halves[a], aligns[a]), (cx, cy, c))
                cp.start()
                first.append(cp)
        for a in range(n):
            for j, (cx, cy) in enumerate(chips):
                landed = _half_rows(outs[a], 2 * cx + cy, c, halves[a], aligns[a])
                copy(6 * a + j, landed, landed, (cx, cy, c)).wait_recv()
                fwd = copy(6 * a + 3 + j, landed, landed, sibling)
                fwd.start()
                passed.append(fwd)
        for a in range(n):
            for j, (cx, cy) in enumerate(chips):
                other = _half_rows(outs[a], 2 * cx + cy, 1 - c, halves[a], aligns[a])
                copy(6 * a + 3 + j, other, other, sibling).wait_recv()
        for cp in first + passed:
            cp.wait_send()

    lands = _comm_call(body, shards, [jax.ShapeDtypeStruct((CHIPS,) + a.shape, a.dtype) for a in shards], 6 * n, name)
    return _with_own(lands, shards)


def _with_own(lands, shards):
    me = 2 * lax.axis_index("x") + lax.axis_index("y")
    return [lax.dynamic_update_slice(g, a[None], (me, 0, 0)) for g, a in zip(lands, shards)]


def _pair_join(arrs, name):
    n = len(arrs)
    halves = [a.shape[0] // 2 for a in arrs]

    def body(*refs):
        outs, (send_sems, recv_sems, _) = refs[n:2 * n], refs[2 * n:]
        x, y, c, _ = _place()
        cps = []
        for a in range(n):
            mine = _half_rows(outs[a], None, c, halves[a], 8)
            cp = pltpu.make_async_remote_copy(src_ref=mine, dst_ref=mine, send_sem=send_sems.at[a], recv_sem=recv_sems.at[a],
                                              device_id=(x, y, 1 - c), device_id_type=MESH)
            cp.start()
            cps.append(cp)
        for cp in cps:
            cp.wait()

    return pl.pallas_call(
        body, name=name, in_specs=[ANY] * n, out_specs=[ANY] * n,
        out_shape=[jax.ShapeDtypeStruct(a.shape, a.dtype) for a in arrs],
        input_output_aliases={i: i for i in range(n)}, scratch_shapes=_sems(n),
        compiler_params=pltpu.CompilerParams(has_side_effects=True),
    )(*arrs)


HBM = pl.BlockSpec(memory_space=pltpu.HBM)
SEM = pl.BlockSpec(memory_space=pltpu.SEMAPHORE)
DATAFLOW = pltpu.SideEffectType.DATAFLOW_SIDE_EFFECTING


def _remote_copies(pairs, ins, lands, send_sems, recv_sems):
    return [pltpu.make_async_remote_copy(src_ref=src, dst_ref=dst, send_sem=send_sems.at[i], recv_sem=recv_sems.at[i],
                                         device_id=to, device_id_type=MESH)
            for i, (src, dst, to) in enumerate(pairs(ins, lands))]


def _split_start(srcs, land_shapes, ncopies, pairs, name, after):
    n, m = len(srcs), len(land_shapes)

    def body(*refs):
        ins, lands = refs[:n], refs[n:n + m]
        send_sems, recv_sems, token = refs[n + m + 1], refs[n + m + 2], refs[-1]
        for cp in _remote_copies(pairs, ins, lands, send_sems, recv_sems):
            cp.start()
        token[...] = jnp.zeros_like(token)

    hbm = lambda a: pltpu.with_memory_space_constraint(a, pltpu.HBM)
    lands = [hbm(lax.empty(s.shape, s.dtype)) for s in land_shapes]
    thru = [pltpu.HBM(a.shape, a.dtype) for a in list(srcs) + lands]
    out = pl.pallas_call(
        body, name=name,
        out_shape=(pltpu.SemaphoreType.DMA((ncopies,)), pltpu.SemaphoreType.DMA((ncopies,)), *thru,
                   jax.ShapeDtypeStruct((8, LANES), F32)),
        in_specs=[HBM] * (n + m) + [ANY], out_specs=(SEM, SEM, *[HBM] * (n + m), pl.BlockSpec(memory_space=pltpu.VMEM)),
        input_output_aliases={i: 2 + i for i in range(n + m)},
        compiler_params=pltpu.CompilerParams(has_side_effects=DATAFLOW),
    )(*[hbm(a) for a in srcs], *lands, after)
    return out[0], out[1], list(out[2:2 + n]), list(out[2 + n:2 + n + m]), out[-1]


def _split_wait(send_sems, recv_sems, srcs, lands, after, pairs, name):
    n, m = len(srcs), len(lands)

    def body(*refs):
        ins, lands_ = refs[:n], refs[n:n + m]
        for cp in _remote_copies(pairs, ins, lands_, refs[n + m], refs[n + m + 1]):
            cp.wait_send()
            cp.wait_recv()

    out = pl.pallas_call(
        body, name=name, out_shape=tuple(pltpu.HBM(a.shape, a.dtype) for a in list(srcs) + list(lands)),
        in_specs=[HBM] * (n + m) + [SEM, SEM, ANY], out_specs=tuple([HBM] * (n + m)),
        input_output_aliases={i: i for i in range(n + m)},
        compiler_params=pltpu.CompilerParams(has_side_effects=DATAFLOW),
    )(*srcs, *lands, send_sems, recv_sems, after)
    return list(out[:n]), list(out[n:])


def _gather_pairs(halves, aligns):
    def pairs(ins, lands):
        x, y, c, chips = _place()
        me = 2 * x + y
        return [(_half_rows(ins[a], None, c, halves[a], aligns[a]), _half_rows(lands[a], me, c, halves[a], aligns[a]),
                 (cx, cy, c)) for a in range(len(ins)) for cx, cy in chips]
    return pairs


PEERS = 7


def _scatter_pairs(ins, lands):
    x, y, c, chips = _place()
    to = [(cx, cy, c) for cx, cy in chips] + [(cx, cy, 1 - c) for cx, cy in chips] + [(x, y, 1 - c)]
    out = []
    for a in range(len(ins)):
        half = ins[a].shape[1] // 2
        for i, (tx, ty, tc) in enumerate(to):
            out.append((_half_rows(ins[a], 2 * tx + ty, tc, half, 8), lands[a].at[i], (tx, ty, tc)))
    return out


def _gather_finish(shards, lands, name):
    n = len(shards)
    halves = [a.shape[0] // 2 for a in shards]
    aligns = [_row_align(a.dtype) for a in shards]

    def body(*refs):
        outs, (send_sems, recv_sems, _) = refs[n:2 * n], refs[2 * n:]
        x, y, c, chips = _place()
        passed = []
        for a in range(n):
            for j, (cx, cy) in enumerate(chips):
                landed = _half_rows(outs[a], 2 * cx + cy, c, halves[a], aligns[a])
                cp = pltpu.make_async_remote_copy(src_ref=landed, dst_ref=landed, send_sem=send_sems.at[3 * a + j],
                                                  recv_sem=recv_sems.at[3 * a + j], device_id=(x, y, 1 - c),
                                                  device_id_type=MESH)
                cp.start()
                passed.append(cp)
        for a in range(n):
            for j, (cx, cy) in enumerate(chips):
                other = _half_rows(outs[a], 2 * cx + cy, 1 - c, halves[a], aligns[a])
                pltpu.make_async_remote_copy(src_ref=other, dst_ref=other, send_sem=send_sems.at[3 * a + j],
                                             recv_sem=recv_sems.at[3 * a + j], device_id=(x, y, 1 - c),
                                             device_id_type=MESH).wait_recv()
        for cp in passed:
            cp.wait_send()

    lands = pl.pallas_call(
        body, name=name, in_specs=[ANY] * n, out_specs=[ANY] * n,
        out_shape=[jax.ShapeDtypeStruct(a.shape, a.dtype) for a in lands],
        input_output_aliases={i: i for i in range(n)}, scratch_shapes=_sems(3 * n),
        compiler_params=pltpu.CompilerParams(has_side_effects=True),
    )(*lands)
    return _with_own(lands, shards)


def _sum_own_and_landed(owns, landeds, where, name):
    n = len(owns)
    _, half, cols = landeds[0].shape
    tr = _row_tile(half, 128)
    nt = half // tr

    grid_spec = pltpu.PrefetchScalarGridSpec(
        num_scalar_prefetch=1, grid=(nt,),
        in_specs=[pl.BlockSpec((1, tr, cols), lambda r, w: (w[0], w[1] * nt + r, 0))] * n
        + [pl.BlockSpec((PEERS, tr, cols), lambda r, w: (0, r, 0))] * n,
        out_specs=[pl.BlockSpec((tr, cols), lambda r, w: (w[1] * nt + r, 0))] * n)

    def body(w_ref, *refs):
        for p_ref, q_ref, o_ref in zip(refs[:n], refs[n:2 * n], refs[2 * n:]):
            acc = p_ref[0]
            for i in range(PEERS):
                acc = acc + q_ref[i]
            o_ref[...] = acc

    return pl.pallas_call(
        body, name=name, grid_spec=grid_spec, out_shape=[jax.ShapeDtypeStruct((2 * half, cols), owns[0].dtype)] * n,
        compiler_params=_cp(("parallel",), VMEM_LIMIT),
    )(where, *owns, *landeds)


BIG = [("w_in", (D, IN_W), 1), ("w_q_up", (QL, HEADS * QK), 1), ("w_kv_up", (KVL, HEADS * (NOPE + VH)), 1),
       ("w_out", (D, D), 0), ("w_gate", (D, HID), 1), ("w_up", (D, HID), 1), ("w_down", (HID, D), 0)]
SMALL = [("g_mix_norm", (D,)), ("g_q_lat", (QL,)), ("g_kv_lat", (KVL,)), ("g_q_head", (QK,)), ("g_k_head", (QK,)),
         ("g_sgu_v", (SGU,)), ("w_spatial", (HEADS, CHUNK, CHUNK)), ("b_spatial", (HEADS, CHUNK)),
         ("w_pool", (4, 64, 64)), ("pool_scale", (POOL,)), ("g_out_mla", (512,)), ("g_out_sgu", (SGU,)),
         ("g_out_pool", (POOL,)), ("g_ffn_norm", (D,))]
ORDER = ["g_mix_norm", "w_in", "g_q_lat", "w_q_up", "g_kv_lat", "w_kv_up", "g_q_head", "g_k_head", "g_sgu_v",
         "w_spatial", "b_spatial", "w_pool", "pool_scale", "g_out_mla", "g_out_sgu", "g_out_pool", "w_out",
         "g_ffn_norm", "w_gate", "w_up", "w_down"]
EARLY_BIG = ["w_in", "w_q_up", "w_kv_up"]
FFN_BIG = ["w_gate", "w_up", "w_down"]
LATE_BIG = ["w_out"] + FFN_BIG
DEPTH = 2
COLS = 1024
SMALL_N = sum(math.prod(s) for _, s in SMALL) * DEPTH
assert SMALL_N % CHIPS == 0
SMALL_ROWS = -(-(SMALL_N // CHIPS + 1) // (16 * COLS)) * 16


def _unsplit_cols(g):
    return g.transpose(1, 0, 2).reshape(g.shape[1], CHIPS * g.shape[2])


def _split_cols(full):
    r, c = full.shape
    return full.reshape(r, CHIPS, c // CHIPS).transpose(1, 0, 2)


def _kernel_weights(g):
    win = _unsplit_cols(g["w_in"])
    zeros = lambda r, c: jnp.zeros((r, c), BF16)
    o2, o3, o4 = QL + KVL, QL + KVL + ROPE, QL + KVL + ROPE + 2 * SGU
    win_p = jnp.concatenate([win[:, :o2], zeros(D, NOPE), win[:, o2:o3], zeros(D, HP - QK), win[:, o3:o4], win[:, o4:]], axis=1)
    wq = _unsplit_cols(g["w_q_up"]).reshape(QL, HEADS, QK)
    wq_p = jnp.pad(wq, ((0, 0), (0, 0), (0, HP - QK))).reshape(QL, HEADS * HP)
    wkv = _unsplit_cols(g["w_kv_up"]).reshape(KVL, HEADS, NOPE + VH)
    wk_p = jnp.pad(wkv[:, :, :NOPE], ((0, 0), (0, 0), (0, HP - NOPE))).reshape(KVL, HEADS * HP)
    wv_p = wkv[:, :, NOPE:].reshape(KVL, HEADS * VH)
    return dict(win=win_p, wq=wq_p, wk=wk_p, wv=wv_p)


def _small_operands(p, l):
    row = lambda v: v.reshape(1, -1)
    pad = lambda v: jnp.pad(v, (0, HP - QK)).reshape(1, HP)
    wpool = p["w_pool"][l]
    wbd = jnp.zeros((POOL, POOL), F32)
    for g in range(4):
        wbd = lax.dynamic_update_slice(wbd, wpool[g], (g * 64, g * 64))
    return dict(
        g_mix=row(p["g_mix_norm"][l]), gql=row(p["g_q_lat"][l]), gkv=row(p["g_kv_lat"][l]),
        gq=pad(p["g_q_head"][l]), gk=pad(p["g_k_head"][l]), gsv=row(p["g_sgu_v"][l]),
        wsp=p["w_spatial"][l], bsp=jnp.repeat(p["b_spatial"][l].T, SGU // HEADS, axis=1),
        wbd=wbd.astype(BF16), psc=row(p["pool_scale"][l]),
        gout=jnp.concatenate([p["g_out_mla"][l], p["g_out_sgu"][l], p["g_out_pool"][l]]).reshape(1, D),
        g_ffn=row(p["g_ffn_norm"][l]))


def _big_grads(g):
    dwin = g["win"]
    o2 = QL + KVL
    gin = jnp.concatenate([dwin[:, :o2], dwin[:, o2 + NOPE:o2 + NOPE + ROPE], dwin[:, 512:]], axis=1)
    gq = g["wq"].reshape(QL, HEADS, HP)[:, :, :QK].reshape(QL, HEADS * QK)
    gk = g["wk"].reshape(KVL, HEADS, HP)[:, :, :NOPE]
    gv = g["wv"].reshape(KVL, HEADS, VH)
    gkv = jnp.concatenate([gk, gv], axis=2).reshape(KVL, HEADS * (NOPE + VH))
    return {"w_in": _split_cols(gin), "w_q_up": _split_cols(gq), "w_kv_up": _split_cols(gkv),
            "w_out": g["wout"].reshape(CHIPS, D // CHIPS, D), "w_gate": g["wg"], "w_up": g["wu"], "w_down": g["wd"]}


TRANSPOSED = ("w_gate", "w_up")


def _small_grads(g):
    go = g["gout"].reshape(-1)
    return {"g_mix_norm": g["g_mix"].reshape(-1), "g_q_lat": g["gql"].reshape(-1), "g_kv_lat": g["gkv"].reshape(-1),
            "g_q_head": g["gq"].reshape(-1)[:QK], "g_k_head": g["gk"].reshape(-1)[:QK], "g_sgu_v": g["gsv"].reshape(-1),
            "w_spatial": g["wsp"], "b_spatial": g["bsp"].reshape(CHUNK, HEADS, SGU // HEADS).sum(-1).T,
            "w_pool": jnp.stack([g["wbd"][i * 64:(i + 1) * 64, i * 64:(i + 1) * 64] for i in range(4)]),
            "pool_scale": g["psc"].reshape(-1), "g_out_mla": go[:512], "g_out_sgu": go[512:768],
            "g_out_pool": go[768:], "g_ffn_norm": g["g_ffn"].reshape(-1)}


def _pack_small_grads(small, loss):
    sm = jnp.concatenate([small[l][n].reshape(-1) for l in range(DEPTH) for n, _ in SMALL]).reshape(CHIPS, SMALL_N // CHIPS)
    sm = jnp.pad(sm, ((0, 0), (0, SMALL_ROWS * COLS - SMALL_N // CHIPS)))
    return sm.at[0, SMALL_N // CHIPS].set(loss).reshape(CHIPS, SMALL_ROWS, COLS)


def _unpack_small_grads(gathered):
    rows = gathered.reshape(CHIPS, SMALL_ROWS * COLS)
    loss = rows[0, SMALL_N // CHIPS]
    flat = rows[:, :SMALL_N // CHIPS].reshape(-1)
    out, off = [], 0
    for _ in range(DEPTH):
        layer = {}
        for n, shape in SMALL:
            k = math.prod(shape)
            layer[n] = flat[off:off + k].reshape(shape)
            off += k
        out.append(layer)
    return out, loss


def _layer_fwd(x, tabs, kw, late_weights, sp, l, tgt):
    t = f"_l{l}"
    z, hb = _in_proj_fwd(x, sp["g_mix"], kw["win"], "in_proj_fwd" + t)
    q, k, v = _mla_prep_fwd(z, tabs, sp["gql"], sp["gkv"], sp["gq"], sp["gk"], kw["wq"], kw["wk"], kw["wv"],
                            "mla_prep_fwd" + t)
    o, lse = _attn_fwd(q, k, v, "attn_fwd" + t)
    m = _pool_win_fwd(z, "pool_win_fwd" + t)
    wout, wg, wu, wd = late_weights(o)
    wout = wout.reshape(D, D)
    x1, mix = _mix_out_fwd(o, z, m, x, sp["wsp"], sp["bsp"], sp["wbd"], sp["psc"], sp["gsv"], sp["gout"], wout,
                           "mix_out_fwd" + t)
    x2, a, b, h2 = _ffn_fwd(x1, sp["g_ffn"], wg, wu, wd, tgt, "ffn_fwd" + t)
    saved = dict(x=x, z=z, hb=hb, q=q, k=k, v=v, o=o, lse=lse, m=m, x1=x1, mix=mix, a=a, b=b, h2=h2, wg=wg, wu=wu, wd=wd,
                 wout=wout)
    return x2, saved


def _layer_bwd(dx2, sv, tabs, kw, sp, l, ffn_hook, out_hook):
    t = f"_l{l}"
    g = {}
    dx1, hid, da, db, dyb, g["g_ffn"] = _ffn_bwd(dx2, sv["x1"], sv["a"], sv["b"], sp["g_ffn"], sv["wg"], sv["wu"],
                                                 sv["wd"], "ffn_bwd" + t)
    g["wd"] = _wgrad_rows(hid, dyb, "wgrad_down" + t)
    g["wg"] = _wgrad_rows(da, sv["h2"], "wgrad_gate" + t)
    g["wu"] = _wgrad_rows(db, sv["h2"], "wgrad_up" + t)
    gout = sp["gout"] + ffn_hook(g)
    do, delta, duv, dm, g["gout"], g["gsv"], g["psc"], g["wsp"], g["bsp"], g["wbd"] = _mix_out_bwd(
        dx1, sv["o"], sv["z"], sv["m"], sp["wsp"], sp["bsp"], sp["wbd"], sp["psc"], sp["gsv"], gout, sv["wout"],
        "mix_out_bwd" + t)
    g["wout"] = _wgrad(sv["mix"], dx1, "wgrad_out" + t)
    dp = _pool_win_bwd(dm, "pool_win_bwd" + t)
    dq, dk, dv = _attn_bwd(sv["q"], sv["k"], sv["v"], do, sv["lse"], delta, out_hook(g), "attn_bwd" + t)
    dzm, g["wq"], g["wk"], g["wv"], g["gql"], g["gkv"], g["gq"], g["gk"] = _mla_prep_bwd(
        dq, dk, dv, sv["z"], tabs, sp["gql"], sp["gkv"], sp["gq"], sp["gk"], kw["wq"], kw["wk"], kw["wv"],
        "mla_prep_bwd" + t)
    dx, g["g_mix"] = _in_proj_bwd(dzm, duv, dp, sv["x"], dx1, sp["g_mix"], kw["win"], "in_proj_bwd" + t)
    g["win"] = _wgrad_in(sv["hb"], dzm, duv, dp, "wgrad_in" + t)
    return dx, g


def _rope_inv_freq():
    half = ROPE // 2
    inv = 1.0 / (ROPE_THETA ** (jnp.arange(half, dtype=F32) / half))
    return jnp.concatenate([jnp.zeros((NOPE,), F32), inv, inv, jnp.zeros((HP - QK,), F32)]).reshape(1, HP)


def kernel(x, positions, g_mix_norm, w_in, g_q_lat, w_q_up, g_kv_lat, w_kv_up, g_q_head, g_k_head, g_sgu_v, w_spatial, b_spatial, w_pool, pool_scale, g_out_mla, g_out_sgu, g_out_pool, w_out, g_ffn_norm, w_gate, w_up, w_down, loss_target, m_g_mix_norm, m_w_in, m_g_q_lat, m_w_q_up, m_g_kv_lat, m_w_kv_up, m_g_q_head, m_g_k_head, m_g_sgu_v, m_w_spatial, m_b_spatial, m_w_pool, m_pool_scale, m_g_out_mla, m_g_out_sgu, m_g_out_pool, m_w_out, m_g_ffn_norm, m_w_gate, m_w_up, m_w_down, v_g_mix_norm, v_w_in, v_g_q_lat, v_w_q_up, v_g_kv_lat, v_w_kv_up, v_g_q_head, v_g_k_head, v_g_sgu_v, v_w_spatial, v_b_spatial, v_w_pool, v_pool_scale, v_g_out_mla, v_g_out_sgu, v_g_out_pool, v_w_out, v_g_ffn_norm, v_w_gate, v_w_up, v_w_down):
    given = dict(locals())
    p = {n: given[n] for n in ORDER}
    view = lambda pre, n: jnp.swapaxes(given[pre + n], 1, 2) if n in TRANSPOSED else given[pre + n]
    seq = x.shape[1]
    where = jnp.stack([2 * lax.axis_index("x") + lax.axis_index("y"), lax.axis_index("c")]).astype(jnp.int32)
    shards = lambda names: [view("", n)[l].astype(BF16) for l, n in names]
    zero11 = lambda token: token[:1, :1]

    names_0a = [(0, n) for n in EARLY_BIG]
    names_0b = [(0, n) for n in LATE_BIG]
    names_1 = [(1, n) for n, _, _ in BIG]
    got_0a = dict(zip(EARLY_BIG, _all_gather_chips(shards(names_0a), "all_gather_w0a")))
    started, issued = {}, got_0a["w_in"]
    for tag, names in (("w0b", names_0b), ("w1", names_1)):
        sh = shards(names)
        pairs = _gather_pairs([a.shape[0] // 2 for a in sh], [_row_align(a.dtype) for a in sh])
        lands = [jax.ShapeDtypeStruct((CHIPS,) + a.shape, a.dtype) for a in sh]
        started[tag] = (sh, pairs) + _split_start(sh, lands, 3 * len(sh), pairs, "gather_start_" + tag, issued)
        issued = started[tag][6]

    def arrived(tag, after):
        _, pairs, send, recv, srcs, lands, _ = started[tag]
        srcs, lands = _split_wait(send, recv, srcs, lands, after, pairs, "gather_wait_" + tag)
        return _gather_finish(srcs, lands, "gather_finish_" + tag)

    layer1 = {}

    def mix_weights(l, h):
        if l == 0:
            return got_0a
        layer1.update(zip([n for _, n in names_1], arrived("w1", h)))
        return layer1

    def late_weights(l, o):
        return arrived("w0b", o) if l == 0 else [layer1[n] for n in LATE_BIG]

    reducing, last = {}, {}

    def reduce_start(tag, arrs):
        lands = [jax.ShapeDtypeStruct((PEERS, a.shape[1] // 2, a.shape[2]), a.dtype) for a in arrs]
        reducing[tag] = _split_start(arrs, lands, PEERS * len(arrs), _scatter_pairs, "grad_scatter_start_" + tag, where)
        return zero11(reducing[tag][4])

    def reduce_finish(tag, after):
        send, recv, srcs, lands, _ = reducing[tag]
        srcs, lands = _split_wait(send, recv, srcs, lands, after, _scatter_pairs, "grad_scatter_wait_" + tag)
        sums = [None] * len(srcs)
        for shape in dict.fromkeys(a.shape for a in srcs):
            idx = [i for i, a in enumerate(srcs) if a.shape == shape]
            res = _sum_own_and_landed([srcs[i] for i in idx], [lands[i] for i in idx], where, f"grad_sum_{tag}_{idx[0]}")
            for i, r in zip(idx, res):
                sums[i] = r
        return sums

    def ffn_hook(l, g):
        if l == 1:
            return jnp.zeros((1, 1), F32)
        return reduce_start("g0b", [g["wg"], g["wu"], g["wd"]])

    def out_hook(l, g):
        if l == 1:
            return where
        reduce_start("g0c", [g["wout"].reshape(CHIPS, D // CHIPS, D)])
        return reducing["g0c"][4]

    def layer_hook(l, big, small):
        last[l] = (big, small)
        if l == 1:
            return reduce_start("g1", [big[n] for n, _, _ in BIG])
        return None

    entry = zero11(started["w0b"][6]) + zero11(started["w1"][6])
    loss_part, dx = _step(x.reshape(seq, D), positions.reshape(seq, 1), loss_target.reshape(seq, D), p, entry,
                          mix_weights, late_weights, ffn_hook, out_hook, layer_hook)

    def adamw(n, g0, g1):
        flip = n in EARLY_BIG
        pick = lambda pre: jnp.swapaxes(given[pre + n], 1, 2) if flip else view(pre, n)
        w = pick("")
        three_d = (DEPTH, -1, w.shape[-1])
        g0, g1 = (g.T if flip else g for g in (g0, g1))
        res = _adamw(w.reshape(three_d), g0.reshape(three_d[1:]), g1.reshape(three_d[1:]),
                     pick("m_").reshape(three_d), pick("v_").reshape(three_d), "adamw_" + n)
        return [jnp.swapaxes(r.reshape(w.shape), 1, 2) if flip else r.reshape(w.shape) for r in res]

    names_rest = [(0, n) for n in EARLY_BIG]
    reduce_start("g0a", [last[0][0][n] for _, n in names_rest]
                 + [_pack_small_grads([last[l][1] for l in range(DEPTH)], loss_part)])
    token = reducing["g0a"][4]
    early = names_1 + [(0, n) for n in FFN_BIG] + [(0, "w_out")]
    landed = reduce_finish("g1", token) + reduce_finish("g0b", token) + reduce_finish("g0c", token)
    sums = dict(zip(early, _pair_join(landed, "grad_pair_join_early")))
    out = {n: adamw(n, sums[(0, n)], sums[(1, n)]) for n in FFN_BIG}
    late = names_rest + ["small"]
    sums.update(zip(late, _pair_join(reduce_finish("g0a", out["w_down"][1]), "grad_pair_join_late")))
    gsmall, loss = _unpack_small_grads(_all_gather_chips([sums["small"]], "all_gather_small_grads")[0])
    for n in ORDER:
        if n not in out:
            g = [sums[(l, n)] for l in range(DEPTH)] if (0, n) in sums else [gsmall[l][n] for l in range(DEPTH)]
            out[n] = adamw(n, *g)
    undo = lambda n, a: jnp.swapaxes(a, 1, 2) if n in TRANSPOSED else a
    return (loss, dx.reshape(x.shape), *[undo(n, out[n][i]) for i in range(4) for n in ORDER])


def _step(xs, pos, tgt, p, entry, mix_weights, late_weights, ffn_hook, out_hook, layer_hook):
    sps = [_small_operands(p, l) for l in range(DEPTH)]
    sps[0]["g_mix"] = sps[0]["g_mix"] + entry
    tabs = _rope_tables(pos, _rope_inv_freq())
    saved, h = [], xs
    for l in range(DEPTH):
        kw = _kernel_weights(mix_weights(l, h))
        h, sv = _layer_fwd(h, tabs, kw, functools.partial(late_weights, l), sps[l], l, tgt if l == DEPTH - 1 else None)
        saved.append(dict(sv, kw=kw))
    dy, lpart = h
    for l in reversed(range(DEPTH)):
        dy, g = _layer_bwd(dy, saved[l], tabs, saved[l]["kw"], sps[l], l, functools.partial(ffn_hook, l),
                           functools.partial(out_hook, l))
        zero = layer_hook(l, _big_grads(g), _small_grads(g))
        if zero is not None and l > 0:
            sps[l - 1]["g_ffn"] = sps[l - 1]["g_ffn"] + zero
    return 0.5 / D * jnp.sum(lpart), dy
```

```python
import functools
import math

import jax
import jax.numpy as jnp
from jax import lax
from jax.experimental import pallas as pl
from jax.experimental.pallas import tpu as pltpu

F32 = jnp.float32
BF16 = jnp.bfloat16
MESH = pl.DeviceIdType.MESH

D = 1024
HEADS = 4
QK = 96
NOPE = 64
ROPE = 32
VH = 128
HP = 128
QL = 256
KVL = 128
SGU = 256
POOL = 256
CHUNK = 128
HID = 2816
CHIPS = 4
SH = HID // CHIPS
IN_W = 1184
IN_P = 1280
EPS = 1e-6
ROPE_THETA = 10000.0
SCALE = 1.0 / math.sqrt(QK)
LOG2E = 1.4426950408889634
EXP2_C = SCALE * LOG2E
ATT_WIDE = 2
ATT_FWD_QUERIES = 2048
ATT_PIECE = 1024
ATT_ROWS = 256
ATT_KEYS = 1024
ATT_QUERIES = 2048
NEG = -1e30
HALO = 16

LR, B1, B2, ADAM_EPS, WD, STEP = 0.001, 0.9, 0.999, 1e-08, 0.01, 10

VMEM_LIMIT = 56 * 1024 * 1024
LANES = 128
TOKENS = 1024


def _cp(sem, vmem=None):
    return pltpu.CompilerParams(dimension_semantics=sem, vmem_limit_bytes=vmem)


def _res(shape):
    nd = len(shape)
    return pl.BlockSpec(shape, lambda *_: (0,) * nd, pipeline_mode=pl.Buffered(1))


def _acc(shape):
    nd = len(shape)
    return pl.BlockSpec(shape, lambda *_: (0,) * nd)


def _dot(a, b):
    return jnp.dot(a, b, preferred_element_type=F32)


def _dot_nt(a, b):
    return lax.dot_general(a, b, (((1,), (1,)), ((), ())), preferred_element_type=F32)


def _dot_tn(a, b):
    return lax.dot_general(a, b, (((0,), (0,)), ((), ())), preferred_element_type=F32)


def _rms(x, n):
    r = lax.rsqrt(jnp.sum(x * x, axis=-1, keepdims=True) * (1.0 / n) + EPS)
    return x * r, r


def _head_ones():
    row = lax.broadcasted_iota(jnp.int32, (HEADS * HP, HEADS * HP), 0) // HP
    col = lax.broadcasted_iota(jnp.int32, (HEADS * HP, HEADS * HP), 1) // HP
    return (row == col).astype(BF16)


def _head_sum(x, ones):
    return _dot(x.astype(BF16), ones)


def _head_rms(x, ones):
    r = lax.rsqrt(_head_sum(x * x, ones) * (1.0 / QK) + EPS)
    return x * r, r


def _rms_bwd(xn, r, g, dy, n):
    dn = dy * g
    dx = r * (dn - xn * (jnp.sum(dn * xn, axis=-1, keepdims=True) * (1.0 / n)))
    return dx, jnp.sum(dy * xn, axis=0, keepdims=True)


def _accumulate(ref, val, first):
    @pl.when(first)
    def _():
        ref[...] = val

    @pl.when(jnp.logical_not(first))
    def _():
        ref[...] += val


def _accumulate0(ref, val, first):
    @pl.when(first)
    def _():
        ref[0] = val

    @pl.when(jnp.logical_not(first))
    def _():
        ref[0] += val


def _tile(s, t):
    return min(s, t)


def _row_tile(r, cap):
    if r <= cap:
        return r
    return max(t for t in range(8, cap + 1, 8) if r % t == 0)


def _rope_tables(pos, invf):
    s = pos.shape[0]
    tm = _tile(s, 1024)

    def body(pos_ref, invf_ref, c_ref, sa_ref, sb_ref):
        ang = pos_ref[...].astype(F32) * invf_ref[...]
        c, sn = jnp.cos(ang), jnp.sin(ang)
        lane = lax.broadcasted_iota(jnp.int32, ang.shape, 1)
        first = (lane >= NOPE) & (lane < NOPE + ROPE // 2)
        second = (lane >= NOPE + ROPE // 2) & (lane < QK)
        c_ref[...] = jnp.where(first | second, c, 1.0)
        sa_ref[...] = jnp.where(first, -sn, 0.0)
        sb_ref[...] = jnp.where(second, sn, 0.0)

    out = jax.ShapeDtypeStruct((s, HP), F32)
    return pl.pallas_call(
        body, name="rope_tables", grid=(s // tm,),
        in_specs=[pl.BlockSpec((tm, 1), lambda i: (i, 0)), _acc((1, HP))],
        out_specs=[pl.BlockSpec((tm, HP), lambda i: (i, 0))] * 3,
        out_shape=[out] * 3, compiler_params=_cp(("parallel",)),
    )(pos, invf)


def _rope(x, c, sa, sb):
    return x * c + pltpu.roll(x, HP - ROPE // 2, 1) * sa + pltpu.roll(x, ROPE // 2, 1) * sb


def _rope_t(d, c, sa, sb):
    return d * c + pltpu.roll(d * sa, ROPE // 2, 1) + pltpu.roll(d * sb, HP - ROPE // 2, 1)


def _in_proj_fwd(x, g, w, name):
    s = x.shape[0]
    tm = _tile(s, TOKENS)

    def body(x_ref, g_ref, w_ref, z_ref, h_ref):
        xn, _ = _rms(x_ref[...], D)
        h = (xn * g_ref[...]).astype(BF16)
        h_ref[...] = h
        z_ref[...] = _dot(h, w_ref[...])

    return pl.pallas_call(
        body, name=name, grid=(s // tm,),
        in_specs=[pl.BlockSpec((tm, D), lambda i: (i, 0)), _acc((1, D)), _res((D, IN_P))],
        out_specs=[pl.BlockSpec((tm, IN_P), lambda i: (i, 0)), pl.BlockSpec((tm, D), lambda i: (i, 0))],
        out_shape=[jax.ShapeDtypeStruct((s, IN_P), F32), jax.ShapeDtypeStruct((s, D), BF16)],
        compiler_params=_cp(("parallel",), VMEM_LIMIT),
    )(x, g, w)


def _mla_prep_fwd(z, tabs, gql, gkv, gq, gk, wq, wk, wv, name):
    s = z.shape[0]
    tm = _tile(s, TOKENS)

    def body(ql_ref, kv_ref, kr_ref, c_ref, sa_ref, sb_ref, gql_ref, gkv_ref, gq_ref, gk_ref,
             wq_ref, wk_ref, wv_ref, q_out, k_out, v_out):
        qn = (_rms(ql_ref[...], QL)[0] * gql_ref[...]).astype(BF16)
        kvn = (_rms(kv_ref[...], KVL)[0] * gkv_ref[...]).astype(BF16)
        qraw = _dot(qn, wq_ref[...])
        kraw = _dot(kvn, wk_ref[...])
        vraw = _dot(kvn, wv_ref[...])
        kr = kr_ref[...]
        c, sa, sb = c_ref[...], sa_ref[...], sb_ref[...]
        ones = _head_ones()
        xq_all = _head_rms(qraw, ones)[0]
        xk_all = _head_rms(kraw + jnp.concatenate([kr] * HEADS, axis=1), ones)[0]
        for h in range(HEADS):
            sl = slice(h * HP, (h + 1) * HP)
            q_out[h] = (_rope(xq_all[:, sl] * gq_ref[...], c, sa, sb) * EXP2_C).astype(BF16)
            k_out[h] = _rope(xk_all[:, sl] * gk_ref[...], c, sa, sb).astype(BF16)
            v_out[h] = vraw[:, sl].astype(BF16)

    row = lambda w, j: pl.BlockSpec((tm, w), lambda i: (i, j))
    hspec = pl.BlockSpec((HEADS, tm, HP), lambda i: (0, i, 0))
    hshape = jax.ShapeDtypeStruct((HEADS, s, HP), BF16)
    return pl.pallas_call(
        body, name=name, grid=(s // tm,),
        in_specs=[row(QL, 0), row(KVL, 2), row(HP, 3), row(HP, 0), row(HP, 0), row(HP, 0),
                  _acc((1, QL)), _acc((1, KVL)), _acc((1, HP)), _acc((1, HP)),
                  _acc((QL, HEADS * HP)), _acc((KVL, HEADS * HP)), _acc((KVL, HEADS * HP))],
        out_specs=[hspec] * 3, out_shape=[hshape] * 3,
        compiler_params=_cp(("parallel",)),
    )(z, z, z, *tabs, gql, gkv, gq, gk, wq, wk, wv)


def _causal_mask(s, row0):
    row = lax.broadcasted_iota(jnp.int32, s.shape, 0) + row0
    col = lax.broadcasted_iota(jnp.int32, s.shape, 1)
    return jnp.where(col <= row, s, NEG)


def _attn_fwd(q, k, v, name):
    s = q.shape[1]
    tq = _tile(s, ATT_FWD_QUERIES)
    rh = _tile(s, ATT_ROWS)
    kp = _tile(s, ATT_PIECE)
    wide = ATT_WIDE * kp if s % (ATT_WIDE * kp) == 0 else tq
    groups = tq // rh

    def body(q_ref, k_ref, v_ref, o_ref, lse_ref):
        i = pl.program_id(1)

        def blk(off, tk, carry, diagonal):
            width = lambda g, t: max(0, min(kp, (g + 1) * rh - t * kp)) if diagonal else kp
            rows = lambda t: pl.ds(pl.multiple_of(off + t * kp, kp), kp)
            score = lambda g, t: _dot_nt(q_ref[0, g * rh:(g + 1) * rh, :], k_ref[0, rows(t), :][:width(g, t)])
            live = lambda t: [g for g in range(groups) if width(g, t) > 0]
            state = list(carry)
            scs = {(g, 0): score(g, 0) for g in live(0)}
            for t in range(tk // kp):
                if (t + 1) * kp < tk:
                    scs.update({(g, t + 1): score(g, t + 1) for g in live(t + 1)})
                vt = v_ref[0, rows(t), :]
                for g in live(t):
                    m, l, acc = state[g]
                    sc = scs.pop((g, t))
                    if diagonal and (g + 1) * rh <= (t + 1) * kp:
                        sc = _causal_mask(sc, g * rh - t * kp)
                    m_new = jnp.maximum(m, jnp.max(sc, axis=-1, keepdims=True))
                    p = jnp.exp2(sc - m_new)
                    alpha = jnp.exp2(m - m_new)
                    l = alpha * l + jnp.sum(p, axis=-1, keepdims=True)
                    acc = alpha * acc + _dot(p.astype(BF16), vt[:width(g, t)])
                    state[g] = (m_new, l, acc)
            return tuple(state)

        one = (jnp.full((rh, 1), NEG, F32), jnp.zeros((rh, 1), F32), jnp.zeros((rh, VH), F32))
        nwide = (i * tq) // wide
        carry = lax.fori_loop(0, nwide, lambda j, c: blk(j * wide, wide, c, False), (one,) * groups)
        carry = lax.fori_loop(nwide * (wide // tq), i, lambda j, c: blk(j * tq, tq, c, False), carry)
        carry = blk(i * tq, tq, carry, True)
        for g, (m, l, acc) in enumerate(carry):
            o_ref[g * rh:(g + 1) * rh, :] = acc / l
            lse_ref[0, g * rh:(g + 1) * rh, :] = jnp.broadcast_to(m + jnp.log(l) * LOG2E, (rh, LANES))

    return pl.pallas_call(
        body, name=name, grid=(HEADS, s // tq),
        in_specs=[pl.BlockSpec((1, tq, HP), lambda h, i: (h, i, 0)),
                  pl.BlockSpec((1, s, HP), lambda h, i: (h, 0, 0)),
                  pl.BlockSpec((1, s, HP), lambda h, i: (h, 0, 0))],
        out_specs=[pl.BlockSpec((tq, VH), lambda h, i: (i, h)),
                   pl.BlockSpec((1, tq, LANES), lambda h, i: (h, i, 0))],
        out_shape=[jax.ShapeDtypeStruct((s, HEADS * VH), F32), jax.ShapeDtypeStruct((HEADS, s, LANES), F32)],
        compiler_params=_cp(("parallel", "arbitrary"), VMEM_LIMIT),
    )(q, k, v)


def _lane_group(shape, j):
    return (lax.broadcasted_iota(jnp.int32, shape, 1) + j * LANES) // (POOL // 4)


def _pool_win_fwd(z, name):
    s = z.shape[0]
    ch = _tile(s, 512)
    col0 = (IN_P - POOL) // LANES

    def body(p_ref, m_ref):
        j = pl.program_id(0)

        def chunk(r, _):
            off = pl.multiple_of(r * ch, ch)
            cur = p_ref[pl.ds(off, ch), :]
            hoff = pl.multiple_of(jnp.maximum(off - HALO, 0), 8)
            halo = jnp.where(r > 0, p_ref[pl.ds(hoff, HALO), :], 0.0)
            x = jnp.concatenate([halo, cur], axis=0)
            s2 = x + pltpu.roll(x, 1, 0)
            s4 = s2 + pltpu.roll(s2, 2, 0)
            s8 = s4 + pltpu.roll(s4, 4, 0)
            s16 = s8 + pltpu.roll(s8, 8, 0)
            grp = _lane_group((ch, LANES), j)
            sel = jnp.where(grp == 0, s2[HALO:], jnp.where(grp == 1, s4[HALO:], jnp.where(grp == 2, s8[HALO:], s16[HALO:])))
            t1 = (lax.broadcasted_iota(jnp.int32, (ch, LANES), 0) + off + 1).astype(F32)
            win = jnp.where(grp == 0, 2.0, jnp.where(grp == 1, 4.0, jnp.where(grp == 2, 8.0, 16.0)))
            m_ref[pl.ds(off, ch), :] = sel / jnp.minimum(t1, win) - cur
            return 0

        lax.fori_loop(0, s // ch, chunk, 0)

    return pl.pallas_call(
        body, name=name, grid=(POOL // LANES,),
        in_specs=[pl.BlockSpec((s, LANES), lambda j: (0, col0 + j))],
        out_specs=pl.BlockSpec((s, LANES), lambda j: (0, j)),
        out_shape=jax.ShapeDtypeStruct((s, POOL), F32),
        compiler_params=_cp(("parallel",), VMEM_LIMIT),
    )(z)


def _pool_win_bwd(dm, name):
    s = dm.shape[0]
    ch = _tile(s, 512)
    n = s // ch

    def body(dm_ref, dp_ref):
        j = pl.program_id(0)

        def chunk(r, _):
            off = pl.multiple_of(r * ch, ch)
            grp = _lane_group((ch + HALO, LANES), j)
            win = jnp.where(grp == 0, 2.0, jnp.where(grp == 1, 4.0, jnp.where(grp == 2, 8.0, 16.0)))
            cur = dm_ref[pl.ds(off, ch), :]
            hoff = pl.multiple_of(jnp.minimum(off + ch, s - HALO), 8)
            halo = jnp.where(r < n - 1, dm_ref[pl.ds(hoff, HALO), :], 0.0)
            x = jnp.concatenate([cur, halo], axis=0)
            t1 = (lax.broadcasted_iota(jnp.int32, (ch + HALO, LANES), 0) + off + 1).astype(F32)
            e = x / jnp.minimum(t1, win)
            tot = ch + HALO
            r2 = e + pltpu.roll(e, tot - 1, 0)
            r4 = r2 + pltpu.roll(r2, tot - 2, 0)
            r8 = r4 + pltpu.roll(r4, tot - 4, 0)
            r16 = r8 + pltpu.roll(r8, tot - 8, 0)
            g = grp[:ch]
            sel = jnp.where(g == 0, r2[:ch], jnp.where(g == 1, r4[:ch], jnp.where(g == 2, r8[:ch], r16[:ch])))
            dp_ref[pl.ds(off, ch), :] = (sel - cur).astype(BF16)
            return 0

        lax.fori_loop(0, n, chunk, 0)

    return pl.pallas_call(
        body, name=name, grid=(POOL // LANES,),
        in_specs=[pl.BlockSpec((s, LANES), lambda j: (0, j))],
        out_specs=pl.BlockSpec((s, LANES), lambda j: (0, j)),
        out_shape=jax.ShapeDtypeStruct((s, POOL), BF16),
        compiler_params=_cp(("parallel",), VMEM_LIMIT),
    )(dm)


def _head_mask(h):
    lane = lax.broadcasted_iota(jnp.int32, (CHUNK, SGU), 1)
    return (lane // (SGU // HEADS)) == h


def _tril(upper=False):
    row = lax.broadcasted_iota(jnp.int32, (CHUNK, CHUNK), 0)
    col = lax.broadcasted_iota(jnp.int32, (CHUNK, CHUNK), 1)
    return col >= row if upper else col <= row


def _sgu_gate(vn, wsp, bsp):
    out = []
    for cidx in range(vn.shape[0] // CHUNK):
        vc = vn[cidx * CHUNK:(cidx + 1) * CHUNK]
        zc = bsp
        for h in range(HEADS):
            zc = zc + jnp.where(_head_mask(h), _dot(wsp[h], vc), 0.0)
        out.append(zc)
    return jnp.concatenate(out, axis=0)


def _mix_out_fwd(o, z, m, x, wsp, bsp, wbd, psc, gsv, gout, wout, name):
    s = x.shape[0]
    tm = _tile(s, TOKENS)

    def body(o_ref, uv_ref, m_ref, x_ref, wsp_ref, bsp_ref, wbd_ref, psc_ref, gsv_ref, gout_ref, wout_ref,
             x1_ref, mix_ref):
        g = gout_ref[...]
        an = _rms(o_ref[...], HEADS * VH)[0] * g[:, :512]
        uv = uv_ref[...]
        u, v = uv[:, :SGU], uv[:, SGU:]
        vn = (_rms(v, SGU)[0] * gsv_ref[...]).astype(BF16)
        tri = _tril()
        wsp_m = [jnp.where(tri, wsp_ref[h], 0.0).astype(BF16) for h in range(HEADS)]
        gm = u * _sgu_gate(vn, wsp_m, bsp_ref[...])
        gn = _rms(gm, SGU)[0] * g[:, 512:768]
        po = _dot(m_ref[...].astype(BF16), wbd_ref[...]) * psc_ref[...]
        pn = _rms(po, POOL)[0] * g[:, 768:]
        mix = jnp.concatenate([an, gn, pn], axis=1).astype(BF16)
        mix_ref[...] = mix
        x1_ref[...] = x_ref[...] + _dot(mix, wout_ref[...])

    row = lambda w, j: pl.BlockSpec((tm, w), lambda i: (i, j))
    return pl.pallas_call(
        body, name=name, grid=(s // tm,),
        in_specs=[row(512, 0), row(512, 1), row(POOL, 0), row(D, 0),
                  _acc((HEADS, CHUNK, CHUNK)), _acc((CHUNK, SGU)), _acc((POOL, POOL)), _acc((1, POOL)),
                  _acc((1, SGU)), _acc((1, D)), _res((D, D))],
        out_specs=[row(D, 0), row(D, 0)],
        out_shape=[jax.ShapeDtypeStruct((s, D), F32), jax.ShapeDtypeStruct((s, D), BF16)],
        compiler_params=_cp(("parallel",), VMEM_LIMIT),
    )(o, z, m, x, wsp, bsp, wbd, psc, gsv, gout, wout)


def _ffn_fwd(x1, g, wg, wu, wd, tgt, name):
    s = x1.shape[0]
    tm = _tile(s, 256)
    last = tgt is not None

    def body(x_ref, g_ref, wg_ref, wu_ref, wd_ref, *rest):
        t_ref = rest[0] if last else None
        outs = rest[1:] if last else rest
        a_ref, b_ref, h_ref = outs[-3:]
        x = x_ref[...]
        h = (_rms(x, D)[0] * g_ref[...]).astype(BF16)
        h_ref[...] = h
        acc = jnp.zeros((tm, D), F32)
        for k in range(CHIPS):
            a = _dot_nt(h, wg_ref[k])
            b = _dot_nt(h, wu_ref[k])
            a_ref[k] = a
            b_ref[k] = b
            acc = acc + _dot((a * jax.nn.sigmoid(a) * b).astype(BF16), wd_ref[k])
        if not last:
            outs[0][...] = x + acc
            return
        dy_ref, l_ref = outs[:2]
        e = (x + acc) - t_ref[...]
        dy_ref[...] = e * (1.0 / D)
        sq = jnp.sum(e * e, axis=0, keepdims=True)
        part = sq[:, :LANES]
        for c in range(1, D // LANES):
            part = part + sq[:, c * LANES:(c + 1) * LANES]
        _accumulate(l_ref, part, pl.program_id(0) == 0)

    row = lambda w: pl.BlockSpec((tm, w), lambda i: (i, 0))
    hrow = pl.BlockSpec((CHIPS, tm, SH), lambda i: (0, i, 0))
    hshape = jax.ShapeDtypeStruct((CHIPS, s, SH), F32)
    tail_specs = [hrow, hrow, row(D)]
    tail_shapes = [hshape, hshape, jax.ShapeDtypeStruct((s, D), BF16)]
    head_specs = [row(D), _acc((1, LANES))] if last else [row(D)]
    head_shapes = [jax.ShapeDtypeStruct((s, D), F32)] + ([jax.ShapeDtypeStruct((1, LANES), F32)] if last else [])
    res = pl.pallas_call(
        body, name=name, grid=(s // tm,),
        in_specs=[row(D), _acc((1, D)), _res((CHIPS, SH, D)), _res((CHIPS, SH, D)), _res((CHIPS, SH, D))]
        + ([row(D)] if last else []),
        out_specs=head_specs + tail_specs, out_shape=head_shapes + tail_shapes,
        compiler_params=_cp(("arbitrary",), VMEM_LIMIT),
    )(x1, g, wg, wu, wd, *([tgt] if last else []))
    return (tuple(res[:2]) if last else res[0]), res[-3], res[-2], res[-1]


def _wgrad(a, b, name):
    s, k = a.shape
    n = b.shape[1]
    half = lambda v: v if v <= 1408 else v // 2
    kb, nb, tt = half(k), half(n), _tile(s, 2048)

    def body(a_ref, b_ref, o_ref):
        _accumulate(o_ref, _dot_tn(a_ref[...].astype(BF16), b_ref[...].astype(BF16)), pl.program_id(2) == 0)

    return pl.pallas_call(
        body, name=name, grid=(k // kb, n // nb, s // tt),
        in_specs=[pl.BlockSpec((tt, kb), lambda i, j, t: (t, i)), pl.BlockSpec((tt, nb), lambda i, j, t: (t, j))],
        out_specs=pl.BlockSpec((kb, nb), lambda i, j, t: (i, j)),
        out_shape=jax.ShapeDtypeStruct((k, n), F32),
        compiler_params=_cp(("parallel", "parallel", "arbitrary"), VMEM_LIMIT),
    )(a, b)


def _wgrad_in(h, dzm, duv, dp, name):
    s = h.shape[0]
    tt = _tile(s, 2048)

    def body(h_ref, a_ref, b_ref, c_ref, o_ref):
        hv = h_ref[...]
        val = jnp.concatenate([_dot_tn(hv, a_ref[...]), _dot_tn(hv, b_ref[...]), _dot_tn(hv, c_ref[...])], axis=1)
        _accumulate(o_ref, val, pl.program_id(0) == 0)

    row = lambda w: pl.BlockSpec((tt, w), lambda t: (t, 0))
    return pl.pallas_call(
        body, name=name, grid=(s // tt,), in_specs=[row(D), row(512), row(512), row(POOL)], out_specs=_acc((D, IN_P)),
        out_shape=jax.ShapeDtypeStruct((D, IN_P), F32), compiler_params=_cp(("arbitrary",), VMEM_LIMIT),
    )(h, dzm, duv, dp)


def _wgrad_rows(a, b, name):
    s, n = a.shape[1:]
    nn = b.shape[1]
    tt = _tile(s, 4096 if b.dtype == BF16 else 2048)

    def body(a_ref, b_ref, o_ref):
        _accumulate0(o_ref, _dot_tn(a_ref[0].astype(BF16), b_ref[...].astype(BF16)), pl.program_id(1) == 0)

    return pl.pallas_call(
        body, name=name, grid=(CHIPS, s // tt),
        in_specs=[pl.BlockSpec((1, tt, n), lambda c, t: (c, t, 0)), pl.BlockSpec((tt, nn), lambda c, t: (t, 0))],
        out_specs=pl.BlockSpec((1, n, nn), lambda c, t: (c, 0, 0)),
        out_shape=jax.ShapeDtypeStruct((CHIPS, n, nn), F32),
        compiler_params=_cp(("parallel", "arbitrary"), VMEM_LIMIT),
    )(a, b)


def _ffn_bwd(dx2, x1, a, b, g, wg, wu, wd, name):
    s = x1.shape[0]
    tm = _tile(s, 256)

    def body(dx2_ref, x_ref, a_ref, b_ref, g_ref, wg_ref, wu_ref, wd_ref,
             dx1_ref, hid_ref, da_ref, db_ref, dyb_ref, dg_ref):
        dx2 = dx2_ref[...]
        dyb = dx2.astype(BF16)
        dyb_ref[...] = dyb
        dh = jnp.zeros((tm, D), F32)
        ahead = _dot_nt(dyb, wd_ref[0])
        for k in range(CHIPS):
            av, bv = a_ref[k], b_ref[k]
            dhid = ahead
            if k + 1 < CHIPS:
                ahead = _dot_nt(dyb, wd_ref[k + 1])
            sig = jax.nn.sigmoid(av)
            sa = av * sig
            hid_ref[k] = (sa * bv).astype(BF16)
            dbv = (dhid * sa).astype(BF16)
            dav = (dhid * bv * (sig * (1.0 + av * (1.0 - sig)))).astype(BF16)
            db_ref[k] = dbv
            da_ref[k] = dav
            dh = dh + _dot(dav, wg_ref[k]) + _dot(dbv, wu_ref[k])
        xn, r = _rms(x_ref[...], D)
        dxr, dg = _rms_bwd(xn, r, g_ref[...], dh, D)
        dx1_ref[...] = dx2 + dxr
        _accumulate(dg_ref, dg, pl.program_id(0) == 0)

    row = lambda w: pl.BlockSpec((tm, w), lambda i: (i, 0))
    hrow = pl.BlockSpec((CHIPS, tm, SH), lambda i: (0, i, 0))
    hid = jax.ShapeDtypeStruct((CHIPS, s, SH), BF16)
    return pl.pallas_call(
        body, name=name, grid=(s // tm,),
        in_specs=[row(D), row(D), hrow, hrow, _acc((1, D)), _res((CHIPS, SH, D)), _res((CHIPS, SH, D)),
                  _res((CHIPS, SH, D))],
        out_specs=[row(D), hrow, hrow, hrow, row(D), _acc((1, D))],
        out_shape=[jax.ShapeDtypeStruct((s, D), F32), hid, hid, hid, jax.ShapeDtypeStruct((s, D), BF16),
                   jax.ShapeDtypeStruct((1, D), F32)],
        compiler_params=_cp(("arbitrary",), VMEM_LIMIT),
    )(dx2, x1, a, b, g, wg, wu, wd)


def _mix_out_bwd(dx1, o, z, m, wsp, bsp, wbd, psc, gsv, gout, wout, name):
    s = dx1.shape[0]
    tm = _tile(s, TOKENS)

    def body(dx1_ref, o_ref, uv_ref, m_ref, wsp_ref, bsp_ref, wbd_ref, psc_ref, gsv_ref, gout_ref, wout_ref,
             do_ref, dl_ref, duv_ref, dm_ref, dgo_ref, dgsv_ref, dpsc_ref, dwsp_ref, dbsp_ref, dwbd_ref):
        first = pl.program_id(0) == 0
        g = gout_ref[...]
        dmix = _dot_nt(dx1_ref[...].astype(BF16), wout_ref[...])
        o = o_ref[...]
        on, ro = _rms(o, HEADS * VH)
        do, dga = _rms_bwd(on, ro, g[:, :512], dmix[:, :512], HEADS * VH)
        for h in range(HEADS):
            sl = slice(h * VH, (h + 1) * VH)
            do_ref[h] = do[:, sl].astype(BF16)
            dl_ref[h] = jnp.broadcast_to(jnp.sum(do[:, sl] * o[:, sl], axis=-1, keepdims=True), (tm, LANES))
        uv = uv_ref[...]
        u, v = uv[:, :SGU], uv[:, SGU:]
        vx, rv = _rms(v, SGU)
        vn = (vx * gsv_ref[...]).astype(BF16)
        tri = _tril()
        wsp_m = [jnp.where(tri, wsp_ref[h], 0.0).astype(BF16) for h in range(HEADS)]
        zc = _sgu_gate(vn, wsp_m, bsp_ref[...])
        gm = u * zc
        gmn, rg = _rms(gm, SGU)
        dgm, dgg = _rms_bwd(gmn, rg, g[:, 512:768], dmix[:, 512:768], SGU)
        du = dgm * zc
        dzc = dgm * u
        dvn_parts = []
        dbsp = jnp.zeros((CHUNK, SGU), F32)
        dwsp = [jnp.zeros((CHUNK, CHUNK), F32) for _ in range(HEADS)]
        for cidx in range(tm // CHUNK):
            rs = slice(cidx * CHUNK, (cidx + 1) * CHUNK)
            dzc_c = dzc[rs]
            dbsp = dbsp + dzc_c
            dzb = dzc_c.astype(BF16)
            vc = vn[rs]
            dvn_c = jnp.zeros((CHUNK, SGU), F32)
            for h in range(HEADS):
                hm = _head_mask(h)
                dvn_c = dvn_c + jnp.where(hm, _dot_tn(wsp_m[h], dzb), 0.0)
                dwsp[h] = dwsp[h] + _dot_nt(jnp.where(hm, dzc_c, 0.0).astype(BF16), vc)
            dvn_parts.append(dvn_c)
        dvn = jnp.concatenate(dvn_parts, axis=0)
        dv, dgsv = _rms_bwd(vx, rv, gsv_ref[...], dvn, SGU)
        duv_ref[...] = jnp.concatenate([du, dv], axis=1).astype(BF16)
        mb = m_ref[...].astype(BF16)
        pw = _dot(mb, wbd_ref[...])
        po = pw * psc_ref[...]
        pon, rp = _rms(po, POOL)
        dpo, dgp = _rms_bwd(pon, rp, g[:, 768:], dmix[:, 768:], POOL)
        dpw = (dpo * psc_ref[...]).astype(BF16)
        dm_ref[...] = _dot_nt(dpw, wbd_ref[...])
        _accumulate(dgo_ref, jnp.concatenate([dga, dgg, dgp], axis=1), first)
        _accumulate(dgsv_ref, dgsv, first)
        _accumulate(dpsc_ref, jnp.sum(dpo * pw, axis=0, keepdims=True), first)
        _accumulate(dbsp_ref, dbsp, first)
        _accumulate(dwbd_ref, _dot_tn(mb, dpw), first)
        for h in range(HEADS):
            val = jnp.where(tri, dwsp[h], 0.0)

            @pl.when(first)
            def _(val=val, h=h):
                dwsp_ref[h] = val

            @pl.when(jnp.logical_not(first))
            def _(val=val, h=h):
                dwsp_ref[h] += val

    row = lambda w, j: pl.BlockSpec((tm, w), lambda i: (i, j))
    hspec = pl.BlockSpec((HEADS, tm, HP), lambda i: (0, i, 0))
    return pl.pallas_call(
        body, name=name, grid=(s // tm,),
        in_specs=[row(D, 0), row(512, 0), row(512, 1), row(POOL, 0),
                  _acc((HEADS, CHUNK, CHUNK)), _acc((CHUNK, SGU)),
                  _acc((POOL, POOL)), _acc((1, POOL)), _acc((1, SGU)), _acc((1, D)), _res((D, D))],
        out_specs=[hspec, hspec, row(512, 0), row(POOL, 0), _acc((1, D)), _acc((1, SGU)), _acc((1, POOL)),
                   _acc((HEADS, CHUNK, CHUNK)), _acc((CHUNK, SGU)), _acc((POOL, POOL))],
        out_shape=[jax.ShapeDtypeStruct((HEADS, s, HP), BF16), jax.ShapeDtypeStruct((HEADS, s, LANES), F32),
                   jax.ShapeDtypeStruct((s, 512), BF16), jax.ShapeDtypeStruct((s, POOL), F32),
                   jax.ShapeDtypeStruct((1, D), F32), jax.ShapeDtypeStruct((1, SGU), F32),
                   jax.ShapeDtypeStruct((1, POOL), F32), jax.ShapeDtypeStruct((HEADS, CHUNK, CHUNK), F32),
                   jax.ShapeDtypeStruct((CHUNK, SGU), F32), jax.ShapeDtypeStruct((POOL, POOL), F32)],
        compiler_params=_cp(("arbitrary",), VMEM_LIMIT),
    )(dx1, o, z, m, wsp, bsp, wbd, psc, gsv, gout, wout)


def _attn_bwd(q, k, v, do, lse, delta, after, name):
    s = q.shape[1]
    rh = _tile(s, ATT_ROWS)
    tk = _tile(s, ATT_KEYS)
    nk = s // tk
    wide = ATT_QUERIES if s % ATT_QUERIES == 0 else tk
    pieces = tk // rh

    def body(q_ref, k_ref, v_ref, do_ref, lse_ref, dl_ref, after_ref, dq_ref, dk_ref, dv_ref):
        del after_ref
        j = pl.program_id(1)

        @pl.when(j == 0)
        def _():
            dq_ref[...] = jnp.zeros_like(dq_ref)

        kj, vj = k_ref[0], v_ref[0]

        def blk(start, rows, dks, dvs, diagonal):
            dks, dvs = list(dks), list(dvs)
            offs = [pl.multiple_of(start + g * rh, rh) for g in range(rows // rh)]
            keys = [(g + 1) * rh if diagonal else tk for g in range(rows // rh)]
            qs = [q_ref[0, pl.ds(off, rh), :] for off in offs]
            dos = [do_ref[0, pl.ds(off, rh), :] for off in offs]
            scs = [_dot_nt(qi, kj[:n]) for qi, n in zip(qs, keys)]
            dps = [_dot_nt(doi, vj[:n]) for doi, n in zip(dos, keys)]
            for g, off in enumerate(offs):
                lse_i = lse_ref[0, pl.ds(off, rh), :][:, :1]
                dl_i = dl_ref[0, pl.ds(off, rh), :][:, :1]
                sc = _causal_mask(scs[g], g * rh) if diagonal else scs[g]
                p = jnp.exp2(sc - lse_i)
                ds = (p * (dps[g] - dl_i)).astype(BF16)
                cv = _dot_tn(p.astype(BF16), dos[g])
                ck = _dot_tn(ds, qs[g])
                for t in range(keys[g] // rh):
                    dvs[t] = dvs[t] + cv[t * rh:(t + 1) * rh]
                    dks[t] = dks[t] + ck[t * rh:(t + 1) * rh]
                dq_ref[0, pl.ds(off, rh), :] += _dot(ds, kj[:keys[g]]) * SCALE
            return tuple(dks), tuple(dvs)

        per = wide // tk
        zero = (jnp.zeros((rh, HP), F32),) * pieces
        acc = blk(j * tk, tk, zero, zero, True)
        first_wide = (j + per) // per
        acc = lax.fori_loop(j + 1, jnp.minimum(first_wide * per, nk), lambda i, c: blk(i * tk, tk, *c, False), acc)
        dks, dvs = lax.fori_loop(first_wide, nk // per, lambda i, c: blk(i * wide, wide, *c, False), acc)
        dk_ref[0] = jnp.concatenate(dks, axis=0) * (SCALE / EXP2_C)
        dv_ref[0] = jnp.concatenate(dvs, axis=0)

    full = lambda: pl.BlockSpec((1, s, HP), lambda h, j: (h, 0, 0))
    blk_spec = lambda: pl.BlockSpec((1, tk, HP), lambda h, j: (h, j, 0))
    out = jax.ShapeDtypeStruct((HEADS, s, HP), F32)
    return pl.pallas_call(
        body, name=name, grid=(HEADS, s // tk),
        in_specs=[full(), blk_spec(), blk_spec(), full(), full(), full(), ANY],
        out_specs=[full(), blk_spec(), blk_spec()], out_shape=[out] * 3,
        compiler_params=_cp(("parallel", "arbitrary"), VMEM_LIMIT),
    )(q, k, v, do, lse, delta, after)


def _mla_prep_bwd(dq, dk, dv, z, tabs, gql, gkv, gq, gk, wq, wk, wv, name):
    s = z.shape[0]
    tm = _tile(s, TOKENS)

    def body(dq_ref, dk_ref, dv_ref, ql_ref, kv_ref, kr_ref, c_ref, sa_ref, sb_ref, gql_ref, gkv_ref, gq_ref, gk_ref,
             wq_ref, wk_ref, wv_ref,
             dz_ref, dwq_ref, dwk_ref, dwv_ref, dgql_ref, dgkv_ref, dgq_ref, dgk_ref, dqr_ref, dkr_ref, dvr_ref):
        first = pl.program_id(0) == 0
        qx, rq = _rms(ql_ref[...], QL)
        qn = (qx * gql_ref[...]).astype(BF16)
        kx, rk = _rms(kv_ref[...], KVL)
        kvn = (kx * gkv_ref[...]).astype(BF16)
        qraw = _dot(qn, wq_ref[...])
        kraw = _dot(kvn, wk_ref[...])
        kr = kr_ref[...]
        c, sa, sb = c_ref[...], sa_ref[...], sb_ref[...]
        lane = lax.broadcasted_iota(jnp.int32, (tm, HP), 1)
        rope_lanes = (lane >= NOPE) & (lane < QK)
        ones = _head_ones()
        heads = lambda f: jnp.concatenate([f(h) for h in range(HEADS)], axis=1)
        fold = lambda v: sum(v[:, h * HP:(h + 1) * HP] for h in range(HEADS))

        def head_rms_bwd(x, g_ref, d_ref):
            xn, r = _head_rms(x, ones)
            dy = heads(lambda h: _rope_t(d_ref[h], c, sa, sb))
            dn = dy * jnp.concatenate([g_ref[...]] * HEADS, axis=1)
            dx = r * (dn - xn * (_head_sum(dn * xn, ones) * (1.0 / QK)))
            return dx, fold(jnp.sum(dy * xn, axis=0, keepdims=True))

        dxq, dgq = head_rms_bwd(qraw, gq_ref, dq_ref)
        dxk, dgk = head_rms_bwd(kraw + jnp.concatenate([kr] * HEADS, axis=1), gk_ref, dk_ref)
        dqr_ref[...] = dxq.astype(BF16)
        dkr_ref[...] = dxk.astype(BF16)
        dvr_ref[...] = heads(lambda h: dv_ref[h]).astype(BF16)
        dkrope = jnp.where(rope_lanes, fold(dxk), 0.0)
        dqn = _dot_nt(dqr_ref[...], wq_ref[...])
        dql, dgql = _rms_bwd(qx, rq, gql_ref[...], dqn, QL)
        dkvn = _dot_nt(dkr_ref[...], wk_ref[...]) + _dot_nt(dvr_ref[...], wv_ref[...])
        dkv, dgkv = _rms_bwd(kx, rk, gkv_ref[...], dkvn, KVL)
        dz_ref[...] = jnp.concatenate([dql, dkv, dkrope], axis=1).astype(BF16)
        _accumulate(dwq_ref, _dot_tn(qn, dqr_ref[...]), first)
        _accumulate(dwk_ref, _dot_tn(kvn, dkr_ref[...]), first)
        _accumulate(dwv_ref, _dot_tn(kvn, dvr_ref[...]), first)
        _accumulate(dgql_ref, dgql, first)
        _accumulate(dgkv_ref, dgkv, first)
        _accumulate(dgq_ref, dgq, first)
        _accumulate(dgk_ref, dgk, first)

    row = lambda w, j: pl.BlockSpec((tm, w), lambda i: (i, j))
    hspec = pl.BlockSpec((HEADS, tm, HP), lambda i: (0, i, 0))
    acc = lambda r, c: (_acc((r, c)), jax.ShapeDtypeStruct((r, c), F32))
    outs = [(row(512, 0), jax.ShapeDtypeStruct((s, 512), BF16)), acc(QL, HEADS * HP), acc(KVL, HEADS * HP),
            acc(KVL, HEADS * HP), acc(1, QL), acc(1, KVL), acc(1, HP), acc(1, HP)]
    return pl.pallas_call(
        body, name=name, grid=(s // tm,),
        in_specs=[hspec, hspec, hspec, row(QL, 0), row(KVL, 2), row(HP, 3), row(HP, 0), row(HP, 0), row(HP, 0),
                  _acc((1, QL)), _acc((1, KVL)), _acc((1, HP)), _acc((1, HP)),
                  _acc((QL, HEADS * HP)), _acc((KVL, HEADS * HP)), _acc((KVL, HEADS * HP))],
        out_specs=[o[0] for o in outs], out_shape=[o[1] for o in outs],
        scratch_shapes=[pltpu.VMEM((tm, HEADS * HP), BF16)] * 3,
        compiler_params=_cp(("arbitrary",), VMEM_LIMIT),
    )(dq, dk, dv, z, z, z, *tabs, gql, gkv, gq, gk, wq, wk, wv)


def _in_proj_bwd(dzm, duv, dp, x, dx1, g, win, name):
    s = x.shape[0]
    tm = _tile(s, TOKENS // 2)

    def body(dzm_ref, duv_ref, dp_ref, x_ref, dx1_ref, g_ref, w_ref, dx_ref, dg_ref):
        groups = [slice(r0, r0 + tm // 2) for r0 in (0, tm // 2)]
        dhs = [_dot_nt(dzm_ref[rs, :], w_ref[:, 0:512]) + _dot_nt(duv_ref[rs, :], w_ref[:, 512:1024])
               + _dot_nt(dp_ref[rs, :], w_ref[:, 1024:IN_P]) for rs in groups]
        dg = jnp.zeros((1, D), F32)
        for rs, dh in zip(groups, dhs):
            xn, r = _rms(x_ref[rs, :], D)
            dxr, dgr = _rms_bwd(xn, r, g_ref[...], dh, D)
            dx_ref[rs, :] = dx1_ref[rs, :] + dxr
            dg = dg + dgr
        _accumulate(dg_ref, dg, pl.program_id(0) == 0)

    row = lambda w: pl.BlockSpec((tm, w), lambda i: (i, 0))
    return pl.pallas_call(
        body, name=name, grid=(s // tm,),
        in_specs=[row(512), row(512), row(POOL), row(D), row(D), _acc((1, D)), _res((D, IN_P))],
        out_specs=[row(D), _acc((1, D))],
        out_shape=[jax.ShapeDtypeStruct((s, D), F32), jax.ShapeDtypeStruct((1, D), F32)],
        compiler_params=_cp(("arbitrary",), VMEM_LIMIT),
    )(dzm, duv, dp, x, dx1, g, win)


def _adamw(w, g0, g1, m, v, name):
    _, r, c = w.shape
    tr = _row_tile(r, 512)
    c1 = 1.0 - B1 ** STEP
    c2 = 1.0 - B2 ** STEP

    def body(w_ref, g0_ref, g1_ref, m_ref, v_ref, g_ref, d_ref, nm_ref, nv_ref):
        gv = jnp.where(pl.program_id(0) == 0, g0_ref[...], g1_ref[...])
        g_ref[0] = gv
        nm = B1 * m_ref[0] + (1.0 - B1) * gv
        nv = B2 * v_ref[0] + (1.0 - B2) * (gv * gv)
        nm_ref[0] = nm
        nv_ref[0] = nv
        d_ref[0] = -LR * ((nm / c1) / (jnp.sqrt(nv / c2) + ADAM_EPS) + WD * w_ref[0])

    spec = pl.BlockSpec((1, tr, c), lambda l, i: (l, i, 0))
    out = jax.ShapeDtypeStruct((DEPTH, r, c), F32)
    return pl.pallas_call(
        body, name=name, grid=(DEPTH, r // tr),
        in_specs=[spec, pl.BlockSpec((tr, c), lambda l, i: (i * (1 - l), 0)), pl.BlockSpec((tr, c), lambda l, i: (i * l, 0)),
                  spec, spec],
        out_specs=[spec] * 4, out_shape=[out] * 4, compiler_params=_cp(("parallel", "parallel")),
    )(w, g0, g1, m, v)


def _adamw_vectors(ws, g0s, g1s, ms, vs, name):
    k = len(ws)
    c1 = 1.0 - B1 ** STEP
    c2 = 1.0 - B2 ** STEP

    def body(*refs):
        w_refs, g0_refs, g1_refs, m_refs, v_refs, g_out, d_out, m_out, v_out = (refs[i * k:(i + 1) * k] for i in range(9))
        for i in range(k):
            for l, g_ref in enumerate((g0_refs[i], g1_refs[i])):
                row = slice(l, l + 1)
                gv = g_ref[...]
                g_out[i][row, :] = gv
                nm = B1 * m_refs[i][row, :] + (1.0 - B1) * gv
                nv = B2 * v_refs[i][row, :] + (1.0 - B2) * (gv * gv)
                m_out[i][row, :] = nm
                v_out[i][row, :] = nv
                d_out[i][row, :] = -LR * ((nm / c1) / (jnp.sqrt(nv / c2) + ADAM_EPS) + WD * w_refs[i][row, :])

    out = [jax.ShapeDtypeStruct(w.shape, F32) for w in ws]
    return pl.pallas_call(body, name=name, out_shape=out * 4)(*ws, *g0s, *g1s, *ms, *vs)


ANY = pl.BlockSpec(memory_space=pl.ANY)


def _place():
    x, y, c = lax.axis_index("x"), lax.axis_index("y"), lax.axis_index("c")
    chips = [(1 - x, y), (x, 1 - y), (1 - x, 1 - y)]
    return x, y, c, chips


def _half_rows(ref, lead, hh, half, align):
    rows = pl.ds(pl.multiple_of(hh * half, align), half)
    return ref.at[rows, :] if lead is None else ref.at[lead, rows, :]


def _row_align(dtype):
    return 16 if dtype == BF16 else 8


def _sems(n):
    return [pltpu.SemaphoreType.DMA((n,)), pltpu.SemaphoreType.DMA((n,)), pltpu.SemaphoreType.DMA((n,))]


def _comm_call(body, ins, out_shapes, nsems, name):
    return pl.pallas_call(
        body, name=name, in_specs=[ANY] * len(ins), out_specs=[ANY] * len(out_shapes), out_shape=out_shapes,
        scratch_shapes=_sems(nsems), compiler_params=pltpu.CompilerParams(has_side_effects=True),
    )(*ins)


def _all_gather_chips(shards, name):
    n = len(shards)
    halves = [a.shape[0] // 2 for a in shards]
    aligns = [_row_align(a.dtype) for a in shards]
    assert all(h % al == 0 for h, al in zip(halves, aligns))

    def body(*refs):
        ins, outs, (send_sems, recv_sems, _) = refs[:n], refs[n:2 * n], refs[2 * n:]
        x, y, c, chips = _place()
        me = 2 * x + y
        sibling = (x, y, 1 - c)

        def copy(sem, src, dst, to):
            return pltpu.make_async_remote_copy(src_ref=src, dst_ref=dst, send_sem=send_sems.at[sem],
                                                recv_sem=recv_sems.at[sem], device_id=to, device_id_type=MESH)

        first, passed = [], []
        for a in range(n):
            my_half = _half_rows(ins[a], None, c, halves[a], aligns[a])
            for j, (cx, cy) in enumerate(chips):
                cp = copy(6 * a + j, my_half, _half_rows(outs[a], me, c, halves[a], aligns[a]), (cx, cy, c))
                cp.start()
                first.append(cp)
        for a in range(n):
            for j, (cx, cy) in enumerate(chips):
                landed = _half_rows(outs[a], 2 * cx + cy, c, halves[a], aligns[a])
                copy(6 * a + j, landed, landed, (cx, cy, c)).wait_recv()
                fwd = copy(6 * a + 3 + j, landed, landed, sibling)
                fwd.start()
                passed.append(fwd)
        for a in range(n):
            for j, (cx, cy) in enumerate(chips):
                other = _half_rows(outs[a], 2 * cx + cy, 1 - c, halves[a], aligns[a])
                copy(6 * a + 3 + j, other, other, sibling).wait_recv()
        for cp in first + passed:
            cp.wait_send()

    lands = _comm_call(body, shards, [jax.ShapeDtypeStruct((CHIPS,) + a.shape, a.dtype) for a in shards], 6 * n, name)
    return _with_own(lands, shards)


def _with_own(lands, shards):
    me = 2 * lax.axis_index("x") + lax.axis_index("y")
    return [lax.dynamic_update_slice(g, a[None], (me, 0, 0)) for g, a in zip(lands, shards)]


def _pair_join(arrs, name):
    n = len(arrs)
    halves = [a.shape[0] // 2 for a in arrs]

    def body(*refs):
        outs, (send_sems, recv_sems, _) = refs[n:2 * n], refs[2 * n:]
        x, y, c, _ = _place()
        cps = []
        for a in range(n):
            mine = _half_rows(outs[a], None, c, halves[a], 8)
            cp = pltpu.make_async_remote_copy(src_ref=mine, dst_ref=mine, send_sem=send_sems.at[a], recv_sem=recv_sems.at[a],
                                              device_id=(x, y, 1 - c), device_id_type=MESH)
            cp.start()
            cps.append(cp)
        for cp in cps:
            cp.wait()

    return pl.pallas_call(
        body, name=name, in_specs=[ANY] * n, out_specs=[ANY] * n,
        out_shape=[jax.ShapeDtypeStruct(a.shape, a.dtype) for a in arrs],
        input_output_aliases={i: i for i in range(n)}, scratch_shapes=_sems(n),
        compiler_params=pltpu.CompilerParams(has_side_effects=True),
    )(*arrs)


HBM = pl.BlockSpec(memory_space=pltpu.HBM)
SEM = pl.BlockSpec(memory_space=pltpu.SEMAPHORE)
DATAFLOW = pltpu.SideEffectType.DATAFLOW_SIDE_EFFECTING


def _remote_copies(pairs, ins, lands, send_sems, recv_sems):
    return [pltpu.make_async_remote_copy(src_ref=src, dst_ref=dst, send_sem=send_sems.at[i], recv_sem=recv_sems.at[i],
                                         device_id=to, device_id_type=MESH)
            for i, (src, dst, to) in enumerate(pairs(ins, lands))]


def _split_start(srcs, land_shapes, ncopies, pairs, name, after):
    n, m = len(srcs), len(land_shapes)

    def body(*refs):
        ins, lands = refs[:n], refs[n:n + m]
        send_sems, recv_sems, token = refs[n + m + 1], refs[n + m + 2], refs[-1]
        for cp in _remote_copies(pairs, ins, lands, send_sems, recv_sems):
            cp.start()
        token[...] = jnp.zeros_like(token)

    hbm = lambda a: pltpu.with_memory_space_constraint(a, pltpu.HBM)
    lands = [hbm(lax.empty(s.shape, s.dtype)) for s in land_shapes]
    thru = [pltpu.HBM(a.shape, a.dtype) for a in list(srcs) + lands]
    out = pl.pallas_call(
        body, name=name,
        out_shape=(pltpu.SemaphoreType.DMA((ncopies,)), pltpu.SemaphoreType.DMA((ncopies,)), *thru,
                   jax.ShapeDtypeStruct((8, LANES), F32)),
        in_specs=[HBM] * (n + m) + [ANY], out_specs=(SEM, SEM, *[HBM] * (n + m), pl.BlockSpec(memory_space=pltpu.VMEM)),
        input_output_aliases={i: 2 + i for i in range(n + m)},
        compiler_params=pltpu.CompilerParams(has_side_effects=DATAFLOW),
    )(*[hbm(a) for a in srcs], *lands, after)
    return out[0], out[1], list(out[2:2 + n]), list(out[2 + n:2 + n + m]), out[-1]


def _split_wait(send_sems, recv_sems, srcs, lands, after, pairs, name):
    n, m = len(srcs), len(lands)

    def body(*refs):
        ins, lands_ = refs[:n], refs[n:n + m]
        for cp in _remote_copies(pairs, ins, lands_, refs[n + m], refs[n + m + 1]):
            cp.wait_send()
            cp.wait_recv()

    out = pl.pallas_call(
        body, name=name, out_shape=tuple(pltpu.HBM(a.shape, a.dtype) for a in list(srcs) + list(lands)),
        in_specs=[HBM] * (n + m) + [SEM, SEM, ANY], out_specs=tuple([HBM] * (n + m)),
        input_output_aliases={i: i for i in range(n + m)},
        compiler_params=pltpu.CompilerParams(has_side_effects=DATAFLOW),
    )(*srcs, *lands, send_sems, recv_sems, after)
    return list(out[:n]), list(out[n:])


def _gather_pairs(halves, aligns):
    def pairs(ins, lands):
        x, y, c, chips = _place()
        me = 2 * x + y
        return [(_half_rows(ins[a], None, c, halves[a], aligns[a]), _half_rows(lands[a], me, c, halves[a], aligns[a]),
                 (cx, cy, c)) for a in range(len(ins)) for cx, cy in chips]
    return pairs


PEERS = 7


def _scatter_pairs(ins, lands):
    x, y, c, chips = _place()
    to = [(cx, cy, c) for cx, cy in chips] + [(cx, cy, 1 - c) for cx, cy in chips] + [(x, y, 1 - c)]
    out = []
    for a in range(len(ins)):
        half = ins[a].shape[1] // 2
        for i, (tx, ty, tc) in enumerate(to):
            out.append((_half_rows(ins[a], 2 * tx + ty, tc, half, 8), lands[a].at[i], (tx, ty, tc)))
    return out


def _gather_finish(shards, lands, name):
    n = len(shards)
    halves = [a.shape[0] // 2 for a in shards]
    aligns = [_row_align(a.dtype) for a in shards]

    def body(*refs):
        outs, (send_sems, recv_sems, _) = refs[n:2 * n], refs[2 * n:]
        x, y, c, chips = _place()
        passed = []
        for a in range(n):
            for j, (cx, cy) in enumerate(chips):
                landed = _half_rows(outs[a], 2 * cx + cy, c, halves[a], aligns[a])
                cp = pltpu.make_async_remote_copy(src_ref=landed, dst_ref=landed, send_sem=send_sems.at[3 * a + j],
                                                  recv_sem=recv_sems.at[3 * a + j], device_id=(x, y, 1 - c),
                                                  device_id_type=MESH)
                cp.start()
                passed.append(cp)
        for a in range(n):
            for j, (cx, cy) in enumerate(chips):
                other = _half_rows(outs[a], 2 * cx + cy, 1 - c, halves[a], aligns[a])
                pltpu.make_async_remote_copy(src_ref=other, dst_ref=other, send_sem=send_sems.at[3 * a + j],
                                             recv_sem=recv_sems.at[3 * a + j], device_id=(x, y, 1 - c),
                                             device_id_type=MESH).wait_recv()
        for cp in passed:
            cp.wait_send()

    lands = pl.pallas_call(
        body, name=name, in_specs=[ANY] * n, out_specs=[ANY] * n,
        out_shape=[jax.ShapeDtypeStruct(a.shape, a.dtype) for a in lands],
        input_output_aliases={i: i for i in range(n)}, scratch_shapes=_sems(3 * n),
        compiler_params=pltpu.CompilerParams(has_side_effects=True),
    )(*lands)
    return _with_own(lands, shards)


def _sum_own_and_landed(owns, landeds, where, name):
    n = len(owns)
    _, half, cols = landeds[0].shape
    tr = _row_tile(half, 128)
    nt = half // tr

    grid_spec = pltpu.PrefetchScalarGridSpec(
        num_scalar_prefetch=1, grid=(nt,),
        in_specs=[pl.BlockSpec((1, tr, cols), lambda r, w: (w[0], w[1] * nt + r, 0))] * n
        + [pl.BlockSpec((PEERS, tr, cols), lambda r, w: (0, r, 0))] * n,
        out_specs=[pl.BlockSpec((tr, cols), lambda r, w: (w[1] * nt + r, 0))] * n)

    def body(w_ref, *refs):
        for p_ref, q_ref, o_ref in zip(refs[:n], refs[n:2 * n], refs[2 * n:]):
            acc = p_ref[0]
            for i in range(PEERS):
                acc = acc + q_ref[i]
            o_ref[...] = acc

    return pl.pallas_call(
        body, name=name, grid_spec=grid_spec, out_shape=[jax.ShapeDtypeStruct((2 * half, cols), owns[0].dtype)] * n,
        compiler_params=_cp(("parallel",), VMEM_LIMIT),
    )(where, *owns, *landeds)


BIG = [("w_in", (D, IN_W), 1), ("w_q_up", (QL, HEADS * QK), 1), ("w_kv_up", (KVL, HEADS * (NOPE + VH)), 1),
       ("w_out", (D, D), 0), ("w_gate", (D, HID), 1), ("w_up", (D, HID), 1), ("w_down", (HID, D), 0)]
SMALL = [("g_mix_norm", (D,)), ("g_q_lat", (QL,)), ("g_kv_lat", (KVL,)), ("g_q_head", (QK,)), ("g_k_head", (QK,)),
         ("g_sgu_v", (SGU,)), ("w_spatial", (HEADS, CHUNK, CHUNK)), ("b_spatial", (HEADS, CHUNK)),
         ("w_pool", (4, 64, 64)), ("pool_scale", (POOL,)), ("g_out_mla", (512,)), ("g_out_sgu", (SGU,)),
         ("g_out_pool", (POOL,)), ("g_ffn_norm", (D,))]
ORDER = ["g_mix_norm", "w_in", "g_q_lat", "w_q_up", "g_kv_lat", "w_kv_up", "g_q_head", "g_k_head", "g_sgu_v",
         "w_spatial", "b_spatial", "w_pool", "pool_scale", "g_out_mla", "g_out_sgu", "g_out_pool", "w_out",
         "g_ffn_norm", "w_gate", "w_up", "w_down"]
EARLY_BIG = ["w_in", "w_q_up", "w_kv_up"]
FFN_BIG = ["w_gate", "w_up", "w_down"]
LATE_BIG = ["w_out"] + FFN_BIG
DEPTH = 2
COLS = 1024
SMALL_N = sum(math.prod(s) for _, s in SMALL) * DEPTH
assert SMALL_N % CHIPS == 0
SMALL_ROWS = -(-(SMALL_N // CHIPS + 1) // (16 * COLS)) * 16


def _unsplit_cols(g):
    return g.transpose(1, 0, 2).reshape(g.shape[1], CHIPS * g.shape[2])


def _split_cols(full):
    r, c = full.shape
    return full.reshape(r, CHIPS, c // CHIPS).transpose(1, 0, 2)


def _kernel_weights(g):
    win = _unsplit_cols(g["w_in"])
    zeros = lambda r, c: jnp.zeros((r, c), BF16)
    o2, o3, o4 = QL + KVL, QL + KVL + ROPE, QL + KVL + ROPE + 2 * SGU
    win_p = jnp.concatenate([win[:, :o2], zeros(D, NOPE), win[:, o2:o3], zeros(D, HP - QK), win[:, o3:o4], win[:, o4:]], axis=1)
    wq = _unsplit_cols(g["w_q_up"]).reshape(QL, HEADS, QK)
    wq_p = jnp.pad(wq, ((0, 0), (0, 0), (0, HP - QK))).reshape(QL, HEADS * HP)
    wkv = _unsplit_cols(g["w_kv_up"]).reshape(KVL, HEADS, NOPE + VH)
    wk_p = jnp.pad(wkv[:, :, :NOPE], ((0, 0), (0, 0), (0, HP - NOPE))).reshape(KVL, HEADS * HP)
    wv_p = wkv[:, :, NOPE:].reshape(KVL, HEADS * VH)
    return dict(win=win_p, wq=wq_p, wk=wk_p, wv=wv_p)


def _small_operands(p, l):
    row = lambda v: v.reshape(1, -1)
    pad = lambda v: jnp.pad(v, (0, HP - QK)).reshape(1, HP)
    wpool = p["w_pool"][l]
    wbd = jnp.zeros((POOL, POOL), F32)
    for g in range(4):
        wbd = lax.dynamic_update_slice(wbd, wpool[g], (g * 64, g * 64))
    return dict(
        g_mix=row(p["g_mix_norm"][l]), gql=row(p["g_q_lat"][l]), gkv=row(p["g_kv_lat"][l]),
        gq=pad(p["g_q_head"][l]), gk=pad(p["g_k_head"][l]), gsv=row(p["g_sgu_v"][l]),
        wsp=p["w_spatial"][l], bsp=jnp.repeat(p["b_spatial"][l].T, SGU // HEADS, axis=1),
        wbd=wbd.astype(BF16), psc=row(p["pool_scale"][l]),
        gout=jnp.concatenate([p["g_out_mla"][l], p["g_out_sgu"][l], p["g_out_pool"][l]]).reshape(1, D),
        g_ffn=row(p["g_ffn_norm"][l]))


def _big_grads(g):
    dwin = g["win"]
    o2 = QL + KVL
    gin = jnp.concatenate([dwin[:, :o2], dwin[:, o2 + NOPE:o2 + NOPE + ROPE], dwin[:, 512:]], axis=1)
    gq = g["wq"].reshape(QL, HEADS, HP)[:, :, :QK].reshape(QL, HEADS * QK)
    gk = g["wk"].reshape(KVL, HEADS, HP)[:, :, :NOPE]
    gv = g["wv"].reshape(KVL, HEADS, VH)
    gkv = jnp.concatenate([gk, gv], axis=2).reshape(KVL, HEADS * (NOPE + VH))
    return {"w_in": _split_cols(gin), "w_q_up": _split_cols(gq), "w_kv_up": _split_cols(gkv),
            "w_out": g["wout"].reshape(CHIPS, D // CHIPS, D), "w_gate": g["wg"], "w_up": g["wu"], "w_down": g["wd"]}


TRANSPOSED = ("w_gate", "w_up")


def _small_grads(g):
    go = g["gout"].reshape(-1)
    return {"g_mix_norm": g["g_mix"].reshape(-1), "g_q_lat": g["gql"].reshape(-1), "g_kv_lat": g["gkv"].reshape(-1),
            "g_q_head": g["gq"].reshape(-1)[:QK], "g_k_head": g["gk"].reshape(-1)[:QK], "g_sgu_v": g["gsv"].reshape(-1),
            "w_spatial": g["wsp"], "b_spatial": g["bsp"].reshape(CHUNK, HEADS, SGU // HEADS).sum(-1).T,
            "w_pool": jnp.stack([g["wbd"][i * 64:(i + 1) * 64, i * 64:(i + 1) * 64] for i in range(4)]),
            "pool_scale": g["psc"].reshape(-1), "g_out_mla": go[:512], "g_out_sgu": go[512:768],
            "g_out_pool": go[768:], "g_ffn_norm": g["g_ffn"].reshape(-1)}


def _pack_small_grads(small, loss):
    sm = jnp.concatenate([small[l][n].reshape(-1) for l in range(DEPTH) for n, _ in SMALL]).reshape(CHIPS, SMALL_N // CHIPS)
    sm = jnp.pad(sm, ((0, 0), (0, SMALL_ROWS * COLS - SMALL_N // CHIPS)))
    return sm.at[0, SMALL_N // CHIPS].set(loss).reshape(CHIPS, SMALL_ROWS, COLS)


def _unpack_small_grads(gathered):
    rows = gathered.reshape(CHIPS, SMALL_ROWS * COLS)
    loss = rows[0, SMALL_N // CHIPS]
    flat = rows[:, :SMALL_N // CHIPS].reshape(-1)
    out, off = [], 0
    for _ in range(DEPTH):
        layer = {}
        for n, shape in SMALL:
            k = math.prod(shape)
            layer[n] = flat[off:off + k].reshape(shape)
            off += k
        out.append(layer)
    return out, loss


def _layer_fwd(x, tabs, kw, late_weights, sp, l, tgt):
    t = f"_l{l}"
    z, hb = _in_proj_fwd(x, sp["g_mix"], kw["win"], "in_proj_fwd" + t)
    q, k, v = _mla_prep_fwd(z, tabs, sp["gql"], sp["gkv"], sp["gq"], sp["gk"], kw["wq"], kw["wk"], kw["wv"],
                            "mla_prep_fwd" + t)
    o, lse = _attn_fwd(q, k, v, "attn_fwd" + t)
    m = _pool_win_fwd(z, "pool_win_fwd" + t)
    wout, wg, wu, wd = late_weights(o)
    wout = wout.reshape(D, D)
    x1, mix = _mix_out_fwd(o, z, m, x, sp["wsp"], sp["bsp"], sp["wbd"], sp["psc"], sp["gsv"], sp["gout"], wout,
                           "mix_out_fwd" + t)
    x2, a, b, h2 = _ffn_fwd(x1, sp["g_ffn"], wg, wu, wd, tgt, "ffn_fwd" + t)
    saved = dict(x=x, z=z, hb=hb, q=q, k=k, v=v, o=o, lse=lse, m=m, x1=x1, mix=mix, a=a, b=b, h2=h2, wg=wg, wu=wu, wd=wd,
                 wout=wout)
    return x2, saved


def _layer_bwd(dx2, sv, tabs, kw, sp, l, ffn_hook, out_hook):
    t = f"_l{l}"
    g = {}
    dx1, hid, da, db, dyb, g["g_ffn"] = _ffn_bwd(dx2, sv["x1"], sv["a"], sv["b"], sp["g_ffn"], sv["wg"], sv["wu"],
                                                 sv["wd"], "ffn_bwd" + t)
    g["wd"] = _wgrad_rows(hid, dyb, "wgrad_down" + t)
    g["wg"] = _wgrad_rows(da, sv["h2"], "wgrad_gate" + t)
    g["wu"] = _wgrad_rows(db, sv["h2"], "wgrad_up" + t)
    gout = sp["gout"] + ffn_hook(g)
    do, delta, duv, dm, g["gout"], g["gsv"], g["psc"], g["wsp"], g["bsp"], g["wbd"] = _mix_out_bwd(
        dx1, sv["o"], sv["z"], sv["m"], sp["wsp"], sp["bsp"], sp["wbd"], sp["psc"], sp["gsv"], gout, sv["wout"],
        "mix_out_bwd" + t)
    g["wout"] = _wgrad(sv["mix"], dx1, "wgrad_out" + t)
    dp = _pool_win_bwd(dm, "pool_win_bwd" + t)
    dq, dk, dv = _attn_bwd(sv["q"], sv["k"], sv["v"], do, sv["lse"], delta, out_hook(g), "attn_bwd" + t)
    dzm, g["wq"], g["wk"], g["wv"], g["gql"], g["gkv"], g["gq"], g["gk"] = _mla_prep_bwd(
        dq, dk, dv, sv["z"], tabs, sp["gql"], sp["gkv"], sp["gq"], sp["gk"], kw["wq"], kw["wk"], kw["wv"],
        "mla_prep_bwd" + t)
    dx, g["g_mix"] = _in_proj_bwd(dzm, duv, dp, sv["x"], dx1, sp["g_mix"], kw["win"], "in_proj_bwd" + t)
    g["win"] = _wgrad_in(sv["hb"], dzm, duv, dp, "wgrad_in" + t)
    return dx, g


def _rope_inv_freq():
    half = ROPE // 2
    inv = 1.0 / (ROPE_THETA ** (jnp.arange(half, dtype=F32) / half))
    return jnp.concatenate([jnp.zeros((NOPE,), F32), inv, inv, jnp.zeros((HP - QK,), F32)]).reshape(1, HP)


def kernel(x, positions, g_mix_norm, w_in, g_q_lat, w_q_up, g_kv_lat, w_kv_up, g_q_head, g_k_head, g_sgu_v, w_spatial, b_spatial, w_pool, pool_scale, g_out_mla, g_out_sgu, g_out_pool, w_out, g_ffn_norm, w_gate, w_up, w_down, loss_target, m_g_mix_norm, m_w_in, m_g_q_lat, m_w_q_up, m_g_kv_lat, m_w_kv_up, m_g_q_head, m_g_k_head, m_g_sgu_v, m_w_spatial, m_b_spatial, m_w_pool, m_pool_scale, m_g_out_mla, m_g_out_sgu, m_g_out_pool, m_w_out, m_g_ffn_norm, m_w_gate, m_w_up, m_w_down, v_g_mix_norm, v_w_in, v_g_q_lat, v_w_q_up, v_g_kv_lat, v_w_kv_up, v_g_q_head, v_g_k_head, v_g_sgu_v, v_w_spatial, v_b_spatial, v_w_pool, v_pool_scale, v_g_out_mla, v_g_out_sgu, v_g_out_pool, v_w_out, v_g_ffn_norm, v_w_gate, v_w_up, v_w_down):
    given = dict(locals())
    p = {n: given[n] for n in ORDER}
    view = lambda pre, n: jnp.swapaxes(given[pre + n], 1, 2) if n in TRANSPOSED else given[pre + n]
    seq = x.shape[1]
    where = jnp.stack([2 * lax.axis_index("x") + lax.axis_index("y"), lax.axis_index("c")]).astype(jnp.int32)
    shards = lambda names: [view("", n)[l].astype(BF16) for l, n in names]
    zero11 = lambda token: token[:1, :1]

    names_0a = [(0, n) for n in EARLY_BIG]
    names_0b = [(0, n) for n in LATE_BIG]
    names_1 = [(1, n) for n, _, _ in BIG]
    got_0a = dict(zip(EARLY_BIG, _all_gather_chips(shards(names_0a), "all_gather_w0a")))
    started, issued = {}, got_0a["w_in"]
    for tag, names in (("w0b", names_0b), ("w1", names_1)):
        sh = shards(names)
        pairs = _gather_pairs([a.shape[0] // 2 for a in sh], [_row_align(a.dtype) for a in sh])
        lands = [jax.ShapeDtypeStruct((CHIPS,) + a.shape, a.dtype) for a in sh]
        started[tag] = (sh, pairs) + _split_start(sh, lands, 3 * len(sh), pairs, "gather_start_" + tag, issued)
        issued = started[tag][6]

    def arrived(tag, after):
        _, pairs, send, recv, srcs, lands, _ = started[tag]
        srcs, lands = _split_wait(send, recv, srcs, lands, after, pairs, "gather_wait_" + tag)
        return _gather_finish(srcs, lands, "gather_finish_" + tag)

    layer1 = {}

    def mix_weights(l, h):
        if l == 0:
            return got_0a
        layer1.update(zip([n for _, n in names_1], arrived("w1", h)))
        return layer1

    def late_weights(l, o):
        return arrived("w0b", o) if l == 0 else [layer1[n] for n in LATE_BIG]

    reducing, last = {}, {}

    def reduce_start(tag, arrs):
        lands = [jax.ShapeDtypeStruct((PEERS, a.shape[1] // 2, a.shape[2]), a.dtype) for a in arrs]
        reducing[tag] = _split_start(arrs, lands, PEERS * len(arrs), _scatter_pairs, "grad_scatter_start_" + tag, where)
        return zero11(reducing[tag][4])

    def reduce_finish(tag, after):
        send, recv, srcs, lands, _ = reducing[tag]
        srcs, lands = _split_wait(send, recv, srcs, lands, after, _scatter_pairs, "grad_scatter_wait_" + tag)
        sums = [None] * len(srcs)
        for shape in dict.fromkeys(a.shape for a in srcs):
            idx = [i for i, a in enumerate(srcs) if a.shape == shape]
            res = _sum_own_and_landed([srcs[i] for i in idx], [lands[i] for i in idx], where, f"grad_sum_{tag}_{idx[0]}")
            for i, r in zip(idx, res):
                sums[i] = r
        return sums

    def ffn_hook(l, g):
        if l == 1:
            return jnp.zeros((1, 1), F32)
        return reduce_start("g0b", [g["wg"], g["wu"], g["wd"]])

    def out_hook(l, g):
        if l == 1:
            return where
        reduce_start("g0c", [g["wout"].reshape(CHIPS, D // CHIPS, D)])
        return reducing["g0c"][4]

    def layer_hook(l, big, small):
        last[l] = (big, small)
        if l == 1:
            return reduce_start("g1", [big[n] for n, _, _ in BIG])
        return None

    entry = zero11(started["w0b"][6]) + zero11(started["w1"][6])
    loss_part, dx = _step(x.reshape(seq, D), positions.reshape(seq, 1), loss_target.reshape(seq, D), p, entry,
                          mix_weights, late_weights, ffn_hook, out_hook, layer_hook)

    def adamw(n, g0, g1):
        flip = n in EARLY_BIG
        pick = lambda pre: jnp.swapaxes(given[pre + n], 1, 2) if flip else view(pre, n)
        w = pick("")
        three_d = (DEPTH, -1, w.shape[-1])
        g0, g1 = (g.T if flip else g for g in (g0, g1))
        res = _adamw(w.reshape(three_d), g0.reshape(three_d[1:]), g1.reshape(three_d[1:]),
                     pick("m_").reshape(three_d), pick("v_").reshape(three_d), "adamw_" + n)
        return [jnp.swapaxes(r.reshape(w.shape), 1, 2) if flip else r.reshape(w.shape) for r in res]

    names_rest = [(0, n) for n in EARLY_BIG]
    reduce_start("g0a", [last[0][0][n] for _, n in names_rest]
                 + [_pack_small_grads([last[l][1] for l in range(DEPTH)], loss_part)])
    token = reducing["g0a"][4]
    early = names_1 + [(0, n) for n in FFN_BIG] + [(0, "w_out")]
    landed = reduce_finish("g1", token) + reduce_finish("g0b", token) + reduce_finish("g0c", token)
    sums = dict(zip(early, _pair_join(landed, "grad_pair_join_early")))
    out = {n: adamw(n, sums[(0, n)], sums[(1, n)]) for n in FFN_BIG}
    late = names_rest + ["small"]
    sums.update(zip(late, _pair_join(reduce_finish("g0a", out["w_down"][1]), "grad_pair_join_late")))
    gsmall, loss = _unpack_small_grads(_all_gather_chips([sums["small"]], "all_gather_small_grads")[0])
    vectors = [n for n, shape in SMALL if len(shape) == 1]
    res = _adamw_vectors([given[n] for n in vectors], *[[gsmall[l][n].reshape(1, -1) for n in vectors] for l in range(DEPTH)],
                         [given["m_" + n] for n in vectors], [given["v_" + n] for n in vectors], "adamw_vectors")
    out.update({n: res[i::len(vectors)] for i, n in enumerate(vectors)})
    for n in ORDER:
        if n not in out:
            g = [sums[(l, n)] for l in range(DEPTH)] if (0, n) in sums else [gsmall[l][n] for l in range(DEPTH)]
            out[n] = adamw(n, *g)
    undo = lambda n, a: jnp.swapaxes(a, 1, 2) if n in TRANSPOSED else a
    return (loss, dx.reshape(x.shape), *[undo(n, out[n][i]) for i in range(4) for n in ORDER])


def _step(xs, pos, tgt, p, entry, mix_weights, late_weights, ffn_hook, out_hook, layer_hook):
    sps = [_small_operands(p, l) for l in range(DEPTH)]
    sps[0]["g_mix"] = sps[0]["g_mix"] + entry
    tabs = _rope_tables(pos, _rope_inv_freq())
    saved, h = [], xs
    for l in range(DEPTH):
        kw = _kernel_weights(mix_weights(l, h))
        h, sv = _layer_fwd(h, tabs, kw, functools.partial(late_weights, l), sps[l], l, tgt if l == DEPTH - 1 else None)
        saved.append(dict(sv, kw=kw))
    dy, lpart = h
    for l in reversed(range(DEPTH)):
        dy, g = _layer_bwd(dy, saved[l], tabs, saved[l]["kw"], sps[l], l, functools.partial(ffn_hook, l),
                           functools.partial(out_hook, l))
        zero = layer_hook(l, _big_grads(g), _small_grads(g))
        if zero is not None and l > 0:
            sps[l - 1]["g_ffn"] = sps[l - 1]["g_ffn"] + zero
    return 0.5 / D * jnp.sum(lpart), dy
```

```python
import functools
import math

import jax
import jax.numpy as jnp
from jax import lax
from jax.experimental import pallas as pl
from jax.experimental.pallas import tpu as pltpu

F32 = jnp.float32
BF16 = jnp.bfloat16
MESH = pl.DeviceIdType.MESH

D = 1024
HEADS = 4
QK = 96
NOPE = 64
ROPE = 32
VH = 128
HP = 128
QL = 256
KVL = 128
SGU = 256
POOL = 256
CHUNK = 128
HID = 2816
CHIPS = 4
SH = HID // CHIPS
IN_W = 1184
IN_P = 1280
EPS = 1e-6
ROPE_THETA = 10000.0
SCALE = 1.0 / math.sqrt(QK)
LOG2E = 1.4426950408889634
EXP2_C = SCALE * LOG2E
ATT_WIDE = 2
ATT_FWD_QUERIES = 2048
ATT_PIECE = 1024
ATT_ROWS = 256
ATT_KEYS = 1024
ATT_QUERIES = 2048
NEG = -1e30
HALO = 16

LR, B1, B2, ADAM_EPS, WD, STEP = 0.001, 0.9, 0.999, 1e-08, 0.01, 10

VMEM_LIMIT = 56 * 1024 * 1024
LANES = 128
TOKENS = 1024


def _cp(sem, vmem=None):
    return pltpu.CompilerParams(dimension_semantics=sem, vmem_limit_bytes=vmem)


def _res(shape):
    nd = len(shape)
    return pl.BlockSpec(shape, lambda *_: (0,) * nd, pipeline_mode=pl.Buffered(1))


def _acc(shape):
    nd = len(shape)
    return pl.BlockSpec(shape, lambda *_: (0,) * nd)


def _dot(a, b):
    return jnp.dot(a, b, preferred_element_type=F32)


def _dot_nt(a, b):
    return lax.dot_general(a, b, (((1,), (1,)), ((), ())), preferred_element_type=F32)


def _dot_tn(a, b):
    return lax.dot_general(a, b, (((0,), (0,)), ((), ())), preferred_element_type=F32)


def _rms(x, n):
    r = lax.rsqrt(jnp.sum(x * x, axis=-1, keepdims=True) * (1.0 / n) + EPS)
    return x * r, r


def _head_ones():
    row = lax.broadcasted_iota(jnp.int32, (HEADS * HP, HEADS * HP), 0) // HP
    col = lax.broadcasted_iota(jnp.int32, (HEADS * HP, HEADS * HP), 1) // HP
    return (row == col).astype(BF16)


def _head_sum(x, ones):
    return _dot(x.astype(BF16), ones)


def _head_rms(x, ones):
    r = lax.rsqrt(_head_sum(x * x, ones) * (1.0 / QK) + EPS)
    return x * r, r


def _rms_bwd(xn, r, g, dy, n):
    dn = dy * g
    dx = r * (dn - xn * (jnp.sum(dn * xn, axis=-1, keepdims=True) * (1.0 / n)))
    return dx, jnp.sum(dy * xn, axis=0, keepdims=True)


def _accumulate(ref, val, first):
    @pl.when(first)
    def _():
        ref[...] = val

    @pl.when(jnp.logical_not(first))
    def _():
        ref[...] += val


def _accumulate0(ref, val, first):
    @pl.when(first)
    def _():
        ref[0] = val

    @pl.when(jnp.logical_not(first))
    def _():
        ref[0] += val


def _tile(s, t):
    return min(s, t)


def _row_tile(r, cap):
    if r <= cap:
        return r
    return max(t for t in range(8, cap + 1, 8) if r % t == 0)


def _rope_table_values(pos, invf):
    ang = pos.astype(F32) * invf
    c, sn = jnp.cos(ang), jnp.sin(ang)
    lane = lax.broadcasted_iota(jnp.int32, ang.shape, 1)
    first = (lane >= NOPE) & (lane < NOPE + ROPE // 2)
    second = (lane >= NOPE + ROPE // 2) & (lane < QK)
    return jnp.where(first | second, c, 1.0), jnp.where(first, -sn, 0.0), jnp.where(second, sn, 0.0)


def _rope_tables_meanwhile(pos, invf):
    s = pos.shape[0]
    tm = _tile(s, 256)

    def work(ins, outs):
        pos_ref, invf_ref = ins

        def step(i, carry):
            rows = pl.ds(pl.multiple_of(i * tm, tm), tm)
            for o_ref, val in zip(outs, _rope_table_values(pos_ref[rows, :], invf_ref[...])):
                o_ref[rows, :] = val
            return carry

        lax.fori_loop(0, s // tm, step, 0)

    return work, [pos, invf], [jax.ShapeDtypeStruct((s, HP), F32)] * 3


def _rope_tables(pos, invf):
    s = pos.shape[0]
    tm = _tile(s, 1024)

    def body(pos_ref, invf_ref, c_ref, sa_ref, sb_ref):
        c_ref[...], sa_ref[...], sb_ref[...] = _rope_table_values(pos_ref[...], invf_ref[...])

    out = jax.ShapeDtypeStruct((s, HP), F32)
    return pl.pallas_call(
        body, name="rope_tables", grid=(s // tm,),
        in_specs=[pl.BlockSpec((tm, 1), lambda i: (i, 0)), _acc((1, HP))],
        out_specs=[pl.BlockSpec((tm, HP), lambda i: (i, 0))] * 3,
        out_shape=[out] * 3, compiler_params=_cp(("parallel",)),
    )(pos, invf)


def _rope(x, c, sa, sb):
    return x * c + pltpu.roll(x, HP - ROPE // 2, 1) * sa + pltpu.roll(x, ROPE // 2, 1) * sb


def _rope_t(d, c, sa, sb):
    return d * c + pltpu.roll(d * sa, ROPE // 2, 1) + pltpu.roll(d * sb, HP - ROPE // 2, 1)


def _in_proj_fwd(x, g, w, name):
    s = x.shape[0]
    tm = _tile(s, TOKENS)

    def body(x_ref, g_ref, w_ref, z_ref, h_ref):
        xn, _ = _rms(x_ref[...], D)
        h = (xn * g_ref[...]).astype(BF16)
        h_ref[...] = h
        z_ref[...] = _dot(h, w_ref[...])

    return pl.pallas_call(
        body, name=name, grid=(s // tm,),
        in_specs=[pl.BlockSpec((tm, D), lambda i: (i, 0)), _acc((1, D)), _res((D, IN_P))],
        out_specs=[pl.BlockSpec((tm, IN_P), lambda i: (i, 0)), pl.BlockSpec((tm, D), lambda i: (i, 0))],
        out_shape=[jax.ShapeDtypeStruct((s, IN_P), F32), jax.ShapeDtypeStruct((s, D), BF16)],
        compiler_params=_cp(("parallel",), VMEM_LIMIT),
    )(x, g, w)


def _mla_prep_fwd(z, tabs, gql, gkv, gq, gk, wq, wk, wv, name):
    s = z.shape[0]
    tm = _tile(s, TOKENS)

    def body(ql_ref, kv_ref, kr_ref, c_ref, sa_ref, sb_ref, gql_ref, gkv_ref, gq_ref, gk_ref,
             wq_ref, wk_ref, wv_ref, q_out, k_out, v_out):
        qn = (_rms(ql_ref[...], QL)[0] * gql_ref[...]).astype(BF16)
        kvn = (_rms(kv_ref[...], KVL)[0] * gkv_ref[...]).astype(BF16)
        qraw = _dot(qn, wq_ref[...])
        kraw = _dot(kvn, wk_ref[...])
        vraw = _dot(kvn, wv_ref[...])
        kr = kr_ref[...]
        c, sa, sb = c_ref[...], sa_ref[...], sb_ref[...]
        ones = _head_ones()
        xq_all = _head_rms(qraw, ones)[0]
        xk_all = _head_rms(kraw + jnp.concatenate([kr] * HEADS, axis=1), ones)[0]
        for h in range(HEADS):
            sl = slice(h * HP, (h + 1) * HP)
            q_out[h] = (_rope(xq_all[:, sl] * gq_ref[...], c, sa, sb) * EXP2_C).astype(BF16)
            k_out[h] = _rope(xk_all[:, sl] * gk_ref[...], c, sa, sb).astype(BF16)
            v_out[h] = vraw[:, sl].astype(BF16)

    row = lambda w, j: pl.BlockSpec((tm, w), lambda i: (i, j))
    hspec = pl.BlockSpec((HEADS, tm, HP), lambda i: (0, i, 0))
    hshape = jax.ShapeDtypeStruct((HEADS, s, HP), BF16)
    return pl.pallas_call(
        body, name=name, grid=(s // tm,),
        in_specs=[row(QL, 0), row(KVL, 2), row(HP, 3), row(HP, 0), row(HP, 0), row(HP, 0),
                  _acc((1, QL)), _acc((1, KVL)), _acc((1, HP)), _acc((1, HP)),
                  _acc((QL, HEADS * HP)), _acc((KVL, HEADS * HP)), _acc((KVL, HEADS * HP))],
        out_specs=[hspec] * 3, out_shape=[hshape] * 3,
        compiler_params=_cp(("parallel",)),
    )(z, z, z, *tabs, gql, gkv, gq, gk, wq, wk, wv)


def _causal_mask(s, row0):
    row = lax.broadcasted_iota(jnp.int32, s.shape, 0) + row0
    col = lax.broadcasted_iota(jnp.int32, s.shape, 1)
    return jnp.where(col <= row, s, NEG)


def _attn_fwd(q, k, v, name):
    s = q.shape[1]
    tq = _tile(s, ATT_FWD_QUERIES)
    rh = _tile(s, ATT_ROWS)
    kp = _tile(s, ATT_PIECE)
    wide = ATT_WIDE * kp if s % (ATT_WIDE * kp) == 0 else tq
    groups = tq // rh

    def body(q_ref, k_ref, v_ref, o_ref, lse_ref):
        i = pl.program_id(1)

        def blk(off, tk, carry, diagonal):
            width = lambda g, t: max(0, min(kp, (g + 1) * rh - t * kp)) if diagonal else kp
            rows = lambda t: pl.ds(pl.multiple_of(off + t * kp, kp), kp)
            score = lambda g, t: _dot_nt(q_ref[0, g * rh:(g + 1) * rh, :], k_ref[0, rows(t), :][:width(g, t)])
            live = lambda t: [g for g in range(groups) if width(g, t) > 0]
            state = list(carry)
            scs = {(g, 0): score(g, 0) for g in live(0)}
            for t in range(tk // kp):
                if (t + 1) * kp < tk:
                    scs.update({(g, t + 1): score(g, t + 1) for g in live(t + 1)})
                vt = v_ref[0, rows(t), :]
                for g in live(t):
                    m, l, acc = state[g]
                    sc = scs.pop((g, t))
                    if diagonal and (g + 1) * rh <= (t + 1) * kp:
                        sc = _causal_mask(sc, g * rh - t * kp)
                    m_new = jnp.maximum(m, jnp.max(sc, axis=-1, keepdims=True))
                    p = jnp.exp2(sc - m_new)
                    alpha = jnp.exp2(m - m_new)
                    l = alpha * l + jnp.sum(p, axis=-1, keepdims=True)
                    acc = alpha * acc + _dot(p.astype(BF16), vt[:width(g, t)])
                    state[g] = (m_new, l, acc)
            return tuple(state)

        one = (jnp.full((rh, 1), NEG, F32), jnp.zeros((rh, 1), F32), jnp.zeros((rh, VH), F32))
        nwide = (i * tq) // wide
        carry = lax.fori_loop(0, nwide, lambda j, c: blk(j * wide, wide, c, False), (one,) * groups)
        carry = lax.fori_loop(nwide * (wide // tq), i, lambda j, c: blk(j * tq, tq, c, False), carry)
        carry = blk(i * tq, tq, carry, True)
        for g, (m, l, acc) in enumerate(carry):
            o_ref[g * rh:(g + 1) * rh, :] = acc / l
            lse_ref[0, g * rh:(g + 1) * rh, :] = jnp.broadcast_to(m + jnp.log(l) * LOG2E, (rh, LANES))

    return pl.pallas_call(
        body, name=name, grid=(HEADS, s // tq),
        in_specs=[pl.BlockSpec((1, tq, HP), lambda h, i: (h, i, 0)),
                  pl.BlockSpec((1, s, HP), lambda h, i: (h, 0, 0)),
                  pl.BlockSpec((1, s, HP), lambda h, i: (h, 0, 0))],
        out_specs=[pl.BlockSpec((tq, VH), lambda h, i: (i, h)),
                   pl.BlockSpec((1, tq, LANES), lambda h, i: (h, i, 0))],
        out_shape=[jax.ShapeDtypeStruct((s, HEADS * VH), F32), jax.ShapeDtypeStruct((HEADS, s, LANES), F32)],
        compiler_params=_cp(("parallel", "arbitrary"), VMEM_LIMIT),
    )(q, k, v)


def _lane_group(shape, j):
    return (lax.broadcasted_iota(jnp.int32, shape, 1) + j * LANES) // (POOL // 4)


def _pool_win_fwd(z, name):
    s = z.shape[0]
    ch = _tile(s, 512)
    col0 = (IN_P - POOL) // LANES

    def body(p_ref, m_ref):
        j = pl.program_id(0)

        def chunk(r, _):
            off = pl.multiple_of(r * ch, ch)
            cur = p_ref[pl.ds(off, ch), :]
            hoff = pl.multiple_of(jnp.maximum(off - HALO, 0), 8)
            halo = jnp.where(r > 0, p_ref[pl.ds(hoff, HALO), :], 0.0)
            x = jnp.concatenate([halo, cur], axis=0)
            s2 = x + pltpu.roll(x, 1, 0)
            s4 = s2 + pltpu.roll(s2, 2, 0)
            s8 = s4 + pltpu.roll(s4, 4, 0)
            s16 = s8 + pltpu.roll(s8, 8, 0)
            grp = _lane_group((ch, LANES), j)
            sel = jnp.where(grp == 0, s2[HALO:], jnp.where(grp == 1, s4[HALO:], jnp.where(grp == 2, s8[HALO:], s16[HALO:])))
            t1 = (lax.broadcasted_iota(jnp.int32, (ch, LANES), 0) + off + 1).astype(F32)
            win = jnp.where(grp == 0, 2.0, jnp.where(grp == 1, 4.0, jnp.where(grp == 2, 8.0, 16.0)))
            m_ref[pl.ds(off, ch), :] = sel / jnp.minimum(t1, win) - cur
            return 0

        lax.fori_loop(0, s // ch, chunk, 0)

    return pl.pallas_call(
        body, name=name, grid=(POOL // LANES,),
        in_specs=[pl.BlockSpec((s, LANES), lambda j: (0, col0 + j))],
        out_specs=pl.BlockSpec((s, LANES), lambda j: (0, j)),
        out_shape=jax.ShapeDtypeStruct((s, POOL), F32),
        compiler_params=_cp(("parallel",), VMEM_LIMIT),
    )(z)


def _pool_win_bwd(dm, name):
    s = dm.shape[0]
    ch = _tile(s, 512)
    n = s // ch

    def body(dm_ref, dp_ref):
        j = pl.program_id(0)

        def chunk(r, _):
            off = pl.multiple_of(r * ch, ch)
            grp = _lane_group((ch + HALO, LANES), j)
            win = jnp.where(grp == 0, 2.0, jnp.where(grp == 1, 4.0, jnp.where(grp == 2, 8.0, 16.0)))
            cur = dm_ref[pl.ds(off, ch), :]
            hoff = pl.multiple_of(jnp.minimum(off + ch, s - HALO), 8)
            halo = jnp.where(r < n - 1, dm_ref[pl.ds(hoff, HALO), :], 0.0)
            x = jnp.concatenate([cur, halo], axis=0)
            t1 = (lax.broadcasted_iota(jnp.int32, (ch + HALO, LANES), 0) + off + 1).astype(F32)
            e = x / jnp.minimum(t1, win)
            tot = ch + HALO
            r2 = e + pltpu.roll(e, tot - 1, 0)
            r4 = r2 + pltpu.roll(r2, tot - 2, 0)
            r8 = r4 + pltpu.roll(r4, tot - 4, 0)
            r16 = r8 + pltpu.roll(r8, tot - 8, 0)
            g = grp[:ch]
            sel = jnp.where(g == 0, r2[:ch], jnp.where(g == 1, r4[:ch], jnp.where(g == 2, r8[:ch], r16[:ch])))
            dp_ref[pl.ds(off, ch), :] = (sel - cur).astype(BF16)
            return 0

        lax.fori_loop(0, n, chunk, 0)

    return pl.pallas_call(
        body, name=name, grid=(POOL // LANES,),
        in_specs=[pl.BlockSpec((s, LANES), lambda j: (0, j))],
        out_specs=pl.BlockSpec((s, LANES), lambda j: (0, j)),
        out_shape=jax.ShapeDtypeStruct((s, POOL), BF16),
        compiler_params=_cp(("parallel",), VMEM_LIMIT),
    )(dm)


def _head_mask(h):
    lane = lax.broadcasted_iota(jnp.int32, (CHUNK, SGU), 1)
    return (lane // (SGU // HEADS)) == h


def _tril(upper=False):
    row = lax.broadcasted_iota(jnp.int32, (CHUNK, CHUNK), 0)
    col = lax.broadcasted_iota(jnp.int32, (CHUNK, CHUNK), 1)
    return col >= row if upper else col <= row


def _sgu_gate(vn, wsp, bsp):
    out = []
    for cidx in range(vn.shape[0] // CHUNK):
        vc = vn[cidx * CHUNK:(cidx + 1) * CHUNK]
        zc = bsp
        for h in range(HEADS):
            zc = zc + jnp.where(_head_mask(h), _dot(wsp[h], vc), 0.0)
        out.append(zc)
    return jnp.concatenate(out, axis=0)


def _mix_out_fwd(o, z, m, x, wsp, bsp, wbd, psc, gsv, gout, wout, name):
    s = x.shape[0]
    tm = _tile(s, TOKENS)

    def body(o_ref, uv_ref, m_ref, x_ref, wsp_ref, bsp_ref, wbd_ref, psc_ref, gsv_ref, gout_ref, wout_ref,
             x1_ref, mix_ref):
        g = gout_ref[...]
        an = _rms(o_ref[...], HEADS * VH)[0] * g[:, :512]
        uv = uv_ref[...]
        u, v = uv[:, :SGU], uv[:, SGU:]
        vn = (_rms(v, SGU)[0] * gsv_ref[...]).astype(BF16)
        tri = _tril()
        wsp_m = [jnp.where(tri, wsp_ref[h], 0.0).astype(BF16) for h in range(HEADS)]
        gm = u * _sgu_gate(vn, wsp_m, bsp_ref[...])
        gn = _rms(gm, SGU)[0] * g[:, 512:768]
        po = _dot(m_ref[...].astype(BF16), wbd_ref[...]) * psc_ref[...]
        pn = _rms(po, POOL)[0] * g[:, 768:]
        mix = jnp.concatenate([an, gn, pn], axis=1).astype(BF16)
        mix_ref[...] = mix
        x1_ref[...] = x_ref[...] + _dot(mix, wout_ref[...])

    row = lambda w, j: pl.BlockSpec((tm, w), lambda i: (i, j))
    return pl.pallas_call(
        body, name=name, grid=(s // tm,),
        in_specs=[row(512, 0), row(512, 1), row(POOL, 0), row(D, 0),
                  _acc((HEADS, CHUNK, CHUNK)), _acc((CHUNK, SGU)), _acc((POOL, POOL)), _acc((1, POOL)),
                  _acc((1, SGU)), _acc((1, D)), _res((D, D))],
        out_specs=[row(D, 0), row(D, 0)],
        out_shape=[jax.ShapeDtypeStruct((s, D), F32), jax.ShapeDtypeStruct((s, D), BF16)],
        compiler_params=_cp(("parallel",), VMEM_LIMIT),
    )(o, z, m, x, wsp, bsp, wbd, psc, gsv, gout, wout)


def _ffn_fwd(x1, g, wg, wu, wd, tgt, name):
    s = x1.shape[0]
    tm = _tile(s, 256)
    last = tgt is not None

    def body(x_ref, g_ref, wg_ref, wu_ref, wd_ref, *rest):
        t_ref = rest[0] if last else None
        outs = rest[1:] if last else rest
        a_ref, b_ref, h_ref = outs[-3:]
        x = x_ref[...]
        h = (_rms(x, D)[0] * g_ref[...]).astype(BF16)
        h_ref[...] = h
        acc = jnp.zeros((tm, D), F32)
        for k in range(CHIPS):
            a = _dot_nt(h, wg_ref[k])
            b = _dot_nt(h, wu_ref[k])
            a_ref[k] = a
            b_ref[k] = b
            acc = acc + _dot((a * jax.nn.sigmoid(a) * b).astype(BF16), wd_ref[k])
        if not last:
            outs[0][...] = x + acc
            return
        dy_ref, l_ref = outs[:2]
        e = (x + acc) - t_ref[...]
        dy_ref[...] = e * (1.0 / D)
        sq = jnp.sum(e * e, axis=0, keepdims=True)
        part = sq[:, :LANES]
        for c in range(1, D // LANES):
            part = part + sq[:, c * LANES:(c + 1) * LANES]
        _accumulate(l_ref, part, pl.program_id(0) == 0)

    row = lambda w: pl.BlockSpec((tm, w), lambda i: (i, 0))
    hrow = pl.BlockSpec((CHIPS, tm, SH), lambda i: (0, i, 0))
    hshape = jax.ShapeDtypeStruct((CHIPS, s, SH), F32)
    tail_specs = [hrow, hrow, row(D)]
    tail_shapes = [hshape, hshape, jax.ShapeDtypeStruct((s, D), BF16)]
    head_specs = [row(D), _acc((1, LANES))] if last else [row(D)]
    head_shapes = [jax.ShapeDtypeStruct((s, D), F32)] + ([jax.ShapeDtypeStruct((1, LANES), F32)] if last else [])
    res = pl.pallas_call(
        body, name=name, grid=(s // tm,),
        in_specs=[row(D), _acc((1, D)), _res((CHIPS, SH, D)), _res((CHIPS, SH, D)), _res((CHIPS, SH, D))]
        + ([row(D)] if last else []),
        out_specs=head_specs + tail_specs, out_shape=head_shapes + tail_shapes,
        compiler_params=_cp(("arbitrary",), VMEM_LIMIT),
    )(x1, g, wg, wu, wd, *([tgt] if last else []))
    return (tuple(res[:2]) if last else res[0]), res[-3], res[-2], res[-1]


def _wgrad(a, b, name):
    s, k = a.shape
    n = b.shape[1]
    half = lambda v: v if v <= 1408 else v // 2
    kb, nb, tt = half(k), half(n), _tile(s, 2048)

    def body(a_ref, b_ref, o_ref):
        _accumulate(o_ref, _dot_tn(a_ref[...].astype(BF16), b_ref[...].astype(BF16)), pl.program_id(2) == 0)

    return pl.pallas_call(
        body, name=name, grid=(k // kb, n // nb, s // tt),
        in_specs=[pl.BlockSpec((tt, kb), lambda i, j, t: (t, i)), pl.BlockSpec((tt, nb), lambda i, j, t: (t, j))],
        out_specs=pl.BlockSpec((kb, nb), lambda i, j, t: (i, j)),
        out_shape=jax.ShapeDtypeStruct((k, n), F32),
        compiler_params=_cp(("parallel", "parallel", "arbitrary"), VMEM_LIMIT),
    )(a, b)


def _wgrad_in(h, dzm, duv, dp, name):
    s = h.shape[0]
    tt = _tile(s, 2048)

    def body(h_ref, a_ref, b_ref, c_ref, o_ref):
        hv = h_ref[...]
        val = jnp.concatenate([_dot_tn(hv, a_ref[...]), _dot_tn(hv, b_ref[...]), _dot_tn(hv, c_ref[...])], axis=1)
        _accumulate(o_ref, val, pl.program_id(0) == 0)

    row = lambda w: pl.BlockSpec((tt, w), lambda t: (t, 0))
    return pl.pallas_call(
        body, name=name, grid=(s // tt,), in_specs=[row(D), row(512), row(512), row(POOL)], out_specs=_acc((D, IN_P)),
        out_shape=jax.ShapeDtypeStruct((D, IN_P), F32), compiler_params=_cp(("arbitrary",), VMEM_LIMIT),
    )(h, dzm, duv, dp)


def _wgrad_rows(a, b, name):
    s, n = a.shape[1:]
    nn = b.shape[1]
    tt = _tile(s, 4096 if b.dtype == BF16 else 2048)

    def body(a_ref, b_ref, o_ref):
        _accumulate0(o_ref, _dot_tn(a_ref[0].astype(BF16), b_ref[...].astype(BF16)), pl.program_id(1) == 0)

    return pl.pallas_call(
        body, name=name, grid=(CHIPS, s // tt),
        in_specs=[pl.BlockSpec((1, tt, n), lambda c, t: (c, t, 0)), pl.BlockSpec((tt, nn), lambda c, t: (t, 0))],
        out_specs=pl.BlockSpec((1, n, nn), lambda c, t: (c, 0, 0)),
        out_shape=jax.ShapeDtypeStruct((CHIPS, n, nn), F32),
        compiler_params=_cp(("parallel", "arbitrary"), VMEM_LIMIT),
    )(a, b)


def _ffn_bwd(dx2, x1, a, b, g, wg, wu, wd, name):
    s = x1.shape[0]
    tm = _tile(s, 256)

    def body(dx2_ref, x_ref, a_ref, b_ref, g_ref, wg_ref, wu_ref, wd_ref,
             dx1_ref, hid_ref, da_ref, db_ref, dyb_ref, dg_ref):
        dx2 = dx2_ref[...]
        dyb = dx2.astype(BF16)
        dyb_ref[...] = dyb
        dh = jnp.zeros((tm, D), F32)
        ahead = _dot_nt(dyb, wd_ref[0])
        for k in range(CHIPS):
            av, bv = a_ref[k], b_ref[k]
            dhid = ahead
            if k + 1 < CHIPS:
                ahead = _dot_nt(dyb, wd_ref[k + 1])
            sig = jax.nn.sigmoid(av)
            sa = av * sig
            hid_ref[k] = (sa * bv).astype(BF16)
            dbv = (dhid * sa).astype(BF16)
            dav = (dhid * bv * (sig * (1.0 + av * (1.0 - sig)))).astype(BF16)
            db_ref[k] = dbv
            da_ref[k] = dav
            dh = dh + _dot(dav, wg_ref[k]) + _dot(dbv, wu_ref[k])
        xn, r = _rms(x_ref[...], D)
        dxr, dg = _rms_bwd(xn, r, g_ref[...], dh, D)
        dx1_ref[...] = dx2 + dxr
        _accumulate(dg_ref, dg, pl.program_id(0) == 0)

    row = lambda w: pl.BlockSpec((tm, w), lambda i: (i, 0))
    hrow = pl.BlockSpec((CHIPS, tm, SH), lambda i: (0, i, 0))
    hid = jax.ShapeDtypeStruct((CHIPS, s, SH), BF16)
    return pl.pallas_call(
        body, name=name, grid=(s // tm,),
        in_specs=[row(D), row(D), hrow, hrow, _acc((1, D)), _res((CHIPS, SH, D)), _res((CHIPS, SH, D)),
                  _res((CHIPS, SH, D))],
        out_specs=[row(D), hrow, hrow, hrow, row(D), _acc((1, D))],
        out_shape=[jax.ShapeDtypeStruct((s, D), F32), hid, hid, hid, jax.ShapeDtypeStruct((s, D), BF16),
                   jax.ShapeDtypeStruct((1, D), F32)],
        compiler_params=_cp(("arbitrary",), VMEM_LIMIT),
    )(dx2, x1, a, b, g, wg, wu, wd)


def _mix_out_bwd(dx1, o, z, m, wsp, bsp, wbd, psc, gsv, gout, wout, name):
    s = dx1.shape[0]
    tm = _tile(s, TOKENS)

    def body(dx1_ref, o_ref, uv_ref, m_ref, wsp_ref, bsp_ref, wbd_ref, psc_ref, gsv_ref, gout_ref, wout_ref,
             do_ref, dl_ref, duv_ref, dm_ref, dgo_ref, dgsv_ref, dpsc_ref, dwsp_ref, dbsp_ref, dwbd_ref):
        first = pl.program_id(0) == 0
        g = gout_ref[...]
        dmix = _dot_nt(dx1_ref[...].astype(BF16), wout_ref[...])
        o = o_ref[...]
        on, ro = _rms(o, HEADS * VH)
        do, dga = _rms_bwd(on, ro, g[:, :512], dmix[:, :512], HEADS * VH)
        for h in range(HEADS):
            sl = slice(h * VH, (h + 1) * VH)
            do_ref[h] = do[:, sl].astype(BF16)
            dl_ref[h] = jnp.broadcast_to(jnp.sum(do[:, sl] * o[:, sl], axis=-1, keepdims=True), (tm, LANES))
        uv = uv_ref[...]
        u, v = uv[:, :SGU], uv[:, SGU:]
        vx, rv = _rms(v, SGU)
        vn = (vx * gsv_ref[...]).astype(BF16)
        tri = _tril()
        wsp_m = [jnp.where(tri, wsp_ref[h], 0.0).astype(BF16) for h in range(HEADS)]
        zc = _sgu_gate(vn, wsp_m, bsp_ref[...])
        gm = u * zc
        gmn, rg = _rms(gm, SGU)
        dgm, dgg = _rms_bwd(gmn, rg, g[:, 512:768], dmix[:, 512:768], SGU)
        du = dgm * zc
        dzc = dgm * u
        dvn_parts = []
        dbsp = jnp.zeros((CHUNK, SGU), F32)
        dwsp = [jnp.zeros((CHUNK, CHUNK), F32) for _ in range(HEADS)]
        for cidx in range(tm // CHUNK):
            rs = slice(cidx * CHUNK, (cidx + 1) * CHUNK)
            dzc_c = dzc[rs]
            dbsp = dbsp + dzc_c
            dzb = dzc_c.astype(BF16)
            vc = vn[rs]
            dvn_c = jnp.zeros((CHUNK, SGU), F32)
            for h in range(HEADS):
                hm = _head_mask(h)
                dvn_c = dvn_c + jnp.where(hm, _dot_tn(wsp_m[h], dzb), 0.0)
                dwsp[h] = dwsp[h] + _dot_nt(jnp.where(hm, dzc_c, 0.0).astype(BF16), vc)
            dvn_parts.append(dvn_c)
        dvn = jnp.concatenate(dvn_parts, axis=0)
        dv, dgsv = _rms_bwd(vx, rv, gsv_ref[...], dvn, SGU)
        duv_ref[...] = jnp.concatenate([du, dv], axis=1).astype(BF16)
        mb = m_ref[...].astype(BF16)
        pw = _dot(mb, wbd_ref[...])
        po = pw * psc_ref[...]
        pon, rp = _rms(po, POOL)
        dpo, dgp = _rms_bwd(pon, rp, g[:, 768:], dmix[:, 768:], POOL)
        dpw = (dpo * psc_ref[...]).astype(BF16)
        dm_ref[...] = _dot_nt(dpw, wbd_ref[...])
        _accumulate(dgo_ref, jnp.concatenate([dga, dgg, dgp], axis=1), first)
        _accumulate(dgsv_ref, dgsv, first)
        _accumulate(dpsc_ref, jnp.sum(dpo * pw, axis=0, keepdims=True), first)
        _accumulate(dbsp_ref, dbsp, first)
        _accumulate(dwbd_ref, _dot_tn(mb, dpw), first)
        for h in range(HEADS):
            val = jnp.where(tri, dwsp[h], 0.0)

            @pl.when(first)
            def _(val=val, h=h):
                dwsp_ref[h] = val

            @pl.when(jnp.logical_not(first))
            def _(val=val, h=h):
                dwsp_ref[h] += val

    row = lambda w, j: pl.BlockSpec((tm, w), lambda i: (i, j))
    hspec = pl.BlockSpec((HEADS, tm, HP), lambda i: (0, i, 0))
    return pl.pallas_call(
        body, name=name, grid=(s // tm,),
        in_specs=[row(D, 0), row(512, 0), row(512, 1), row(POOL, 0),
                  _acc((HEADS, CHUNK, CHUNK)), _acc((CHUNK, SGU)),
                  _acc((POOL, POOL)), _acc((1, POOL)), _acc((1, SGU)), _acc((1, D)), _res((D, D))],
        out_specs=[hspec, hspec, row(512, 0), row(POOL, 0), _acc((1, D)), _acc((1, SGU)), _acc((1, POOL)),
                   _acc((HEADS, CHUNK, CHUNK)), _acc((CHUNK, SGU)), _acc((POOL, POOL))],
        out_shape=[jax.ShapeDtypeStruct((HEADS, s, HP), BF16), jax.ShapeDtypeStruct((HEADS, s, LANES), F32),
                   jax.ShapeDtypeStruct((s, 512), BF16), jax.ShapeDtypeStruct((s, POOL), F32),
                   jax.ShapeDtypeStruct((1, D), F32), jax.ShapeDtypeStruct((1, SGU), F32),
                   jax.ShapeDtypeStruct((1, POOL), F32), jax.ShapeDtypeStruct((HEADS, CHUNK, CHUNK), F32),
                   jax.ShapeDtypeStruct((CHUNK, SGU), F32), jax.ShapeDtypeStruct((POOL, POOL), F32)],
        compiler_params=_cp(("arbitrary",), VMEM_LIMIT),
    )(dx1, o, z, m, wsp, bsp, wbd, psc, gsv, gout, wout)


def _attn_bwd(q, k, v, do, lse, delta, after, name):
    s = q.shape[1]
    rh = _tile(s, ATT_ROWS)
    tk = _tile(s, ATT_KEYS)
    nk = s // tk
    wide = ATT_QUERIES if s % ATT_QUERIES == 0 else tk
    pieces = tk // rh

    def body(q_ref, k_ref, v_ref, do_ref, lse_ref, dl_ref, after_ref, dq_ref, dk_ref, dv_ref):
        del after_ref
        j = pl.program_id(1)

        @pl.when(j == 0)
        def _():
            dq_ref[...] = jnp.zeros_like(dq_ref)

        kj, vj = k_ref[0], v_ref[0]

        def blk(start, rows, dks, dvs, diagonal):
            dks, dvs = list(dks), list(dvs)
            offs = [pl.multiple_of(start + g * rh, rh) for g in range(rows // rh)]
            keys = [(g + 1) * rh if diagonal else tk for g in range(rows // rh)]
            qs = [q_ref[0, pl.ds(off, rh), :] for off in offs]
            dos = [do_ref[0, pl.ds(off, rh), :] for off in offs]
            scs = [_dot_nt(qi, kj[:n]) for qi, n in zip(qs, keys)]
            dps = [_dot_nt(doi, vj[:n]) for doi, n in zip(dos, keys)]
            for g, off in enumerate(offs):
                lse_i = lse_ref[0, pl.ds(off, rh), :][:, :1]
                dl_i = dl_ref[0, pl.ds(off, rh), :][:, :1]
                sc = _causal_mask(scs[g], g * rh) if diagonal else scs[g]
                p = jnp.exp2(sc - lse_i)
                ds = (p * (dps[g] - dl_i)).astype(BF16)
                cv = _dot_tn(p.astype(BF16), dos[g])
                ck = _dot_tn(ds, qs[g])
                for t in range(keys[g] // rh):
                    dvs[t] = dvs[t] + cv[t * rh:(t + 1) * rh]
                    dks[t] = dks[t] + ck[t * rh:(t + 1) * rh]
                dq_ref[0, pl.ds(off, rh), :] += _dot(ds, kj[:keys[g]]) * SCALE
            return tuple(dks), tuple(dvs)

        per = wide // tk
        zero = (jnp.zeros((rh, HP), F32),) * pieces
        acc = blk(j * tk, tk, zero, zero, True)
        first_wide = (j + per) // per
        acc = lax.fori_loop(j + 1, jnp.minimum(first_wide * per, nk), lambda i, c: blk(i * tk, tk, *c, False), acc)
        dks, dvs = lax.fori_loop(first_wide, nk // per, lambda i, c: blk(i * wide, wide, *c, False), acc)
        dk_ref[0] = jnp.concatenate(dks, axis=0) * (SCALE / EXP2_C)
        dv_ref[0] = jnp.concatenate(dvs, axis=0)

    full = lambda: pl.BlockSpec((1, s, HP), lambda h, j: (h, 0, 0))
    blk_spec = lambda: pl.BlockSpec((1, tk, HP), lambda h, j: (h, j, 0))
    out = jax.ShapeDtypeStruct((HEADS, s, HP), F32)
    return pl.pallas_call(
        body, name=name, grid=(HEADS, s // tk),
        in_specs=[full(), blk_spec(), blk_spec(), full(), full(), full(), ANY],
        out_specs=[full(), blk_spec(), blk_spec()], out_shape=[out] * 3,
        compiler_params=_cp(("parallel", "arbitrary"), VMEM_LIMIT),
    )(q, k, v, do, lse, delta, after)


def _mla_prep_bwd(dq, dk, dv, z, tabs, gql, gkv, gq, gk, wq, wk, wv, name):
    s = z.shape[0]
    tm = _tile(s, TOKENS)

    def body(dq_ref, dk_ref, dv_ref, ql_ref, kv_ref, kr_ref, c_ref, sa_ref, sb_ref, gql_ref, gkv_ref, gq_ref, gk_ref,
             wq_ref, wk_ref, wv_ref,
             dz_ref, dwq_ref, dwk_ref, dwv_ref, dgql_ref, dgkv_ref, dgq_ref, dgk_ref, dqr_ref, dkr_ref, dvr_ref):
        first = pl.program_id(0) == 0
        qx, rq = _rms(ql_ref[...], QL)
        qn = (qx * gql_ref[...]).astype(BF16)
        kx, rk = _rms(kv_ref[...], KVL)
        kvn = (kx * gkv_ref[...]).astype(BF16)
        qraw = _dot(qn, wq_ref[...])
        kraw = _dot(kvn, wk_ref[...])
        kr = kr_ref[...]
        c, sa, sb = c_ref[...], sa_ref[...], sb_ref[...]
        lane = lax.broadcasted_iota(jnp.int32, (tm, HP), 1)
        rope_lanes = (lane >= NOPE) & (lane < QK)
        ones = _head_ones()
        heads = lambda f: jnp.concatenate([f(h) for h in range(HEADS)], axis=1)
        fold = lambda v: sum(v[:, h * HP:(h + 1) * HP] for h in range(HEADS))

        def head_rms_bwd(x, g_ref, d_ref):
            xn, r = _head_rms(x, ones)
            dy = heads(lambda h: _rope_t(d_ref[h], c, sa, sb))
            dn = dy * jnp.concatenate([g_ref[...]] * HEADS, axis=1)
            dx = r * (dn - xn * (_head_sum(dn * xn, ones) * (1.0 / QK)))
            return dx, fold(jnp.sum(dy * xn, axis=0, keepdims=True))

        dxq, dgq = head_rms_bwd(qraw, gq_ref, dq_ref)
        dxk, dgk = head_rms_bwd(kraw + jnp.concatenate([kr] * HEADS, axis=1), gk_ref, dk_ref)
        dqr_ref[...] = dxq.astype(BF16)
        dkr_ref[...] = dxk.astype(BF16)
        dvr_ref[...] = heads(lambda h: dv_ref[h]).astype(BF16)
        dkrope = jnp.where(rope_lanes, fold(dxk), 0.0)
        dqn = _dot_nt(dqr_ref[...], wq_ref[...])
        dql, dgql = _rms_bwd(qx, rq, gql_ref[...], dqn, QL)
        dkvn = _dot_nt(dkr_ref[...], wk_ref[...]) + _dot_nt(dvr_ref[...], wv_ref[...])
        dkv, dgkv = _rms_bwd(kx, rk, gkv_ref[...], dkvn, KVL)
        dz_ref[...] = jnp.concatenate([dql, dkv, dkrope], axis=1).astype(BF16)
        _accumulate(dwq_ref, _dot_tn(qn, dqr_ref[...]), first)
        _accumulate(dwk_ref, _dot_tn(kvn, dkr_ref[...]), first)
        _accumulate(dwv_ref, _dot_tn(kvn, dvr_ref[...]), first)
        _accumulate(dgql_ref, dgql, first)
        _accumulate(dgkv_ref, dgkv, first)
        _accumulate(dgq_ref, dgq, first)
        _accumulate(dgk_ref, dgk, first)

    row = lambda w, j: pl.BlockSpec((tm, w), lambda i: (i, j))
    hspec = pl.BlockSpec((HEADS, tm, HP), lambda i: (0, i, 0))
    acc = lambda r, c: (_acc((r, c)), jax.ShapeDtypeStruct((r, c), F32))
    outs = [(row(512, 0), jax.ShapeDtypeStruct((s, 512), BF16)), acc(QL, HEADS * HP), acc(KVL, HEADS * HP),
            acc(KVL, HEADS * HP), acc(1, QL), acc(1, KVL), acc(1, HP), acc(1, HP)]
    return pl.pallas_call(
        body, name=name, grid=(s // tm,),
        in_specs=[hspec, hspec, hspec, row(QL, 0), row(KVL, 2), row(HP, 3), row(HP, 0), row(HP, 0), row(HP, 0),
                  _acc((1, QL)), _acc((1, KVL)), _acc((1, HP)), _acc((1, HP)),
                  _acc((QL, HEADS * HP)), _acc((KVL, HEADS * HP)), _acc((KVL, HEADS * HP))],
        out_specs=[o[0] for o in outs], out_shape=[o[1] for o in outs],
        scratch_shapes=[pltpu.VMEM((tm, HEADS * HP), BF16)] * 3,
        compiler_params=_cp(("arbitrary",), VMEM_LIMIT),
    )(dq, dk, dv, z, z, z, *tabs, gql, gkv, gq, gk, wq, wk, wv)


def _in_proj_bwd(dzm, duv, dp, x, dx1, g, win, name):
    s = x.shape[0]
    tm = _tile(s, TOKENS // 2)

    def body(dzm_ref, duv_ref, dp_ref, x_ref, dx1_ref, g_ref, w_ref, dx_ref, dg_ref):
        groups = [slice(r0, r0 + tm // 2) for r0 in (0, tm // 2)]
        dhs = [_dot_nt(dzm_ref[rs, :], w_ref[:, 0:512]) + _dot_nt(duv_ref[rs, :], w_ref[:, 512:1024])
               + _dot_nt(dp_ref[rs, :], w_ref[:, 1024:IN_P]) for rs in groups]
        dg = jnp.zeros((1, D), F32)
        for rs, dh in zip(groups, dhs):
            xn, r = _rms(x_ref[rs, :], D)
            dxr, dgr = _rms_bwd(xn, r, g_ref[...], dh, D)
            dx_ref[rs, :] = dx1_ref[rs, :] + dxr
            dg = dg + dgr
        _accumulate(dg_ref, dg, pl.program_id(0) == 0)

    row = lambda w: pl.BlockSpec((tm, w), lambda i: (i, 0))
    return pl.pallas_call(
        body, name=name, grid=(s // tm,),
        in_specs=[row(512), row(512), row(POOL), row(D), row(D), _acc((1, D)), _res((D, IN_P))],
        out_specs=[row(D), _acc((1, D))],
        out_shape=[jax.ShapeDtypeStruct((s, D), F32), jax.ShapeDtypeStruct((1, D), F32)],
        compiler_params=_cp(("arbitrary",), VMEM_LIMIT),
    )(dzm, duv, dp, x, dx1, g, win)


def _adamw(w, g0, g1, m, v, name):
    _, r, c = w.shape
    tr = _row_tile(r, 512)
    c1 = 1.0 - B1 ** STEP
    c2 = 1.0 - B2 ** STEP

    def body(w_ref, g0_ref, g1_ref, m_ref, v_ref, g_ref, d_ref, nm_ref, nv_ref):
        gv = jnp.where(pl.program_id(0) == 0, g0_ref[...], g1_ref[...])
        g_ref[0] = gv
        nm = B1 * m_ref[0] + (1.0 - B1) * gv
        nv = B2 * v_ref[0] + (1.0 - B2) * (gv * gv)
        nm_ref[0] = nm
        nv_ref[0] = nv
        d_ref[0] = -LR * ((nm / c1) / (jnp.sqrt(nv / c2) + ADAM_EPS) + WD * w_ref[0])

    spec = pl.BlockSpec((1, tr, c), lambda l, i: (l, i, 0))
    out = jax.ShapeDtypeStruct((DEPTH, r, c), F32)
    return pl.pallas_call(
        body, name=name, grid=(DEPTH, r // tr),
        in_specs=[spec, pl.BlockSpec((tr, c), lambda l, i: (i * (1 - l), 0)), pl.BlockSpec((tr, c), lambda l, i: (i * l, 0)),
                  spec, spec],
        out_specs=[spec] * 4, out_shape=[out] * 4, compiler_params=_cp(("parallel", "parallel")),
    )(w, g0, g1, m, v)


def _adamw_vectors(ws, g0s, g1s, ms, vs, name):
    k = len(ws)
    c1 = 1.0 - B1 ** STEP
    c2 = 1.0 - B2 ** STEP

    def body(*refs):
        w_refs, g0_refs, g1_refs, m_refs, v_refs, g_out, d_out, m_out, v_out = (refs[i * k:(i + 1) * k] for i in range(9))
        for i in range(k):
            for l, g_ref in enumerate((g0_refs[i], g1_refs[i])):
                row = slice(l, l + 1)
                gv = g_ref[...]
                g_out[i][row, :] = gv
                nm = B1 * m_refs[i][row, :] + (1.0 - B1) * gv
                nv = B2 * v_refs[i][row, :] + (1.0 - B2) * (gv * gv)
                m_out[i][row, :] = nm
                v_out[i][row, :] = nv
                d_out[i][row, :] = -LR * ((nm / c1) / (jnp.sqrt(nv / c2) + ADAM_EPS) + WD * w_refs[i][row, :])

    out = [jax.ShapeDtypeStruct(w.shape, F32) for w in ws]
    return pl.pallas_call(body, name=name, out_shape=out * 4)(*ws, *g0s, *g1s, *ms, *vs)


ANY = pl.BlockSpec(memory_space=pl.ANY)


def _place():
    x, y, c = lax.axis_index("x"), lax.axis_index("y"), lax.axis_index("c")
    chips = [(1 - x, y), (x, 1 - y), (1 - x, 1 - y)]
    return x, y, c, chips


def _half_rows(ref, lead, hh, half, align):
    rows = pl.ds(pl.multiple_of(hh * half, align), half)
    return ref.at[rows, :] if lead is None else ref.at[lead, rows, :]


def _row_align(dtype):
    return 16 if dtype == BF16 else 8


def _sems(n):
    return [pltpu.SemaphoreType.DMA((n,)), pltpu.SemaphoreType.DMA((n,)), pltpu.SemaphoreType.DMA((n,))]


def _comm_call(body, ins, out_shapes, nsems, name):
    return pl.pallas_call(
        body, name=name, in_specs=[ANY] * len(ins), out_specs=[ANY] * len(out_shapes), out_shape=out_shapes,
        scratch_shapes=_sems(nsems), compiler_params=pltpu.CompilerParams(has_side_effects=True),
    )(*ins)


def _all_gather_chips(shards, name, meanwhile=None):
    n = len(shards)
    halves = [a.shape[0] // 2 for a in shards]
    aligns = [_row_align(a.dtype) for a in shards]
    assert all(h % al == 0 for h, al in zip(halves, aligns))
    work, work_ins, work_outs = meanwhile or (None, [], [])
    k, m = len(work_ins), len(work_outs)

    def body(*refs):
        ins, outs, (send_sems, recv_sems, _) = refs[:n], refs[n + k:2 * n + k], refs[2 * n + k + m:]
        x, y, c, chips = _place()
        me = 2 * x + y
        sibling = (x, y, 1 - c)

        def copy(sem, src, dst, to):
            return pltpu.make_async_remote_copy(src_ref=src, dst_ref=dst, send_sem=send_sems.at[sem],
                                                recv_sem=recv_sems.at[sem], device_id=to, device_id_type=MESH)

        first, passed = [], []
        for a in range(n):
            my_half = _half_rows(ins[a], None, c, halves[a], aligns[a])
            for j, (cx, cy) in enumerate(chips):
                cp = copy(6 * a + j, my_half, _half_rows(outs[a], me, c, halves[a], aligns[a]), (cx, cy, c))
                cp.start()
                first.append(cp)
        if work is not None:
            work(refs[n:n + k], refs[2 * n + k:2 * n + k + m])
        for a in range(n):
            for j, (cx, cy) in enumerate(chips):
                landed = _half_rows(outs[a], 2 * cx + cy, c, halves[a], aligns[a])
                copy(6 * a + j, landed, landed, (cx, cy, c)).wait_recv()
                fwd = copy(6 * a + 3 + j, landed, landed, sibling)
                fwd.start()
                passed.append(fwd)
        for a in range(n):
            for j, (cx, cy) in enumerate(chips):
                other = _half_rows(outs[a], 2 * cx + cy, 1 - c, halves[a], aligns[a])
                copy(6 * a + 3 + j, other, other, sibling).wait_recv()
        for cp in first + passed:
            cp.wait_send()

    land_shapes = [jax.ShapeDtypeStruct((CHIPS,) + a.shape, a.dtype) for a in shards]
    if work is None:
        return _with_own(_comm_call(body, shards, land_shapes, 6 * n, name), shards)
    vmem = pl.BlockSpec(memory_space=pltpu.VMEM)
    res = pl.pallas_call(
        body, name=name, in_specs=[ANY] * n + [vmem] * k, out_specs=[ANY] * n + [vmem] * m,
        out_shape=land_shapes + list(work_outs), scratch_shapes=_sems(6 * n),
        compiler_params=pltpu.CompilerParams(has_side_effects=True, vmem_limit_bytes=VMEM_LIMIT),
    )(*shards, *work_ins)
    return _with_own(res[:n], shards), res[n:]


def _with_own(lands, shards):
    me = 2 * lax.axis_index("x") + lax.axis_index("y")
    return [lax.dynamic_update_slice(g, a[None], (me, 0, 0)) for g, a in zip(lands, shards)]


def _pair_join(arrs, name):
    n = len(arrs)
    halves = [a.shape[0] // 2 for a in arrs]

    def body(*refs):
        outs, (send_sems, recv_sems, _) = refs[n:2 * n], refs[2 * n:]
        x, y, c, _ = _place()
        cps = []
        for a in range(n):
            mine = _half_rows(outs[a], None, c, halves[a], 8)
            cp = pltpu.make_async_remote_copy(src_ref=mine, dst_ref=mine, send_sem=send_sems.at[a], recv_sem=recv_sems.at[a],
                                              device_id=(x, y, 1 - c), device_id_type=MESH)
            cp.start()
            cps.append(cp)
        for cp in cps:
            cp.wait()

    return pl.pallas_call(
        body, name=name, in_specs=[ANY] * n, out_specs=[ANY] * n,
        out_shape=[jax.ShapeDtypeStruct(a.shape, a.dtype) for a in arrs],
        input_output_aliases={i: i for i in range(n)}, scratch_shapes=_sems(n),
        compiler_params=pltpu.CompilerParams(has_side_effects=True),
    )(*arrs)


HBM = pl.BlockSpec(memory_space=pltpu.HBM)
SEM = pl.BlockSpec(memory_space=pltpu.SEMAPHORE)
DATAFLOW = pltpu.SideEffectType.DATAFLOW_SIDE_EFFECTING


def _remote_copies(pairs, ins, lands, send_sems, recv_sems):
    return [pltpu.make_async_remote_copy(src_ref=src, dst_ref=dst, send_sem=send_sems.at[i], recv_sem=recv_sems.at[i],
                                         device_id=to, device_id_type=MESH)
            for i, (src, dst, to) in enumerate(pairs(ins, lands))]


def _split_start(srcs, land_shapes, ncopies, pairs, name, after):
    n, m = len(srcs), len(land_shapes)

    def body(*refs):
        ins, lands = refs[:n], refs[n:n + m]
        send_sems, recv_sems, token = refs[n + m + 1], refs[n + m + 2], refs[-1]
        for cp in _remote_copies(pairs, ins, lands, send_sems, recv_sems):
            cp.start()
        token[...] = jnp.zeros_like(token)

    hbm = lambda a: pltpu.with_memory_space_constraint(a, pltpu.HBM)
    lands = [hbm(lax.empty(s.shape, s.dtype)) for s in land_shapes]
    thru = [pltpu.HBM(a.shape, a.dtype) for a in list(srcs) + lands]
    out = pl.pallas_call(
        body, name=name,
        out_shape=(pltpu.SemaphoreType.DMA((ncopies,)), pltpu.SemaphoreType.DMA((ncopies,)), *thru,
                   jax.ShapeDtypeStruct((8, LANES), F32)),
        in_specs=[HBM] * (n + m) + [ANY], out_specs=(SEM, SEM, *[HBM] * (n + m), pl.BlockSpec(memory_space=pltpu.VMEM)),
        input_output_aliases={i: 2 + i for i in range(n + m)},
        compiler_params=pltpu.CompilerParams(has_side_effects=DATAFLOW),
    )(*[hbm(a) for a in srcs], *lands, after)
    return out[0], out[1], list(out[2:2 + n]), list(out[2 + n:2 + n + m]), out[-1]


def _split_wait(send_sems, recv_sems, srcs, lands, after, pairs, name):
    n, m = len(srcs), len(lands)

    def body(*refs):
        ins, lands_ = refs[:n], refs[n:n + m]
        for cp in _remote_copies(pairs, ins, lands_, refs[n + m], refs[n + m + 1]):
            cp.wait_send()
            cp.wait_recv()

    out = pl.pallas_call(
        body, name=name, out_shape=tuple(pltpu.HBM(a.shape, a.dtype) for a in list(srcs) + list(lands)),
        in_specs=[HBM] * (n + m) + [SEM, SEM, ANY], out_specs=tuple([HBM] * (n + m)),
        input_output_aliases={i: i for i in range(n + m)},
        compiler_params=pltpu.CompilerParams(has_side_effects=DATAFLOW),
    )(*srcs, *lands, send_sems, recv_sems, after)
    return list(out[:n]), list(out[n:])


def _gather_pairs(halves, aligns):
    def pairs(ins, lands):
        x, y, c, chips = _place()
        me = 2 * x + y
        return [(_half_rows(ins[a], None, c, halves[a], aligns[a]), _half_rows(lands[a], me, c, halves[a], aligns[a]),
                 (cx, cy, c)) for a in range(len(ins)) for cx, cy in chips]
    return pairs


PEERS = 7


def _scatter_pairs(ins, lands):
    x, y, c, chips = _place()
    to = [(cx, cy, c) for cx, cy in chips] + [(cx, cy, 1 - c) for cx, cy in chips] + [(x, y, 1 - c)]
    out = []
    for a in range(len(ins)):
        half = ins[a].shape[1] // 2
        for i, (tx, ty, tc) in enumerate(to):
            out.append((_half_rows(ins[a], 2 * tx + ty, tc, half, 8), lands[a].at[i], (tx, ty, tc)))
    return out


def _gather_finish(shards, lands, name):
    n = len(shards)
    halves = [a.shape[0] // 2 for a in shards]
    aligns = [_row_align(a.dtype) for a in shards]

    def body(*refs):
        outs, (send_sems, recv_sems, _) = refs[n:2 * n], refs[2 * n:]
        x, y, c, chips = _place()
        passed = []
        for a in range(n):
            for j, (cx, cy) in enumerate(chips):
                landed = _half_rows(outs[a], 2 * cx + cy, c, halves[a], aligns[a])
                cp = pltpu.make_async_remote_copy(src_ref=landed, dst_ref=landed, send_sem=send_sems.at[3 * a + j],
                                                  recv_sem=recv_sems.at[3 * a + j], device_id=(x, y, 1 - c),
                                                  device_id_type=MESH)
                cp.start()
                passed.append(cp)
        for a in range(n):
            for j, (cx, cy) in enumerate(chips):
                other = _half_rows(outs[a], 2 * cx + cy, 1 - c, halves[a], aligns[a])
                pltpu.make_async_remote_copy(src_ref=other, dst_ref=other, send_sem=send_sems.at[3 * a + j],
                                             recv_sem=recv_sems.at[3 * a + j], device_id=(x, y, 1 - c),
                                             device_id_type=MESH).wait_recv()
        for cp in passed:
            cp.wait_send()

    lands = pl.pallas_call(
        body, name=name, in_specs=[ANY] * n, out_specs=[ANY] * n,
        out_shape=[jax.ShapeDtypeStruct(a.shape, a.dtype) for a in lands],
        input_output_aliases={i: i for i in range(n)}, scratch_shapes=_sems(3 * n),
        compiler_params=pltpu.CompilerParams(has_side_effects=True),
    )(*lands)
    return _with_own(lands, shards)


def _sum_own_and_landed(owns, landeds, where, name):
    n = len(owns)
    _, half, cols = landeds[0].shape
    tr = _row_tile(half, 128)
    nt = half // tr

    grid_spec = pltpu.PrefetchScalarGridSpec(
        num_scalar_prefetch=1, grid=(nt,),
        in_specs=[pl.BlockSpec((1, tr, cols), lambda r, w: (w[0], w[1] * nt + r, 0))] * n
        + [pl.BlockSpec((PEERS, tr, cols), lambda r, w: (0, r, 0))] * n,
        out_specs=[pl.BlockSpec((tr, cols), lambda r, w: (w[1] * nt + r, 0))] * n)

    def body(w_ref, *refs):
        for p_ref, q_ref, o_ref in zip(refs[:n], refs[n:2 * n], refs[2 * n:]):
            acc = p_ref[0]
            for i in range(PEERS):
                acc = acc + q_ref[i]
            o_ref[...] = acc

    return pl.pallas_call(
        body, name=name, grid_spec=grid_spec, out_shape=[jax.ShapeDtypeStruct((2 * half, cols), owns[0].dtype)] * n,
        compiler_params=_cp(("parallel",), VMEM_LIMIT),
    )(where, *owns, *landeds)


BIG = [("w_in", (D, IN_W), 1), ("w_q_up", (QL, HEADS * QK), 1), ("w_kv_up", (KVL, HEADS * (NOPE + VH)), 1),
       ("w_out", (D, D), 0), ("w_gate", (D, HID), 1), ("w_up", (D, HID), 1), ("w_down", (HID, D), 0)]
SMALL = [("g_mix_norm", (D,)), ("g_q_lat", (QL,)), ("g_kv_lat", (KVL,)), ("g_q_head", (QK,)), ("g_k_head", (QK,)),
         ("g_sgu_v", (SGU,)), ("w_spatial", (HEADS, CHUNK, CHUNK)), ("b_spatial", (HEADS, CHUNK)),
         ("w_pool", (4, 64, 64)), ("pool_scale", (POOL,)), ("g_out_mla", (512,)), ("g_out_sgu", (SGU,)),
         ("g_out_pool", (POOL,)), ("g_ffn_norm", (D,))]
ORDER = ["g_mix_norm", "w_in", "g_q_lat", "w_q_up", "g_kv_lat", "w_kv_up", "g_q_head", "g_k_head", "g_sgu_v",
         "w_spatial", "b_spatial", "w_pool", "pool_scale", "g_out_mla", "g_out_sgu", "g_out_pool", "w_out",
         "g_ffn_norm", "w_gate", "w_up", "w_down"]
EARLY_BIG = ["w_in", "w_q_up", "w_kv_up"]
FFN_BIG = ["w_gate", "w_up", "w_down"]
LATE_BIG = ["w_out"] + FFN_BIG
DEPTH = 2
COLS = 1024
SMALL_N = sum(math.prod(s) for _, s in SMALL) * DEPTH
assert SMALL_N % CHIPS == 0
SMALL_ROWS = -(-(SMALL_N // CHIPS + 1) // (16 * COLS)) * 16


def _unsplit_cols(g):
    return g.transpose(1, 0, 2).reshape(g.shape[1], CHIPS * g.shape[2])


def _split_cols(full):
    r, c = full.shape
    return full.reshape(r, CHIPS, c // CHIPS).transpose(1, 0, 2)


def _kernel_weights(g):
    win = _unsplit_cols(g["w_in"])
    zeros = lambda r, c: jnp.zeros((r, c), BF16)
    o2, o3, o4 = QL + KVL, QL + KVL + ROPE, QL + KVL + ROPE + 2 * SGU
    win_p = jnp.concatenate([win[:, :o2], zeros(D, NOPE), win[:, o2:o3], zeros(D, HP - QK), win[:, o3:o4], win[:, o4:]], axis=1)
    wq = _unsplit_cols(g["w_q_up"]).reshape(QL, HEADS, QK)
    wq_p = jnp.pad(wq, ((0, 0), (0, 0), (0, HP - QK))).reshape(QL, HEADS * HP)
    wkv = _unsplit_cols(g["w_kv_up"]).reshape(KVL, HEADS, NOPE + VH)
    wk_p = jnp.pad(wkv[:, :, :NOPE], ((0, 0), (0, 0), (0, HP - NOPE))).reshape(KVL, HEADS * HP)
    wv_p = wkv[:, :, NOPE:].reshape(KVL, HEADS * VH)
    return dict(win=win_p, wq=wq_p, wk=wk_p, wv=wv_p)


def _small_operands(p, l):
    row = lambda v: v.reshape(1, -1)
    pad = lambda v: jnp.pad(v, (0, HP - QK)).reshape(1, HP)
    wpool = p["w_pool"][l]
    wbd = jnp.zeros((POOL, POOL), F32)
    for g in range(4):
        wbd = lax.dynamic_update_slice(wbd, wpool[g], (g * 64, g * 64))
    return dict(
        g_mix=row(p["g_mix_norm"][l]), gql=row(p["g_q_lat"][l]), gkv=row(p["g_kv_lat"][l]),
        gq=pad(p["g_q_head"][l]), gk=pad(p["g_k_head"][l]), gsv=row(p["g_sgu_v"][l]),
        wsp=p["w_spatial"][l], bsp=jnp.repeat(p["b_spatial"][l].T, SGU // HEADS, axis=1),
        wbd=wbd.astype(BF16), psc=row(p["pool_scale"][l]),
        gout=jnp.concatenate([p["g_out_mla"][l], p["g_out_sgu"][l], p["g_out_pool"][l]]).reshape(1, D),
        g_ffn=row(p["g_ffn_norm"][l]))


def _big_grads(g):
    dwin = g["win"]
    o2 = QL + KVL
    gin = jnp.concatenate([dwin[:, :o2], dwin[:, o2 + NOPE:o2 + NOPE + ROPE], dwin[:, 512:]], axis=1)
    gq = g["wq"].reshape(QL, HEADS, HP)[:, :, :QK].reshape(QL, HEADS * QK)
    gk = g["wk"].reshape(KVL, HEADS, HP)[:, :, :NOPE]
    gv = g["wv"].reshape(KVL, HEADS, VH)
    gkv = jnp.concatenate([gk, gv], axis=2).reshape(KVL, HEADS * (NOPE + VH))
    return {"w_in": _split_cols(gin), "w_q_up": _split_cols(gq), "w_kv_up": _split_cols(gkv),
            "w_out": g["wout"].reshape(CHIPS, D // CHIPS, D), "w_gate": g["wg"], "w_up": g["wu"], "w_down": g["wd"]}


TRANSPOSED = ("w_gate", "w_up")


def _small_grads(g):
    go = g["gout"].reshape(-1)
    return {"g_mix_norm": g["g_mix"].reshape(-1), "g_q_lat": g["gql"].reshape(-1), "g_kv_lat": g["gkv"].reshape(-1),
            "g_q_head": g["gq"].reshape(-1)[:QK], "g_k_head": g["gk"].reshape(-1)[:QK], "g_sgu_v": g["gsv"].reshape(-1),
            "w_spatial": g["wsp"], "b_spatial": g["bsp"].reshape(CHUNK, HEADS, SGU // HEADS).sum(-1).T,
            "w_pool": jnp.stack([g["wbd"][i * 64:(i + 1) * 64, i * 64:(i + 1) * 64] for i in range(4)]),
            "pool_scale": g["psc"].reshape(-1), "g_out_mla": go[:512], "g_out_sgu": go[512:768],
            "g_out_pool": go[768:], "g_ffn_norm": g["g_ffn"].reshape(-1)}


def _pack_small_grads(small, loss):
    sm = jnp.concatenate([small[l][n].reshape(-1) for l in range(DEPTH) for n, _ in SMALL]).reshape(CHIPS, SMALL_N // CHIPS)
    sm = jnp.pad(sm, ((0, 0), (0, SMALL_ROWS * COLS - SMALL_N // CHIPS)))
    return sm.at[0, SMALL_N // CHIPS].set(loss).reshape(CHIPS, SMALL_ROWS, COLS)


def _unpack_small_grads(gathered):
    rows = gathered.reshape(CHIPS, SMALL_ROWS * COLS)
    loss = rows[0, SMALL_N // CHIPS]
    flat = rows[:, :SMALL_N // CHIPS].reshape(-1)
    out, off = [], 0
    for _ in range(DEPTH):
        layer = {}
        for n, shape in SMALL:
            k = math.prod(shape)
            layer[n] = flat[off:off + k].reshape(shape)
            off += k
        out.append(layer)
    return out, loss


def _layer_fwd(x, tabs, kw, late_weights, sp, l, tgt):
    t = f"_l{l}"
    z, hb = _in_proj_fwd(x, sp["g_mix"], kw["win"], "in_proj_fwd" + t)
    q, k, v = _mla_prep_fwd(z, tabs, sp["gql"], sp["gkv"], sp["gq"], sp["gk"], kw["wq"], kw["wk"], kw["wv"],
                            "mla_prep_fwd" + t)
    o, lse = _attn_fwd(q, k, v, "attn_fwd" + t)
    m = _pool_win_fwd(z, "pool_win_fwd" + t)
    wout, wg, wu, wd = late_weights(o)
    wout = wout.reshape(D, D)
    x1, mix = _mix_out_fwd(o, z, m, x, sp["wsp"], sp["bsp"], sp["wbd"], sp["psc"], sp["gsv"], sp["gout"], wout,
                           "mix_out_fwd" + t)
    x2, a, b, h2 = _ffn_fwd(x1, sp["g_ffn"], wg, wu, wd, tgt, "ffn_fwd" + t)
    saved = dict(x=x, z=z, hb=hb, q=q, k=k, v=v, o=o, lse=lse, m=m, x1=x1, mix=mix, a=a, b=b, h2=h2, wg=wg, wu=wu, wd=wd,
                 wout=wout)
    return x2, saved


def _layer_bwd(dx2, sv, tabs, kw, sp, l, ffn_hook, out_hook):
    t = f"_l{l}"
    g = {}
    dx1, hid, da, db, dyb, g["g_ffn"] = _ffn_bwd(dx2, sv["x1"], sv["a"], sv["b"], sp["g_ffn"], sv["wg"], sv["wu"],
                                                 sv["wd"], "ffn_bwd" + t)
    g["wd"] = _wgrad_rows(hid, dyb, "wgrad_down" + t)
    g["wg"] = _wgrad_rows(da, sv["h2"], "wgrad_gate" + t)
    g["wu"] = _wgrad_rows(db, sv["h2"], "wgrad_up" + t)
    gout = sp["gout"] + ffn_hook(g)
    do, delta, duv, dm, g["gout"], g["gsv"], g["psc"], g["wsp"], g["bsp"], g["wbd"] = _mix_out_bwd(
        dx1, sv["o"], sv["z"], sv["m"], sp["wsp"], sp["bsp"], sp["wbd"], sp["psc"], sp["gsv"], gout, sv["wout"],
        "mix_out_bwd" + t)
    g["wout"] = _wgrad(sv["mix"], dx1, "wgrad_out" + t)
    dp = _pool_win_bwd(dm, "pool_win_bwd" + t)
    dq, dk, dv = _attn_bwd(sv["q"], sv["k"], sv["v"], do, sv["lse"], delta, out_hook(g), "attn_bwd" + t)
    dzm, g["wq"], g["wk"], g["wv"], g["gql"], g["gkv"], g["gq"], g["gk"] = _mla_prep_bwd(
        dq, dk, dv, sv["z"], tabs, sp["gql"], sp["gkv"], sp["gq"], sp["gk"], kw["wq"], kw["wk"], kw["wv"],
        "mla_prep_bwd" + t)
    dx, g["g_mix"] = _in_proj_bwd(dzm, duv, dp, sv["x"], dx1, sp["g_mix"], kw["win"], "in_proj_bwd" + t)
    g["win"] = _wgrad_in(sv["hb"], dzm, duv, dp, "wgrad_in" + t)
    return dx, g


def _rope_inv_freq():
    half = ROPE // 2
    inv = 1.0 / (ROPE_THETA ** (jnp.arange(half, dtype=F32) / half))
    return jnp.concatenate([jnp.zeros((NOPE,), F32), inv, inv, jnp.zeros((HP - QK,), F32)]).reshape(1, HP)


def kernel(x, positions, g_mix_norm, w_in, g_q_lat, w_q_up, g_kv_lat, w_kv_up, g_q_head, g_k_head, g_sgu_v, w_spatial, b_spatial, w_pool, pool_scale, g_out_mla, g_out_sgu, g_out_pool, w_out, g_ffn_norm, w_gate, w_up, w_down, loss_target, m_g_mix_norm, m_w_in, m_g_q_lat, m_w_q_up, m_g_kv_lat, m_w_kv_up, m_g_q_head, m_g_k_head, m_g_sgu_v, m_w_spatial, m_b_spatial, m_w_pool, m_pool_scale, m_g_out_mla, m_g_out_sgu, m_g_out_pool, m_w_out, m_g_ffn_norm, m_w_gate, m_w_up, m_w_down, v_g_mix_norm, v_w_in, v_g_q_lat, v_w_q_up, v_g_kv_lat, v_w_kv_up, v_g_q_head, v_g_k_head, v_g_sgu_v, v_w_spatial, v_b_spatial, v_w_pool, v_pool_scale, v_g_out_mla, v_g_out_sgu, v_g_out_pool, v_w_out, v_g_ffn_norm, v_w_gate, v_w_up, v_w_down):
    given = dict(locals())
    p = {n: given[n] for n in ORDER}
    view = lambda pre, n: jnp.swapaxes(given[pre + n], 1, 2) if n in TRANSPOSED else given[pre + n]
    seq = x.shape[1]
    where = jnp.stack([2 * lax.axis_index("x") + lax.axis_index("y"), lax.axis_index("c")]).astype(jnp.int32)
    shards = lambda names: [view("", n)[l].astype(BF16) for l, n in names]
    zero11 = lambda token: token[:1, :1]

    names_0a = [(0, n) for n in EARLY_BIG]
    names_0b = [(0, n) for n in LATE_BIG]
    names_1 = [(1, n) for n, _, _ in BIG]
    w0a, tabs = _all_gather_chips(shards(names_0a), "all_gather_w0a",
                                  _rope_tables_meanwhile(positions.reshape(seq, 1), _rope_inv_freq()))
    got_0a = dict(zip(EARLY_BIG, w0a))
    started, issued = {}, got_0a["w_in"]
    for tag, names in (("w0b", names_0b), ("w1", names_1)):
        sh = shards(names)
        pairs = _gather_pairs([a.shape[0] // 2 for a in sh], [_row_align(a.dtype) for a in sh])
        lands = [jax.ShapeDtypeStruct((CHIPS,) + a.shape, a.dtype) for a in sh]
        started[tag] = (sh, pairs) + _split_start(sh, lands, 3 * len(sh), pairs, "gather_start_" + tag, issued)
        issued = started[tag][6]

    def arrived(tag, after):
        _, pairs, send, recv, srcs, lands, _ = started[tag]
        srcs, lands = _split_wait(send, recv, srcs, lands, after, pairs, "gather_wait_" + tag)
        return _gather_finish(srcs, lands, "gather_finish_" + tag)

    layer1 = {}

    def mix_weights(l, h):
        if l == 0:
            return got_0a
        layer1.update(zip([n for _, n in names_1], arrived("w1", h)))
        return layer1

    def late_weights(l, o):
        return arrived("w0b", o) if l == 0 else [layer1[n] for n in LATE_BIG]

    reducing, last = {}, {}

    def reduce_start(tag, arrs):
        lands = [jax.ShapeDtypeStruct((PEERS, a.shape[1] // 2, a.shape[2]), a.dtype) for a in arrs]
        reducing[tag] = _split_start(arrs, lands, PEERS * len(arrs), _scatter_pairs, "grad_scatter_start_" + tag, where)
        return zero11(reducing[tag][4])

    def reduce_finish(tag, after):
        send, recv, srcs, lands, _ = reducing[tag]
        srcs, lands = _split_wait(send, recv, srcs, lands, after, _scatter_pairs, "grad_scatter_wait_" + tag)
        sums = [None] * len(srcs)
        for shape in dict.fromkeys(a.shape for a in srcs):
            idx = [i for i, a in enumerate(srcs) if a.shape == shape]
            res = _sum_own_and_landed([srcs[i] for i in idx], [lands[i] for i in idx], where, f"grad_sum_{tag}_{idx[0]}")
            for i, r in zip(idx, res):
                sums[i] = r
        return sums

    def ffn_hook(l, g):
        if l == 1:
            return jnp.zeros((1, 1), F32)
        return reduce_start("g0b", [g["wg"], g["wu"], g["wd"]])

    def out_hook(l, g):
        if l == 1:
            return where
        reduce_start("g0c", [g["wout"].reshape(CHIPS, D // CHIPS, D)])
        return reducing["g0c"][4]

    def layer_hook(l, big, small):
        last[l] = (big, small)
        if l == 1:
            return reduce_start("g1", [big[n] for n, _, _ in BIG])
        return None

    entry = zero11(started["w0b"][6]) + zero11(started["w1"][6])
    loss_part, dx = _step(x.reshape(seq, D), tuple(tabs), loss_target.reshape(seq, D), p, entry,
                          mix_weights, late_weights, ffn_hook, out_hook, layer_hook)

    def adamw(n, g0, g1):
        flip = n in EARLY_BIG
        pick = lambda pre: jnp.swapaxes(given[pre + n], 1, 2) if flip else view(pre, n)
        w = pick("")
        three_d = (DEPTH, -1, w.shape[-1])
        g0, g1 = (g.T if flip else g for g in (g0, g1))
        res = _adamw(w.reshape(three_d), g0.reshape(three_d[1:]), g1.reshape(three_d[1:]),
                     pick("m_").reshape(three_d), pick("v_").reshape(three_d), "adamw_" + n)
        return [jnp.swapaxes(r.reshape(w.shape), 1, 2) if flip else r.reshape(w.shape) for r in res]

    names_rest = [(0, n) for n in EARLY_BIG]
    reduce_start("g0a", [last[0][0][n] for _, n in names_rest]
                 + [_pack_small_grads([last[l][1] for l in range(DEPTH)], loss_part)])
    token = reducing["g0a"][4]
    early = names_1 + [(0, n) for n in FFN_BIG] + [(0, "w_out")]
    landed = reduce_finish("g1", token) + reduce_finish("g0b", token) + reduce_finish("g0c", token)
    sums = dict(zip(early, _pair_join(landed, "grad_pair_join_early")))
    out = {n: adamw(n, sums[(0, n)], sums[(1, n)]) for n in FFN_BIG}
    late = names_rest + ["small"]
    sums.update(zip(late, _pair_join(reduce_finish("g0a", out["w_down"][1]), "grad_pair_join_late")))
    gsmall, loss = _unpack_small_grads(_all_gather_chips([sums["small"]], "all_gather_small_grads")[0])
    vectors = [n for n, shape in SMALL if len(shape) == 1]
    res = _adamw_vectors([given[n] for n in vectors], *[[gsmall[l][n].reshape(1, -1) for n in vectors] for l in range(DEPTH)],
                         [given["m_" + n] for n in vectors], [given["v_" + n] for n in vectors], "adamw_vectors")
    out.update({n: res[i::len(vectors)] for i, n in enumerate(vectors)})
    for n in ORDER:
        if n not in out:
            g = [sums[(l, n)] for l in range(DEPTH)] if (0, n) in sums else [gsmall[l][n] for l in range(DEPTH)]
            out[n] = adamw(n, *g)
    undo = lambda n, a: jnp.swapaxes(a, 1, 2) if n in TRANSPOSED else a
    return (loss, dx.reshape(x.shape), *[undo(n, out[n][i]) for i in range(4) for n in ORDER])


def _step(xs, tabs, tgt, p, entry, mix_weights, late_weights, ffn_hook, out_hook, layer_hook):
    sps = [_small_operands(p, l) for l in range(DEPTH)]
    sps[0]["g_mix"] = sps[0]["g_mix"] + entry
    saved, h = [], xs
    for l in range(DEPTH):
        kw = _kernel_weights(mix_weights(l, h))
        h, sv = _layer_fwd(h, tabs, kw, functools.partial(late_weights, l), sps[l], l, tgt if l == DEPTH - 1 else None)
        saved.append(dict(sv, kw=kw))
    dy, lpart = h
    for l in reversed(range(DEPTH)):
        dy, g = _layer_bwd(dy, saved[l], tabs, saved[l]["kw"], sps[l], l, functools.partial(ffn_hook, l),
                           functools.partial(out_hook, l))
        zero = layer_hook(l, _big_grads(g), _small_grads(g))
        if zero is not None and l > 0:
            sps[l - 1]["g_ffn"] = sps[l - 1]["g_ffn"] + zero
    return 0.5 / D * jnp.sum(lpart), dy
```

```python
import functools
import math

import jax
import jax.numpy as jnp
from jax import lax
from jax.experimental import pallas as pl
from jax.experimental.pallas import tpu as pltpu

F32 = jnp.float32
BF16 = jnp.bfloat16
MESH = pl.DeviceIdType.MESH

D = 1024
HEADS = 4
QK = 96
NOPE = 64
ROPE = 32
VH = 128
HP = 128
QL = 256
KVL = 128
SGU = 256
POOL = 256
CHUNK = 128
HID = 2816
CHIPS = 4
SH = HID // CHIPS
IN_W = 1184
IN_P = 1280
EPS = 1e-6
ROPE_THETA = 10000.0
SCALE = 1.0 / math.sqrt(QK)
LOG2E = 1.4426950408889634
EXP2_C = SCALE * LOG2E
ATT_WIDE = 2
ATT_FWD_QUERIES = 2048
ATT_PIECE = 1024
ATT_ROWS = 256
ATT_KEYS = 1024
ATT_QUERIES = 2048
NEG = -1e30
HALO = 16

LR, B1, B2, ADAM_EPS, WD, STEP = 0.001, 0.9, 0.999, 1e-08, 0.01, 10

VMEM_LIMIT = 56 * 1024 * 1024
LANES = 128
TOKENS = 1024


def _cp(sem, vmem=None):
    return pltpu.CompilerParams(dimension_semantics=sem, vmem_limit_bytes=vmem)


def _res(shape):
    nd = len(shape)
    return pl.BlockSpec(shape, lambda *_: (0,) * nd, pipeline_mode=pl.Buffered(1))


def _acc(shape):
    nd = len(shape)
    return pl.BlockSpec(shape, lambda *_: (0,) * nd)


def _dot(a, b):
    return jnp.dot(a, b, preferred_element_type=F32)


def _dot_nt(a, b):
    return lax.dot_general(a, b, (((1,), (1,)), ((), ())), preferred_element_type=F32)


def _dot_tn(a, b):
    return lax.dot_general(a, b, (((0,), (0,)), ((), ())), preferred_element_type=F32)


def _rms(x, n):
    r = lax.rsqrt(jnp.sum(x * x, axis=-1, keepdims=True) * (1.0 / n) + EPS)
    return x * r, r


def _head_ones():
    row = lax.broadcasted_iota(jnp.int32, (HEADS * HP, HEADS * HP), 0) // HP
    col = lax.broadcasted_iota(jnp.int32, (HEADS * HP, HEADS * HP), 1) // HP
    return (row == col).astype(BF16)


def _head_sum(x, ones):
    return _dot(x.astype(BF16), ones)


def _head_rms(x, ones):
    r = lax.rsqrt(_head_sum(x * x, ones) * (1.0 / QK) + EPS)
    return x * r, r


def _rms_bwd(xn, r, g, dy, n):
    dn = dy * g
    dx = r * (dn - xn * (jnp.sum(dn * xn, axis=-1, keepdims=True) * (1.0 / n)))
    return dx, jnp.sum(dy * xn, axis=0, keepdims=True)


def _accumulate(ref, val, first):
    @pl.when(first)
    def _():
        ref[...] = val

    @pl.when(jnp.logical_not(first))
    def _():
        ref[...] += val


def _accumulate0(ref, val, first):
    @pl.when(first)
    def _():
        ref[0] = val

    @pl.when(jnp.logical_not(first))
    def _():
        ref[0] += val


def _tile(s, t):
    return min(s, t)


def _row_tile(r, cap):
    if r <= cap:
        return r
    return max(t for t in range(8, cap + 1, 8) if r % t == 0)


def _rope_tables_meanwhile(pos, invf):
    s = pos.shape[0]
    tm = _tile(s, 256)
    steps = s // tm // 2

    def work(ins, outs, part):
        pos_ref, invf_ref = ins
        c_ref, sa_ref, sb_ref = outs

        def step(i, carry):
            rows = pl.ds(pl.multiple_of(i * tm, tm), tm)
            ang = pos_ref[rows, :].astype(F32) * invf_ref[...]
            c, sn = jnp.cos(ang), jnp.sin(ang)
            lane = lax.broadcasted_iota(jnp.int32, ang.shape, 1)
            first = (lane >= NOPE) & (lane < NOPE + ROPE // 2)
            second = (lane >= NOPE + ROPE // 2) & (lane < QK)
            c_ref[rows, :] = jnp.where(first | second, c, 1.0)
            sa_ref[rows, :] = jnp.where(first, -sn, 0.0)
            sb_ref[rows, :] = jnp.where(second, sn, 0.0)
            return carry

        lax.fori_loop(part * steps, (part + 1) * steps, step, 0)

    return work, [pos, invf], [jax.ShapeDtypeStruct((s, HP), F32)] * 3


def _rope(x, c, sa, sb):
    return x * c + pltpu.roll(x, HP - ROPE // 2, 1) * sa + pltpu.roll(x, ROPE // 2, 1) * sb


def _rope_t(d, c, sa, sb):
    return d * c + pltpu.roll(d * sa, ROPE // 2, 1) + pltpu.roll(d * sb, HP - ROPE // 2, 1)


def _in_proj_fwd(x, g, w, name):
    s = x.shape[0]
    tm = _tile(s, TOKENS)

    def body(x_ref, g_ref, w_ref, z_ref, h_ref):
        xn, _ = _rms(x_ref[...], D)
        h = (xn * g_ref[...]).astype(BF16)
        h_ref[...] = h
        z_ref[...] = _dot(h, w_ref[...])

    return pl.pallas_call(
        body, name=name, grid=(s // tm,),
        in_specs=[pl.BlockSpec((tm, D), lambda i: (i, 0)), _acc((1, D)), _res((D, IN_P))],
        out_specs=[pl.BlockSpec((tm, IN_P), lambda i: (i, 0)), pl.BlockSpec((tm, D), lambda i: (i, 0))],
        out_shape=[jax.ShapeDtypeStruct((s, IN_P), F32), jax.ShapeDtypeStruct((s, D), BF16)],
        compiler_params=_cp(("parallel",), VMEM_LIMIT),
    )(x, g, w)


def _mla_prep_fwd(z, tabs, gql, gkv, gq, gk, wq, wk, wv, name):
    s = z.shape[0]
    tm = _tile(s, TOKENS)

    def body(ql_ref, kv_ref, kr_ref, c_ref, sa_ref, sb_ref, gql_ref, gkv_ref, gq_ref, gk_ref,
             wq_ref, wk_ref, wv_ref, q_out, k_out, v_out):
        qn = (_rms(ql_ref[...], QL)[0] * gql_ref[...]).astype(BF16)
        kvn = (_rms(kv_ref[...], KVL)[0] * gkv_ref[...]).astype(BF16)
        qraw = _dot(qn, wq_ref[...])
        kraw = _dot(kvn, wk_ref[...])
        vraw = _dot(kvn, wv_ref[...])
        kr = kr_ref[...]
        c, sa, sb = c_ref[...], sa_ref[...], sb_ref[...]
        ones = _head_ones()
        xq_all = _head_rms(qraw, ones)[0]
        xk_all = _head_rms(kraw + jnp.concatenate([kr] * HEADS, axis=1), ones)[0]
        for h in range(HEADS):
            sl = slice(h * HP, (h + 1) * HP)
            q_out[h] = (_rope(xq_all[:, sl] * gq_ref[...], c, sa, sb) * EXP2_C).astype(BF16)
            k_out[h] = _rope(xk_all[:, sl] * gk_ref[...], c, sa, sb).astype(BF16)
            v_out[h] = vraw[:, sl].astype(BF16)

    row = lambda w, j: pl.BlockSpec((tm, w), lambda i: (i, j))
    hspec = pl.BlockSpec((HEADS, tm, HP), lambda i: (0, i, 0))
    hshape = jax.ShapeDtypeStruct((HEADS, s, HP), BF16)
    return pl.pallas_call(
        body, name=name, grid=(s // tm,),
        in_specs=[row(QL, 0), row(KVL, 2), row(HP, 3), row(HP, 0), row(HP, 0), row(HP, 0),
                  _acc((1, QL)), _acc((1, KVL)), _acc((1, HP)), _acc((1, HP)),
                  _acc((QL, HEADS * HP)), _acc((KVL, HEADS * HP)), _acc((KVL, HEADS * HP))],
        out_specs=[hspec] * 3, out_shape=[hshape] * 3,
        compiler_params=_cp(("parallel",)),
    )(z, z, z, *tabs, gql, gkv, gq, gk, wq, wk, wv)


def _causal_mask(s, row0):
    row = lax.broadcasted_iota(jnp.int32, s.shape, 0) + row0
    col = lax.broadcasted_iota(jnp.int32, s.shape, 1)
    return jnp.where(col <= row, s, NEG)


def _attn_fwd(q, k, v, name):
    s = q.shape[1]
    tq = _tile(s, ATT_FWD_QUERIES)
    rh = _tile(s, ATT_ROWS)
    kp = _tile(s, ATT_PIECE)
    wide = ATT_WIDE * kp if s % (ATT_WIDE * kp) == 0 else tq
    groups = tq // rh

    def body(q_ref, k_ref, v_ref, o_ref, lse_ref):
        i = pl.program_id(1)

        def blk(off, tk, carry, diagonal):
            width = lambda g, t: max(0, min(kp, (g + 1) * rh - t * kp)) if diagonal else kp
            rows = lambda t: pl.ds(pl.multiple_of(off + t * kp, kp), kp)
            score = lambda g, t: _dot_nt(q_ref[0, g * rh:(g + 1) * rh, :], k_ref[0, rows(t), :][:width(g, t)])
            live = lambda t: [g for g in range(groups) if width(g, t) > 0]
            state = list(carry)
            scs = {(g, 0): score(g, 0) for g in live(0)}
            for t in range(tk // kp):
                if (t + 1) * kp < tk:
                    scs.update({(g, t + 1): score(g, t + 1) for g in live(t + 1)})
                vt = v_ref[0, rows(t), :]
                for g in live(t):
                    m, l, acc = state[g]
                    sc = scs.pop((g, t))
                    if diagonal and (g + 1) * rh <= (t + 1) * kp:
                        sc = _causal_mask(sc, g * rh - t * kp)
                    m_new = jnp.maximum(m, jnp.max(sc, axis=-1, keepdims=True))
                    p = jnp.exp2(sc - m_new)
                    alpha = jnp.exp2(m - m_new)
                    l = alpha * l + jnp.sum(p, axis=-1, keepdims=True)
                    acc = alpha * acc + _dot(p.astype(BF16), vt[:width(g, t)])
                    state[g] = (m_new, l, acc)
            return tuple(state)

        one = (jnp.full((rh, 1), NEG, F32), jnp.zeros((rh, 1), F32), jnp.zeros((rh, VH), F32))
        nwide = (i * tq) // wide
        carry = lax.fori_loop(0, nwide, lambda j, c: blk(j * wide, wide, c, False), (one,) * groups)
        carry = lax.fori_loop(nwide * (wide // tq), i, lambda j, c: blk(j * tq, tq, c, False), carry)
        carry = blk(i * tq, tq, carry, True)
        for g, (m, l, acc) in enumerate(carry):
            o_ref[g * rh:(g + 1) * rh, :] = acc / l
            lse_ref[0, g * rh:(g + 1) * rh, :] = jnp.broadcast_to(m + jnp.log(l) * LOG2E, (rh, LANES))

    return pl.pallas_call(
        body, name=name, grid=(HEADS, s // tq),
        in_specs=[pl.BlockSpec((1, tq, HP), lambda h, i: (h, i, 0)),
                  pl.BlockSpec((1, s, HP), lambda h, i: (h, 0, 0)),
                  pl.BlockSpec((1, s, HP), lambda h, i: (h, 0, 0))],
        out_specs=[pl.BlockSpec((tq, VH), lambda h, i: (i, h)),
                   pl.BlockSpec((1, tq, LANES), lambda h, i: (h, i, 0))],
        out_shape=[jax.ShapeDtypeStruct((s, HEADS * VH), F32), jax.ShapeDtypeStruct((HEADS, s, LANES), F32)],
        compiler_params=_cp(("parallel", "arbitrary"), VMEM_LIMIT),
    )(q, k, v)


def _lane_group(shape, j):
    return (lax.broadcasted_iota(jnp.int32, shape, 1) + j * LANES) // (POOL // 4)


def _pool_win_fwd(z, name):
    s = z.shape[0]
    ch = _tile(s, 512)
    col0 = (IN_P - POOL) // LANES

    def body(p_ref, m_ref):
        j = pl.program_id(0)

        def chunk(r, _):
            off = pl.multiple_of(r * ch, ch)
            cur = p_ref[pl.ds(off, ch), :]
            hoff = pl.multiple_of(jnp.maximum(off - HALO, 0), 8)
            halo = jnp.where(r > 0, p_ref[pl.ds(hoff, HALO), :], 0.0)
            x = jnp.concatenate([halo, cur], axis=0)
            s2 = x + pltpu.roll(x, 1, 0)
            s4 = s2 + pltpu.roll(s2, 2, 0)
            s8 = s4 + pltpu.roll(s4, 4, 0)
            s16 = s8 + pltpu.roll(s8, 8, 0)
            grp = _lane_group((ch, LANES), j)
            sel = jnp.where(grp == 0, s2[HALO:], jnp.where(grp == 1, s4[HALO:], jnp.where(grp == 2, s8[HALO:], s16[HALO:])))
            t1 = (lax.broadcasted_iota(jnp.int32, (ch, LANES), 0) + off + 1).astype(F32)
            win = jnp.where(grp == 0, 2.0, jnp.where(grp == 1, 4.0, jnp.where(grp == 2, 8.0, 16.0)))
            m_ref[pl.ds(off, ch), :] = sel / jnp.minimum(t1, win) - cur
            return 0

        lax.fori_loop(0, s // ch, chunk, 0)

    return pl.pallas_call(
        body, name=name, grid=(POOL // LANES,),
        in_specs=[pl.BlockSpec((s, LANES), lambda j: (0, col0 + j))],
        out_specs=pl.BlockSpec((s, LANES), lambda j: (0, j)),
        out_shape=jax.ShapeDtypeStruct((s, POOL), F32),
        compiler_params=_cp(("parallel",), VMEM_LIMIT),
    )(z)


def _pool_win_bwd(dm, name):
    s = dm.shape[0]
    ch = _tile(s, 512)
    n = s // ch

    def body(dm_ref, dp_ref):
        j = pl.program_id(0)

        def chunk(r, _):
            off = pl.multiple_of(r * ch, ch)
            grp = _lane_group((ch + HALO, LANES), j)
            win = jnp.where(grp == 0, 2.0, jnp.where(grp == 1, 4.0, jnp.where(grp == 2, 8.0, 16.0)))
            cur = dm_ref[pl.ds(off, ch), :]
            hoff = pl.multiple_of(jnp.minimum(off + ch, s - HALO), 8)
            halo = jnp.where(r < n - 1, dm_ref[pl.ds(hoff, HALO), :], 0.0)
            x = jnp.concatenate([cur, halo], axis=0)
            t1 = (lax.broadcasted_iota(jnp.int32, (ch + HALO, LANES), 0) + off + 1).astype(F32)
            e = x / jnp.minimum(t1, win)
            tot = ch + HALO
            r2 = e + pltpu.roll(e, tot - 1, 0)
            r4 = r2 + pltpu.roll(r2, tot - 2, 0)
            r8 = r4 + pltpu.roll(r4, tot - 4, 0)
            r16 = r8 + pltpu.roll(r8, tot - 8, 0)
            g = grp[:ch]
            sel = jnp.where(g == 0, r2[:ch], jnp.where(g == 1, r4[:ch], jnp.where(g == 2, r8[:ch], r16[:ch])))
            dp_ref[pl.ds(off, ch), :] = (sel - cur).astype(BF16)
            return 0

        lax.fori_loop(0, n, chunk, 0)

    return pl.pallas_call(
        body, name=name, grid=(POOL // LANES,),
        in_specs=[pl.BlockSpec((s, LANES), lambda j: (0, j))],
        out_specs=pl.BlockSpec((s, LANES), lambda j: (0, j)),
        out_shape=jax.ShapeDtypeStruct((s, POOL), BF16),
        compiler_params=_cp(("parallel",), VMEM_LIMIT),
    )(dm)


def _head_mask(h):
    lane = lax.broadcasted_iota(jnp.int32, (CHUNK, SGU), 1)
    return (lane // (SGU // HEADS)) == h


def _tril(upper=False):
    row = lax.broadcasted_iota(jnp.int32, (CHUNK, CHUNK), 0)
    col = lax.broadcasted_iota(jnp.int32, (CHUNK, CHUNK), 1)
    return col >= row if upper else col <= row


def _sgu_gate(vn, wsp, bsp):
    out = []
    for cidx in range(vn.shape[0] // CHUNK):
        vc = vn[cidx * CHUNK:(cidx + 1) * CHUNK]
        zc = bsp
        for h in range(HEADS):
            zc = zc + jnp.where(_head_mask(h), _dot(wsp[h], vc), 0.0)
        out.append(zc)
    return jnp.concatenate(out, axis=0)


def _mix_out_fwd(o, z, m, x, wsp, bsp, wbd, psc, gsv, gout, wout, name):
    s = x.shape[0]
    tm = _tile(s, TOKENS)

    def body(o_ref, uv_ref, m_ref, x_ref, wsp_ref, bsp_ref, wbd_ref, psc_ref, gsv_ref, gout_ref, wout_ref,
             x1_ref, mix_ref):
        g = gout_ref[...]
        an = _rms(o_ref[...], HEADS * VH)[0] * g[:, :512]
        uv = uv_ref[...]
        u, v = uv[:, :SGU], uv[:, SGU:]
        vn = (_rms(v, SGU)[0] * gsv_ref[...]).astype(BF16)
        tri = _tril()
        wsp_m = [jnp.where(tri, wsp_ref[h], 0.0).astype(BF16) for h in range(HEADS)]
        gm = u * _sgu_gate(vn, wsp_m, bsp_ref[...])
        gn = _rms(gm, SGU)[0] * g[:, 512:768]
        po = _dot(m_ref[...].astype(BF16), wbd_ref[...]) * psc_ref[...]
        pn = _rms(po, POOL)[0] * g[:, 768:]
        mix = jnp.concatenate([an, gn, pn], axis=1).astype(BF16)
        mix_ref[...] = mix
        x1_ref[...] = x_ref[...] + _dot(mix, wout_ref[...])

    row = lambda w, j: pl.BlockSpec((tm, w), lambda i: (i, j))
    return pl.pallas_call(
        body, name=name, grid=(s // tm,),
        in_specs=[row(512, 0), row(512, 1), row(POOL, 0), row(D, 0),
                  _acc((HEADS, CHUNK, CHUNK)), _acc((CHUNK, SGU)), _acc((POOL, POOL)), _acc((1, POOL)),
                  _acc((1, SGU)), _acc((1, D)), _res((D, D))],
        out_specs=[row(D, 0), row(D, 0)],
        out_shape=[jax.ShapeDtypeStruct((s, D), F32), jax.ShapeDtypeStruct((s, D), BF16)],
        compiler_params=_cp(("parallel",), VMEM_LIMIT),
    )(o, z, m, x, wsp, bsp, wbd, psc, gsv, gout, wout)


def _ffn_fwd(x1, g, wg, wu, wd, tgt, name):
    s = x1.shape[0]
    tm = _tile(s, 256)
    last = tgt is not None

    def body(x_ref, g_ref, wg_ref, wu_ref, wd_ref, *rest):
        t_ref = rest[0] if last else None
        outs = rest[1:] if last else rest
        a_ref, b_ref, h_ref = outs[-3:]
        x = x_ref[...]
        h = (_rms(x, D)[0] * g_ref[...]).astype(BF16)
        h_ref[...] = h
        acc = jnp.zeros((tm, D), F32)
        for k in range(CHIPS):
            a = _dot_nt(h, wg_ref[k])
            b = _dot_nt(h, wu_ref[k])
            a_ref[k] = a
            b_ref[k] = b
            acc = acc + _dot((a * jax.nn.sigmoid(a) * b).astype(BF16), wd_ref[k])
        if not last:
            outs[0][...] = x + acc
            return
        dy_ref, l_ref = outs[:2]
        e = (x + acc) - t_ref[...]
        dy_ref[...] = e * (1.0 / D)
        sq = jnp.sum(e * e, axis=0, keepdims=True)
        part = sq[:, :LANES]
        for c in range(1, D // LANES):
            part = part + sq[:, c * LANES:(c + 1) * LANES]
        _accumulate(l_ref, part, pl.program_id(0) == 0)

    row = lambda w: pl.BlockSpec((tm, w), lambda i: (i, 0))
    hrow = pl.BlockSpec((CHIPS, tm, SH), lambda i: (0, i, 0))
    hshape = jax.ShapeDtypeStruct((CHIPS, s, SH), F32)
    tail_specs = [hrow, hrow, row(D)]
    tail_shapes = [hshape, hshape, jax.ShapeDtypeStruct((s, D), BF16)]
    head_specs = [row(D), _acc((1, LANES))] if last else [row(D)]
    head_shapes = [jax.ShapeDtypeStruct((s, D), F32)] + ([jax.ShapeDtypeStruct((1, LANES), F32)] if last else [])
    res = pl.pallas_call(
        body, name=name, grid=(s // tm,),
        in_specs=[row(D), _acc((1, D)), _res((CHIPS, SH, D)), _res((CHIPS, SH, D)), _res((CHIPS, SH, D))]
        + ([row(D)] if last else []),
        out_specs=head_specs + tail_specs, out_shape=head_shapes + tail_shapes,
        compiler_params=_cp(("arbitrary",), VMEM_LIMIT),
    )(x1, g, wg, wu, wd, *([tgt] if last else []))
    return (tuple(res[:2]) if last else res[0]), res[-3], res[-2], res[-1]


def _wgrad(a, b, name):
    s, k = a.shape
    n = b.shape[1]
    half = lambda v: v if v <= 1408 else v // 2
    kb, nb, tt = half(k), half(n), _tile(s, 2048)

    def body(a_ref, b_ref, o_ref):
        _accumulate(o_ref, _dot_tn(a_ref[...].astype(BF16), b_ref[...].astype(BF16)), pl.program_id(2) == 0)

    return pl.pallas_call(
        body, name=name, grid=(k // kb, n // nb, s // tt),
        in_specs=[pl.BlockSpec((tt, kb), lambda i, j, t: (t, i)), pl.BlockSpec((tt, nb), lambda i, j, t: (t, j))],
        out_specs=pl.BlockSpec((kb, nb), lambda i, j, t: (i, j)),
        out_shape=jax.ShapeDtypeStruct((k, n), F32),
        compiler_params=_cp(("parallel", "parallel", "arbitrary"), VMEM_LIMIT),
    )(a, b)


def _wgrad_in(h, dzm, duv, dp, name):
    s = h.shape[0]
    tt = _tile(s, 2048)

    def body(h_ref, a_ref, b_ref, c_ref, o_ref):
        hv = h_ref[...]
        val = jnp.concatenate([_dot_tn(hv, a_ref[...]), _dot_tn(hv, b_ref[...]), _dot_tn(hv, c_ref[...])], axis=1)
        _accumulate(o_ref, val, pl.program_id(0) == 0)

    row = lambda w: pl.BlockSpec((tt, w), lambda t: (t, 0))
    return pl.pallas_call(
        body, name=name, grid=(s // tt,), in_specs=[row(D), row(512), row(512), row(POOL)], out_specs=_acc((D, IN_P)),
        out_shape=jax.ShapeDtypeStruct((D, IN_P), F32), compiler_params=_cp(("arbitrary",), VMEM_LIMIT),
    )(h, dzm, duv, dp)


def _wgrad_rows(a, b, name):
    s, n = a.shape[1:]
    nn = b.shape[1]
    tt = _tile(s, 4096 if b.dtype == BF16 else 2048)

    def body(a_ref, b_ref, o_ref):
        _accumulate0(o_ref, _dot_tn(a_ref[0].astype(BF16), b_ref[...].astype(BF16)), pl.program_id(1) == 0)

    return pl.pallas_call(
        body, name=name, grid=(CHIPS, s // tt),
        in_specs=[pl.BlockSpec((1, tt, n), lambda c, t: (c, t, 0)), pl.BlockSpec((tt, nn), lambda c, t: (t, 0))],
        out_specs=pl.BlockSpec((1, n, nn), lambda c, t: (c, 0, 0)),
        out_shape=jax.ShapeDtypeStruct((CHIPS, n, nn), F32),
        compiler_params=_cp(("parallel", "arbitrary"), VMEM_LIMIT),
    )(a, b)


def _ffn_bwd(dx2, x1, a, b, g, wg, wu, wd, name):
    s = x1.shape[0]
    tm = _tile(s, 256)

    def body(dx2_ref, x_ref, a_ref, b_ref, g_ref, wg_ref, wu_ref, wd_ref,
             dx1_ref, hid_ref, da_ref, db_ref, dyb_ref, dg_ref):
        dx2 = dx2_ref[...]
        dyb = dx2.astype(BF16)
        dyb_ref[...] = dyb
        dh = jnp.zeros((tm, D), F32)
        ahead = _dot_nt(dyb, wd_ref[0])
        for k in range(CHIPS):
            av, bv = a_ref[k], b_ref[k]
            dhid = ahead
            if k + 1 < CHIPS:
                ahead = _dot_nt(dyb, wd_ref[k + 1])
            sig = jax.nn.sigmoid(av)
            sa = av * sig
            hid_ref[k] = (sa * bv).astype(BF16)
            dbv = (dhid * sa).astype(BF16)
            dav = (dhid * bv * (sig * (1.0 + av * (1.0 - sig)))).astype(BF16)
            db_ref[k] = dbv
            da_ref[k] = dav
            dh = dh + _dot(dav, wg_ref[k]) + _dot(dbv, wu_ref[k])
        xn, r = _rms(x_ref[...], D)
        dxr, dg = _rms_bwd(xn, r, g_ref[...], dh, D)
        dx1_ref[...] = dx2 + dxr
        _accumulate(dg_ref, dg, pl.program_id(0) == 0)

    row = lambda w: pl.BlockSpec((tm, w), lambda i: (i, 0))
    hrow = pl.BlockSpec((CHIPS, tm, SH), lambda i: (0, i, 0))
    hid = jax.ShapeDtypeStruct((CHIPS, s, SH), BF16)
    return pl.pallas_call(
        body, name=name, grid=(s // tm,),
        in_specs=[row(D), row(D), hrow, hrow, _acc((1, D)), _res((CHIPS, SH, D)), _res((CHIPS, SH, D)),
                  _res((CHIPS, SH, D))],
        out_specs=[row(D), hrow, hrow, hrow, row(D), _acc((1, D))],
        out_shape=[jax.ShapeDtypeStruct((s, D), F32), hid, hid, hid, jax.ShapeDtypeStruct((s, D), BF16),
                   jax.ShapeDtypeStruct((1, D), F32)],
        compiler_params=_cp(("arbitrary",), VMEM_LIMIT),
    )(dx2, x1, a, b, g, wg, wu, wd)


def _mix_out_bwd(dx1, o, z, m, wsp, bsp, wbd, psc, gsv, gout, wout, name):
    s = dx1.shape[0]
    tm = _tile(s, TOKENS)

    def body(dx1_ref, o_ref, uv_ref, m_ref, wsp_ref, bsp_ref, wbd_ref, psc_ref, gsv_ref, gout_ref, wout_ref,
             do_ref, dl_ref, duv_ref, dm_ref, dgo_ref, dgsv_ref, dpsc_ref, dwsp_ref, dbsp_ref, dwbd_ref):
        first = pl.program_id(0) == 0
        g = gout_ref[...]
        dmix = _dot_nt(dx1_ref[...].astype(BF16), wout_ref[...])
        o = o_ref[...]
        on, ro = _rms(o, HEADS * VH)
        do, dga = _rms_bwd(on, ro, g[:, :512], dmix[:, :512], HEADS * VH)
        for h in range(HEADS):
            sl = slice(h * VH, (h + 1) * VH)
            do_ref[h] = do[:, sl].astype(BF16)
            dl_ref[h] = jnp.broadcast_to(jnp.sum(do[:, sl] * o[:, sl], axis=-1, keepdims=True), (tm, LANES))
        uv = uv_ref[...]
        u, v = uv[:, :SGU], uv[:, SGU:]
        vx, rv = _rms(v, SGU)
        vn = (vx * gsv_ref[...]).astype(BF16)
        tri = _tril()
        wsp_m = [jnp.where(tri, wsp_ref[h], 0.0).astype(BF16) for h in range(HEADS)]
        zc = _sgu_gate(vn, wsp_m, bsp_ref[...])
        gm = u * zc
        gmn, rg = _rms(gm, SGU)
        dgm, dgg = _rms_bwd(gmn, rg, g[:, 512:768], dmix[:, 512:768], SGU)
        du = dgm * zc
        dzc = dgm * u
        dvn_parts = []
        dbsp = jnp.zeros((CHUNK, SGU), F32)
        dwsp = [jnp.zeros((CHUNK, CHUNK), F32) for _ in range(HEADS)]
        for cidx in range(tm // CHUNK):
            rs = slice(cidx * CHUNK, (cidx + 1) * CHUNK)
            dzc_c = dzc[rs]
            dbsp = dbsp + dzc_c
            dzb = dzc_c.astype(BF16)
            vc = vn[rs]
            dvn_c = jnp.zeros((CHUNK, SGU), F32)
            for h in range(HEADS):
                hm = _head_mask(h)
                dvn_c = dvn_c + jnp.where(hm, _dot_tn(wsp_m[h], dzb), 0.0)
                dwsp[h] = dwsp[h] + _dot_nt(jnp.where(hm, dzc_c, 0.0).astype(BF16), vc)
            dvn_parts.append(dvn_c)
        dvn = jnp.concatenate(dvn_parts, axis=0)
        dv, dgsv = _rms_bwd(vx, rv, gsv_ref[...], dvn, SGU)
        duv_ref[...] = jnp.concatenate([du, dv], axis=1).astype(BF16)
        mb = m_ref[...].astype(BF16)
        pw = _dot(mb, wbd_ref[...])
        po = pw * psc_ref[...]
        pon, rp = _rms(po, POOL)
        dpo, dgp = _rms_bwd(pon, rp, g[:, 768:], dmix[:, 768:], POOL)
        dpw = (dpo * psc_ref[...]).astype(BF16)
        dm_ref[...] = _dot_nt(dpw, wbd_ref[...])
        _accumulate(dgo_ref, jnp.concatenate([dga, dgg, dgp], axis=1), first)
        _accumulate(dgsv_ref, dgsv, first)
        _accumulate(dpsc_ref, jnp.sum(dpo * pw, axis=0, keepdims=True), first)
        _accumulate(dbsp_ref, dbsp, first)
        _accumulate(dwbd_ref, _dot_tn(mb, dpw), first)
        for h in range(HEADS):
            val = jnp.where(tri, dwsp[h], 0.0)

            @pl.when(first)
            def _(val=val, h=h):
                dwsp_ref[h] = val

            @pl.when(jnp.logical_not(first))
            def _(val=val, h=h):
                dwsp_ref[h] += val

    row = lambda w, j: pl.BlockSpec((tm, w), lambda i: (i, j))
    hspec = pl.BlockSpec((HEADS, tm, HP), lambda i: (0, i, 0))
    return pl.pallas_call(
        body, name=name, grid=(s // tm,),
        in_specs=[row(D, 0), row(512, 0), row(512, 1), row(POOL, 0),
                  _acc((HEADS, CHUNK, CHUNK)), _acc((CHUNK, SGU)),
                  _acc((POOL, POOL)), _acc((1, POOL)), _acc((1, SGU)), _acc((1, D)), _res((D, D))],
        out_specs=[hspec, hspec, row(512, 0), row(POOL, 0), _acc((1, D)), _acc((1, SGU)), _acc((1, POOL)),
                   _acc((HEADS, CHUNK, CHUNK)), _acc((CHUNK, SGU)), _acc((POOL, POOL))],
        out_shape=[jax.ShapeDtypeStruct((HEADS, s, HP), BF16), jax.ShapeDtypeStruct((HEADS, s, LANES), F32),
                   jax.ShapeDtypeStruct((s, 512), BF16), jax.ShapeDtypeStruct((s, POOL), F32),
                   jax.ShapeDtypeStruct((1, D), F32), jax.ShapeDtypeStruct((1, SGU), F32),
                   jax.ShapeDtypeStruct((1, POOL), F32), jax.ShapeDtypeStruct((HEADS, CHUNK, CHUNK), F32),
                   jax.ShapeDtypeStruct((CHUNK, SGU), F32), jax.ShapeDtypeStruct((POOL, POOL), F32)],
        compiler_params=_cp(("arbitrary",), VMEM_LIMIT),
    )(dx1, o, z, m, wsp, bsp, wbd, psc, gsv, gout, wout)


def _attn_bwd(q, k, v, do, lse, delta, after, name):
    s = q.shape[1]
    rh = _tile(s, ATT_ROWS)
    tk = _tile(s, ATT_KEYS)
    nk = s // tk
    wide = ATT_QUERIES if s % ATT_QUERIES == 0 else tk
    pieces = tk // rh

    def body(q_ref, k_ref, v_ref, do_ref, lse_ref, dl_ref, after_ref, dq_ref, dk_ref, dv_ref):
        del after_ref
        j = pl.program_id(1)

        @pl.when(j == 0)
        def _():
            dq_ref[...] = jnp.zeros_like(dq_ref)

        kj, vj = k_ref[0], v_ref[0]

        def blk(start, rows, dks, dvs, diagonal):
            dks, dvs = list(dks), list(dvs)
            offs = [pl.multiple_of(start + g * rh, rh) for g in range(rows // rh)]
            keys = [(g + 1) * rh if diagonal else tk for g in range(rows // rh)]
            qs = [q_ref[0, pl.ds(off, rh), :] for off in offs]
            dos = [do_ref[0, pl.ds(off, rh), :] for off in offs]
            scs = [_dot_nt(qi, kj[:n]) for qi, n in zip(qs, keys)]
            dps = [_dot_nt(doi, vj[:n]) for doi, n in zip(dos, keys)]
            for g, off in enumerate(offs):
                lse_i = lse_ref[0, pl.ds(off, rh), :][:, :1]
                dl_i = dl_ref[0, pl.ds(off, rh), :][:, :1]
                sc = _causal_mask(scs[g], g * rh) if diagonal else scs[g]
                p = jnp.exp2(sc - lse_i)
                ds = (p * (dps[g] - dl_i)).astype(BF16)
                cv = _dot_tn(p.astype(BF16), dos[g])
                ck = _dot_tn(ds, qs[g])
                for t in range(keys[g] // rh):
                    dvs[t] = dvs[t] + cv[t * rh:(t + 1) * rh]
                    dks[t] = dks[t] + ck[t * rh:(t + 1) * rh]
                dq_ref[0, pl.ds(off, rh), :] += _dot(ds, kj[:keys[g]]) * SCALE
            return tuple(dks), tuple(dvs)

        per = wide // tk
        zero = (jnp.zeros((rh, HP), F32),) * pieces
        acc = blk(j * tk, tk, zero, zero, True)
        first_wide = (j + per) // per
        acc = lax.fori_loop(j + 1, jnp.minimum(first_wide * per, nk), lambda i, c: blk(i * tk, tk, *c, False), acc)
        dks, dvs = lax.fori_loop(first_wide, nk // per, lambda i, c: blk(i * wide, wide, *c, False), acc)
        dk_ref[0] = jnp.concatenate(dks, axis=0) * (SCALE / EXP2_C)
        dv_ref[0] = jnp.concatenate(dvs, axis=0)

    full = lambda: pl.BlockSpec((1, s, HP), lambda h, j: (h, 0, 0))
    blk_spec = lambda: pl.BlockSpec((1, tk, HP), lambda h, j: (h, j, 0))
    out = jax.ShapeDtypeStruct((HEADS, s, HP), F32)
    return pl.pallas_call(
        body, name=name, grid=(HEADS, s // tk),
        in_specs=[full(), blk_spec(), blk_spec(), full(), full(), full(), ANY],
        out_specs=[full(), blk_spec(), blk_spec()], out_shape=[out] * 3,
        compiler_params=_cp(("parallel", "arbitrary"), VMEM_LIMIT),
    )(q, k, v, do, lse, delta, after)


def _mla_prep_bwd(dq, dk, dv, z, tabs, gql, gkv, gq, gk, wq, wk, wv, name):
    s = z.shape[0]
    tm = _tile(s, TOKENS)

    def body(dq_ref, dk_ref, dv_ref, ql_ref, kv_ref, kr_ref, c_ref, sa_ref, sb_ref, gql_ref, gkv_ref, gq_ref, gk_ref,
             wq_ref, wk_ref, wv_ref,
             dz_ref, dwq_ref, dwk_ref, dwv_ref, dgql_ref, dgkv_ref, dgq_ref, dgk_ref, dqr_ref, dkr_ref, dvr_ref):
        first = pl.program_id(0) == 0
        qx, rq = _rms(ql_ref[...], QL)
        qn = (qx * gql_ref[...]).astype(BF16)
        kx, rk = _rms(kv_ref[...], KVL)
        kvn = (kx * gkv_ref[...]).astype(BF16)
        qraw = _dot(qn, wq_ref[...])
        kraw = _dot(kvn, wk_ref[...])
        kr = kr_ref[...]
        c, sa, sb = c_ref[...], sa_ref[...], sb_ref[...]
        lane = lax.broadcasted_iota(jnp.int32, (tm, HP), 1)
        rope_lanes = (lane >= NOPE) & (lane < QK)
        ones = _head_ones()
        heads = lambda f: jnp.concatenate([f(h) for h in range(HEADS)], axis=1)
        fold = lambda v: sum(v[:, h * HP:(h + 1) * HP] for h in range(HEADS))

        def head_rms_bwd(x, g_ref, d_ref):
            xn, r = _head_rms(x, ones)
            dy = heads(lambda h: _rope_t(d_ref[h], c, sa, sb))
            dn = dy * jnp.concatenate([g_ref[...]] * HEADS, axis=1)
            dx = r * (dn - xn * (_head_sum(dn * xn, ones) * (1.0 / QK)))
            return dx, fold(jnp.sum(dy * xn, axis=0, keepdims=True))

        dxq, dgq = head_rms_bwd(qraw, gq_ref, dq_ref)
        dxk, dgk = head_rms_bwd(kraw + jnp.concatenate([kr] * HEADS, axis=1), gk_ref, dk_ref)
        dqr_ref[...] = dxq.astype(BF16)
        dkr_ref[...] = dxk.astype(BF16)
        dvr_ref[...] = heads(lambda h: dv_ref[h]).astype(BF16)
        dkrope = jnp.where(rope_lanes, fold(dxk), 0.0)
        dqn = _dot_nt(dqr_ref[...], wq_ref[...])
        dql, dgql = _rms_bwd(qx, rq, gql_ref[...], dqn, QL)
        dkvn = _dot_nt(dkr_ref[...], wk_ref[...]) + _dot_nt(dvr_ref[...], wv_ref[...])
        dkv, dgkv = _rms_bwd(kx, rk, gkv_ref[...], dkvn, KVL)
        dz_ref[...] = jnp.concatenate([dql, dkv, dkrope], axis=1).astype(BF16)
        _accumulate(dwq_ref, _dot_tn(qn, dqr_ref[...]), first)
        _accumulate(dwk_ref, _dot_tn(kvn, dkr_ref[...]), first)
        _accumulate(dwv_ref, _dot_tn(kvn, dvr_ref[...]), first)
        _accumulate(dgql_ref, dgql, first)
        _accumulate(dgkv_ref, dgkv, first)
        _accumulate(dgq_ref, dgq, first)
        _accumulate(dgk_ref, dgk, first)

    row = lambda w, j: pl.BlockSpec((tm, w), lambda i: (i, j))
    hspec = pl.BlockSpec((HEADS, tm, HP), lambda i: (0, i, 0))
    acc = lambda r, c: (_acc((r, c)), jax.ShapeDtypeStruct((r, c), F32))
    outs = [(row(512, 0), jax.ShapeDtypeStruct((s, 512), BF16)), acc(QL, HEADS * HP), acc(KVL, HEADS * HP),
            acc(KVL, HEADS * HP), acc(1, QL), acc(1, KVL), acc(1, HP), acc(1, HP)]
    return pl.pallas_call(
        body, name=name, grid=(s // tm,),
        in_specs=[hspec, hspec, hspec, row(QL, 0), row(KVL, 2), row(HP, 3), row(HP, 0), row(HP, 0), row(HP, 0),
                  _acc((1, QL)), _acc((1, KVL)), _acc((1, HP)), _acc((1, HP)),
                  _acc((QL, HEADS * HP)), _acc((KVL, HEADS * HP)), _acc((KVL, HEADS * HP))],
        out_specs=[o[0] for o in outs], out_shape=[o[1] for o in outs],
        scratch_shapes=[pltpu.VMEM((tm, HEADS * HP), BF16)] * 3,
        compiler_params=_cp(("arbitrary",), VMEM_LIMIT),
    )(dq, dk, dv, z, z, z, *tabs, gql, gkv, gq, gk, wq, wk, wv)


def _in_proj_bwd(dzm, duv, dp, x, dx1, g, win, name):
    s = x.shape[0]
    tm = _tile(s, TOKENS // 2)

    def body(dzm_ref, duv_ref, dp_ref, x_ref, dx1_ref, g_ref, w_ref, dx_ref, dg_ref):
        groups = [slice(r0, r0 + tm // 2) for r0 in (0, tm // 2)]
        dhs = [_dot_nt(dzm_ref[rs, :], w_ref[:, 0:512]) + _dot_nt(duv_ref[rs, :], w_ref[:, 512:1024])
               + _dot_nt(dp_ref[rs, :], w_ref[:, 1024:IN_P]) for rs in groups]
        dg = jnp.zeros((1, D), F32)
        for rs, dh in zip(groups, dhs):
            xn, r = _rms(x_ref[rs, :], D)
            dxr, dgr = _rms_bwd(xn, r, g_ref[...], dh, D)
            dx_ref[rs, :] = dx1_ref[rs, :] + dxr
            dg = dg + dgr
        _accumulate(dg_ref, dg, pl.program_id(0) == 0)

    row = lambda w: pl.BlockSpec((tm, w), lambda i: (i, 0))
    return pl.pallas_call(
        body, name=name, grid=(s // tm,),
        in_specs=[row(512), row(512), row(POOL), row(D), row(D), _acc((1, D)), _res((D, IN_P))],
        out_specs=[row(D), _acc((1, D))],
        out_shape=[jax.ShapeDtypeStruct((s, D), F32), jax.ShapeDtypeStruct((1, D), F32)],
        compiler_params=_cp(("arbitrary",), VMEM_LIMIT),
    )(dzm, duv, dp, x, dx1, g, win)


def _adamw(w, g0, g1, m, v, name):
    _, r, c = w.shape
    tr = _row_tile(r, 512)
    c1 = 1.0 - B1 ** STEP
    c2 = 1.0 - B2 ** STEP

    def body(w_ref, g0_ref, g1_ref, m_ref, v_ref, g_ref, d_ref, nm_ref, nv_ref):
        gv = jnp.where(pl.program_id(0) == 0, g0_ref[...], g1_ref[...])
        g_ref[0] = gv
        nm = B1 * m_ref[0] + (1.0 - B1) * gv
        nv = B2 * v_ref[0] + (1.0 - B2) * (gv * gv)
        nm_ref[0] = nm
        nv_ref[0] = nv
        d_ref[0] = -LR * ((nm / c1) / (jnp.sqrt(nv / c2) + ADAM_EPS) + WD * w_ref[0])

    spec = pl.BlockSpec((1, tr, c), lambda l, i: (l, i, 0))
    out = jax.ShapeDtypeStruct((DEPTH, r, c), F32)
    return pl.pallas_call(
        body, name=name, grid=(DEPTH, r // tr),
        in_specs=[spec, pl.BlockSpec((tr, c), lambda l, i: (i * (1 - l), 0)), pl.BlockSpec((tr, c), lambda l, i: (i * l, 0)),
                  spec, spec],
        out_specs=[spec] * 4, out_shape=[out] * 4, compiler_params=_cp(("parallel", "parallel")),
    )(w, g0, g1, m, v)


def _adamw_vectors(ws, g0s, g1s, ms, vs, name):
    k = len(ws)
    c1 = 1.0 - B1 ** STEP
    c2 = 1.0 - B2 ** STEP

    def body(*refs):
        w_refs, g0_refs, g1_refs, m_refs, v_refs, g_out, d_out, m_out, v_out = (refs[i * k:(i + 1) * k] for i in range(9))
        for i in range(k):
            for l, g_ref in enumerate((g0_refs[i], g1_refs[i])):
                row = slice(l, l + 1)
                gv = g_ref[...]
                g_out[i][row, :] = gv
                nm = B1 * m_refs[i][row, :] + (1.0 - B1) * gv
                nv = B2 * v_refs[i][row, :] + (1.0 - B2) * (gv * gv)
                m_out[i][row, :] = nm
                v_out[i][row, :] = nv
                d_out[i][row, :] = -LR * ((nm / c1) / (jnp.sqrt(nv / c2) + ADAM_EPS) + WD * w_refs[i][row, :])

    out = [jax.ShapeDtypeStruct(w.shape, F32) for w in ws]
    return pl.pallas_call(body, name=name, out_shape=out * 4)(*ws, *g0s, *g1s, *ms, *vs)


ANY = pl.BlockSpec(memory_space=pl.ANY)


def _place():
    x, y, c = lax.axis_index("x"), lax.axis_index("y"), lax.axis_index("c")
    chips = [(1 - x, y), (x, 1 - y), (1 - x, 1 - y)]
    return x, y, c, chips


def _half_rows(ref, lead, hh, half, align):
    rows = pl.ds(pl.multiple_of(hh * half, align), half)
    return ref.at[rows, :] if lead is None else ref.at[lead, rows, :]


def _row_align(dtype):
    return 16 if dtype == BF16 else 8


def _sems(n):
    return [pltpu.SemaphoreType.DMA((n,)), pltpu.SemaphoreType.DMA((n,)), pltpu.SemaphoreType.DMA((n,))]


def _comm_call(body, ins, out_shapes, nsems, name):
    return pl.pallas_call(
        body, name=name, in_specs=[ANY] * len(ins), out_specs=[ANY] * len(out_shapes), out_shape=out_shapes,
        scratch_shapes=_sems(nsems), compiler_params=pltpu.CompilerParams(has_side_effects=True),
    )(*ins)


def _all_gather_chips(shards, name, meanwhile=None):
    n = len(shards)
    halves = [a.shape[0] // 2 for a in shards]
    aligns = [_row_align(a.dtype) for a in shards]
    assert all(h % al == 0 for h, al in zip(halves, aligns))
    work, work_ins, work_outs = meanwhile or (None, [], [])
    k, m = len(work_ins), len(work_outs)

    def body(*refs):
        ins, outs, (send_sems, recv_sems, _) = refs[:n], refs[n + k:2 * n + k], refs[2 * n + k + m:]
        x, y, c, chips = _place()
        me = 2 * x + y
        sibling = (x, y, 1 - c)

        def copy(sem, src, dst, to):
            return pltpu.make_async_remote_copy(src_ref=src, dst_ref=dst, send_sem=send_sems.at[sem],
                                                recv_sem=recv_sems.at[sem], device_id=to, device_id_type=MESH)

        first, passed = [], []
        for a in range(n):
            my_half = _half_rows(ins[a], None, c, halves[a], aligns[a])
            for j, (cx, cy) in enumerate(chips):
                cp = copy(6 * a + j, my_half, _half_rows(outs[a], me, c, halves[a], aligns[a]), (cx, cy, c))
                cp.start()
                first.append(cp)
        if work is not None:
            work(refs[n:n + k], refs[2 * n + k:2 * n + k + m], 0)
        for a in range(n):
            for j, (cx, cy) in enumerate(chips):
                landed = _half_rows(outs[a], 2 * cx + cy, c, halves[a], aligns[a])
                copy(6 * a + j, landed, landed, (cx, cy, c)).wait_recv()
                fwd = copy(6 * a + 3 + j, landed, landed, sibling)
                fwd.start()
                passed.append(fwd)
        if work is not None:
            work(refs[n:n + k], refs[2 * n + k:2 * n + k + m], 1)
        for a in range(n):
            for j, (cx, cy) in enumerate(chips):
                other = _half_rows(outs[a], 2 * cx + cy, 1 - c, halves[a], aligns[a])
                copy(6 * a + 3 + j, other, other, sibling).wait_recv()
        for cp in first + passed:
            cp.wait_send()

    land_shapes = [jax.ShapeDtypeStruct((CHIPS,) + a.shape, a.dtype) for a in shards]
    if work is None:
        return _with_own(_comm_call(body, shards, land_shapes, 6 * n, name), shards)
    vmem = pl.BlockSpec(memory_space=pltpu.VMEM)
    res = pl.pallas_call(
        body, name=name, in_specs=[ANY] * n + [vmem] * k, out_specs=[ANY] * n + [vmem] * m,
        out_shape=land_shapes + list(work_outs), scratch_shapes=_sems(6 * n),
        compiler_params=pltpu.CompilerParams(has_side_effects=True, vmem_limit_bytes=VMEM_LIMIT),
    )(*shards, *work_ins)
    return _with_own(res[:n], shards), res[n:]


def _with_own(lands, shards):
    me = 2 * lax.axis_index("x") + lax.axis_index("y")
    return [lax.dynamic_update_slice(g, a[None], (me, 0, 0)) for g, a in zip(lands, shards)]


def _pair_join(arrs, name):
    n = len(arrs)
    halves = [a.shape[0] // 2 for a in arrs]

    def body(*refs):
        outs, (send_sems, recv_sems, _) = refs[n:2 * n], refs[2 * n:]
        x, y, c, _ = _place()
        cps = []
        for a in range(n):
            mine = _half_rows(outs[a], None, c, halves[a], 8)
            cp = pltpu.make_async_remote_copy(src_ref=mine, dst_ref=mine, send_sem=send_sems.at[a], recv_sem=recv_sems.at[a],
                                              device_id=(x, y, 1 - c), device_id_type=MESH)
            cp.start()
            cps.append(cp)
        for cp in cps:
            cp.wait()

    return pl.pallas_call(
        body, name=name, in_specs=[ANY] * n, out_specs=[ANY] * n,
        out_shape=[jax.ShapeDtypeStruct(a.shape, a.dtype) for a in arrs],
        input_output_aliases={i: i for i in range(n)}, scratch_shapes=_sems(n),
        compiler_params=pltpu.CompilerParams(has_side_effects=True),
    )(*arrs)


HBM = pl.BlockSpec(memory_space=pltpu.HBM)
SEM = pl.BlockSpec(memory_space=pltpu.SEMAPHORE)
DATAFLOW = pltpu.SideEffectType.DATAFLOW_SIDE_EFFECTING


def _remote_copies(pairs, ins, lands, send_sems, recv_sems):
    return [pltpu.make_async_remote_copy(src_ref=src, dst_ref=dst, send_sem=send_sems.at[i], recv_sem=recv_sems.at[i],
                                         device_id=to, device_id_type=MESH)
            for i, (src, dst, to) in enumerate(pairs(ins, lands))]


def _split_start(srcs, land_shapes, ncopies, pairs, name, after):
    n, m = len(srcs), len(land_shapes)

    def body(*refs):
        ins, lands = refs[:n], refs[n:n + m]
        send_sems, recv_sems, token = refs[n + m + 1], refs[n + m + 2], refs[-1]
        for cp in _remote_copies(pairs, ins, lands, send_sems, recv_sems):
            cp.start()
        token[...] = jnp.zeros_like(token)

    hbm = lambda a: pltpu.with_memory_space_constraint(a, pltpu.HBM)
    lands = [hbm(lax.empty(s.shape, s.dtype)) for s in land_shapes]
    thru = [pltpu.HBM(a.shape, a.dtype) for a in list(srcs) + lands]
    out = pl.pallas_call(
        body, name=name,
        out_shape=(pltpu.SemaphoreType.DMA((ncopies,)), pltpu.SemaphoreType.DMA((ncopies,)), *thru,
                   jax.ShapeDtypeStruct((8, LANES), F32)),
        in_specs=[HBM] * (n + m) + [ANY], out_specs=(SEM, SEM, *[HBM] * (n + m), pl.BlockSpec(memory_space=pltpu.VMEM)),
        input_output_aliases={i: 2 + i for i in range(n + m)},
        compiler_params=pltpu.CompilerParams(has_side_effects=DATAFLOW),
    )(*[hbm(a) for a in srcs], *lands, after)
    return out[0], out[1], list(out[2:2 + n]), list(out[2 + n:2 + n + m]), out[-1]


def _split_wait(send_sems, recv_sems, srcs, lands, after, pairs, name):
    n, m = len(srcs), len(lands)

    def body(*refs):
        ins, lands_ = refs[:n], refs[n:n + m]
        for cp in _remote_copies(pairs, ins, lands_, refs[n + m], refs[n + m + 1]):
            cp.wait_send()
            cp.wait_recv()

    out = pl.pallas_call(
        body, name=name, out_shape=tuple(pltpu.HBM(a.shape, a.dtype) for a in list(srcs) + list(lands)),
        in_specs=[HBM] * (n + m) + [SEM, SEM, ANY], out_specs=tuple([HBM] * (n + m)),
        input_output_aliases={i: i for i in range(n + m)},
        compiler_params=pltpu.CompilerParams(has_side_effects=DATAFLOW),
    )(*srcs, *lands, send_sems, recv_sems, after)
    return list(out[:n]), list(out[n:])


def _gather_pairs(halves, aligns):
    def pairs(ins, lands):
        x, y, c, chips = _place()
        me = 2 * x + y
        return [(_half_rows(ins[a], None, c, halves[a], aligns[a]), _half_rows(lands[a], me, c, halves[a], aligns[a]),
                 (cx, cy, c)) for a in range(len(ins)) for cx, cy in chips]
    return pairs


PEERS = 7


def _scatter_pairs(ins, lands):
    x, y, c, chips = _place()
    to = [(cx, cy, c) for cx, cy in chips] + [(cx, cy, 1 - c) for cx, cy in chips] + [(x, y, 1 - c)]
    out = []
    for a in range(len(ins)):
        half = ins[a].shape[1] // 2
        for i, (tx, ty, tc) in enumerate(to):
            out.append((_half_rows(ins[a], 2 * tx + ty, tc, half, 8), lands[a].at[i], (tx, ty, tc)))
    return out


def _gather_finish(shards, lands, name):
    n = len(shards)
    halves = [a.shape[0] // 2 for a in shards]
    aligns = [_row_align(a.dtype) for a in shards]

    def body(*refs):
        outs, (send_sems, recv_sems, _) = refs[n:2 * n], refs[2 * n:]
        x, y, c, chips = _place()
        passed = []
        for a in range(n):
            for j, (cx, cy) in enumerate(chips):
                landed = _half_rows(outs[a], 2 * cx + cy, c, halves[a], aligns[a])
                cp = pltpu.make_async_remote_copy(src_ref=landed, dst_ref=landed, send_sem=send_sems.at[3 * a + j],
                                                  recv_sem=recv_sems.at[3 * a + j], device_id=(x, y, 1 - c),
                                                  device_id_type=MESH)
                cp.start()
                passed.append(cp)
        for a in range(n):
            for j, (cx, cy) in enumerate(chips):
                other = _half_rows(outs[a], 2 * cx + cy, 1 - c, halves[a], aligns[a])
                pltpu.make_async_remote_copy(src_ref=other, dst_ref=other, send_sem=send_sems.at[3 * a + j],
                                             recv_sem=recv_sems.at[3 * a + j], device_id=(x, y, 1 - c),
                                             device_id_type=MESH).wait_recv()
        for cp in passed:
            cp.wait_send()

    lands = pl.pallas_call(
        body, name=name, in_specs=[ANY] * n, out_specs=[ANY] * n,
        out_shape=[jax.ShapeDtypeStruct(a.shape, a.dtype) for a in lands],
        input_output_aliases={i: i for i in range(n)}, scratch_shapes=_sems(3 * n),
        compiler_params=pltpu.CompilerParams(has_side_effects=True),
    )(*lands)
    return _with_own(lands, shards)


def _sum_own_and_landed(owns, landeds, where, name):
    n = len(owns)
    _, half, cols = landeds[0].shape
    tr = _row_tile(half, 128)
    nt = half // tr

    grid_spec = pltpu.PrefetchScalarGridSpec(
        num_scalar_prefetch=1, grid=(nt,),
        in_specs=[pl.BlockSpec((1, tr, cols), lambda r, w: (w[0], w[1] * nt + r, 0))] * n
        + [pl.BlockSpec((PEERS, tr, cols), lambda r, w: (0, r, 0))] * n,
        out_specs=[pl.BlockSpec((tr, cols), lambda r, w: (w[1] * nt + r, 0))] * n)

    def body(w_ref, *refs):
        for p_ref, q_ref, o_ref in zip(refs[:n], refs[n:2 * n], refs[2 * n:]):
            acc = p_ref[0]
            for i in range(PEERS):
                acc = acc + q_ref[i]
            o_ref[...] = acc

    return pl.pallas_call(
        body, name=name, grid_spec=grid_spec, out_shape=[jax.ShapeDtypeStruct((2 * half, cols), owns[0].dtype)] * n,
        compiler_params=_cp(("parallel",), VMEM_LIMIT),
    )(where, *owns, *landeds)


BIG = [("w_in", (D, IN_W), 1), ("w_q_up", (QL, HEADS * QK), 1), ("w_kv_up", (KVL, HEADS * (NOPE + VH)), 1),
       ("w_out", (D, D), 0), ("w_gate", (D, HID), 1), ("w_up", (D, HID), 1), ("w_down", (HID, D), 0)]
SMALL = [("g_mix_norm", (D,)), ("g_q_lat", (QL,)), ("g_kv_lat", (KVL,)), ("g_q_head", (QK,)), ("g_k_head", (QK,)),
         ("g_sgu_v", (SGU,)), ("w_spatial", (HEADS, CHUNK, CHUNK)), ("b_spatial", (HEADS, CHUNK)),
         ("w_pool", (4, 64, 64)), ("pool_scale", (POOL,)), ("g_out_mla", (512,)), ("g_out_sgu", (SGU,)),
         ("g_out_pool", (POOL,)), ("g_ffn_norm", (D,))]
ORDER = ["g_mix_norm", "w_in", "g_q_lat", "w_q_up", "g_kv_lat", "w_kv_up", "g_q_head", "g_k_head", "g_sgu_v",
         "w_spatial", "b_spatial", "w_pool", "pool_scale", "g_out_mla", "g_out_sgu", "g_out_pool", "w_out",
         "g_ffn_norm", "w_gate", "w_up", "w_down"]
EARLY_BIG = ["w_in", "w_q_up", "w_kv_up"]
FFN_BIG = ["w_gate", "w_up", "w_down"]
LATE_BIG = ["w_out"] + FFN_BIG
DEPTH = 2
COLS = 1024
SMALL_N = sum(math.prod(s) for _, s in SMALL) * DEPTH
assert SMALL_N % CHIPS == 0
SMALL_ROWS = -(-(SMALL_N // CHIPS + 1) // (16 * COLS)) * 16


def _unsplit_cols(g):
    return g.transpose(1, 0, 2).reshape(g.shape[1], CHIPS * g.shape[2])


def _split_cols(full):
    r, c = full.shape
    return full.reshape(r, CHIPS, c // CHIPS).transpose(1, 0, 2)


def _kernel_weights(g):
    win = _unsplit_cols(g["w_in"])
    zeros = lambda r, c: jnp.zeros((r, c), BF16)
    o2, o3, o4 = QL + KVL, QL + KVL + ROPE, QL + KVL + ROPE + 2 * SGU
    win_p = jnp.concatenate([win[:, :o2], zeros(D, NOPE), win[:, o2:o3], zeros(D, HP - QK), win[:, o3:o4], win[:, o4:]], axis=1)
    wq = _unsplit_cols(g["w_q_up"]).reshape(QL, HEADS, QK)
    wq_p = jnp.pad(wq, ((0, 0), (0, 0), (0, HP - QK))).reshape(QL, HEADS * HP)
    wkv = _unsplit_cols(g["w_kv_up"]).reshape(KVL, HEADS, NOPE + VH)
    wk_p = jnp.pad(wkv[:, :, :NOPE], ((0, 0), (0, 0), (0, HP - NOPE))).reshape(KVL, HEADS * HP)
    wv_p = wkv[:, :, NOPE:].reshape(KVL, HEADS * VH)
    return dict(win=win_p, wq=wq_p, wk=wk_p, wv=wv_p)


def _small_operands(p, l):
    row = lambda v: v.reshape(1, -1)
    pad = lambda v: jnp.pad(v, (0, HP - QK)).reshape(1, HP)
    wpool = p["w_pool"][l]
    wbd = jnp.zeros((POOL, POOL), F32)
    for g in range(4):
        wbd = lax.dynamic_update_slice(wbd, wpool[g], (g * 64, g * 64))
    return dict(
        g_mix=row(p["g_mix_norm"][l]), gql=row(p["g_q_lat"][l]), gkv=row(p["g_kv_lat"][l]),
        gq=pad(p["g_q_head"][l]), gk=pad(p["g_k_head"][l]), gsv=row(p["g_sgu_v"][l]),
        wsp=p["w_spatial"][l], bsp=jnp.repeat(p["b_spatial"][l].T, SGU // HEADS, axis=1),
        wbd=wbd.astype(BF16), psc=row(p["pool_scale"][l]),
        gout=jnp.concatenate([p["g_out_mla"][l], p["g_out_sgu"][l], p["g_out_pool"][l]]).reshape(1, D),
        g_ffn=row(p["g_ffn_norm"][l]))


def _big_grads(g):
    dwin = g["win"]
    o2 = QL + KVL
    gin = jnp.concatenate([dwin[:, :o2], dwin[:, o2 + NOPE:o2 + NOPE + ROPE], dwin[:, 512:]], axis=1)
    gq = g["wq"].reshape(QL, HEADS, HP)[:, :, :QK].reshape(QL, HEADS * QK)
    gk = g["wk"].reshape(KVL, HEADS, HP)[:, :, :NOPE]
    gv = g["wv"].reshape(KVL, HEADS, VH)
    gkv = jnp.concatenate([gk, gv], axis=2).reshape(KVL, HEADS * (NOPE + VH))
    return {"w_in": _split_cols(gin), "w_q_up": _split_cols(gq), "w_kv_up": _split_cols(gkv),
            "w_out": g["wout"].reshape(CHIPS, D // CHIPS, D), "w_gate": g["wg"], "w_up": g["wu"], "w_down": g["wd"]}


TRANSPOSED = ("w_gate", "w_up")


def _small_grads(g):
    go = g["gout"].reshape(-1)
    return {"g_mix_norm": g["g_mix"].reshape(-1), "g_q_lat": g["gql"].reshape(-1), "g_kv_lat": g["gkv"].reshape(-1),
            "g_q_head": g["gq"].reshape(-1)[:QK], "g_k_head": g["gk"].reshape(-1)[:QK], "g_sgu_v": g["gsv"].reshape(-1),
            "w_spatial": g["wsp"], "b_spatial": g["bsp"].reshape(CHUNK, HEADS, SGU // HEADS).sum(-1).T,
            "w_pool": jnp.stack([g["wbd"][i * 64:(i + 1) * 64, i * 64:(i + 1) * 64] for i in range(4)]),
            "pool_scale": g["psc"].reshape(-1), "g_out_mla": go[:512], "g_out_sgu": go[512:768],
            "g_out_pool": go[768:], "g_ffn_norm": g["g_ffn"].reshape(-1)}


def _pack_small_grads(small, loss):
    sm = jnp.concatenate([small[l][n].reshape(-1) for l in range(DEPTH) for n, _ in SMALL]).reshape(CHIPS, SMALL_N // CHIPS)
    sm = jnp.pad(sm, ((0, 0), (0, SMALL_ROWS * COLS - SMALL_N // CHIPS)))
    return sm.at[0, SMALL_N // CHIPS].set(loss).reshape(CHIPS, SMALL_ROWS, COLS)


def _unpack_small_grads(gathered):
    rows = gathered.reshape(CHIPS, SMALL_ROWS * COLS)
    loss = rows[0, SMALL_N // CHIPS]
    flat = rows[:, :SMALL_N // CHIPS].reshape(-1)
    out, off = [], 0
    for _ in range(DEPTH):
        layer = {}
        for n, shape in SMALL:
            k = math.prod(shape)
            layer[n] = flat[off:off + k].reshape(shape)
            off += k
        out.append(layer)
    return out, loss


def _layer_fwd(x, tabs, kw, late_weights, sp, l, tgt):
    t = f"_l{l}"
    z, hb = _in_proj_fwd(x, sp["g_mix"], kw["win"], "in_proj_fwd" + t)
    q, k, v = _mla_prep_fwd(z, tabs, sp["gql"], sp["gkv"], sp["gq"], sp["gk"], kw["wq"], kw["wk"], kw["wv"],
                            "mla_prep_fwd" + t)
    o, lse = _attn_fwd(q, k, v, "attn_fwd" + t)
    m = _pool_win_fwd(z, "pool_win_fwd" + t)
    wout, wg, wu, wd = late_weights(o)
    wout = wout.reshape(D, D)
    x1, mix = _mix_out_fwd(o, z, m, x, sp["wsp"], sp["bsp"], sp["wbd"], sp["psc"], sp["gsv"], sp["gout"], wout,
                           "mix_out_fwd" + t)
    x2, a, b, h2 = _ffn_fwd(x1, sp["g_ffn"], wg, wu, wd, tgt, "ffn_fwd" + t)
    saved = dict(x=x, z=z, hb=hb, q=q, k=k, v=v, o=o, lse=lse, m=m, x1=x1, mix=mix, a=a, b=b, h2=h2, wg=wg, wu=wu, wd=wd,
                 wout=wout)
    return x2, saved


def _layer_bwd(dx2, sv, tabs, kw, sp, l, ffn_hook, out_hook):
    t = f"_l{l}"
    g = {}
    dx1, hid, da, db, dyb, g["g_ffn"] = _ffn_bwd(dx2, sv["x1"], sv["a"], sv["b"], sp["g_ffn"], sv["wg"], sv["wu"],
                                                 sv["wd"], "ffn_bwd" + t)
    g["wd"] = _wgrad_rows(hid, dyb, "wgrad_down" + t)
    g["wg"] = _wgrad_rows(da, sv["h2"], "wgrad_gate" + t)
    g["wu"] = _wgrad_rows(db, sv["h2"], "wgrad_up" + t)
    gout = sp["gout"] + ffn_hook(g)
    do, delta, duv, dm, g["gout"], g["gsv"], g["psc"], g["wsp"], g["bsp"], g["wbd"] = _mix_out_bwd(
        dx1, sv["o"], sv["z"], sv["m"], sp["wsp"], sp["bsp"], sp["wbd"], sp["psc"], sp["gsv"], gout, sv["wout"],
        "mix_out_bwd" + t)
    g["wout"] = _wgrad(sv["mix"], dx1, "wgrad_out" + t)
    dp = _pool_win_bwd(dm, "pool_win_bwd" + t)
    dq, dk, dv = _attn_bwd(sv["q"], sv["k"], sv["v"], do, sv["lse"], delta, out_hook(g), "attn_bwd" + t)
    dzm, g["wq"], g["wk"], g["wv"], g["gql"], g["gkv"], g["gq"], g["gk"] = _mla_prep_bwd(
        dq, dk, dv, sv["z"], tabs, sp["gql"], sp["gkv"], sp["gq"], sp["gk"], kw["wq"], kw["wk"], kw["wv"],
        "mla_prep_bwd" + t)
    dx, g["g_mix"] = _in_proj_bwd(dzm, duv, dp, sv["x"], dx1, sp["g_mix"], kw["win"], "in_proj_bwd" + t)
    g["win"] = _wgrad_in(sv["hb"], dzm, duv, dp, "wgrad_in" + t)
    return dx, g


def _rope_inv_freq():
    half = ROPE // 2
    inv = 1.0 / (ROPE_THETA ** (jnp.arange(half, dtype=F32) / half))
    return jnp.concatenate([jnp.zeros((NOPE,), F32), inv, inv, jnp.zeros((HP - QK,), F32)]).reshape(1, HP)


def kernel(x, positions, g_mix_norm, w_in, g_q_lat, w_q_up, g_kv_lat, w_kv_up, g_q_head, g_k_head, g_sgu_v, w_spatial, b_spatial, w_pool, pool_scale, g_out_mla, g_out_sgu, g_out_pool, w_out, g_ffn_norm, w_gate, w_up, w_down, loss_target, m_g_mix_norm, m_w_in, m_g_q_lat, m_w_q_up, m_g_kv_lat, m_w_kv_up, m_g_q_head, m_g_k_head, m_g_sgu_v, m_w_spatial, m_b_spatial, m_w_pool, m_pool_scale, m_g_out_mla, m_g_out_sgu, m_g_out_pool, m_w_out, m_g_ffn_norm, m_w_gate, m_w_up, m_w_down, v_g_mix_norm, v_w_in, v_g_q_lat, v_w_q_up, v_g_kv_lat, v_w_kv_up, v_g_q_head, v_g_k_head, v_g_sgu_v, v_w_spatial, v_b_spatial, v_w_pool, v_pool_scale, v_g_out_mla, v_g_out_sgu, v_g_out_pool, v_w_out, v_g_ffn_norm, v_w_gate, v_w_up, v_w_down):
    given = dict(locals())
    p = {n: given[n] for n in ORDER}
    view = lambda pre, n: jnp.swapaxes(given[pre + n], 1, 2) if n in TRANSPOSED else given[pre + n]
    seq = x.shape[1]
    where = jnp.stack([2 * lax.axis_index("x") + lax.axis_index("y"), lax.axis_index("c")]).astype(jnp.int32)
    shards = lambda names: [view("", n)[l].astype(BF16) for l, n in names]
    zero11 = lambda token: token[:1, :1]

    names_0a = [(0, n) for n in EARLY_BIG]
    names_0b = [(0, n) for n in LATE_BIG]
    names_1 = [(1, n) for n, _, _ in BIG]
    w0a, tabs = _all_gather_chips(shards(names_0a), "all_gather_w0a",
                                  _rope_tables_meanwhile(positions.reshape(seq, 1), _rope_inv_freq()))
    got_0a = dict(zip(EARLY_BIG, w0a))
    started, issued = {}, got_0a["w_in"]
    for tag, names in (("w0b", names_0b), ("w1", names_1)):
        sh = shards(names)
        pairs = _gather_pairs([a.shape[0] // 2 for a in sh], [_row_align(a.dtype) for a in sh])
        lands = [jax.ShapeDtypeStruct((CHIPS,) + a.shape, a.dtype) for a in sh]
        started[tag] = (sh, pairs) + _split_start(sh, lands, 3 * len(sh), pairs, "gather_start_" + tag, issued)
        issued = started[tag][6]

    def arrived(tag, after):
        _, pairs, send, recv, srcs, lands, _ = started[tag]
        srcs, lands = _split_wait(send, recv, srcs, lands, after, pairs, "gather_wait_" + tag)
        return _gather_finish(srcs, lands, "gather_finish_" + tag)

    layer1 = {}

    def mix_weights(l, h):
        if l == 0:
            return got_0a
        layer1.update(zip([n for _, n in names_1], arrived("w1", h)))
        return layer1

    def late_weights(l, o):
        return arrived("w0b", o) if l == 0 else [layer1[n] for n in LATE_BIG]

    reducing, last = {}, {}

    def reduce_start(tag, arrs):
        lands = [jax.ShapeDtypeStruct((PEERS, a.shape[1] // 2, a.shape[2]), a.dtype) for a in arrs]
        reducing[tag] = _split_start(arrs, lands, PEERS * len(arrs), _scatter_pairs, "grad_scatter_start_" + tag, where)
        return zero11(reducing[tag][4])

    def reduce_finish(tag, after):
        send, recv, srcs, lands, _ = reducing[tag]
        srcs, lands = _split_wait(send, recv, srcs, lands, after, _scatter_pairs, "grad_scatter_wait_" + tag)
        sums = [None] * len(srcs)
        for shape in dict.fromkeys(a.shape for a in srcs):
            idx = [i for i, a in enumerate(srcs) if a.shape == shape]
            res = _sum_own_and_landed([srcs[i] for i in idx], [lands[i] for i in idx], where, f"grad_sum_{tag}_{idx[0]}")
            for i, r in zip(idx, res):
                sums[i] = r
        return sums

    def ffn_hook(l, g):
        if l == 1:
            return jnp.zeros((1, 1), F32)
        return reduce_start("g0b", [g["wg"], g["wu"], g["wd"]])

    def out_hook(l, g):
        if l == 1:
            return where
        reduce_start("g0c", [g["wout"].reshape(CHIPS, D // CHIPS, D)])
        return reducing["g0c"][4]

    def layer_hook(l, big, small):
        last[l] = (big, small)
        if l == 1:
            return reduce_start("g1", [big[n] for n, _, _ in BIG])
        return None

    entry = zero11(started["w0b"][6]) + zero11(started["w1"][6])
    loss_part, dx = _step(x.reshape(seq, D), tuple(tabs), loss_target.reshape(seq, D), p, entry,
                          mix_weights, late_weights, ffn_hook, out_hook, layer_hook)

    def adamw(n, g0, g1):
        flip = n in EARLY_BIG
        pick = lambda pre: jnp.swapaxes(given[pre + n], 1, 2) if flip else view(pre, n)
        w = pick("")
        three_d = (DEPTH, -1, w.shape[-1])
        g0, g1 = (g.T if flip else g for g in (g0, g1))
        res = _adamw(w.reshape(three_d), g0.reshape(three_d[1:]), g1.reshape(three_d[1:]),
                     pick("m_").reshape(three_d), pick("v_").reshape(three_d), "adamw_" + n)
        return [jnp.swapaxes(r.reshape(w.shape), 1, 2) if flip else r.reshape(w.shape) for r in res]

    names_rest = [(0, n) for n in EARLY_BIG]
    reduce_start("g0a", [last[0][0][n] for _, n in names_rest]
                 + [_pack_small_grads([last[l][1] for l in range(DEPTH)], loss_part)])
    token = reducing["g0a"][4]
    early = names_1 + [(0, n) for n in FFN_BIG] + [(0, "w_out")]
    landed = reduce_finish("g1", token) + reduce_finish("g0b", token) + reduce_finish("g0c", token)
    sums = dict(zip(early, _pair_join(landed, "grad_pair_join_early")))
    out = {n: adamw(n, sums[(0, n)], sums[(1, n)]) for n in FFN_BIG}
    late = names_rest + ["small"]
    sums.update(zip(late, _pair_join(reduce_finish("g0a", out["w_down"][1]), "grad_pair_join_late")))
    gsmall, loss = _unpack_small_grads(_all_gather_chips([sums["small"]], "all_gather_small_grads")[0])
    vectors = [n for n, shape in SMALL if len(shape) == 1]
    res = _adamw_vectors([given[n] for n in vectors], *[[gsmall[l][n].reshape(1, -1) for n in vectors] for l in range(DEPTH)],
                         [given["m_" + n] for n in vectors], [given["v_" + n] for n in vectors], "adamw_vectors")
    out.update({n: res[i::len(vectors)] for i, n in enumerate(vectors)})
    for n in ORDER:
        if n not in out:
            g = [sums[(l, n)] for l in range(DEPTH)] if (0, n) in sums else [gsmall[l][n] for l in range(DEPTH)]
            out[n] = adamw(n, *g)
    undo = lambda n, a: jnp.swapaxes(a, 1, 2) if n in TRANSPOSED else a
    return (loss, dx.reshape(x.shape), *[undo(n, out[n][i]) for i in range(4) for n in ORDER])


def _step(xs, tabs, tgt, p, entry, mix_weights, late_weights, ffn_hook, out_hook, layer_hook):
    sps = [_small_operands(p, l) for l in range(DEPTH)]
    sps[0]["g_mix"] = sps[0]["g_mix"] + entry
    saved, h = [], xs
    for l in range(DEPTH):
        kw = _kernel_weights(mix_weights(l, h))
        h, sv = _layer_fwd(h, tabs, kw, functools.partial(late_weights, l), sps[l], l, tgt if l == DEPTH - 1 else None)
        saved.append(dict(sv, kw=kw))
    dy, lpart = h
    for l in reversed(range(DEPTH)):
        dy, g = _layer_bwd(dy, saved[l], tabs, saved[l]["kw"], sps[l], l, functools.partial(ffn_hook, l),
                           functools.partial(out_hook, l))
        zero = layer_hook(l, _big_grads(g), _small_grads(g))
        if zero is not None and l > 0:
            sps[l - 1]["g_ffn"] = sps[l - 1]["g_ffn"] + zero
    return 0.5 / D * jnp.sum(lpart), dy
```

```python
import functools
import math

import jax
import jax.numpy as jnp
from jax import lax
from jax.experimental import pallas as pl
from jax.experimental.pallas import tpu as pltpu

F32 = jnp.float32
BF16 = jnp.bfloat16
MESH = pl.DeviceIdType.MESH

D = 1024
HEADS = 4
QK = 96
NOPE = 64
ROPE = 32
VH = 128
HP = 128
QL = 256
KVL = 128
SGU = 256
POOL = 256
CHUNK = 128
HID = 2816
CHIPS = 4
SH = HID // CHIPS
IN_W = 1184
IN_P = 1280
EPS = 1e-6
ROPE_THETA = 10000.0
SCALE = 1.0 / math.sqrt(QK)
LOG2E = 1.4426950408889634
EXP2_C = SCALE * LOG2E
ATT_WIDE = 2
ATT_FWD_QUERIES = 2048
ATT_PIECE = 1024
ATT_ROWS = 256
ATT_KEYS = 1024
ATT_QUERIES = 2048
NEG = -1e30
HALO = 16

LR, B1, B2, ADAM_EPS, WD, STEP = 0.001, 0.9, 0.999, 1e-08, 0.01, 10

VMEM_LIMIT = 56 * 1024 * 1024
LANES = 128
TOKENS = 1024


def _cp(sem, vmem=None):
    return pltpu.CompilerParams(dimension_semantics=sem, vmem_limit_bytes=vmem)


def _res(shape):
    nd = len(shape)
    return pl.BlockSpec(shape, lambda *_: (0,) * nd, pipeline_mode=pl.Buffered(1))


def _acc(shape):
    nd = len(shape)
    return pl.BlockSpec(shape, lambda *_: (0,) * nd)


def _dot(a, b):
    return jnp.dot(a, b, preferred_element_type=F32)


def _dot_nt(a, b):
    return lax.dot_general(a, b, (((1,), (1,)), ((), ())), preferred_element_type=F32)


def _dot_tn(a, b):
    return lax.dot_general(a, b, (((0,), (0,)), ((), ())), preferred_element_type=F32)


def _rms(x, n):
    r = lax.rsqrt(jnp.sum(x * x, axis=-1, keepdims=True) * (1.0 / n) + EPS)
    return x * r, r


def _head_ones():
    row = lax.broadcasted_iota(jnp.int32, (HEADS * HP, HEADS * HP), 0) // HP
    col = lax.broadcasted_iota(jnp.int32, (HEADS * HP, HEADS * HP), 1) // HP
    return (row == col).astype(BF16)


def _head_sum(x, ones):
    return _dot(x.astype(BF16), ones)


def _head_rms(x, ones):
    r = lax.rsqrt(_head_sum(x * x, ones) * (1.0 / QK) + EPS)
    return x * r, r


def _rms_bwd(xn, r, g, dy, n):
    dn = dy * g
    dx = r * (dn - xn * (jnp.sum(dn * xn, axis=-1, keepdims=True) * (1.0 / n)))
    return dx, jnp.sum(dy * xn, axis=0, keepdims=True)


def _accumulate(ref, val, first):
    @pl.when(first)
    def _():
        ref[...] = val

    @pl.when(jnp.logical_not(first))
    def _():
        ref[...] += val


def _accumulate0(ref, val, first):
    @pl.when(first)
    def _():
        ref[0] = val

    @pl.when(jnp.logical_not(first))
    def _():
        ref[0] += val


def _tile(s, t):
    return min(s, t)


def _row_tile(r, cap):
    if r <= cap:
        return r
    return max(t for t in range(8, cap + 1, 8) if r % t == 0)


def _rope_tables_meanwhile(pos, invf):
    s = pos.shape[0]
    tm = _tile(s, 256)
    steps = s // tm // 2

    def work(ins, outs, part):
        pos_ref, invf_ref = ins
        c_ref, sa_ref, sb_ref = outs

        def step(i, carry):
            rows = pl.ds(pl.multiple_of(i * tm, tm), tm)
            ang = pos_ref[rows, :].astype(F32) * invf_ref[...]
            c, sn = jnp.cos(ang), jnp.sin(ang)
            lane = lax.broadcasted_iota(jnp.int32, ang.shape, 1)
            first = (lane >= NOPE) & (lane < NOPE + ROPE // 2)
            second = (lane >= NOPE + ROPE // 2) & (lane < QK)
            c_ref[rows, :] = jnp.where(first | second, c, 1.0)
            sa_ref[rows, :] = jnp.where(first, -sn, 0.0)
            sb_ref[rows, :] = jnp.where(second, sn, 0.0)
            return carry

        lax.fori_loop(part * steps, (part + 1) * steps, step, 0)

    return work, [pos, invf], [jax.ShapeDtypeStruct((s, HP), F32)] * 3


def _rope(x, c, sa, sb):
    return x * c + pltpu.roll(x, HP - ROPE // 2, 1) * sa + pltpu.roll(x, ROPE // 2, 1) * sb


def _rope_t(d, c, sa, sb):
    return d * c + pltpu.roll(d * sa, ROPE // 2, 1) + pltpu.roll(d * sb, HP - ROPE // 2, 1)


def _in_proj_fwd(x, g, w, name):
    s = x.shape[0]
    tm = _tile(s, TOKENS)

    def body(x_ref, g_ref, w_ref, z_ref, h_ref):
        xn, _ = _rms(x_ref[...], D)
        h = (xn * g_ref[...]).astype(BF16)
        h_ref[...] = h
        z_ref[...] = _dot(h, w_ref[...])

    return pl.pallas_call(
        body, name=name, grid=(s // tm,),
        in_specs=[pl.BlockSpec((tm, D), lambda i: (i, 0)), _acc((1, D)), _res((D, IN_P))],
        out_specs=[pl.BlockSpec((tm, IN_P), lambda i: (i, 0)), pl.BlockSpec((tm, D), lambda i: (i, 0))],
        out_shape=[jax.ShapeDtypeStruct((s, IN_P), F32), jax.ShapeDtypeStruct((s, D), BF16)],
        compiler_params=_cp(("parallel",), VMEM_LIMIT),
    )(x, g, w)


def _mla_prep_fwd(z, tabs, gql, gkv, gq, gk, wq, wk, wv, name):
    s = z.shape[0]
    tm = _tile(s, TOKENS)

    def body(ql_ref, kv_ref, kr_ref, c_ref, sa_ref, sb_ref, gql_ref, gkv_ref, gq_ref, gk_ref,
             wq_ref, wk_ref, wv_ref, q_out, k_out, v_out):
        qn = (_rms(ql_ref[...], QL)[0] * gql_ref[...]).astype(BF16)
        kvn = (_rms(kv_ref[...], KVL)[0] * gkv_ref[...]).astype(BF16)
        qraw = _dot(qn, wq_ref[...])
        kraw = _dot(kvn, wk_ref[...])
        vraw = _dot(kvn, wv_ref[...])
        kr = kr_ref[...]
        c, sa, sb = c_ref[...], sa_ref[...], sb_ref[...]
        ones = _head_ones()
        xq_all = _head_rms(qraw, ones)[0]
        xk_all = _head_rms(kraw + jnp.concatenate([kr] * HEADS, axis=1), ones)[0]
        for h in range(HEADS):
            sl = slice(h * HP, (h + 1) * HP)
            q_out[h] = (_rope(xq_all[:, sl] * gq_ref[...], c, sa, sb) * EXP2_C).astype(BF16)
            k_out[h] = _rope(xk_all[:, sl] * gk_ref[...], c, sa, sb).astype(BF16)
            v_out[h] = vraw[:, sl].astype(BF16)

    row = lambda w, j: pl.BlockSpec((tm, w), lambda i: (i, j))
    hspec = pl.BlockSpec((HEADS, tm, HP), lambda i: (0, i, 0))
    hshape = jax.ShapeDtypeStruct((HEADS, s, HP), BF16)
    return pl.pallas_call(
        body, name=name, grid=(s // tm,),
        in_specs=[row(QL, 0), row(KVL, 2), row(HP, 3), row(HP, 0), row(HP, 0), row(HP, 0),
                  _acc((1, QL)), _acc((1, KVL)), _acc((1, HP)), _acc((1, HP)),
                  _acc((QL, HEADS * HP)), _acc((KVL, HEADS * HP)), _acc((KVL, HEADS * HP))],
        out_specs=[hspec] * 3, out_shape=[hshape] * 3,
        compiler_params=_cp(("parallel",)),
    )(z, z, z, *tabs, gql, gkv, gq, gk, wq, wk, wv)


def _causal_mask(s, row0):
    row = lax.broadcasted_iota(jnp.int32, s.shape, 0) + row0
    col = lax.broadcasted_iota(jnp.int32, s.shape, 1)
    return jnp.where(col <= row, s, NEG)


def _attn_fwd(q, k, v, name):
    s = q.shape[1]
    tq = _tile(s, ATT_FWD_QUERIES)
    rh = _tile(s, ATT_ROWS)
    kp = _tile(s, ATT_PIECE)
    wide = ATT_WIDE * kp if s % (ATT_WIDE * kp) == 0 else tq
    groups = tq // rh

    def body(q_ref, k_ref, v_ref, o_ref, lse_ref):
        i = pl.program_id(1)

        def blk(off, tk, carry, diagonal):
            width = lambda g, t: max(0, min(kp, (g + 1) * rh - t * kp)) if diagonal else kp
            rows = lambda t: pl.ds(pl.multiple_of(off + t * kp, kp), kp)
            score = lambda g, t: _dot_nt(q_ref[0, g * rh:(g + 1) * rh, :], k_ref[0, rows(t), :][:width(g, t)])
            live = lambda t: [g for g in range(groups) if width(g, t) > 0]
            state = list(carry)
            scs = {(g, 0): score(g, 0) for g in live(0)}
            for t in range(tk // kp):
                if (t + 1) * kp < tk:
                    scs.update({(g, t + 1): score(g, t + 1) for g in live(t + 1)})
                vt = v_ref[0, rows(t), :]
                for g in live(t):
                    m, l, acc = state[g]
                    sc = scs.pop((g, t))
                    if diagonal and (g + 1) * rh <= (t + 1) * kp:
                        sc = _causal_mask(sc, g * rh - t * kp)
                    m_new = jnp.maximum(m, jnp.max(sc, axis=-1, keepdims=True))
                    p = jnp.exp2(sc - m_new)
                    alpha = jnp.exp2(m - m_new)
                    l = alpha * l + jnp.sum(p, axis=-1, keepdims=True)
                    acc = alpha * acc + _dot(p.astype(BF16), vt[:width(g, t)])
                    state[g] = (m_new, l, acc)
            return tuple(state)

        one = (jnp.full((rh, 1), NEG, F32), jnp.zeros((rh, 1), F32), jnp.zeros((rh, VH), F32))
        nwide = (i * tq) // wide
        carry = lax.fori_loop(0, nwide, lambda j, c: blk(j * wide, wide, c, False), (one,) * groups)
        carry = lax.fori_loop(nwide * (wide // tq), i, lambda j, c: blk(j * tq, tq, c, False), carry)
        carry = blk(i * tq, tq, carry, True)
        for g, (m, l, acc) in enumerate(carry):
            o_ref[g * rh:(g + 1) * rh, :] = acc / l
            lse_ref[0, g * rh:(g + 1) * rh, :] = jnp.broadcast_to(m + jnp.log(l) * LOG2E, (rh, LANES))

    return pl.pallas_call(
        body, name=name, grid=(HEADS, s // tq),
        in_specs=[pl.BlockSpec((1, tq, HP), lambda h, i: (h, i, 0)),
                  pl.BlockSpec((1, s, HP), lambda h, i: (h, 0, 0)),
                  pl.BlockSpec((1, s, HP), lambda h, i: (h, 0, 0))],
        out_specs=[pl.BlockSpec((tq, VH), lambda h, i: (i, h)),
                   pl.BlockSpec((1, tq, LANES), lambda h, i: (h, i, 0))],
        out_shape=[jax.ShapeDtypeStruct((s, HEADS * VH), F32), jax.ShapeDtypeStruct((HEADS, s, LANES), F32)],
        compiler_params=_cp(("parallel", "arbitrary"), VMEM_LIMIT),
    )(q, k, v)


def _lane_group(shape, j):
    return (lax.broadcasted_iota(jnp.int32, shape, 1) + j * LANES) // (POOL // 4)


def _pool_win_fwd(z, name):
    s = z.shape[0]
    ch = _tile(s, 512)
    col0 = (IN_P - POOL) // LANES

    def body(p_ref, m_ref):
        j = pl.program_id(0)

        def chunk(r, _):
            off = pl.multiple_of(r * ch, ch)
            cur = p_ref[pl.ds(off, ch), :]
            hoff = pl.multiple_of(jnp.maximum(off - HALO, 0), 8)
            halo = jnp.where(r > 0, p_ref[pl.ds(hoff, HALO), :], 0.0)
            x = jnp.concatenate([halo, cur], axis=0)
            s2 = x + pltpu.roll(x, 1, 0)
            s4 = s2 + pltpu.roll(s2, 2, 0)
            s8 = s4 + pltpu.roll(s4, 4, 0)
            s16 = s8 + pltpu.roll(s8, 8, 0)
            grp = _lane_group((ch, LANES), j)
            sel = jnp.where(grp == 0, s2[HALO:], jnp.where(grp == 1, s4[HALO:], jnp.where(grp == 2, s8[HALO:], s16[HALO:])))
            t1 = (lax.broadcasted_iota(jnp.int32, (ch, LANES), 0) + off + 1).astype(F32)
            win = jnp.where(grp == 0, 2.0, jnp.where(grp == 1, 4.0, jnp.where(grp == 2, 8.0, 16.0)))
            m_ref[pl.ds(off, ch), :] = sel / jnp.minimum(t1, win) - cur
            return 0

        lax.fori_loop(0, s // ch, chunk, 0)

    return pl.pallas_call(
        body, name=name, grid=(POOL // LANES,),
        in_specs=[pl.BlockSpec((s, LANES), lambda j: (0, col0 + j))],
        out_specs=pl.BlockSpec((s, LANES), lambda j: (0, j)),
        out_shape=jax.ShapeDtypeStruct((s, POOL), F32),
        compiler_params=_cp(("parallel",), VMEM_LIMIT),
    )(z)


def _pool_win_bwd(dm, name):
    s = dm.shape[0]
    ch = _tile(s, 512)
    n = s // ch

    def body(dm_ref, dp_ref):
        j = pl.program_id(0)

        def chunk(r, _):
            off = pl.multiple_of(r * ch, ch)
            grp = _lane_group((ch + HALO, LANES), j)
            win = jnp.where(grp == 0, 2.0, jnp.where(grp == 1, 4.0, jnp.where(grp == 2, 8.0, 16.0)))
            cur = dm_ref[pl.ds(off, ch), :]
            hoff = pl.multiple_of(jnp.minimum(off + ch, s - HALO), 8)
            halo = jnp.where(r < n - 1, dm_ref[pl.ds(hoff, HALO), :], 0.0)
            x = jnp.concatenate([cur, halo], axis=0)
            t1 = (lax.broadcasted_iota(jnp.int32, (ch + HALO, LANES), 0) + off + 1).astype(F32)
            e = x / jnp.minimum(t1, win)
            tot = ch + HALO
            r2 = e + pltpu.roll(e, tot - 1, 0)
            r4 = r2 + pltpu.roll(r2, tot - 2, 0)
            r8 = r4 + pltpu.roll(r4, tot - 4, 0)
            r16 = r8 + pltpu.roll(r8, tot - 8, 0)
            g = grp[:ch]
            sel = jnp.where(g == 0, r2[:ch], jnp.where(g == 1, r4[:ch], jnp.where(g == 2, r8[:ch], r16[:ch])))
            dp_ref[pl.ds(off, ch), :] = (sel - cur).astype(BF16)
            return 0

        lax.fori_loop(0, n, chunk, 0)

    return pl.pallas_call(
        body, name=name, grid=(POOL // LANES,),
        in_specs=[pl.BlockSpec((s, LANES), lambda j: (0, j))],
        out_specs=pl.BlockSpec((s, LANES), lambda j: (0, j)),
        out_shape=jax.ShapeDtypeStruct((s, POOL), BF16),
        compiler_params=_cp(("parallel",), VMEM_LIMIT),
    )(dm)


def _head_mask(h):
    lane = lax.broadcasted_iota(jnp.int32, (CHUNK, SGU), 1)
    return (lane // (SGU // HEADS)) == h


def _tril(upper=False):
    row = lax.broadcasted_iota(jnp.int32, (CHUNK, CHUNK), 0)
    col = lax.broadcasted_iota(jnp.int32, (CHUNK, CHUNK), 1)
    return col >= row if upper else col <= row


def _sgu_gate(vn, wsp, bsp):
    out = []
    for cidx in range(vn.shape[0] // CHUNK):
        vc = vn[cidx * CHUNK:(cidx + 1) * CHUNK]
        zc = bsp
        for h in range(HEADS):
            zc = zc + jnp.where(_head_mask(h), _dot(wsp[h], vc), 0.0)
        out.append(zc)
    return jnp.concatenate(out, axis=0)


def _mix_out_fwd(o, z, m, x, wsp, bsp, wbd, psc, gsv, gout, wout, name):
    s = x.shape[0]
    tm = _tile(s, TOKENS)

    def body(o_ref, uv_ref, m_ref, x_ref, wsp_ref, bsp_ref, wbd_ref, psc_ref, gsv_ref, gout_ref, wout_ref,
             x1_ref, mix_ref):
        g = gout_ref[...]
        an = _rms(o_ref[...], HEADS * VH)[0] * g[:, :512]
        uv = uv_ref[...]
        u, v = uv[:, :SGU], uv[:, SGU:]
        vn = (_rms(v, SGU)[0] * gsv_ref[...]).astype(BF16)
        tri = _tril()
        wsp_m = [jnp.where(tri, wsp_ref[h], 0.0).astype(BF16) for h in range(HEADS)]
        gm = u * _sgu_gate(vn, wsp_m, bsp_ref[...])
        gn = _rms(gm, SGU)[0] * g[:, 512:768]
        po = _dot(m_ref[...].astype(BF16), wbd_ref[...]) * psc_ref[...]
        pn = _rms(po, POOL)[0] * g[:, 768:]
        mix = jnp.concatenate([an, gn, pn], axis=1).astype(BF16)
        mix_ref[...] = mix
        x1_ref[...] = x_ref[...] + _dot(mix, wout_ref[...])

    row = lambda w, j: pl.BlockSpec((tm, w), lambda i: (i, j))
    return pl.pallas_call(
        body, name=name, grid=(s // tm,),
        in_specs=[row(512, 0), row(512, 1), row(POOL, 0), row(D, 0),
                  _acc((HEADS, CHUNK, CHUNK)), _acc((CHUNK, SGU)), _acc((POOL, POOL)), _acc((1, POOL)),
                  _acc((1, SGU)), _acc((1, D)), _res((D, D))],
        out_specs=[row(D, 0), row(D, 0)],
        out_shape=[jax.ShapeDtypeStruct((s, D), F32), jax.ShapeDtypeStruct((s, D), BF16)],
        compiler_params=_cp(("parallel",), VMEM_LIMIT),
    )(o, z, m, x, wsp, bsp, wbd, psc, gsv, gout, wout)


def _ffn_fwd(x1, g, wg, wu, wd, tgt, name):
    s = x1.shape[0]
    tm = _tile(s, 256)
    last = tgt is not None

    def body(x_ref, g_ref, wg_ref, wu_ref, wd_ref, *rest):
        t_ref = rest[0] if last else None
        outs = rest[1:] if last else rest
        a_ref, b_ref, h_ref = outs[-3:]
        x = x_ref[...]
        h = (_rms(x, D)[0] * g_ref[...]).astype(BF16)
        h_ref[...] = h
        acc = jnp.zeros((tm, D), F32)
        for k in range(CHIPS):
            a = _dot_nt(h, wg_ref[k])
            b = _dot_nt(h, wu_ref[k])
            a_ref[k] = a
            b_ref[k] = b
            acc = acc + _dot((a * jax.nn.sigmoid(a) * b).astype(BF16), wd_ref[k])
        if not last:
            outs[0][...] = x + acc
            return
        dy_ref, l_ref = outs[:2]
        e = (x + acc) - t_ref[...]
        dy_ref[...] = e * (1.0 / D)
        sq = jnp.sum(e * e, axis=0, keepdims=True)
        part = sq[:, :LANES]
        for c in range(1, D // LANES):
            part = part + sq[:, c * LANES:(c + 1) * LANES]
        _accumulate(l_ref, part, pl.program_id(0) == 0)

    row = lambda w: pl.BlockSpec((tm, w), lambda i: (i, 0))
    hrow = pl.BlockSpec((CHIPS, tm, SH), lambda i: (0, i, 0))
    hshape = jax.ShapeDtypeStruct((CHIPS, s, SH), F32)
    tail_specs = [hrow, hrow, row(D)]
    tail_shapes = [hshape, hshape, jax.ShapeDtypeStruct((s, D), BF16)]
    head_specs = [row(D), _acc((1, LANES))] if last else [row(D)]
    head_shapes = [jax.ShapeDtypeStruct((s, D), F32)] + ([jax.ShapeDtypeStruct((1, LANES), F32)] if last else [])
    res = pl.pallas_call(
        body, name=name, grid=(s // tm,),
        in_specs=[row(D), _acc((1, D)), _res((CHIPS, SH, D)), _res((CHIPS, SH, D)), _res((CHIPS, SH, D))]
        + ([row(D)] if last else []),
        out_specs=head_specs + tail_specs, out_shape=head_shapes + tail_shapes,
        compiler_params=_cp(("arbitrary",), VMEM_LIMIT),
    )(x1, g, wg, wu, wd, *([tgt] if last else []))
    return (tuple(res[:2]) if last else res[0]), res[-3], res[-2], res[-1]


def _wgrad(a, b, name):
    s, k = a.shape
    n = b.shape[1]
    half = lambda v: v if v <= 1408 else v // 2
    kb, nb, tt = half(k), half(n), _tile(s, 2048)

    def body(a_ref, b_ref, o_ref):
        _accumulate(o_ref, _dot_tn(a_ref[...].astype(BF16), b_ref[...].astype(BF16)), pl.program_id(2) == 0)

    return pl.pallas_call(
        body, name=name, grid=(k // kb, n // nb, s // tt),
        in_specs=[pl.BlockSpec((tt, kb), lambda i, j, t: (t, i)), pl.BlockSpec((tt, nb), lambda i, j, t: (t, j))],
        out_specs=pl.BlockSpec((kb, nb), lambda i, j, t: (i, j)),
        out_shape=jax.ShapeDtypeStruct((k, n), F32),
        compiler_params=_cp(("parallel", "parallel", "arbitrary"), VMEM_LIMIT),
    )(a, b)


def _wgrad_in(h, dzm, duv, dp, name):
    s = h.shape[0]
    tt = _tile(s, 2048)

    def body(h_ref, a_ref, b_ref, c_ref, o_ref):
        hv = h_ref[...]
        val = jnp.concatenate([_dot_tn(hv, a_ref[...]), _dot_tn(hv, b_ref[...]), _dot_tn(hv, c_ref[...])], axis=1)
        _accumulate(o_ref, val, pl.program_id(0) == 0)

    row = lambda w: pl.BlockSpec((tt, w), lambda t: (t, 0))
    return pl.pallas_call(
        body, name=name, grid=(s // tt,), in_specs=[row(D), row(512), row(512), row(POOL)], out_specs=_acc((D, IN_P)),
        out_shape=jax.ShapeDtypeStruct((D, IN_P), F32), compiler_params=_cp(("arbitrary",), VMEM_LIMIT),
    )(h, dzm, duv, dp)


def _wgrad_rows(a, b, name):
    s, n = a.shape[1:]
    nn = b.shape[1]
    tt = _tile(s, 4096 if b.dtype == BF16 else 2048)

    def body(a_ref, b_ref, o_ref):
        _accumulate0(o_ref, _dot_tn(a_ref[0].astype(BF16), b_ref[...].astype(BF16)), pl.program_id(1) == 0)

    return pl.pallas_call(
        body, name=name, grid=(CHIPS, s // tt),
        in_specs=[pl.BlockSpec((1, tt, n), lambda c, t: (c, t, 0)), pl.BlockSpec((tt, nn), lambda c, t: (t, 0))],
        out_specs=pl.BlockSpec((1, n, nn), lambda c, t: (c, 0, 0)),
        out_shape=jax.ShapeDtypeStruct((CHIPS, n, nn), F32),
        compiler_params=_cp(("parallel", "arbitrary"), VMEM_LIMIT),
    )(a, b)


def _ffn_bwd(dx2, x1, a, b, g, wg, wu, wd, name):
    s = x1.shape[0]
    tm = _tile(s, 256)

    def body(dx2_ref, x_ref, a_ref, b_ref, g_ref, wg_ref, wu_ref, wd_ref,
             dx1_ref, hid_ref, da_ref, db_ref, dyb_ref, dg_ref):
        dx2 = dx2_ref[...]
        dyb = dx2.astype(BF16)
        dyb_ref[...] = dyb
        dh = jnp.zeros((tm, D), F32)
        ahead = _dot_nt(dyb, wd_ref[0])
        for k in range(CHIPS):
            av, bv = a_ref[k], b_ref[k]
            dhid = ahead
            if k + 1 < CHIPS:
                ahead = _dot_nt(dyb, wd_ref[k + 1])
            sig = jax.nn.sigmoid(av)
            sa = av * sig
            hid_ref[k] = (sa * bv).astype(BF16)
            dbv = (dhid * sa).astype(BF16)
            dav = (dhid * bv * (sig * (1.0 + av * (1.0 - sig)))).astype(BF16)
            db_ref[k] = dbv
            da_ref[k] = dav
            dh = dh + _dot(dav, wg_ref[k]) + _dot(dbv, wu_ref[k])
        xn, r = _rms(x_ref[...], D)
        dxr, dg = _rms_bwd(xn, r, g_ref[...], dh, D)
        dx1_ref[...] = dx2 + dxr
        _accumulate(dg_ref, dg, pl.program_id(0) == 0)

    row = lambda w: pl.BlockSpec((tm, w), lambda i: (i, 0))
    hrow = pl.BlockSpec((CHIPS, tm, SH), lambda i: (0, i, 0))
    hid = jax.ShapeDtypeStruct((CHIPS, s, SH), BF16)
    return pl.pallas_call(
        body, name=name, grid=(s // tm,),
        in_specs=[row(D), row(D), hrow, hrow, _acc((1, D)), _res((CHIPS, SH, D)), _res((CHIPS, SH, D)),
                  _res((CHIPS, SH, D))],
        out_specs=[row(D), hrow, hrow, hrow, row(D), _acc((1, D))],
        out_shape=[jax.ShapeDtypeStruct((s, D), F32), hid, hid, hid, jax.ShapeDtypeStruct((s, D), BF16),
                   jax.ShapeDtypeStruct((1, D), F32)],
        compiler_params=_cp(("arbitrary",), VMEM_LIMIT),
    )(dx2, x1, a, b, g, wg, wu, wd)


def _mix_out_bwd(dx1, o, z, m, wsp, bsp, wbd, psc, gsv, gout, wout, name):
    s = dx1.shape[0]
    tm = _tile(s, TOKENS)

    def body(dx1_ref, o_ref, uv_ref, m_ref, wsp_ref, bsp_ref, wbd_ref, psc_ref, gsv_ref, gout_ref, wout_ref,
             do_ref, dl_ref, duv_ref, dm_ref, dgo_ref, dgsv_ref, dpsc_ref, dwsp_ref, dbsp_ref, dwbd_ref):
        first = pl.program_id(0) == 0
        g = gout_ref[...]
        dmix = _dot_nt(dx1_ref[...].astype(BF16), wout_ref[...])
        o = o_ref[...]
        on, ro = _rms(o, HEADS * VH)
        do, dga = _rms_bwd(on, ro, g[:, :512], dmix[:, :512], HEADS * VH)
        for h in range(HEADS):
            sl = slice(h * VH, (h + 1) * VH)
            do_ref[h] = do[:, sl].astype(BF16)
            dl_ref[h] = jnp.broadcast_to(jnp.sum(do[:, sl] * o[:, sl], axis=-1, keepdims=True), (tm, LANES))
        uv = uv_ref[...]
        u, v = uv[:, :SGU], uv[:, SGU:]
        vx, rv = _rms(v, SGU)
        vn = (vx * gsv_ref[...]).astype(BF16)
        tri = _tril()
        wsp_m = [jnp.where(tri, wsp_ref[h], 0.0).astype(BF16) for h in range(HEADS)]
        zc = _sgu_gate(vn, wsp_m, bsp_ref[...])
        gm = u * zc
        gmn, rg = _rms(gm, SGU)
        dgm, dgg = _rms_bwd(gmn, rg, g[:, 512:768], dmix[:, 512:768], SGU)
        du = dgm * zc
        dzc = dgm * u
        dvn_parts = []
        dbsp = jnp.zeros((CHUNK, SGU), F32)
        dwsp = [jnp.zeros((CHUNK, CHUNK), F32) for _ in range(HEADS)]
        for cidx in range(tm // CHUNK):
            rs = slice(cidx * CHUNK, (cidx + 1) * CHUNK)
            dzc_c = dzc[rs]
            dbsp = dbsp + dzc_c
            dzb = dzc_c.astype(BF16)
            vc = vn[rs]
            dvn_c = jnp.zeros((CHUNK, SGU), F32)
            for h in range(HEADS):
                hm = _head_mask(h)
                dvn_c = dvn_c + jnp.where(hm, _dot_tn(wsp_m[h], dzb), 0.0)
                dwsp[h] = dwsp[h] + _dot_nt(jnp.where(hm, dzc_c, 0.0).astype(BF16), vc)
            dvn_parts.append(dvn_c)
        dvn = jnp.concatenate(dvn_parts, axis=0)
        dv, dgsv = _rms_bwd(vx, rv, gsv_ref[...], dvn, SGU)
        duv_ref[...] = jnp.concatenate([du, dv], axis=1).astype(BF16)
        mb = m_ref[...].astype(BF16)
        pw = _dot(mb, wbd_ref[...])
        po = pw * psc_ref[...]
        pon, rp = _rms(po, POOL)
        dpo, dgp = _rms_bwd(pon, rp, g[:, 768:], dmix[:, 768:], POOL)
        dpw = (dpo * psc_ref[...]).astype(BF16)
        dm_ref[...] = _dot_nt(dpw, wbd_ref[...])
        _accumulate(dgo_ref, jnp.concatenate([dga, dgg, dgp], axis=1), first)
        _accumulate(dgsv_ref, dgsv, first)
        _accumulate(dpsc_ref, jnp.sum(dpo * pw, axis=0, keepdims=True), first)
        _accumulate(dbsp_ref, dbsp, first)
        _accumulate(dwbd_ref, _dot_tn(mb, dpw), first)
        for h in range(HEADS):
            val = jnp.where(tri, dwsp[h], 0.0)

            @pl.when(first)
            def _(val=val, h=h):
                dwsp_ref[h] = val

            @pl.when(jnp.logical_not(first))
            def _(val=val, h=h):
                dwsp_ref[h] += val

    row = lambda w, j: pl.BlockSpec((tm, w), lambda i: (i, j))
    hspec = pl.BlockSpec((HEADS, tm, HP), lambda i: (0, i, 0))
    return pl.pallas_call(
        body, name=name, grid=(s // tm,),
        in_specs=[row(D, 0), row(512, 0), row(512, 1), row(POOL, 0),
                  _acc((HEADS, CHUNK, CHUNK)), _acc((CHUNK, SGU)),
                  _acc((POOL, POOL)), _acc((1, POOL)), _acc((1, SGU)), _acc((1, D)), _res((D, D))],
        out_specs=[hspec, hspec, row(512, 0), row(POOL, 0), _acc((1, D)), _acc((1, SGU)), _acc((1, POOL)),
                   _acc((HEADS, CHUNK, CHUNK)), _acc((CHUNK, SGU)), _acc((POOL, POOL))],
        out_shape=[jax.ShapeDtypeStruct((HEADS, s, HP), BF16), jax.ShapeDtypeStruct((HEADS, s, LANES), F32),
                   jax.ShapeDtypeStruct((s, 512), BF16), jax.ShapeDtypeStruct((s, POOL), F32),
                   jax.ShapeDtypeStruct((1, D), F32), jax.ShapeDtypeStruct((1, SGU), F32),
                   jax.ShapeDtypeStruct((1, POOL), F32), jax.ShapeDtypeStruct((HEADS, CHUNK, CHUNK), F32),
                   jax.ShapeDtypeStruct((CHUNK, SGU), F32), jax.ShapeDtypeStruct((POOL, POOL), F32)],
        compiler_params=_cp(("arbitrary",), VMEM_LIMIT),
    )(dx1, o, z, m, wsp, bsp, wbd, psc, gsv, gout, wout)


def _attn_bwd(q, k, v, do, lse, delta, after, name):
    s = q.shape[1]
    rh = _tile(s, ATT_ROWS)
    tk = _tile(s, ATT_KEYS)
    nk = s // tk
    wide = ATT_QUERIES if s % ATT_QUERIES == 0 else tk
    pieces = tk // rh

    def body(q_ref, k_ref, v_ref, do_ref, lse_ref, dl_ref, after_ref, dq_ref, dk_ref, dv_ref):
        del after_ref
        j = pl.program_id(1)

        @pl.when(j == 0)
        def _():
            dq_ref[...] = jnp.zeros_like(dq_ref)

        kj, vj = k_ref[0], v_ref[0]

        def blk(start, rows, dks, dvs, diagonal):
            dks, dvs = list(dks), list(dvs)
            offs = [pl.multiple_of(start + g * rh, rh) for g in range(rows // rh)]
            keys = [(g + 1) * rh if diagonal else tk for g in range(rows // rh)]
            qs = [q_ref[0, pl.ds(off, rh), :] for off in offs]
            dos = [do_ref[0, pl.ds(off, rh), :] for off in offs]
            scs = [_dot_nt(qi, kj[:n]) for qi, n in zip(qs, keys)]
            dps = [_dot_nt(doi, vj[:n]) for doi, n in zip(dos, keys)]
            for g, off in enumerate(offs):
                lse_i = lse_ref[0, pl.ds(off, rh), :][:, :1]
                dl_i = dl_ref[0, pl.ds(off, rh), :][:, :1]
                sc = _causal_mask(scs[g], g * rh) if diagonal else scs[g]
                p = jnp.exp2(sc - lse_i)
                ds = (p * (dps[g] - dl_i)).astype(BF16)
                cv = _dot_tn(p.astype(BF16), dos[g])
                ck = _dot_tn(ds, qs[g])
                for t in range(keys[g] // rh):
                    dvs[t] = dvs[t] + cv[t * rh:(t + 1) * rh]
                    dks[t] = dks[t] + ck[t * rh:(t + 1) * rh]
                dq_ref[0, pl.ds(off, rh), :] += _dot(ds, kj[:keys[g]]) * SCALE
            return tuple(dks), tuple(dvs)

        per = wide // tk
        zero = (jnp.zeros((rh, HP), F32),) * pieces
        acc = blk(j * tk, tk, zero, zero, True)
        first_wide = (j + per) // per
        acc = lax.fori_loop(j + 1, jnp.minimum(first_wide * per, nk), lambda i, c: blk(i * tk, tk, *c, False), acc)
        dks, dvs = lax.fori_loop(first_wide, nk // per, lambda i, c: blk(i * wide, wide, *c, False), acc)
        dk_ref[0] = jnp.concatenate(dks, axis=0) * (SCALE / EXP2_C)
        dv_ref[0] = jnp.concatenate(dvs, axis=0)

    full = lambda: pl.BlockSpec((1, s, HP), lambda h, j: (h, 0, 0))
    blk_spec = lambda: pl.BlockSpec((1, tk, HP), lambda h, j: (h, j, 0))
    out = jax.ShapeDtypeStruct((HEADS, s, HP), F32)
    return pl.pallas_call(
        body, name=name, grid=(HEADS, s // tk),
        in_specs=[full(), blk_spec(), blk_spec(), full(), full(), full(), ANY],
        out_specs=[full(), blk_spec(), blk_spec()], out_shape=[out] * 3,
        compiler_params=_cp(("parallel", "arbitrary"), VMEM_LIMIT),
    )(q, k, v, do, lse, delta, after)


def _mla_prep_bwd(dq, dk, dv, z, tabs, gql, gkv, gq, gk, wq, wk, wv, name):
    s = z.shape[0]
    tm = _tile(s, TOKENS)

    def body(dq_ref, dk_ref, dv_ref, ql_ref, kv_ref, kr_ref, c_ref, sa_ref, sb_ref, gql_ref, gkv_ref, gq_ref, gk_ref,
             wq_ref, wk_ref, wv_ref,
             dz_ref, dwq_ref, dwk_ref, dwv_ref, dgql_ref, dgkv_ref, dgq_ref, dgk_ref, dqr_ref, dkr_ref, dvr_ref):
        first = pl.program_id(0) == 0
        qx, rq = _rms(ql_ref[...], QL)
        qn = (qx * gql_ref[...]).astype(BF16)
        kx, rk = _rms(kv_ref[...], KVL)
        kvn = (kx * gkv_ref[...]).astype(BF16)
        qraw = _dot(qn, wq_ref[...])
        kraw = _dot(kvn, wk_ref[...])
        kr = kr_ref[...]
        c, sa, sb = c_ref[...], sa_ref[...], sb_ref[...]
        lane = lax.broadcasted_iota(jnp.int32, (tm, HP), 1)
        rope_lanes = (lane >= NOPE) & (lane < QK)
        ones = _head_ones()
        heads = lambda f: jnp.concatenate([f(h) for h in range(HEADS)], axis=1)
        fold = lambda v: sum(v[:, h * HP:(h + 1) * HP] for h in range(HEADS))

        def head_rms_bwd(x, g_ref, d_ref):
            xn, r = _head_rms(x, ones)
            dy = heads(lambda h: _rope_t(d_ref[h], c, sa, sb))
            dn = dy * jnp.concatenate([g_ref[...]] * HEADS, axis=1)
            dx = r * (dn - xn * (_head_sum(dn * xn, ones) * (1.0 / QK)))
            return dx, fold(jnp.sum(dy * xn, axis=0, keepdims=True))

        dxq, dgq = head_rms_bwd(qraw, gq_ref, dq_ref)
        dxk, dgk = head_rms_bwd(kraw + jnp.concatenate([kr] * HEADS, axis=1), gk_ref, dk_ref)
        dqr_ref[...] = dxq.astype(BF16)
        dkr_ref[...] = dxk.astype(BF16)
        dvr_ref[...] = heads(lambda h: dv_ref[h]).astype(BF16)
        dkrope = jnp.where(rope_lanes, fold(dxk), 0.0)
        dqn = _dot_nt(dqr_ref[...], wq_ref[...])
        dql, dgql = _rms_bwd(qx, rq, gql_ref[...], dqn, QL)
        dkvn = _dot_nt(dkr_ref[...], wk_ref[...]) + _dot_nt(dvr_ref[...], wv_ref[...])
        dkv, dgkv = _rms_bwd(kx, rk, gkv_ref[...], dkvn, KVL)
        dz_ref[...] = jnp.concatenate([dql, dkv, dkrope], axis=1).astype(BF16)
        _accumulate(dwq_ref, _dot_tn(qn, dqr_ref[...]), first)
        _accumulate(dwk_ref, _dot_tn(kvn, dkr_ref[...]), first)
        _accumulate(dwv_ref, _dot_tn(kvn, dvr_ref[...]), first)
        _accumulate(dgql_ref, dgql, first)
        _accumulate(dgkv_ref, dgkv, first)
        _accumulate(dgq_ref, dgq, first)
        _accumulate(dgk_ref, dgk, first)

    row = lambda w, j: pl.BlockSpec((tm, w), lambda i: (i, j))
    hspec = pl.BlockSpec((HEADS, tm, HP), lambda i: (0, i, 0))
    acc = lambda r, c: (_acc((r, c)), jax.ShapeDtypeStruct((r, c), F32))
    outs = [(row(512, 0), jax.ShapeDtypeStruct((s, 512), BF16)), acc(QL, HEADS * HP), acc(KVL, HEADS * HP),
            acc(KVL, HEADS * HP), acc(1, QL), acc(1, KVL), acc(1, HP), acc(1, HP)]
    return pl.pallas_call(
        body, name=name, grid=(s // tm,),
        in_specs=[hspec, hspec, hspec, row(QL, 0), row(KVL, 2), row(HP, 3), row(HP, 0), row(HP, 0), row(HP, 0),
                  _acc((1, QL)), _acc((1, KVL)), _acc((1, HP)), _acc((1, HP)),
                  _acc((QL, HEADS * HP)), _acc((KVL, HEADS * HP)), _acc((KVL, HEADS * HP))],
        out_specs=[o[0] for o in outs], out_shape=[o[1] for o in outs],
        scratch_shapes=[pltpu.VMEM((tm, HEADS * HP), BF16)] * 3,
        compiler_params=_cp(("arbitrary",), VMEM_LIMIT),
    )(dq, dk, dv, z, z, z, *tabs, gql, gkv, gq, gk, wq, wk, wv)


def _in_proj_bwd(dzm, duv, dp, x, dx1, g, win, name):
    s = x.shape[0]
    tm = _tile(s, TOKENS // 2)

    def body(dzm_ref, duv_ref, dp_ref, x_ref, dx1_ref, g_ref, w_ref, dx_ref, dg_ref):
        groups = [slice(r0, r0 + tm // 2) for r0 in (0, tm // 2)]
        dhs = [_dot_nt(dzm_ref[rs, :], w_ref[:, 0:512]) + _dot_nt(duv_ref[rs, :], w_ref[:, 512:1024])
               + _dot_nt(dp_ref[rs, :], w_ref[:, 1024:IN_P]) for rs in groups]
        dg = jnp.zeros((1, D), F32)
        for rs, dh in zip(groups, dhs):
            xn, r = _rms(x_ref[rs, :], D)
            dxr, dgr = _rms_bwd(xn, r, g_ref[...], dh, D)
            dx_ref[rs, :] = dx1_ref[rs, :] + dxr
            dg = dg + dgr
        _accumulate(dg_ref, dg, pl.program_id(0) == 0)

    row = lambda w: pl.BlockSpec((tm, w), lambda i: (i, 0))
    return pl.pallas_call(
        body, name=name, grid=(s // tm,),
        in_specs=[row(512), row(512), row(POOL), row(D), row(D), _acc((1, D)), _res((D, IN_P))],
        out_specs=[row(D), _acc((1, D))],
        out_shape=[jax.ShapeDtypeStruct((s, D), F32), jax.ShapeDtypeStruct((1, D), F32)],
        compiler_params=_cp(("arbitrary",), VMEM_LIMIT),
    )(dzm, duv, dp, x, dx1, g, win)


def _adamw(w, g0, g1, m, v, name):
    _, r, c = w.shape
    tr = _row_tile(r, 512)
    c1 = 1.0 - B1 ** STEP
    c2 = 1.0 - B2 ** STEP

    def body(w_ref, g0_ref, g1_ref, m_ref, v_ref, g_ref, d_ref, nm_ref, nv_ref):
        gv = jnp.where(pl.program_id(0) == 0, g0_ref[...], g1_ref[...])
        g_ref[0] = gv
        nm = B1 * m_ref[0] + (1.0 - B1) * gv
        nv = B2 * v_ref[0] + (1.0 - B2) * (gv * gv)
        nm_ref[0] = nm
        nv_ref[0] = nv
        d_ref[0] = -LR * ((nm / c1) / (jnp.sqrt(nv / c2) + ADAM_EPS) + WD * w_ref[0])

    spec = pl.BlockSpec((1, tr, c), lambda l, i: (l, i, 0))
    out = jax.ShapeDtypeStruct((DEPTH, r, c), F32)
    return pl.pallas_call(
        body, name=name, grid=(DEPTH, r // tr),
        in_specs=[spec, pl.BlockSpec((tr, c), lambda l, i: (i * (1 - l), 0)), pl.BlockSpec((tr, c), lambda l, i: (i * l, 0)),
                  spec, spec],
        out_specs=[spec] * 4, out_shape=[out] * 4, compiler_params=_cp(("parallel", "parallel")),
    )(w, g0, g1, m, v)


def _adamw_vectors(ws, g0s, g1s, ms, vs, name):
    k = len(ws)
    c1 = 1.0 - B1 ** STEP
    c2 = 1.0 - B2 ** STEP

    def body(*refs):
        w_refs, g0_refs, g1_refs, m_refs, v_refs, g_out, d_out, m_out, v_out = (refs[i * k:(i + 1) * k] for i in range(9))
        for i in range(k):
            for l, g_ref in enumerate((g0_refs[i], g1_refs[i])):
                row = slice(l, l + 1)
                gv = g_ref[...]
                g_out[i][row, :] = gv
                nm = B1 * m_refs[i][row, :] + (1.0 - B1) * gv
                nv = B2 * v_refs[i][row, :] + (1.0 - B2) * (gv * gv)
                m_out[i][row, :] = nm
                v_out[i][row, :] = nv
                d_out[i][row, :] = -LR * ((nm / c1) / (jnp.sqrt(nv / c2) + ADAM_EPS) + WD * w_refs[i][row, :])

    out = [jax.ShapeDtypeStruct(w.shape, F32) for w in ws]
    return pl.pallas_call(body, name=name, out_shape=out * 4)(*ws, *g0s, *g1s, *ms, *vs)


ANY = pl.BlockSpec(memory_space=pl.ANY)


def _place():
    x, y, c = lax.axis_index("x"), lax.axis_index("y"), lax.axis_index("c")
    chips = [(1 - x, y), (x, 1 - y), (1 - x, 1 - y)]
    return x, y, c, chips


def _half_rows(ref, lead, hh, half, align):
    rows = pl.ds(pl.multiple_of(hh * half, align), half)
    return ref.at[rows, :] if lead is None else ref.at[lead, rows, :]


def _row_align(dtype):
    return 16 if dtype == BF16 else 8


def _sems(n):
    return [pltpu.SemaphoreType.DMA((n,)), pltpu.SemaphoreType.DMA((n,)), pltpu.SemaphoreType.DMA((n,))]


def _comm_call(body, ins, out_shapes, nsems, name):
    return pl.pallas_call(
        body, name=name, in_specs=[ANY] * len(ins), out_specs=[ANY] * len(out_shapes), out_shape=out_shapes,
        scratch_shapes=_sems(nsems), compiler_params=pltpu.CompilerParams(has_side_effects=True),
    )(*ins)


def _all_gather_chips(shards, name, meanwhile=None):
    n = len(shards)
    halves = [a.shape[0] // 2 for a in shards]
    aligns = [_row_align(a.dtype) for a in shards]
    assert all(h % al == 0 for h, al in zip(halves, aligns))
    work, work_ins, work_outs = meanwhile or (None, [], [])
    k, m = len(work_ins), len(work_outs)

    def body(*refs):
        ins, outs, (send_sems, recv_sems, _) = refs[:n], refs[n + k:2 * n + k], refs[2 * n + k + m:]
        x, y, c, chips = _place()
        me = 2 * x + y
        sibling = (x, y, 1 - c)

        def copy(sem, src, dst, to):
            return pltpu.make_async_remote_copy(src_ref=src, dst_ref=dst, send_sem=send_sems.at[sem],
                                                recv_sem=recv_sems.at[sem], device_id=to, device_id_type=MESH)

        first, passed = [], []
        for a in range(n):
            my_half = _half_rows(ins[a], None, c, halves[a], aligns[a])
            for j, (cx, cy) in enumerate(chips):
                cp = copy(6 * a + j, my_half, _half_rows(outs[a], me, c, halves[a], aligns[a]), (cx, cy, c))
                cp.start()
                first.append(cp)
        if work is not None:
            work(refs[n:n + k], refs[2 * n + k:2 * n + k + m], 0)
        for a in range(n):
            for j, (cx, cy) in enumerate(chips):
                landed = _half_rows(outs[a], 2 * cx + cy, c, halves[a], aligns[a])
                copy(6 * a + j, landed, landed, (cx, cy, c)).wait_recv()
                fwd = copy(6 * a + 3 + j, landed, landed, sibling)
                fwd.start()
                passed.append(fwd)
        if work is not None:
            work(refs[n:n + k], refs[2 * n + k:2 * n + k + m], 1)
        for a in range(n):
            for j, (cx, cy) in enumerate(chips):
                other = _half_rows(outs[a], 2 * cx + cy, 1 - c, halves[a], aligns[a])
                copy(6 * a + 3 + j, other, other, sibling).wait_recv()
        for cp in first + passed:
            cp.wait_send()

    land_shapes = [jax.ShapeDtypeStruct((CHIPS,) + a.shape, a.dtype) for a in shards]
    if work is None:
        return _with_own(_comm_call(body, shards, land_shapes, 6 * n, name), shards)
    vmem = pl.BlockSpec(memory_space=pltpu.VMEM)
    res = pl.pallas_call(
        body, name=name, in_specs=[ANY] * n + [vmem] * k, out_specs=[ANY] * n + [vmem] * m,
        out_shape=land_shapes + list(work_outs), scratch_shapes=_sems(6 * n),
        compiler_params=pltpu.CompilerParams(has_side_effects=True, vmem_limit_bytes=VMEM_LIMIT),
    )(*shards, *work_ins)
    return _with_own(res[:n], shards), res[n:]


def _with_own(lands, shards):
    me = 2 * lax.axis_index("x") + lax.axis_index("y")
    return [lax.dynamic_update_slice(g, a[None], (me, 0, 0)) for g, a in zip(lands, shards)]


def _pair_join(arrs, name):
    n = len(arrs)
    halves = [a.shape[0] // 2 for a in arrs]

    def body(*refs):
        outs, (send_sems, recv_sems, _) = refs[n:2 * n], refs[2 * n:]
        x, y, c, _ = _place()
        cps = []
        for a in range(n):
            mine = _half_rows(outs[a], None, c, halves[a], 8)
            cp = pltpu.make_async_remote_copy(src_ref=mine, dst_ref=mine, send_sem=send_sems.at[a], recv_sem=recv_sems.at[a],
                                              device_id=(x, y, 1 - c), device_id_type=MESH)
            cp.start()
            cps.append(cp)
        for cp in cps:
            cp.wait()

    return pl.pallas_call(
        body, name=name, in_specs=[ANY] * n, out_specs=[ANY] * n,
        out_shape=[jax.ShapeDtypeStruct(a.shape, a.dtype) for a in arrs],
        input_output_aliases={i: i for i in range(n)}, scratch_shapes=_sems(n),
        compiler_params=pltpu.CompilerParams(has_side_effects=True),
    )(*arrs)


HBM = pl.BlockSpec(memory_space=pltpu.HBM)
SEM = pl.BlockSpec(memory_space=pltpu.SEMAPHORE)
DATAFLOW = pltpu.SideEffectType.DATAFLOW_SIDE_EFFECTING


def _remote_copies(pairs, ins, lands, send_sems, recv_sems):
    return [pltpu.make_async_remote_copy(src_ref=src, dst_ref=dst, send_sem=send_sems.at[i], recv_sem=recv_sems.at[i],
                                         device_id=to, device_id_type=MESH)
            for i, (src, dst, to) in enumerate(pairs(ins, lands))]


def _split_start(srcs, land_shapes, ncopies, pairs, name, after):
    n, m = len(srcs), len(land_shapes)

    def body(*refs):
        ins, lands = refs[:n], refs[n:n + m]
        send_sems, recv_sems, token = refs[n + m + 1], refs[n + m + 2], refs[-1]
        for cp in _remote_copies(pairs, ins, lands, send_sems, recv_sems):
            cp.start()
        token[...] = jnp.zeros_like(token)

    hbm = lambda a: pltpu.with_memory_space_constraint(a, pltpu.HBM)
    lands = [hbm(lax.empty(s.shape, s.dtype)) for s in land_shapes]
    thru = [pltpu.HBM(a.shape, a.dtype) for a in list(srcs) + lands]
    out = pl.pallas_call(
        body, name=name,
        out_shape=(pltpu.SemaphoreType.DMA((ncopies,)), pltpu.SemaphoreType.DMA((ncopies,)), *thru,
                   jax.ShapeDtypeStruct((8, LANES), F32)),
        in_specs=[HBM] * (n + m) + [ANY], out_specs=(SEM, SEM, *[HBM] * (n + m), pl.BlockSpec(memory_space=pltpu.VMEM)),
        input_output_aliases={i: 2 + i for i in range(n + m)},
        compiler_params=pltpu.CompilerParams(has_side_effects=DATAFLOW),
    )(*[hbm(a) for a in srcs], *lands, after)
    return out[0], out[1], list(out[2:2 + n]), list(out[2 + n:2 + n + m]), out[-1]


def _split_wait(send_sems, recv_sems, srcs, lands, after, pairs, name):
    n, m = len(srcs), len(lands)

    def body(*refs):
        ins, lands_ = refs[:n], refs[n:n + m]
        for cp in _remote_copies(pairs, ins, lands_, refs[n + m], refs[n + m + 1]):
            cp.wait_send()
            cp.wait_recv()

    out = pl.pallas_call(
        body, name=name, out_shape=tuple(pltpu.HBM(a.shape, a.dtype) for a in list(srcs) + list(lands)),
        in_specs=[HBM] * (n + m) + [SEM, SEM, ANY], out_specs=tuple([HBM] * (n + m)),
        input_output_aliases={i: i for i in range(n + m)},
        compiler_params=pltpu.CompilerParams(has_side_effects=DATAFLOW),
    )(*srcs, *lands, send_sems, recv_sems, after)
    return list(out[:n]), list(out[n:])


def _gather_pairs(halves, aligns):
    def pairs(ins, lands):
        x, y, c, chips = _place()
        me = 2 * x + y
        return [(_half_rows(ins[a], None, c, halves[a], aligns[a]), _half_rows(lands[a], me, c, halves[a], aligns[a]),
                 (cx, cy, c)) for a in range(len(ins)) for cx, cy in chips]
    return pairs


PEERS = 7


def _scatter_pairs(ins, lands):
    x, y, c, chips = _place()
    to = [(cx, cy, c) for cx, cy in chips] + [(cx, cy, 1 - c) for cx, cy in chips] + [(x, y, 1 - c)]
    out = []
    for a in range(len(ins)):
        half = ins[a].shape[1] // 2
        for i, (tx, ty, tc) in enumerate(to):
            out.append((_half_rows(ins[a], 2 * tx + ty, tc, half, 8), lands[a].at[i], (tx, ty, tc)))
    return out


def _gather_finish(shards, lands, name):
    n = len(shards)
    halves = [a.shape[0] // 2 for a in shards]
    aligns = [_row_align(a.dtype) for a in shards]

    def body(*refs):
        outs, (send_sems, recv_sems, _) = refs[n:2 * n], refs[2 * n:]
        x, y, c, chips = _place()
        passed = []
        for a in range(n):
            for j, (cx, cy) in enumerate(chips):
                landed = _half_rows(outs[a], 2 * cx + cy, c, halves[a], aligns[a])
                cp = pltpu.make_async_remote_copy(src_ref=landed, dst_ref=landed, send_sem=send_sems.at[3 * a + j],
                                                  recv_sem=recv_sems.at[3 * a + j], device_id=(x, y, 1 - c),
                                                  device_id_type=MESH)
                cp.start()
                passed.append(cp)
        for a in range(n):
            for j, (cx, cy) in enumerate(chips):
                other = _half_rows(outs[a], 2 * cx + cy, 1 - c, halves[a], aligns[a])
                pltpu.make_async_remote_copy(src_ref=other, dst_ref=other, send_sem=send_sems.at[3 * a + j],
                                             recv_sem=recv_sems.at[3 * a + j], device_id=(x, y, 1 - c),
                                             device_id_type=MESH).wait_recv()
        for cp in passed:
            cp.wait_send()

    lands = pl.pallas_call(
        body, name=name, in_specs=[ANY] * n, out_specs=[ANY] * n,
        out_shape=[jax.ShapeDtypeStruct(a.shape, a.dtype) for a in lands],
        input_output_aliases={i: i for i in range(n)}, scratch_shapes=_sems(3 * n),
        compiler_params=pltpu.CompilerParams(has_side_effects=True),
    )(*lands)
    return _with_own(lands, shards)


def _sum_own_and_landed(owns, landeds, where, name):
    n = len(owns)
    _, half, cols = landeds[0].shape
    tr = _row_tile(half, 128)
    nt = half // tr

    grid_spec = pltpu.PrefetchScalarGridSpec(
        num_scalar_prefetch=1, grid=(nt,),
        in_specs=[pl.BlockSpec((1, tr, cols), lambda r, w: (w[0], w[1] * nt + r, 0))] * n
        + [pl.BlockSpec((PEERS, tr, cols), lambda r, w: (0, r, 0))] * n,
        out_specs=[pl.BlockSpec((tr, cols), lambda r, w: (w[1] * nt + r, 0))] * n)

    def body(w_ref, *refs):
        for p_ref, q_ref, o_ref in zip(refs[:n], refs[n:2 * n], refs[2 * n:]):
            acc = p_ref[0]
            for i in range(PEERS):
                acc = acc + q_ref[i]
            o_ref[...] = acc

    return pl.pallas_call(
        body, name=name, grid_spec=grid_spec, out_shape=[jax.ShapeDtypeStruct((2 * half, cols), owns[0].dtype)] * n,
        compiler_params=_cp(("parallel",), VMEM_LIMIT),
    )(where, *owns, *landeds)


BIG = [("w_in", (D, IN_W), 1), ("w_q_up", (QL, HEADS * QK), 1), ("w_kv_up", (KVL, HEADS * (NOPE + VH)), 1),
       ("w_out", (D, D), 0), ("w_gate", (D, HID), 1), ("w_up", (D, HID), 1), ("w_down", (HID, D), 0)]
SMALL = [("g_mix_norm", (D,)), ("g_q_lat", (QL,)), ("g_kv_lat", (KVL,)), ("g_q_head", (QK,)), ("g_k_head", (QK,)),
         ("g_sgu_v", (SGU,)), ("w_spatial", (HEADS, CHUNK, CHUNK)), ("b_spatial", (HEADS, CHUNK)),
         ("w_pool", (4, 64, 64)), ("pool_scale", (POOL,)), ("g_out_mla", (512,)), ("g_out_sgu", (SGU,)),
         ("g_out_pool", (POOL,)), ("g_ffn_norm", (D,))]
ORDER = ["g_mix_norm", "w_in", "g_q_lat", "w_q_up", "g_kv_lat", "w_kv_up", "g_q_head", "g_k_head", "g_sgu_v",
         "w_spatial", "b_spatial", "w_pool", "pool_scale", "g_out_mla", "g_out_sgu", "g_out_pool", "w_out",
         "g_ffn_norm", "w_gate", "w_up", "w_down"]
EARLY_BIG = ["w_in", "w_q_up", "w_kv_up"]
FFN_BIG = ["w_gate", "w_up", "w_down"]
LATE_BIG = ["w_out"] + FFN_BIG
DEPTH = 2
COLS = 1024
SMALL_N = sum(math.prod(s) for _, s in SMALL) * DEPTH
assert SMALL_N % CHIPS == 0
SMALL_ROWS = -(-(SMALL_N // CHIPS + 1) // (16 * COLS)) * 16


def _unsplit_cols(g):
    return g.transpose(1, 0, 2).reshape(g.shape[1], CHIPS * g.shape[2])


def _split_cols(full):
    r, c = full.shape
    return full.reshape(r, CHIPS, c // CHIPS).transpose(1, 0, 2)


def _kernel_weights(g):
    win = _unsplit_cols(g["w_in"])
    zeros = lambda r, c: jnp.zeros((r, c), BF16)
    o2, o3, o4 = QL + KVL, QL + KVL + ROPE, QL + KVL + ROPE + 2 * SGU
    win_p = jnp.concatenate([win[:, :o2], zeros(D, NOPE), win[:, o2:o3], zeros(D, HP - QK), win[:, o3:o4], win[:, o4:]], axis=1)
    wq = _unsplit_cols(g["w_q_up"]).reshape(QL, HEADS, QK)
    wq_p = jnp.pad(wq, ((0, 0), (0, 0), (0, HP - QK))).reshape(QL, HEADS * HP)
    wkv = _unsplit_cols(g["w_kv_up"]).reshape(KVL, HEADS, NOPE + VH)
    wk_p = jnp.pad(wkv[:, :, :NOPE], ((0, 0), (0, 0), (0, HP - NOPE))).reshape(KVL, HEADS * HP)
    wv_p = wkv[:, :, NOPE:].reshape(KVL, HEADS * VH)
    return dict(win=win_p, wq=wq_p, wk=wk_p, wv=wv_p)


def _small_operands(p, l):
    row = lambda v: v.reshape(1, -1)
    pad = lambda v: jnp.pad(v, (0, HP - QK)).reshape(1, HP)
    wpool = p["w_pool"][l]
    wbd = jnp.zeros((POOL, POOL), F32)
    for g in range(4):
        wbd = lax.dynamic_update_slice(wbd, wpool[g], (g * 64, g * 64))
    return dict(
        g_mix=row(p["g_mix_norm"][l]), gql=row(p["g_q_lat"][l]), gkv=row(p["g_kv_lat"][l]),
        gq=pad(p["g_q_head"][l]), gk=pad(p["g_k_head"][l]), gsv=row(p["g_sgu_v"][l]),
        wsp=p["w_spatial"][l], bsp=jnp.repeat(p["b_spatial"][l].T, SGU // HEADS, axis=1),
        wbd=wbd.astype(BF16), psc=row(p["pool_scale"][l]),
        gout=jnp.concatenate([p["g_out_mla"][l], p["g_out_sgu"][l], p["g_out_pool"][l]]).reshape(1, D),
        g_ffn=row(p["g_ffn_norm"][l]))


def _big_grads(g):
    dwin = g["win"]
    o2 = QL + KVL
    gin = jnp.concatenate([dwin[:, :o2], dwin[:, o2 + NOPE:o2 + NOPE + ROPE], dwin[:, 512:]], axis=1)
    gq = g["wq"].reshape(QL, HEADS, HP)[:, :, :QK].reshape(QL, HEADS * QK)
    gk = g["wk"].reshape(KVL, HEADS, HP)[:, :, :NOPE]
    gv = g["wv"].reshape(KVL, HEADS, VH)
    gkv = jnp.concatenate([gk, gv], axis=2).reshape(KVL, HEADS * (NOPE + VH))
    return {"w_in": _split_cols(gin), "w_q_up": _split_cols(gq), "w_kv_up": _split_cols(gkv),
            "w_out": g["wout"].reshape(CHIPS, D // CHIPS, D), "w_gate": g["wg"], "w_up": g["wu"], "w_down": g["wd"]}


TRANSPOSED = ("w_gate", "w_up")


def _small_grads(g):
    go = g["gout"].reshape(-1)
    return {"g_mix_norm": g["g_mix"].reshape(-1), "g_q_lat": g["gql"].reshape(-1), "g_kv_lat": g["gkv"].reshape(-1),
            "g_q_head": g["gq"].reshape(-1)[:QK], "g_k_head": g["gk"].reshape(-1)[:QK], "g_sgu_v": g["gsv"].reshape(-1),
            "w_spatial": g["wsp"], "b_spatial": g["bsp"].reshape(CHUNK, HEADS, SGU // HEADS).sum(-1).T,
            "w_pool": jnp.stack([g["wbd"][i * 64:(i + 1) * 64, i * 64:(i + 1) * 64] for i in range(4)]),
            "pool_scale": g["psc"].reshape(-1), "g_out_mla": go[:512], "g_out_sgu": go[512:768],
            "g_out_pool": go[768:], "g_ffn_norm": g["g_ffn"].reshape(-1)}


def _pack_small_grads(small, loss):
    sm = jnp.concatenate([small[l][n].reshape(-1) for l in range(DEPTH) for n, _ in SMALL]).reshape(CHIPS, SMALL_N // CHIPS)
    sm = jnp.pad(sm, ((0, 0), (0, SMALL_ROWS * COLS - SMALL_N // CHIPS)))
    return sm.at[0, SMALL_N // CHIPS].set(loss).reshape(CHIPS, SMALL_ROWS, COLS)


def _unpack_small_grads(gathered):
    rows = gathered.reshape(CHIPS, SMALL_ROWS * COLS)
    loss = rows[0, SMALL_N // CHIPS]
    flat = rows[:, :SMALL_N // CHIPS].reshape(-1)
    out, off = [], 0
    for _ in range(DEPTH):
        layer = {}
        for n, shape in SMALL:
            k = math.prod(shape)
            layer[n] = flat[off:off + k].reshape(shape)
            off += k
        out.append(layer)
    return out, loss


def _layer_fwd(x, tabs, kw, late_weights, sp, l, tgt):
    t = f"_l{l}"
    z, hb = _in_proj_fwd(x, sp["g_mix"], kw["win"], "in_proj_fwd" + t)
    q, k, v = _mla_prep_fwd(z, tabs, sp["gql"], sp["gkv"], sp["gq"], sp["gk"], kw["wq"], kw["wk"], kw["wv"],
                            "mla_prep_fwd" + t)
    o, lse = _attn_fwd(q, k, v, "attn_fwd" + t)
    m = _pool_win_fwd(z, "pool_win_fwd" + t)
    wout, wg, wu, wd = late_weights(o)
    wout = wout.reshape(D, D)
    x1, mix = _mix_out_fwd(o, z, m, x, sp["wsp"], sp["bsp"], sp["wbd"], sp["psc"], sp["gsv"], sp["gout"], wout,
                           "mix_out_fwd" + t)
    x2, a, b, h2 = _ffn_fwd(x1, sp["g_ffn"], wg, wu, wd, tgt, "ffn_fwd" + t)
    saved = dict(x=x, z=z, hb=hb, q=q, k=k, v=v, o=o, lse=lse, m=m, x1=x1, mix=mix, a=a, b=b, h2=h2, wg=wg, wu=wu, wd=wd,
                 wout=wout)
    return x2, saved


def _layer_bwd(dx2, sv, tabs, kw, sp, l, ffn_hook, out_hook):
    t = f"_l{l}"
    g = {}
    dx1, hid, da, db, dyb, g["g_ffn"] = _ffn_bwd(dx2, sv["x1"], sv["a"], sv["b"], sp["g_ffn"], sv["wg"], sv["wu"],
                                                 sv["wd"], "ffn_bwd" + t)
    g["wd"] = _wgrad_rows(hid, dyb, "wgrad_down" + t)
    g["wg"] = _wgrad_rows(da, sv["h2"], "wgrad_gate" + t)
    g["wu"] = _wgrad_rows(db, sv["h2"], "wgrad_up" + t)
    gout = sp["gout"] + ffn_hook(g)
    do, delta, duv, dm, g["gout"], g["gsv"], g["psc"], g["wsp"], g["bsp"], g["wbd"] = _mix_out_bwd(
        dx1, sv["o"], sv["z"], sv["m"], sp["wsp"], sp["bsp"], sp["wbd"], sp["psc"], sp["gsv"], gout, sv["wout"],
        "mix_out_bwd" + t)
    g["wout"] = _wgrad(sv["mix"], dx1, "wgrad_out" + t)
    dp = _pool_win_bwd(dm, "pool_win_bwd" + t)
    dq, dk, dv = _attn_bwd(sv["q"], sv["k"], sv["v"], do, sv["lse"], delta, out_hook(g), "attn_bwd" + t)
    dzm, g["wq"], g["wk"], g["wv"], g["gql"], g["gkv"], g["gq"], g["gk"] = _mla_prep_bwd(
        dq, dk, dv, sv["z"], tabs, sp["gql"], sp["gkv"], sp["gq"], sp["gk"], kw["wq"], kw["wk"], kw["wv"],
        "mla_prep_bwd" + t)
    dx, g["g_mix"] = _in_proj_bwd(dzm, duv, dp, sv["x"], dx1, sp["g_mix"], kw["win"], "in_proj_bwd" + t)
    g["win"] = _wgrad_in(sv["hb"], dzm, duv, dp, "wgrad_in" + t)
    return dx, g


def _rope_inv_freq():
    half = ROPE // 2
    inv = 1.0 / (ROPE_THETA ** (jnp.arange(half, dtype=F32) / half))
    return jnp.concatenate([jnp.zeros((NOPE,), F32), inv, inv, jnp.zeros((HP - QK,), F32)]).reshape(1, HP)


def kernel(x, positions, g_mix_norm, w_in, g_q_lat, w_q_up, g_kv_lat, w_kv_up, g_q_head, g_k_head, g_sgu_v, w_spatial, b_spatial, w_pool, pool_scale, g_out_mla, g_out_sgu, g_out_pool, w_out, g_ffn_norm, w_gate, w_up, w_down, loss_target, m_g_mix_norm, m_w_in, m_g_q_lat, m_w_q_up, m_g_kv_lat, m_w_kv_up, m_g_q_head, m_g_k_head, m_g_sgu_v, m_w_spatial, m_b_spatial, m_w_pool, m_pool_scale, m_g_out_mla, m_g_out_sgu, m_g_out_pool, m_w_out, m_g_ffn_norm, m_w_gate, m_w_up, m_w_down, v_g_mix_norm, v_w_in, v_g_q_lat, v_w_q_up, v_g_kv_lat, v_w_kv_up, v_g_q_head, v_g_k_head, v_g_sgu_v, v_w_spatial, v_b_spatial, v_w_pool, v_pool_scale, v_g_out_mla, v_g_out_sgu, v_g_out_pool, v_w_out, v_g_ffn_norm, v_w_gate, v_w_up, v_w_down):
    given = dict(locals())
    p = {n: given[n] for n in ORDER}
    view = lambda pre, n: jnp.swapaxes(given[pre + n], 1, 2) if n in TRANSPOSED else given[pre + n]
    seq = x.shape[1]
    where = jnp.stack([2 * lax.axis_index("x") + lax.axis_index("y"), lax.axis_index("c")]).astype(jnp.int32)
    shards = lambda names: [view("", n)[l].astype(BF16) for l, n in names]
    zero11 = lambda token: token[:1, :1]

    names_0a = [(0, n) for n in EARLY_BIG]
    names_0b = [(0, n) for n in LATE_BIG]
    names_1 = [(1, n) for n, _, _ in BIG]
    w0a, tabs = _all_gather_chips(shards(names_0a), "all_gather_w0a",
                                  _rope_tables_meanwhile(positions.reshape(seq, 1), _rope_inv_freq()))
    got_0a = dict(zip(EARLY_BIG, w0a))
    started, issued = {}, got_0a["w_in"]
    for tag, names in (("w0b", names_0b), ("w1", names_1)):
        sh = shards(names)
        pairs = _gather_pairs([a.shape[0] // 2 for a in sh], [_row_align(a.dtype) for a in sh])
        lands = [jax.ShapeDtypeStruct((CHIPS,) + a.shape, a.dtype) for a in sh]
        started[tag] = (sh, pairs) + _split_start(sh, lands, 3 * len(sh), pairs, "gather_start_" + tag, issued)
        issued = started[tag][6]

    def arrived(tag, after):
        _, pairs, send, recv, srcs, lands, _ = started[tag]
        srcs, lands = _split_wait(send, recv, srcs, lands, after, pairs, "gather_wait_" + tag)
        return _gather_finish(srcs, lands, "gather_finish_" + tag)

    layer1 = {}

    def mix_weights(l, h):
        if l == 0:
            return got_0a
        layer1.update(zip([n for _, n in names_1], arrived("w1", h)))
        return layer1

    def late_weights(l, o):
        return arrived("w0b", o) if l == 0 else [layer1[n] for n in LATE_BIG]

    reducing, last = {}, {}

    def reduce_start(tag, arrs):
        lands = [jax.ShapeDtypeStruct((PEERS, a.shape[1] // 2, a.shape[2]), a.dtype) for a in arrs]
        reducing[tag] = _split_start(arrs, lands, PEERS * len(arrs), _scatter_pairs, "grad_scatter_start_" + tag, where)
        return zero11(reducing[tag][4])

    def reduce_finish(tag, after):
        send, recv, srcs, lands, _ = reducing[tag]
        srcs, lands = _split_wait(send, recv, srcs, lands, after, _scatter_pairs, "grad_scatter_wait_" + tag)
        sums = [None] * len(srcs)
        for shape in dict.fromkeys(a.shape for a in srcs):
            idx = [i for i, a in enumerate(srcs) if a.shape == shape]
            res = _sum_own_and_landed([srcs[i] for i in idx], [lands[i] for i in idx], where, f"grad_sum_{tag}_{idx[0]}")
            for i, r in zip(idx, res):
                sums[i] = r
        return sums

    def ffn_hook(l, g):
        if l == 1:
            return jnp.zeros((1, 1), F32)
        return reduce_start("g0b", [g["wg"], g["wu"], g["wd"]])

    def out_hook(l, g):
        if l == 1:
            return where
        reduce_start("g0c", [g["wout"].reshape(CHIPS, D // CHIPS, D)])
        return reducing["g0c"][4]

    def layer_hook(l, big, small):
        last[l] = (big, small)
        if l == 1:
            return reduce_start("g1", [big[n] for n, _, _ in BIG])
        return None

    entry = zero11(started["w0b"][6]) + zero11(started["w1"][6])
    loss_part, dx = _step(x.reshape(seq, D), tuple(tabs), loss_target.reshape(seq, D), p, entry,
                          mix_weights, late_weights, ffn_hook, out_hook, layer_hook)

    def adamw(n, g0, g1):
        flip = n in EARLY_BIG
        pick = lambda pre: jnp.swapaxes(given[pre + n], 1, 2) if flip else view(pre, n)
        w = pick("")
        three_d = (DEPTH, -1, w.shape[-1])
        g0, g1 = (g.T if flip else g for g in (g0, g1))
        res = _adamw(w.reshape(three_d), g0.reshape(three_d[1:]), g1.reshape(three_d[1:]),
                     pick("m_").reshape(three_d), pick("v_").reshape(three_d), "adamw_" + n)
        return [jnp.swapaxes(r.reshape(w.shape), 1, 2) if flip else r.reshape(w.shape) for r in res]

    names_rest = [(0, n) for n in EARLY_BIG]
    reduce_start("g0a", [last[0][0][n] for _, n in names_rest]
                 + [_pack_small_grads([last[l][1] for l in range(DEPTH)], loss_part)])
    token = reducing["g0a"][4]
    early = names_1 + [(0, n) for n in FFN_BIG] + [(0, "w_out")]
    landed = reduce_finish("g1", token) + reduce_finish("g0b", token) + reduce_finish("g0c", token)
    sums = dict(zip(early, _pair_join(landed, "grad_pair_join_early")))
    out = {n: adamw(n, sums[(0, n)], sums[(1, n)]) for n in FFN_BIG}
    late = names_rest + ["small"]
    sums.update(zip(late, _pair_join(reduce_finish("g0a", out["w_down"][1]), "grad_pair_join_late")))
    small = [sums["small"]]
    pairs = _gather_pairs([a.shape[0] // 2 for a in small], [_row_align(a.dtype) for a in small])
    lands = [jax.ShapeDtypeStruct((CHIPS,) + a.shape, a.dtype) for a in small]
    send, recv, srcs, lands, _ = _split_start(small, lands, 3, pairs, "gather_start_small_grads", sums[(0, "w_in")])
    for n, _, _ in BIG:
        if n not in out:
            out[n] = adamw(n, sums[(0, n)], sums[(1, n)])
    srcs, lands = _split_wait(send, recv, srcs, lands, out["w_out"][1], pairs, "gather_wait_small_grads")
    gsmall, loss = _unpack_small_grads(_gather_finish(srcs, lands, "gather_finish_small_grads")[0])
    vectors = [n for n, shape in SMALL if len(shape) == 1]
    res = _adamw_vectors([given[n] for n in vectors], *[[gsmall[l][n].reshape(1, -1) for n in vectors] for l in range(DEPTH)],
                         [given["m_" + n] for n in vectors], [given["v_" + n] for n in vectors], "adamw_vectors")
    out.update({n: res[i::len(vectors)] for i, n in enumerate(vectors)})
    for n in ORDER:
        if n not in out:
            g = [sums[(l, n)] for l in range(DEPTH)] if (0, n) in sums else [gsmall[l][n] for l in range(DEPTH)]
            out[n] = adamw(n, *g)
    undo = lambda n, a: jnp.swapaxes(a, 1, 2) if n in TRANSPOSED else a
    return (loss, dx.reshape(x.shape), *[undo(n, out[n][i]) for i in range(4) for n in ORDER])


def _step(xs, tabs, tgt, p, entry, mix_weights, late_weights, ffn_hook, out_hook, layer_hook):
    sps = [_small_operands(p, l) for l in range(DEPTH)]
    sps[0]["g_mix"] = sps[0]["g_mix"] + entry
    saved, h = [], xs
    for l in range(DEPTH):
        kw = _kernel_weights(mix_weights(l, h))
        h, sv = _layer_fwd(h, tabs, kw, functools.partial(late_weights, l), sps[l], l, tgt if l == DEPTH - 1 else None)
        saved.append(dict(sv, kw=kw))
    dy, lpart = h
    for l in reversed(range(DEPTH)):
        dy, g = _layer_bwd(dy, saved[l], tabs, saved[l]["kw"], sps[l], l, functools.partial(ffn_hook, l),
                           functools.partial(out_hook, l))
        zero = layer_hook(l, _big_grads(g), _small_grads(g))
        if zero is not None and l > 0:
            sps[l - 1]["g_ffn"] = sps[l - 1]["g_ffn"] + zero
    return 0.5 / D * jnp.sum(lpart), dy
```
